```python
import jax, jax.numpy as jnp
from jax import lax
import numpy as np

D_MODEL = 1024
BATCH = 8
SEQ = 4096
DEPTH = 2

ATTN_HEADS = 8
ATTN_KV_HEADS = 2
ATTN_HEAD_DIM = 64
WINDOW = 128
GDN_HEADS = 4
GDN_HEAD_DIM = 128
GDN_CHUNK = 64
GDN_CONV = 4
D_FF = 2816
FFN_CONV = 3
NORM_EPS = 1e-6

ATTN_Q_DIM = ATTN_HEADS * ATTN_HEAD_DIM
ATTN_KV_DIM = ATTN_KV_HEADS * ATTN_HEAD_DIM
GDN_DIM = GDN_HEADS * GDN_HEAD_DIM
MIX_DIM = ATTN_Q_DIM + GDN_DIM
IN_SPLITS = (ATTN_Q_DIM, ATTN_KV_DIM, ATTN_KV_DIM, 3 * GDN_DIM, GDN_DIM, GDN_HEADS, GDN_HEADS)
IN_DIM = sum(IN_SPLITS)

kernel_name = "hybrid_swa_sink_alibi_gdn_convffn"


def rmsnorm(x, w):
    xf = x.astype(jnp.float32)
    y = xf * lax.rsqrt(jnp.mean(xf * xf, axis=-1, keepdims=True) + NORM_EPS)
    return (y * w.astype(jnp.float32)).astype(x.dtype)


def l2norm(x):
    return x * lax.rsqrt(jnp.sum(x * x, axis=-1, keepdims=True) + NORM_EPS)


def causal_dwconv(x, w):
    K, C = w.shape
    return lax.conv_general_dilated(
        x, w.astype(x.dtype)[:, None, :], window_strides=(1,), padding=[(K - 1, 0)],
        dimension_numbers=("NWC", "WIO", "NWC"), feature_group_count=C)


def alibi_slopes(n):
    return 2.0 ** (-8.0 * jnp.arange(1, n + 1, dtype=jnp.float32) / n)


def sliding_window_attention(q, k, v, sinks):
    B, T, Hq, Dh = q.shape
    Hkv = k.shape[2]
    G = Hq // Hkv
    W = WINDOW
    NB = T // W
    qb = q.reshape(B, NB, W, Hkv, G, Dh)

    def with_prev(x):
        xb = x.reshape(B, NB, W, Hkv, Dh)
        prev = jnp.pad(xb[:, :-1], ((0, 0), (1, 0), (0, 0), (0, 0), (0, 0)))
        return jnp.concatenate([prev, xb], axis=2)

    kb, vb = with_prev(k), with_prev(v)
    s = jnp.einsum("bnqhgd,bnkhd->bnhgqk", qb, kb).astype(jnp.float32) * (Dh ** -0.5)
    qpos = jnp.arange(W)[:, None] + W
    kpos = jnp.arange(2 * W)[None, :]
    rel = qpos - kpos
    band = (rel >= 0) & (rel < W)
    blk = jnp.arange(NB)[:, None, None]
    valid = band[None] & ((blk > 0) | (kpos >= W)[None])
    slopes = alibi_slopes(Hq).reshape(Hkv, G)
    alibi = -slopes[:, :, None, None] * rel.astype(jnp.float32)[None, None]
    s = jnp.where(valid[None, :, None, None], s + alibi, -jnp.inf)
    sink = jnp.broadcast_to(sinks.astype(jnp.float32).reshape(Hkv, G)[:, :, None, None], s.shape[:-1] + (1,))
    p = jax.nn.softmax(jnp.concatenate([s, sink], axis=-1), axis=-1)[..., :-1]
    o = jnp.einsum("bnhgqk,bnkhd->bnqhgd", p.astype(v.dtype), vb)
    return o.reshape(B, T, Hq * Dh)


def gated_delta_rule_chunked(q, k, v, g, beta):
    B, T, H, Dk = q.shape
    Dv = v.shape[-1]
    C = GDN_CHUNK
    N = T // C
    q = q * (Dk ** -0.5)

    def chunkify(x):
        return x.reshape(B, N, C, H, -1).transpose(0, 3, 1, 2, 4)

    qc, kc, vc = chunkify(q), chunkify(k), chunkify(v)
    gc = g.reshape(B, N, C, H).transpose(0, 3, 1, 2)
    bc = beta.reshape(B, N, C, H).transpose(0, 3, 1, 2)
    Gc = jnp.cumsum(gc, axis=-1)
    causal = jnp.tril(jnp.ones((C, C), dtype=bool))
    strict = jnp.tril(jnp.ones((C, C), dtype=bool), -1)
    decay = jnp.exp(jnp.where(causal, Gc[..., :, None] - Gc[..., None, :], -jnp.inf))
    kbeta = kc * bc[..., None]
    A = jnp.where(strict, jnp.einsum("bhncd,bhnsd->bhncs", kbeta, kc) * decay, 0.0)
    rhs = jnp.concatenate([vc * bc[..., None], kbeta * jnp.exp(Gc)[..., None]], axis=-1)
    sol = lax.linalg.triangular_solve(A, rhs, left_side=True, lower=True, unit_diagonal=True)
    u = sol[..., :Dv]
    w = sol[..., Dv:]
    qk = jnp.einsum("bhncd,bhnsd->bhncs", qc, kc) * decay
    qd = qc * jnp.exp(Gc)[..., None]
    kd = kc * jnp.exp(Gc[..., -1:] - Gc)[..., None]
    glast = jnp.exp(Gc[..., -1])

    def step(S, xs):
        qk_i, qd_i, w_i, u_i, kd_i, gl_i = xs
        v_new = u_i - jnp.einsum("bhck,bhkv->bhcv", w_i, S)
        o_i = jnp.einsum("bhck,bhkv->bhcv", qd_i, S) + jnp.einsum("bhcs,bhsv->bhcv", qk_i, v_new)
        S = S * gl_i[..., None, None] + jnp.einsum("bhck,bhcv->bhkv", kd_i, v_new)
        return S, o_i

    xs = tuple(jnp.moveaxis(a, 2, 0) for a in (qk, qd, w, u, kd, glast))
    S0 = jnp.zeros((B, H, Dk, Dv), jnp.float32)
    _, o = lax.scan(step, S0, xs)
    return o.transpose(1, 0, 3, 2, 4).reshape(B, T, H, Dv)


def gated_deltanet(qkv, z, b, a, conv_w, a_log, dt_bias, norm_w):
    B, T, _ = qkv.shape
    qkv = jax.nn.silu(causal_dwconv(qkv, conv_w)).astype(jnp.float32)
    qg, kg, vg = jnp.split(qkv, 3, axis=-1)
    qg = l2norm(qg.reshape(B, T, GDN_HEADS, GDN_HEAD_DIM))
    kg = l2norm(kg.reshape(B, T, GDN_HEADS, GDN_HEAD_DIM))
    vg = vg.reshape(B, T, GDN_HEADS, GDN_HEAD_DIM)
    beta = jax.nn.sigmoid(b.astype(jnp.float32))
    g = -jnp.exp(a_log.astype(jnp.float32)) * jax.nn.softplus(a.astype(jnp.float32) + dt_bias.astype(jnp.float32))
    o = gated_delta_rule_chunked(qg, kg, vg, g, beta)
    o = o * lax.rsqrt(jnp.mean(o * o, axis=-1, keepdims=True) + NORM_EPS) * norm_w.astype(jnp.float32)
    o = o * jax.nn.silu(z.astype(jnp.float32).reshape(B, T, GDN_HEADS, GDN_HEAD_DIM))
    return o.reshape(B, T, GDN_DIM).astype(z.dtype)


def conv_gated_mlp(h, w_in, conv_w, conv_b, w_down):
    gate, up = jnp.split(h @ w_in, 2, axis=-1)
    gate = causal_dwconv(gate, conv_w) + conv_b
    return (jax.nn.silu(gate) * up) @ w_down


def _fwd_setup_inputs(seed: int = 0) -> dict:
    key = jax.random.key(seed)
    ks = jax.random.split(key, 16)
    f32 = jnp.float32
    nrm = lambda k, s, sc: jax.random.normal(k, s, f32) * sc
    dt = jnp.exp(jax.random.uniform(ks[6], (DEPTH, GDN_HEADS), f32, np.log(1e-3), np.log(1e-1)))
    return {
        "x": jax.random.normal(ks[0], (BATCH, SEQ, D_MODEL), f32),
        "attn_norm": 1.0 + nrm(ks[1], (DEPTH, D_MODEL), 0.02),
        "w_in": nrm(ks[2], (DEPTH, D_MODEL, IN_DIM), D_MODEL ** -0.5),
        "attn_sinks": nrm(ks[3], (DEPTH, ATTN_HEADS), 0.5),
        "gdn_conv_w": nrm(ks[4], (DEPTH, GDN_CONV, 3 * GDN_DIM), GDN_CONV ** -0.5),
        "gdn_a_log": jnp.log(jax.random.uniform(ks[5], (DEPTH, GDN_HEADS), f32, 1.0, 16.0)),
        "gdn_dt_bias": dt + jnp.log(-jnp.expm1(-dt)),
        "gdn_norm": 1.0 + nrm(ks[7], (DEPTH, GDN_HEAD_DIM), 0.02),
        "w_out": nrm(ks[8], (DEPTH, MIX_DIM, D_MODEL), MIX_DIM ** -0.5),
        "ffn_norm": 1.0 + nrm(ks[9], (DEPTH, D_MODEL), 0.02),
        "w_ffn_in": nrm(ks[10], (DEPTH, D_MODEL, 2 * D_FF), D_MODEL ** -0.5),
        "ffn_conv_w": nrm(ks[11], (DEPTH, FFN_CONV, D_FF), FFN_CONV ** -0.5),
        "ffn_conv_b": nrm(ks[12], (DEPTH, D_FF), 0.01),
        "w_down": nrm(ks[13], (DEPTH, D_FF, D_MODEL), D_FF ** -0.5),
        "final_norm": 1.0 + nrm(ks[14], (D_MODEL,), 0.02),
    }


def _fwd_reference(x, attn_norm, w_in, attn_sinks, gdn_conv_w, gdn_a_log, gdn_dt_bias, gdn_norm,
              w_out, ffn_norm, w_ffn_in, ffn_conv_w, ffn_conv_b, w_down, final_norm):
    B, T, _ = x.shape
    cuts = np.cumsum(IN_SPLITS)[:-1].tolist()
    for l in range(DEPTH):
        h = rmsnorm(x, attn_norm[l])
        qa, ka, va, qkv_g, z, b, a = jnp.split(h @ w_in[l], cuts, axis=-1)
        attn_out = sliding_window_attention(
            qa.reshape(B, T, ATTN_HEADS, ATTN_HEAD_DIM),
            ka.reshape(B, T, ATTN_KV_HEADS, ATTN_HEAD_DIM),
            va.reshape(B, T, ATTN_KV_HEADS, ATTN_HEAD_DIM),
            attn_sinks[l])
        gdn_out = gated_deltanet(qkv_g, z, b, a, gdn_conv_w[l], gdn_a_log[l], gdn_dt_bias[l], gdn_norm[l])
        mixed = jnp.concatenate([attn_out.astype(x.dtype), gdn_out.astype(x.dtype)], axis=-1)
        x = x + mixed @ w_out[l]
        x = x + conv_gated_mlp(rmsnorm(x, ffn_norm[l]), w_ffn_in[l], ffn_conv_w[l], ffn_conv_b[l], w_down[l])
    return rmsnorm(x, final_norm)


import jax as _jax
import jax.numpy as _jnp

TWIN_FORMAT = 'train_step'
FWD_PARAMS = ['x', 'attn_norm', 'w_in', 'attn_sinks', 'gdn_conv_w', 'gdn_a_log', 'gdn_dt_bias', 'gdn_norm', 'w_out', 'ffn_norm', 'w_ffn_in', 'ffn_conv_w', 'ffn_conv_b', 'w_down', 'final_norm']
TWIN_WEIGHTS = ['attn_norm', 'w_in', 'attn_sinks', 'gdn_conv_w', 'gdn_a_log', 'gdn_dt_bias', 'gdn_norm', 'w_out', 'ffn_norm', 'w_ffn_in', 'ffn_conv_w', 'ffn_conv_b', 'w_down', 'final_norm']
TWIN_DIFF_INPUT = 'x'
TWIN_INPUTS = ['x', 'attn_norm', 'w_in', 'attn_sinks', 'gdn_conv_w', 'gdn_a_log', 'gdn_dt_bias', 'gdn_norm', 'w_out', 'ffn_norm', 'w_ffn_in', 'ffn_conv_w', 'ffn_conv_b', 'w_down', 'final_norm', 'loss_target', 'm_attn_norm', 'm_w_in', 'm_attn_sinks', 'm_gdn_conv_w', 'm_gdn_a_log', 'm_gdn_dt_bias', 'm_gdn_norm', 'm_w_out', 'm_ffn_norm', 'm_w_ffn_in', 'm_ffn_conv_w', 'm_ffn_conv_b', 'm_w_down', 'm_final_norm', 'v_attn_norm', 'v_w_in', 'v_attn_sinks', 'v_gdn_conv_w', 'v_gdn_a_log', 'v_gdn_dt_bias', 'v_gdn_norm', 'v_w_out', 'v_ffn_norm', 'v_w_ffn_in', 'v_ffn_conv_w', 'v_ffn_conv_b', 'v_w_down', 'v_final_norm']
TWIN_OUTPUTS = ['loss', 'grad_x', 'grad_attn_norm', 'grad_w_in', 'grad_attn_sinks', 'grad_gdn_conv_w', 'grad_gdn_a_log', 'grad_gdn_dt_bias', 'grad_gdn_norm', 'grad_w_out', 'grad_ffn_norm', 'grad_w_ffn_in', 'grad_ffn_conv_w', 'grad_ffn_conv_b', 'grad_w_down', 'grad_final_norm', 'delta_attn_norm', 'delta_w_in', 'delta_attn_sinks', 'delta_gdn_conv_w', 'delta_gdn_a_log', 'delta_gdn_dt_bias', 'delta_gdn_norm', 'delta_w_out', 'delta_ffn_norm', 'delta_w_ffn_in', 'delta_ffn_conv_w', 'delta_ffn_conv_b', 'delta_w_down', 'delta_final_norm', 'new_m_attn_norm', 'new_m_w_in', 'new_m_attn_sinks', 'new_m_gdn_conv_w', 'new_m_gdn_a_log', 'new_m_gdn_dt_bias', 'new_m_gdn_norm', 'new_m_w_out', 'new_m_ffn_norm', 'new_m_w_ffn_in', 'new_m_ffn_conv_w', 'new_m_ffn_conv_b', 'new_m_w_down', 'new_m_final_norm', 'new_v_attn_norm', 'new_v_w_in', 'new_v_attn_sinks', 'new_v_gdn_conv_w', 'new_v_gdn_a_log', 'new_v_gdn_dt_bias', 'new_v_gdn_norm', 'new_v_w_out', 'new_v_ffn_norm', 'new_v_w_ffn_in', 'new_v_ffn_conv_w', 'new_v_ffn_conv_b', 'new_v_w_down', 'new_v_final_norm']
TWIN_LEAF_KINDS = {'loss': 'loss', 'grad_x': 'grad_x', 'grad_attn_norm': 'grad_w', 'grad_w_in': 'grad_w', 'grad_attn_sinks': 'grad_w', 'grad_gdn_conv_w': 'grad_w', 'grad_gdn_a_log': 'grad_w', 'grad_gdn_dt_bias': 'grad_w', 'grad_gdn_norm': 'grad_w', 'grad_w_out': 'grad_w', 'grad_ffn_norm': 'grad_w', 'grad_w_ffn_in': 'grad_w', 'grad_ffn_conv_w': 'grad_w', 'grad_ffn_conv_b': 'grad_w', 'grad_w_down': 'grad_w', 'grad_final_norm': 'grad_w', 'delta_attn_norm': 'delta_w', 'delta_w_in': 'delta_w', 'delta_attn_sinks': 'delta_w', 'delta_gdn_conv_w': 'delta_w', 'delta_gdn_a_log': 'delta_w', 'delta_gdn_dt_bias': 'delta_w', 'delta_gdn_norm': 'delta_w', 'delta_w_out': 'delta_w', 'delta_ffn_norm': 'delta_w', 'delta_w_ffn_in': 'delta_w', 'delta_ffn_conv_w': 'delta_w', 'delta_ffn_conv_b': 'delta_w', 'delta_w_down': 'delta_w', 'delta_final_norm': 'delta_w', 'new_m_attn_norm': 'new_m', 'new_m_w_in': 'new_m', 'new_m_attn_sinks': 'new_m', 'new_m_gdn_conv_w': 'new_m', 'new_m_gdn_a_log': 'new_m', 'new_m_gdn_dt_bias': 'new_m', 'new_m_gdn_norm': 'new_m', 'new_m_w_out': 'new_m', 'new_m_ffn_norm': 'new_m', 'new_m_w_ffn_in': 'new_m', 'new_m_ffn_conv_w': 'new_m', 'new_m_ffn_conv_b': 'new_m', 'new_m_w_down': 'new_m', 'new_m_final_norm': 'new_m', 'new_v_attn_norm': 'new_v', 'new_v_w_in': 'new_v', 'new_v_attn_sinks': 'new_v', 'new_v_gdn_conv_w': 'new_v', 'new_v_gdn_a_log': 'new_v', 'new_v_gdn_dt_bias': 'new_v', 'new_v_gdn_norm': 'new_v', 'new_v_w_out': 'new_v', 'new_v_ffn_norm': 'new_v', 'new_v_w_ffn_in': 'new_v', 'new_v_ffn_conv_w': 'new_v', 'new_v_ffn_conv_b': 'new_v', 'new_v_w_down': 'new_v', 'new_v_final_norm': 'new_v'}


def _forward(args):
    return _fwd_reference(*[args[k] for k in FWD_PARAMS])


def _output_shape():
    out = _jax.eval_shape(lambda: _forward(_fwd_setup_inputs(0)))
    return out.shape, out.dtype

N_MICROBATCH = 1
ADAM_LR = 0.001
ADAM_B1 = 0.9
ADAM_B2 = 0.999
ADAM_EPS = 1e-08
ADAM_WD = 0.01
ADAM_STEP = 10
PER_EXAMPLE_BATCH_AXIS = {'x': 0, 'loss_target': 0}
SHARED_INPUTS = []
_WEIGHT_DTYPES = {'attn_norm': _jnp.float32, 'w_in': _jnp.float32, 'attn_sinks': _jnp.float32, 'gdn_conv_w': _jnp.float32, 'gdn_a_log': _jnp.float32, 'gdn_dt_bias': _jnp.float32, 'gdn_norm': _jnp.float32, 'w_out': _jnp.float32, 'ffn_norm': _jnp.float32, 'w_ffn_in': _jnp.float32, 'ffn_conv_w': _jnp.float32, 'ffn_conv_b': _jnp.float32, 'w_down': _jnp.float32, 'final_norm': _jnp.float32}
MOMENT_SCALE = {'attn_norm': 1.460425e-01, 'w_in': 8.502112e-02, 'attn_sinks': 5.871392e-02, 'gdn_conv_w': 8.275173e-02, 'gdn_a_log': 5.179553e-01, 'gdn_dt_bias': 5.045371e-01, 'gdn_norm': 2.083796e-01, 'w_out': 8.430805e-02, 'ffn_norm': 1.295272e-01, 'w_ffn_in': 5.542069e-02, 'ffn_conv_w': 5.537583e-02, 'ffn_conv_b': 5.269044e-02, 'w_down': 9.074089e-02, 'final_norm': 3.205369e+01}


def _to_microbatches(a, axis):
    t = _jnp.moveaxis(a, axis, 0)
    t = t.reshape((N_MICROBATCH, t.shape[0] // N_MICROBATCH) + t.shape[1:])
    return _jnp.moveaxis(t, 1, axis + 1)


def setup_inputs(seed: int = 0) -> dict:
    inp = _fwd_setup_inputs(seed)
    key = _jax.random.fold_in(_jax.random.key(seed), 7919)
    shape, _ = _output_shape()
    out = dict(inp)
    out["loss_target"] = _jax.random.normal(_jax.random.fold_in(key, 0), shape, _jnp.float32)
    for i, name in enumerate(TWIN_WEIGHTS):
        w = inp[name].astype(_jnp.float32)
        if MOMENT_SCALE is None:
            s = _jnp.sqrt(_jnp.mean(_jnp.square(w)) + 1e-30)
        else:
            s = MOMENT_SCALE[name]
        km, kv = _jax.random.split(_jax.random.fold_in(key, i + 1))
        out[name] = w
        out["m_" + name] = s * _jax.random.normal(km, w.shape, _jnp.float32)
        out["v_" + name] = (s * s) * _jax.random.uniform(kv, w.shape, _jnp.float32, 0.5, 1.5)
    if N_MICROBATCH > 1:
        for name, axis in PER_EXAMPLE_BATCH_AXIS.items():
            out[name] = _to_microbatches(out[name], axis)
    return {'x': out['x'], 'attn_norm': out['attn_norm'], 'w_in': out['w_in'], 'attn_sinks': out['attn_sinks'], 'gdn_conv_w': out['gdn_conv_w'], 'gdn_a_log': out['gdn_a_log'], 'gdn_dt_bias': out['gdn_dt_bias'], 'gdn_norm': out['gdn_norm'], 'w_out': out['w_out'], 'ffn_norm': out['ffn_norm'], 'w_ffn_in': out['w_ffn_in'], 'ffn_conv_w': out['ffn_conv_w'], 'ffn_conv_b': out['ffn_conv_b'], 'w_down': out['w_down'], 'final_norm': out['final_norm'], 'loss_target': out['loss_target'], 'm_attn_norm': out['m_attn_norm'], 'm_w_in': out['m_w_in'], 'm_attn_sinks': out['m_attn_sinks'], 'm_gdn_conv_w': out['m_gdn_conv_w'], 'm_gdn_a_log': out['m_gdn_a_log'], 'm_gdn_dt_bias': out['m_gdn_dt_bias'], 'm_gdn_norm': out['m_gdn_norm'], 'm_w_out': out['m_w_out'], 'm_ffn_norm': out['m_ffn_norm'], 'm_w_ffn_in': out['m_w_ffn_in'], 'm_ffn_conv_w': out['m_ffn_conv_w'], 'm_ffn_conv_b': out['m_ffn_conv_b'], 'm_w_down': out['m_w_down'], 'm_final_norm': out['m_final_norm'], 'v_attn_norm': out['v_attn_norm'], 'v_w_in': out['v_w_in'], 'v_attn_sinks': out['v_attn_sinks'], 'v_gdn_conv_w': out['v_gdn_conv_w'], 'v_gdn_a_log': out['v_gdn_a_log'], 'v_gdn_dt_bias': out['v_gdn_dt_bias'], 'v_gdn_norm': out['v_gdn_norm'], 'v_w_out': out['v_w_out'], 'v_ffn_norm': out['v_ffn_norm'], 'v_w_ffn_in': out['v_w_ffn_in'], 'v_ffn_conv_w': out['v_ffn_conv_w'], 'v_ffn_conv_b': out['v_ffn_conv_b'], 'v_w_down': out['v_w_down'], 'v_final_norm': out['v_final_norm']}


def _loss(weights, diff, rest, loss_target):
    with _jax.named_scope("forward"):
        args = {**rest, TWIN_DIFF_INPUT: diff, **{k: w.astype(_WEIGHT_DTYPES[k]) for k, w in weights.items()}}
        y = _forward(args)
    with _jax.named_scope("loss_head"):
        err = _jnp.square(y.astype(_jnp.float32) - loss_target)
        return 0.5 * _jnp.sum(_jnp.mean(err, axis=-1)) if err.ndim else 0.5 * err


def _adamw(w, g, m, v):
    m = ADAM_B1 * m + (1.0 - ADAM_B1) * g
    v = ADAM_B2 * v + (1.0 - ADAM_B2) * _jnp.square(g)
    m_hat = m / (1.0 - ADAM_B1 ** ADAM_STEP)
    v_hat = v / (1.0 - ADAM_B2 ** ADAM_STEP)
    delta = -ADAM_LR * (m_hat / (_jnp.sqrt(v_hat) + ADAM_EPS) + ADAM_WD * w)
    return delta, m, v


def reference(x, attn_norm, w_in, attn_sinks, gdn_conv_w, gdn_a_log, gdn_dt_bias, gdn_norm, w_out, ffn_norm, w_ffn_in, ffn_conv_w, ffn_conv_b, w_down, final_norm, loss_target, m_attn_norm, m_w_in, m_attn_sinks, m_gdn_conv_w, m_gdn_a_log, m_gdn_dt_bias, m_gdn_norm, m_w_out, m_ffn_norm, m_w_ffn_in, m_ffn_conv_w, m_ffn_conv_b, m_w_down, m_final_norm, v_attn_norm, v_w_in, v_attn_sinks, v_gdn_conv_w, v_gdn_a_log, v_gdn_dt_bias, v_gdn_norm, v_w_out, v_ffn_norm, v_w_ffn_in, v_ffn_conv_w, v_ffn_conv_b, v_w_down, v_final_norm):
    given = dict(x=x, attn_norm=attn_norm, w_in=w_in, attn_sinks=attn_sinks, gdn_conv_w=gdn_conv_w, gdn_a_log=gdn_a_log, gdn_dt_bias=gdn_dt_bias, gdn_norm=gdn_norm, w_out=w_out, ffn_norm=ffn_norm, w_ffn_in=w_ffn_in, ffn_conv_w=ffn_conv_w, ffn_conv_b=ffn_conv_b, w_down=w_down, final_norm=final_norm, loss_target=loss_target, m_attn_norm=m_attn_norm, m_w_in=m_w_in, m_attn_sinks=m_attn_sinks, m_gdn_conv_w=m_gdn_conv_w, m_gdn_a_log=m_gdn_a_log, m_gdn_dt_bias=m_gdn_dt_bias, m_gdn_norm=m_gdn_norm, m_w_out=m_w_out, m_ffn_norm=m_ffn_norm, m_w_ffn_in=m_w_ffn_in, m_ffn_conv_w=m_ffn_conv_w, m_ffn_conv_b=m_ffn_conv_b, m_w_down=m_w_down, m_final_norm=m_final_norm, v_attn_norm=v_attn_norm, v_w_in=v_w_in, v_attn_sinks=v_attn_sinks, v_gdn_conv_w=v_gdn_conv_w, v_gdn_a_log=v_gdn_a_log, v_gdn_dt_bias=v_gdn_dt_bias, v_gdn_norm=v_gdn_norm, v_w_out=v_w_out, v_ffn_norm=v_ffn_norm, v_w_ffn_in=v_w_ffn_in, v_ffn_conv_w=v_ffn_conv_w, v_ffn_conv_b=v_ffn_conv_b, v_w_down=v_w_down, v_final_norm=v_final_norm)
    weights = {n: given[n] for n in TWIN_WEIGHTS}
    shared = {n: given[n] for n in SHARED_INPUTS}
    per_example = {n: given[n] for n in ['x']}
    grad_fn = _jax.value_and_grad(_loss, argnums=(0, 1))

    def one_microbatch(ex, loss_target):
        ex = dict(ex)
        diff = ex.pop(TWIN_DIFF_INPUT)
        return grad_fn(weights, diff, {**shared, **ex}, loss_target)

    if N_MICROBATCH == 1:
        loss, (grad_w, grad_x) = one_microbatch(per_example, given["loss_target"])
    else:
        def body(carry, xs):
            loss_sum, grad_sum = carry
            l_k, (gw_k, gx_k) = one_microbatch(xs[0], xs[1])
            with _jax.named_scope("update"):
                return (loss_sum + l_k, _jax.tree.map(_jnp.add, grad_sum, gw_k)), gx_k

        init = (_jnp.zeros((), _jnp.float32), _jax.tree.map(_jnp.zeros_like, weights))
        (loss, grad_w), grad_x = _jax.lax.scan(body, init, (per_example, given["loss_target"]))
    with _jax.named_scope("update"):
        delta_w, new_m, new_v = {}, {}, {}
        for n in TWIN_WEIGHTS:
            delta_w[n], new_m[n], new_v[n] = _adamw(weights[n], grad_w[n], given["m_" + n], given["v_" + n])
    return (loss, grad_x, *[grad_w[n] for n in TWIN_WEIGHTS], *[delta_w[n] for n in TWIN_WEIGHTS],
            *[new_m[n] for n in TWIN_WEIGHTS], *[new_v[n] for n in TWIN_WEIGHTS])
```

```python
import functools

import jax
import jax.numpy as jnp
from jax import lax
from jax.experimental import pallas as pl
from jax.experimental.pallas import tpu as pltpu

F32, BF16 = jnp.float32, jnp.bfloat16
HIGHEST = lax.Precision.HIGHEST
MESH = pl.DeviceIdType.MESH

D = 1024
T = 4096
DEPTH = 2
HQ, HKV, DH, WIN = 8, 2, 64, 128
GH, GD, GC, GK = 4, 128, 64, 4
DFF, FK = 2816, 3
EPS = 1e-6
N_MAIN = 2816
N_PROJ = N_MAIN + 128
COL_K, COL_V, COL_G, COL_Z, COL_BA = 4, 5, 6, 18, 22
N_CHUNK = T // GC
N_DEV, N_CHIP = 8, 4
LR, B1, B2, AEPS, WD, STEP = 0.001, 0.9, 0.999, 1e-08, 0.01, 10

VMEM_LIMIT = 56 * 1024 * 1024


def _cp(*sem):
    return pltpu.CompilerParams(dimension_semantics=sem if sem else None, vmem_limit_bytes=VMEM_LIMIT)


def _dot(a, b):
    return jnp.dot(a.astype(BF16), b.astype(BF16), preferred_element_type=F32)


def _dot_nt(a, b):
    return lax.dot_general(a.astype(BF16), b.astype(BF16), (((1,), (1,)), ((), ())), preferred_element_type=F32)


def _dot_tn(a, b):
    return lax.dot_general(a.astype(BF16), b.astype(BF16), (((0,), (0,)), ((), ())), preferred_element_type=F32)


def _iota(shape, dim):
    return lax.broadcasted_iota(jnp.int32, shape, dim)


def _col(x, idx):
    return jnp.sum(jnp.where(_iota(x.shape, 1) == idx, x, 0.0), axis=1, keepdims=True)


def _silu(y):
    return y * jax.nn.sigmoid(y)


def _mm(a, b, *, name, tm, tn, tk, tn_mode=False, res=None, out_dtype=F32):
    if tn_mode:
        K, M = a.shape
    else:
        M, K = a.shape
    N = b.shape[1]
    assert b.shape[0] == K and M % tm == 0 and N % tn == 0 and K % tk == 0, (name, a.shape, b.shape)
    nk = K // tk
    dn = (((0,), (0,)), ((), ())) if tn_mode else (((1,), (0,)), ((), ()))

    def body(*refs):
        if res is None:
            a_ref, b_ref, o_ref, acc = refs
        else:
            a_ref, b_ref, r_ref, o_ref, acc = refs
        k = pl.program_id(2)

        @pl.when(k == 0)
        def _():
            acc[...] = jnp.zeros_like(acc)

        acc[...] += lax.dot_general(a_ref[...].astype(BF16), b_ref[...].astype(BF16), dn, preferred_element_type=F32)

        @pl.when(k == nk - 1)
        def _():
            r = acc[...]
            if res is not None:
                r = r + r_ref[...]
            o_ref[...] = r.astype(out_dtype)

    a_spec = pl.BlockSpec((tk, tm), lambda i, j, k: (k, i)) if tn_mode else pl.BlockSpec((tm, tk), lambda i, j, k: (i, k))
    in_specs = [a_spec, pl.BlockSpec((tk, tn), lambda i, j, k: (k, j))]
    args = [a, b]
    if res is not None:
        in_specs.append(pl.BlockSpec((tm, tn), lambda i, j, k: (i, j)))
        args.append(res)
    return pl.pallas_call(
        body, name=name, grid=(M // tm, N // tn, nk), in_specs=in_specs,
        out_specs=pl.BlockSpec((tm, tn), lambda i, j, k: (i, j)),
        out_shape=jax.ShapeDtypeStruct((M, N), out_dtype),
        scratch_shapes=[pltpu.VMEM((tm, tn), F32)],
        compiler_params=_cp("parallel", "parallel", "arbitrary"),
    )(*args)


def _rms(x, w):
    return x * lax.rsqrt(jnp.mean(x * x, axis=-1, keepdims=True) + EPS) * w


def _rmsnorm_fwd(x, w, *, name, tm=512):
    def body(x_ref, w_ref, o_ref):
        o_ref[...] = _rms(x_ref[...], w_ref[...]).astype(BF16)

    return pl.pallas_call(
        body, name=name, grid=(x.shape[0] // tm,),
        in_specs=[pl.BlockSpec((tm, D), lambda i: (i, 0)), pl.BlockSpec((1, D), lambda i: (0, 0))],
        out_specs=pl.BlockSpec((tm, D), lambda i: (i, 0)),
        out_shape=jax.ShapeDtypeStruct(x.shape, BF16), compiler_params=_cp("parallel"),
    )(x, w)


def _rmsnorm_bwd(x, w, dh, dres, *, name, tm=512):
    def body(x_ref, w_ref, dh_ref, dr_ref, dx_ref, dw_ref):
        _, vjp = jax.vjp(_rms, x_ref[...], w_ref[...])
        dx, dw = vjp(dh_ref[...])
        dx_ref[...] = dx + dr_ref[...]

        @pl.when(pl.program_id(0) == 0)
        def _():
            dw_ref[...] = jnp.zeros_like(dw_ref)

        dw_ref[...] += dw

    row = pl.BlockSpec((tm, D), lambda i: (i, 0))
    vec = pl.BlockSpec((1, D), lambda i: (0, 0))
    return pl.pallas_call(
        body, name=name, grid=(x.shape[0] // tm,), in_specs=[row, vec, row, row], out_specs=[row, vec],
        out_shape=[jax.ShapeDtypeStruct(x.shape, F32), jax.ShapeDtypeStruct((1, D), F32)],
        compiler_params=_cp("arbitrary"),
    )(x, w, dh, dres)


def _loss_head(x, w, tgt, *, name, tm=512):
    def f(xv, wv, tv):
        err = _rms(xv, wv) - tv
        per_row = jnp.sum(err * err, axis=1, keepdims=True) * (0.5 / D)
        return jnp.sum(per_row, axis=0, keepdims=True)

    def body(x_ref, w_ref, t_ref, dx_ref, dw_ref, loss_ref):
        tv = t_ref[...]
        loss, vjp = jax.vjp(lambda xv, wv: f(xv, wv, tv), x_ref[...], w_ref[...])
        dx, dw = vjp(jnp.ones((1, 1), F32))
        dx_ref[...] = dx

        @pl.when(pl.program_id(0) == 0)
        def _():
            dw_ref[...] = jnp.zeros_like(dw_ref)
            loss_ref[...] = jnp.zeros_like(loss_ref)

        dw_ref[...] += dw
        loss_ref[...] += jnp.broadcast_to(loss, loss_ref.shape)

    row = pl.BlockSpec((tm, D), lambda i: (i, 0))
    vec = pl.BlockSpec((1, D), lambda i: (0, 0))
    return pl.pallas_call(
        body, name=name, grid=(x.shape[0] // tm,), in_specs=[row, vec, row],
        out_specs=[row, vec, pl.BlockSpec((1, 128), lambda i: (0, 0))],
        out_shape=[jax.ShapeDtypeStruct(x.shape, F32), jax.ShapeDtypeStruct((1, D), F32),
                   jax.ShapeDtypeStruct((1, 128), F32)],
        compiler_params=_cp("arbitrary"),
    )(x, w, tgt)


def _swa_group(qg, k2, v2, sk, blk, h2):
    s = _dot_nt(qg, k2) * (DH ** -0.5)
    r, c = _iota(s.shape, 0), _iota(s.shape, 1)
    g = r >> 7
    rel = (r & (WIN - 1)) + WIN - c
    valid = (rel >= 0) & (rel < WIN) & ((blk > 0) | (c >= WIN))
    slope = jnp.zeros(s.shape, F32)
    g1 = _iota((s.shape[0], 1), 0) >> 7
    sink = jnp.zeros((s.shape[0], 1), F32)
    for gi in range(HQ // HKV):
        hq = h2 * (HQ // HKV) + gi
        slope = jnp.where(g == gi, 2.0 ** (-8.0 * (hq + 1) / HQ), slope)
        sink = jnp.where(g1 == gi, _col(sk, hq), sink)
    s = jnp.where(valid, s - slope * rel.astype(F32), -1e30)
    m = lax.stop_gradient(jnp.maximum(jnp.max(s, axis=1, keepdims=True), sink))
    p = jnp.exp(s - m)
    den = jnp.sum(p, axis=1, keepdims=True) + jnp.exp(sink - m)
    return _dot(p / den, v2)


def _swa_split_q(q, h2):
    G = HQ // HKV
    return jnp.concatenate([q[:, (h2 * G + g) * DH:(h2 * G + g + 1) * DH] for g in range(G)], axis=0)


def _swa_merge_q(parts):
    G = HQ // HKV
    return jnp.concatenate([parts[h2][g * WIN:(g + 1) * WIN] for h2 in range(HKV) for g in range(G)], axis=1)


def _swa_specs():
    prev = lambda i: jnp.maximum(jnp.minimum(i, T // WIN - 1) - 1, 0)
    cur = lambda i: jnp.minimum(i, T // WIN - 1)
    return [
        pl.BlockSpec((WIN, HQ * DH), lambda i: (cur(i), 0)),
        pl.BlockSpec((WIN, 128), lambda i: (prev(i), COL_K)),
        pl.BlockSpec((WIN, 128), lambda i: (cur(i), COL_K)),
        pl.BlockSpec((WIN, 128), lambda i: (prev(i), COL_V)),
        pl.BlockSpec((WIN, 128), lambda i: (cur(i), COL_V)),
        pl.BlockSpec((1, 128), lambda i: (0, 0)),
    ]


def _swa_fwd(proj, sinks, *, name):
    def body(q_ref, kp_ref, kc_ref, vp_ref, vc_ref, sk_ref, o_ref):
        i = pl.program_id(0)
        q, sk = q_ref[...], sk_ref[...]
        outs = []
        for h2 in range(HKV):
            sl = slice(h2 * DH, (h2 + 1) * DH)
            k2 = jnp.concatenate([kp_ref[:, sl], kc_ref[:, sl]], axis=0)
            v2 = jnp.concatenate([vp_ref[:, sl], vc_ref[:, sl]], axis=0)
            outs.append(_swa_group(_swa_split_q(q, h2), k2, v2, sk, i, h2))
        o_ref[...] = _swa_merge_q(outs).astype(BF16)

    return pl.pallas_call(
        body, name=name, grid=(T // WIN,), in_specs=_swa_specs(),
        out_specs=pl.BlockSpec((WIN, HQ * DH), lambda i: (i, 0)),
        out_shape=jax.ShapeDtypeStruct((T, HQ * DH), BF16), compiler_params=_cp("parallel"),
    )(proj, proj, proj, proj, proj, sinks)


def _swa_bwd(proj, sinks, do, *, name):
    NB = T // WIN

    def body(q_ref, kp_ref, kc_ref, vp_ref, vc_ref, sk_ref, do_ref, dq_ref, dk_ref, dv_ref, dsk_ref, ck, cv):
        i = pl.program_id(0)

        @pl.when(i == 0)
        def _():
            ck[...] = jnp.zeros_like(ck)
            cv[...] = jnp.zeros_like(cv)
            dsk_ref[...] = jnp.zeros_like(dsk_ref)

        @pl.when(i < NB)
        def _():
            q, sk, dov = q_ref[...], sk_ref[...], do_ref[...].astype(F32)
            dqs, dkp, dkc, dvp, dvc = [], [], [], [], []
            dsk = jnp.zeros_like(sk)
            for h2 in range(HKV):
                sl = slice(h2 * DH, (h2 + 1) * DH)
                k2 = jnp.concatenate([kp_ref[:, sl], kc_ref[:, sl]], axis=0)
                v2 = jnp.concatenate([vp_ref[:, sl], vc_ref[:, sl]], axis=0)
                _, vjp = jax.vjp(functools.partial(_swa_group, blk=i, h2=h2), _swa_split_q(q, h2), k2, v2, sk)
                dqg, dk2, dv2, dsk_h = vjp(_swa_split_q(dov, h2))
                dqs.append(dqg)
                dkp.append(dk2[:WIN]); dkc.append(dk2[WIN:])
                dvp.append(dv2[:WIN]); dvc.append(dv2[WIN:])
                dsk = dsk + dsk_h
            dq_ref[...] = _swa_merge_q(dqs).astype(BF16)
            dk_ref[...] = (ck[...] + jnp.concatenate(dkp, axis=1)).astype(BF16)
            dv_ref[...] = (cv[...] + jnp.concatenate(dvp, axis=1)).astype(BF16)
            ck[...] = jnp.concatenate(dkc, axis=1)
            cv[...] = jnp.concatenate(dvc, axis=1)
            dsk_ref[...] += dsk

        @pl.when(i == NB)
        def _():
            dk_ref[...] = ck[...].astype(BF16)
            dv_ref[...] = cv[...].astype(BF16)

    cur = lambda i: jnp.minimum(i, NB - 1)
    keyblk = pl.BlockSpec((WIN, 128), lambda i: (jnp.maximum(i - 1, 0), 0))
    qblk = pl.BlockSpec((WIN, HQ * DH), lambda i: (cur(i), 0))
    return pl.pallas_call(
        body, name=name, grid=(NB + 1,), in_specs=_swa_specs() + [qblk],
        out_specs=[qblk, keyblk, keyblk, pl.BlockSpec((1, 128), lambda i: (0, 0))],
        out_shape=[jax.ShapeDtypeStruct((T, HQ * DH), BF16), jax.ShapeDtypeStruct((T, 128), BF16),
                   jax.ShapeDtypeStruct((T, 128), BF16), jax.ShapeDtypeStruct((1, 128), F32)],
        scratch_shapes=[pltpu.VMEM((WIN, 128), F32), pltpu.VMEM((WIN, 128), F32)],
        compiler_params=_cp("arbitrary"),
    )(proj, proj, proj, proj, proj, sinks, do)


RC = 256
HALO = 8


def _load_ext(ref, c):
    nch = T // RC
    r0 = pl.multiple_of(c * RC, RC)
    p0 = pl.multiple_of(jnp.maximum(r0 - HALO, 0), HALO)
    n0 = pl.multiple_of(jnp.minimum(r0 + RC, T - HALO), HALO)
    prev = jnp.where(c > 0, ref[pl.ds(p0, HALO), :].astype(F32), 0.0)
    nxt = jnp.where(c < nch - 1, ref[pl.ds(n0, HALO), :].astype(F32), 0.0)
    return jnp.concatenate([prev, ref[pl.ds(r0, RC), :].astype(F32), nxt], axis=0)


def _conv_ext(xe, w, K):
    y = w[K - 1:K, :] * xe
    for s in range(1, K):
        y = y + w[K - 1 - s:K - s, :] * pltpu.roll(xe, s, 0)
    return y


def _conv_bwd_ext(xe, dye, w, K, dw_ref):
    n = xe.shape[0]
    own = slice(HALO, HALO + RC)
    dx = w[K - 1:K, :] * dye
    dw_ref[K - 1:K, :] += jnp.sum(dye[own] * xe[own], axis=0, keepdims=True)
    for s in range(1, K):
        dx = dx + w[K - 1 - s:K - s, :] * pltpu.roll(dye, n - s, 0)
        dw_ref[K - 1 - s:K - s, :] += jnp.sum(dye[own] * pltpu.roll(xe, s, 0)[own], axis=0, keepdims=True)
    return dx[own]


def _gdn_post_conv(y, is_qk):
    a = _silu(y)
    nrm = a * lax.rsqrt(jnp.sum(a * a, axis=1, keepdims=True) + EPS)
    return jnp.where(is_qk, nrm, a)


def _gdn_prep_fwd(proj, conv_w, *, name):
    nblk = 3 * GH

    def body(x_ref, w_ref, o_ref):
        is_qk = pl.program_id(0) < 2 * GH
        w = w_ref[...]

        def chunk(c, carry):
            y = _conv_ext(_load_ext(x_ref, c), w, GK)[HALO:HALO + RC]
            o_ref[pl.ds(pl.multiple_of(c * RC, RC), RC), :] = _gdn_post_conv(y, is_qk)
            return carry

        lax.fori_loop(0, T // RC, chunk, 0)

    return pl.pallas_call(
        body, name=name, grid=(nblk,),
        in_specs=[pl.BlockSpec((T, 128), lambda j: (0, COL_G + j)), pl.BlockSpec((GK, 128), lambda j: (0, j))],
        out_specs=pl.BlockSpec((T, 128), lambda j: (0, j)),
        out_shape=jax.ShapeDtypeStruct((T, nblk * 128), F32), compiler_params=_cp("parallel"),
    )(proj, conv_w)


def _gdn_prep_bwd(proj, conv_w, dout, *, name):
    nblk = 3 * GH

    def body(x_ref, w_ref, d_ref, dx_ref, dw_ref):
        is_qk = pl.program_id(0) < 2 * GH
        w = w_ref[...]
        dw_ref[...] = jnp.zeros_like(dw_ref)

        def chunk(c, carry):
            xe = _load_ext(x_ref, c)
            _, vjp = jax.vjp(lambda y: _gdn_post_conv(y, is_qk), _conv_ext(xe, w, GK))
            (dye,) = vjp(_load_ext(d_ref, c))
            dx_ref[pl.ds(pl.multiple_of(c * RC, RC), RC), :] = _conv_bwd_ext(xe, dye, w, GK, dw_ref).astype(BF16)
            return carry

        lax.fori_loop(0, T // RC, chunk, 0)

    return pl.pallas_call(
        body, name=name, grid=(nblk,),
        in_specs=[pl.BlockSpec((T, 128), lambda j: (0, COL_G + j)), pl.BlockSpec((GK, 128), lambda j: (0, j)),
                  pl.BlockSpec((T, 128), lambda j: (0, j))],
        out_specs=[pl.BlockSpec((T, 128), lambda j: (0, j)), pl.BlockSpec((GK, 128), lambda j: (0, j))],
        out_shape=[jax.ShapeDtypeStruct((T, nblk * 128), BF16), jax.ShapeDtypeStruct((GK, nblk * 128), F32)],
        compiler_params=_cp("parallel"),
    )(proj, conv_w, dout)


def _ffn_post_conv(y, b, up):
    return _silu(y + b) * up


def _ffn_act_fwd(gu, conv_w, conv_b, *, name):
    nblk = DFF // 128

    def body(g_ref, u_ref, w_ref, b_ref, o_ref):
        w, b = w_ref[...], b_ref[...]

        def chunk(c, carry):
            rows = pl.ds(pl.multiple_of(c * RC, RC), RC)
            y = _conv_ext(_load_ext(g_ref, c), w, FK)[HALO:HALO + RC]
            o_ref[rows, :] = _ffn_post_conv(y, b, u_ref[rows, :]).astype(BF16)
            return carry

        lax.fori_loop(0, T // RC, chunk, 0)

    return pl.pallas_call(
        body, name=name, grid=(nblk,),
        in_specs=[pl.BlockSpec((T, 128), lambda j: (0, j)), pl.BlockSpec((T, 128), lambda j: (0, nblk + j)),
                  pl.BlockSpec((FK, 128), lambda j: (0, j)), pl.BlockSpec((1, 128), lambda j: (0, j))],
        out_specs=pl.BlockSpec((T, 128), lambda j: (0, j)),
        out_shape=jax.ShapeDtypeStruct((T, DFF), BF16), compiler_params=_cp("parallel"),
    )(gu, gu, conv_w, conv_b)


def _ffn_act_bwd(gu, conv_w, conv_b, dact, *, name):
    nblk = DFF // 128

    def body(g_ref, u_ref, w_ref, b_ref, d_ref, dg_ref, du_ref, dw_ref, db_ref):
        w, b = w_ref[...], b_ref[...]
        dw_ref[...] = jnp.zeros_like(dw_ref)
        db_ref[...] = jnp.zeros_like(db_ref)

        def chunk(c, carry):
            rows = pl.ds(pl.multiple_of(c * RC, RC), RC)
            xe = _load_ext(g_ref, c)
            ue = _load_ext(u_ref, c)
            _, vjp = jax.vjp(_ffn_post_conv, _conv_ext(xe, w, FK), b, ue)
            dye, db, due = vjp(_load_ext(d_ref, c))
            du_ref[rows, :] = due[HALO:HALO + RC].astype(BF16)
            db_ref[...] += jnp.sum(dye[HALO:HALO + RC], axis=0, keepdims=True)
            dg_ref[rows, :] = _conv_bwd_ext(xe, dye, w, FK, dw_ref).astype(BF16)
            return carry

        lax.fori_loop(0, T // RC, chunk, 0)

    col = pl.BlockSpec((T, 128), lambda j: (0, j))
    return pl.pallas_call(
        body, name=name, grid=(nblk,),
        in_specs=[col, pl.BlockSpec((T, 128), lambda j: (0, nblk + j)), pl.BlockSpec((FK, 128), lambda j: (0, j)),
                  pl.BlockSpec((1, 128), lambda j: (0, j)), col],
        out_specs=[col, col, pl.BlockSpec((FK, 128), lambda j: (0, j)), pl.BlockSpec((1, 128), lambda j: (0, j))],
        out_shape=[jax.ShapeDtypeStruct((T, DFF), BF16), jax.ShapeDtypeStruct((T, DFF), BF16),
                   jax.ShapeDtypeStruct((FK, DFF), F32), jax.ShapeDtypeStruct((1, DFF), F32)],
        compiler_params=_cp("parallel"),
    )(gu, gu, conv_w, conv_b, dact)


def _gdn_gates(ba, alog, dtb):
    beta = jax.nn.sigmoid(ba)
    g = -jnp.exp(alog) * jax.nn.softplus(ba + dtb)
    tril = (_iota((GC, GC), 0) >= _iota((GC, GC), 1)).astype(F32)
    return beta, jnp.dot(tril, g, precision=HIGHEST, preferred_element_type=F32)


def _gdn_decay(Gc):
    r, c = _iota((GC, GC), 0), _iota((GC, GC), 1)
    eye = (r == c).astype(F32)
    grow = jnp.dot(jnp.ones((GC, GC), F32), eye * Gc, precision=HIGHEST, preferred_element_type=F32)
    return jnp.exp(jnp.where(r >= c, Gc - grow, -1e30))


def _gdn_A(k, beta, Gc):
    strict = _iota((GC, GC), 0) > _iota((GC, GC), 1)
    return jnp.where(strict, _dot_nt(k * beta, k) * _gdn_decay(Gc), 0.0)


def _tri_inv(A):
    eye = (_iota((GC, GC), 0) == _iota((GC, GC), 1)).astype(F32)
    Tm, P = eye - A, A
    for _ in range(GC.bit_length() - 2):
        P = _dot(P, P)
        Tm = Tm + _dot(Tm, P)
    return Tm


def _gdn_chunk(q, k, v, beta, Gc, S, Tm):
    decay = _gdn_decay(Gc)
    eG = jnp.exp(Gc)
    kb = k * beta
    u = _dot(Tm, v * beta)
    w = _dot(Tm, kb * eG)
    qs = q * (GD ** -0.5)
    qk = _dot_nt(qs, k) * decay
    glast = jnp.sum(jnp.where(_iota(Gc.shape, 0) == GC - 1, Gc, 0.0), axis=0, keepdims=True)
    kd = k * jnp.exp(glast - Gc)
    v_new = u - _dot(w, S)
    o = _dot(qs * eG, S) + _dot(qk, v_new)
    return o, S * jnp.exp(glast) + _dot_tn(kd, v_new)


def _gdn_chunk_fwd(qkv_c, proj, alog, dtb, *, name):
    W3 = 3 * GH * GD

    def body(x_ref, ba_ref, al_ref, dt_ref, o_ref, s_ref, t_ref, S):
        @pl.when(pl.program_id(0) == 0)
        def _():
            S[...] = jnp.zeros_like(S)

        beta_all, gc_all = _gdn_gates(ba_ref[...], al_ref[...], dt_ref[...])
        for h in range(GH):
            q = x_ref[:, h * GD:(h + 1) * GD]
            k = x_ref[:, (GH + h) * GD:(GH + h + 1) * GD]
            v = x_ref[:, (2 * GH + h) * GD:(2 * GH + h + 1) * GD]
            beta, Gc = _col(beta_all, h), _col(gc_all, GH + h)
            Tm = _tri_inv(_gdn_A(k, beta, Gc))
            Sh = S[h]
            s_ref[0, h] = Sh
            t_ref[0, h] = Tm
            o, S_new = _gdn_chunk(q, k, v, beta, Gc, Sh, Tm)
            o_ref[:, h * GD:(h + 1) * GD] = o
            S[h] = S_new

    vec = pl.BlockSpec((1, 128), lambda n: (0, 0))
    return pl.pallas_call(
        body, name=name, grid=(N_CHUNK,),
        in_specs=[pl.BlockSpec((GC, W3), lambda n: (n, 0)), pl.BlockSpec((GC, 128), lambda n: (n, COL_BA)), vec, vec],
        out_specs=[pl.BlockSpec((GC, GH * GD), lambda n: (n, 0)),
                   pl.BlockSpec((1, GH, GD, GD), lambda n: (n, 0, 0, 0)),
                   pl.BlockSpec((1, GH, GC, GC), lambda n: (n, 0, 0, 0))],
        out_shape=[jax.ShapeDtypeStruct((T, GH * GD), F32), jax.ShapeDtypeStruct((N_CHUNK, GH, GD, GD), F32),
                   jax.ShapeDtypeStruct((N_CHUNK, GH, GC, GC), F32)],
        scratch_shapes=[pltpu.VMEM((GH, GD, GD), F32)], compiler_params=_cp("arbitrary"),
    )(qkv_c, proj, alog, dtb)


def _gdn_chunk_bwd(qkv_c, proj, alog, dtb, s_all, t_all, do, *, name):
    W3 = 3 * GH * GD
    rev = lambda n: N_CHUNK - 1 - n

    def body(x_ref, ba_ref, al_ref, dt_ref, s_ref, t_ref, do_ref, dx_ref, dba_ref, dal_ref, ddt_ref, dS):
        @pl.when(pl.program_id(0) == 0)
        def _():
            dS[...] = jnp.zeros_like(dS)
            dal_ref[...] = jnp.zeros_like(dal_ref)
            ddt_ref[...] = jnp.zeros_like(ddt_ref)

        (beta_all, gc_all), vjp_gates = jax.vjp(_gdn_gates, ba_ref[...], al_ref[...], dt_ref[...])
        dbeta_all, dgc_all = jnp.zeros_like(beta_all), jnp.zeros_like(gc_all)
        for h in range(GH):
            q = x_ref[:, h * GD:(h + 1) * GD]
            k = x_ref[:, (GH + h) * GD:(GH + h + 1) * GD]
            v = x_ref[:, (2 * GH + h) * GD:(2 * GH + h + 1) * GD]
            Tm = t_ref[0, h]

            def f(q, k, v, b_all, g_all, Sh, Tm, h=h):
                return _gdn_chunk(q, k, v, _col(b_all, h), _col(g_all, GH + h), Sh, Tm)

            def fa(k, b_all, g_all, h=h):
                return _gdn_A(k, _col(b_all, h), _col(g_all, GH + h))

            _, vjp = jax.vjp(f, q, k, v, beta_all, gc_all, s_ref[0, h], Tm)
            dq, dk, dv, db1, dg1, dS_prev, dT = vjp((do_ref[:, h * GD:(h + 1) * GD], dS[h]))
            dA = -_dot_tn(Tm, _dot_nt(dT, Tm))
            _, vjp_a = jax.vjp(fa, k, beta_all, gc_all)
            dk2, db2, dg2 = vjp_a(dA)
            dS[h] = dS_prev
            dx_ref[:, h * GD:(h + 1) * GD] = dq
            dx_ref[:, (GH + h) * GD:(GH + h + 1) * GD] = dk + dk2
            dx_ref[:, (2 * GH + h) * GD:(2 * GH + h + 1) * GD] = dv
            dbeta_all = dbeta_all + db1 + db2
            dgc_all = dgc_all + dg1 + dg2
        dba, dal, ddt = vjp_gates((dbeta_all, dgc_all))
        dba_ref[...] = dba.astype(BF16)
        dal_ref[...] += dal
        ddt_ref[...] += ddt

    vec = pl.BlockSpec((1, 128), lambda n: (0, 0))
    return pl.pallas_call(
        body, name=name, grid=(N_CHUNK,),
        in_specs=[pl.BlockSpec((GC, W3), lambda n: (rev(n), 0)), pl.BlockSpec((GC, 128), lambda n: (rev(n), COL_BA)),
                  vec, vec, pl.BlockSpec((1, GH, GD, GD), lambda n: (rev(n), 0, 0, 0)),
                  pl.BlockSpec((1, GH, GC, GC), lambda n: (rev(n), 0, 0, 0)),
                  pl.BlockSpec((GC, GH * GD), lambda n: (rev(n), 0))],
        out_specs=[pl.BlockSpec((GC, W3), lambda n: (rev(n), 0)), pl.BlockSpec((GC, 128), lambda n: (rev(n), 0)), vec, vec],
        out_shape=[jax.ShapeDtypeStruct((T, W3), F32), jax.ShapeDtypeStruct((T, 128), BF16),
                   jax.ShapeDtypeStruct((1, 128), F32), jax.ShapeDtypeStruct((1, 128), F32)],
        scratch_shapes=[pltpu.VMEM((GH, GD, GD), F32)], compiler_params=_cp("arbitrary"),
    )(qkv_c, proj, alog, dtb, s_all, t_all, do)


def _gdn_post(o, z, nw):
    return o * lax.rsqrt(jnp.mean(o * o, axis=-1, keepdims=True) + EPS) * nw * _silu(z)


def _gdn_post_fwd(o_raw, proj, nw, *, name, tm=512):
    def body(o_ref, z_ref, w_ref, out_ref):
        out_ref[...] = _gdn_post(o_ref[...], z_ref[...], w_ref[...]).astype(BF16)

    return pl.pallas_call(
        body, name=name, grid=(T // tm, GH),
        in_specs=[pl.BlockSpec((tm, GD), lambda i, h: (i, h)), pl.BlockSpec((tm, GD), lambda i, h: (i, COL_Z + h)),
                  pl.BlockSpec((1, GD), lambda i, h: (0, 0))],
        out_specs=pl.BlockSpec((tm, GD), lambda i, h: (i, h)),
        out_shape=jax.ShapeDtypeStruct((T, GH * GD), BF16), compiler_params=_cp("parallel", "parallel"),
    )(o_raw, proj, nw)


def _gdn_post_bwd(o_raw, proj, nw, dmixed, *, name, tm=512):
    def body(o_ref, z_ref, w_ref, d_ref, do_ref, dz_ref, dw_ref):
        _, vjp = jax.vjp(_gdn_post, o_ref[...], z_ref[...], w_ref[...])
        do, dz, dw = vjp(d_ref[...])
        do_ref[...] = do
        dz_ref[...] = dz.astype(BF16)

        @pl.when((pl.program_id(0) == 0) & (pl.program_id(1) == 0))
        def _():
            dw_ref[...] = jnp.zeros_like(dw_ref)

        dw_ref[...] += dw

    blk = pl.BlockSpec((tm, GD), lambda i, h: (i, h))
    vec = pl.BlockSpec((1, GD), lambda i, h: (0, 0))
    return pl.pallas_call(
        body, name=name, grid=(T // tm, GH),
        in_specs=[blk, pl.BlockSpec((tm, GD), lambda i, h: (i, COL_Z + h)), vec,
                  pl.BlockSpec((tm, GD), lambda i, h: (i, GH + h))],
        out_specs=[blk, blk, vec],
        out_shape=[jax.ShapeDtypeStruct((T, GH * GD), F32), jax.ShapeDtypeStruct((T, GH * GD), BF16),
                   jax.ShapeDtypeStruct((1, GD), F32)],
        compiler_params=_cp("arbitrary", "arbitrary"),
    )(o_raw, proj, nw, dmixed)


def _adamw(w, g, m, v, *, name):
    shape = w.shape
    cols = shape[-1]
    w2, g2, m2, v2 = (a.reshape(-1, cols) for a in (w, g, m, v))
    rows = w2.shape[0]
    tr = next((t for t in (512, 256, 128, 64, 32, 16, 8) if rows % t == 0), rows)

    def body(w_ref, g_ref, m_ref, v_ref, d_ref, nm_ref, nv_ref):
        gv = g_ref[...]
        nm = B1 * m_ref[...] + (1.0 - B1) * gv
        nv = B2 * v_ref[...] + (1.0 - B2) * jnp.square(gv)
        m_hat = nm / (1.0 - B1 ** STEP)
        v_hat = nv / (1.0 - B2 ** STEP)
        d_ref[...] = -LR * (m_hat / (jnp.sqrt(v_hat) + AEPS) + WD * w_ref[...])
        nm_ref[...] = nm
        nv_ref[...] = nv

    blk = pl.BlockSpec((tr, cols), lambda i: (i, 0))
    out = pl.pallas_call(
        body, name=name, grid=(rows // tr,), in_specs=[blk] * 4, out_specs=[blk] * 3,
        out_shape=[jax.ShapeDtypeStruct((rows, cols), F32)] * 3, compiler_params=_cp("parallel"),
    )(w2, g2, m2, v2)
    return tuple(o.reshape(shape) for o in out)


def _layer_weights(w_in, w_out, w_ffn, w_down):
    w_all = jnp.concatenate([w_in[:, :N_MAIN], jnp.pad(w_in[:, N_MAIN:], ((0, 0), (0, 128 - 2 * GH)))], axis=1)
    return dict(all=w_all, allT=w_all.T, out=w_out, outT=w_out.T, ffn=w_ffn, gT=w_ffn[:, :DFF].T, uT=w_ffn[:, DFF:].T,
                down=w_down, downT=w_down.T)


def _layer_params(attn_norm, sinks, gcw, a_log, dt_bias, gnw, ffn_norm, fcw, fcb):
    lanes4 = lambda v: jnp.pad(v, (GH, 128 - 2 * GH))[None]
    return dict(attn_norm=attn_norm[None], sinks=jnp.pad(sinks, (0, 128 - HQ))[None], gcw=gcw, alog=lanes4(a_log),
                dtb=lanes4(dt_bias), gnw=gnw[None], ffn_norm=ffn_norm[None], fcw=fcw, fcb=fcb[None])


def _layer_fwd(x, W, P, l):
    n = lambda s: f"l{l}_{s}"
    h = _rmsnorm_fwd(x, P["attn_norm"], name=n("norm1"))
    proj = _mm(h, W["all"], name=n("proj"), tm=512, tn=N_PROJ, tk=D)
    attn = _swa_fwd(proj, P["sinks"], name=n("swa"))
    qkv_c = _gdn_prep_fwd(proj, P["gcw"], name=n("gdn_prep"))
    o_raw, s_all, t_all = _gdn_chunk_fwd(qkv_c, proj, P["alog"], P["dtb"], name=n("gdn_chunk"))
    gdn = _gdn_post_fwd(o_raw, proj, P["gnw"], name=n("gdn_post"))
    mixed = jnp.concatenate([attn, gdn], axis=1)
    x1 = _mm(mixed, W["out"], res=x, name=n("out_proj"), tm=512, tn=D, tk=D)
    h2 = _rmsnorm_fwd(x1, P["ffn_norm"], name=n("norm2"))
    gu = _mm(h2, W["ffn"], name=n("ffn_in"), tm=512, tn=DFF // 2, tk=D)
    act = _ffn_act_fwd(gu, P["fcw"], P["fcb"], name=n("ffn_act"))
    x2 = _mm(act, W["down"], res=x1, name=n("ffn_down"), tm=512, tn=D, tk=DFF)
    saved = dict(x=x, h=h, proj=proj, qkv_c=qkv_c, o_raw=o_raw, s_all=s_all, t_all=t_all, mixed=mixed, x1=x1, h2=h2,
                 gu=gu, act=act)
    return x2, saved


def _layer_bwd(dx2, sv, W, P, l):
    n = lambda s: f"l{l}_{s}"
    dact = _mm(dx2, W["downT"], name=n("d_act"), tm=512, tn=DFF // 2, tk=D, out_dtype=BF16)
    g_down = _mm(sv["act"], dx2, tn_mode=True, name=n("g_down"), tm=DFF // 2, tn=D, tk=1024)
    dgate, dup, g_fcw, g_fcb = _ffn_act_bwd(sv["gu"], P["fcw"], P["fcb"], dact, name=n("d_ffn_act"))
    dh2 = _mm(dgate, W["gT"], name=n("d_h2_gate"), tm=512, tn=D, tk=DFF)
    dh2 = _mm(dup, W["uT"], res=dh2, name=n("d_h2_up"), tm=512, tn=D, tk=DFF)
    g_ffn_gate = _mm(sv["h2"], dgate, tn_mode=True, name=n("g_ffn_gate"), tm=D, tn=DFF // 2, tk=1024)
    g_ffn_up = _mm(sv["h2"], dup, tn_mode=True, name=n("g_ffn_up"), tm=D, tn=DFF // 2, tk=1024)
    dx1, g_ffn_norm = _rmsnorm_bwd(sv["x1"], P["ffn_norm"], dh2, dx2, name=n("d_norm2"))
    dmixed = _mm(dx1, W["outT"], name=n("d_mixed"), tm=512, tn=D, tk=D)
    g_out = _mm(sv["mixed"], dx1, tn_mode=True, name=n("g_out"), tm=D, tn=D, tk=1024)
    do_raw, dz, g_gnw = _gdn_post_bwd(sv["o_raw"], sv["proj"], P["gnw"], dmixed, name=n("d_gdn_post"))
    dqkv_c, dba, g_alog, g_dtb = _gdn_chunk_bwd(sv["qkv_c"], sv["proj"], P["alog"], P["dtb"], sv["s_all"], sv["t_all"],
                                                 do_raw, name=n("d_gdn_chunk"))
    dqkv_g, g_gcw = _gdn_prep_bwd(sv["proj"], P["gcw"], dqkv_c, name=n("d_gdn_prep"))
    dq, dk, dv, g_sinks = _swa_bwd(sv["proj"], P["sinks"], dmixed, name=n("d_swa"))
    dproj = jnp.concatenate([dq, dk, dv, dqkv_g, dz, dba], axis=1)
    dh = _mm(dproj, W["allT"], name=n("d_h"), tm=512, tn=D, tk=N_PROJ)
    g_all = _mm(sv["h"], dproj, tn_mode=True, name=n("g_in"), tm=512, tn=N_PROJ, tk=1024)
    dx, g_attn_norm = _rmsnorm_bwd(sv["x"], P["attn_norm"], dh, dx1, name=n("d_norm1"))
    grads = dict(
        attn_norm=g_attn_norm[0], w_in=g_all[:, :N_MAIN + 2 * GH], attn_sinks=g_sinks[0, :HQ], gdn_conv_w=g_gcw,
        gdn_a_log=g_alog[0, GH:2 * GH], gdn_dt_bias=g_dtb[0, GH:2 * GH], gdn_norm=g_gnw[0], w_out=g_out,
        ffn_norm=g_ffn_norm[0], w_ffn_in=jnp.concatenate([g_ffn_gate, g_ffn_up], axis=1), ffn_conv_w=g_fcw,
        ffn_conv_b=g_fcb[0], w_down=g_down)
    return dx, grads


ANY = pl.BlockSpec(memory_space=pl.ANY)
VMEM_SPEC = pl.BlockSpec(memory_space=pltpu.VMEM)


def _pos():
    return lax.axis_index("x"), lax.axis_index("y"), lax.axis_index("c")


def _other_chips(x, y):
    return [(1 - x, y), (x, 1 - y), (1 - x, 1 - y)]


def _remote(src, dst, send_sems, recv_sems, k, to):
    return pltpu.make_async_remote_copy(src_ref=src, dst_ref=dst, send_sem=send_sems.at[k], recv_sem=recv_sems.at[k],
                                        device_id=to, device_id_type=MESH)


def _ag_big(wpack, *, name):
    def body(w_ref, out_ref, send_sems, recv_sems, local_sem):
        x, y, c = _pos()
        me, sibling = (x, y, c), (x, y, 1 - c)
        chips = _other_chips(x, y)
        half = lambda px, py, pc: out_ref.at[2 * px + py, pc]
        mine = pltpu.make_async_copy(w_ref, out_ref.at[2 * x + y], local_sem)
        mine.start()
        first = [_remote(w_ref.at[c], half(x, y, c), send_sems, recv_sems, k, (*chip, c)) for k, chip in enumerate(chips)]
        for cp in first:
            cp.start()
        passed = [_remote(half(*chip, c), half(*chip, c), send_sems, recv_sems, 3 + k, sibling) for k, chip in enumerate(chips)]
        for k, chip in enumerate(chips):
            _remote(half(*chip, c), half(*chip, c), send_sems, recv_sems, k, me).wait_recv()
            passed[k].start()
        for k, chip in enumerate(chips):
            _remote(half(*chip, 1 - c), half(*chip, 1 - c), send_sems, recv_sems, 3 + k, me).wait_recv()
        for cp in first + passed:
            cp.wait_send()
        mine.wait()

    return pl.pallas_call(
        body, name=name, in_specs=[ANY], out_specs=ANY,
        out_shape=jax.ShapeDtypeStruct((N_CHIP,) + wpack.shape, wpack.dtype),
        scratch_shapes=[pltpu.SemaphoreType.DMA((6,)), pltpu.SemaphoreType.DMA((6,)), pltpu.SemaphoreType.DMA],
    )(wpack)


def _ag_small(v, *, name):
    m, n = v.shape

    def body(x_ref, out_ref, red_ref, send_sems, recv_sems, local_sem):
        x, y, c = _pos()
        me, sibling = (x, y, c), (x, y, 1 - c)
        chips = _other_chips(x, y)
        rows = lambda px, py, pc: out_ref.at[pl.ds(pl.multiple_of((4 * px + 2 * py + pc) * m, 8), m), :]
        mine = pltpu.make_async_copy(x_ref, rows(*me), local_sem)
        mine.start()
        first = [_remote(x_ref, rows(*me), send_sems, recv_sems, 0, sibling)]
        first += [_remote(x_ref, rows(*me), send_sems, recv_sems, 1 + k, (*chip, c)) for k, chip in enumerate(chips)]
        for cp in first:
            cp.start()
        passed = [_remote(rows(*chip, c), rows(*chip, c), send_sems, recv_sems, 4 + k, sibling) for k, chip in enumerate(chips)]
        for k, chip in enumerate(chips):
            _remote(rows(*chip, c), rows(*chip, c), send_sems, recv_sems, 1 + k, me).wait_recv()
            passed[k].start()
        _remote(rows(*sibling), rows(*sibling), send_sems, recv_sems, 0, me).wait_recv()
        for k, chip in enumerate(chips):
            _remote(rows(*chip, 1 - c), rows(*chip, 1 - c), send_sems, recv_sems, 4 + k, me).wait_recv()
        for cp in first + passed:
            cp.wait_send()
        mine.wait()
        acc = out_ref[0:m, :]
        for d in range(1, N_DEV):
            acc = acc + out_ref[d * m:(d + 1) * m, :]
        red_ref[...] = acc

    return pl.pallas_call(
        body, name=name, in_specs=[VMEM_SPEC], out_specs=[VMEM_SPEC, VMEM_SPEC],
        out_shape=[jax.ShapeDtypeStruct((N_DEV * m, n), v.dtype), jax.ShapeDtypeStruct((m, n), v.dtype)],
        scratch_shapes=[pltpu.SemaphoreType.DMA((7,)), pltpu.SemaphoreType.DMA((7,)), pltpu.SemaphoreType.DMA],
    )(v)


def _swap_with_sibling(a, *, name):
    def body(a_ref, out_ref, send_sem, recv_sem):
        x, y, c = _pos()
        cp = pltpu.make_async_remote_copy(src_ref=a_ref, dst_ref=out_ref, send_sem=send_sem, recv_sem=recv_sem,
                                          device_id=(x, y, 1 - c), device_id_type=MESH)
        cp.start()
        cp.wait()

    return pl.pallas_call(
        body, name=name, in_specs=[ANY], out_specs=ANY, out_shape=jax.ShapeDtypeStruct(a.shape, a.dtype),
        scratch_shapes=[pltpu.SemaphoreType.DMA, pltpu.SemaphoreType.DMA],
    )(a)


def _chip_all_to_all(s, *, name):
    def body(s_ref, out_ref, send_sems, recv_sems, local_sem):
        x, y, c = _pos()
        j = 2 * x + y
        chips = _other_chips(x, y)
        mine = pltpu.make_async_copy(s_ref.at[j], out_ref.at[j], local_sem)
        mine.start()
        sends = [_remote(s_ref.at[2 * px + py], out_ref.at[j], send_sems, recv_sems, k, (px, py, c))
                 for k, (px, py) in enumerate(chips)]
        for cp in sends:
            cp.start()
        for k, (px, py) in enumerate(chips):
            _remote(s_ref.at[2 * px + py], out_ref.at[2 * px + py], send_sems, recv_sems, k, (x, y, c)).wait_recv()
        for cp in sends:
            cp.wait_send()
        mine.wait()

    return pl.pallas_call(
        body, name=name, in_specs=[ANY], out_specs=ANY, out_shape=jax.ShapeDtypeStruct(s.shape, s.dtype),
        scratch_shapes=[pltpu.SemaphoreType.DMA((3,)), pltpu.SemaphoreType.DMA((3,)), pltpu.SemaphoreType.DMA],
    )(s)


def _join_halves(r, *, name):
    def body(r_ref, out_ref, send_sem, recv_sem, local_sem):
        x, y, c = _pos()
        mine = pltpu.make_async_copy(r_ref, out_ref.at[c], local_sem)
        mine.start()
        cp = pltpu.make_async_remote_copy(src_ref=r_ref, dst_ref=out_ref.at[c], send_sem=send_sem, recv_sem=recv_sem,
                                          device_id=(x, y, 1 - c), device_id_type=MESH)
        cp.start()
        pltpu.make_async_remote_copy(src_ref=r_ref, dst_ref=out_ref.at[1 - c], send_sem=send_sem, recv_sem=recv_sem,
                                     device_id=(x, y, c), device_id_type=MESH).wait_recv()
        cp.wait_send()
        mine.wait()

    return pl.pallas_call(
        body, name=name, in_specs=[ANY], out_specs=ANY, out_shape=jax.ShapeDtypeStruct((2,) + r.shape, r.dtype),
        scratch_shapes=[pltpu.SemaphoreType.DMA, pltpu.SemaphoreType.DMA, pltpu.SemaphoreType.DMA],
    )(r)


def _add_pair(a, b, *, name, tr=448):
    n, H, L = a.shape

    def body(a_ref, b_ref, o_ref):
        o_ref[...] = (a_ref[...].astype(F32) + b_ref[...].astype(F32)).astype(BF16)

    blk = pl.BlockSpec((1, tr, L), lambda j, i: (j, i, 0))
    return pl.pallas_call(
        body, name=name, grid=(n, H // tr), in_specs=[blk, blk], out_specs=blk,
        out_shape=jax.ShapeDtypeStruct(a.shape, BF16), compiler_params=_cp("parallel", "parallel"),
    )(a, b)


def _sum_chips(b, *, name, tr=448):
    n, H, L = b.shape

    def body(b_ref, o_ref):
        acc = b_ref[0].astype(F32)
        for j in range(1, n):
            acc = acc + b_ref[j].astype(F32)
        o_ref[...] = acc

    return pl.pallas_call(
        body, name=name, grid=(H // tr,), in_specs=[pl.BlockSpec((n, tr, L), lambda i: (0, i, 0))],
        out_specs=pl.BlockSpec((tr, L), lambda i: (i, 0)),
        out_shape=jax.ShapeDtypeStruct((H, L), F32), compiler_params=_cp("parallel"),
    )(b)


PACK_L = 1024
BIG = ("w_in", "w_out", "w_ffn_in", "w_down")
BIG_SHARD = {"w_in": (DEPTH, D, 706), "w_out": (DEPTH, 256, D), "w_ffn_in": (DEPTH, D, 1408), "w_down": (DEPTH, 704, D)}
BIG_AXIS = {"w_in": 2, "w_out": 1, "w_ffn_in": 2, "w_down": 1}
PACK_H = 3136


def _size(shape):
    n = 1
    for s in shape:
        n *= s
    return n


def _pack_flat(parts, total, dtype):
    flat = jnp.concatenate([p.reshape(-1).astype(dtype) for p in parts])
    return jnp.pad(flat, (0, total - flat.shape[0]))


def _unpack_flat(flat, shapes):
    out, o = [], 0
    for s in shapes:
        out.append(flat[o:o + _size(s)].reshape(s))
        o += _size(s)
    return out


def _shard_of(a, name, j):
    ax = BIG_AXIS[name]
    w = BIG_SHARD[name][ax]
    return lax.slice_in_dim(a, j * w, (j + 1) * w, axis=ax)


WEIGHTS = ("attn_norm", "w_in", "attn_sinks", "gdn_conv_w", "gdn_a_log", "gdn_dt_bias", "gdn_norm", "w_out", "ffn_norm",
           "w_ffn_in", "ffn_conv_w", "ffn_conv_b", "w_down", "final_norm")
SMALL = {"attn_norm": (DEPTH, D), "attn_sinks": (DEPTH, HQ), "gdn_a_log": (DEPTH, GH), "gdn_dt_bias": (DEPTH, GH),
         "gdn_norm": (DEPTH, GD), "ffn_norm": (DEPTH, D), "ffn_conv_b": (DEPTH, DFF), "final_norm": (D,)}
CONV_FULL = {"gdn_conv_w": (DEPTH, GK, 3 * GH * GD), "ffn_conv_w": (DEPTH, FK, DFF)}
CONV_SHARD = {"gdn_conv_w": (DEPTH, GK, 3 * GH * GD // N_CHIP), "ffn_conv_w": (DEPTH, FK, DFF // N_CHIP)}
CONV_ROWS, SMALLG_ROWS, SMALLW_ROWS = 64, 320, 144


def kernel(x, attn_norm, w_in, attn_sinks, gdn_conv_w, gdn_a_log, gdn_dt_bias, gdn_norm, w_out, ffn_norm, w_ffn_in, ffn_conv_w, ffn_conv_b, w_down, final_norm, loss_target, m_attn_norm, m_w_in, m_attn_sinks, m_gdn_conv_w, m_gdn_a_log, m_gdn_dt_bias, m_gdn_norm, m_w_out, m_ffn_norm, m_w_ffn_in, m_ffn_conv_w, m_ffn_conv_b, m_w_down, m_final_norm, v_attn_norm, v_w_in, v_attn_sinks, v_gdn_conv_w, v_gdn_a_log, v_gdn_dt_bias, v_gdn_norm, v_w_out, v_ffn_norm, v_w_ffn_in, v_ffn_conv_w, v_ffn_conv_b, v_w_down, v_final_norm):
    w = dict(zip(WEIGHTS, (attn_norm, w_in, attn_sinks, gdn_conv_w, gdn_a_log, gdn_dt_bias, gdn_norm, w_out, ffn_norm,
                           w_ffn_in, ffn_conv_w, ffn_conv_b, w_down, final_norm)))
    m = dict(zip(WEIGHTS, (m_attn_norm, m_w_in, m_attn_sinks, m_gdn_conv_w, m_gdn_a_log, m_gdn_dt_bias, m_gdn_norm, m_w_out,
                           m_ffn_norm, m_w_ffn_in, m_ffn_conv_w, m_ffn_conv_b, m_w_down, m_final_norm)))
    v = dict(zip(WEIGHTS, (v_attn_norm, v_w_in, v_attn_sinks, v_gdn_conv_w, v_gdn_a_log, v_gdn_dt_bias, v_gdn_norm, v_w_out,
                           v_ffn_norm, v_w_ffn_in, v_ffn_conv_w, v_ffn_conv_b, v_w_down, v_final_norm)))
    cx, cy, cc = _pos()
    chip = 2 * cx + cy
    slab = 2 * PACK_H * PACK_L

    wpack = _pack_flat([w[n] for n in BIG], slab, BF16).reshape(2, PACK_H, PACK_L)
    gathered = _ag_big(wpack, name="gather_weights").reshape(N_CHIP, slab)
    shards = [_unpack_flat(gathered[j], [BIG_SHARD[n] for n in BIG]) for j in range(N_CHIP)]
    full = {n: jnp.concatenate([shards[j][i] for j in range(N_CHIP)], axis=BIG_AXIS[n]) for i, n in enumerate(BIG)}
    cpack = _pack_flat([w[n] for n in CONV_SHARD], CONV_ROWS * 128, F32).reshape(CONV_ROWS, 128)
    cgath, _ = _ag_small(cpack, name="gather_conv_w")
    cgath = cgath.reshape(N_DEV, CONV_ROWS * 128)
    cshards = [_unpack_flat(cgath[2 * j], list(CONV_SHARD.values())) for j in range(N_CHIP)]
    conv = {n: jnp.concatenate([cshards[j][i] for j in range(N_CHIP)], axis=2) for i, n in enumerate(CONV_SHARD)}

    Ws = [_layer_weights(full["w_in"][l], full["w_out"][l], full["w_ffn_in"][l], full["w_down"][l]) for l in range(DEPTH)]
    Ps = [_layer_params(attn_norm[l], attn_sinks[l], conv["gdn_conv_w"][l], gdn_a_log[l], gdn_dt_bias[l], gdn_norm[l],
                        ffn_norm[l], conv["ffn_conv_w"][l], ffn_conv_b[l]) for l in range(DEPTH)]

    h, saved = x[0], []
    for l in range(DEPTH):
        h, sv = _layer_fwd(h, Ws[l], Ps[l], l)
        saved.append(sv)
    dx, g_final, loss_part = _loss_head(h, final_norm[None], loss_target[0], name="loss_head")
    lg = [None] * DEPTH
    for l in reversed(range(DEPTH)):
        dx, lg[l] = _layer_bwd(dx, saved[l], Ws[l], Ps[l], l)
    grad_x = dx[None]
    stacked = lambda n: jnp.stack([lg[l][n] for l in range(DEPTH)])

    small_parts = [g_final[0] if n == "final_norm" else stacked(n) for n in SMALL] + [stacked(n) for n in CONV_FULL]
    svec = _pack_flat(small_parts + [loss_part[0, :1]], SMALLG_ROWS * 128, F32).reshape(SMALLG_ROWS, 128)
    _, sred = _ag_small(svec, name="reduce_small")
    small_g = _unpack_flat(sred.reshape(-1), list(SMALL.values()) + list(CONV_FULL.values()) + [(1,)])
    g = dict(zip(list(SMALL) + list(CONV_FULL), small_g[:-1]))
    loss = small_g[-1][0]
    for n in CONV_FULL:
        wd = CONV_SHARD[n][2]
        g[n] = lax.dynamic_slice_in_dim(g[n], chip * wd, wd, axis=2)

    gfull = {n: stacked(n) for n in BIG}
    gp = jnp.stack([_pack_flat([_shard_of(gfull[n], n, j) for n in BIG], slab, BF16).reshape(2 * PACK_H, PACK_L)
                    for j in range(N_CHIP)])
    gp_mine = lax.dynamic_slice_in_dim(gp, cc * PACK_H, PACK_H, axis=1)
    gp_other = lax.dynamic_slice_in_dim(gp, (1 - cc) * PACK_H, PACK_H, axis=1)
    pair = _add_pair(gp_mine, _swap_with_sibling(gp_other, name="rs_sibling_halves"), name="rs_add_sibling")
    summed = _sum_chips(_chip_all_to_all(pair, name="rs_chip_exchange"), name="rs_sum_chips")
    reduced = _join_halves(summed, name="rs_join_halves").reshape(slab)
    g.update(zip(BIG, _unpack_flat(reduced, [BIG_SHARD[n] for n in BIG])))

    delta, new_m, new_v = {}, {}, {}
    for n in BIG:
        delta[n], new_m[n], new_v[n] = _adamw(w[n], g[n], m[n], v[n], name=f"adamw_{n}")
    small_names = list(SMALL) + list(CONV_SHARD)
    small_shapes = list(SMALL.values()) + list(CONV_SHARD.values())
    packed = [_pack_flat([t[n] for n in small_names], SMALLW_ROWS * 128, F32).reshape(SMALLW_ROWS, 128) for t in (w, g, m, v)]
    for res, out in zip(_adamw(*packed, name="adamw_small"), (delta, new_m, new_v)):
        out.update(zip(small_names, _unpack_flat(res.reshape(-1), small_shapes)))

    return (loss, grad_x, *[g[n] for n in WEIGHTS], *[delta[n] for n in WEIGHTS], *[new_m[n] for n in WEIGHTS],
            *[new_v[n] for n in WEIGHTS])
```

```python
import functools

import jax
import jax.numpy as jnp
from jax import lax
from jax.experimental import pallas as pl
from jax.experimental.pallas import tpu as pltpu

F32, BF16 = jnp.float32, jnp.bfloat16
HIGHEST = lax.Precision.HIGHEST
MESH = pl.DeviceIdType.MESH

D = 1024
T = 4096
DEPTH = 2
HQ, HKV, DH, WIN = 8, 2, 64, 128
GH, GD, GC, GK = 4, 128, 64, 4
DFF, FK = 2816, 3
EPS = 1e-6
N_MAIN = 2816
N_PROJ = N_MAIN + 128
COL_K, COL_V, COL_G, COL_Z, COL_BA = 4, 5, 6, 18, 22
N_CHUNK = T // GC
N_DEV, N_CHIP = 8, 4
FFN_CW = 2 * DFF // N_CHIP
LR, B1, B2, AEPS, WD, STEP = 0.001, 0.9, 0.999, 1e-08, 0.01, 10

VMEM_LIMIT = 56 * 1024 * 1024
ANY = pl.BlockSpec(memory_space=pl.ANY)
VMEM_SPEC = pl.BlockSpec(memory_space=pltpu.VMEM)


def _cp(*sem):
    return pltpu.CompilerParams(dimension_semantics=sem if sem else None, vmem_limit_bytes=VMEM_LIMIT)


def _dot(a, b):
    return jnp.dot(a.astype(BF16), b.astype(BF16), preferred_element_type=F32)


def _dot_nt(a, b):
    return lax.dot_general(a.astype(BF16), b.astype(BF16), (((1,), (1,)), ((), ())), preferred_element_type=F32)


def _dot_tn(a, b):
    return lax.dot_general(a.astype(BF16), b.astype(BF16), (((0,), (0,)), ((), ())), preferred_element_type=F32)


def _iota(shape, dim):
    return lax.broadcasted_iota(jnp.int32, shape, dim)


def _col(x, idx):
    return jnp.sum(jnp.where(_iota(x.shape, 1) == idx, x, 0.0), axis=1, keepdims=True)


def _silu(y):
    return y * jax.nn.sigmoid(y)


_MM_DN = {"nn": (((1,), (0,)), ((), ())), "nt": (((1,), (1,)), ((), ())), "tn": (((0,), (0,)), ((), ()))}


def _mm(a, b, *, name, mode, dims, tm, tn, tk, res=None, out_dtype=F32, a_spec=None, b_spec=None, out_spec=None,
        out_shape=None):
    M, N, K = dims
    assert M % tm == 0 and N % tn == 0 and K % tk == 0, (name, dims)
    nk = K // tk
    dn = _MM_DN[mode]

    def body(*refs):
        if res is None:
            a_ref, b_ref, o_ref, acc = refs
        else:
            a_ref, b_ref, r_ref, o_ref, acc = refs
        k = pl.program_id(2)

        @pl.when(k == 0)
        def _():
            acc[...] = jnp.zeros_like(acc)

        acc[...] += lax.dot_general(a_ref[...].astype(BF16), b_ref[...].astype(BF16), dn, preferred_element_type=F32)

        @pl.when(k == nk - 1)
        def _():
            r = acc[...]
            if res is not None:
                r = r + r_ref[...]
            o_ref[...] = r.astype(out_dtype)

    if a_spec is None:
        a_spec = pl.BlockSpec((tk, tm), lambda i, j, k: (k, i)) if mode == "tn" else pl.BlockSpec((tm, tk), lambda i, j, k: (i, k))
    if b_spec is None:
        b_spec = pl.BlockSpec((tn, tk), lambda i, j, k: (j, k)) if mode == "nt" else pl.BlockSpec((tk, tn), lambda i, j, k: (k, j))
    in_specs, args = [a_spec, b_spec], [a, b]
    if res is not None:
        in_specs.append(pl.BlockSpec((tm, tn), lambda i, j, k: (i, j)))
        args.append(res)
    return pl.pallas_call(
        body, name=name, grid=(M // tm, N // tn, nk), in_specs=in_specs,
        out_specs=out_spec or pl.BlockSpec((tm, tn), lambda i, j, k: (i, j)),
        out_shape=out_shape or jax.ShapeDtypeStruct((M, N), out_dtype),
        scratch_shapes=[pltpu.VMEM((tm, tn), F32)],
        compiler_params=_cp("parallel", "parallel", "arbitrary"),
    )(*args)


def _rms(x, w):
    return x * lax.rsqrt(jnp.mean(x * x, axis=-1, keepdims=True) + EPS) * w


def _rmsnorm_fwd(x, w, *, name, tm=512):
    def body(x_ref, w_ref, o_ref):
        o_ref[...] = _rms(x_ref[...], w_ref[...]).astype(BF16)

    return pl.pallas_call(
        body, name=name, grid=(x.shape[0] // tm,),
        in_specs=[pl.BlockSpec((tm, D), lambda i: (i, 0)), pl.BlockSpec((1, D), lambda i: (0, 0))],
        out_specs=pl.BlockSpec((tm, D), lambda i: (i, 0)),
        out_shape=jax.ShapeDtypeStruct(x.shape, BF16), compiler_params=_cp("parallel"),
    )(x, w)


def _rmsnorm_bwd(x, w, dh, dres, *, name, tm=512):
    def body(x_ref, w_ref, dh_ref, dr_ref, dx_ref, dw_ref):
        _, vjp = jax.vjp(_rms, x_ref[...], w_ref[...])
        dx, dw = vjp(dh_ref[...])
        dx_ref[...] = dx + dr_ref[...]

        @pl.when(pl.program_id(0) == 0)
        def _():
            dw_ref[...] = jnp.zeros_like(dw_ref)

        dw_ref[...] += dw

    row = pl.BlockSpec((tm, D), lambda i: (i, 0))
    vec = pl.BlockSpec((1, D), lambda i: (0, 0))
    return pl.pallas_call(
        body, name=name, grid=(x.shape[0] // tm,), in_specs=[row, vec, row, row], out_specs=[row, vec],
        out_shape=[jax.ShapeDtypeStruct(x.shape, F32), jax.ShapeDtypeStruct((1, D), F32)],
        compiler_params=_cp("arbitrary"),
    )(x, w, dh, dres)


def _loss_head(x, w, tgt, *, name, tm=512):
    def f(xv, wv, tv):
        err = _rms(xv, wv) - tv
        per_row = jnp.sum(err * err, axis=1, keepdims=True) * (0.5 / D)
        return jnp.sum(per_row, axis=0, keepdims=True)

    def body(x_ref, w_ref, t_ref, dx_ref, dw_ref, loss_ref):
        tv = t_ref[...]
        loss, vjp = jax.vjp(lambda xv, wv: f(xv, wv, tv), x_ref[...], w_ref[...])
        dx, dw = vjp(jnp.ones((1, 1), F32))
        dx_ref[...] = dx

        @pl.when(pl.program_id(0) == 0)
        def _():
            dw_ref[...] = jnp.zeros_like(dw_ref)
            loss_ref[...] = jnp.zeros_like(loss_ref)

        dw_ref[...] += dw
        loss_ref[...] += jnp.broadcast_to(loss, loss_ref.shape)

    row = pl.BlockSpec((tm, D), lambda i: (i, 0))
    vec = pl.BlockSpec((1, D), lambda i: (0, 0))
    return pl.pallas_call(
        body, name=name, grid=(x.shape[0] // tm,), in_specs=[row, vec, row],
        out_specs=[row, vec, pl.BlockSpec((1, 128), lambda i: (0, 0))],
        out_shape=[jax.ShapeDtypeStruct(x.shape, F32), jax.ShapeDtypeStruct((1, D), F32),
                   jax.ShapeDtypeStruct((1, 128), F32)],
        compiler_params=_cp("arbitrary"),
    )(x, w, tgt)


def _swa_group(qg, k2, v2, sk, blk, h2):
    s = _dot_nt(qg, k2) * (DH ** -0.5)
    r, c = _iota(s.shape, 0), _iota(s.shape, 1)
    g = r >> 7
    rel = (r & (WIN - 1)) + WIN - c
    valid = (rel >= 0) & (rel < WIN) & ((blk > 0) | (c >= WIN))
    slope = jnp.zeros(s.shape, F32)
    g1 = _iota((s.shape[0], 1), 0) >> 7
    sink = jnp.zeros((s.shape[0], 1), F32)
    for gi in range(HQ // HKV):
        hq = h2 * (HQ // HKV) + gi
        slope = jnp.where(g == gi, 2.0 ** (-8.0 * (hq + 1) / HQ), slope)
        sink = jnp.where(g1 == gi, _col(sk, hq), sink)
    s = jnp.where(valid, s - slope * rel.astype(F32), -1e30)
    m = lax.stop_gradient(jnp.maximum(jnp.max(s, axis=1, keepdims=True), sink))
    p = jnp.exp(s - m)
    den = jnp.sum(p, axis=1, keepdims=True) + jnp.exp(sink - m)
    return _dot(p / den, v2)


def _swa_split_q(q, h2):
    G = HQ // HKV
    return jnp.concatenate([q[:, (h2 * G + g) * DH:(h2 * G + g + 1) * DH] for g in range(G)], axis=0)


def _swa_merge_q(parts):
    G = HQ // HKV
    return jnp.concatenate([parts[h2][g * WIN:(g + 1) * WIN] for h2 in range(HKV) for g in range(G)], axis=1)


def _swa_specs():
    prev = lambda i: jnp.maximum(jnp.minimum(i, T // WIN - 1) - 1, 0)
    cur = lambda i: jnp.minimum(i, T // WIN - 1)
    return [
        pl.BlockSpec((WIN, HQ * DH), lambda i: (cur(i), 0)),
        pl.BlockSpec((WIN, 128), lambda i: (prev(i), COL_K)),
        pl.BlockSpec((WIN, 128), lambda i: (cur(i), COL_K)),
        pl.BlockSpec((WIN, 128), lambda i: (prev(i), COL_V)),
        pl.BlockSpec((WIN, 128), lambda i: (cur(i), COL_V)),
        pl.BlockSpec((1, 128), lambda i: (0, 0)),
    ]


def _swa_fwd(proj, sinks, *, name):
    def body(q_ref, kp_ref, kc_ref, vp_ref, vc_ref, sk_ref, o_ref):
        i = pl.program_id(0)
        q, sk = q_ref[...], sk_ref[...]
        outs = []
        for h2 in range(HKV):
            sl = slice(h2 * DH, (h2 + 1) * DH)
            k2 = jnp.concatenate([kp_ref[:, sl], kc_ref[:, sl]], axis=0)
            v2 = jnp.concatenate([vp_ref[:, sl], vc_ref[:, sl]], axis=0)
            outs.append(_swa_group(_swa_split_q(q, h2), k2, v2, sk, i, h2))
        o_ref[...] = _swa_merge_q(outs).astype(BF16)

    return pl.pallas_call(
        body, name=name, grid=(T // WIN,), in_specs=_swa_specs(),
        out_specs=pl.BlockSpec((WIN, HQ * DH), lambda i: (i, 0)),
        out_shape=jax.ShapeDtypeStruct((T, HQ * DH + GH * GD), BF16), compiler_params=_cp("parallel"),
    )(proj, proj, proj, proj, proj, sinks)


def _swa_bwd(proj, sinks, do, dproj, *, name):
    NB = T // WIN
    QW = HQ * DH

    def body(q_ref, kp_ref, kc_ref, vp_ref, vc_ref, sk_ref, do_ref, _, out_ref, dsk_ref, cq, ck, cv):
        i = pl.program_id(0)

        @pl.when(i == 0)
        def _():
            cq[...] = jnp.zeros_like(cq)
            ck[...] = jnp.zeros_like(ck)
            cv[...] = jnp.zeros_like(cv)
            dsk_ref[...] = jnp.zeros_like(dsk_ref)

        @pl.when(i < NB)
        def _():
            q, sk, dov = q_ref[...], sk_ref[...], do_ref[...].astype(F32)
            dqs, dkp, dkc, dvp, dvc = [], [], [], [], []
            dsk = jnp.zeros_like(sk)
            for h2 in range(HKV):
                sl = slice(h2 * DH, (h2 + 1) * DH)
                k2 = jnp.concatenate([kp_ref[:, sl], kc_ref[:, sl]], axis=0)
                v2 = jnp.concatenate([vp_ref[:, sl], vc_ref[:, sl]], axis=0)
                _, vjp = jax.vjp(functools.partial(_swa_group, blk=i, h2=h2), _swa_split_q(q, h2), k2, v2, sk)
                dqg, dk2, dv2, dsk_h = vjp(_swa_split_q(dov, h2))
                dqs.append(dqg)
                dkp.append(dk2[:WIN]); dkc.append(dk2[WIN:])
                dvp.append(dv2[:WIN]); dvc.append(dv2[WIN:])
                dsk = dsk + dsk_h
            out_ref[:, :QW] = cq[...].astype(BF16)
            out_ref[:, QW:QW + 128] = (ck[...] + jnp.concatenate(dkp, axis=1)).astype(BF16)
            out_ref[:, QW + 128:] = (cv[...] + jnp.concatenate(dvp, axis=1)).astype(BF16)
            cq[...] = _swa_merge_q(dqs)
            ck[...] = jnp.concatenate(dkc, axis=1)
            cv[...] = jnp.concatenate(dvc, axis=1)
            dsk_ref[...] += dsk

        @pl.when(i == NB)
        def _():
            out_ref[:, :QW] = cq[...].astype(BF16)
            out_ref[:, QW:QW + 128] = ck[...].astype(BF16)
            out_ref[:, QW + 128:] = cv[...].astype(BF16)

    qblk = pl.BlockSpec((WIN, QW), lambda i: (jnp.minimum(i, NB - 1), 0))
    return pl.pallas_call(
        body, name=name, grid=(NB + 1,), in_specs=_swa_specs() + [qblk, ANY],
        out_specs=[pl.BlockSpec((WIN, QW + 256), lambda i: (jnp.maximum(i - 1, 0), 0)), pl.BlockSpec((1, 128), lambda i: (0, 0))],
        out_shape=[jax.ShapeDtypeStruct(dproj.shape, dproj.dtype), jax.ShapeDtypeStruct((1, 128), F32)],
        scratch_shapes=[pltpu.VMEM((WIN, QW), F32), pltpu.VMEM((WIN, 128), F32), pltpu.VMEM((WIN, 128), F32)],
        input_output_aliases={7: 0}, compiler_params=_cp("arbitrary"),
    )(proj, proj, proj, proj, proj, sinks, do, dproj)


RC = 256
HALO = 8


def _load_ext(ref, c):
    nch = T // RC
    r0 = pl.multiple_of(c * RC, RC)
    p0 = pl.multiple_of(jnp.maximum(r0 - HALO, 0), HALO)
    n0 = pl.multiple_of(jnp.minimum(r0 + RC, T - HALO), HALO)
    prev = jnp.where(c > 0, ref[pl.ds(p0, HALO), :].astype(F32), 0.0)
    nxt = jnp.where(c < nch - 1, ref[pl.ds(n0, HALO), :].astype(F32), 0.0)
    return jnp.concatenate([prev, ref[pl.ds(r0, RC), :].astype(F32), nxt], axis=0)


def _conv_ext(xe, w, K):
    y = w[K - 1:K, :] * xe
    for s in range(1, K):
        y = y + w[K - 1 - s:K - s, :] * pltpu.roll(xe, s, 0)
    return y


def _conv_bwd_ext(xe, dye, w, K, dw_ref):
    n = xe.shape[0]
    own = slice(HALO, HALO + RC)
    dx = w[K - 1:K, :] * dye
    dw_ref[K - 1:K, :] += jnp.sum(dye[own] * xe[own], axis=0, keepdims=True)
    for s in range(1, K):
        dx = dx + w[K - 1 - s:K - s, :] * pltpu.roll(dye, n - s, 0)
        dw_ref[K - 1 - s:K - s, :] += jnp.sum(dye[own] * pltpu.roll(xe, s, 0)[own], axis=0, keepdims=True)
    return dx[own]


def _gdn_post_conv(y, is_qk):
    a = _silu(y)
    nrm = a * lax.rsqrt(jnp.sum(a * a, axis=1, keepdims=True) + EPS)
    return jnp.where(is_qk, nrm, a)


def _gdn_prep_fwd(proj, conv_w, *, name):
    nblk = 3 * GH

    def body(x_ref, w_ref, o_ref):
        is_qk = pl.program_id(0) < 2 * GH
        w = w_ref[...]

        def chunk(c, carry):
            y = _conv_ext(_load_ext(x_ref, c), w, GK)[HALO:HALO + RC]
            o_ref[pl.ds(pl.multiple_of(c * RC, RC), RC), :] = _gdn_post_conv(y, is_qk)
            return carry

        lax.fori_loop(0, T // RC, chunk, 0)

    return pl.pallas_call(
        body, name=name, grid=(nblk,),
        in_specs=[pl.BlockSpec((T, 128), lambda j: (0, COL_G + j)), pl.BlockSpec((GK, 128), lambda j: (0, j))],
        out_specs=pl.BlockSpec((T, 128), lambda j: (0, j)),
        out_shape=jax.ShapeDtypeStruct((T, nblk * 128), F32), compiler_params=_cp("parallel"),
    )(proj, conv_w)


def _gdn_prep_bwd(proj, conv_w, dout, dproj, *, name):
    nblk = 3 * GH

    def body(x_ref, w_ref, d_ref, _, dx_ref, dw_ref):
        is_qk = pl.program_id(0) < 2 * GH
        w = w_ref[...]
        dw_ref[...] = jnp.zeros_like(dw_ref)

        def chunk(c, carry):
            xe = _load_ext(x_ref, c)
            _, vjp = jax.vjp(lambda y: _gdn_post_conv(y, is_qk), _conv_ext(xe, w, GK))
            (dye,) = vjp(_load_ext(d_ref, c))
            dx_ref[pl.ds(pl.multiple_of(c * RC, RC), RC), :] = _conv_bwd_ext(xe, dye, w, GK, dw_ref).astype(BF16)
            return carry

        lax.fori_loop(0, T // RC, chunk, 0)

    return pl.pallas_call(
        body, name=name, grid=(nblk,),
        in_specs=[pl.BlockSpec((T, 128), lambda j: (0, COL_G + j)), pl.BlockSpec((GK, 128), lambda j: (0, j)),
                  pl.BlockSpec((T, 128), lambda j: (0, j)), ANY],
        out_specs=[pl.BlockSpec((T, 128), lambda j: (0, COL_G + j)), pl.BlockSpec((GK, 128), lambda j: (0, j))],
        out_shape=[jax.ShapeDtypeStruct(dproj.shape, dproj.dtype), jax.ShapeDtypeStruct((GK, nblk * 128), F32)],
        input_output_aliases={3: 0}, compiler_params=_cp("parallel"),
    )(proj, conv_w, dout, dproj)


def _ffn_post_conv(y, b, up):
    return _silu(y + b) * up


def _ffn_act_fwd(gu, conv_w, conv_b, *, name):
    nblk = DFF // 128

    def body(g_ref, u_ref, w_ref, b_ref, o_ref):
        w, b = w_ref[...], b_ref[...]

        def chunk(c, carry):
            rows = pl.ds(pl.multiple_of(c * RC, RC), RC)
            y = _conv_ext(_load_ext(g_ref, c), w, FK)[HALO:HALO + RC]
            o_ref[rows, :] = _ffn_post_conv(y, b, u_ref[rows, :]).astype(BF16)
            return carry

        lax.fori_loop(0, T // RC, chunk, 0)

    return pl.pallas_call(
        body, name=name, grid=(nblk,),
        in_specs=[pl.BlockSpec((T, 128), lambda j: (0, j)), pl.BlockSpec((T, 128), lambda j: (0, nblk + j)),
                  pl.BlockSpec((FK, 128), lambda j: (0, j)), pl.BlockSpec((1, 128), lambda j: (0, j))],
        out_specs=pl.BlockSpec((T, 128), lambda j: (0, j)),
        out_shape=jax.ShapeDtypeStruct((T, DFF), BF16), compiler_params=_cp("parallel"),
    )(gu, gu, conv_w, conv_b)


def _ffn_act_bwd(gu, conv_w, conv_b, dact, *, name):
    nblk = DFF // 128

    def body(g_ref, u_ref, w_ref, b_ref, d_ref, dgu_ref, dw_ref, db_ref):
        dg_ref, du_ref = dgu_ref.at[0], dgu_ref.at[1]
        w, b = w_ref[...], b_ref[...]
        dw_ref[...] = jnp.zeros_like(dw_ref)
        db_ref[...] = jnp.zeros_like(db_ref)

        def chunk(c, carry):
            rows = pl.ds(pl.multiple_of(c * RC, RC), RC)
            xe = _load_ext(g_ref, c)
            ue = _load_ext(u_ref, c)
            _, vjp = jax.vjp(_ffn_post_conv, _conv_ext(xe, w, FK), b, ue)
            dye, db, due = vjp(_load_ext(d_ref, c))
            du_ref[rows, :] = due[HALO:HALO + RC].astype(BF16)
            db_ref[...] += jnp.sum(dye[HALO:HALO + RC], axis=0, keepdims=True)
            dg_ref[rows, :] = _conv_bwd_ext(xe, dye, w, FK, dw_ref).astype(BF16)
            return carry

        lax.fori_loop(0, T // RC, chunk, 0)

    col = pl.BlockSpec((T, 128), lambda j: (0, j))
    return pl.pallas_call(
        body, name=name, grid=(nblk,),
        in_specs=[col, pl.BlockSpec((T, 128), lambda j: (0, nblk + j)), pl.BlockSpec((FK, 128), lambda j: (0, j)),
                  pl.BlockSpec((1, 128), lambda j: (0, j)), col],
        out_specs=[pl.BlockSpec((2, T, 128), lambda j: (0, 0, j)), pl.BlockSpec((FK, 128), lambda j: (0, j)),
                   pl.BlockSpec((1, 128), lambda j: (0, j))],
        out_shape=[jax.ShapeDtypeStruct((2, T, DFF), BF16), jax.ShapeDtypeStruct((FK, DFF), F32),
                   jax.ShapeDtypeStruct((1, DFF), F32)],
        compiler_params=_cp("parallel"),
    )(gu, gu, conv_w, conv_b, dact)


def _gdn_gates(ba, alog, dtb):
    beta = jax.nn.sigmoid(ba)
    g = -jnp.exp(alog) * jax.nn.softplus(ba + dtb)
    tril = (_iota((GC, GC), 0) >= _iota((GC, GC), 1)).astype(F32)
    return beta, jnp.dot(tril, g, precision=HIGHEST, preferred_element_type=F32)


def _gdn_decay(Gc):
    r, c = _iota((GC, GC), 0), _iota((GC, GC), 1)
    eye = (r == c).astype(F32)
    grow = jnp.dot(jnp.ones((GC, GC), F32), eye * Gc, precision=HIGHEST, preferred_element_type=F32)
    return jnp.exp(jnp.where(r >= c, Gc - grow, -1e30))


def _gdn_A(k, beta, Gc):
    strict = _iota((GC, GC), 0) > _iota((GC, GC), 1)
    return jnp.where(strict, _dot_nt(k * beta, k) * _gdn_decay(Gc), 0.0)


def _tri_inv(A):
    eye = (_iota((GC, GC), 0) == _iota((GC, GC), 1)).astype(F32)
    Tm, P = eye - A, A
    for _ in range(GC.bit_length() - 2):
        P = _dot(P, P)
        Tm = Tm + _dot(Tm, P)
    return Tm


def _gdn_chunk(q, k, v, beta, Gc, S, Tm):
    decay = _gdn_decay(Gc)
    eG = jnp.exp(Gc)
    kb = k * beta
    u = _dot(Tm, v * beta)
    w = _dot(Tm, kb * eG)
    qs = q * (GD ** -0.5)
    qk = _dot_nt(qs, k) * decay
    glast = jnp.sum(jnp.where(_iota(Gc.shape, 0) == GC - 1, Gc, 0.0), axis=0, keepdims=True)
    kd = k * jnp.exp(glast - Gc)
    v_new = u - _dot(w, S)
    o = _dot(qs * eG, S) + _dot(qk, v_new)
    return o, S * jnp.exp(glast) + _dot_tn(kd, v_new)


def _gdn_chunk_fwd(qkv_c, proj, alog, dtb, *, name):
    W3 = 3 * GH * GD

    def body(x_ref, ba_ref, al_ref, dt_ref, o_ref, s_ref, t_ref, S):
        @pl.when(pl.program_id(0) == 0)
        def _():
            S[...] = jnp.zeros_like(S)

        beta_all, gc_all = _gdn_gates(ba_ref[...], al_ref[...], dt_ref[...])
        for h in range(GH):
            q = x_ref[:, h * GD:(h + 1) * GD]
            k = x_ref[:, (GH + h) * GD:(GH + h + 1) * GD]
            v = x_ref[:, (2 * GH + h) * GD:(2 * GH + h + 1) * GD]
            beta, Gc = _col(beta_all, h), _col(gc_all, GH + h)
            Tm = _tri_inv(_gdn_A(k, beta, Gc))
            Sh = S[h]
            s_ref[0, h] = Sh
            t_ref[0, h] = Tm
            o, S_new = _gdn_chunk(q, k, v, beta, Gc, Sh, Tm)
            o_ref[:, h * GD:(h + 1) * GD] = o
            S[h] = S_new

    vec = pl.BlockSpec((1, 128), lambda n: (0, 0))
    return pl.pallas_call(
        body, name=name, grid=(N_CHUNK,),
        in_specs=[pl.BlockSpec((GC, W3), lambda n: (n, 0)), pl.BlockSpec((GC, 128), lambda n: (n, COL_BA)), vec, vec],
        out_specs=[pl.BlockSpec((GC, GH * GD), lambda n: (n, 0)),
                   pl.BlockSpec((1, GH, GD, GD), lambda n: (n, 0, 0, 0)),
                   pl.BlockSpec((1, GH, GC, GC), lambda n: (n, 0, 0, 0))],
        out_shape=[jax.ShapeDtypeStruct((T, GH * GD), F32), jax.ShapeDtypeStruct((N_CHUNK, GH, GD, GD), F32),
                   jax.ShapeDtypeStruct((N_CHUNK, GH, GC, GC), F32)],
        scratch_shapes=[pltpu.VMEM((GH, GD, GD), F32)], compiler_params=_cp("arbitrary"),
    )(qkv_c, proj, alog, dtb)


def _gdn_chunk_bwd(qkv_c, proj, alog, dtb, s_all, t_all, do, dproj, *, name):
    W3 = 3 * GH * GD
    rev = lambda n: N_CHUNK - 1 - n

    def body(x_ref, ba_ref, al_ref, dt_ref, s_ref, t_ref, do_ref, _, dx_ref, dba_ref, dal_ref, ddt_ref, dS):
        @pl.when(pl.program_id(0) == 0)
        def _():
            dS[...] = jnp.zeros_like(dS)
            dal_ref[...] = jnp.zeros_like(dal_ref)
            ddt_ref[...] = jnp.zeros_like(ddt_ref)

        (beta_all, gc_all), vjp_gates = jax.vjp(_gdn_gates, ba_ref[...], al_ref[...], dt_ref[...])
        dbeta_all, dgc_all = jnp.zeros_like(beta_all), jnp.zeros_like(gc_all)
        for h in range(GH):
            q = x_ref[:, h * GD:(h + 1) * GD]
            k = x_ref[:, (GH + h) * GD:(GH + h + 1) * GD]
            v = x_ref[:, (2 * GH + h) * GD:(2 * GH + h + 1) * GD]
            Tm = t_ref[0, h]

            def f(q, k, v, b_all, g_all, Sh, Tm, h=h):
                return _gdn_chunk(q, k, v, _col(b_all, h), _col(g_all, GH + h), Sh, Tm)

            def fa(k, b_all, g_all, h=h):
                return _gdn_A(k, _col(b_all, h), _col(g_all, GH + h))

            _, vjp = jax.vjp(f, q, k, v, beta_all, gc_all, s_ref[0, h], Tm)
            dq, dk, dv, db1, dg1, dS_prev, dT = vjp((do_ref[:, h * GD:(h + 1) * GD], dS[h]))
            dA = -_dot_tn(Tm, _dot_nt(dT, Tm))
            _, vjp_a = jax.vjp(fa, k, beta_all, gc_all)
            dk2, db2, dg2 = vjp_a(dA)
            dS[h] = dS_prev
            dx_ref[:, h * GD:(h + 1) * GD] = dq
            dx_ref[:, (GH + h) * GD:(GH + h + 1) * GD] = dk + dk2
            dx_ref[:, (2 * GH + h) * GD:(2 * GH + h + 1) * GD] = dv
            dbeta_all = dbeta_all + db1 + db2
            dgc_all = dgc_all + dg1 + dg2
        dba, dal, ddt = vjp_gates((dbeta_all, dgc_all))
        dba_ref[...] = dba.astype(BF16)
        dal_ref[...] += dal
        ddt_ref[...] += ddt

    vec = pl.BlockSpec((1, 128), lambda n: (0, 0))
    return pl.pallas_call(
        body, name=name, grid=(N_CHUNK,),
        in_specs=[pl.BlockSpec((GC, W3), lambda n: (rev(n), 0)), pl.BlockSpec((GC, 128), lambda n: (rev(n), COL_BA)),
                  vec, vec, pl.BlockSpec((1, GH, GD, GD), lambda n: (rev(n), 0, 0, 0)),
                  pl.BlockSpec((1, GH, GC, GC), lambda n: (rev(n), 0, 0, 0)),
                  pl.BlockSpec((GC, GH * GD), lambda n: (rev(n), 0)),
                  ANY],
        out_specs=[pl.BlockSpec((GC, W3), lambda n: (rev(n), 0)), pl.BlockSpec((GC, 128), lambda n: (rev(n), COL_BA)), vec, vec],
        out_shape=[jax.ShapeDtypeStruct((T, W3), F32), jax.ShapeDtypeStruct(dproj.shape, dproj.dtype),
                   jax.ShapeDtypeStruct((1, 128), F32), jax.ShapeDtypeStruct((1, 128), F32)],
        scratch_shapes=[pltpu.VMEM((GH, GD, GD), F32)], input_output_aliases={7: 1}, compiler_params=_cp("arbitrary"),
    )(qkv_c, proj, alog, dtb, s_all, t_all, do, dproj)


def _gdn_post(o, z, nw):
    return o * lax.rsqrt(jnp.mean(o * o, axis=-1, keepdims=True) + EPS) * nw * _silu(z)


def _gdn_post_fwd(o_raw, proj, nw, mixed, *, name, tm=512):
    def body(o_ref, z_ref, w_ref, _, out_ref):
        out_ref[...] = _gdn_post(o_ref[...], z_ref[...], w_ref[...]).astype(BF16)

    return pl.pallas_call(
        body, name=name, grid=(T // tm, GH),
        in_specs=[pl.BlockSpec((tm, GD), lambda i, h: (i, h)), pl.BlockSpec((tm, GD), lambda i, h: (i, COL_Z + h)),
                  pl.BlockSpec((1, GD), lambda i, h: (0, 0)), ANY],
        out_specs=pl.BlockSpec((tm, GD), lambda i, h: (i, HQ * DH // GD + h)),
        out_shape=jax.ShapeDtypeStruct(mixed.shape, mixed.dtype), input_output_aliases={3: 0},
        compiler_params=_cp("parallel", "parallel"),
    )(o_raw, proj, nw, mixed)


def _gdn_post_bwd(o_raw, proj, nw, dmixed, *, name, tm=512):
    def body(o_ref, z_ref, w_ref, d_ref, do_ref, dz_ref, dw_ref):
        _, vjp = jax.vjp(_gdn_post, o_ref[...], z_ref[...], w_ref[...])
        do, dz, dw = vjp(d_ref[...])
        do_ref[...] = do
        dz_ref[...] = dz.astype(BF16)

        @pl.when((pl.program_id(0) == 0) & (pl.program_id(1) == 0))
        def _():
            dw_ref[...] = jnp.zeros_like(dw_ref)

        dw_ref[...] += dw

    blk = pl.BlockSpec((tm, GD), lambda i, h: (i, h))
    vec = pl.BlockSpec((1, GD), lambda i, h: (0, 0))
    return pl.pallas_call(
        body, name=name, grid=(T // tm, GH),
        in_specs=[blk, pl.BlockSpec((tm, GD), lambda i, h: (i, COL_Z + h)), vec,
                  pl.BlockSpec((tm, GD), lambda i, h: (i, GH + h))],
        out_specs=[blk, pl.BlockSpec((tm, GD), lambda i, h: (i, COL_Z + h)), vec],
        out_shape=[jax.ShapeDtypeStruct((T, GH * GD), F32), jax.ShapeDtypeStruct((T, N_PROJ), BF16),
                   jax.ShapeDtypeStruct((1, GD), F32)],
        compiler_params=_cp("arbitrary", "arbitrary"),
    )(o_raw, proj, nw, dmixed)


def _adamw(w, g, m, v, *, name):
    shape = w.shape
    cols = shape[-1]
    w2, g2, m2, v2 = (a.reshape(-1, cols) for a in (w, g, m, v))
    rows = w2.shape[0]
    tr = next((t for t in (512, 256, 128, 64, 32, 16, 8) if rows % t == 0), rows)

    def body(w_ref, g_ref, m_ref, v_ref, d_ref, nm_ref, nv_ref):
        gv = g_ref[...]
        nm = B1 * m_ref[...] + (1.0 - B1) * gv
        nv = B2 * v_ref[...] + (1.0 - B2) * jnp.square(gv)
        m_hat = nm / (1.0 - B1 ** STEP)
        v_hat = nv / (1.0 - B2 ** STEP)
        d_ref[...] = -LR * (m_hat / (jnp.sqrt(v_hat) + AEPS) + WD * w_ref[...])
        nm_ref[...] = nm
        nv_ref[...] = nv

    blk = pl.BlockSpec((tr, cols), lambda i: (i, 0))
    out = pl.pallas_call(
        body, name=name, grid=(rows // tr,), in_specs=[blk] * 4, out_specs=[blk] * 3,
        out_shape=[jax.ShapeDtypeStruct((rows, cols), F32)] * 3, compiler_params=_cp("parallel"),
    )(w2, g2, m2, v2)
    return tuple(o.reshape(shape) for o in out)


def _layer_weights(w_in, w_out, w_ffn, w_down):
    w_all = jnp.pad(w_in, ((0, 0), (0, N_PROJ - w_in.shape[1])))
    return dict(all=w_all, out=w_out, ffn=w_ffn, down=w_down)


def _layer_params(attn_norm, sinks, gcw, a_log, dt_bias, gnw, ffn_norm, fcw, fcb):
    lanes4 = lambda v: jnp.pad(v, (GH, 128 - 2 * GH))[None]
    return dict(attn_norm=attn_norm[None], sinks=jnp.pad(sinks, (0, 128 - HQ))[None], gcw=gcw, alog=lanes4(a_log),
                dtb=lanes4(dt_bias), gnw=gnw[None], ffn_norm=ffn_norm[None], fcw=fcw, fcb=fcb[None])


def _layer_fwd(x, W, P, l):
    n = lambda s: f"l{l}_{s}"
    h = _rmsnorm_fwd(x, P["attn_norm"], name=n("norm1"))
    proj = _mm(h, W["all"], name=n("proj"), mode="nn", dims=(T, N_PROJ, D), tm=512, tn=N_PROJ, tk=D)
    mixed = _swa_fwd(proj, P["sinks"], name=n("swa"))
    qkv_c = _gdn_prep_fwd(proj, P["gcw"], name=n("gdn_prep"))
    o_raw, s_all, t_all = _gdn_chunk_fwd(qkv_c, proj, P["alog"], P["dtb"], name=n("gdn_chunk"))
    mixed = _gdn_post_fwd(o_raw, proj, P["gnw"], mixed, name=n("gdn_post"))
    x1 = _mm(mixed, W["out"], res=x, name=n("out_proj"), mode="nn", dims=(T, D, D), tm=512, tn=D, tk=D)
    h2 = _rmsnorm_fwd(x1, P["ffn_norm"], name=n("norm2"))
    gu = _mm(h2, W["ffn"], name=n("ffn_in"), mode="nn", dims=(T, 2 * DFF, D), tm=512, tn=FFN_CW, tk=D,
             b_spec=pl.BlockSpec((None, D, FFN_CW), lambda i, j, k: (j, k, 0)))
    act = _ffn_act_fwd(gu, P["fcw"], P["fcb"], name=n("ffn_act"))
    x2 = _mm(act, W["down"], res=x1, name=n("ffn_down"), mode="nn", dims=(T, D, DFF), tm=512, tn=D, tk=DFF)
    saved = dict(x=x, h=h, proj=proj, qkv_c=qkv_c, o_raw=o_raw, s_all=s_all, t_all=t_all, mixed=mixed, x1=x1, h2=h2,
                 gu=gu, act=act)
    return x2, saved


def _layer_bwd(dx2, sv, W, P, l):
    n = lambda s: f"l{l}_{s}"
    CW = FFN_CW
    dact =_mm(dx2, W["down"], name=n("d_act"), mode="nt", dims=(T, DFF, D), tm=512, tn=DFF // 2, tk=D, out_dtype=BF16)
    g_down = _mm(sv["act"], dx2, name=n("g_down"), mode="tn", dims=(DFF, D, T), tm=DFF // 2, tn=D, tk=1024, out_dtype=BF16)
    dgu, g_fcw, g_fcb = _ffn_act_bwd(sv["gu"], P["fcw"], P["fcb"], dact, name=n("d_ffn_act"))
    dh2 = _mm(dgu, W["ffn"], name=n("d_h2"), mode="nt", dims=(T, D, 2 * DFF), tm=512, tn=D, tk=CW,
              a_spec=pl.BlockSpec((None, 512, CW), lambda i, j, k: (k // 2, i, k % 2)),
              b_spec=pl.BlockSpec((None, D, CW), lambda i, j, k: (k, j, 0)))
    g_ffn = _mm(sv["h2"], dgu, name=n("g_ffn"), mode="tn", dims=(D, 2 * DFF, T), tm=D, tn=CW, tk=1024,
                b_spec=pl.BlockSpec((None, 1024, CW), lambda i, j, k: (j // 2, k, j % 2)),
                out_spec=pl.BlockSpec((None, D, CW), lambda i, j, k: (j, i, 0)),
                out_shape=jax.ShapeDtypeStruct((N_CHIP, D, CW), BF16), out_dtype=BF16)
    dx1, g_ffn_norm = _rmsnorm_bwd(sv["x1"], P["ffn_norm"], dh2, dx2, name=n("d_norm2"))
    dmixed = _mm(dx1, W["out"], name=n("d_mixed"), mode="nt", dims=(T, D, D), tm=512, tn=D, tk=D)
    g_out = _mm(sv["mixed"], dx1, name=n("g_out"), mode="tn", dims=(D, D, T), tm=D, tn=D, tk=1024, out_dtype=BF16)
    do_raw, dproj, g_gnw = _gdn_post_bwd(sv["o_raw"], sv["proj"], P["gnw"], dmixed, name=n("d_gdn_post"))
    dqkv_c, dproj, g_alog, g_dtb = _gdn_chunk_bwd(sv["qkv_c"], sv["proj"], P["alog"], P["dtb"], sv["s_all"], sv["t_all"],
                                                   do_raw, dproj, name=n("d_gdn_chunk"))
    dproj, g_gcw = _gdn_prep_bwd(sv["proj"], P["gcw"], dqkv_c, dproj, name=n("d_gdn_prep"))
    dproj, g_sinks = _swa_bwd(sv["proj"], P["sinks"], dmixed, dproj, name=n("d_swa"))
    dh = _mm(dproj, W["all"], name=n("d_h"), mode="nt", dims=(T, D, N_PROJ), tm=512, tn=D, tk=N_PROJ)
    g_all = _mm(sv["h"], dproj, name=n("g_in"), mode="tn", dims=(D, N_PROJ, T), tm=512, tn=N_PROJ, tk=1024, out_dtype=BF16)
    dx, g_attn_norm = _rmsnorm_bwd(sv["x"], P["attn_norm"], dh, dx1, name=n("d_norm1"))
    grads = dict(
        attn_norm=g_attn_norm[0], w_in=g_all, attn_sinks=g_sinks[0, :HQ], gdn_conv_w=g_gcw,
        gdn_a_log=g_alog[0, GH:2 * GH], gdn_dt_bias=g_dtb[0, GH:2 * GH], gdn_norm=g_gnw[0], w_out=g_out,
        ffn_norm=g_ffn_norm[0], w_ffn_in=g_ffn, ffn_conv_w=g_fcw, ffn_conv_b=g_fcb[0], w_down=g_down)
    return dx, grads


def _pos():
    return lax.axis_index("x"), lax.axis_index("y"), lax.axis_index("c")


def _other_chips(x, y):
    return [(1 - x, y), (x, 1 - y), (1 - x, 1 - y)]


def _remote(src, dst, send_sems, recv_sems, k, to):
    return pltpu.make_async_remote_copy(src_ref=src, dst_ref=dst, send_sem=send_sems.at[k], recv_sem=recv_sems.at[k],
                                        device_id=to, device_id_type=MESH)


def _ag_weights(shards, *, name):
    nt = len(shards)

    def body(*refs):
        srcs, outs = refs[:nt], refs[nt:2 * nt]
        send_sems, recv_sems, local_sems = refs[2 * nt:]
        x, y, c = _pos()
        me, sibling, chip = (x, y, c), (x, y, 1 - c), 2 * x + y
        others = [(k, px, py, 2 * px + py) for k, (px, py) in enumerate(_other_chips(x, y))]
        local = [pltpu.make_async_copy(srcs[t].at[l], outs[t].at[l, chip], local_sems.at[DEPTH * t + l])
                 for t in range(nt) for l in range(DEPTH)]
        for cp in local:
            cp.start()
        first = [_remote(srcs[t].at[c], outs[t].at[c, chip], send_sems, recv_sems, 3 * t + k, (px, py, c))
                 for t in range(nt) for k, px, py, _ in others]
        for cp in first:
            cp.start()
        passed = []
        for t in range(nt):
            for k, _, _, j in others:
                slot = outs[t].at[c, j]
                _remote(slot, slot, send_sems, recv_sems, 3 * t + k, me).wait_recv()
                passed.append(_remote(slot, slot, send_sems, recv_sems, 3 * (nt + t) + k, sibling))
                passed[-1].start()
        for t in range(nt):
            for k, _, _, j in others:
                slot = outs[t].at[1 - c, j]
                _remote(slot, slot, send_sems, recv_sems, 3 * (nt + t) + k, me).wait_recv()
        for cp in first + passed:
            cp.wait_send()
        for cp in local:
            cp.wait()

    return pl.pallas_call(
        body, name=name, in_specs=[ANY] * nt, out_specs=[ANY] * nt,
        out_shape=[jax.ShapeDtypeStruct((DEPTH, N_CHIP) + s.shape[1:], s.dtype) for s in shards],
        scratch_shapes=[pltpu.SemaphoreType.DMA((6 * nt,)), pltpu.SemaphoreType.DMA((6 * nt,)),
                        pltpu.SemaphoreType.DMA((DEPTH * nt,))],
    )(*shards)


def _ag_small(v, *, name):
    m, n = v.shape

    def body(x_ref, out_ref, red_ref, send_sems, recv_sems, local_sem):
        x, y, c = _pos()
        me, sibling = (x, y, c), (x, y, 1 - c)
        chips = _other_chips(x, y)
        rows = lambda px, py, pc: out_ref.at[pl.ds(pl.multiple_of((4 * px + 2 * py + pc) * m, 8), m), :]
        mine = pltpu.make_async_copy(x_ref, rows(*me), local_sem)
        mine.start()
        first = [_remote(x_ref, rows(*me), send_sems, recv_sems, 0, sibling)]
        first += [_remote(x_ref, rows(*me), send_sems, recv_sems, 1 + k, (*chip, c)) for k, chip in enumerate(chips)]
        for cp in first:
            cp.start()
        passed = [_remote(rows(*chip, c), rows(*chip, c), send_sems, recv_sems, 4 + k, sibling) for k, chip in enumerate(chips)]
        for k, chip in enumerate(chips):
            _remote(rows(*chip, c), rows(*chip, c), send_sems, recv_sems, 1 + k, me).wait_recv()
            passed[k].start()
        _remote(rows(*sibling), rows(*sibling), send_sems, recv_sems, 0, me).wait_recv()
        for k, chip in enumerate(chips):
            _remote(rows(*chip, 1 - c), rows(*chip, 1 - c), send_sems, recv_sems, 4 + k, me).wait_recv()
        for cp in first + passed:
            cp.wait_send()
        mine.wait()
        acc = out_ref[0:m, :]
        for d in range(1, N_DEV):
            acc = acc + out_ref[d * m:(d + 1) * m, :]
        red_ref[...] = acc

    return pl.pallas_call(
        body, name=name, in_specs=[VMEM_SPEC], out_specs=[VMEM_SPEC, VMEM_SPEC],
        out_shape=[jax.ShapeDtypeStruct((N_DEV * m, n), v.dtype), jax.ShapeDtypeStruct((m, n), v.dtype)],
        scratch_shapes=[pltpu.SemaphoreType.DMA((7,)), pltpu.SemaphoreType.DMA((7,)), pltpu.SemaphoreType.DMA],
    )(v)


def _halves(ref, c):
    rh = ref.shape[1] // 2
    return ref.at[:, pl.ds(pl.multiple_of(c * rh, 16), rh), :]


def _rs_split(gs, *, name):
    nt = len(gs)

    def body(*refs):
        g, mine, theirs = refs[:nt], refs[nt:2 * nt], refs[2 * nt:3 * nt]
        send_sems, recv_sems, local_sems = refs[3 * nt:]
        x, y, c = _pos()
        local = [pltpu.make_async_copy(_halves(g[t], c), mine[t], local_sems.at[t]) for t in range(nt)]
        swaps = [_remote(_halves(g[t], 1 - c), theirs[t], send_sems, recv_sems, t, (x, y, 1 - c)) for t in range(nt)]
        for cp in local + swaps:
            cp.start()
        for cp in swaps + local:
            cp.wait()

    half = [jax.ShapeDtypeStruct((a.shape[0], a.shape[1] // 2, a.shape[2]), a.dtype) for a in gs]
    out = pl.pallas_call(
        body, name=name, in_specs=[ANY] * nt, out_specs=[ANY] * (2 * nt), out_shape=half + half,
        scratch_shapes=[pltpu.SemaphoreType.DMA((nt,)), pltpu.SemaphoreType.DMA((nt,)), pltpu.SemaphoreType.DMA((nt,))],
    )(*gs)
    return out[:nt], out[nt:]


def _rs_exchange(ss, *, name):
    nt = len(ss)

    def body(*refs):
        s, out = refs[:nt], refs[nt:2 * nt]
        send_sems, recv_sems, local_sems = refs[2 * nt:]
        x, y, c = _pos()
        chip = 2 * x + y
        others = [(k, px, py, 2 * px + py) for k, (px, py) in enumerate(_other_chips(x, y))]
        local = [pltpu.make_async_copy(s[t].at[chip], out[t].at[chip], local_sems.at[t]) for t in range(nt)]
        sends = [_remote(s[t].at[j], out[t].at[chip], send_sems, recv_sems, 3 * t + k, (px, py, c))
                 for t in range(nt) for k, px, py, j in others]
        for cp in local + sends:
            cp.start()
        for t in range(nt):
            for k, _, _, j in others:
                _remote(s[t].at[j], out[t].at[j], send_sems, recv_sems, 3 * t + k, (x, y, c)).wait_recv()
        for cp in sends:
            cp.wait_send()
        for cp in local:
            cp.wait()

    return pl.pallas_call(
        body, name=name, in_specs=[ANY] * nt, out_specs=[ANY] * nt,
        out_shape=[jax.ShapeDtypeStruct(a.shape, a.dtype) for a in ss],
        scratch_shapes=[pltpu.SemaphoreType.DMA((3 * nt,)), pltpu.SemaphoreType.DMA((3 * nt,)), pltpu.SemaphoreType.DMA((nt,))],
    )(*ss)


def _rs_join(rs, *, name):
    nt = len(rs)

    def body(*refs):
        r, out = refs[:nt], refs[nt:2 * nt]
        send_sems, recv_sems, local_sems = refs[2 * nt:]
        x, y, c = _pos()
        rows = lambda t, h: out[t].at[pl.ds(pl.multiple_of(h * r[t].shape[0], 8), r[t].shape[0]), :]
        local = [pltpu.make_async_copy(r[t], rows(t, c), local_sems.at[t]) for t in range(nt)]
        sends = [_remote(r[t], rows(t, c), send_sems, recv_sems, t, (x, y, 1 - c)) for t in range(nt)]
        for cp in local + sends:
            cp.start()
        for t in range(nt):
            _remote(r[t], rows(t, 1 - c), send_sems, recv_sems, t, (x, y, c)).wait_recv()
        for cp in sends:
            cp.wait_send()
        for cp in local:
            cp.wait()

    return pl.pallas_call(
        body, name=name, in_specs=[ANY] * nt, out_specs=[ANY] * nt,
        out_shape=[jax.ShapeDtypeStruct((2 * a.shape[0], a.shape[1]), a.dtype) for a in rs],
        scratch_shapes=[pltpu.SemaphoreType.DMA((nt,)), pltpu.SemaphoreType.DMA((nt,)), pltpu.SemaphoreType.DMA((nt,))],
    )(*rs)


def _row_tile(rows, dtype):
    unit = 16 if dtype == BF16 else 8
    return next((t for t in (512, 256, 128, 64, 32, 16, 8) if rows % t == 0 and t % unit == 0), rows)


def _add_pair(a, b, *, name):
    n, rh, cols = a.shape
    tr = _row_tile(rh, BF16)

    def body(a_ref, b_ref, o_ref):
        o_ref[...] = (a_ref[...].astype(F32) + b_ref[...].astype(F32)).astype(BF16)

    blk = pl.BlockSpec((None, tr, cols), lambda j, i: (j, i, 0))
    return pl.pallas_call(
        body, name=name, grid=(n, rh // tr), in_specs=[blk, blk], out_specs=blk,
        out_shape=jax.ShapeDtypeStruct(a.shape, BF16), compiler_params=_cp("parallel", "parallel"),
    )(a, b)


def _sum_chips(b, *, name):
    n, rh, cols = b.shape
    tr = _row_tile(rh, BF16)

    def body(b_ref, o_ref):
        acc = b_ref[0].astype(F32)
        for j in range(1, n):
            acc = acc + b_ref[j].astype(F32)
        o_ref[...] = acc

    return pl.pallas_call(
        body, name=name, grid=(rh // tr,), in_specs=[pl.BlockSpec((n, tr, cols), lambda i: (0, i, 0))],
        out_specs=pl.BlockSpec((tr, cols), lambda i: (i, 0)),
        out_shape=jax.ShapeDtypeStruct((rh, cols), F32), compiler_params=_cp("parallel"),
    )(b)


BIG = ("w_in", "w_out", "w_ffn_in", "w_down")
W_IN_SHARD = (N_MAIN + 2 * GH) // N_CHIP


def _reduce_scatter_layer(lg, l):
    g_in = jnp.stack([lg["w_in"][:, j * W_IN_SHARD:(j + 1) * W_IN_SHARD] for j in range(N_CHIP)])
    gs = [g_in, lg["w_out"].reshape(N_CHIP, D // N_CHIP, D), lg["w_ffn_in"], lg["w_down"].reshape(N_CHIP, DFF // N_CHIP, D)]
    mine, theirs = _rs_split(gs, name=f"rs{l}_split")
    pairs = [_add_pair(a, b, name=f"rs{l}_add_{n}") for a, b, n in zip(mine, theirs, BIG)]
    got = _rs_exchange(pairs, name=f"rs{l}_exchange")
    return _rs_join([_sum_chips(b, name=f"rs{l}_sum_{n}") for b, n in zip(got, BIG)], name=f"rs{l}_join")


def _size(shape):
    n = 1
    for s in shape:
        n *= s
    return n


def _pack_flat(parts, total, dtype):
    flat = jnp.concatenate([p.reshape(-1).astype(dtype) for p in parts])
    return jnp.pad(flat, (0, total - flat.shape[0]))


def _unpack_flat(flat, shapes):
    out, o = [], 0
    for s in shapes:
        out.append(flat[o:o + _size(s)].reshape(s))
        o += _size(s)
    return out


WEIGHTS = ("attn_norm", "w_in", "attn_sinks", "gdn_conv_w", "gdn_a_log", "gdn_dt_bias", "gdn_norm", "w_out", "ffn_norm",
           "w_ffn_in", "ffn_conv_w", "ffn_conv_b", "w_down", "final_norm")
SMALL = {"attn_norm": (DEPTH, D), "attn_sinks": (DEPTH, HQ), "gdn_a_log": (DEPTH, GH), "gdn_dt_bias": (DEPTH, GH),
         "gdn_norm": (DEPTH, GD), "ffn_norm": (DEPTH, D), "ffn_conv_b": (DEPTH, DFF), "final_norm": (D,)}
CONV_FULL = {"gdn_conv_w": (DEPTH, GK, 3 * GH * GD), "ffn_conv_w": (DEPTH, FK, DFF)}
CONV_SHARD = {"gdn_conv_w": (DEPTH, GK, 3 * GH * GD // N_CHIP), "ffn_conv_w": (DEPTH, FK, DFF // N_CHIP)}
CONV_ROWS, SMALLG_ROWS, SMALLW_ROWS = 64, 320, 144


def kernel(x, attn_norm, w_in, attn_sinks, gdn_conv_w, gdn_a_log, gdn_dt_bias, gdn_norm, w_out, ffn_norm, w_ffn_in, ffn_conv_w, ffn_conv_b, w_down, final_norm, loss_target, m_attn_norm, m_w_in, m_attn_sinks, m_gdn_conv_w, m_gdn_a_log, m_gdn_dt_bias, m_gdn_norm, m_w_out, m_ffn_norm, m_w_ffn_in, m_ffn_conv_w, m_ffn_conv_b, m_w_down, m_final_norm, v_attn_norm, v_w_in, v_attn_sinks, v_gdn_conv_w, v_gdn_a_log, v_gdn_dt_bias, v_gdn_norm, v_w_out, v_ffn_norm, v_w_ffn_in, v_ffn_conv_w, v_ffn_conv_b, v_w_down, v_final_norm):
    w = dict(zip(WEIGHTS, (attn_norm, w_in, attn_sinks, gdn_conv_w, gdn_a_log, gdn_dt_bias, gdn_norm, w_out, ffn_norm,
                           w_ffn_in, ffn_conv_w, ffn_conv_b, w_down, final_norm)))
    m = dict(zip(WEIGHTS, (m_attn_norm, m_w_in, m_attn_sinks, m_gdn_conv_w, m_gdn_a_log, m_gdn_dt_bias, m_gdn_norm, m_w_out,
                           m_ffn_norm, m_w_ffn_in, m_ffn_conv_w, m_ffn_conv_b, m_w_down, m_final_norm)))
    v = dict(zip(WEIGHTS, (v_attn_norm, v_w_in, v_attn_sinks, v_gdn_conv_w, v_gdn_a_log, v_gdn_dt_bias, v_gdn_norm, v_w_out,
                           v_ffn_norm, v_w_ffn_in, v_ffn_conv_w, v_ffn_conv_b, v_w_down, v_final_norm)))
    cx, cy, _ = _pos()
    chip = 2 * cx + cy

    gath = dict(zip(BIG, _ag_weights([w[n].astype(BF16) for n in BIG], name="gather_weights")))
    full = {"w_in": jnp.concatenate([gath["w_in"][:, j] for j in range(N_CHIP)], axis=2),
            "w_out": gath["w_out"].reshape(DEPTH, D, D), "w_ffn_in": gath["w_ffn_in"],
            "w_down": gath["w_down"].reshape(DEPTH, DFF, D)}
    cpack = _pack_flat([w[n] for n in CONV_SHARD], CONV_ROWS * 128, F32).reshape(CONV_ROWS, 128)
    cgath, _ = _ag_small(cpack, name="gather_conv_w")
    cgath = cgath.reshape(N_DEV, CONV_ROWS * 128)
    cshards = [_unpack_flat(cgath[2 * j], list(CONV_SHARD.values())) for j in range(N_CHIP)]
    conv = {n: jnp.concatenate([cshards[j][i] for j in range(N_CHIP)], axis=2) for i, n in enumerate(CONV_SHARD)}

    Ws = [_layer_weights(full["w_in"][l], full["w_out"][l], full["w_ffn_in"][l], full["w_down"][l]) for l in range(DEPTH)]
    Ps = [_layer_params(attn_norm[l], attn_sinks[l], conv["gdn_conv_w"][l], gdn_a_log[l], gdn_dt_bias[l], gdn_norm[l],
                        ffn_norm[l], conv["ffn_conv_w"][l], ffn_conv_b[l]) for l in range(DEPTH)]

    h, saved = x[0], []
    for l in range(DEPTH):
        h, sv = _layer_fwd(h, Ws[l], Ps[l], l)
        saved.append(sv)
    dx, g_final, loss_part = _loss_head(h, final_norm[None], loss_target[0], name="loss_head")
    lg, reduced = [None] * DEPTH, [None] * DEPTH
    for l in reversed(range(DEPTH)):
        dx, lg[l] = _layer_bwd(dx, saved[l], Ws[l], Ps[l], l)
        reduced[l] = _reduce_scatter_layer(lg[l], l)
    grad_x = dx[None]
    stacked = lambda n: jnp.stack([lg[l][n] for l in range(DEPTH)])

    small_parts = [g_final[0] if n == "final_norm" else stacked(n) for n in SMALL] + [stacked(n) for n in CONV_FULL]
    svec = _pack_flat(small_parts + [loss_part[0, :1]], SMALLG_ROWS * 128, F32).reshape(SMALLG_ROWS, 128)
    _, sred = _ag_small(svec, name="reduce_small")
    small_g = _unpack_flat(sred.reshape(-1), list(SMALL.values()) + list(CONV_FULL.values()) + [(1,)])
    g = dict(zip(list(SMALL) + list(CONV_FULL), small_g[:-1]))
    loss = small_g[-1][0]
    for n in CONV_FULL:
        wd = CONV_SHARD[n][2]
        g[n] = lax.dynamic_slice_in_dim(g[n], chip * wd, wd, axis=2)

    g.update({n: jnp.stack([reduced[l][i] for l in range(DEPTH)]) for i, n in enumerate(BIG)})

    delta, new_m, new_v = {}, {}, {}
    for n in BIG:
        delta[n], new_m[n], new_v[n] = _adamw(w[n], g[n], m[n], v[n], name=f"adamw_{n}")
    small_names = list(SMALL) + list(CONV_SHARD)
    small_shapes = list(SMALL.values()) + list(CONV_SHARD.values())
    packed = [_pack_flat([t[n] for n in small_names], SMALLW_ROWS * 128, F32).reshape(SMALLW_ROWS, 128) for t in (w, g, m, v)]
    for res, out in zip(_adamw(*packed, name="adamw_small"), (delta, new_m, new_v)):
        out.update(zip(small_names, _unpack_flat(res.reshape(-1), small_shapes)))

    return (loss, grad_x, *[g[n] for n in WEIGHTS], *[delta[n] for n in WEIGHTS], *[new_m[n] for n in WEIGHTS],
            *[new_v[n] for n in WEIGHTS])
```

```python
import functools

import jax
import jax.numpy as jnp
from jax import lax
from jax.experimental import pallas as pl
from jax.experimental.pallas import tpu as pltpu

F32, BF16 = jnp.float32, jnp.bfloat16
HIGHEST = lax.Precision.HIGHEST
MESH = pl.DeviceIdType.MESH

D = 1024
T = 4096
DEPTH = 2
HQ, HKV, DH, WIN = 8, 2, 64, 128
GH, GD, GC, GK = 4, 128, 64, 4
DFF, FK = 2816, 3
EPS = 1e-6
N_MAIN = 2816
N_PROJ = N_MAIN + 128
COL_K, COL_V, COL_G, COL_Z, COL_BA = 4, 5, 6, 18, 22
N_CHUNK = T // GC
N_DEV, N_CHIP = 8, 4
FFN_CW = 2 * DFF // N_CHIP
LR, B1, B2, AEPS, WD, STEP = 0.001, 0.9, 0.999, 1e-08, 0.01, 10

VMEM_LIMIT = 56 * 1024 * 1024
ANY = pl.BlockSpec(memory_space=pl.ANY)
VMEM_SPEC = pl.BlockSpec(memory_space=pltpu.VMEM)


def _cp(*sem):
    return pltpu.CompilerParams(dimension_semantics=sem if sem else None, vmem_limit_bytes=VMEM_LIMIT)


def _dot(a, b):
    return jnp.dot(a.astype(BF16), b.astype(BF16), preferred_element_type=F32)


def _dot_nt(a, b):
    return lax.dot_general(a.astype(BF16), b.astype(BF16), (((1,), (1,)), ((), ())), preferred_element_type=F32)


def _dot_tn(a, b):
    return lax.dot_general(a.astype(BF16), b.astype(BF16), (((0,), (0,)), ((), ())), preferred_element_type=F32)


def _iota(shape, dim):
    return lax.broadcasted_iota(jnp.int32, shape, dim)


def _col(x, idx):
    return jnp.sum(jnp.where(_iota(x.shape, 1) == idx, x, 0.0), axis=1, keepdims=True)


def _silu(y):
    return y * jax.nn.sigmoid(y)


_MM_DN = {"nn": (((1,), (0,)), ((), ())), "nt": (((1,), (1,)), ((), ())), "tn": (((0,), (0,)), ((), ()))}


def _mm(a, b, *, name, mode, dims, tm, tn, tk, res=None, out_dtype=F32, a_spec=None, b_spec=None, out_spec=None,
        out_shape=None):
    M, N, K = dims
    assert M % tm == 0 and N % tn == 0 and K % tk == 0, (name, dims)
    nk = K // tk
    dn = _MM_DN[mode]

    def body(*refs):
        if res is None:
            a_ref, b_ref, o_ref, acc = refs
        else:
            a_ref, b_ref, r_ref, o_ref, acc = refs
        k = pl.program_id(2)

        @pl.when(k == 0)
        def _():
            acc[...] = jnp.zeros_like(acc)

        acc[...] += lax.dot_general(a_ref[...].astype(BF16), b_ref[...].astype(BF16), dn, preferred_element_type=F32)

        @pl.when(k == nk - 1)
        def _():
            r = acc[...]
            if res is not None:
                r = r + r_ref[...]
            o_ref[...] = r.astype(out_dtype)

    if a_spec is None:
        a_spec = pl.BlockSpec((tk, tm), lambda i, j, k: (k, i)) if mode == "tn" else pl.BlockSpec((tm, tk), lambda i, j, k: (i, k))
    if b_spec is None:
        b_spec = pl.BlockSpec((tn, tk), lambda i, j, k: (j, k)) if mode == "nt" else pl.BlockSpec((tk, tn), lambda i, j, k: (k, j))
    in_specs, args = [a_spec, b_spec], [a, b]
    if res is not None:
        in_specs.append(pl.BlockSpec((tm, tn), lambda i, j, k: (i, j)))
        args.append(res)
    return pl.pallas_call(
        body, name=name, grid=(M // tm, N // tn, nk), in_specs=in_specs,
        out_specs=out_spec or pl.BlockSpec((tm, tn), lambda i, j, k: (i, j)),
        out_shape=out_shape or jax.ShapeDtypeStruct((M, N), out_dtype),
        scratch_shapes=[pltpu.VMEM((tm, tn), F32)],
        compiler_params=_cp("parallel", "parallel", "arbitrary"),
    )(*args)


def _rms(x, w):
    return x * lax.rsqrt(jnp.mean(x * x, axis=-1, keepdims=True) + EPS) * w


def _rmsnorm_fwd(x, w, *, name, tm=512):
    def body(x_ref, w_ref, o_ref):
        o_ref[...] = _rms(x_ref[...], w_ref[...]).astype(BF16)

    return pl.pallas_call(
        body, name=name, grid=(x.shape[0] // tm,),
        in_specs=[pl.BlockSpec((tm, D), lambda i: (i, 0)), pl.BlockSpec((1, D), lambda i: (0, 0))],
        out_specs=pl.BlockSpec((tm, D), lambda i: (i, 0)),
        out_shape=jax.ShapeDtypeStruct(x.shape, BF16), compiler_params=_cp("parallel"),
    )(x, w)


def _rmsnorm_bwd(x, w, dh, dres, *, name, tm=512):
    def body(x_ref, w_ref, dh_ref, dr_ref, dx_ref, dw_ref):
        _, vjp = jax.vjp(_rms, x_ref[...], w_ref[...])
        dx, dw = vjp(dh_ref[...])
        dx_ref[...] = dx + dr_ref[...]

        @pl.when(pl.program_id(0) == 0)
        def _():
            dw_ref[...] = jnp.zeros_like(dw_ref)

        dw_ref[...] += dw

    row = pl.BlockSpec((tm, D), lambda i: (i, 0))
    vec = pl.BlockSpec((1, D), lambda i: (0, 0))
    return pl.pallas_call(
        body, name=name, grid=(x.shape[0] // tm,), in_specs=[row, vec, row, row], out_specs=[row, vec],
        out_shape=[jax.ShapeDtypeStruct(x.shape, F32), jax.ShapeDtypeStruct((1, D), F32)],
        compiler_params=_cp("arbitrary"),
    )(x, w, dh, dres)


def _loss_head(x, w, tgt, *, name, tm=512):
    def f(xv, wv, tv):
        err = _rms(xv, wv) - tv
        per_row = jnp.sum(err * err, axis=1, keepdims=True) * (0.5 / D)
        return jnp.sum(per_row, axis=0, keepdims=True)

    def body(x_ref, w_ref, t_ref, dx_ref, dw_ref, loss_ref):
        tv = t_ref[...]
        loss, vjp = jax.vjp(lambda xv, wv: f(xv, wv, tv), x_ref[...], w_ref[...])
        dx, dw = vjp(jnp.ones((1, 1), F32))
        dx_ref[...] = dx

        @pl.when(pl.program_id(0) == 0)
        def _():
            dw_ref[...] = jnp.zeros_like(dw_ref)
            loss_ref[...] = jnp.zeros_like(loss_ref)

        dw_ref[...] += dw
        loss_ref[...] += jnp.broadcast_to(loss, loss_ref.shape)

    row = pl.BlockSpec((tm, D), lambda i: (i, 0))
    vec = pl.BlockSpec((1, D), lambda i: (0, 0))
    return pl.pallas_call(
        body, name=name, grid=(x.shape[0] // tm,), in_specs=[row, vec, row],
        out_specs=[row, vec, pl.BlockSpec((1, 128), lambda i: (0, 0))],
        out_shape=[jax.ShapeDtypeStruct(x.shape, F32), jax.ShapeDtypeStruct((1, D), F32),
                   jax.ShapeDtypeStruct((1, 128), F32)],
        compiler_params=_cp("arbitrary"),
    )(x, w, tgt)


def _swa_group(qg, k2, v2, sk, blk, h2):
    s = _dot_nt(qg, k2) * (DH ** -0.5)
    r, c = _iota(s.shape, 0), _iota(s.shape, 1)
    g = r >> 7
    rel = (r & (WIN - 1)) + WIN - c
    valid = (rel >= 0) & (rel < WIN) & ((blk > 0) | (c >= WIN))
    slope = jnp.zeros(s.shape, F32)
    g1 = _iota((s.shape[0], 1), 0) >> 7
    sink = jnp.zeros((s.shape[0], 1), F32)
    for gi in range(HQ // HKV):
        hq = h2 * (HQ // HKV) + gi
        slope = jnp.where(g == gi, 2.0 ** (-8.0 * (hq + 1) / HQ), slope)
        sink = jnp.where(g1 == gi, _col(sk, hq), sink)
    s = jnp.where(valid, s - slope * rel.astype(F32), -1e30)
    m = lax.stop_gradient(jnp.maximum(jnp.max(s, axis=1, keepdims=True), sink))
    p = jnp.exp(s - m)
    den = jnp.sum(p, axis=1, keepdims=True) + jnp.exp(sink - m)
    return _dot(p / den, v2)


def _swa_split_q(q, h2):
    G = HQ // HKV
    return jnp.concatenate([q[:, (h2 * G + g) * DH:(h2 * G + g + 1) * DH] for g in range(G)], axis=0)


def _swa_merge_q(parts):
    G = HQ // HKV
    return jnp.concatenate([parts[h2][g * WIN:(g + 1) * WIN] for h2 in range(HKV) for g in range(G)], axis=1)


def _swa_specs():
    prev = lambda i: jnp.maximum(jnp.minimum(i, T // WIN - 1) - 1, 0)
    cur = lambda i: jnp.minimum(i, T // WIN - 1)
    return [
        pl.BlockSpec((WIN, HQ * DH), lambda i: (cur(i), 0)),
        pl.BlockSpec((WIN, 128), lambda i: (prev(i), COL_K)),
        pl.BlockSpec((WIN, 128), lambda i: (cur(i), COL_K)),
        pl.BlockSpec((WIN, 128), lambda i: (prev(i), COL_V)),
        pl.BlockSpec((WIN, 128), lambda i: (cur(i), COL_V)),
        pl.BlockSpec((1, 128), lambda i: (0, 0)),
    ]


def _swa_fwd(proj, sinks, *, name):
    def body(q_ref, kp_ref, kc_ref, vp_ref, vc_ref, sk_ref, o_ref):
        i = pl.program_id(0)
        q, sk = q_ref[...], sk_ref[...]
        outs = []
        for h2 in range(HKV):
            sl = slice(h2 * DH, (h2 + 1) * DH)
            k2 = jnp.concatenate([kp_ref[:, sl], kc_ref[:, sl]], axis=0)
            v2 = jnp.concatenate([vp_ref[:, sl], vc_ref[:, sl]], axis=0)
            outs.append(_swa_group(_swa_split_q(q, h2), k2, v2, sk, i, h2))
        o_ref[...] = _swa_merge_q(outs).astype(BF16)

    return pl.pallas_call(
        body, name=name, grid=(T // WIN,), in_specs=_swa_specs(),
        out_specs=pl.BlockSpec((WIN, HQ * DH), lambda i: (i, 0)),
        out_shape=jax.ShapeDtypeStruct((T, HQ * DH + GH * GD), BF16), compiler_params=_cp("parallel"),
    )(proj, proj, proj, proj, proj, sinks)


def _swa_bwd(proj, sinks, do, dproj, *, name):
    NB = T // WIN
    QW = HQ * DH

    def body(q_ref, kp_ref, kc_ref, vp_ref, vc_ref, sk_ref, do_ref, _, out_ref, dsk_ref, cq, ck, cv):
        i = pl.program_id(0)

        @pl.when(i == 0)
        def _():
            cq[...] = jnp.zeros_like(cq)
            ck[...] = jnp.zeros_like(ck)
            cv[...] = jnp.zeros_like(cv)
            dsk_ref[...] = jnp.zeros_like(dsk_ref)

        @pl.when(i < NB)
        def _():
            q, sk, dov = q_ref[...], sk_ref[...], do_ref[...].astype(F32)
            dqs, dkp, dkc, dvp, dvc = [], [], [], [], []
            dsk = jnp.zeros_like(sk)
            for h2 in range(HKV):
                sl = slice(h2 * DH, (h2 + 1) * DH)
                k2 = jnp.concatenate([kp_ref[:, sl], kc_ref[:, sl]], axis=0)
                v2 = jnp.concatenate([vp_ref[:, sl], vc_ref[:, sl]], axis=0)
                _, vjp = jax.vjp(functools.partial(_swa_group, blk=i, h2=h2), _swa_split_q(q, h2), k2, v2, sk)
                dqg, dk2, dv2, dsk_h = vjp(_swa_split_q(dov, h2))
                dqs.append(dqg)
                dkp.append(dk2[:WIN]); dkc.append(dk2[WIN:])
                dvp.append(dv2[:WIN]); dvc.append(dv2[WIN:])
                dsk = dsk + dsk_h
            out_ref[:, :QW] = cq[...].astype(BF16)
            out_ref[:, QW:QW + 128] = (ck[...] + jnp.concatenate(dkp, axis=1)).astype(BF16)
            out_ref[:, QW + 128:] = (cv[...] + jnp.concatenate(dvp, axis=1)).astype(BF16)
            cq[...] = _swa_merge_q(dqs)
            ck[...] = jnp.concatenate(dkc, axis=1)
            cv[...] = jnp.concatenate(dvc, axis=1)
            dsk_ref[...] += dsk

        @pl.when(i == NB)
        def _():
            out_ref[:, :QW] = cq[...].astype(BF16)
            out_ref[:, QW:QW + 128] = ck[...].astype(BF16)
            out_ref[:, QW + 128:] = cv[...].astype(BF16)

    qblk = pl.BlockSpec((WIN, QW), lambda i: (jnp.minimum(i, NB - 1), 0))
    return pl.pallas_call(
        body, name=name, grid=(NB + 1,), in_specs=_swa_specs() + [qblk, ANY],
        out_specs=[pl.BlockSpec((WIN, QW + 256), lambda i: (jnp.maximum(i - 1, 0), 0)), pl.BlockSpec((1, 128), lambda i: (0, 0))],
        out_shape=[jax.ShapeDtypeStruct(dproj.shape, dproj.dtype), jax.ShapeDtypeStruct((1, 128), F32)],
        scratch_shapes=[pltpu.VMEM((WIN, QW), F32), pltpu.VMEM((WIN, 128), F32), pltpu.VMEM((WIN, 128), F32)],
        input_output_aliases={7: 0}, compiler_params=_cp("arbitrary"),
    )(proj, proj, proj, proj, proj, sinks, do, dproj)


RC = 256
HALO = 8


def _load_ext(ref, c):
    nch = T // RC
    r0 = pl.multiple_of(c * RC, RC)
    p0 = pl.multiple_of(jnp.maximum(r0 - HALO, 0), HALO)
    n0 = pl.multiple_of(jnp.minimum(r0 + RC, T - HALO), HALO)
    prev = jnp.where(c > 0, ref[pl.ds(p0, HALO), :].astype(F32), 0.0)
    nxt = jnp.where(c < nch - 1, ref[pl.ds(n0, HALO), :].astype(F32), 0.0)
    return jnp.concatenate([prev, ref[pl.ds(r0, RC), :].astype(F32), nxt], axis=0)


def _conv_ext(xe, w, K):
    y = w[K - 1:K, :] * xe
    for s in range(1, K):
        y = y + w[K - 1 - s:K - s, :] * pltpu.roll(xe, s, 0)
    return y


def _conv_bwd_ext(xe, dye, w, K, dw_ref):
    n = xe.shape[0]
    own = slice(HALO, HALO + RC)
    dx = w[K - 1:K, :] * dye
    dw_ref[K - 1:K, :] += jnp.sum(dye[own] * xe[own], axis=0, keepdims=True)
    for s in range(1, K):
        dx = dx + w[K - 1 - s:K - s, :] * pltpu.roll(dye, n - s, 0)
        dw_ref[K - 1 - s:K - s, :] += jnp.sum(dye[own] * pltpu.roll(xe, s, 0)[own], axis=0, keepdims=True)
    return dx[own]


def _gdn_post_conv(y, is_qk):
    a = _silu(y)
    nrm = a * lax.rsqrt(jnp.sum(a * a, axis=1, keepdims=True) + EPS)
    return jnp.where(is_qk, nrm, a)


def _gdn_prep_fwd(proj, conv_w, *, name):
    nblk = 3 * GH

    def body(x_ref, w_ref, o_ref):
        is_qk = pl.program_id(0) < 2 * GH
        w = w_ref[...]

        def chunk(c, carry):
            y = _conv_ext(_load_ext(x_ref, c), w, GK)[HALO:HALO + RC]
            o_ref[pl.ds(pl.multiple_of(c * RC, RC), RC), :] = _gdn_post_conv(y, is_qk)
            return carry

        lax.fori_loop(0, T // RC, chunk, 0)

    return pl.pallas_call(
        body, name=name, grid=(nblk,),
        in_specs=[pl.BlockSpec((T, 128), lambda j: (0, COL_G + j)), pl.BlockSpec((GK, 128), lambda j: (0, j))],
        out_specs=pl.BlockSpec((T, 128), lambda j: (0, j)),
        out_shape=jax.ShapeDtypeStruct((T, nblk * 128), F32), compiler_params=_cp("parallel"),
    )(proj, conv_w)


def _gdn_prep_bwd(proj, conv_w, dout, dproj, *, name):
    nblk = 3 * GH

    def body(x_ref, w_ref, d_ref, _, dx_ref, dw_ref):
        is_qk = pl.program_id(0) < 2 * GH
        w = w_ref[...]
        dw_ref[...] = jnp.zeros_like(dw_ref)

        def chunk(c, carry):
            xe = _load_ext(x_ref, c)
            _, vjp = jax.vjp(lambda y: _gdn_post_conv(y, is_qk), _conv_ext(xe, w, GK))
            (dye,) = vjp(_load_ext(d_ref, c))
            dx_ref[pl.ds(pl.multiple_of(c * RC, RC), RC), :] = _conv_bwd_ext(xe, dye, w, GK, dw_ref).astype(BF16)
            return carry

        lax.fori_loop(0, T // RC, chunk, 0)

    return pl.pallas_call(
        body, name=name, grid=(nblk,),
        in_specs=[pl.BlockSpec((T, 128), lambda j: (0, COL_G + j)), pl.BlockSpec((GK, 128), lambda j: (0, j)),
                  pl.BlockSpec((T, 128), lambda j: (0, j)), ANY],
        out_specs=[pl.BlockSpec((T, 128), lambda j: (0, COL_G + j)), pl.BlockSpec((GK, 128), lambda j: (0, j))],
        out_shape=[jax.ShapeDtypeStruct(dproj.shape, dproj.dtype), jax.ShapeDtypeStruct((GK, nblk * 128), F32)],
        input_output_aliases={3: 0}, compiler_params=_cp("parallel"),
    )(proj, conv_w, dout, dproj)


def _ffn_post_conv(y, b, up):
    return _silu(y + b) * up


def _ffn_act_fwd(gu, conv_w, conv_b, *, name):
    nblk = DFF // 128

    def body(g_ref, u_ref, w_ref, b_ref, o_ref):
        w, b = w_ref[...], b_ref[...]

        def chunk(c, carry):
            rows = pl.ds(pl.multiple_of(c * RC, RC), RC)
            y = _conv_ext(_load_ext(g_ref, c), w, FK)[HALO:HALO + RC]
            o_ref[rows, :] = _ffn_post_conv(y, b, u_ref[rows, :]).astype(BF16)
            return carry

        lax.fori_loop(0, T // RC, chunk, 0)

    return pl.pallas_call(
        body, name=name, grid=(nblk,),
        in_specs=[pl.BlockSpec((T, 128), lambda j: (0, j)), pl.BlockSpec((T, 128), lambda j: (0, nblk + j)),
                  pl.BlockSpec((FK, 128), lambda j: (0, j)), pl.BlockSpec((1, 128), lambda j: (0, j))],
        out_specs=pl.BlockSpec((T, 128), lambda j: (0, j)),
        out_shape=jax.ShapeDtypeStruct((T, DFF), BF16), compiler_params=_cp("parallel"),
    )(gu, gu, conv_w, conv_b)


def _ffn_act_bwd(gu, conv_w, conv_b, dact, *, name):
    nblk = DFF // 128

    def body(g_ref, u_ref, w_ref, b_ref, d_ref, dgu_ref, dw_ref, db_ref):
        dg_ref, du_ref = dgu_ref.at[0], dgu_ref.at[1]
        w, b = w_ref[...], b_ref[...]
        dw_ref[...] = jnp.zeros_like(dw_ref)
        db_ref[...] = jnp.zeros_like(db_ref)

        def chunk(c, carry):
            rows = pl.ds(pl.multiple_of(c * RC, RC), RC)
            xe = _load_ext(g_ref, c)
            ue = _load_ext(u_ref, c)
            _, vjp = jax.vjp(_ffn_post_conv, _conv_ext(xe, w, FK), b, ue)
            dye, db, due = vjp(_load_ext(d_ref, c))
            du_ref[rows, :] = due[HALO:HALO + RC].astype(BF16)
            db_ref[...] += jnp.sum(dye[HALO:HALO + RC], axis=0, keepdims=True)
            dg_ref[rows, :] = _conv_bwd_ext(xe, dye, w, FK, dw_ref).astype(BF16)
            return carry

        lax.fori_loop(0, T // RC, chunk, 0)

    col = pl.BlockSpec((T, 128), lambda j: (0, j))
    return pl.pallas_call(
        body, name=name, grid=(nblk,),
        in_specs=[col, pl.BlockSpec((T, 128), lambda j: (0, nblk + j)), pl.BlockSpec((FK, 128), lambda j: (0, j)),
                  pl.BlockSpec((1, 128), lambda j: (0, j)), col],
        out_specs=[pl.BlockSpec((2, T, 128), lambda j: (0, 0, j)), pl.BlockSpec((FK, 128), lambda j: (0, j)),
                   pl.BlockSpec((1, 128), lambda j: (0, j))],
        out_shape=[jax.ShapeDtypeStruct((2, T, DFF), BF16), jax.ShapeDtypeStruct((FK, DFF), F32),
                   jax.ShapeDtypeStruct((1, DFF), F32)],
        compiler_params=_cp("parallel"),
    )(gu, gu, conv_w, conv_b, dact)


def _gdn_gates(ba, alog, dtb):
    beta = jax.nn.sigmoid(ba)
    g = -jnp.exp(alog) * jax.nn.softplus(ba + dtb)
    tril = (_iota((GC, GC), 0) >= _iota((GC, GC), 1)).astype(F32)
    return beta, jnp.dot(tril, g, precision=HIGHEST, preferred_element_type=F32)


def _gdn_decay(Gc):
    r, c = _iota((GC, GC), 0), _iota((GC, GC), 1)
    eye = (r == c).astype(F32)
    grow = jnp.dot(jnp.ones((GC, GC), F32), eye * Gc, precision=HIGHEST, preferred_element_type=F32)
    return jnp.exp(jnp.where(r >= c, Gc - grow, -1e30))


def _gdn_A(k, beta, Gc):
    strict = _iota((GC, GC), 0) > _iota((GC, GC), 1)
    return jnp.where(strict, _dot_nt(k * beta, k) * _gdn_decay(Gc), 0.0)


def _tri_inv(A):
    eye = (_iota((GC, GC), 0) == _iota((GC, GC), 1)).astype(F32)
    Tm, P = eye - A, A
    for _ in range(GC.bit_length() - 2):
        P = _dot(P, P)
        Tm = Tm + _dot(Tm, P)
    return Tm


def _gdn_chunk(q, k, v, beta, Gc, S, Tm):
    decay = _gdn_decay(Gc)
    eG = jnp.exp(Gc)
    kb = k * beta
    u = _dot(Tm, v * beta)
    w = _dot(Tm, kb * eG)
    qs = q * (GD ** -0.5)
    qk = _dot_nt(qs, k) * decay
    glast = jnp.sum(jnp.where(_iota(Gc.shape, 0) == GC - 1, Gc, 0.0), axis=0, keepdims=True)
    kd = k * jnp.exp(glast - Gc)
    v_new = u - _dot(w, S)
    o = _dot(qs * eG, S) + _dot(qk, v_new)
    return o, S * jnp.exp(glast) + _dot_tn(kd, v_new)


def _gdn_chunk_fwd(qkv_c, proj, alog, dtb, *, name):
    W3 = 3 * GH * GD

    def body(x_ref, ba_ref, al_ref, dt_ref, o_ref, s_ref, t_ref, S):
        @pl.when(pl.program_id(0) == 0)
        def _():
            S[...] = jnp.zeros_like(S)

        beta_all, gc_all = _gdn_gates(ba_ref[...], al_ref[...], dt_ref[...])
        for h in range(GH):
            q = x_ref[:, h * GD:(h + 1) * GD]
            k = x_ref[:, (GH + h) * GD:(GH + h + 1) * GD]
            v = x_ref[:, (2 * GH + h) * GD:(2 * GH + h + 1) * GD]
            beta, Gc = _col(beta_all, h), _col(gc_all, GH + h)
            Tm = _tri_inv(_gdn_A(k, beta, Gc))
            Sh = S[h]
            s_ref[0, h] = Sh
            t_ref[0, h] = Tm
            o, S_new = _gdn_chunk(q, k, v, beta, Gc, Sh, Tm)
            o_ref[:, h * GD:(h + 1) * GD] = o
            S[h] = S_new

    vec = pl.BlockSpec((1, 128), lambda n: (0, 0))
    return pl.pallas_call(
        body, name=name, grid=(N_CHUNK,),
        in_specs=[pl.BlockSpec((GC, W3), lambda n: (n, 0)), pl.BlockSpec((GC, 128), lambda n: (n, COL_BA)), vec, vec],
        out_specs=[pl.BlockSpec((GC, GH * GD), lambda n: (n, 0)),
                   pl.BlockSpec((1, GH, GD, GD), lambda n: (n, 0, 0, 0)),
                   pl.BlockSpec((1, GH, GC, GC), lambda n: (n, 0, 0, 0))],
        out_shape=[jax.ShapeDtypeStruct((T, GH * GD), F32), jax.ShapeDtypeStruct((N_CHUNK, GH, GD, GD), F32),
                   jax.ShapeDtypeStruct((N_CHUNK, GH, GC, GC), F32)],
        scratch_shapes=[pltpu.VMEM((GH, GD, GD), F32)], compiler_params=_cp("arbitrary"),
    )(qkv_c, proj, alog, dtb)


def _gdn_chunk_bwd(qkv_c, proj, alog, dtb, s_all, t_all, do, dproj, *, name):
    W3 = 3 * GH * GD
    rev = lambda n: N_CHUNK - 1 - n

    def body(x_ref, ba_ref, al_ref, dt_ref, s_ref, t_ref, do_ref, _, dx_ref, dba_ref, dal_ref, ddt_ref, dS):
        @pl.when(pl.program_id(0) == 0)
        def _():
            dS[...] = jnp.zeros_like(dS)
            dal_ref[...] = jnp.zeros_like(dal_ref)
            ddt_ref[...] = jnp.zeros_like(ddt_ref)

        (beta_all, gc_all), vjp_gates = jax.vjp(_gdn_gates, ba_ref[...], al_ref[...], dt_ref[...])
        dbeta_all, dgc_all = jnp.zeros_like(beta_all), jnp.zeros_like(gc_all)
        for h in range(GH):
            q = x_ref[:, h * GD:(h + 1) * GD]
            k = x_ref[:, (GH + h) * GD:(GH + h + 1) * GD]
            v = x_ref[:, (2 * GH + h) * GD:(2 * GH + h + 1) * GD]
            Tm = t_ref[0, h]

            def f(q, k, v, b_all, g_all, Sh, Tm, h=h):
                return _gdn_chunk(q, k, v, _col(b_all, h), _col(g_all, GH + h), Sh, Tm)

            def fa(k, b_all, g_all, h=h):
                return _gdn_A(k, _col(b_all, h), _col(g_all, GH + h))

            _, vjp = jax.vjp(f, q, k, v, beta_all, gc_all, s_ref[0, h], Tm)
            dq, dk, dv, db1, dg1, dS_prev, dT = vjp((do_ref[:, h * GD:(h + 1) * GD], dS[h]))
            dA = -_dot_tn(Tm, _dot_nt(dT, Tm))
            _, vjp_a = jax.vjp(fa, k, beta_all, gc_all)
            dk2, db2, dg2 = vjp_a(dA)
            dS[h] = dS_prev
            dx_ref[:, h * GD:(h + 1) * GD] = dq
            dx_ref[:, (GH + h) * GD:(GH + h + 1) * GD] = dk + dk2
            dx_ref[:, (2 * GH + h) * GD:(2 * GH + h + 1) * GD] = dv
            dbeta_all = dbeta_all + db1 + db2
            dgc_all = dgc_all + dg1 + dg2
        dba, dal, ddt = vjp_gates((dbeta_all, dgc_all))
        dba_ref[...] = dba.astype(BF16)
        dal_ref[...] += dal
        ddt_ref[...] += ddt

    vec = pl.BlockSpec((1, 128), lambda n: (0, 0))
    return pl.pallas_call(
        body, name=name, grid=(N_CHUNK,),
        in_specs=[pl.BlockSpec((GC, W3), lambda n: (rev(n), 0)), pl.BlockSpec((GC, 128), lambda n: (rev(n), COL_BA)),
                  vec, vec, pl.BlockSpec((1, GH, GD, GD), lambda n: (rev(n), 0, 0, 0)),
                  pl.BlockSpec((1, GH, GC, GC), lambda n: (rev(n), 0, 0, 0)),
                  pl.BlockSpec((GC, GH * GD), lambda n: (rev(n), 0)),
                  ANY],
        out_specs=[pl.BlockSpec((GC, W3), lambda n: (rev(n), 0)), pl.BlockSpec((GC, 128), lambda n: (rev(n), COL_BA)), vec, vec],
        out_shape=[jax.ShapeDtypeStruct((T, W3), F32), jax.ShapeDtypeStruct(dproj.shape, dproj.dtype),
                   jax.ShapeDtypeStruct((1, 128), F32), jax.ShapeDtypeStruct((1, 128), F32)],
        scratch_shapes=[pltpu.VMEM((GH, GD, GD), F32)], input_output_aliases={7: 1}, compiler_params=_cp("arbitrary"),
    )(qkv_c, proj, alog, dtb, s_all, t_all, do, dproj)


def _gdn_post(o, z, nw):
    return o * lax.rsqrt(jnp.mean(o * o, axis=-1, keepdims=True) + EPS) * nw * _silu(z)


def _gdn_post_fwd(o_raw, proj, nw, mixed, *, name, tm=512):
    def body(o_ref, z_ref, w_ref, _, out_ref):
        out_ref[...] = _gdn_post(o_ref[...], z_ref[...], w_ref[...]).astype(BF16)

    return pl.pallas_call(
        body, name=name, grid=(T // tm, GH),
        in_specs=[pl.BlockSpec((tm, GD), lambda i, h: (i, h)), pl.BlockSpec((tm, GD), lambda i, h: (i, COL_Z + h)),
                  pl.BlockSpec((1, GD), lambda i, h: (0, 0)), ANY],
        out_specs=pl.BlockSpec((tm, GD), lambda i, h: (i, HQ * DH // GD + h)),
        out_shape=jax.ShapeDtypeStruct(mixed.shape, mixed.dtype), input_output_aliases={3: 0},
        compiler_params=_cp("parallel", "parallel"),
    )(o_raw, proj, nw, mixed)


def _gdn_post_bwd(o_raw, proj, nw, dmixed, *, name, tm=512):
    def body(o_ref, z_ref, w_ref, d_ref, do_ref, dz_ref, dw_ref):
        _, vjp = jax.vjp(_gdn_post, o_ref[...], z_ref[...], w_ref[...])
        do, dz, dw = vjp(d_ref[...])
        do_ref[...] = do
        dz_ref[...] = dz.astype(BF16)

        @pl.when((pl.program_id(0) == 0) & (pl.program_id(1) == 0))
        def _():
            dw_ref[...] = jnp.zeros_like(dw_ref)

        dw_ref[...] += dw

    blk = pl.BlockSpec((tm, GD), lambda i, h: (i, h))
    vec = pl.BlockSpec((1, GD), lambda i, h: (0, 0))
    return pl.pallas_call(
        body, name=name, grid=(T // tm, GH),
        in_specs=[blk, pl.BlockSpec((tm, GD), lambda i, h: (i, COL_Z + h)), vec,
                  pl.BlockSpec((tm, GD), lambda i, h: (i, GH + h))],
        out_specs=[blk, pl.BlockSpec((tm, GD), lambda i, h: (i, COL_Z + h)), vec],
        out_shape=[jax.ShapeDtypeStruct((T, GH * GD), F32), jax.ShapeDtypeStruct((T, N_PROJ), BF16),
                   jax.ShapeDtypeStruct((1, GD), F32)],
        compiler_params=_cp("arbitrary", "arbitrary"),
    )(o_raw, proj, nw, dmixed)


def _adamw(w, g, m, v, *, name):
    shape = w.shape
    cols = shape[-1]
    w2, g2, m2, v2 = (a.reshape(-1, cols) for a in (w, g, m, v))
    rows = w2.shape[0]
    tr = next((t for t in (512, 256, 128, 64, 32, 16, 8) if rows % t == 0), rows)

    def body(w_ref, g_ref, m_ref, v_ref, d_ref, nm_ref, nv_ref):
        gv = g_ref[...]
        nm = B1 * m_ref[...] + (1.0 - B1) * gv
        nv = B2 * v_ref[...] + (1.0 - B2) * jnp.square(gv)
        m_hat = nm / (1.0 - B1 ** STEP)
        v_hat = nv / (1.0 - B2 ** STEP)
        d_ref[...] = -LR * (m_hat / (jnp.sqrt(v_hat) + AEPS) + WD * w_ref[...])
        nm_ref[...] = nm
        nv_ref[...] = nv

    blk = pl.BlockSpec((tr, cols), lambda i: (i, 0))
    out = pl.pallas_call(
        body, name=name, grid=(rows // tr,), in_specs=[blk] * 4, out_specs=[blk] * 3,
        out_shape=[jax.ShapeDtypeStruct((rows, cols), F32)] * 3, compiler_params=_cp("parallel"),
    )(w2, g2, m2, v2)
    return tuple(o.reshape(shape) for o in out)


def _layer_weights(w_in, w_out, w_ffn, w_down):
    w_all = jnp.pad(w_in, ((0, 0), (0, N_PROJ - w_in.shape[1])))
    return dict(all=w_all, out=w_out, ffn=w_ffn, down=w_down)


def _layer_params(attn_norm, sinks, gcw, a_log, dt_bias, gnw, ffn_norm, fcw, fcb):
    lanes4 = lambda v: jnp.pad(v, (GH, 128 - 2 * GH))[None]
    return dict(attn_norm=attn_norm[None], sinks=jnp.pad(sinks, (0, 128 - HQ))[None], gcw=gcw, alog=lanes4(a_log),
                dtb=lanes4(dt_bias), gnw=gnw[None], ffn_norm=ffn_norm[None], fcw=fcw, fcb=fcb[None])


def _layer_fwd(x, W, P, l):
    n = lambda s: f"l{l}_{s}"
    h = _rmsnorm_fwd(x, P["attn_norm"], name=n("norm1"))
    proj = _mm(h, W["all"], name=n("proj"), mode="nn", dims=(T, N_PROJ, D), tm=512, tn=N_PROJ, tk=D)
    mixed = _swa_fwd(proj, P["sinks"], name=n("swa"))
    qkv_c = _gdn_prep_fwd(proj, P["gcw"], name=n("gdn_prep"))
    o_raw, s_all, t_all = _gdn_chunk_fwd(qkv_c, proj, P["alog"], P["dtb"], name=n("gdn_chunk"))
    mixed = _gdn_post_fwd(o_raw, proj, P["gnw"], mixed, name=n("gdn_post"))
    x1 = _mm(mixed, W["out"], res=x, name=n("out_proj"), mode="nn", dims=(T, D, D), tm=512, tn=D, tk=D)
    h2 = _rmsnorm_fwd(x1, P["ffn_norm"], name=n("norm2"))
    gu = _mm(h2, W["ffn"], name=n("ffn_in"), mode="nn", dims=(T, 2 * DFF, D), tm=512, tn=FFN_CW, tk=D,
             b_spec=pl.BlockSpec((None, D, FFN_CW), lambda i, j, k: (j, k, 0)))
    act = _ffn_act_fwd(gu, P["fcw"], P["fcb"], name=n("ffn_act"))
    x2 = _mm(act, W["down"], res=x1, name=n("ffn_down"), mode="nn", dims=(T, D, DFF), tm=512, tn=D, tk=DFF)
    saved = dict(x=x, h=h, proj=proj, qkv_c=qkv_c, o_raw=o_raw, s_all=s_all, t_all=t_all, mixed=mixed, x1=x1, h2=h2,
                 gu=gu, act=act)
    return x2, saved


def _layer_bwd(dx2, sv, W, P, l):
    n = lambda s: f"l{l}_{s}"
    CW = FFN_CW
    dact =_mm(dx2, W["down"], name=n("d_act"), mode="nt", dims=(T, DFF, D), tm=512, tn=DFF // 2, tk=D, out_dtype=BF16)
    g_down = _mm(sv["act"], dx2, name=n("g_down"), mode="tn", dims=(DFF, D, T), tm=DFF // 2, tn=D, tk=1024, out_dtype=BF16)
    dgu, g_fcw, g_fcb = _ffn_act_bwd(sv["gu"], P["fcw"], P["fcb"], dact, name=n("d_ffn_act"))
    dh2 = _mm(dgu, W["ffn"], name=n("d_h2"), mode="nt", dims=(T, D, 2 * DFF), tm=512, tn=D, tk=CW,
              a_spec=pl.BlockSpec((None, 512, CW), lambda i, j, k: (k // 2, i, k % 2)),
              b_spec=pl.BlockSpec((None, D, CW), lambda i, j, k: (k, j, 0)))
    g_ffn = _mm(sv["h2"], dgu, name=n("g_ffn"), mode="tn", dims=(D, 2 * DFF, T), tm=D, tn=CW, tk=1024,
                b_spec=pl.BlockSpec((None, 1024, CW), lambda i, j, k: (j // 2, k, j % 2)),
                out_spec=pl.BlockSpec((None, D, CW), lambda i, j, k: (j, i, 0)),
                out_shape=jax.ShapeDtypeStruct((N_CHIP, D, CW), BF16), out_dtype=BF16)
    dx1, g_ffn_norm = _rmsnorm_bwd(sv["x1"], P["ffn_norm"], dh2, dx2, name=n("d_norm2"))
    dmixed = _mm(dx1, W["out"], name=n("d_mixed"), mode="nt", dims=(T, D, D), tm=512, tn=D, tk=D)
    g_out = _mm(sv["mixed"], dx1, name=n("g_out"), mode="tn", dims=(D, D, T), tm=D, tn=D, tk=1024, out_dtype=BF16)
    do_raw, dproj, g_gnw = _gdn_post_bwd(sv["o_raw"], sv["proj"], P["gnw"], dmixed, name=n("d_gdn_post"))
    dqkv_c, dproj, g_alog, g_dtb = _gdn_chunk_bwd(sv["qkv_c"], sv["proj"], P["alog"], P["dtb"], sv["s_all"], sv["t_all"],
                                                   do_raw, dproj, name=n("d_gdn_chunk"))
    dproj, g_gcw = _gdn_prep_bwd(sv["proj"], P["gcw"], dqkv_c, dproj, name=n("d_gdn_prep"))
    dproj, g_sinks = _swa_bwd(sv["proj"], P["sinks"], dmixed, dproj, name=n("d_swa"))
    dh = _mm(dproj, W["all"], name=n("d_h"), mode="nt", dims=(T, D, N_PROJ), tm=512, tn=D, tk=N_PROJ)
    g_all = _mm(sv["h"], dproj, name=n("g_in"), mode="tn", dims=(D, N_PROJ, T), tm=512, tn=N_PROJ, tk=1024, out_dtype=BF16)
    dx, g_attn_norm = _rmsnorm_bwd(sv["x"], P["attn_norm"], dh, dx1, name=n("d_norm1"))
    grads = dict(
        attn_norm=g_attn_norm[0], w_in=g_all, attn_sinks=g_sinks[0, :HQ], gdn_conv_w=g_gcw,
        gdn_a_log=g_alog[0, GH:2 * GH], gdn_dt_bias=g_dtb[0, GH:2 * GH], gdn_norm=g_gnw[0], w_out=g_out,
        ffn_norm=g_ffn_norm[0], w_ffn_in=g_ffn, ffn_conv_w=g_fcw, ffn_conv_b=g_fcb[0], w_down=g_down)
    return dx, grads


def _pos():
    return lax.axis_index("x"), lax.axis_index("y"), lax.axis_index("c")


def _other_chips(x, y):
    return [(1 - x, y), (x, 1 - y), (1 - x, 1 - y)]


def _remote(src, dst, send_sems, recv_sems, k, to):
    return pltpu.make_async_remote_copy(src_ref=src, dst_ref=dst, send_sem=send_sems.at[k], recv_sem=recv_sems.at[k],
                                        device_id=to, device_id_type=MESH)


def _ag_weights(shards, *, name):
    nt = len(shards)

    def body(*refs):
        srcs, outs = refs[:nt], refs[nt:2 * nt]
        send_sems, recv_sems = refs[2 * nt:]
        x, y, c = _pos()
        me, sibling, chip = (x, y, c), (x, y, 1 - c), 2 * x + y
        others = [(k, px, py, 2 * px + py) for k, (px, py) in enumerate(_other_chips(x, y))]
        first = [_remote(srcs[t].at[c], outs[t].at[c, chip], send_sems, recv_sems, 3 * t + k, (px, py, c))
                 for t in range(nt) for k, px, py, _ in others]
        for cp in first:
            cp.start()
        passed = []
        for t in range(nt):
            for k, _, _, j in others:
                slot = outs[t].at[c, j]
                _remote(slot, slot, send_sems, recv_sems, 3 * t + k, me).wait_recv()
                passed.append(_remote(slot, slot, send_sems, recv_sems, 3 * (nt + t) + k, sibling))
                passed[-1].start()
        for t in range(nt):
            for k, _, _, j in others:
                slot = outs[t].at[1 - c, j]
                _remote(slot, slot, send_sems, recv_sems, 3 * (nt + t) + k, me).wait_recv()
        for cp in first + passed:
            cp.wait_send()

    outs = pl.pallas_call(
        body, name=name, in_specs=[ANY] * nt, out_specs=[ANY] * nt,
        out_shape=[jax.ShapeDtypeStruct((DEPTH, N_CHIP) + s.shape[1:], s.dtype) for s in shards],
        scratch_shapes=[pltpu.SemaphoreType.DMA((6 * nt,)), pltpu.SemaphoreType.DMA((6 * nt,))],
    )(*shards)
    chip = 2 * lax.axis_index("x") + lax.axis_index("y")
    return [lax.dynamic_update_index_in_dim(o, s, chip, 1) for o, s in zip(outs, shards)]


def _ag_small(v, *, name):
    m, n = v.shape

    def body(x_ref, out_ref, red_ref, send_sems, recv_sems, local_sem):
        x, y, c = _pos()
        me, sibling = (x, y, c), (x, y, 1 - c)
        chips = _other_chips(x, y)
        rows = lambda px, py, pc: out_ref.at[pl.ds(pl.multiple_of((4 * px + 2 * py + pc) * m, 8), m), :]
        mine = pltpu.make_async_copy(x_ref, rows(*me), local_sem)
        mine.start()
        first = [_remote(x_ref, rows(*me), send_sems, recv_sems, 0, sibling)]
        first += [_remote(x_ref, rows(*me), send_sems, recv_sems, 1 + k, (*chip, c)) for k, chip in enumerate(chips)]
        for cp in first:
            cp.start()
        passed = [_remote(rows(*chip, c), rows(*chip, c), send_sems, recv_sems, 4 + k, sibling) for k, chip in enumerate(chips)]
        for k, chip in enumerate(chips):
            _remote(rows(*chip, c), rows(*chip, c), send_sems, recv_sems, 1 + k, me).wait_recv()
            passed[k].start()
        _remote(rows(*sibling), rows(*sibling), send_sems, recv_sems, 0, me).wait_recv()
        for k, chip in enumerate(chips):
            _remote(rows(*chip, 1 - c), rows(*chip, 1 - c), send_sems, recv_sems, 4 + k, me).wait_recv()
        for cp in first + passed:
            cp.wait_send()
        mine.wait()
        acc = out_ref[0:m, :]
        for d in range(1, N_DEV):
            acc = acc + out_ref[d * m:(d + 1) * m, :]
        red_ref[...] = acc

    return pl.pallas_call(
        body, name=name, in_specs=[VMEM_SPEC], out_specs=[VMEM_SPEC, VMEM_SPEC],
        out_shape=[jax.ShapeDtypeStruct((N_DEV * m, n), v.dtype), jax.ShapeDtypeStruct((m, n), v.dtype)],
        scratch_shapes=[pltpu.SemaphoreType.DMA((7,)), pltpu.SemaphoreType.DMA((7,)), pltpu.SemaphoreType.DMA],
    )(v)


def _halves(ref, c):
    rh = ref.shape[1] // 2
    return ref.at[:, pl.ds(pl.multiple_of(c * rh, 16), rh), :]


def _rs_swap(gs, *, name):
    nt = len(gs)

    def body(*refs):
        g, theirs = refs[:nt], refs[nt:2 * nt]
        send_sems, recv_sems = refs[2 * nt:]
        x, y, c = _pos()
        swaps = [_remote(_halves(g[t], 1 - c), theirs[t], send_sems, recv_sems, t, (x, y, 1 - c)) for t in range(nt)]
        for cp in swaps:
            cp.start()
        for cp in swaps:
            cp.wait()

    return pl.pallas_call(
        body, name=name, in_specs=[ANY] * nt, out_specs=[ANY] * nt,
        out_shape=[jax.ShapeDtypeStruct((a.shape[0], a.shape[1] // 2, a.shape[2]), a.dtype) for a in gs],
        scratch_shapes=[pltpu.SemaphoreType.DMA((nt,)), pltpu.SemaphoreType.DMA((nt,))],
    )(*gs)


def _rs_exchange(ss, *, name):
    nt = len(ss)

    def body(*refs):
        s, out = refs[:nt], refs[nt:2 * nt]
        send_sems, recv_sems = refs[2 * nt:]
        x, y, c = _pos()
        chip = 2 * x + y
        others = [(k, px, py, 2 * px + py) for k, (px, py) in enumerate(_other_chips(x, y))]
        sends = [_remote(s[t].at[j], out[t].at[chip], send_sems, recv_sems, 3 * t + k, (px, py, c))
                 for t in range(nt) for k, px, py, j in others]
        for cp in sends:
            cp.start()
        for t in range(nt):
            for k, _, _, j in others:
                _remote(s[t].at[j], out[t].at[j], send_sems, recv_sems, 3 * t + k, (x, y, c)).wait_recv()
        for cp in sends:
            cp.wait_send()

    outs = pl.pallas_call(
        body, name=name, in_specs=[ANY] * nt, out_specs=[ANY] * nt,
        out_shape=[jax.ShapeDtypeStruct(a.shape, a.dtype) for a in ss],
        scratch_shapes=[pltpu.SemaphoreType.DMA((3 * nt,)), pltpu.SemaphoreType.DMA((3 * nt,))],
    )(*ss)
    chip = 2 * lax.axis_index("x") + lax.axis_index("y")
    return [lax.dynamic_update_index_in_dim(o, lax.dynamic_index_in_dim(s, chip, 0, keepdims=False), chip, 0)
            for o, s in zip(outs, ss)]


def _rs_join(rs, *, name):
    nt = len(rs)

    def body(*refs):
        r, theirs = refs[:nt], refs[nt:2 * nt]
        send_sems, recv_sems = refs[2 * nt:]
        x, y, c = _pos()
        swaps = [_remote(r[t], theirs[t], send_sems, recv_sems, t, (x, y, 1 - c)) for t in range(nt)]
        for cp in swaps:
            cp.start()
        for cp in swaps:
            cp.wait()

    theirs = pl.pallas_call(
        body, name=name, in_specs=[ANY] * nt, out_specs=[ANY] * nt,
        out_shape=[jax.ShapeDtypeStruct(a.shape, a.dtype) for a in rs],
        scratch_shapes=[pltpu.SemaphoreType.DMA((nt,)), pltpu.SemaphoreType.DMA((nt,))],
    )(*rs)
    first = lax.axis_index("c") == 0
    return [jnp.concatenate([jnp.where(first, a, b), jnp.where(first, b, a)], axis=0) for a, b in zip(rs, theirs)]


def _row_tile(rows, dtype):
    unit = 16 if dtype == BF16 else 8
    return next((t for t in (512, 256, 128, 64, 32, 16, 8) if rows % t == 0 and t % unit == 0), rows)


def _add_pair(a, b, *, name):
    n, rh, cols = a.shape
    tr = _row_tile(rh, BF16)

    def body(a_ref, b_ref, o_ref):
        o_ref[...] = (a_ref[...].astype(F32) + b_ref[...].astype(F32)).astype(BF16)

    blk = pl.BlockSpec((None, tr, cols), lambda j, i: (j, i, 0))
    return pl.pallas_call(
        body, name=name, grid=(n, rh // tr), in_specs=[blk, blk], out_specs=blk,
        out_shape=jax.ShapeDtypeStruct(a.shape, BF16), compiler_params=_cp("parallel", "parallel"),
    )(a, b)


def _sum_chips(b, *, name):
    n, rh, cols = b.shape
    tr = _row_tile(rh, BF16)

    def body(b_ref, o_ref):
        acc = b_ref[0].astype(F32)
        for j in range(1, n):
            acc = acc + b_ref[j].astype(F32)
        o_ref[...] = acc

    return pl.pallas_call(
        body, name=name, grid=(rh // tr,), in_specs=[pl.BlockSpec((n, tr, cols), lambda i: (0, i, 0))],
        out_specs=pl.BlockSpec((tr, cols), lambda i: (i, 0)),
        out_shape=jax.ShapeDtypeStruct((rh, cols), F32), compiler_params=_cp("parallel"),
    )(b)


BIG = ("w_in", "w_out", "w_ffn_in", "w_down")
W_IN_SHARD = (N_MAIN + 2 * GH) // N_CHIP


def _reduce_scatter_layer(lg, l):
    g_in = jnp.stack([lg["w_in"][:, j * W_IN_SHARD:(j + 1) * W_IN_SHARD] for j in range(N_CHIP)])
    gs = [g_in, lg["w_out"].reshape(N_CHIP, D // N_CHIP, D), lg["w_ffn_in"], lg["w_down"].reshape(N_CHIP, DFF // N_CHIP, D)]
    c = lax.axis_index("c")
    mine = [lax.dynamic_slice_in_dim(a, c * (a.shape[1] // 2), a.shape[1] // 2, axis=1) for a in gs]
    theirs = _rs_swap(gs, name=f"rs{l}_swap")
    pairs = [_add_pair(a, b, name=f"rs{l}_add_{n}") for a, b, n in zip(mine, theirs, BIG)]
    got = _rs_exchange(pairs, name=f"rs{l}_exchange")
    return _rs_join([_sum_chips(b, name=f"rs{l}_sum_{n}") for b, n in zip(got, BIG)], name=f"rs{l}_join")


def _size(shape):
    n = 1
    for s in shape:
        n *= s
    return n


def _pack_flat(parts, total, dtype):
    flat = jnp.concatenate([p.reshape(-1).astype(dtype) for p in parts])
    return jnp.pad(flat, (0, total - flat.shape[0]))


def _unpack_flat(flat, shapes):
    out, o = [], 0
    for s in shapes:
        out.append(flat[o:o + _size(s)].reshape(s))
        o += _size(s)
    return out


WEIGHTS = ("attn_norm", "w_in", "attn_sinks", "gdn_conv_w", "gdn_a_log", "gdn_dt_bias", "gdn_norm", "w_out", "ffn_norm",
           "w_ffn_in", "ffn_conv_w", "ffn_conv_b", "w_down", "final_norm")
SMALL = {"attn_norm": (DEPTH, D), "attn_sinks": (DEPTH, HQ), "gdn_a_log": (DEPTH, GH), "gdn_dt_bias": (DEPTH, GH),
         "gdn_norm": (DEPTH, GD), "ffn_norm": (DEPTH, D), "ffn_conv_b": (DEPTH, DFF), "final_norm": (D,)}
CONV_FULL = {"gdn_conv_w": (DEPTH, GK, 3 * GH * GD), "ffn_conv_w": (DEPTH, FK, DFF)}
CONV_SHARD = {"gdn_conv_w": (DEPTH, GK, 3 * GH * GD // N_CHIP), "ffn_conv_w": (DEPTH, FK, DFF // N_CHIP)}
CONV_ROWS, SMALLG_ROWS, SMALLW_ROWS = 64, 320, 144


def kernel(x, attn_norm, w_in, attn_sinks, gdn_conv_w, gdn_a_log, gdn_dt_bias, gdn_norm, w_out, ffn_norm, w_ffn_in, ffn_conv_w, ffn_conv_b, w_down, final_norm, loss_target, m_attn_norm, m_w_in, m_attn_sinks, m_gdn_conv_w, m_gdn_a_log, m_gdn_dt_bias, m_gdn_norm, m_w_out, m_ffn_norm, m_w_ffn_in, m_ffn_conv_w, m_ffn_conv_b, m_w_down, m_final_norm, v_attn_norm, v_w_in, v_attn_sinks, v_gdn_conv_w, v_gdn_a_log, v_gdn_dt_bias, v_gdn_norm, v_w_out, v_ffn_norm, v_w_ffn_in, v_ffn_conv_w, v_ffn_conv_b, v_w_down, v_final_norm):
    w = dict(zip(WEIGHTS, (attn_norm, w_in, attn_sinks, gdn_conv_w, gdn_a_log, gdn_dt_bias, gdn_norm, w_out, ffn_norm,
                           w_ffn_in, ffn_conv_w, ffn_conv_b, w_down, final_norm)))
    m = dict(zip(WEIGHTS, (m_attn_norm, m_w_in, m_attn_sinks, m_gdn_conv_w, m_gdn_a_log, m_gdn_dt_bias, m_gdn_norm, m_w_out,
                           m_ffn_norm, m_w_ffn_in, m_ffn_conv_w, m_ffn_conv_b, m_w_down, m_final_norm)))
    v = dict(zip(WEIGHTS, (v_attn_norm, v_w_in, v_attn_sinks, v_gdn_conv_w, v_gdn_a_log, v_gdn_dt_bias, v_gdn_norm, v_w_out,
                           v_ffn_norm, v_w_ffn_in, v_ffn_conv_w, v_ffn_conv_b, v_w_down, v_final_norm)))
    cx, cy, _ = _pos()
    chip = 2 * cx + cy

    gath = dict(zip(BIG, _ag_weights([w[n].astype(BF16) for n in BIG], name="gather_weights")))
    full = {"w_in": jnp.concatenate([gath["w_in"][:, j] for j in range(N_CHIP)], axis=2),
            "w_out": gath["w_out"].reshape(DEPTH, D, D), "w_ffn_in": gath["w_ffn_in"],
            "w_down": gath["w_down"].reshape(DEPTH, DFF, D)}
    cpack = _pack_flat([w[n] for n in CONV_SHARD], CONV_ROWS * 128, F32).reshape(CONV_ROWS, 128)
    cgath, _ = _ag_small(cpack, name="gather_conv_w")
    cgath = cgath.reshape(N_DEV, CONV_ROWS * 128)
    cshards = [_unpack_flat(cgath[2 * j], list(CONV_SHARD.values())) for j in range(N_CHIP)]
    conv = {n: jnp.concatenate([cshards[j][i] for j in range(N_CHIP)], axis=2) for i, n in enumerate(CONV_SHARD)}

    Ws = [_layer_weights(full["w_in"][l], full["w_out"][l], full["w_ffn_in"][l], full["w_down"][l]) for l in range(DEPTH)]
    Ps = [_layer_params(attn_norm[l], attn_sinks[l], conv["gdn_conv_w"][l], gdn_a_log[l], gdn_dt_bias[l], gdn_norm[l],
                        ffn_norm[l], conv["ffn_conv_w"][l], ffn_conv_b[l]) for l in range(DEPTH)]

    h, saved = x[0], []
    for l in range(DEPTH):
        h, sv = _layer_fwd(h, Ws[l], Ps[l], l)
        saved.append(sv)
    dx, g_final, loss_part = _loss_head(h, final_norm[None], loss_target[0], name="loss_head")
    lg, reduced = [None] * DEPTH, [None] * DEPTH
    for l in reversed(range(DEPTH)):
        dx, lg[l] = _layer_bwd(dx, saved[l], Ws[l], Ps[l], l)
        reduced[l] = _reduce_scatter_layer(lg[l], l)
    grad_x = dx[None]
    stacked = lambda n: jnp.stack([lg[l][n] for l in range(DEPTH)])

    small_parts = [g_final[0] if n == "final_norm" else stacked(n) for n in SMALL] + [stacked(n) for n in CONV_FULL]
    svec = _pack_flat(small_parts + [loss_part[0, :1]], SMALLG_ROWS * 128, F32).reshape(SMALLG_ROWS, 128)
    _, sred = _ag_small(svec, name="reduce_small")
    small_g = _unpack_flat(sred.reshape(-1), list(SMALL.values()) + list(CONV_FULL.values()) + [(1,)])
    g = dict(zip(list(SMALL) + list(CONV_FULL), small_g[:-1]))
    loss = small_g[-1][0]
    for n in CONV_FULL:
        wd = CONV_SHARD[n][2]
        g[n] = lax.dynamic_slice_in_dim(g[n], chip * wd, wd, axis=2)

    g.update({n: jnp.stack([reduced[l][i] for l in range(DEPTH)]) for i, n in enumerate(BIG)})

    delta, new_m, new_v = {}, {}, {}
    for n in BIG:
        delta[n], new_m[n], new_v[n] = _adamw(w[n], g[n], m[n], v[n], name=f"adamw_{n}")
    small_names = list(SMALL) + list(CONV_SHARD)
    small_shapes = list(SMALL.values()) + list(CONV_SHARD.values())
    packed = [_pack_flat([t[n] for n in small_names], SMALLW_ROWS * 128, F32).reshape(SMALLW_ROWS, 128) for t in (w, g, m, v)]
    for res, out in zip(_adamw(*packed, name="adamw_small"), (delta, new_m, new_v)):
        out.update(zip(small_names, _unpack_flat(res.reshape(-1), small_shapes)))

    return (loss, grad_x, *[g[n] for n in WEIGHTS], *[delta[n] for n in WEIGHTS], *[new_m[n] for n in WEIGHTS],
            *[new_v[n] for n in WEIGHTS])
```

```python
import functools

import jax
import jax.numpy as jnp
import numpy as np
from jax import lax
from jax.experimental import pallas as pl
from jax.experimental.pallas import tpu as pltpu

F32, BF16 = jnp.float32, jnp.bfloat16
HIGHEST = lax.Precision.HIGHEST
MESH = pl.DeviceIdType.MESH

D = 1024
T = 4096
DEPTH = 2
HQ, HKV, DH, WIN = 8, 2, 64, 128
GH, GD, GC, GK = 4, 128, 64, 4
DFF, FK = 2816, 3
EPS = 1e-6
N_MAIN = 2816
N_PROJ = N_MAIN + 128
COL_K, COL_V, COL_G, COL_Z, COL_BA = 4, 5, 6, 18, 22
N_CHUNK = T // GC
N_DEV, N_CHIP = 8, 4
FFN_CW = 2 * DFF // N_CHIP
LR, B1, B2, AEPS, WD, STEP = 0.001, 0.9, 0.999, 1e-08, 0.01, 10

VMEM_LIMIT = 56 * 1024 * 1024
ANY = pl.BlockSpec(memory_space=pl.ANY)
VMEM_SPEC = pl.BlockSpec(memory_space=pltpu.VMEM)


def _cp(*sem):
    return pltpu.CompilerParams(dimension_semantics=sem if sem else None, vmem_limit_bytes=VMEM_LIMIT)


def _dot(a, b):
    return jnp.dot(a.astype(BF16), b.astype(BF16), preferred_element_type=F32)


def _dot_nt(a, b):
    return lax.dot_general(a.astype(BF16), b.astype(BF16), (((1,), (1,)), ((), ())), preferred_element_type=F32)


def _dot_tn(a, b):
    return lax.dot_general(a.astype(BF16), b.astype(BF16), (((0,), (0,)), ((), ())), preferred_element_type=F32)


def _iota(shape, dim):
    return lax.broadcasted_iota(jnp.int32, shape, dim)


def _col(x, idx):
    return jnp.sum(jnp.where(_iota(x.shape, 1) == idx, x, 0.0), axis=1, keepdims=True)


def _silu(y):
    return y * jax.nn.sigmoid(y)


_MM_DN = {"nn": (((1,), (0,)), ((), ())), "nt": (((1,), (1,)), ((), ())), "tn": (((0,), (0,)), ((), ()))}


def _mm(a, b, *, name, mode, dims, tm, tn, tk, res=None, out_dtype=F32, a_spec=None, b_spec=None, out_spec=None,
        out_shape=None):
    M, N, K = dims
    assert M % tm == 0 and N % tn == 0 and K % tk == 0, (name, dims)
    nk = K // tk
    dn = _MM_DN[mode]

    def body(*refs):
        if res is None:
            a_ref, b_ref, o_ref, acc = refs
        else:
            a_ref, b_ref, r_ref, o_ref, acc = refs
        k = pl.program_id(2)

        @pl.when(k == 0)
        def _():
            acc[...] = jnp.zeros_like(acc)

        acc[...] += lax.dot_general(a_ref[...].astype(BF16), b_ref[...].astype(BF16), dn, preferred_element_type=F32)

        @pl.when(k == nk - 1)
        def _():
            r = acc[...]
            if res is not None:
                r = r + r_ref[...]
            o_ref[...] = r.astype(out_dtype)

    if a_spec is None:
        a_spec = pl.BlockSpec((tk, tm), lambda i, j, k: (k, i)) if mode == "tn" else pl.BlockSpec((tm, tk), lambda i, j, k: (i, k))
    if b_spec is None:
        b_spec = pl.BlockSpec((tn, tk), lambda i, j, k: (j, k)) if mode == "nt" else pl.BlockSpec((tk, tn), lambda i, j, k: (k, j))
    in_specs, args = [a_spec, b_spec], [a, b]
    if res is not None:
        in_specs.append(pl.BlockSpec((tm, tn), lambda i, j, k: (i, j)))
        args.append(res)
    return pl.pallas_call(
        body, name=name, grid=(M // tm, N // tn, nk), in_specs=in_specs,
        out_specs=out_spec or pl.BlockSpec((tm, tn), lambda i, j, k: (i, j)),
        out_shape=out_shape or jax.ShapeDtypeStruct((M, N), out_dtype),
        scratch_shapes=[pltpu.VMEM((tm, tn), F32)],
        compiler_params=_cp("parallel", "parallel", "arbitrary"),
    )(*args)


def _rms(x, w):
    return x * lax.rsqrt(jnp.mean(x * x, axis=-1, keepdims=True) + EPS) * w


def _rmsnorm_fwd(x, w, *, name, tm=512):
    def body(x_ref, w_ref, o_ref):
        o_ref[...] = _rms(x_ref[...], w_ref[...]).astype(BF16)

    return pl.pallas_call(
        body, name=name, grid=(x.shape[0] // tm,),
        in_specs=[pl.BlockSpec((tm, D), lambda i: (i, 0)), pl.BlockSpec((1, D), lambda i: (0, 0))],
        out_specs=pl.BlockSpec((tm, D), lambda i: (i, 0)),
        out_shape=jax.ShapeDtypeStruct(x.shape, BF16), compiler_params=_cp("parallel"),
    )(x, w)


def _rmsnorm_bwd(x, w, dh, dres, *, name, tm=512):
    def body(x_ref, w_ref, dh_ref, dr_ref, dx_ref, dw_ref):
        _, vjp = jax.vjp(_rms, x_ref[...], w_ref[...])
        dx, dw = vjp(dh_ref[...])
        dx_ref[...] = dx + dr_ref[...]

        @pl.when(pl.program_id(0) == 0)
        def _():
            dw_ref[...] = jnp.zeros_like(dw_ref)

        dw_ref[...] += dw

    row = pl.BlockSpec((tm, D), lambda i: (i, 0))
    vec = pl.BlockSpec((1, D), lambda i: (0, 0))
    return pl.pallas_call(
        body, name=name, grid=(x.shape[0] // tm,), in_specs=[row, vec, row, row], out_specs=[row, vec],
        out_shape=[jax.ShapeDtypeStruct(x.shape, F32), jax.ShapeDtypeStruct((1, D), F32)],
        compiler_params=_cp("arbitrary"),
    )(x, w, dh, dres)


def _loss_head(x, w, tgt, *, name, tm=512):
    def f(xv, wv, tv):
        err = _rms(xv, wv) - tv
        per_row = jnp.sum(err * err, axis=1, keepdims=True) * (0.5 / D)
        return jnp.sum(per_row, axis=0, keepdims=True)

    def body(x_ref, w_ref, t_ref, dx_ref, dw_ref, loss_ref):
        tv = t_ref[...]
        loss, vjp = jax.vjp(lambda xv, wv: f(xv, wv, tv), x_ref[...], w_ref[...])
        dx, dw = vjp(jnp.ones((1, 1), F32))
        dx_ref[...] = dx

        @pl.when(pl.program_id(0) == 0)
        def _():
            dw_ref[...] = jnp.zeros_like(dw_ref)
            loss_ref[...] = jnp.zeros_like(loss_ref)

        dw_ref[...] += dw
        loss_ref[...] += jnp.broadcast_to(loss, loss_ref.shape)

    row = pl.BlockSpec((tm, D), lambda i: (i, 0))
    vec = pl.BlockSpec((1, D), lambda i: (0, 0))
    return pl.pallas_call(
        body, name=name, grid=(x.shape[0] // tm,), in_specs=[row, vec, row],
        out_specs=[row, vec, pl.BlockSpec((1, 128), lambda i: (0, 0))],
        out_shape=[jax.ShapeDtypeStruct(x.shape, F32), jax.ShapeDtypeStruct((1, D), F32),
                   jax.ShapeDtypeStruct((1, 128), F32)],
        compiler_params=_cp("arbitrary"),
    )(x, w, tgt)


def _swa_bias():
    G = HQ // HKV
    r, c = np.arange(G * WIN)[:, None], np.arange(2 * WIN)[None, :]
    rel = (r % WIN) + WIN - c
    band = (rel >= 0) & (rel < WIN)
    out = np.empty((2, HKV, G * WIN, 2 * WIN), np.float32)
    for h2 in range(HKV):
        slope = 2.0 ** (-8.0 * (h2 * G + r // WIN + 1) / HQ)
        out[0, h2] = np.where(band & (c >= WIN), -slope * rel, -1e30)
        out[1, h2] = np.where(band, -slope * rel, -1e30)
    return jnp.asarray(out)


def _swa_group(qg, k2, v2, sk, bias, h2):
    s = _dot_nt(qg, k2) * (DH ** -0.5) + bias
    g1 = _iota((s.shape[0], 1), 0) >> 7
    sink = jnp.zeros((s.shape[0], 1), F32)
    for gi in range(HQ // HKV):
        sink = jnp.where(g1 == gi, _col(sk, h2 * (HQ // HKV) + gi), sink)
    m = lax.stop_gradient(jnp.maximum(jnp.max(s, axis=1, keepdims=True), sink))
    p = jnp.exp(s - m)
    den = jnp.sum(p, axis=1, keepdims=True) + jnp.exp(sink - m)
    return _dot(p * (1.0 / den), v2)


def _swa_split_q(q, h2):
    G = HQ // HKV
    return jnp.concatenate([q[:, (h2 * G + g) * DH:(h2 * G + g + 1) * DH] for g in range(G)], axis=0)


def _swa_merge_q(parts):
    G = HQ // HKV
    return jnp.concatenate([parts[h2][g * WIN:(g + 1) * WIN] for h2 in range(HKV) for g in range(G)], axis=1)


def _swa_specs():
    prev = lambda i: jnp.maximum(jnp.minimum(i, T // WIN - 1) - 1, 0)
    cur = lambda i: jnp.minimum(i, T // WIN - 1)
    return [
        pl.BlockSpec((WIN, HQ * DH), lambda i: (cur(i), 0)),
        pl.BlockSpec((WIN, 128), lambda i: (prev(i), COL_K)),
        pl.BlockSpec((WIN, 128), lambda i: (cur(i), COL_K)),
        pl.BlockSpec((WIN, 128), lambda i: (prev(i), COL_V)),
        pl.BlockSpec((WIN, 128), lambda i: (cur(i), COL_V)),
        pl.BlockSpec((1, 128), lambda i: (0, 0)),
        pl.BlockSpec((None, HKV, (HQ // HKV) * WIN, 2 * WIN), lambda i: (jnp.minimum(i, 1), 0, 0, 0)),
    ]


def _swa_fwd(proj, sinks, *, name):
    def body(q_ref, kp_ref, kc_ref, vp_ref, vc_ref, sk_ref, bias_ref, o_ref):
        q, sk = q_ref[...], sk_ref[...]
        outs = []
        for h2 in range(HKV):
            sl = slice(h2 * DH, (h2 + 1) * DH)
            k2 = jnp.concatenate([kp_ref[:, sl], kc_ref[:, sl]], axis=0)
            v2 = jnp.concatenate([vp_ref[:, sl], vc_ref[:, sl]], axis=0)
            outs.append(_swa_group(_swa_split_q(q, h2), k2, v2, sk, bias_ref[h2], h2))
        o_ref[...] = _swa_merge_q(outs).astype(BF16)

    return pl.pallas_call(
        body, name=name, grid=(T // WIN,), in_specs=_swa_specs(),
        out_specs=pl.BlockSpec((WIN, HQ * DH), lambda i: (i, 0)),
        out_shape=jax.ShapeDtypeStruct((T, HQ * DH + GH * GD), BF16), compiler_params=_cp("parallel"),
    )(proj, proj, proj, proj, proj, sinks, _swa_bias())


def _swa_bwd(proj, sinks, do, dproj, *, name):
    NB = T // WIN
    QW = HQ * DH

    def body(q_ref, kp_ref, kc_ref, vp_ref, vc_ref, sk_ref, bias_ref, do_ref, _, out_ref, dsk_ref, cq, ck, cv):
        i = pl.program_id(0)

        @pl.when(i == 0)
        def _():
            cq[...] = jnp.zeros_like(cq)
            ck[...] = jnp.zeros_like(ck)
            cv[...] = jnp.zeros_like(cv)
            dsk_ref[...] = jnp.zeros_like(dsk_ref)

        @pl.when(i < NB)
        def _():
            q, sk, dov = q_ref[...], sk_ref[...], do_ref[...].astype(F32)
            dqs, dkp, dkc, dvp, dvc = [], [], [], [], []
            dsk = jnp.zeros_like(sk)
            for h2 in range(HKV):
                sl = slice(h2 * DH, (h2 + 1) * DH)
                k2 = jnp.concatenate([kp_ref[:, sl], kc_ref[:, sl]], axis=0)
                v2 = jnp.concatenate([vp_ref[:, sl], vc_ref[:, sl]], axis=0)
                _, vjp = jax.vjp(functools.partial(_swa_group, bias=bias_ref[h2], h2=h2), _swa_split_q(q, h2), k2, v2, sk)
                dqg, dk2, dv2, dsk_h = vjp(_swa_split_q(dov, h2))
                dqs.append(dqg)
                dkp.append(dk2[:WIN]); dkc.append(dk2[WIN:])
                dvp.append(dv2[:WIN]); dvc.append(dv2[WIN:])
                dsk = dsk + dsk_h
            out_ref[:, :QW] = cq[...].astype(BF16)
            out_ref[:, QW:QW + 128] = (ck[...] + jnp.concatenate(dkp, axis=1)).astype(BF16)
            out_ref[:, QW + 128:] = (cv[...] + jnp.concatenate(dvp, axis=1)).astype(BF16)
            cq[...] = _swa_merge_q(dqs)
            ck[...] = jnp.concatenate(dkc, axis=1)
            cv[...] = jnp.concatenate(dvc, axis=1)
            dsk_ref[...] += dsk

        @pl.when(i == NB)
        def _():
            out_ref[:, :QW] = cq[...].astype(BF16)
            out_ref[:, QW:QW + 128] = ck[...].astype(BF16)
            out_ref[:, QW + 128:] = cv[...].astype(BF16)

    qblk = pl.BlockSpec((WIN, QW), lambda i: (jnp.minimum(i, NB - 1), 0))
    return pl.pallas_call(
        body, name=name, grid=(NB + 1,), in_specs=_swa_specs() + [qblk, ANY],
        out_specs=[pl.BlockSpec((WIN, QW + 256), lambda i: (jnp.maximum(i - 1, 0), 0)), pl.BlockSpec((1, 128), lambda i: (0, 0))],
        out_shape=[jax.ShapeDtypeStruct(dproj.shape, dproj.dtype), jax.ShapeDtypeStruct((1, 128), F32)],
        scratch_shapes=[pltpu.VMEM((WIN, QW), F32), pltpu.VMEM((WIN, 128), F32), pltpu.VMEM((WIN, 128), F32)],
        input_output_aliases={8: 0}, compiler_params=_cp("arbitrary"),
    )(proj, proj, proj, proj, proj, sinks, _swa_bias(), do, dproj)


RC = 256
HALO = 8


def _load_ext(ref, c):
    nch = T // RC
    r0 = pl.multiple_of(c * RC, RC)
    p0 = pl.multiple_of(jnp.maximum(r0 - HALO, 0), HALO)
    n0 = pl.multiple_of(jnp.minimum(r0 + RC, T - HALO), HALO)
    prev = jnp.where(c > 0, ref[pl.ds(p0, HALO), :].astype(F32), 0.0)
    nxt = jnp.where(c < nch - 1, ref[pl.ds(n0, HALO), :].astype(F32), 0.0)
    return jnp.concatenate([prev, ref[pl.ds(r0, RC), :].astype(F32), nxt], axis=0)


def _conv_ext(xe, w, K):
    y = w[K - 1:K, :] * xe
    for s in range(1, K):
        y = y + w[K - 1 - s:K - s, :] * pltpu.roll(xe, s, 0)
    return y


def _conv_bwd_ext(xe, dye, w, K, dw_ref):
    n = xe.shape[0]
    own = slice(HALO, HALO + RC)
    dx = w[K - 1:K, :] * dye
    dw_ref[K - 1:K, :] += jnp.sum(dye[own] * xe[own], axis=0, keepdims=True)
    for s in range(1, K):
        dx = dx + w[K - 1 - s:K - s, :] * pltpu.roll(dye, n - s, 0)
        dw_ref[K - 1 - s:K - s, :] += jnp.sum(dye[own] * pltpu.roll(xe, s, 0)[own], axis=0, keepdims=True)
    return dx[own]


def _gdn_post_conv(y, is_qk):
    a = _silu(y)
    nrm = a * lax.rsqrt(jnp.sum(a * a, axis=1, keepdims=True) + EPS)
    return jnp.where(is_qk, nrm, a)


def _gdn_prep_fwd(proj, conv_w, *, name):
    nblk = 3 * GH

    def body(x_ref, w_ref, o_ref):
        is_qk = pl.program_id(0) < 2 * GH
        w = w_ref[...]

        def chunk(c, carry):
            y = _conv_ext(_load_ext(x_ref, c), w, GK)[HALO:HALO + RC]
            o_ref[pl.ds(pl.multiple_of(c * RC, RC), RC), :] = _gdn_post_conv(y, is_qk)
            return carry

        lax.fori_loop(0, T // RC, chunk, 0)

    return pl.pallas_call(
        body, name=name, grid=(nblk,),
        in_specs=[pl.BlockSpec((T, 128), lambda j: (0, COL_G + j)), pl.BlockSpec((GK, 128), lambda j: (0, j))],
        out_specs=pl.BlockSpec((T, 128), lambda j: (0, j)),
        out_shape=jax.ShapeDtypeStruct((T, nblk * 128), F32), compiler_params=_cp("parallel"),
    )(proj, conv_w)


def _gdn_prep_bwd(proj, conv_w, dout, dproj, *, name):
    nblk = 3 * GH

    def body(x_ref, w_ref, d_ref, _, dx_ref, dw_ref):
        is_qk = pl.program_id(0) < 2 * GH
        w = w_ref[...]
        dw_ref[...] = jnp.zeros_like(dw_ref)

        def chunk(c, carry):
            xe = _load_ext(x_ref, c)
            _, vjp = jax.vjp(lambda y: _gdn_post_conv(y, is_qk), _conv_ext(xe, w, GK))
            (dye,) = vjp(_load_ext(d_ref, c))
            dx_ref[pl.ds(pl.multiple_of(c * RC, RC), RC), :] = _conv_bwd_ext(xe, dye, w, GK, dw_ref).astype(BF16)
            return carry

        lax.fori_loop(0, T // RC, chunk, 0)

    return pl.pallas_call(
        body, name=name, grid=(nblk,),
        in_specs=[pl.BlockSpec((T, 128), lambda j: (0, COL_G + j)), pl.BlockSpec((GK, 128), lambda j: (0, j)),
                  pl.BlockSpec((T, 128), lambda j: (0, j)), ANY],
        out_specs=[pl.BlockSpec((T, 128), lambda j: (0, COL_G + j)), pl.BlockSpec((GK, 128), lambda j: (0, j))],
        out_shape=[jax.ShapeDtypeStruct(dproj.shape, dproj.dtype), jax.ShapeDtypeStruct((GK, nblk * 128), F32)],
        input_output_aliases={3: 0}, compiler_params=_cp("parallel"),
    )(proj, conv_w, dout, dproj)


def _ffn_post_conv(y, b, up):
    return _silu(y + b) * up


def _ffn_act_fwd(gu, conv_w, conv_b, *, name):
    nblk = DFF // 128

    def body(g_ref, u_ref, w_ref, b_ref, o_ref):
        w, b = w_ref[...], b_ref[...]

        def chunk(c, carry):
            rows = pl.ds(pl.multiple_of(c * RC, RC), RC)
            y = _conv_ext(_load_ext(g_ref, c), w, FK)[HALO:HALO + RC]
            o_ref[rows, :] = _ffn_post_conv(y, b, u_ref[rows, :]).astype(BF16)
            return carry

        lax.fori_loop(0, T // RC, chunk, 0)

    return pl.pallas_call(
        body, name=name, grid=(nblk,),
        in_specs=[pl.BlockSpec((T, 128), lambda j: (0, j)), pl.BlockSpec((T, 128), lambda j: (0, nblk + j)),
                  pl.BlockSpec((FK, 128), lambda j: (0, j)), pl.BlockSpec((1, 128), lambda j: (0, j))],
        out_specs=pl.BlockSpec((T, 128), lambda j: (0, j)),
        out_shape=jax.ShapeDtypeStruct((T, DFF), BF16), compiler_params=_cp("parallel"),
    )(gu, gu, conv_w, conv_b)


def _ffn_act_bwd(gu, conv_w, conv_b, dact, *, name):
    nblk = DFF // 128

    def body(g_ref, u_ref, w_ref, b_ref, d_ref, dgu_ref, dw_ref, db_ref):
        dg_ref, du_ref = dgu_ref.at[0], dgu_ref.at[1]
        w, b = w_ref[...], b_ref[...]
        dw_ref[...] = jnp.zeros_like(dw_ref)
        db_ref[...] = jnp.zeros_like(db_ref)

        def chunk(c, carry):
            rows = pl.ds(pl.multiple_of(c * RC, RC), RC)
            xe = _load_ext(g_ref, c)
            ue = _load_ext(u_ref, c)
            _, vjp = jax.vjp(_ffn_post_conv, _conv_ext(xe, w, FK), b, ue)
            dye, db, due = vjp(_load_ext(d_ref, c))
            du_ref[rows, :] = due[HALO:HALO + RC].astype(BF16)
            db_ref[...] += jnp.sum(dye[HALO:HALO + RC], axis=0, keepdims=True)
            dg_ref[rows, :] = _conv_bwd_ext(xe, dye, w, FK, dw_ref).astype(BF16)
            return carry

        lax.fori_loop(0, T // RC, chunk, 0)

    col = pl.BlockSpec((T, 128), lambda j: (0, j))
    return pl.pallas_call(
        body, name=name, grid=(nblk,),
        in_specs=[col, pl.BlockSpec((T, 128), lambda j: (0, nblk + j)), pl.BlockSpec((FK, 128), lambda j: (0, j)),
                  pl.BlockSpec((1, 128), lambda j: (0, j)), col],
        out_specs=[pl.BlockSpec((2, T, 128), lambda j: (0, 0, j)), pl.BlockSpec((FK, 128), lambda j: (0, j)),
                   pl.BlockSpec((1, 128), lambda j: (0, j))],
        out_shape=[jax.ShapeDtypeStruct((2, T, DFF), BF16), jax.ShapeDtypeStruct((FK, DFF), F32),
                   jax.ShapeDtypeStruct((1, DFF), F32)],
        compiler_params=_cp("parallel"),
    )(gu, gu, conv_w, conv_b, dact)


def _gdn_gates(ba, alog, dtb):
    beta = jax.nn.sigmoid(ba)
    g = -jnp.exp(alog) * jax.nn.softplus(ba + dtb)
    tril = (_iota((GC, GC), 0) >= _iota((GC, GC), 1)).astype(F32)
    return beta, jnp.dot(tril, g, precision=HIGHEST, preferred_element_type=F32)


def _hmap(f, *lists):
    return [f(*xs) for xs in zip(*lists)]


def _gdn_cols(beta_all, gc_all):
    return [_col(beta_all, h) for h in range(GH)], [_col(gc_all, GH + h) for h in range(GH)]


def _gdn_decay(Gcs):
    r, c = _iota((GC, GC), 0), _iota((GC, GC), 1)
    eye, ones = (r == c).astype(F32), jnp.ones((GC, GC), F32)
    grows = _hmap(lambda G: jnp.dot(ones, eye * G, precision=HIGHEST, preferred_element_type=F32), Gcs)
    return _hmap(lambda G, grow: jnp.exp(jnp.where(r >= c, G - grow, -1e30)), Gcs, grows)


def _gdn_A(ks, betas, Gcs):
    strict = _iota((GC, GC), 0) > _iota((GC, GC), 1)
    kk = _hmap(lambda k, b: _dot_nt(k * b, k), ks, betas)
    return _hmap(lambda a, d: jnp.where(strict, a * d, 0.0), kk, _gdn_decay(Gcs))


def _tri_inv(As):
    eye = (_iota((GC, GC), 0) == _iota((GC, GC), 1)).astype(F32)
    Tms, Ps = [eye - A for A in As], As
    for _ in range(GC.bit_length() - 2):
        Ps = _hmap(lambda P: _dot(P, P), Ps)
        Tms = _hmap(lambda Tm, P: Tm + _dot(Tm, P), Tms, Ps)
    return Tms


def _gdn_chunk(qs, ks, vs, betas, Gcs, Ss, Tms):
    decays = _gdn_decay(Gcs)
    eGs = _hmap(jnp.exp, Gcs)
    us = _hmap(lambda Tm, v, b: _dot(Tm, v * b), Tms, vs, betas)
    ws = _hmap(lambda Tm, k, b, eG: _dot(Tm, k * b * eG), Tms, ks, betas, eGs)
    qss = [q * (GD ** -0.5) for q in qs]
    qks = _hmap(lambda q, k, d: _dot_nt(q, k) * d, qss, ks, decays)
    glasts = [jnp.sum(jnp.where(_iota(G.shape, 0) == GC - 1, G, 0.0), axis=0, keepdims=True) for G in Gcs]
    kds = _hmap(lambda k, gl, G: k * jnp.exp(gl - G), ks, glasts, Gcs)
    v_news = _hmap(lambda u, w, S: u - _dot(w, S), us, ws, Ss)
    qS = _hmap(lambda q, eG, S: _dot(q * eG, S), qss, eGs, Ss)
    os = _hmap(lambda a, qk, vn: a + _dot(qk, vn), qS, qks, v_news)
    S_news = _hmap(lambda S, gl, kd, vn: S * jnp.exp(gl) + _dot_tn(kd, vn), Ss, glasts, kds, v_news)
    return os, S_news


def _gdn_chunk_fwd(qkv_c, proj, alog, dtb, *, name):
    W3 = 3 * GH * GD

    def body(x_ref, ba_ref, al_ref, dt_ref, o_ref, s_ref, t_ref, S):
        @pl.when(pl.program_id(0) == 0)
        def _():
            S[...] = jnp.zeros_like(S)

        beta_all, gc_all = _gdn_gates(ba_ref[...], al_ref[...], dt_ref[...])
        qs, ks, vs = ([x_ref[:, (p * GH + h) * GD:(p * GH + h + 1) * GD] for h in range(GH)] for p in range(3))
        betas, Gcs = _gdn_cols(beta_all, gc_all)
        Tms = _tri_inv(_gdn_A(ks, betas, Gcs))
        Ss = [S[h] for h in range(GH)]
        os, S_news = _gdn_chunk(qs, ks, vs, betas, Gcs, Ss, Tms)
        for h in range(GH):
            s_ref[0, h] = Ss[h]
            t_ref[0, h] = Tms[h]
            o_ref[:, h * GD:(h + 1) * GD] = os[h]
            S[h] = S_news[h]

    vec = pl.BlockSpec((1, 128), lambda n: (0, 0))
    return pl.pallas_call(
        body, name=name, grid=(N_CHUNK,),
        in_specs=[pl.BlockSpec((GC, W3), lambda n: (n, 0)), pl.BlockSpec((GC, 128), lambda n: (n, COL_BA)), vec, vec],
        out_specs=[pl.BlockSpec((GC, GH * GD), lambda n: (n, 0)),
                   pl.BlockSpec((1, GH, GD, GD), lambda n: (n, 0, 0, 0)),
                   pl.BlockSpec((1, GH, GC, GC), lambda n: (n, 0, 0, 0))],
        out_shape=[jax.ShapeDtypeStruct((T, GH * GD), F32), jax.ShapeDtypeStruct((N_CHUNK, GH, GD, GD), F32),
                   jax.ShapeDtypeStruct((N_CHUNK, GH, GC, GC), F32)],
        scratch_shapes=[pltpu.VMEM((GH, GD, GD), F32)], compiler_params=_cp("arbitrary"),
    )(qkv_c, proj, alog, dtb)


def _gdn_chunk_bwd(qkv_c, proj, alog, dtb, s_all, t_all, do, dproj, *, name):
    W3 = 3 * GH * GD
    rev = lambda n: N_CHUNK - 1 - n

    def body(x_ref, ba_ref, al_ref, dt_ref, s_ref, t_ref, do_ref, _, dx_ref, dba_ref, dal_ref, ddt_ref, dS):
        @pl.when(pl.program_id(0) == 0)
        def _():
            dS[...] = jnp.zeros_like(dS)
            dal_ref[...] = jnp.zeros_like(dal_ref)
            ddt_ref[...] = jnp.zeros_like(ddt_ref)

        (beta_all, gc_all), vjp_gates = jax.vjp(_gdn_gates, ba_ref[...], al_ref[...], dt_ref[...])
        qs, ks, vs = ([x_ref[:, (p * GH + h) * GD:(p * GH + h + 1) * GD] for h in range(GH)] for p in range(3))
        Tms = [t_ref[0, h] for h in range(GH)]
        Ss = [s_ref[0, h] for h in range(GH)]
        dos = [do_ref[:, h * GD:(h + 1) * GD] for h in range(GH)]
        dSs = [dS[h] for h in range(GH)]

        def f(qs, ks, vs, b_all, g_all, Ss, Tms):
            return _gdn_chunk(qs, ks, vs, *_gdn_cols(b_all, g_all), Ss, Tms)

        def fa(ks, b_all, g_all):
            return _gdn_A(ks, *_gdn_cols(b_all, g_all))

        _, vjp = jax.vjp(f, qs, ks, vs, beta_all, gc_all, Ss, Tms)
        dqs, dks, dvs, db1, dg1, dS_prev, dTs = vjp((dos, dSs))
        dXs = _hmap(_dot_nt, dTs, Tms)
        dAs = _hmap(lambda Tm, dX: -_dot_tn(Tm, dX), Tms, dXs)
        _, vjp_a = jax.vjp(fa, ks, beta_all, gc_all)
        dks2, db2, dg2 = vjp_a(dAs)
        for h in range(GH):
            dS[h] = dS_prev[h]
            dx_ref[:, h * GD:(h + 1) * GD] = dqs[h]
            dx_ref[:, (GH + h) * GD:(GH + h + 1) * GD] = dks[h] + dks2[h]
            dx_ref[:, (2 * GH + h) * GD:(2 * GH + h + 1) * GD] = dvs[h]
        dba, dal, ddt = vjp_gates((db1 + db2, dg1 + dg2))
        dba_ref[...] = dba.astype(BF16)
        dal_ref[...] += dal
        ddt_ref[...] += ddt

    vec = pl.BlockSpec((1, 128), lambda n: (0, 0))
    return pl.pallas_call(
        body, name=name, grid=(N_CHUNK,),
        in_specs=[pl.BlockSpec((GC, W3), lambda n: (rev(n), 0)), pl.BlockSpec((GC, 128), lambda n: (rev(n), COL_BA)),
                  vec, vec, pl.BlockSpec((1, GH, GD, GD), lambda n: (rev(n), 0, 0, 0)),
                  pl.BlockSpec((1, GH, GC, GC), lambda n: (rev(n), 0, 0, 0)),
                  pl.BlockSpec((GC, GH * GD), lambda n: (rev(n), 0)),
                  ANY],
        out_specs=[pl.BlockSpec((GC, W3), lambda n: (rev(n), 0)), pl.BlockSpec((GC, 128), lambda n: (rev(n), COL_BA)), vec, vec],
        out_shape=[jax.ShapeDtypeStruct((T, W3), F32), jax.ShapeDtypeStruct(dproj.shape, dproj.dtype),
                   jax.ShapeDtypeStruct((1, 128), F32), jax.ShapeDtypeStruct((1, 128), F32)],
        scratch_shapes=[pltpu.VMEM((GH, GD, GD), F32)], input_output_aliases={7: 1}, compiler_params=_cp("arbitrary"),
    )(qkv_c, proj, alog, dtb, s_all, t_all, do, dproj)


def _gdn_post(o, z, nw):
    return o * lax.rsqrt(jnp.mean(o * o, axis=-1, keepdims=True) + EPS) * nw * _silu(z)


def _gdn_post_fwd(o_raw, proj, nw, mixed, *, name, tm=512):
    def body(o_ref, z_ref, w_ref, _, out_ref):
        out_ref[...] = _gdn_post(o_ref[...], z_ref[...], w_ref[...]).astype(BF16)

    return pl.pallas_call(
        body, name=name, grid=(T // tm, GH),
        in_specs=[pl.BlockSpec((tm, GD), lambda i, h: (i, h)), pl.BlockSpec((tm, GD), lambda i, h: (i, COL_Z + h)),
                  pl.BlockSpec((1, GD), lambda i, h: (0, 0)), ANY],
        out_specs=pl.BlockSpec((tm, GD), lambda i, h: (i, HQ * DH // GD + h)),
        out_shape=jax.ShapeDtypeStruct(mixed.shape, mixed.dtype), input_output_aliases={3: 0},
        compiler_params=_cp("parallel", "parallel"),
    )(o_raw, proj, nw, mixed)


def _gdn_post_bwd(o_raw, proj, nw, dmixed, *, name, tm=512):
    def body(o_ref, z_ref, w_ref, d_ref, do_ref, dz_ref, dw_ref):
        _, vjp = jax.vjp(_gdn_post, o_ref[...], z_ref[...], w_ref[...])
        do, dz, dw = vjp(d_ref[...])
        do_ref[...] = do
        dz_ref[...] = dz.astype(BF16)

        @pl.when((pl.program_id(0) == 0) & (pl.program_id(1) == 0))
        def _():
            dw_ref[...] = jnp.zeros_like(dw_ref)

        dw_ref[...] += dw

    blk = pl.BlockSpec((tm, GD), lambda i, h: (i, h))
    vec = pl.BlockSpec((1, GD), lambda i, h: (0, 0))
    return pl.pallas_call(
        body, name=name, grid=(T // tm, GH),
        in_specs=[blk, pl.BlockSpec((tm, GD), lambda i, h: (i, COL_Z + h)), vec,
                  pl.BlockSpec((tm, GD), lambda i, h: (i, GH + h))],
        out_specs=[blk, pl.BlockSpec((tm, GD), lambda i, h: (i, COL_Z + h)), vec],
        out_shape=[jax.ShapeDtypeStruct((T, GH * GD), F32), jax.ShapeDtypeStruct((T, N_PROJ), BF16),
                   jax.ShapeDtypeStruct((1, GD), F32)],
        compiler_params=_cp("arbitrary", "arbitrary"),
    )(o_raw, proj, nw, dmixed)


def _adamw(w, g, m, v, *, name):
    shape = w.shape
    cols = shape[-1]
    w2, g2, m2, v2 = (a.reshape(-1, cols) for a in (w, g, m, v))
    rows = w2.shape[0]
    tr = next((t for t in (512, 256, 128, 64, 32, 16, 8) if rows % t == 0), rows)

    def body(w_ref, g_ref, m_ref, v_ref, d_ref, nm_ref, nv_ref):
        gv = g_ref[...]
        nm = B1 * m_ref[...] + (1.0 - B1) * gv
        nv = B2 * v_ref[...] + (1.0 - B2) * jnp.square(gv)
        m_hat = nm / (1.0 - B1 ** STEP)
        v_hat = nv / (1.0 - B2 ** STEP)
        d_ref[...] = -LR * (m_hat / (jnp.sqrt(v_hat) + AEPS) + WD * w_ref[...])
        nm_ref[...] = nm
        nv_ref[...] = nv

    blk = pl.BlockSpec((tr, cols), lambda i: (i, 0))
    out = pl.pallas_call(
        body, name=name, grid=(rows // tr,), in_specs=[blk] * 4, out_specs=[blk] * 3,
        out_shape=[jax.ShapeDtypeStruct((rows, cols), F32)] * 3, compiler_params=_cp("parallel"),
    )(w2, g2, m2, v2)
    return tuple(o.reshape(shape) for o in out)


def _layer_weights(w_in, w_out, w_ffn, w_down):
    w_all = jnp.pad(w_in, ((0, 0), (0, N_PROJ - w_in.shape[1])))
    return dict(all=w_all, out=w_out, ffn=w_ffn, down=w_down)


def _layer_params(attn_norm, sinks, gcw, a_log, dt_bias, gnw, ffn_norm, fcw, fcb):
    lanes4 = lambda v: jnp.pad(v, (GH, 128 - 2 * GH))[None]
    return dict(attn_norm=attn_norm[None], sinks=jnp.pad(sinks, (0, 128 - HQ))[None], gcw=gcw, alog=lanes4(a_log),
                dtb=lanes4(dt_bias), gnw=gnw[None], ffn_norm=ffn_norm[None], fcw=fcw, fcb=fcb[None])


def _layer_fwd(x, W, P, l):
    n = lambda s: f"l{l}_{s}"
    h = _rmsnorm_fwd(x, P["attn_norm"], name=n("norm1"))
    proj = _mm(h, W["all"], name=n("proj"), mode="nn", dims=(T, N_PROJ, D), tm=512, tn=N_PROJ, tk=D)
    mixed = _swa_fwd(proj, P["sinks"], name=n("swa"))
    qkv_c = _gdn_prep_fwd(proj, P["gcw"], name=n("gdn_prep"))
    o_raw, s_all, t_all = _gdn_chunk_fwd(qkv_c, proj, P["alog"], P["dtb"], name=n("gdn_chunk"))
    mixed = _gdn_post_fwd(o_raw, proj, P["gnw"], mixed, name=n("gdn_post"))
    x1 = _mm(mixed, W["out"], res=x, name=n("out_proj"), mode="nn", dims=(T, D, D), tm=512, tn=D, tk=D)
    h2 = _rmsnorm_fwd(x1, P["ffn_norm"], name=n("norm2"))
    gu = _mm(h2, W["ffn"], name=n("ffn_in"), mode="nn", dims=(T, 2 * DFF, D), tm=512, tn=FFN_CW, tk=D,
             b_spec=pl.BlockSpec((None, D, FFN_CW), lambda i, j, k: (j, k, 0)))
    act = _ffn_act_fwd(gu, P["fcw"], P["fcb"], name=n("ffn_act"))
    x2 = _mm(act, W["down"], res=x1, name=n("ffn_down"), mode="nn", dims=(T, D, DFF), tm=512, tn=D, tk=DFF)
    saved = dict(x=x, h=h, proj=proj, qkv_c=qkv_c, o_raw=o_raw, s_all=s_all, t_all=t_all, mixed=mixed, x1=x1, h2=h2,
                 gu=gu, act=act)
    return x2, saved


def _layer_bwd(dx2, sv, W, P, l):
    n = lambda s: f"l{l}_{s}"
    CW = FFN_CW
    dact =_mm(dx2, W["down"], name=n("d_act"), mode="nt", dims=(T, DFF, D), tm=512, tn=DFF // 2, tk=D, out_dtype=BF16)
    g_down = _mm(sv["act"], dx2, name=n("g_down"), mode="tn", dims=(DFF, D, T), tm=DFF // 2, tn=D, tk=1024, out_dtype=BF16)
    dgu, g_fcw, g_fcb = _ffn_act_bwd(sv["gu"], P["fcw"], P["fcb"], dact, name=n("d_ffn_act"))
    dh2 = _mm(dgu, W["ffn"], name=n("d_h2"), mode="nt", dims=(T, D, 2 * DFF), tm=512, tn=D, tk=CW,
              a_spec=pl.BlockSpec((None, 512, CW), lambda i, j, k: (k // 2, i, k % 2)),
              b_spec=pl.BlockSpec((None, D, CW), lambda i, j, k: (k, j, 0)))
    g_ffn = _mm(sv["h2"], dgu, name=n("g_ffn"), mode="tn", dims=(D, 2 * DFF, T), tm=D, tn=CW, tk=1024,
                b_spec=pl.BlockSpec((None, 1024, CW), lambda i, j, k: (j // 2, k, j % 2)),
                out_spec=pl.BlockSpec((None, D, CW), lambda i, j, k: (j, i, 0)),
                out_shape=jax.ShapeDtypeStruct((N_CHIP, D, CW), BF16), out_dtype=BF16)
    dx1, g_ffn_norm = _rmsnorm_bwd(sv["x1"], P["ffn_norm"], dh2, dx2, name=n("d_norm2"))
    dmixed = _mm(dx1, W["out"], name=n("d_mixed"), mode="nt", dims=(T, D, D), tm=512, tn=D, tk=D)
    g_out = _mm(sv["mixed"], dx1, name=n("g_out"), mode="tn", dims=(D, D, T), tm=D, tn=D, tk=1024, out_dtype=BF16)
    do_raw, dproj, g_gnw = _gdn_post_bwd(sv["o_raw"], sv["proj"], P["gnw"], dmixed, name=n("d_gdn_post"))
    dqkv_c, dproj, g_alog, g_dtb = _gdn_chunk_bwd(sv["qkv_c"], sv["proj"], P["alog"], P["dtb"], sv["s_all"], sv["t_all"],
                                                   do_raw, dproj, name=n("d_gdn_chunk"))
    dproj, g_gcw = _gdn_prep_bwd(sv["proj"], P["gcw"], dqkv_c, dproj, name=n("d_gdn_prep"))
    dproj, g_sinks = _swa_bwd(sv["proj"], P["sinks"], dmixed, dproj, name=n("d_swa"))
    dh = _mm(dproj, W["all"], name=n("d_h"), mode="nt", dims=(T, D, N_PROJ), tm=512, tn=D, tk=N_PROJ)
    g_all = _mm(sv["h"], dproj, name=n("g_in"), mode="tn", dims=(D, N_PROJ, T), tm=512, tn=N_PROJ, tk=1024, out_dtype=BF16)
    dx, g_attn_norm = _rmsnorm_bwd(sv["x"], P["attn_norm"], dh, dx1, name=n("d_norm1"))
    grads = dict(
        attn_norm=g_attn_norm[0], w_in=g_all, attn_sinks=g_sinks[0, :HQ], gdn_conv_w=g_gcw,
        gdn_a_log=g_alog[0, GH:2 * GH], gdn_dt_bias=g_dtb[0, GH:2 * GH], gdn_norm=g_gnw[0], w_out=g_out,
        ffn_norm=g_ffn_norm[0], w_ffn_in=g_ffn, ffn_conv_w=g_fcw, ffn_conv_b=g_fcb[0], w_down=g_down)
    return dx, grads


def _pos():
    return lax.axis_index("x"), lax.axis_index("y"), lax.axis_index("c")


def _other_chips(x, y):
    return [(1 - x, y), (x, 1 - y), (1 - x, 1 - y)]


def _remote(src, dst, send_sems, recv_sems, k, to):
    return pltpu.make_async_remote_copy(src_ref=src, dst_ref=dst, send_sem=send_sems.at[k], recv_sem=recv_sems.at[k],
                                        device_id=to, device_id_type=MESH)


def _ag_weights(shards, *, name):
    nt = len(shards)

    def body(*refs):
        srcs, outs = refs[:nt], refs[nt:2 * nt]
        send_sems, recv_sems = refs[2 * nt:]
        x, y, c = _pos()
        me, sibling, chip = (x, y, c), (x, y, 1 - c), 2 * x + y
        others = [(k, px, py, 2 * px + py) for k, (px, py) in enumerate(_other_chips(x, y))]
        first = [_remote(srcs[t].at[c], outs[t].at[c, chip], send_sems, recv_sems, 3 * t + k, (px, py, c))
                 for t in range(nt) for k, px, py, _ in others]
        for cp in first:
            cp.start()
        passed = []
        for t in range(nt):
            for k, _, _, j in others:
                slot = outs[t].at[c, j]
                _remote(slot, slot, send_sems, recv_sems, 3 * t + k, me).wait_recv()
                passed.append(_remote(slot, slot, send_sems, recv_sems, 3 * (nt + t) + k, sibling))
                passed[-1].start()
        for t in range(nt):
            for k, _, _, j in others:
                slot = outs[t].at[1 - c, j]
                _remote(slot, slot, send_sems, recv_sems, 3 * (nt + t) + k, me).wait_recv()
        for cp in first + passed:
            cp.wait_send()

    outs = pl.pallas_call(
        body, name=name, in_specs=[ANY] * nt, out_specs=[ANY] * nt,
        out_shape=[jax.ShapeDtypeStruct((DEPTH, N_CHIP) + s.shape[1:], s.dtype) for s in shards],
        scratch_shapes=[pltpu.SemaphoreType.DMA((6 * nt,)), pltpu.SemaphoreType.DMA((6 * nt,))],
    )(*shards)
    chip = 2 * lax.axis_index("x") + lax.axis_index("y")
    return [lax.dynamic_update_index_in_dim(o, s, chip, 1) for o, s in zip(outs, shards)]


def _ag_small(v, *, name):
    m, n = v.shape

    def body(x_ref, out_ref, red_ref, send_sems, recv_sems, local_sem):
        x, y, c = _pos()
        me, sibling = (x, y, c), (x, y, 1 - c)
        chips = _other_chips(x, y)
        rows = lambda px, py, pc: out_ref.at[pl.ds(pl.multiple_of((4 * px + 2 * py + pc) * m, 8), m), :]
        mine = pltpu.make_async_copy(x_ref, rows(*me), local_sem)
        mine.start()
        first = [_remote(x_ref, rows(*me), send_sems, recv_sems, 0, sibling)]
        first += [_remote(x_ref, rows(*me), send_sems, recv_sems, 1 + k, (*chip, c)) for k, chip in enumerate(chips)]
        for cp in first:
            cp.start()
        passed = [_remote(rows(*chip, c), rows(*chip, c), send_sems, recv_sems, 4 + k, sibling) for k, chip in enumerate(chips)]
        for k, chip in enumerate(chips):
            _remote(rows(*chip, c), rows(*chip, c), send_sems, recv_sems, 1 + k, me).wait_recv()
            passed[k].start()
        _remote(rows(*sibling), rows(*sibling), send_sems, recv_sems, 0, me).wait_recv()
        for k, chip in enumerate(chips):
            _remote(rows(*chip, 1 - c), rows(*chip, 1 - c), send_sems, recv_sems, 4 + k, me).wait_recv()
        for cp in first + passed:
            cp.wait_send()
        mine.wait()
        acc = out_ref[0:m, :]
        for d in range(1, N_DEV):
            acc = acc + out_ref[d * m:(d + 1) * m, :]
        red_ref[...] = acc

    return pl.pallas_call(
        body, name=name, in_specs=[VMEM_SPEC], out_specs=[VMEM_SPEC, VMEM_SPEC],
        out_shape=[jax.ShapeDtypeStruct((N_DEV * m, n), v.dtype), jax.ShapeDtypeStruct((m, n), v.dtype)],
        scratch_shapes=[pltpu.SemaphoreType.DMA((7,)), pltpu.SemaphoreType.DMA((7,)), pltpu.SemaphoreType.DMA],
    )(v)


def _halves(ref, c):
    rh = ref.shape[1] // 2
    return ref.at[:, pl.ds(pl.multiple_of(c * rh, 16), rh), :]


def _rs_swap(gs, *, name):
    nt = len(gs)

    def body(*refs):
        g, theirs = refs[:nt], refs[nt:2 * nt]
        send_sems, recv_sems = refs[2 * nt:]
        x, y, c = _pos()
        swaps = [_remote(_halves(g[t], 1 - c), theirs[t], send_sems, recv_sems, t, (x, y, 1 - c)) for t in range(nt)]
        for cp in swaps:
            cp.start()
        for cp in swaps:
            cp.wait()

    return pl.pallas_call(
        body, name=name, in_specs=[ANY] * nt, out_specs=[ANY] * nt,
        out_shape=[jax.ShapeDtypeStruct((a.shape[0], a.shape[1] // 2, a.shape[2]), a.dtype) for a in gs],
        scratch_shapes=[pltpu.SemaphoreType.DMA((nt,)), pltpu.SemaphoreType.DMA((nt,))],
    )(*gs)


def _rs_exchange(ss, *, name):
    nt = len(ss)

    def body(*refs):
        s, out = refs[:nt], refs[nt:2 * nt]
        send_sems, recv_sems = refs[2 * nt:]
        x, y, c = _pos()
        chip = 2 * x + y
        others = [(k, px, py, 2 * px + py) for k, (px, py) in enumerate(_other_chips(x, y))]
        sends = [_remote(s[t].at[j], out[t].at[chip], send_sems, recv_sems, 3 * t + k, (px, py, c))
                 for t in range(nt) for k, px, py, j in others]
        for cp in sends:
            cp.start()
        for t in range(nt):
            for k, _, _, j in others:
                _remote(s[t].at[j], out[t].at[j], send_sems, recv_sems, 3 * t + k, (x, y, c)).wait_recv()
        for cp in sends:
            cp.wait_send()

    outs = pl.pallas_call(
        body, name=name, in_specs=[ANY] * nt, out_specs=[ANY] * nt,
        out_shape=[jax.ShapeDtypeStruct(a.shape, a.dtype) for a in ss],
        scratch_shapes=[pltpu.SemaphoreType.DMA((3 * nt,)), pltpu.SemaphoreType.DMA((3 * nt,))],
    )(*ss)
    chip = 2 * lax.axis_index("x") + lax.axis_index("y")
    return [lax.dynamic_update_index_in_dim(o, lax.dynamic_index_in_dim(s, chip, 0, keepdims=False), chip, 0)
            for o, s in zip(outs, ss)]


def _rs_join(rs, *, name):
    nt = len(rs)

    def body(*refs):
        r, theirs = refs[:nt], refs[nt:2 * nt]
        send_sems, recv_sems = refs[2 * nt:]
        x, y, c = _pos()
        swaps = [_remote(r[t], theirs[t], send_sems, recv_sems, t, (x, y, 1 - c)) for t in range(nt)]
        for cp in swaps:
            cp.start()
        for cp in swaps:
            cp.wait()

    theirs = pl.pallas_call(
        body, name=name, in_specs=[ANY] * nt, out_specs=[ANY] * nt,
        out_shape=[jax.ShapeDtypeStruct(a.shape, a.dtype) for a in rs],
        scratch_shapes=[pltpu.SemaphoreType.DMA((nt,)), pltpu.SemaphoreType.DMA((nt,))],
    )(*rs)
    first = lax.axis_index("c") == 0
    return [jnp.concatenate([jnp.where(first, a, b), jnp.where(first, b, a)], axis=0) for a, b in zip(rs, theirs)]


def _row_tile(rows, dtype):
    unit = 16 if dtype == BF16 else 8
    if rows <= 512:
        return rows
    return next((t for t in (512, 256, 128, 64, 32, 16, 8) if rows % t == 0 and t % unit == 0), rows)


def _add_pair(a, b, *, name):
    n, rh, cols = a.shape
    tr = _row_tile(rh, BF16)

    def body(a_ref, b_ref, o_ref):
        o_ref[...] = (a_ref[...].astype(F32) + b_ref[...].astype(F32)).astype(BF16)

    blk = pl.BlockSpec((None, tr, cols), lambda j, i: (j, i, 0))
    return pl.pallas_call(
        body, name=name, grid=(n, rh // tr), in_specs=[blk, blk], out_specs=blk,
        out_shape=jax.ShapeDtypeStruct(a.shape, BF16), compiler_params=_cp("parallel", "parallel"),
    )(a, b)


def _sum_chips(b, *, name):
    n, rh, cols = b.shape
    tr = _row_tile(rh, BF16)

    def body(b_ref, o_ref):
        acc = b_ref[0].astype(F32)
        for j in range(1, n):
            acc = acc + b_ref[j].astype(F32)
        o_ref[...] = acc

    return pl.pallas_call(
        body, name=name, grid=(rh // tr,), in_specs=[pl.BlockSpec((n, tr, cols), lambda i: (0, i, 0))],
        out_specs=pl.BlockSpec((tr, cols), lambda i: (i, 0)),
        out_shape=jax.ShapeDtypeStruct((rh, cols), F32), compiler_params=_cp("parallel"),
    )(b)


BIG = ("w_in", "w_out", "w_ffn_in", "w_down")
W_IN_SHARD = (N_MAIN + 2 * GH) // N_CHIP


def _reduce_scatter_layer(lg, l):
    g_in = jnp.stack([lg["w_in"][:, j * W_IN_SHARD:(j + 1) * W_IN_SHARD] for j in range(N_CHIP)])
    gs = [g_in, lg["w_out"].reshape(N_CHIP, D // N_CHIP, D), lg["w_ffn_in"], lg["w_down"].reshape(N_CHIP, DFF // N_CHIP, D)]
    c = lax.axis_index("c")
    mine = [lax.dynamic_slice_in_dim(a, c * (a.shape[1] // 2), a.shape[1] // 2, axis=1) for a in gs]
    theirs = _rs_swap(gs, name=f"rs{l}_swap")
    pairs = [_add_pair(a, b, name=f"rs{l}_add_{n}") for a, b, n in zip(mine, theirs, BIG)]
    got = _rs_exchange(pairs, name=f"rs{l}_exchange")
    return _rs_join([_sum_chips(b, name=f"rs{l}_sum_{n}") for b, n in zip(got, BIG)], name=f"rs{l}_join")


def _size(shape):
    n = 1
    for s in shape:
        n *= s
    return n


def _pack_flat(parts, total, dtype):
    flat = jnp.concatenate([p.reshape(-1).astype(dtype) for p in parts])
    return jnp.pad(flat, (0, total - flat.shape[0]))


def _unpack_flat(flat, shapes):
    out, o = [], 0
    for s in shapes:
        out.append(flat[o:o + _size(s)].reshape(s))
        o += _size(s)
    return out


WEIGHTS = ("attn_norm", "w_in", "attn_sinks", "gdn_conv_w", "gdn_a_log", "gdn_dt_bias", "gdn_norm", "w_out", "ffn_norm",
           "w_ffn_in", "ffn_conv_w", "ffn_conv_b", "w_down", "final_norm")
SMALL = {"attn_norm": (DEPTH, D), "attn_sinks": (DEPTH, HQ), "gdn_a_log": (DEPTH, GH), "gdn_dt_bias": (DEPTH, GH),
         "gdn_norm": (DEPTH, GD), "ffn_norm": (DEPTH, D), "ffn_conv_b": (DEPTH, DFF), "final_norm": (D,)}
CONV_FULL = {"gdn_conv_w": (DEPTH, GK, 3 * GH * GD), "ffn_conv_w": (DEPTH, FK, DFF)}
CONV_SHARD = {"gdn_conv_w": (DEPTH, GK, 3 * GH * GD // N_CHIP), "ffn_conv_w": (DEPTH, FK, DFF // N_CHIP)}
CONV_ROWS, SMALLG_ROWS, SMALLW_ROWS = 64, 320, 144


def kernel(x, attn_norm, w_in, attn_sinks, gdn_conv_w, gdn_a_log, gdn_dt_bias, gdn_norm, w_out, ffn_norm, w_ffn_in, ffn_conv_w, ffn_conv_b, w_down, final_norm, loss_target, m_attn_norm, m_w_in, m_attn_sinks, m_gdn_conv_w, m_gdn_a_log, m_gdn_dt_bias, m_gdn_norm, m_w_out, m_ffn_norm, m_w_ffn_in, m_ffn_conv_w, m_ffn_conv_b, m_w_down, m_final_norm, v_attn_norm, v_w_in, v_attn_sinks, v_gdn_conv_w, v_gdn_a_log, v_gdn_dt_bias, v_gdn_norm, v_w_out, v_ffn_norm, v_w_ffn_in, v_ffn_conv_w, v_ffn_conv_b, v_w_down, v_final_norm):
    w = dict(zip(WEIGHTS, (attn_norm, w_in, attn_sinks, gdn_conv_w, gdn_a_log, gdn_dt_bias, gdn_norm, w_out, ffn_norm,
                           w_ffn_in, ffn_conv_w, ffn_conv_b, w_down, final_norm)))
    m = dict(zip(WEIGHTS, (m_attn_norm, m_w_in, m_attn_sinks, m_gdn_conv_w, m_gdn_a_log, m_gdn_dt_bias, m_gdn_norm, m_w_out,
                           m_ffn_norm, m_w_ffn_in, m_ffn_conv_w, m_ffn_conv_b, m_w_down, m_final_norm)))
    v = dict(zip(WEIGHTS, (v_attn_norm, v_w_in, v_attn_sinks, v_gdn_conv_w, v_gdn_a_log, v_gdn_dt_bias, v_gdn_norm, v_w_out,
                           v_ffn_norm, v_w_ffn_in, v_ffn_conv_w, v_ffn_conv_b, v_w_down, v_final_norm)))
    cx, cy, _ = _pos()
    chip = 2 * cx + cy

    gath = dict(zip(BIG, _ag_weights([w[n].astype(BF16) for n in BIG], name="gather_weights")))
    full = {"w_in": jnp.concatenate([gath["w_in"][:, j] for j in range(N_CHIP)], axis=2),
            "w_out": gath["w_out"].reshape(DEPTH, D, D), "w_ffn_in": gath["w_ffn_in"],
            "w_down": gath["w_down"].reshape(DEPTH, DFF, D)}
    cpack = _pack_flat([w[n] for n in CONV_SHARD], CONV_ROWS * 128, F32).reshape(CONV_ROWS, 128)
    cgath, _ = _ag_small(cpack, name="gather_conv_w")
    cgath = cgath.reshape(N_DEV, CONV_ROWS * 128)
    cshards = [_unpack_flat(cgath[2 * j], list(CONV_SHARD.values())) for j in range(N_CHIP)]
    conv = {n: jnp.concatenate([cshards[j][i] for j in range(N_CHIP)], axis=2) for i, n in enumerate(CONV_SHARD)}

    Ws = [_layer_weights(full["w_in"][l], full["w_out"][l], full["w_ffn_in"][l], full["w_down"][l]) for l in range(DEPTH)]
    Ps = [_layer_params(attn_norm[l], attn_sinks[l], conv["gdn_conv_w"][l], gdn_a_log[l], gdn_dt_bias[l], gdn_norm[l],
                        ffn_norm[l], conv["ffn_conv_w"][l], ffn_conv_b[l]) for l in range(DEPTH)]

    h, saved = x[0], []
    for l in range(DEPTH):
        h, sv = _layer_fwd(h, Ws[l], Ps[l], l)
        saved.append(sv)
    dx, g_final, loss_part = _loss_head(h, final_norm[None], loss_target[0], name="loss_head")
    lg, reduced = [None] * DEPTH, [None] * DEPTH
    for l in reversed(range(DEPTH)):
        dx, lg[l] = _layer_bwd(dx, saved[l], Ws[l], Ps[l], l)
        reduced[l] = _reduce_scatter_layer(lg[l], l)
    grad_x = dx[None]
    stacked = lambda n: jnp.stack([lg[l][n] for l in range(DEPTH)])

    small_parts = [g_final[0] if n == "final_norm" else stacked(n) for n in SMALL] + [stacked(n) for n in CONV_FULL]
    svec = _pack_flat(small_parts + [loss_part[0, :1]], SMALLG_ROWS * 128, F32).reshape(SMALLG_ROWS, 128)
    _, sred = _ag_small(svec, name="reduce_small")
    small_g = _unpack_flat(sred.reshape(-1), list(SMALL.values()) + list(CONV_FULL.values()) + [(1,)])
    g = dict(zip(list(SMALL) + list(CONV_FULL), small_g[:-1]))
    loss = small_g[-1][0]
    for n in CONV_FULL:
        wd = CONV_SHARD[n][2]
        g[n] = lax.dynamic_slice_in_dim(g[n], chip * wd, wd, axis=2)

    g.update({n: jnp.stack([reduced[l][i] for l in range(DEPTH)]) for i, n in enumerate(BIG)})

    delta, new_m, new_v = {}, {}, {}
    for n in BIG:
        delta[n], new_m[n], new_v[n] = _adamw(w[n], g[n], m[n], v[n], name=f"adamw_{n}")
    small_names = list(SMALL) + list(CONV_SHARD)
    small_shapes = list(SMALL.values()) + list(CONV_SHARD.values())
    packed = [_pack_flat([t[n] for n in small_names], SMALLW_ROWS * 128, F32).reshape(SMALLW_ROWS, 128) for t in (w, g, m, v)]
    for res, out in zip(_adamw(*packed, name="adamw_small"), (delta, new_m, new_v)):
        out.update(zip(small_names, _unpack_flat(res.reshape(-1), small_shapes)))

    return (loss, grad_x, *[g[n] for n in WEIGHTS], *[delta[n] for n in WEIGHTS], *[new_m[n] for n in WEIGHTS],
            *[new_v[n] for n in WEIGHTS])
```

```python
import functools

import jax
import jax.numpy as jnp
import numpy as np
from jax import lax
from jax.experimental import pallas as pl
from jax.experimental.pallas import tpu as pltpu

F32, BF16 = jnp.float32, jnp.bfloat16
HIGHEST = lax.Precision.HIGHEST
MESH = pl.DeviceIdType.MESH

D = 1024
T = 4096
DEPTH = 2
HQ, HKV, DH, WIN = 8, 2, 64, 128
GH, GD, GC, GK = 4, 128, 64, 4
DFF, FK = 2816, 3
EPS = 1e-6
N_MAIN = 2816
N_PROJ = N_MAIN + 128
COL_K, COL_V, COL_G, COL_Z, COL_BA = 4, 5, 6, 18, 22
N_CHUNK = T // GC
N_DEV, N_CHIP = 8, 4
FFN_CW = 2 * DFF // N_CHIP
LR, B1, B2, AEPS, WD, STEP = 0.001, 0.9, 0.999, 1e-08, 0.01, 10

VMEM_LIMIT = 56 * 1024 * 1024
ANY = pl.BlockSpec(memory_space=pl.ANY)
VMEM_SPEC = pl.BlockSpec(memory_space=pltpu.VMEM)


def _cp(*sem):
    return pltpu.CompilerParams(dimension_semantics=sem if sem else None, vmem_limit_bytes=VMEM_LIMIT)


def _dot(a, b):
    return jnp.dot(a.astype(BF16), b.astype(BF16), preferred_element_type=F32)


def _dot_nt(a, b):
    return lax.dot_general(a.astype(BF16), b.astype(BF16), (((1,), (1,)), ((), ())), preferred_element_type=F32)


def _dot_tn(a, b):
    return lax.dot_general(a.astype(BF16), b.astype(BF16), (((0,), (0,)), ((), ())), preferred_element_type=F32)


def _iota(shape, dim):
    return lax.broadcasted_iota(jnp.int32, shape, dim)


def _col(x, idx):
    return jnp.sum(jnp.where(_iota(x.shape, 1) == idx, x, 0.0), axis=1, keepdims=True)


def _silu(y):
    return y * jax.nn.sigmoid(y)


_MM_DN = {"nn": (((1,), (0,)), ((), ())), "nt": (((1,), (1,)), ((), ())), "tn": (((0,), (0,)), ((), ()))}


def _mm(a, b, *, name, mode, dims, tm, tn, tk, res=None, out_dtype=F32, a_spec=None, b_spec=None, out_spec=None,
        out_shape=None, n_outer=False):
    M, N, K = dims
    assert M % tm == 0 and N % tn == 0 and K % tk == 0, (name, dims)
    nk = K // tk
    dn = _MM_DN[mode]

    def body(*refs):
        if res is None:
            a_ref, b_ref, o_ref, acc = refs
        else:
            a_ref, b_ref, r_ref, o_ref, acc = refs
        k = pl.program_id(2)

        @pl.when(k == 0)
        def _():
            acc[...] = jnp.zeros_like(acc)

        acc[...] += lax.dot_general(a_ref[...].astype(BF16), b_ref[...].astype(BF16), dn, preferred_element_type=F32)

        @pl.when(k == nk - 1)
        def _():
            r = acc[...]
            if res is not None:
                r = r + r_ref[...]
            o_ref[...] = r.astype(out_dtype)

    if a_spec is None:
        a_spec = pl.BlockSpec((tk, tm), lambda i, j, k: (k, i)) if mode == "tn" else pl.BlockSpec((tm, tk), lambda i, j, k: (i, k))
    if b_spec is None:
        b_spec = pl.BlockSpec((tn, tk), lambda i, j, k: (j, k)) if mode == "nt" else pl.BlockSpec((tk, tn), lambda i, j, k: (k, j))
    in_specs, args = [a_spec, b_spec], [a, b]
    if res is not None:
        in_specs.append(pl.BlockSpec((tm, tn), lambda i, j, k: (i, j)))
        args.append(res)
    out_spec = out_spec or pl.BlockSpec((tm, tn), lambda i, j, k: (i, j))
    grid = (M // tm, N // tn, nk)
    if n_outer:
        swap = lambda s: pl.BlockSpec(s.block_shape, lambda j, i, k, f=s.index_map: f(i, j, k))
        in_specs, out_spec, grid = [swap(s) for s in in_specs], swap(out_spec), (N // tn, M // tm, nk)
    return pl.pallas_call(
        body, name=name, grid=grid, in_specs=in_specs, out_specs=out_spec,
        out_shape=out_shape or jax.ShapeDtypeStruct((M, N), out_dtype),
        scratch_shapes=[pltpu.VMEM((tm, tn), F32)],
        compiler_params=_cp("parallel", "parallel", "arbitrary"),
    )(*args)


def _rms(x, w):
    return x * lax.rsqrt(jnp.mean(x * x, axis=-1, keepdims=True) + EPS) * w


def _rmsnorm_fwd(x, w, *, name, tm=512):
    def body(x_ref, w_ref, o_ref):
        o_ref[...] = _rms(x_ref[...], w_ref[...]).astype(BF16)

    return pl.pallas_call(
        body, name=name, grid=(x.shape[0] // tm,),
        in_specs=[pl.BlockSpec((tm, D), lambda i: (i, 0)), pl.BlockSpec((1, D), lambda i: (0, 0))],
        out_specs=pl.BlockSpec((tm, D), lambda i: (i, 0)),
        out_shape=jax.ShapeDtypeStruct(x.shape, BF16), compiler_params=_cp("parallel"),
    )(x, w)


def _rmsnorm_bwd(x, w, dh, dres, *, name, tm=512):
    def body(x_ref, w_ref, dh_ref, dr_ref, dx_ref, dw_ref):
        _, vjp = jax.vjp(_rms, x_ref[...], w_ref[...])
        dx, dw = vjp(dh_ref[...])
        dx_ref[...] = dx + dr_ref[...]

        @pl.when(pl.program_id(0) == 0)
        def _():
            dw_ref[...] = jnp.zeros_like(dw_ref)

        dw_ref[...] += dw

    row = pl.BlockSpec((tm, D), lambda i: (i, 0))
    vec = pl.BlockSpec((1, D), lambda i: (0, 0))
    return pl.pallas_call(
        body, name=name, grid=(x.shape[0] // tm,), in_specs=[row, vec, row, row], out_specs=[row, vec],
        out_shape=[jax.ShapeDtypeStruct(x.shape, F32), jax.ShapeDtypeStruct((1, D), F32)],
        compiler_params=_cp("arbitrary"),
    )(x, w, dh, dres)


def _loss_head(x, w, tgt, *, name, tm=512):
    def f(xv, wv, tv):
        err = _rms(xv, wv) - tv
        per_row = jnp.sum(err * err, axis=1, keepdims=True) * (0.5 / D)
        return jnp.sum(per_row, axis=0, keepdims=True)

    def body(x_ref, w_ref, t_ref, dx_ref, dw_ref, loss_ref):
        tv = t_ref[...]
        loss, vjp = jax.vjp(lambda xv, wv: f(xv, wv, tv), x_ref[...], w_ref[...])
        dx, dw = vjp(jnp.ones((1, 1), F32))
        dx_ref[...] = dx

        @pl.when(pl.program_id(0) == 0)
        def _():
            dw_ref[...] = jnp.zeros_like(dw_ref)
            loss_ref[...] = jnp.zeros_like(loss_ref)

        dw_ref[...] += dw
        loss_ref[...] += jnp.broadcast_to(loss, loss_ref.shape)

    row = pl.BlockSpec((tm, D), lambda i: (i, 0))
    vec = pl.BlockSpec((1, D), lambda i: (0, 0))
    return pl.pallas_call(
        body, name=name, grid=(x.shape[0] // tm,), in_specs=[row, vec, row],
        out_specs=[row, vec, pl.BlockSpec((1, 128), lambda i: (0, 0))],
        out_shape=[jax.ShapeDtypeStruct(x.shape, F32), jax.ShapeDtypeStruct((1, D), F32),
                   jax.ShapeDtypeStruct((1, 128), F32)],
        compiler_params=_cp("arbitrary"),
    )(x, w, tgt)


def _swa_bias():
    G = HQ // HKV
    r, c = np.arange(G * WIN)[:, None], np.arange(2 * WIN)[None, :]
    rel = (r % WIN) + WIN - c
    band = (rel >= 0) & (rel < WIN)
    out = np.empty((2, HKV, G * WIN, 2 * WIN), np.float32)
    for h2 in range(HKV):
        slope = 2.0 ** (-8.0 * (h2 * G + r // WIN + 1) / HQ)
        out[0, h2] = np.where(band & (c >= WIN), -slope * rel, -1e30)
        out[1, h2] = np.where(band, -slope * rel, -1e30)
    return jnp.asarray(out)


def _swa_group(qg, k2, v2, sk, bias, h2):
    s = _dot_nt(qg, k2) * (DH ** -0.5) + bias
    g1 = _iota((s.shape[0], 1), 0) >> 7
    sink = jnp.zeros((s.shape[0], 1), F32)
    for gi in range(HQ // HKV):
        sink = jnp.where(g1 == gi, _col(sk, h2 * (HQ // HKV) + gi), sink)
    m = lax.stop_gradient(jnp.maximum(jnp.max(s, axis=1, keepdims=True), sink))
    p = jnp.exp(s - m)
    den = jnp.sum(p, axis=1, keepdims=True) + jnp.exp(sink - m)
    return _dot(p * (1.0 / den), v2)


def _swa_split_q(q, h2):
    G = HQ // HKV
    return jnp.concatenate([q[:, (h2 * G + g) * DH:(h2 * G + g + 1) * DH] for g in range(G)], axis=0)


def _swa_merge_q(parts):
    G = HQ // HKV
    return jnp.concatenate([parts[h2][g * WIN:(g + 1) * WIN] for h2 in range(HKV) for g in range(G)], axis=1)


def _swa_specs():
    prev = lambda i: jnp.maximum(jnp.minimum(i, T // WIN - 1) - 1, 0)
    cur = lambda i: jnp.minimum(i, T // WIN - 1)
    return [
        pl.BlockSpec((WIN, HQ * DH), lambda i: (cur(i), 0)),
        pl.BlockSpec((WIN, 128), lambda i: (prev(i), COL_K)),
        pl.BlockSpec((WIN, 128), lambda i: (cur(i), COL_K)),
        pl.BlockSpec((WIN, 128), lambda i: (prev(i), COL_V)),
        pl.BlockSpec((WIN, 128), lambda i: (cur(i), COL_V)),
        pl.BlockSpec((1, 128), lambda i: (0, 0)),
        pl.BlockSpec((None, HKV, (HQ // HKV) * WIN, 2 * WIN), lambda i: (jnp.minimum(i, 1), 0, 0, 0)),
    ]


def _swa_fwd(proj, sinks, *, name):
    def body(q_ref, kp_ref, kc_ref, vp_ref, vc_ref, sk_ref, bias_ref, o_ref):
        q, sk = q_ref[...].astype(F32), sk_ref[...]
        outs = []
        for h2 in range(HKV):
            sl = slice(h2 * DH, (h2 + 1) * DH)
            k2 = jnp.concatenate([kp_ref[:, sl], kc_ref[:, sl]], axis=0).astype(F32)
            v2 = jnp.concatenate([vp_ref[:, sl], vc_ref[:, sl]], axis=0).astype(F32)
            outs.append(_swa_group(_swa_split_q(q, h2), k2, v2, sk, bias_ref[h2], h2))
        o_ref[...] = _swa_merge_q(outs).astype(BF16)

    return pl.pallas_call(
        body, name=name, grid=(T // WIN,), in_specs=_swa_specs(),
        out_specs=pl.BlockSpec((WIN, HQ * DH), lambda i: (i, 0)),
        out_shape=jax.ShapeDtypeStruct((T, HQ * DH + GH * GD), BF16), compiler_params=_cp("parallel"),
    )(proj, proj, proj, proj, proj, sinks, _swa_bias())


def _swa_bwd(proj, sinks, do, dproj, *, name):
    NB = T // WIN
    QW = HQ * DH

    def body(q_ref, kp_ref, kc_ref, vp_ref, vc_ref, sk_ref, bias_ref, do_ref, _, out_ref, dsk_ref, cq, ck, cv):
        i = pl.program_id(0)

        @pl.when(i == 0)
        def _():
            cq[...] = jnp.zeros_like(cq)
            ck[...] = jnp.zeros_like(ck)
            cv[...] = jnp.zeros_like(cv)
            dsk_ref[...] = jnp.zeros_like(dsk_ref)

        @pl.when(i < NB)
        def _():
            q, sk, dov = q_ref[...].astype(F32), sk_ref[...], do_ref[...].astype(F32)
            dqs, dkp, dkc, dvp, dvc = [], [], [], [], []
            dsk = jnp.zeros_like(sk)
            for h2 in range(HKV):
                sl = slice(h2 * DH, (h2 + 1) * DH)
                k2 = jnp.concatenate([kp_ref[:, sl], kc_ref[:, sl]], axis=0).astype(F32)
                v2 = jnp.concatenate([vp_ref[:, sl], vc_ref[:, sl]], axis=0).astype(F32)
                _, vjp = jax.vjp(functools.partial(_swa_group, bias=bias_ref[h2], h2=h2), _swa_split_q(q, h2), k2, v2, sk)
                dqg, dk2, dv2, dsk_h = vjp(_swa_split_q(dov, h2))
                dqs.append(dqg)
                dkp.append(dk2[:WIN]); dkc.append(dk2[WIN:])
                dvp.append(dv2[:WIN]); dvc.append(dv2[WIN:])
                dsk = dsk + dsk_h
            out_ref[:, :QW] = cq[...].astype(BF16)
            out_ref[:, QW:QW + 128] = (ck[...] + jnp.concatenate(dkp, axis=1)).astype(BF16)
            out_ref[:, QW + 128:] = (cv[...] + jnp.concatenate(dvp, axis=1)).astype(BF16)
            cq[...] = _swa_merge_q(dqs)
            ck[...] = jnp.concatenate(dkc, axis=1)
            cv[...] = jnp.concatenate(dvc, axis=1)
            dsk_ref[...] += dsk

        @pl.when(i == NB)
        def _():
            out_ref[:, :QW] = cq[...].astype(BF16)
            out_ref[:, QW:QW + 128] = ck[...].astype(BF16)
            out_ref[:, QW + 128:] = cv[...].astype(BF16)

    qblk = pl.BlockSpec((WIN, QW), lambda i: (jnp.minimum(i, NB - 1), 0))
    return pl.pallas_call(
        body, name=name, grid=(NB + 1,), in_specs=_swa_specs() + [qblk, ANY],
        out_specs=[pl.BlockSpec((WIN, QW + 256), lambda i: (jnp.maximum(i - 1, 0), 0)), pl.BlockSpec((1, 128), lambda i: (0, 0))],
        out_shape=[jax.ShapeDtypeStruct(dproj.shape, dproj.dtype), jax.ShapeDtypeStruct((1, 128), F32)],
        scratch_shapes=[pltpu.VMEM((WIN, QW), F32), pltpu.VMEM((WIN, 128), F32), pltpu.VMEM((WIN, 128), F32)],
        input_output_aliases={8: 0}, compiler_params=_cp("arbitrary"),
    )(proj, proj, proj, proj, proj, sinks, _swa_bias(), do, dproj)


RC = 256
HALO = 8


def _load_ext(ref, c):
    nch = T // RC
    r0 = pl.multiple_of(c * RC, RC)
    p0 = pl.multiple_of(jnp.maximum(r0 - HALO, 0), HALO)
    n0 = pl.multiple_of(jnp.minimum(r0 + RC, T - HALO), HALO)
    prev = jnp.where(c > 0, ref[pl.ds(p0, HALO), :].astype(F32), 0.0)
    nxt = jnp.where(c < nch - 1, ref[pl.ds(n0, HALO), :].astype(F32), 0.0)
    return jnp.concatenate([prev, ref[pl.ds(r0, RC), :].astype(F32), nxt], axis=0)


def _conv_ext(xe, w, K):
    y = w[K - 1:K, :] * xe
    for s in range(1, K):
        y = y + w[K - 1 - s:K - s, :] * pltpu.roll(xe, s, 0)
    return y


def _conv_bwd_ext(xe, dye, w, K, dw_ref):
    n = xe.shape[0]
    own = slice(HALO, HALO + RC)
    dx = w[K - 1:K, :] * dye
    dw_ref[K - 1:K, :] += jnp.sum(dye[own] * xe[own], axis=0, keepdims=True)
    for s in range(1, K):
        dx = dx + w[K - 1 - s:K - s, :] * pltpu.roll(dye, n - s, 0)
        dw_ref[K - 1 - s:K - s, :] += jnp.sum(dye[own] * pltpu.roll(xe, s, 0)[own], axis=0, keepdims=True)
    return dx[own]


def _gdn_post_conv(y, is_qk):
    a = _silu(y)
    nrm = a * lax.rsqrt(jnp.sum(a * a, axis=1, keepdims=True) + EPS)
    return jnp.where(is_qk, nrm, a)


def _gdn_prep_fwd(proj, conv_w, *, name):
    nblk = 3 * GH

    def body(x_ref, w_ref, o_ref):
        is_qk = pl.program_id(0) < 2 * GH
        w = w_ref[...]

        def chunk(c, carry):
            y = _conv_ext(_load_ext(x_ref, c), w, GK)[HALO:HALO + RC]
            o_ref[pl.ds(pl.multiple_of(c * RC, RC), RC), :] = _gdn_post_conv(y, is_qk)
            return carry

        lax.fori_loop(0, T // RC, chunk, 0)

    return pl.pallas_call(
        body, name=name, grid=(nblk,),
        in_specs=[pl.BlockSpec((T, 128), lambda j: (0, COL_G + j)), pl.BlockSpec((GK, 128), lambda j: (0, j))],
        out_specs=pl.BlockSpec((T, 128), lambda j: (0, j)),
        out_shape=jax.ShapeDtypeStruct((T, nblk * 128), F32), compiler_params=_cp("parallel"),
    )(proj, conv_w)


def _gdn_prep_bwd(proj, conv_w, dout, dproj, *, name):
    nblk = 3 * GH

    def body(x_ref, w_ref, d_ref, _, dx_ref, dw_ref):
        is_qk = pl.program_id(0) < 2 * GH
        w = w_ref[...]
        dw_ref[...] = jnp.zeros_like(dw_ref)

        def chunk(c, carry):
            xe = _load_ext(x_ref, c)
            _, vjp = jax.vjp(lambda y: _gdn_post_conv(y, is_qk), _conv_ext(xe, w, GK))
            (dye,) = vjp(_load_ext(d_ref, c))
            dx_ref[pl.ds(pl.multiple_of(c * RC, RC), RC), :] = _conv_bwd_ext(xe, dye, w, GK, dw_ref).astype(BF16)
            return carry

        lax.fori_loop(0, T // RC, chunk, 0)

    return pl.pallas_call(
        body, name=name, grid=(nblk,),
        in_specs=[pl.BlockSpec((T, 128), lambda j: (0, COL_G + j)), pl.BlockSpec((GK, 128), lambda j: (0, j)),
                  pl.BlockSpec((T, 128), lambda j: (0, j)), ANY],
        out_specs=[pl.BlockSpec((T, 128), lambda j: (0, COL_G + j)), pl.BlockSpec((GK, 128), lambda j: (0, j))],
        out_shape=[jax.ShapeDtypeStruct(dproj.shape, dproj.dtype), jax.ShapeDtypeStruct((GK, nblk * 128), F32)],
        input_output_aliases={3: 0}, compiler_params=_cp("parallel"),
    )(proj, conv_w, dout, dproj)


def _ffn_post_conv(y, b, up):
    return _silu(y + b) * up


def _ffn_act_fwd(gu, conv_w, conv_b, *, name):
    nblk = DFF // 128

    def body(g_ref, u_ref, w_ref, b_ref, o_ref):
        w, b = w_ref[...], b_ref[...]

        def chunk(c, carry):
            rows = pl.ds(pl.multiple_of(c * RC, RC), RC)
            y = _conv_ext(_load_ext(g_ref, c), w, FK)[HALO:HALO + RC]
            o_ref[rows, :] = _ffn_post_conv(y, b, u_ref[rows, :].astype(F32)).astype(BF16)
            return carry

        lax.fori_loop(0, T // RC, chunk, 0)

    return pl.pallas_call(
        body, name=name, grid=(nblk,),
        in_specs=[pl.BlockSpec((T, 128), lambda j: (0, j)), pl.BlockSpec((T, 128), lambda j: (0, nblk + j)),
                  pl.BlockSpec((FK, 128), lambda j: (0, j)), pl.BlockSpec((1, 128), lambda j: (0, j))],
        out_specs=pl.BlockSpec((T, 128), lambda j: (0, j)),
        out_shape=jax.ShapeDtypeStruct((T, DFF), BF16), compiler_params=_cp("parallel"),
    )(gu, gu, conv_w, conv_b)


def _ffn_act_bwd(gu, conv_w, conv_b, dact, *, name):
    nblk = DFF // 128

    def body(g_ref, u_ref, w_ref, b_ref, d_ref, dgu_ref, dw_ref, db_ref):
        dg_ref, du_ref = dgu_ref.at[0], dgu_ref.at[1]
        w, b = w_ref[...], b_ref[...]
        dw_ref[...] = jnp.zeros_like(dw_ref)
        db_ref[...] = jnp.zeros_like(db_ref)

        def chunk(c, carry):
            rows = pl.ds(pl.multiple_of(c * RC, RC), RC)
            xe = _load_ext(g_ref, c)
            ue = _load_ext(u_ref, c)
            _, vjp = jax.vjp(_ffn_post_conv, _conv_ext(xe, w, FK), b, ue)
            dye, db, due = vjp(_load_ext(d_ref, c))
            du_ref[rows, :] = due[HALO:HALO + RC].astype(BF16)
            db_ref[...] += jnp.sum(dye[HALO:HALO + RC], axis=0, keepdims=True)
            dg_ref[rows, :] = _conv_bwd_ext(xe, dye, w, FK, dw_ref).astype(BF16)
            return carry

        lax.fori_loop(0, T // RC, chunk, 0)

    col = pl.BlockSpec((T, 128), lambda j: (0, j))
    return pl.pallas_call(
        body, name=name, grid=(nblk,),
        in_specs=[col, pl.BlockSpec((T, 128), lambda j: (0, nblk + j)), pl.BlockSpec((FK, 128), lambda j: (0, j)),
                  pl.BlockSpec((1, 128), lambda j: (0, j)), col],
        out_specs=[pl.BlockSpec((2, T, 128), lambda j: (0, 0, j)), pl.BlockSpec((FK, 128), lambda j: (0, j)),
                   pl.BlockSpec((1, 128), lambda j: (0, j))],
        out_shape=[jax.ShapeDtypeStruct((2, T, DFF), BF16), jax.ShapeDtypeStruct((FK, DFF), F32),
                   jax.ShapeDtypeStruct((1, DFF), F32)],
        compiler_params=_cp("parallel"),
    )(gu, gu, conv_w, conv_b, dact)


def _gdn_gates(ba, alog, dtb):
    beta = jax.nn.sigmoid(ba)
    g = -jnp.exp(alog) * jax.nn.softplus(ba + dtb)
    tril = (_iota((GC, GC), 0) >= _iota((GC, GC), 1)).astype(F32)
    return beta, jnp.dot(tril, g, precision=HIGHEST, preferred_element_type=F32)


def _hmap(f, *lists):
    return [f(*xs) for xs in zip(*lists)]


def _gdn_cols(beta_all, gc_all):
    return [_col(beta_all, h) for h in range(GH)], [_col(gc_all, GH + h) for h in range(GH)]


def _gdn_decay(Gcs):
    r, c = _iota((GC, GC), 0), _iota((GC, GC), 1)
    eye, ones = (r == c).astype(F32), jnp.ones((GC, GC), F32)
    grows = _hmap(lambda G: jnp.dot(ones, eye * G, precision=HIGHEST, preferred_element_type=F32), Gcs)
    return _hmap(lambda G, grow: jnp.exp(jnp.where(r >= c, G - grow, -1e30)), Gcs, grows)


def _gdn_A(ks, betas, Gcs):
    strict = _iota((GC, GC), 0) > _iota((GC, GC), 1)
    kk = _hmap(lambda k, b: _dot_nt(k * b, k), ks, betas)
    return _hmap(lambda a, d: jnp.where(strict, a * d, 0.0), kk, _gdn_decay(Gcs))


def _tri_inv(As):
    eye = (_iota((GC, GC), 0) == _iota((GC, GC), 1)).astype(F32)
    Tms, Ps = [eye - A for A in As], As
    for _ in range(GC.bit_length() - 2):
        Ps = _hmap(lambda P: _dot(P, P), Ps)
        Tms = _hmap(lambda Tm, P: Tm + _dot(Tm, P), Tms, Ps)
    return Tms


def _gdn_chunk(qs, ks, vs, betas, Gcs, Ss, Tms):
    decays = _gdn_decay(Gcs)
    eGs = _hmap(jnp.exp, Gcs)
    us = _hmap(lambda Tm, v, b: _dot(Tm, v * b), Tms, vs, betas)
    ws = _hmap(lambda Tm, k, b, eG: _dot(Tm, k * b * eG), Tms, ks, betas, eGs)
    qss = [q * (GD ** -0.5) for q in qs]
    qks = _hmap(lambda q, k, d: _dot_nt(q, k) * d, qss, ks, decays)
    glasts = [jnp.sum(jnp.where(_iota(G.shape, 0) == GC - 1, G, 0.0), axis=0, keepdims=True) for G in Gcs]
    kds = _hmap(lambda k, gl, G: k * jnp.exp(gl - G), ks, glasts, Gcs)
    v_news = _hmap(lambda u, w, S: u - _dot(w, S), us, ws, Ss)
    qS = _hmap(lambda q, eG, S: _dot(q * eG, S), qss, eGs, Ss)
    os = _hmap(lambda a, qk, vn: a + _dot(qk, vn), qS, qks, v_news)
    S_news = _hmap(lambda S, gl, kd, vn: S * jnp.exp(gl) + _dot_tn(kd, vn), Ss, glasts, kds, v_news)
    return os, S_news


def _gdn_chunk_fwd(qkv_c, ba, alog, dtb, *, name):
    W3 = 3 * GH * GD

    def body(x_ref, ba_ref, al_ref, dt_ref, o_ref, s_ref, t_ref, S):
        @pl.when(pl.program_id(0) == 0)
        def _():
            S[...] = jnp.zeros_like(S)

        beta_all, gc_all = _gdn_gates(ba_ref[...], al_ref[...], dt_ref[...])
        qs, ks, vs = ([x_ref[:, (p * GH + h) * GD:(p * GH + h + 1) * GD] for h in range(GH)] for p in range(3))
        betas, Gcs = _gdn_cols(beta_all, gc_all)
        Tms = _tri_inv(_gdn_A(ks, betas, Gcs))
        Ss = [S[h] for h in range(GH)]
        os, S_news = _gdn_chunk(qs, ks, vs, betas, Gcs, Ss, Tms)
        for h in range(GH):
            s_ref[0, h] = Ss[h]
            t_ref[0, h] = Tms[h]
            o_ref[:, h * GD:(h + 1) * GD] = os[h]
            S[h] = S_news[h]

    vec = pl.BlockSpec((1, 128), lambda n: (0, 0))
    return pl.pallas_call(
        body, name=name, grid=(N_CHUNK,),
        in_specs=[pl.BlockSpec((GC, W3), lambda n: (n, 0)), pl.BlockSpec((GC, 128), lambda n: (n, 0)), vec, vec],
        out_specs=[pl.BlockSpec((GC, GH * GD), lambda n: (n, 0)),
                   pl.BlockSpec((1, GH, GD, GD), lambda n: (n, 0, 0, 0)),
                   pl.BlockSpec((1, GH, GC, GC), lambda n: (n, 0, 0, 0))],
        out_shape=[jax.ShapeDtypeStruct((T, GH * GD), F32), jax.ShapeDtypeStruct((N_CHUNK, GH, GD, GD), F32),
                   jax.ShapeDtypeStruct((N_CHUNK, GH, GC, GC), F32)],
        scratch_shapes=[pltpu.VMEM((GH, GD, GD), F32)], compiler_params=_cp("arbitrary"),
    )(qkv_c, ba, alog, dtb)


def _gdn_chunk_bwd(qkv_c, ba, alog, dtb, s_all, t_all, do, dproj, *, name):
    W3 = 3 * GH * GD
    rev = lambda n: N_CHUNK - 1 - n

    def body(x_ref, ba_ref, al_ref, dt_ref, s_ref, t_ref, do_ref, _, dx_ref, dba_ref, dal_ref, ddt_ref, dS):
        @pl.when(pl.program_id(0) == 0)
        def _():
            dS[...] = jnp.zeros_like(dS)
            dal_ref[...] = jnp.zeros_like(dal_ref)
            ddt_ref[...] = jnp.zeros_like(ddt_ref)

        (beta_all, gc_all), vjp_gates = jax.vjp(_gdn_gates, ba_ref[...], al_ref[...], dt_ref[...])
        qs, ks, vs = ([x_ref[:, (p * GH + h) * GD:(p * GH + h + 1) * GD] for h in range(GH)] for p in range(3))
        Tms = [t_ref[0, h] for h in range(GH)]
        Ss = [s_ref[0, h] for h in range(GH)]
        dos = [do_ref[:, h * GD:(h + 1) * GD] for h in range(GH)]
        dSs = [dS[h] for h in range(GH)]

        def f(qs, ks, vs, b_all, g_all, Ss, Tms):
            return _gdn_chunk(qs, ks, vs, *_gdn_cols(b_all, g_all), Ss, Tms)

        def fa(ks, b_all, g_all):
            return _gdn_A(ks, *_gdn_cols(b_all, g_all))

        _, vjp = jax.vjp(f, qs, ks, vs, beta_all, gc_all, Ss, Tms)
        dqs, dks, dvs, db1, dg1, dS_prev, dTs = vjp((dos, dSs))
        dXs = _hmap(_dot_nt, dTs, Tms)
        dAs = _hmap(lambda Tm, dX: -_dot_tn(Tm, dX), Tms, dXs)
        _, vjp_a = jax.vjp(fa, ks, beta_all, gc_all)
        dks2, db2, dg2 = vjp_a(dAs)
        for h in range(GH):
            dS[h] = dS_prev[h]
            dx_ref[:, h * GD:(h + 1) * GD] = dqs[h]
            dx_ref[:, (GH + h) * GD:(GH + h + 1) * GD] = dks[h] + dks2[h]
            dx_ref[:, (2 * GH + h) * GD:(2 * GH + h + 1) * GD] = dvs[h]
        dba, dal, ddt = vjp_gates((db1 + db2, dg1 + dg2))
        dba_ref[...] = dba.astype(BF16)
        dal_ref[...] += dal
        ddt_ref[...] += ddt

    vec = pl.BlockSpec((1, 128), lambda n: (0, 0))
    return pl.pallas_call(
        body, name=name, grid=(N_CHUNK,),
        in_specs=[pl.BlockSpec((GC, W3), lambda n: (rev(n), 0)), pl.BlockSpec((GC, 128), lambda n: (rev(n), 0)),
                  vec, vec, pl.BlockSpec((1, GH, GD, GD), lambda n: (rev(n), 0, 0, 0)),
                  pl.BlockSpec((1, GH, GC, GC), lambda n: (rev(n), 0, 0, 0)),
                  pl.BlockSpec((GC, GH * GD), lambda n: (rev(n), 0)),
                  ANY],
        out_specs=[pl.BlockSpec((GC, W3), lambda n: (rev(n), 0)), pl.BlockSpec((GC, 128), lambda n: (rev(n), COL_BA)), vec, vec],
        out_shape=[jax.ShapeDtypeStruct((T, W3), F32), jax.ShapeDtypeStruct(dproj.shape, dproj.dtype),
                   jax.ShapeDtypeStruct((1, 128), F32), jax.ShapeDtypeStruct((1, 128), F32)],
        scratch_shapes=[pltpu.VMEM((GH, GD, GD), F32)], input_output_aliases={7: 1}, compiler_params=_cp("arbitrary"),
    )(qkv_c, ba, alog, dtb, s_all, t_all, do, dproj)


def _gdn_post(o, z, nw):
    return o * lax.rsqrt(jnp.mean(o * o, axis=-1, keepdims=True) + EPS) * nw * _silu(z)


def _gdn_post_fwd(o_raw, proj, nw, mixed, *, name, tm=512):
    def body(o_ref, z_ref, w_ref, _, out_ref):
        out_ref[...] = _gdn_post(o_ref[...], z_ref[...].astype(F32), w_ref[...]).astype(BF16)

    return pl.pallas_call(
        body, name=name, grid=(T // tm, GH),
        in_specs=[pl.BlockSpec((tm, GD), lambda i, h: (i, h)), pl.BlockSpec((tm, GD), lambda i, h: (i, COL_Z + h)),
                  pl.BlockSpec((1, GD), lambda i, h: (0, 0)), ANY],
        out_specs=pl.BlockSpec((tm, GD), lambda i, h: (i, HQ * DH // GD + h)),
        out_shape=jax.ShapeDtypeStruct(mixed.shape, mixed.dtype), input_output_aliases={3: 0},
        compiler_params=_cp("parallel", "parallel"),
    )(o_raw, proj, nw, mixed)


def _gdn_post_bwd(o_raw, proj, nw, dmixed, *, name, tm=512):
    def body(o_ref, z_ref, w_ref, d_ref, do_ref, dz_ref, dw_ref):
        _, vjp = jax.vjp(_gdn_post, o_ref[...], z_ref[...].astype(F32), w_ref[...])
        do, dz, dw = vjp(d_ref[...])
        do_ref[...] = do
        dz_ref[...] = dz.astype(BF16)

        @pl.when((pl.program_id(0) == 0) & (pl.program_id(1) == 0))
        def _():
            dw_ref[...] = jnp.zeros_like(dw_ref)

        dw_ref[...] += dw

    blk = pl.BlockSpec((tm, GD), lambda i, h: (i, h))
    vec = pl.BlockSpec((1, GD), lambda i, h: (0, 0))
    return pl.pallas_call(
        body, name=name, grid=(T // tm, GH),
        in_specs=[blk, pl.BlockSpec((tm, GD), lambda i, h: (i, COL_Z + h)), vec,
                  pl.BlockSpec((tm, GD), lambda i, h: (i, GH + h))],
        out_specs=[blk, pl.BlockSpec((tm, GD), lambda i, h: (i, COL_Z + h)), vec],
        out_shape=[jax.ShapeDtypeStruct((T, GH * GD), F32), jax.ShapeDtypeStruct((T, N_PROJ), BF16),
                   jax.ShapeDtypeStruct((1, GD), F32)],
        compiler_params=_cp("arbitrary", "arbitrary"),
    )(o_raw, proj, nw, dmixed)


def _adamw(w, g, m, v, *, name):
    shape = w.shape
    cols = shape[-1]
    w2, g2, m2, v2 = (a.reshape(-1, cols) for a in (w, g, m, v))
    rows = w2.shape[0]
    tr = next((t for t in (512, 256, 128, 64, 32, 16, 8) if rows % t == 0), rows)

    def body(w_ref, g_ref, m_ref, v_ref, d_ref, nm_ref, nv_ref):
        gv = g_ref[...]
        nm = B1 * m_ref[...] + (1.0 - B1) * gv
        nv = B2 * v_ref[...] + (1.0 - B2) * jnp.square(gv)
        m_hat = nm / (1.0 - B1 ** STEP)
        v_hat = nv / (1.0 - B2 ** STEP)
        d_ref[...] = -LR * (m_hat / (jnp.sqrt(v_hat) + AEPS) + WD * w_ref[...])
        nm_ref[...] = nm
        nv_ref[...] = nv

    blk = pl.BlockSpec((tr, cols), lambda i: (i, 0))
    out = pl.pallas_call(
        body, name=name, grid=(rows // tr,), in_specs=[blk] * 4, out_specs=[blk] * 3,
        out_shape=[jax.ShapeDtypeStruct((rows, cols), F32)] * 3, compiler_params=_cp("parallel"),
    )(w2, g2, m2, v2)
    return tuple(o.reshape(shape) for o in out)


def _layer_weights(w_in, w_out, w_ffn, w_down):
    w_all = jnp.pad(w_in, ((0, 0), (0, N_PROJ - w_in.shape[1])))
    return dict(all=w_all, out=w_out, ffn=w_ffn, down=w_down)


def _layer_params(attn_norm, sinks, gcw, a_log, dt_bias, gnw, ffn_norm, fcw, fcb):
    lanes4 = lambda v: jnp.pad(v, (GH, 128 - 2 * GH))[None]
    return dict(attn_norm=attn_norm[None], sinks=jnp.pad(sinks, (0, 128 - HQ))[None], gcw=gcw, alog=lanes4(a_log),
                dtb=lanes4(dt_bias), gnw=gnw[None], ffn_norm=ffn_norm[None], fcw=fcw, fcb=fcb[None])


def _layer_fwd(x, W, P, l):
    n = lambda s: f"l{l}_{s}"
    h = _rmsnorm_fwd(x, P["attn_norm"], name=n("norm1"))
    proj = _mm(h, W["all"], name=n("proj"), mode="nn", dims=(T, N_MAIN, D), tm=512, tn=N_MAIN, tk=D, out_dtype=BF16)
    ba = _mm(h, W["all"], name=n("proj_ba"), mode="nn", dims=(T, 128, D), tm=1024, tn=128, tk=D,
             b_spec=pl.BlockSpec((D, 128), lambda i, j, k: (k, COL_BA)))
    mixed = _swa_fwd(proj, P["sinks"], name=n("swa"))
    qkv_c = _gdn_prep_fwd(proj, P["gcw"], name=n("gdn_prep"))
    o_raw, s_all, t_all = _gdn_chunk_fwd(qkv_c, ba, P["alog"], P["dtb"], name=n("gdn_chunk"))
    mixed = _gdn_post_fwd(o_raw, proj, P["gnw"], mixed, name=n("gdn_post"))
    x1 = _mm(mixed, W["out"], res=x, name=n("out_proj"), mode="nn", dims=(T, D, D), tm=512, tn=D, tk=D)
    h2 = _rmsnorm_fwd(x1, P["ffn_norm"], name=n("norm2"))
    gu = _mm(h2, W["ffn"], name=n("ffn_in"), mode="nn", dims=(T, 2 * DFF, D), tm=512, tn=FFN_CW, tk=D, out_dtype=BF16,
             b_spec=pl.BlockSpec((None, D, FFN_CW), lambda i, j, k: (j, k, 0)), n_outer=True)
    act = _ffn_act_fwd(gu, P["fcw"], P["fcb"], name=n("ffn_act"))
    x2 = _mm(act, W["down"], res=x1, name=n("ffn_down"), mode="nn", dims=(T, D, DFF), tm=512, tn=D, tk=DFF)
    saved = dict(x=x, h=h, proj=proj, ba=ba, qkv_c=qkv_c, o_raw=o_raw, s_all=s_all, t_all=t_all, mixed=mixed, x1=x1,
                 h2=h2, gu=gu, act=act)
    return x2, saved


def _layer_bwd(dx2, sv, W, P, l):
    n = lambda s: f"l{l}_{s}"
    CW = FFN_CW
    dact = _mm(dx2, W["down"], name=n("d_act"), mode="nt", dims=(T, DFF, D), tm=512, tn=DFF // 2, tk=D, out_dtype=BF16,
               n_outer=True)
    g_down = _mm(sv["act"], dx2, name=n("g_down"), mode="tn", dims=(DFF, D, T), tm=DFF // 2, tn=D, tk=1024, out_dtype=BF16)
    dgu, g_fcw, g_fcb = _ffn_act_bwd(sv["gu"], P["fcw"], P["fcb"], dact, name=n("d_ffn_act"))
    dh2 = _mm(dgu, W["ffn"], name=n("d_h2"), mode="nt", dims=(T, D, 2 * DFF), tm=1024, tn=D, tk=CW,
              a_spec=pl.BlockSpec((None, 1024, CW), lambda i, j, k: (k // 2, i, k % 2)),
              b_spec=pl.BlockSpec((None, D, CW), lambda i, j, k: (k, j, 0)))
    g_ffn = _mm(sv["h2"], dgu, name=n("g_ffn"), mode="tn", dims=(D, 2 * DFF, T), tm=D, tn=CW, tk=1024,
                b_spec=pl.BlockSpec((None, 1024, CW), lambda i, j, k: (j // 2, k, j % 2)),
                out_spec=pl.BlockSpec((None, D, CW), lambda i, j, k: (j, i, 0)),
                out_shape=jax.ShapeDtypeStruct((N_CHIP, D, CW), BF16), out_dtype=BF16)
    dx1, g_ffn_norm = _rmsnorm_bwd(sv["x1"], P["ffn_norm"], dh2, dx2, name=n("d_norm2"))
    dmixed = _mm(dx1, W["out"], name=n("d_mixed"), mode="nt", dims=(T, D, D), tm=512, tn=D, tk=D)
    g_out = _mm(sv["mixed"], dx1, name=n("g_out"), mode="tn", dims=(D, D, T), tm=D, tn=D, tk=1024, out_dtype=BF16)
    do_raw, dproj, g_gnw = _gdn_post_bwd(sv["o_raw"], sv["proj"], P["gnw"], dmixed, name=n("d_gdn_post"))
    dqkv_c, dproj, g_alog, g_dtb = _gdn_chunk_bwd(sv["qkv_c"], sv["ba"], P["alog"], P["dtb"], sv["s_all"], sv["t_all"],
                                                   do_raw, dproj, name=n("d_gdn_chunk"))
    dproj, g_gcw = _gdn_prep_bwd(sv["proj"], P["gcw"], dqkv_c, dproj, name=n("d_gdn_prep"))
    dproj, g_sinks = _swa_bwd(sv["proj"], P["sinks"], dmixed, dproj, name=n("d_swa"))
    dh = _mm(dproj, W["all"], name=n("d_h"), mode="nt", dims=(T, D, N_PROJ), tm=512, tn=D, tk=N_PROJ)
    g_all = _mm(sv["h"], dproj, name=n("g_in"), mode="tn", dims=(D, N_PROJ, T), tm=D, tn=N_PROJ, tk=1024, out_dtype=BF16)
    dx, g_attn_norm = _rmsnorm_bwd(sv["x"], P["attn_norm"], dh, dx1, name=n("d_norm1"))
    grads = dict(
        attn_norm=g_attn_norm[0], w_in=g_all, attn_sinks=g_sinks[0, :HQ], gdn_conv_w=g_gcw,
        gdn_a_log=g_alog[0, GH:2 * GH], gdn_dt_bias=g_dtb[0, GH:2 * GH], gdn_norm=g_gnw[0], w_out=g_out,
        ffn_norm=g_ffn_norm[0], w_ffn_in=g_ffn, ffn_conv_w=g_fcw, ffn_conv_b=g_fcb[0], w_down=g_down)
    return dx, grads


def _pos():
    return lax.axis_index("x"), lax.axis_index("y"), lax.axis_index("c")


def _other_chips(x, y):
    return [(1 - x, y), (x, 1 - y), (1 - x, 1 - y)]


def _remote(src, dst, send_sems, recv_sems, k, to):
    return pltpu.make_async_remote_copy(src_ref=src, dst_ref=dst, send_sem=send_sems.at[k], recv_sem=recv_sems.at[k],
                                        device_id=to, device_id_type=MESH)


def _ag_weights(shards, *, name):
    nt = len(shards)

    def body(*refs):
        srcs, outs = refs[:nt], refs[nt:2 * nt]
        send_sems, recv_sems = refs[2 * nt:]
        x, y, c = _pos()
        me, sibling, chip = (x, y, c), (x, y, 1 - c), 2 * x + y
        others = [(k, px, py, 2 * px + py) for k, (px, py) in enumerate(_other_chips(x, y))]
        first = [_remote(srcs[t].at[c], outs[t].at[c, chip], send_sems, recv_sems, 3 * t + k, (px, py, c))
                 for t in range(nt) for k, px, py, _ in others]
        for cp in first:
            cp.start()
        passed = []
        for t in range(nt):
            for k, _, _, j in others:
                slot = outs[t].at[c, j]
                _remote(slot, slot, send_sems, recv_sems, 3 * t + k, me).wait_recv()
                passed.append(_remote(slot, slot, send_sems, recv_sems, 3 * (nt + t) + k, sibling))
                passed[-1].start()
        for t in range(nt):
            for k, _, _, j in others:
                slot = outs[t].at[1 - c, j]
                _remote(slot, slot, send_sems, recv_sems, 3 * (nt + t) + k, me).wait_recv()
        for cp in first + passed:
            cp.wait_send()

    outs = pl.pallas_call(
        body, name=name, in_specs=[ANY] * nt, out_specs=[ANY] * nt,
        out_shape=[jax.ShapeDtypeStruct((DEPTH, N_CHIP) + s.shape[1:], s.dtype) for s in shards],
        scratch_shapes=[pltpu.SemaphoreType.DMA((6 * nt,)), pltpu.SemaphoreType.DMA((6 * nt,))],
    )(*shards)
    chip = 2 * lax.axis_index("x") + lax.axis_index("y")
    return [lax.dynamic_update_index_in_dim(o, s, chip, 1) for o, s in zip(outs, shards)]


def _ag_small(v, *, name):
    m, n = v.shape

    def body(x_ref, out_ref, red_ref, send_sems, recv_sems, local_sem):
        x, y, c = _pos()
        me, sibling = (x, y, c), (x, y, 1 - c)
        chips = _other_chips(x, y)
        rows = lambda px, py, pc: out_ref.at[pl.ds(pl.multiple_of((4 * px + 2 * py + pc) * m, 8), m), :]
        mine = pltpu.make_async_copy(x_ref, rows(*me), local_sem)
        mine.start()
        first = [_remote(x_ref, rows(*me), send_sems, recv_sems, 0, sibling)]
        first += [_remote(x_ref, rows(*me), send_sems, recv_sems, 1 + k, (*chip, c)) for k, chip in enumerate(chips)]
        for cp in first:
            cp.start()
        passed = [_remote(rows(*chip, c), rows(*chip, c), send_sems, recv_sems, 4 + k, sibling) for k, chip in enumerate(chips)]
        for k, chip in enumerate(chips):
            _remote(rows(*chip, c), rows(*chip, c), send_sems, recv_sems, 1 + k, me).wait_recv()
            passed[k].start()
        _remote(rows(*sibling), rows(*sibling), send_sems, recv_sems, 0, me).wait_recv()
        for k, chip in enumerate(chips):
            _remote(rows(*chip, 1 - c), rows(*chip, 1 - c), send_sems, recv_sems, 4 + k, me).wait_recv()
        for cp in first + passed:
            cp.wait_send()
        mine.wait()
        acc = out_ref[0:m, :]
        for d in range(1, N_DEV):
            acc = acc + out_ref[d * m:(d + 1) * m, :]
        red_ref[...] = acc

    return pl.pallas_call(
        body, name=name, in_specs=[VMEM_SPEC], out_specs=[VMEM_SPEC, VMEM_SPEC],
        out_shape=[jax.ShapeDtypeStruct((N_DEV * m, n), v.dtype), jax.ShapeDtypeStruct((m, n), v.dtype)],
        scratch_shapes=[pltpu.SemaphoreType.DMA((7,)), pltpu.SemaphoreType.DMA((7,)), pltpu.SemaphoreType.DMA],
    )(v)


def _halves(ref, c):
    rh = ref.shape[1] // 2
    return ref.at[:, pl.ds(pl.multiple_of(c * rh, 16), rh), :]


def _rs_swap(gs, *, name):
    nt = len(gs)

    def body(*refs):
        g, theirs = refs[:nt], refs[nt:2 * nt]
        send_sems, recv_sems = refs[2 * nt:]
        x, y, c = _pos()
        swaps = [_remote(_halves(g[t], 1 - c), theirs[t], send_sems, recv_sems, t, (x, y, 1 - c)) for t in range(nt)]
        for cp in swaps:
            cp.start()
        for cp in swaps:
            cp.wait()

    return pl.pallas_call(
        body, name=name, in_specs=[ANY] * nt, out_specs=[ANY] * nt,
        out_shape=[jax.ShapeDtypeStruct((a.shape[0], a.shape[1] // 2, a.shape[2]), a.dtype) for a in gs],
        scratch_shapes=[pltpu.SemaphoreType.DMA((nt,)), pltpu.SemaphoreType.DMA((nt,))],
    )(*gs)


def _rs_exchange(ss, *, name):
    nt = len(ss)

    def body(*refs):
        s, out = refs[:nt], refs[nt:2 * nt]
        send_sems, recv_sems = refs[2 * nt:]
        x, y, c = _pos()
        chip = 2 * x + y
        others = [(k, px, py, 2 * px + py) for k, (px, py) in enumerate(_other_chips(x, y))]
        sends = [_remote(s[t].at[j], out[t].at[chip], send_sems, recv_sems, 3 * t + k, (px, py, c))
                 for t in range(nt) for k, px, py, j in others]
        for cp in sends:
            cp.start()
        for t in range(nt):
            for k, _, _, j in others:
                _remote(s[t].at[j], out[t].at[j], send_sems, recv_sems, 3 * t + k, (x, y, c)).wait_recv()
        for cp in sends:
            cp.wait_send()

    outs = pl.pallas_call(
        body, name=name, in_specs=[ANY] * nt, out_specs=[ANY] * nt,
        out_shape=[jax.ShapeDtypeStruct(a.shape, a.dtype) for a in ss],
        scratch_shapes=[pltpu.SemaphoreType.DMA((3 * nt,)), pltpu.SemaphoreType.DMA((3 * nt,))],
    )(*ss)
    chip = 2 * lax.axis_index("x") + lax.axis_index("y")
    return [lax.dynamic_update_index_in_dim(o, lax.dynamic_index_in_dim(s, chip, 0, keepdims=False), chip, 0)
            for o, s in zip(outs, ss)]


def _rs_join(rs, *, name):
    nt = len(rs)

    def body(*refs):
        r, theirs = refs[:nt], refs[nt:2 * nt]
        send_sems, recv_sems = refs[2 * nt:]
        x, y, c = _pos()
        swaps = [_remote(r[t], theirs[t], send_sems, recv_sems, t, (x, y, 1 - c)) for t in range(nt)]
        for cp in swaps:
            cp.start()
        for cp in swaps:
            cp.wait()

    theirs = pl.pallas_call(
        body, name=name, in_specs=[ANY] * nt, out_specs=[ANY] * nt,
        out_shape=[jax.ShapeDtypeStruct(a.shape, a.dtype) for a in rs],
        scratch_shapes=[pltpu.SemaphoreType.DMA((nt,)), pltpu.SemaphoreType.DMA((nt,))],
    )(*rs)
    first = lax.axis_index("c") == 0
    return [jnp.concatenate([jnp.where(first, a, b), jnp.where(first, b, a)], axis=0) for a, b in zip(rs, theirs)]


def _row_tile(rows, dtype):
    unit = 16 if dtype == BF16 else 8
    if rows <= 512:
        return rows
    return next((t for t in (512, 256, 128, 64, 32, 16, 8) if rows % t == 0 and t % unit == 0), rows)


def _add_pair(a, b, *, name):
    n, rh, cols = a.shape
    tr = _row_tile(rh, BF16)

    def body(a_ref, b_ref, o_ref):
        o_ref[...] = (a_ref[...].astype(F32) + b_ref[...].astype(F32)).astype(BF16)

    blk = pl.BlockSpec((None, tr, cols), lambda j, i: (j, i, 0))
    return pl.pallas_call(
        body, name=name, grid=(n, rh // tr), in_specs=[blk, blk], out_specs=blk,
        out_shape=jax.ShapeDtypeStruct(a.shape, BF16), compiler_params=_cp("parallel", "parallel"),
    )(a, b)


def _sum_chips(b, *, name):
    n, rh, cols = b.shape
    tr = _row_tile(rh, BF16)

    def body(b_ref, o_ref):
        acc = b_ref[0].astype(F32)
        for j in range(1, n):
            acc = acc + b_ref[j].astype(F32)
        o_ref[...] = acc

    return pl.pallas_call(
        body, name=name, grid=(rh // tr,), in_specs=[pl.BlockSpec((n, tr, cols), lambda i: (0, i, 0))],
        out_specs=pl.BlockSpec((tr, cols), lambda i: (i, 0)),
        out_shape=jax.ShapeDtypeStruct((rh, cols), F32), compiler_params=_cp("parallel"),
    )(b)


BIG = ("w_in", "w_out", "w_ffn_in", "w_down")
W_IN_SHARD = (N_MAIN + 2 * GH) // N_CHIP


def _reduce_scatter_layer(lg, l):
    g_in = jnp.stack([lg["w_in"][:, j * W_IN_SHARD:(j + 1) * W_IN_SHARD] for j in range(N_CHIP)])
    gs = [g_in, lg["w_out"].reshape(N_CHIP, D // N_CHIP, D), lg["w_ffn_in"], lg["w_down"].reshape(N_CHIP, DFF // N_CHIP, D)]
    c = lax.axis_index("c")
    mine = [lax.dynamic_slice_in_dim(a, c * (a.shape[1] // 2), a.shape[1] // 2, axis=1) for a in gs]
    theirs = _rs_swap(gs, name=f"rs{l}_swap")
    pairs = [_add_pair(a, b, name=f"rs{l}_add_{n}") for a, b, n in zip(mine, theirs, BIG)]
    got = _rs_exchange(pairs, name=f"rs{l}_exchange")
    return _rs_join([_sum_chips(b, name=f"rs{l}_sum_{n}") for b, n in zip(got, BIG)], name=f"rs{l}_join")


def _size(shape):
    n = 1
    for s in shape:
        n *= s
    return n


def _pack_flat(parts, total, dtype):
    flat = jnp.concatenate([p.reshape(-1).astype(dtype) for p in parts])
    return jnp.pad(flat, (0, total - flat.shape[0]))


def _unpack_flat(flat, shapes):
    out, o = [], 0
    for s in shapes:
        out.append(flat[o:o + _size(s)].reshape(s))
        o += _size(s)
    return out


WEIGHTS = ("attn_norm", "w_in", "attn_sinks", "gdn_conv_w", "gdn_a_log", "gdn_dt_bias", "gdn_norm", "w_out", "ffn_norm",
           "w_ffn_in", "ffn_conv_w", "ffn_conv_b", "w_down", "final_norm")
SMALL = {"attn_norm": (DEPTH, D), "attn_sinks": (DEPTH, HQ), "gdn_a_log": (DEPTH, GH), "gdn_dt_bias": (DEPTH, GH),
         "gdn_norm": (DEPTH, GD), "ffn_norm": (DEPTH, D), "ffn_conv_b": (DEPTH, DFF), "final_norm": (D,)}
CONV_FULL = {"gdn_conv_w": (DEPTH, GK, 3 * GH * GD), "ffn_conv_w": (DEPTH, FK, DFF)}
CONV_SHARD = {"gdn_conv_w": (DEPTH, GK, 3 * GH * GD // N_CHIP), "ffn_conv_w": (DEPTH, FK, DFF // N_CHIP)}
CONV_ROWS, SMALLG_ROWS, SMALLW_ROWS = 64, 320, 144


def kernel(x, attn_norm, w_in, attn_sinks, gdn_conv_w, gdn_a_log, gdn_dt_bias, gdn_norm, w_out, ffn_norm, w_ffn_in, ffn_conv_w, ffn_conv_b, w_down, final_norm, loss_target, m_attn_norm, m_w_in, m_attn_sinks, m_gdn_conv_w, m_gdn_a_log, m_gdn_dt_bias, m_gdn_norm, m_w_out, m_ffn_norm, m_w_ffn_in, m_ffn_conv_w, m_ffn_conv_b, m_w_down, m_final_norm, v_attn_norm, v_w_in, v_attn_sinks, v_gdn_conv_w, v_gdn_a_log, v_gdn_dt_bias, v_gdn_norm, v_w_out, v_ffn_norm, v_w_ffn_in, v_ffn_conv_w, v_ffn_conv_b, v_w_down, v_final_norm):
    w = dict(zip(WEIGHTS, (attn_norm, w_in, attn_sinks, gdn_conv_w, gdn_a_log, gdn_dt_bias, gdn_norm, w_out, ffn_norm,
                           w_ffn_in, ffn_conv_w, ffn_conv_b, w_down, final_norm)))
    m = dict(zip(WEIGHTS, (m_attn_norm, m_w_in, m_attn_sinks, m_gdn_conv_w, m_gdn_a_log, m_gdn_dt_bias, m_gdn_norm, m_w_out,
                           m_ffn_norm, m_w_ffn_in, m_ffn_conv_w, m_ffn_conv_b, m_w_down, m_final_norm)))
    v = dict(zip(WEIGHTS, (v_attn_norm, v_w_in, v_attn_sinks, v_gdn_conv_w, v_gdn_a_log, v_gdn_dt_bias, v_gdn_norm, v_w_out,
                           v_ffn_norm, v_w_ffn_in, v_ffn_conv_w, v_ffn_conv_b, v_w_down, v_final_norm)))
    cx, cy, _ = _pos()
    chip = 2 * cx + cy

    gath = dict(zip(BIG, _ag_weights([w[n].astype(BF16) for n in BIG], name="gather_weights")))
    full = {"w_in": jnp.concatenate([gath["w_in"][:, j] for j in range(N_CHIP)], axis=2),
            "w_out": gath["w_out"].reshape(DEPTH, D, D), "w_ffn_in": gath["w_ffn_in"],
            "w_down": gath["w_down"].reshape(DEPTH, DFF, D)}
    cpack = _pack_flat([w[n] for n in CONV_SHARD], CONV_ROWS * 128, F32).reshape(CONV_ROWS, 128)
    cgath, _ = _ag_small(cpack, name="gather_conv_w")
    cgath = cgath.reshape(N_DEV, CONV_ROWS * 128)
    cshards = [_unpack_flat(cgath[2 * j], list(CONV_SHARD.values())) for j in range(N_CHIP)]
    conv = {n: jnp.concatenate([cshards[j][i] for j in range(N_CHIP)], axis=2) for i, n in enumerate(CONV_SHARD)}

    Ws = [_layer_weights(full["w_in"][l], full["w_out"][l], full["w_ffn_in"][l], full["w_down"][l]) for l in range(DEPTH)]
    Ps = [_layer_params(attn_norm[l], attn_sinks[l], conv["gdn_conv_w"][l], gdn_a_log[l], gdn_dt_bias[l], gdn_norm[l],
                        ffn_norm[l], conv["ffn_conv_w"][l], ffn_conv_b[l]) for l in range(DEPTH)]

    h, saved = x[0], []
    for l in range(DEPTH):
        h, sv = _layer_fwd(h, Ws[l], Ps[l], l)
        saved.append(sv)
    dx, g_final, loss_part = _loss_head(h, final_norm[None], loss_target[0], name="loss_head")
    lg, reduced = [None] * DEPTH, [None] * DEPTH
    for l in reversed(range(DEPTH)):
        dx, lg[l] = _layer_bwd(dx, saved[l], Ws[l], Ps[l], l)
        reduced[l] = _reduce_scatter_layer(lg[l], l)
    grad_x = dx[None]
    stacked = lambda n: jnp.stack([lg[l][n] for l in range(DEPTH)])

    small_parts = [g_final[0] if n == "final_norm" else stacked(n) for n in SMALL] + [stacked(n) for n in CONV_FULL]
    svec = _pack_flat(small_parts + [loss_part[0, :1]], SMALLG_ROWS * 128, F32).reshape(SMALLG_ROWS, 128)
    _, sred = _ag_small(svec, name="reduce_small")
    small_g = _unpack_flat(sred.reshape(-1), list(SMALL.values()) + list(CONV_FULL.values()) + [(1,)])
    g = dict(zip(list(SMALL) + list(CONV_FULL), small_g[:-1]))
    loss = small_g[-1][0]
    for n in CONV_FULL:
        wd = CONV_SHARD[n][2]
        g[n] = lax.dynamic_slice_in_dim(g[n], chip * wd, wd, axis=2)

    g.update({n: jnp.stack([reduced[l][i] for l in range(DEPTH)]) for i, n in enumerate(BIG)})

    delta, new_m, new_v = {}, {}, {}
    for n in BIG:
        delta[n], new_m[n], new_v[n] = _adamw(w[n], g[n], m[n], v[n], name=f"adamw_{n}")
    small_names = list(SMALL) + list(CONV_SHARD)
    small_shapes = list(SMALL.values()) + list(CONV_SHARD.values())
    packed = [_pack_flat([t[n] for n in small_names], SMALLW_ROWS * 128, F32).reshape(SMALLW_ROWS, 128) for t in (w, g, m, v)]
    for res, out in zip(_adamw(*packed, name="adamw_small"), (delta, new_m, new_v)):
        out.update(zip(small_names, _unpack_flat(res.reshape(-1), small_shapes)))

    return (loss, grad_x, *[g[n] for n in WEIGHTS], *[delta[n] for n in WEIGHTS], *[new_m[n] for n in WEIGHTS],
            *[new_v[n] for n in WEIGHTS])
```

```python
import functools

import jax
import jax.numpy as jnp
import numpy as np
from jax import lax
from jax.experimental import pallas as pl
from jax.experimental.pallas import tpu as pltpu

F32, BF16 = jnp.float32, jnp.bfloat16
HIGHEST = lax.Precision.HIGHEST
MESH = pl.DeviceIdType.MESH

D = 1024
T = 4096
DEPTH = 2
HQ, HKV, DH, WIN = 8, 2, 64, 128
GH, GD, GC, GK = 4, 128, 64, 4
DFF, FK = 2816, 3
EPS = 1e-6
N_MAIN = 2816
N_PROJ = N_MAIN + 128
COL_K, COL_V, COL_G, COL_Z, COL_BA = 4, 5, 6, 18, 22
N_CHUNK = T // GC
N_DEV, N_CHIP = 8, 4
FFN_CW = 2 * DFF // N_CHIP
LR, B1, B2, AEPS, WD, STEP = 0.001, 0.9, 0.999, 1e-08, 0.01, 10

VMEM_LIMIT = 56 * 1024 * 1024
ANY = pl.BlockSpec(memory_space=pl.ANY)
VMEM_SPEC = pl.BlockSpec(memory_space=pltpu.VMEM)


def _cp(*sem):
    return pltpu.CompilerParams(dimension_semantics=sem if sem else None, vmem_limit_bytes=VMEM_LIMIT)


class _Carry:
    def __init__(self, arrays, out_shapes, nsem, plan):
        self.arrays, self.out_shapes, self.nsem, self.plan = list(arrays), list(out_shapes), nsem, plan


def _seq_call(body, carry, *, name, steps, in_specs, out_specs, out_shape, args, scratch_shapes=(), aliases=None):
    in_specs, out_specs, out_shape, scratch = list(in_specs), list(out_specs), list(out_shape), list(scratch_shapes)
    n_in, n_out, n_scr = len(in_specs), len(out_specs), len(scratch)
    if carry is None:
        fn = body
    else:
        ci, co = len(carry.arrays), len(carry.out_shapes)

        def fn(*refs):
            ins, cins = refs[:n_in], refs[n_in:n_in + ci]
            outs, couts = refs[n_in + ci:n_in + ci + n_out], refs[n_in + ci + n_out:n_in + ci + n_out + co]
            scr, (ssem, rsem) = refs[n_in + ci + n_out + co:-2], refs[-2:]
            sends, recvs = carry.plan(cins, couts, ssem, rsem)

            @pl.when(pl.program_id(0) == 0)
            def _():
                for cp in sends:
                    cp.start()

            body(*ins, *outs, *scr)

            @pl.when(pl.program_id(0) == steps - 1)
            def _():
                for cp in recvs:
                    cp.wait_recv()
                for cp in sends:
                    cp.wait_send()

        in_specs += [ANY] * ci
        out_specs += [ANY] * co
        out_shape += carry.out_shapes
        scratch += [pltpu.SemaphoreType.DMA((carry.nsem,)), pltpu.SemaphoreType.DMA((carry.nsem,))]
        args = list(args) + carry.arrays
    outs = pl.pallas_call(
        fn, name=name, grid=(steps,), in_specs=in_specs, out_specs=out_specs, out_shape=out_shape,
        scratch_shapes=scratch, input_output_aliases=aliases or {}, compiler_params=_cp("arbitrary"),
    )(*args)
    return list(outs[:n_out]), list(outs[n_out:])


def _dot(a, b):
    return jnp.dot(a.astype(BF16), b.astype(BF16), preferred_element_type=F32)


def _dot_nt(a, b):
    return lax.dot_general(a.astype(BF16), b.astype(BF16), (((1,), (1,)), ((), ())), preferred_element_type=F32)


def _dot_tn(a, b):
    return lax.dot_general(a.astype(BF16), b.astype(BF16), (((0,), (0,)), ((), ())), preferred_element_type=F32)


def _iota(shape, dim):
    return lax.broadcasted_iota(jnp.int32, shape, dim)


def _col(x, idx):
    return jnp.sum(jnp.where(_iota(x.shape, 1) == idx, x, 0.0), axis=1, keepdims=True)


def _silu(y):
    return y * jax.nn.sigmoid(y)


_MM_DN = {"nn": (((1,), (0,)), ((), ())), "nt": (((1,), (1,)), ((), ())), "tn": (((0,), (0,)), ((), ()))}


def _mm(a, b, *, name, mode, dims, tm, tn, tk, res=None, out_dtype=F32, a_spec=None, b_spec=None, out_spec=None,
        out_shape=None, n_outer=False):
    M, N, K = dims
    assert M % tm == 0 and N % tn == 0 and K % tk == 0, (name, dims)
    nk = K // tk
    dn = _MM_DN[mode]

    def body(*refs):
        if res is None:
            a_ref, b_ref, o_ref, acc = refs
        else:
            a_ref, b_ref, r_ref, o_ref, acc = refs
        k = pl.program_id(2)

        @pl.when(k == 0)
        def _():
            acc[...] = jnp.zeros_like(acc)

        acc[...] += lax.dot_general(a_ref[...].astype(BF16), b_ref[...].astype(BF16), dn, preferred_element_type=F32)

        @pl.when(k == nk - 1)
        def _():
            r = acc[...]
            if res is not None:
                r = r + r_ref[...]
            o_ref[...] = r.astype(out_dtype)

    if a_spec is None:
        a_spec = pl.BlockSpec((tk, tm), lambda i, j, k: (k, i)) if mode == "tn" else pl.BlockSpec((tm, tk), lambda i, j, k: (i, k))
    if b_spec is None:
        b_spec = pl.BlockSpec((tn, tk), lambda i, j, k: (j, k)) if mode == "nt" else pl.BlockSpec((tk, tn), lambda i, j, k: (k, j))
    in_specs, args = [a_spec, b_spec], [a, b]
    if res is not None:
        in_specs.append(pl.BlockSpec((tm, tn), lambda i, j, k: (i, j)))
        args.append(res)
    out_spec = out_spec or pl.BlockSpec((tm, tn), lambda i, j, k: (i, j))
    grid = (M // tm, N // tn, nk)
    if n_outer:
        swap = lambda s: pl.BlockSpec(s.block_shape, lambda j, i, k, f=s.index_map: f(i, j, k))
        in_specs, out_spec, grid = [swap(s) for s in in_specs], swap(out_spec), (N // tn, M // tm, nk)
    return pl.pallas_call(
        body, name=name, grid=grid, in_specs=in_specs, out_specs=out_spec,
        out_shape=out_shape or jax.ShapeDtypeStruct((M, N), out_dtype),
        scratch_shapes=[pltpu.VMEM((tm, tn), F32)],
        compiler_params=_cp("parallel", "parallel", "arbitrary"),
    )(*args)


def _rms(x, w):
    return x * lax.rsqrt(jnp.mean(x * x, axis=-1, keepdims=True) + EPS) * w


def _rmsnorm_fwd(x, w, *, name, tm=512):
    def body(x_ref, w_ref, o_ref):
        o_ref[...] = _rms(x_ref[...], w_ref[...]).astype(BF16)

    return pl.pallas_call(
        body, name=name, grid=(x.shape[0] // tm,),
        in_specs=[pl.BlockSpec((tm, D), lambda i: (i, 0)), pl.BlockSpec((1, D), lambda i: (0, 0))],
        out_specs=pl.BlockSpec((tm, D), lambda i: (i, 0)),
        out_shape=jax.ShapeDtypeStruct(x.shape, BF16), compiler_params=_cp("parallel"),
    )(x, w)


def _rmsnorm_bwd(x, w, dh, dres, *, name, tm=512):
    def body(x_ref, w_ref, dh_ref, dr_ref, dx_ref, dw_ref):
        _, vjp = jax.vjp(_rms, x_ref[...], w_ref[...])
        dx, dw = vjp(dh_ref[...])
        dx_ref[...] = dx + dr_ref[...]

        @pl.when(pl.program_id(0) == 0)
        def _():
            dw_ref[...] = jnp.zeros_like(dw_ref)

        dw_ref[...] += dw

    row = pl.BlockSpec((tm, D), lambda i: (i, 0))
    vec = pl.BlockSpec((1, D), lambda i: (0, 0))
    return pl.pallas_call(
        body, name=name, grid=(x.shape[0] // tm,), in_specs=[row, vec, row, row], out_specs=[row, vec],
        out_shape=[jax.ShapeDtypeStruct(x.shape, F32), jax.ShapeDtypeStruct((1, D), F32)],
        compiler_params=_cp("arbitrary"),
    )(x, w, dh, dres)


def _loss_head(x, w, tgt, *, name, tm=512):
    def f(xv, wv, tv):
        err = _rms(xv, wv) - tv
        per_row = jnp.sum(err * err, axis=1, keepdims=True) * (0.5 / D)
        return jnp.sum(per_row, axis=0, keepdims=True)

    def body(x_ref, w_ref, t_ref, dx_ref, dw_ref, loss_ref):
        tv = t_ref[...]
        loss, vjp = jax.vjp(lambda xv, wv: f(xv, wv, tv), x_ref[...], w_ref[...])
        dx, dw = vjp(jnp.ones((1, 1), F32))
        dx_ref[...] = dx

        @pl.when(pl.program_id(0) == 0)
        def _():
            dw_ref[...] = jnp.zeros_like(dw_ref)
            loss_ref[...] = jnp.zeros_like(loss_ref)

        dw_ref[...] += dw
        loss_ref[...] += jnp.broadcast_to(loss, loss_ref.shape)

    row = pl.BlockSpec((tm, D), lambda i: (i, 0))
    vec = pl.BlockSpec((1, D), lambda i: (0, 0))
    return pl.pallas_call(
        body, name=name, grid=(x.shape[0] // tm,), in_specs=[row, vec, row],
        out_specs=[row, vec, pl.BlockSpec((1, 128), lambda i: (0, 0))],
        out_shape=[jax.ShapeDtypeStruct(x.shape, F32), jax.ShapeDtypeStruct((1, D), F32),
                   jax.ShapeDtypeStruct((1, 128), F32)],
        compiler_params=_cp("arbitrary"),
    )(x, w, tgt)


def _swa_bias():
    G = HQ // HKV
    r, c = np.arange(G * WIN)[:, None], np.arange(2 * WIN)[None, :]
    rel = (r % WIN) + WIN - c
    band = (rel >= 0) & (rel < WIN)
    out = np.empty((2, HKV, G * WIN, 2 * WIN), np.float32)
    for h2 in range(HKV):
        slope = 2.0 ** (-8.0 * (h2 * G + r // WIN + 1) / HQ)
        out[0, h2] = np.where(band & (c >= WIN), -slope * rel, -1e30)
        out[1, h2] = np.where(band, -slope * rel, -1e30)
    return jnp.asarray(out)


def _swa_group(qg, k2, v2, sk, bias, h2):
    s = _dot_nt(qg, k2) * (DH ** -0.5) + bias
    g1 = _iota((s.shape[0], 1), 0) >> 7
    sink = jnp.zeros((s.shape[0], 1), F32)
    for gi in range(HQ // HKV):
        sink = jnp.where(g1 == gi, _col(sk, h2 * (HQ // HKV) + gi), sink)
    m = lax.stop_gradient(jnp.maximum(jnp.max(s, axis=1, keepdims=True), sink))
    p = jnp.exp(s - m)
    den = jnp.sum(p, axis=1, keepdims=True) + jnp.exp(sink - m)
    return _dot(p * (1.0 / den), v2)


def _swa_split_q(q, h2):
    G = HQ // HKV
    return jnp.concatenate([q[:, (h2 * G + g) * DH:(h2 * G + g + 1) * DH] for g in range(G)], axis=0)


def _swa_merge_q(parts):
    G = HQ // HKV
    return jnp.concatenate([parts[h2][g * WIN:(g + 1) * WIN] for h2 in range(HKV) for g in range(G)], axis=1)


def _swa_specs():
    prev = lambda i: jnp.maximum(jnp.minimum(i, T // WIN - 1) - 1, 0)
    cur = lambda i: jnp.minimum(i, T // WIN - 1)
    return [
        pl.BlockSpec((WIN, HQ * DH), lambda i: (cur(i), 0)),
        pl.BlockSpec((WIN, 128), lambda i: (prev(i), COL_K)),
        pl.BlockSpec((WIN, 128), lambda i: (cur(i), COL_K)),
        pl.BlockSpec((WIN, 128), lambda i: (prev(i), COL_V)),
        pl.BlockSpec((WIN, 128), lambda i: (cur(i), COL_V)),
        pl.BlockSpec((1, 128), lambda i: (0, 0)),
        pl.BlockSpec((None, HKV, (HQ // HKV) * WIN, 2 * WIN), lambda i: (jnp.minimum(i, 1), 0, 0, 0)),
    ]


def _swa_fwd(proj, sinks, *, name):
    def body(q_ref, kp_ref, kc_ref, vp_ref, vc_ref, sk_ref, bias_ref, o_ref):
        q, sk = q_ref[...].astype(F32), sk_ref[...]
        outs = []
        for h2 in range(HKV):
            sl = slice(h2 * DH, (h2 + 1) * DH)
            k2 = jnp.concatenate([kp_ref[:, sl], kc_ref[:, sl]], axis=0).astype(F32)
            v2 = jnp.concatenate([vp_ref[:, sl], vc_ref[:, sl]], axis=0).astype(F32)
            outs.append(_swa_group(_swa_split_q(q, h2), k2, v2, sk, bias_ref[h2], h2))
        o_ref[...] = _swa_merge_q(outs).astype(BF16)

    return pl.pallas_call(
        body, name=name, grid=(T // WIN,), in_specs=_swa_specs(),
        out_specs=pl.BlockSpec((WIN, HQ * DH), lambda i: (i, 0)),
        out_shape=jax.ShapeDtypeStruct((T, HQ * DH + GH * GD), BF16), compiler_params=_cp("parallel"),
    )(proj, proj, proj, proj, proj, sinks, _swa_bias())


def _swa_bwd(proj, sinks, do, dproj, *, name):
    NB = T // WIN
    QW = HQ * DH

    def body(q_ref, kp_ref, kc_ref, vp_ref, vc_ref, sk_ref, bias_ref, do_ref, _, out_ref, dsk_ref, cq, ck, cv):
        i = pl.program_id(0)

        @pl.when(i == 0)
        def _():
            cq[...] = jnp.zeros_like(cq)
            ck[...] = jnp.zeros_like(ck)
            cv[...] = jnp.zeros_like(cv)
            dsk_ref[...] = jnp.zeros_like(dsk_ref)

        @pl.when(i < NB)
        def _():
            q, sk, dov = q_ref[...].astype(F32), sk_ref[...], do_ref[...].astype(F32)
            dqs, dkp, dkc, dvp, dvc = [], [], [], [], []
            dsk = jnp.zeros_like(sk)
            for h2 in range(HKV):
                sl = slice(h2 * DH, (h2 + 1) * DH)
                k2 = jnp.concatenate([kp_ref[:, sl], kc_ref[:, sl]], axis=0).astype(F32)
                v2 = jnp.concatenate([vp_ref[:, sl], vc_ref[:, sl]], axis=0).astype(F32)
                _, vjp = jax.vjp(functools.partial(_swa_group, bias=bias_ref[h2], h2=h2), _swa_split_q(q, h2), k2, v2, sk)
                dqg, dk2, dv2, dsk_h = vjp(_swa_split_q(dov, h2))
                dqs.append(dqg)
                dkp.append(dk2[:WIN]); dkc.append(dk2[WIN:])
                dvp.append(dv2[:WIN]); dvc.append(dv2[WIN:])
                dsk = dsk + dsk_h
            out_ref[:, :QW] = cq[...].astype(BF16)
            out_ref[:, QW:QW + 128] = (ck[...] + jnp.concatenate(dkp, axis=1)).astype(BF16)
            out_ref[:, QW + 128:] = (cv[...] + jnp.concatenate(dvp, axis=1)).astype(BF16)
            cq[...] = _swa_merge_q(dqs)
            ck[...] = jnp.concatenate(dkc, axis=1)
            cv[...] = jnp.concatenate(dvc, axis=1)
            dsk_ref[...] += dsk

        @pl.when(i == NB)
        def _():
            out_ref[:, :QW] = cq[...].astype(BF16)
            out_ref[:, QW:QW + 128] = ck[...].astype(BF16)
            out_ref[:, QW + 128:] = cv[...].astype(BF16)

    qblk = pl.BlockSpec((WIN, QW), lambda i: (jnp.minimum(i, NB - 1), 0))
    return pl.pallas_call(
        body, name=name, grid=(NB + 1,), in_specs=_swa_specs() + [qblk, ANY],
        out_specs=[pl.BlockSpec((WIN, QW + 256), lambda i: (jnp.maximum(i - 1, 0), 0)), pl.BlockSpec((1, 128), lambda i: (0, 0))],
        out_shape=[jax.ShapeDtypeStruct(dproj.shape, dproj.dtype), jax.ShapeDtypeStruct((1, 128), F32)],
        scratch_shapes=[pltpu.VMEM((WIN, QW), F32), pltpu.VMEM((WIN, 128), F32), pltpu.VMEM((WIN, 128), F32)],
        input_output_aliases={8: 0}, compiler_params=_cp("arbitrary"),
    )(proj, proj, proj, proj, proj, sinks, _swa_bias(), do, dproj)


RC = 256
HALO = 8


def _load_ext(ref, c):
    nch = T // RC
    r0 = pl.multiple_of(c * RC, RC)
    p0 = pl.multiple_of(jnp.maximum(r0 - HALO, 0), HALO)
    n0 = pl.multiple_of(jnp.minimum(r0 + RC, T - HALO), HALO)
    prev = jnp.where(c > 0, ref[pl.ds(p0, HALO), :].astype(F32), 0.0)
    nxt = jnp.where(c < nch - 1, ref[pl.ds(n0, HALO), :].astype(F32), 0.0)
    return jnp.concatenate([prev, ref[pl.ds(r0, RC), :].astype(F32), nxt], axis=0)


def _conv_ext(xe, w, K):
    y = w[K - 1:K, :] * xe
    for s in range(1, K):
        y = y + w[K - 1 - s:K - s, :] * pltpu.roll(xe, s, 0)
    return y


def _conv_bwd_ext(xe, dye, w, K, dw_ref):
    n = xe.shape[0]
    own = slice(HALO, HALO + RC)
    dx = w[K - 1:K, :] * dye
    dw_ref[K - 1:K, :] += jnp.sum(dye[own] * xe[own], axis=0, keepdims=True)
    for s in range(1, K):
        dx = dx + w[K - 1 - s:K - s, :] * pltpu.roll(dye, n - s, 0)
        dw_ref[K - 1 - s:K - s, :] += jnp.sum(dye[own] * pltpu.roll(xe, s, 0)[own], axis=0, keepdims=True)
    return dx[own]


def _gdn_post_conv(y, is_qk):
    a = _silu(y)
    nrm = a * lax.rsqrt(jnp.sum(a * a, axis=1, keepdims=True) + EPS)
    return jnp.where(is_qk, nrm, a)


def _gdn_prep_fwd(proj, conv_w, *, name):
    nblk = 3 * GH

    def body(x_ref, w_ref, o_ref):
        is_qk = pl.program_id(0) < 2 * GH
        w = w_ref[...]

        def chunk(c, carry):
            y = _conv_ext(_load_ext(x_ref, c), w, GK)[HALO:HALO + RC]
            o_ref[pl.ds(pl.multiple_of(c * RC, RC), RC), :] = _gdn_post_conv(y, is_qk)
            return carry

        lax.fori_loop(0, T // RC, chunk, 0)

    return pl.pallas_call(
        body, name=name, grid=(nblk,),
        in_specs=[pl.BlockSpec((T, 128), lambda j: (0, COL_G + j)), pl.BlockSpec((GK, 128), lambda j: (0, j))],
        out_specs=pl.BlockSpec((T, 128), lambda j: (0, j)),
        out_shape=jax.ShapeDtypeStruct((T, nblk * 128), F32), compiler_params=_cp("parallel"),
    )(proj, conv_w)


def _gdn_prep_bwd(proj, conv_w, dout, dproj, *, name):
    nblk = 3 * GH

    def body(x_ref, w_ref, d_ref, _, dx_ref, dw_ref):
        is_qk = pl.program_id(0) < 2 * GH
        w = w_ref[...]
        dw_ref[...] = jnp.zeros_like(dw_ref)

        def chunk(c, carry):
            xe = _load_ext(x_ref, c)
            _, vjp = jax.vjp(lambda y: _gdn_post_conv(y, is_qk), _conv_ext(xe, w, GK))
            (dye,) = vjp(_load_ext(d_ref, c))
            dx_ref[pl.ds(pl.multiple_of(c * RC, RC), RC), :] = _conv_bwd_ext(xe, dye, w, GK, dw_ref).astype(BF16)
            return carry

        lax.fori_loop(0, T // RC, chunk, 0)

    return pl.pallas_call(
        body, name=name, grid=(nblk,),
        in_specs=[pl.BlockSpec((T, 128), lambda j: (0, COL_G + j)), pl.BlockSpec((GK, 128), lambda j: (0, j)),
                  pl.BlockSpec((T, 128), lambda j: (0, j)), ANY],
        out_specs=[pl.BlockSpec((T, 128), lambda j: (0, COL_G + j)), pl.BlockSpec((GK, 128), lambda j: (0, j))],
        out_shape=[jax.ShapeDtypeStruct(dproj.shape, dproj.dtype), jax.ShapeDtypeStruct((GK, nblk * 128), F32)],
        input_output_aliases={3: 0}, compiler_params=_cp("parallel"),
    )(proj, conv_w, dout, dproj)


def _ffn_post_conv(y, b, up):
    return _silu(y + b) * up


def _ffn_act_fwd(gu, conv_w, conv_b, *, name, carry=None):
    nblk = DFF // 128

    def body(g_ref, u_ref, w_ref, b_ref, o_ref):
        w, b = w_ref[...], b_ref[...]

        def chunk(c, carry):
            rows = pl.ds(pl.multiple_of(c * RC, RC), RC)
            y = _conv_ext(_load_ext(g_ref, c), w, FK)[HALO:HALO + RC]
            o_ref[rows, :] = _ffn_post_conv(y, b, u_ref[rows, :].astype(F32)).astype(BF16)
            return carry

        lax.fori_loop(0, T // RC, chunk, 0)

    (act,), carried = _seq_call(
        body, carry, name=name, steps=nblk,
        in_specs=[pl.BlockSpec((T, 128), lambda j: (0, j)), pl.BlockSpec((T, 128), lambda j: (0, nblk + j)),
                  pl.BlockSpec((FK, 128), lambda j: (0, j)), pl.BlockSpec((1, 128), lambda j: (0, j))],
        out_specs=[pl.BlockSpec((T, 128), lambda j: (0, j))], out_shape=[jax.ShapeDtypeStruct((T, DFF), BF16)],
        args=(gu, gu, conv_w, conv_b))
    return act, carried


def _ffn_act_bwd(gu, conv_w, conv_b, dact, *, name):
    nblk = DFF // 128

    def body(g_ref, u_ref, w_ref, b_ref, d_ref, dgu_ref, dw_ref, db_ref):
        dg_ref, du_ref = dgu_ref.at[0], dgu_ref.at[1]
        w, b = w_ref[...], b_ref[...]
        dw_ref[...] = jnp.zeros_like(dw_ref)
        db_ref[...] = jnp.zeros_like(db_ref)

        def chunk(c, carry):
            rows = pl.ds(pl.multiple_of(c * RC, RC), RC)
            xe = _load_ext(g_ref, c)
            ue = _load_ext(u_ref, c)
            _, vjp = jax.vjp(_ffn_post_conv, _conv_ext(xe, w, FK), b, ue)
            dye, db, due = vjp(_load_ext(d_ref, c))
            du_ref[rows, :] = due[HALO:HALO + RC].astype(BF16)
            db_ref[...] += jnp.sum(dye[HALO:HALO + RC], axis=0, keepdims=True)
            dg_ref[rows, :] = _conv_bwd_ext(xe, dye, w, FK, dw_ref).astype(BF16)
            return carry

        lax.fori_loop(0, T // RC, chunk, 0)

    col = pl.BlockSpec((T, 128), lambda j: (0, j))
    return pl.pallas_call(
        body, name=name, grid=(nblk,),
        in_specs=[col, pl.BlockSpec((T, 128), lambda j: (0, nblk + j)), pl.BlockSpec((FK, 128), lambda j: (0, j)),
                  pl.BlockSpec((1, 128), lambda j: (0, j)), col],
        out_specs=[pl.BlockSpec((2, T, 128), lambda j: (0, 0, j)), pl.BlockSpec((FK, 128), lambda j: (0, j)),
                   pl.BlockSpec((1, 128), lambda j: (0, j))],
        out_shape=[jax.ShapeDtypeStruct((2, T, DFF), BF16), jax.ShapeDtypeStruct((FK, DFF), F32),
                   jax.ShapeDtypeStruct((1, DFF), F32)],
        compiler_params=_cp("parallel"),
    )(gu, gu, conv_w, conv_b, dact)


def _gdn_gates(ba, alog, dtb):
    beta = jax.nn.sigmoid(ba)
    g = -jnp.exp(alog) * jax.nn.softplus(ba + dtb)
    tril = (_iota((GC, GC), 0) >= _iota((GC, GC), 1)).astype(F32)
    return beta, jnp.dot(tril, g, precision=HIGHEST, preferred_element_type=F32)


def _hmap(f, *lists):
    return [f(*xs) for xs in zip(*lists)]


def _gdn_cols(beta_all, gc_all):
    return [_col(beta_all, h) for h in range(GH)], [_col(gc_all, GH + h) for h in range(GH)]


def _gdn_decay(Gcs):
    r, c = _iota((GC, GC), 0), _iota((GC, GC), 1)
    eye, ones = (r == c).astype(F32), jnp.ones((GC, GC), F32)
    grows = _hmap(lambda G: jnp.dot(ones, eye * G, precision=HIGHEST, preferred_element_type=F32), Gcs)
    return _hmap(lambda G, grow: jnp.exp(jnp.where(r >= c, G - grow, -1e30)), Gcs, grows)


def _gdn_A(ks, betas, Gcs):
    strict = _iota((GC, GC), 0) > _iota((GC, GC), 1)
    kk = _hmap(lambda k, b: _dot_nt(k * b, k), ks, betas)
    return _hmap(lambda a, d: jnp.where(strict, a * d, 0.0), kk, _gdn_decay(Gcs))


def _tri_inv(As):
    eye = (_iota((GC, GC), 0) == _iota((GC, GC), 1)).astype(F32)
    Tms, Ps = [eye - A for A in As], As
    for _ in range(GC.bit_length() - 2):
        Ps = _hmap(lambda P: _dot(P, P), Ps)
        Tms = _hmap(lambda Tm, P: Tm + _dot(Tm, P), Tms, Ps)
    return Tms


def _gdn_chunk(qs, ks, vs, betas, Gcs, Ss, Tms):
    decays = _gdn_decay(Gcs)
    eGs = _hmap(jnp.exp, Gcs)
    us = _hmap(lambda Tm, v, b: _dot(Tm, v * b), Tms, vs, betas)
    ws = _hmap(lambda Tm, k, b, eG: _dot(Tm, k * b * eG), Tms, ks, betas, eGs)
    qss = [q * (GD ** -0.5) for q in qs]
    qks = _hmap(lambda q, k, d: _dot_nt(q, k) * d, qss, ks, decays)
    glasts = [jnp.sum(jnp.where(_iota(G.shape, 0) == GC - 1, G, 0.0), axis=0, keepdims=True) for G in Gcs]
    kds = _hmap(lambda k, gl, G: k * jnp.exp(gl - G), ks, glasts, Gcs)
    v_news = _hmap(lambda u, w, S: u - _dot(w, S), us, ws, Ss)
    qS = _hmap(lambda q, eG, S: _dot(q * eG, S), qss, eGs, Ss)
    os = _hmap(lambda a, qk, vn: a + _dot(qk, vn), qS, qks, v_news)
    S_news = _hmap(lambda S, gl, kd, vn: S * jnp.exp(gl) + _dot_tn(kd, vn), Ss, glasts, kds, v_news)
    return os, S_news


def _gdn_chunk_fwd(qkv_c, ba, alog, dtb, *, name, carry=None):
    W3 = 3 * GH * GD

    def body(x_ref, ba_ref, al_ref, dt_ref, o_ref, s_ref, t_ref, S):
        @pl.when(pl.program_id(0) == 0)
        def _():
            S[...] = jnp.zeros_like(S)

        beta_all, gc_all = _gdn_gates(ba_ref[...], al_ref[...], dt_ref[...])
        qs, ks, vs = ([x_ref[:, (p * GH + h) * GD:(p * GH + h + 1) * GD] for h in range(GH)] for p in range(3))
        betas, Gcs = _gdn_cols(beta_all, gc_all)
        Tms = _tri_inv(_gdn_A(ks, betas, Gcs))
        Ss = [S[h] for h in range(GH)]
        os, S_news = _gdn_chunk(qs, ks, vs, betas, Gcs, Ss, Tms)
        for h in range(GH):
            s_ref[0, h] = Ss[h]
            t_ref[0, h] = Tms[h]
            o_ref[:, h * GD:(h + 1) * GD] = os[h]
            S[h] = S_news[h]

    vec = pl.BlockSpec((1, 128), lambda n: (0, 0))
    return _seq_call(
        body, carry, name=name, steps=N_CHUNK,
        in_specs=[pl.BlockSpec((GC, W3), lambda n: (n, 0)), pl.BlockSpec((GC, 128), lambda n: (n, 0)), vec, vec],
        out_specs=[pl.BlockSpec((GC, GH * GD), lambda n: (n, 0)),
                   pl.BlockSpec((1, GH, GD, GD), lambda n: (n, 0, 0, 0)),
                   pl.BlockSpec((1, GH, GC, GC), lambda n: (n, 0, 0, 0))],
        out_shape=[jax.ShapeDtypeStruct((T, GH * GD), F32), jax.ShapeDtypeStruct((N_CHUNK, GH, GD, GD), F32),
                   jax.ShapeDtypeStruct((N_CHUNK, GH, GC, GC), F32)],
        scratch_shapes=[pltpu.VMEM((GH, GD, GD), F32)], args=(qkv_c, ba, alog, dtb))


def _gdn_chunk_bwd(qkv_c, ba, alog, dtb, s_all, t_all, do, dproj, *, name, carry=None):
    W3 = 3 * GH * GD
    rev = lambda n: N_CHUNK - 1 - n

    def body(x_ref, ba_ref, al_ref, dt_ref, s_ref, t_ref, do_ref, _, dx_ref, dba_ref, dal_ref, ddt_ref, dS):
        @pl.when(pl.program_id(0) == 0)
        def _():
            dS[...] = jnp.zeros_like(dS)
            dal_ref[...] = jnp.zeros_like(dal_ref)
            ddt_ref[...] = jnp.zeros_like(ddt_ref)

        (beta_all, gc_all), vjp_gates = jax.vjp(_gdn_gates, ba_ref[...], al_ref[...], dt_ref[...])
        qs, ks, vs = ([x_ref[:, (p * GH + h) * GD:(p * GH + h + 1) * GD] for h in range(GH)] for p in range(3))
        Tms = [t_ref[0, h] for h in range(GH)]
        Ss = [s_ref[0, h] for h in range(GH)]
        dos = [do_ref[:, h * GD:(h + 1) * GD] for h in range(GH)]
        dSs = [dS[h] for h in range(GH)]

        def f(qs, ks, vs, b_all, g_all, Ss, Tms):
            return _gdn_chunk(qs, ks, vs, *_gdn_cols(b_all, g_all), Ss, Tms)

        def fa(ks, b_all, g_all):
            return _gdn_A(ks, *_gdn_cols(b_all, g_all))

        _, vjp = jax.vjp(f, qs, ks, vs, beta_all, gc_all, Ss, Tms)
        dqs, dks, dvs, db1, dg1, dS_prev, dTs = vjp((dos, dSs))
        dXs = _hmap(_dot_nt, dTs, Tms)
        dAs = _hmap(lambda Tm, dX: -_dot_tn(Tm, dX), Tms, dXs)
        _, vjp_a = jax.vjp(fa, ks, beta_all, gc_all)
        dks2, db2, dg2 = vjp_a(dAs)
        for h in range(GH):
            dS[h] = dS_prev[h]
            dx_ref[:, h * GD:(h + 1) * GD] = dqs[h]
            dx_ref[:, (GH + h) * GD:(GH + h + 1) * GD] = dks[h] + dks2[h]
            dx_ref[:, (2 * GH + h) * GD:(2 * GH + h + 1) * GD] = dvs[h]
        dba, dal, ddt = vjp_gates((db1 + db2, dg1 + dg2))
        dba_ref[...] = dba.astype(BF16)
        dal_ref[...] += dal
        ddt_ref[...] += ddt

    vec = pl.BlockSpec((1, 128), lambda n: (0, 0))
    return _seq_call(
        body, carry, name=name, steps=N_CHUNK,
        in_specs=[pl.BlockSpec((GC, W3), lambda n: (rev(n), 0)), pl.BlockSpec((GC, 128), lambda n: (rev(n), 0)),
                  vec, vec, pl.BlockSpec((1, GH, GD, GD), lambda n: (rev(n), 0, 0, 0)),
                  pl.BlockSpec((1, GH, GC, GC), lambda n: (rev(n), 0, 0, 0)),
                  pl.BlockSpec((GC, GH * GD), lambda n: (rev(n), 0)),
                  ANY],
        out_specs=[pl.BlockSpec((GC, W3), lambda n: (rev(n), 0)), pl.BlockSpec((GC, 128), lambda n: (rev(n), COL_BA)), vec, vec],
        out_shape=[jax.ShapeDtypeStruct((T, W3), F32), jax.ShapeDtypeStruct(dproj.shape, dproj.dtype),
                   jax.ShapeDtypeStruct((1, 128), F32), jax.ShapeDtypeStruct((1, 128), F32)],
        scratch_shapes=[pltpu.VMEM((GH, GD, GD), F32)], aliases={7: 1},
        args=(qkv_c, ba, alog, dtb, s_all, t_all, do, dproj))


def _gdn_post(o, z, nw):
    return o * lax.rsqrt(jnp.mean(o * o, axis=-1, keepdims=True) + EPS) * nw * _silu(z)


def _gdn_post_fwd(o_raw, proj, nw, mixed, *, name, tm=512):
    def body(o_ref, z_ref, w_ref, _, out_ref):
        out_ref[...] = _gdn_post(o_ref[...], z_ref[...].astype(F32), w_ref[...]).astype(BF16)

    return pl.pallas_call(
        body, name=name, grid=(T // tm, GH),
        in_specs=[pl.BlockSpec((tm, GD), lambda i, h: (i, h)), pl.BlockSpec((tm, GD), lambda i, h: (i, COL_Z + h)),
                  pl.BlockSpec((1, GD), lambda i, h: (0, 0)), ANY],
        out_specs=pl.BlockSpec((tm, GD), lambda i, h: (i, HQ * DH // GD + h)),
        out_shape=jax.ShapeDtypeStruct(mixed.shape, mixed.dtype), input_output_aliases={3: 0},
        compiler_params=_cp("parallel", "parallel"),
    )(o_raw, proj, nw, mixed)


def _gdn_post_bwd(o_raw, proj, nw, dmixed, *, name, tm=512):
    def body(o_ref, z_ref, w_ref, d_ref, do_ref, dz_ref, dw_ref):
        _, vjp = jax.vjp(_gdn_post, o_ref[...], z_ref[...].astype(F32), w_ref[...])
        do, dz, dw = vjp(d_ref[...])
        do_ref[...] = do
        dz_ref[...] = dz.astype(BF16)

        @pl.when((pl.program_id(0) == 0) & (pl.program_id(1) == 0))
        def _():
            dw_ref[...] = jnp.zeros_like(dw_ref)

        dw_ref[...] += dw

    blk = pl.BlockSpec((tm, GD), lambda i, h: (i, h))
    vec = pl.BlockSpec((1, GD), lambda i, h: (0, 0))
    return pl.pallas_call(
        body, name=name, grid=(T // tm, GH),
        in_specs=[blk, pl.BlockSpec((tm, GD), lambda i, h: (i, COL_Z + h)), vec,
                  pl.BlockSpec((tm, GD), lambda i, h: (i, GH + h))],
        out_specs=[blk, pl.BlockSpec((tm, GD), lambda i, h: (i, COL_Z + h)), vec],
        out_shape=[jax.ShapeDtypeStruct((T, GH * GD), F32), jax.ShapeDtypeStruct((T, N_PROJ), BF16),
                   jax.ShapeDtypeStruct((1, GD), F32)],
        compiler_params=_cp("arbitrary", "arbitrary"),
    )(o_raw, proj, nw, dmixed)


def _adamw(w, g, m, v, *, name):
    shape = w.shape
    cols = shape[-1]
    w2, g2, m2, v2 = (a.reshape(-1, cols) for a in (w, g, m, v))
    rows = w2.shape[0]
    tr = next((t for t in (512, 256, 128, 64, 32, 16, 8) if rows % t == 0), rows)

    def body(w_ref, g_ref, m_ref, v_ref, d_ref, nm_ref, nv_ref):
        gv = g_ref[...]
        nm = B1 * m_ref[...] + (1.0 - B1) * gv
        nv = B2 * v_ref[...] + (1.0 - B2) * jnp.square(gv)
        m_hat = nm / (1.0 - B1 ** STEP)
        v_hat = nv / (1.0 - B2 ** STEP)
        d_ref[...] = -LR * (m_hat / (jnp.sqrt(v_hat) + AEPS) + WD * w_ref[...])
        nm_ref[...] = nm
        nv_ref[...] = nv

    blk = pl.BlockSpec((tr, cols), lambda i: (i, 0))
    out = pl.pallas_call(
        body, name=name, grid=(rows // tr,), in_specs=[blk] * 4, out_specs=[blk] * 3,
        out_shape=[jax.ShapeDtypeStruct((rows, cols), F32)] * 3, compiler_params=_cp("parallel"),
    )(w2, g2, m2, v2)
    return tuple(o.reshape(shape) for o in out)


def _layer_weights(w_in):
    return jnp.pad(w_in, ((0, 0), (0, N_PROJ - w_in.shape[1])))


def _layer_params(attn_norm, sinks, gcw, a_log, dt_bias, gnw, ffn_norm, fcw, fcb):
    lanes4 = lambda v: jnp.pad(v, (GH, 128 - 2 * GH))[None]
    return dict(attn_norm=attn_norm[None], sinks=jnp.pad(sinks, (0, 128 - HQ))[None], gcw=gcw, alog=lanes4(a_log),
                dtb=lanes4(dt_bias), gnw=gnw[None], ffn_norm=ffn_norm[None], fcw=fcw, fcb=fcb[None])


def _mixer_fwd(x, W, P, l, carry=None):
    n = lambda s: f"l{l}_{s}"
    h = _rmsnorm_fwd(x, P["attn_norm"], name=n("norm1"))
    proj = _mm(h, W["all"], name=n("proj"), mode="nn", dims=(T, N_MAIN, D), tm=512, tn=N_MAIN, tk=D, out_dtype=BF16)
    ba = _mm(h, W["all"], name=n("proj_ba"), mode="nn", dims=(T, 128, D), tm=1024, tn=128, tk=D,
             b_spec=pl.BlockSpec((D, 128), lambda i, j, k: (k, COL_BA)))
    mixed = _swa_fwd(proj, P["sinks"], name=n("swa"))
    qkv_c = _gdn_prep_fwd(proj, P["gcw"], name=n("gdn_prep"))
    (o_raw, s_all, t_all), carried = _gdn_chunk_fwd(qkv_c, ba, P["alog"], P["dtb"], name=n("gdn_chunk"), carry=carry)
    mixed = _gdn_post_fwd(o_raw, proj, P["gnw"], mixed, name=n("gdn_post"))
    x1 = _mm(mixed, W["out"], res=x, name=n("out_proj"), mode="nn", dims=(T, D, D), tm=512, tn=D, tk=D)
    saved = dict(x=x, h=h, proj=proj, ba=ba, qkv_c=qkv_c, o_raw=o_raw, s_all=s_all, t_all=t_all, mixed=mixed, x1=x1)
    return x1, saved, carried


def _ffn_fwd(x1, W, P, l, carry=None):
    n = lambda s: f"l{l}_{s}"
    h2 = _rmsnorm_fwd(x1, P["ffn_norm"], name=n("norm2"))
    gu = _mm(h2, W["ffn"], name=n("ffn_in"), mode="nn", dims=(T, 2 * DFF, D), tm=512, tn=FFN_CW, tk=D, out_dtype=BF16,
             b_spec=pl.BlockSpec((None, D, FFN_CW), lambda i, j, k: (j, k, 0)), n_outer=True)
    act, carried = _ffn_act_fwd(gu, P["fcw"], P["fcb"], name=n("ffn_act"), carry=carry)
    x2 = _mm(act, W["down"], res=x1, name=n("ffn_down"), mode="nn", dims=(T, D, DFF), tm=512, tn=D, tk=DFF)
    return x2, dict(h2=h2, gu=gu, act=act), carried


def _layer_bwd(dx2, sv, W, P, l, carry=None):
    n = lambda s: f"l{l}_{s}"
    CW = FFN_CW
    dact = _mm(dx2, W["down"], name=n("d_act"), mode="nt", dims=(T, DFF, D), tm=512, tn=DFF // 2, tk=D, out_dtype=BF16,
               n_outer=True)
    g_down = _mm(sv["act"], dx2, name=n("g_down"), mode="tn", dims=(DFF, D, T), tm=DFF // 2, tn=D, tk=1024, out_dtype=BF16)
    dgu, g_fcw, g_fcb = _ffn_act_bwd(sv["gu"], P["fcw"], P["fcb"], dact, name=n("d_ffn_act"))
    dh2 = _mm(dgu, W["ffn"], name=n("d_h2"), mode="nt", dims=(T, D, 2 * DFF), tm=1024, tn=D, tk=CW,
              a_spec=pl.BlockSpec((None, 1024, CW), lambda i, j, k: (k // 2, i, k % 2)),
              b_spec=pl.BlockSpec((None, D, CW), lambda i, j, k: (k, j, 0)))
    g_ffn = _mm(sv["h2"], dgu, name=n("g_ffn"), mode="tn", dims=(D, 2 * DFF, T), tm=D, tn=CW, tk=1024,
                b_spec=pl.BlockSpec((None, 1024, CW), lambda i, j, k: (j // 2, k, j % 2)),
                out_spec=pl.BlockSpec((None, D, CW), lambda i, j, k: (j, i, 0)),
                out_shape=jax.ShapeDtypeStruct((N_CHIP, D, CW), BF16), out_dtype=BF16)
    dx1, g_ffn_norm = _rmsnorm_bwd(sv["x1"], P["ffn_norm"], dh2, dx2, name=n("d_norm2"))
    dmixed = _mm(dx1, W["out"], name=n("d_mixed"), mode="nt", dims=(T, D, D), tm=512, tn=D, tk=D)
    g_out = _mm(sv["mixed"], dx1, name=n("g_out"), mode="tn", dims=(D, D, T), tm=D, tn=D, tk=1024, out_dtype=BF16)
    do_raw, dproj, g_gnw = _gdn_post_bwd(sv["o_raw"], sv["proj"], P["gnw"], dmixed, name=n("d_gdn_post"))
    (dqkv_c, dproj, g_alog, g_dtb), carried = _gdn_chunk_bwd(
        sv["qkv_c"], sv["ba"], P["alog"], P["dtb"], sv["s_all"], sv["t_all"], do_raw, dproj, name=n("d_gdn_chunk"), carry=carry)
    dproj, g_gcw = _gdn_prep_bwd(sv["proj"], P["gcw"], dqkv_c, dproj, name=n("d_gdn_prep"))
    dproj, g_sinks = _swa_bwd(sv["proj"], P["sinks"], dmixed, dproj, name=n("d_swa"))
    dh = _mm(dproj, W["all"], name=n("d_h"), mode="nt", dims=(T, D, N_PROJ), tm=512, tn=D, tk=N_PROJ)
    g_all = _mm(sv["h"], dproj, name=n("g_in"), mode="tn", dims=(D, N_PROJ, T), tm=D, tn=N_PROJ, tk=1024, out_dtype=BF16)
    dx, g_attn_norm = _rmsnorm_bwd(sv["x"], P["attn_norm"], dh, dx1, name=n("d_norm1"))
    grads = dict(
        attn_norm=g_attn_norm[0], w_in=g_all, attn_sinks=g_sinks[0, :HQ], gdn_conv_w=g_gcw,
        gdn_a_log=g_alog[0, GH:2 * GH], gdn_dt_bias=g_dtb[0, GH:2 * GH], gdn_norm=g_gnw[0], w_out=g_out,
        ffn_norm=g_ffn_norm[0], w_ffn_in=g_ffn, ffn_conv_w=g_fcw, ffn_conv_b=g_fcb[0], w_down=g_down)
    return dx, grads, carried


def _pos():
    return lax.axis_index("x"), lax.axis_index("y"), lax.axis_index("c")


def _other_chips(x, y):
    return [(1 - x, y), (x, 1 - y), (1 - x, 1 - y)]


def _remote(src, dst, send_sems, recv_sems, k, to):
    return pltpu.make_async_remote_copy(src_ref=src, dst_ref=dst, send_sem=send_sems.at[k], recv_sem=recv_sems.at[k],
                                        device_id=to, device_id_type=MESH)


def _run_carry(carry, *, name):
    ci, co = len(carry.arrays), len(carry.out_shapes)

    def body(*refs):
        sends, recvs = carry.plan(refs[:ci], refs[ci:ci + co], refs[-2], refs[-1])
        for cp in sends:
            cp.start()
        for cp in recvs:
            cp.wait_recv()
        for cp in sends:
            cp.wait_send()

    return list(pl.pallas_call(
        body, name=name, in_specs=[ANY] * ci, out_specs=[ANY] * co, out_shape=carry.out_shapes,
        scratch_shapes=[pltpu.SemaphoreType.DMA((carry.nsem,)), pltpu.SemaphoreType.DMA((carry.nsem,))],
    )(*carry.arrays))


def _chip_index():
    return 2 * lax.axis_index("x") + lax.axis_index("y")


def _gather_carry(shards):
    def plan(srcs, outs, send_sems, recv_sems):
        x, y, c = _pos()
        chip = 2 * x + y
        others = [(k, px, py, 2 * px + py) for k, (px, py) in enumerate(_other_chips(x, y))]
        sends = [_remote(srcs[t], outs[t].at[chip], send_sems, recv_sems, 3 * t + k, (px, py, c))
                 for t in range(len(srcs)) for k, px, py, _ in others]
        recvs = [_remote(srcs[t], outs[t].at[j], send_sems, recv_sems, 3 * t + k, (x, y, c))
                 for t in range(len(srcs)) for k, _, _, j in others]
        return sends, recvs

    return _Carry(shards, [jax.ShapeDtypeStruct((N_CHIP,) + s.shape, s.dtype) for s in shards], 3 * len(shards), plan)


def _fill_own(outs, shards):
    return [lax.dynamic_update_index_in_dim(o, s, _chip_index(), 0) for o, s in zip(outs, shards)]


def _ag_small(v, *, name):
    m, n = v.shape

    def body(x_ref, out_ref, red_ref, send_sems, recv_sems, local_sem):
        x, y, c = _pos()
        me, sibling = (x, y, c), (x, y, 1 - c)
        chips = _other_chips(x, y)
        rows = lambda px, py, pc: out_ref.at[pl.ds(pl.multiple_of((4 * px + 2 * py + pc) * m, 8), m), :]
        mine = pltpu.make_async_copy(x_ref, rows(*me), local_sem)
        mine.start()
        first = [_remote(x_ref, rows(*me), send_sems, recv_sems, 0, sibling)]
        first += [_remote(x_ref, rows(*me), send_sems, recv_sems, 1 + k, (*chip, c)) for k, chip in enumerate(chips)]
        for cp in first:
            cp.start()
        passed = [_remote(rows(*chip, c), rows(*chip, c), send_sems, recv_sems, 4 + k, sibling) for k, chip in enumerate(chips)]
        for k, chip in enumerate(chips):
            _remote(rows(*chip, c), rows(*chip, c), send_sems, recv_sems, 1 + k, me).wait_recv()
            passed[k].start()
        _remote(rows(*sibling), rows(*sibling), send_sems, recv_sems, 0, me).wait_recv()
        for k, chip in enumerate(chips):
            _remote(rows(*chip, 1 - c), rows(*chip, 1 - c), send_sems, recv_sems, 4 + k, me).wait_recv()
        for cp in first + passed:
            cp.wait_send()
        mine.wait()
        acc = out_ref[0:m, :]
        for d in range(1, N_DEV):
            acc = acc + out_ref[d * m:(d + 1) * m, :]
        red_ref[...] = acc

    return pl.pallas_call(
        body, name=name, in_specs=[VMEM_SPEC], out_specs=[VMEM_SPEC, VMEM_SPEC],
        out_shape=[jax.ShapeDtypeStruct((N_DEV * m, n), v.dtype), jax.ShapeDtypeStruct((m, n), v.dtype)],
        scratch_shapes=[pltpu.SemaphoreType.DMA((7,)), pltpu.SemaphoreType.DMA((7,)), pltpu.SemaphoreType.DMA],
    )(v)


def _halves(ref, c):
    rh = ref.shape[1] // 2
    return ref.at[:, pl.ds(pl.multiple_of(c * rh, 16), rh), :]


def _rs_swap(gs, *, name):
    nt = len(gs)

    def body(*refs):
        g, theirs = refs[:nt], refs[nt:2 * nt]
        send_sems, recv_sems = refs[2 * nt:]
        x, y, c = _pos()
        swaps = [_remote(_halves(g[t], 1 - c), theirs[t], send_sems, recv_sems, t, (x, y, 1 - c)) for t in range(nt)]
        for cp in swaps:
            cp.start()
        for cp in swaps:
            cp.wait()

    return pl.pallas_call(
        body, name=name, in_specs=[ANY] * nt, out_specs=[ANY] * nt,
        out_shape=[jax.ShapeDtypeStruct((a.shape[0], a.shape[1] // 2, a.shape[2]), a.dtype) for a in gs],
        scratch_shapes=[pltpu.SemaphoreType.DMA((nt,)), pltpu.SemaphoreType.DMA((nt,))],
    )(*gs)


def _exchange_carry(ss):
    def plan(s, out, send_sems, recv_sems):
        x, y, c = _pos()
        chip = 2 * x + y
        others = [(k, px, py, 2 * px + py) for k, (px, py) in enumerate(_other_chips(x, y))]
        sends = [_remote(s[t].at[j], out[t].at[chip], send_sems, recv_sems, 3 * t + k, (px, py, c))
                 for t in range(len(s)) for k, px, py, j in others]
        recvs = [_remote(s[t].at[j], out[t].at[j], send_sems, recv_sems, 3 * t + k, (x, y, c))
                 for t in range(len(s)) for k, _, _, j in others]
        return sends, recvs

    return _Carry(ss, [jax.ShapeDtypeStruct(a.shape, a.dtype) for a in ss], 3 * len(ss), plan)


def _fill_own_slab(outs, ss):
    chip = _chip_index()
    return [lax.dynamic_update_index_in_dim(o, lax.dynamic_index_in_dim(s, chip, 0, keepdims=False), chip, 0)
            for o, s in zip(outs, ss)]


def _rs_join(rs, *, name):
    nt = len(rs)

    def body(*refs):
        r, theirs = refs[:nt], refs[nt:2 * nt]
        send_sems, recv_sems = refs[2 * nt:]
        x, y, c = _pos()
        swaps = [_remote(r[t], theirs[t], send_sems, recv_sems, t, (x, y, 1 - c)) for t in range(nt)]
        for cp in swaps:
            cp.start()
        for cp in swaps:
            cp.wait()

    theirs = pl.pallas_call(
        body, name=name, in_specs=[ANY] * nt, out_specs=[ANY] * nt,
        out_shape=[jax.ShapeDtypeStruct(a.shape, a.dtype) for a in rs],
        scratch_shapes=[pltpu.SemaphoreType.DMA((nt,)), pltpu.SemaphoreType.DMA((nt,))],
    )(*rs)
    first = lax.axis_index("c") == 0
    return [jnp.concatenate([jnp.where(first, a, b), jnp.where(first, b, a)], axis=0) for a, b in zip(rs, theirs)]


def _row_tile(rows, dtype):
    unit = 16 if dtype == BF16 else 8
    if rows <= 512:
        return rows
    return next((t for t in (512, 256, 128, 64, 32, 16, 8) if rows % t == 0 and t % unit == 0), rows)


def _add_pair(a, b, *, name):
    n, rh, cols = a.shape
    tr = _row_tile(rh, BF16)

    def body(a_ref, b_ref, o_ref):
        o_ref[...] = (a_ref[...].astype(F32) + b_ref[...].astype(F32)).astype(BF16)

    blk = pl.BlockSpec((None, tr, cols), lambda j, i: (j, i, 0))
    return pl.pallas_call(
        body, name=name, grid=(n, rh // tr), in_specs=[blk, blk], out_specs=blk,
        out_shape=jax.ShapeDtypeStruct(a.shape, BF16), compiler_params=_cp("parallel", "parallel"),
    )(a, b)


def _sum_chips(b, *, name):
    n, rh, cols = b.shape
    tr = _row_tile(rh, BF16)

    def body(b_ref, o_ref):
        acc = b_ref[0].astype(F32)
        for j in range(1, n):
            acc = acc + b_ref[j].astype(F32)
        o_ref[...] = acc

    return pl.pallas_call(
        body, name=name, grid=(rh // tr,), in_specs=[pl.BlockSpec((n, tr, cols), lambda i: (0, i, 0))],
        out_specs=pl.BlockSpec((tr, cols), lambda i: (i, 0)),
        out_shape=jax.ShapeDtypeStruct((rh, cols), F32), compiler_params=_cp("parallel"),
    )(b)


BIG = ("w_in", "w_out", "w_ffn_in", "w_down")
W_IN_SHARD = (N_MAIN + 2 * GH) // N_CHIP


def _rs_pairs(lg, l):
    g_in = jnp.stack([lg["w_in"][:, j * W_IN_SHARD:(j + 1) * W_IN_SHARD] for j in range(N_CHIP)])
    gs = [g_in, lg["w_out"].reshape(N_CHIP, D // N_CHIP, D), lg["w_ffn_in"], lg["w_down"].reshape(N_CHIP, DFF // N_CHIP, D)]
    c = lax.axis_index("c")
    mine = [lax.dynamic_slice_in_dim(a, c * (a.shape[1] // 2), a.shape[1] // 2, axis=1) for a in gs]
    theirs = _rs_swap(gs, name=f"rs{l}_swap")
    return [_add_pair(a, b, name=f"rs{l}_add_{n}") for a, b, n in zip(mine, theirs, BIG)]


def _rs_finish(got, pairs, l):
    got = _fill_own_slab(got, pairs)
    return _rs_join([_sum_chips(b, name=f"rs{l}_sum_{n}") for b, n in zip(got, BIG)], name=f"rs{l}_join")


def _size(shape):
    n = 1
    for s in shape:
        n *= s
    return n


def _pack_flat(parts, total, dtype):
    flat = jnp.concatenate([p.reshape(-1).astype(dtype) for p in parts])
    return jnp.pad(flat, (0, total - flat.shape[0]))


def _unpack_flat(flat, shapes):
    out, o = [], 0
    for s in shapes:
        out.append(flat[o:o + _size(s)].reshape(s))
        o += _size(s)
    return out


WEIGHTS = ("attn_norm", "w_in", "attn_sinks", "gdn_conv_w", "gdn_a_log", "gdn_dt_bias", "gdn_norm", "w_out", "ffn_norm",
           "w_ffn_in", "ffn_conv_w", "ffn_conv_b", "w_down", "final_norm")
SMALL = {"attn_norm": (DEPTH, D), "attn_sinks": (DEPTH, HQ), "gdn_a_log": (DEPTH, GH), "gdn_dt_bias": (DEPTH, GH),
         "gdn_norm": (DEPTH, GD), "ffn_norm": (DEPTH, D), "ffn_conv_b": (DEPTH, DFF), "final_norm": (D,)}
CONV_FULL = {"gdn_conv_w": (DEPTH, GK, 3 * GH * GD), "ffn_conv_w": (DEPTH, FK, DFF)}
CONV_SHARD = {"gdn_conv_w": (DEPTH, GK, 3 * GH * GD // N_CHIP), "ffn_conv_w": (DEPTH, FK, DFF // N_CHIP)}
CONV_ROWS, SMALLG_ROWS, SMALLW_ROWS = 64, 320, 144


def kernel(x, attn_norm, w_in, attn_sinks, gdn_conv_w, gdn_a_log, gdn_dt_bias, gdn_norm, w_out, ffn_norm, w_ffn_in, ffn_conv_w, ffn_conv_b, w_down, final_norm, loss_target, m_attn_norm, m_w_in, m_attn_sinks, m_gdn_conv_w, m_gdn_a_log, m_gdn_dt_bias, m_gdn_norm, m_w_out, m_ffn_norm, m_w_ffn_in, m_ffn_conv_w, m_ffn_conv_b, m_w_down, m_final_norm, v_attn_norm, v_w_in, v_attn_sinks, v_gdn_conv_w, v_gdn_a_log, v_gdn_dt_bias, v_gdn_norm, v_w_out, v_ffn_norm, v_w_ffn_in, v_ffn_conv_w, v_ffn_conv_b, v_w_down, v_final_norm):
    w = dict(zip(WEIGHTS, (attn_norm, w_in, attn_sinks, gdn_conv_w, gdn_a_log, gdn_dt_bias, gdn_norm, w_out, ffn_norm,
                           w_ffn_in, ffn_conv_w, ffn_conv_b, w_down, final_norm)))
    m = dict(zip(WEIGHTS, (m_attn_norm, m_w_in, m_attn_sinks, m_gdn_conv_w, m_gdn_a_log, m_gdn_dt_bias, m_gdn_norm, m_w_out,
                           m_ffn_norm, m_w_ffn_in, m_ffn_conv_w, m_ffn_conv_b, m_w_down, m_final_norm)))
    v = dict(zip(WEIGHTS, (v_attn_norm, v_w_in, v_attn_sinks, v_gdn_conv_w, v_gdn_a_log, v_gdn_dt_bias, v_gdn_norm, v_w_out,
                           v_ffn_norm, v_w_ffn_in, v_ffn_conv_w, v_ffn_conv_b, v_w_down, v_final_norm)))
    cx, cy, _ = _pos()
    chip = 2 * cx + cy

    cpack = _pack_flat([w[n] for n in CONV_SHARD], CONV_ROWS * 128, F32).reshape(CONV_ROWS, 128)
    cgath, _ = _ag_small(cpack, name="gather_conv_w")
    cgath = cgath.reshape(N_DEV, CONV_ROWS * 128)
    cshards = [_unpack_flat(cgath[2 * j], list(CONV_SHARD.values())) for j in range(N_CHIP)]
    conv = {n: jnp.concatenate([cshards[j][i] for j in range(N_CHIP)], axis=2) for i, n in enumerate(CONV_SHARD)}

    Ps = [_layer_params(attn_norm[l], attn_sinks[l], conv["gdn_conv_w"][l], gdn_a_log[l], gdn_dt_bias[l], gdn_norm[l],
                        ffn_norm[l], conv["ffn_conv_w"][l], ffn_conv_b[l]) for l in range(DEPTH)]

    wb = {n: [w[n][l].astype(BF16) for l in range(DEPTH)] for n in BIG}
    mixer_w = lambda g_in, g_out: dict(all=_layer_weights(jnp.concatenate(list(g_in), axis=1)), out=g_out.reshape(D, D))
    ffn_w = lambda g_ffn, g_down: dict(ffn=g_ffn, down=g_down.reshape(DFF, D))
    first = _gather_carry([wb["w_in"][0], wb["w_out"][0]])
    Wm0 = mixer_w(*_fill_own(_run_carry(first, name="gather_l0_mixer"), first.arrays))
    ride = _gather_carry([wb["w_ffn_in"][0], wb["w_down"][0]])
    h, sv0, got = _mixer_fwd(x[0], Wm0, Ps[0], 0, carry=ride)
    Wf0 = ffn_w(*_fill_own(got, ride.arrays))
    ride = _gather_carry([wb["w_in"][1], wb["w_out"][1]])
    h, sv0f, got = _ffn_fwd(h, Wf0, Ps[0], 0, carry=ride)
    Wm1 = mixer_w(*_fill_own(got, ride.arrays))
    ride = _gather_carry([wb["w_ffn_in"][1], wb["w_down"][1]])
    h, sv1, got = _mixer_fwd(h, Wm1, Ps[1], 1, carry=ride)
    Wf1 = ffn_w(*_fill_own(got, ride.arrays))
    h, sv1f, _ = _ffn_fwd(h, Wf1, Ps[1], 1)
    dx, g_final, loss_part = _loss_head(h, final_norm[None], loss_target[0], name="loss_head")

    lg, reduced = [None] * DEPTH, [None] * DEPTH
    dx, lg[1], _ = _layer_bwd(dx, {**sv1, **sv1f}, {**Wm1, **Wf1}, Ps[1], 1)
    pairs1 = _rs_pairs(lg[1], 1)
    dx, lg[0], got1 = _layer_bwd(dx, {**sv0, **sv0f}, {**Wm0, **Wf0}, Ps[0], 0, carry=_exchange_carry(pairs1))
    reduced[1] = _rs_finish(got1, pairs1, 1)
    pairs0 = _rs_pairs(lg[0], 0)
    reduced[0] = _rs_finish(_run_carry(_exchange_carry(pairs0), name="rs0_exchange"), pairs0, 0)
    grad_x = dx[None]
    stacked = lambda n: jnp.stack([lg[l][n] for l in range(DEPTH)])

    small_parts = [g_final[0] if n == "final_norm" else stacked(n) for n in SMALL] + [stacked(n) for n in CONV_FULL]
    svec = _pack_flat(small_parts + [loss_part[0, :1]], SMALLG_ROWS * 128, F32).reshape(SMALLG_ROWS, 128)
    _, sred = _ag_small(svec, name="reduce_small")
    small_g = _unpack_flat(sred.reshape(-1), list(SMALL.values()) + list(CONV_FULL.values()) + [(1,)])
    g = dict(zip(list(SMALL) + list(CONV_FULL), small_g[:-1]))
    loss = small_g[-1][0]
    for n in CONV_FULL:
        wd = CONV_SHARD[n][2]
        g[n] = lax.dynamic_slice_in_dim(g[n], chip * wd, wd, axis=2)

    g.update({n: jnp.stack([reduced[l][i] for l in range(DEPTH)]) for i, n in enumerate(BIG)})

    delta, new_m, new_v = {}, {}, {}
    for n in BIG:
        delta[n], new_m[n], new_v[n] = _adamw(w[n], g[n], m[n], v[n], name=f"adamw_{n}")
    small_names = list(SMALL) + list(CONV_SHARD)
    small_shapes = list(SMALL.values()) + list(CONV_SHARD.values())
    packed = [_pack_flat([t[n] for n in small_names], SMALLW_ROWS * 128, F32).reshape(SMALLW_ROWS, 128) for t in (w, g, m, v)]
    for res, out in zip(_adamw(*packed, name="adamw_small"), (delta, new_m, new_v)):
        out.update(zip(small_names, _unpack_flat(res.reshape(-1), small_shapes)))

    return (loss, grad_x, *[g[n] for n in WEIGHTS], *[delta[n] for n in WEIGHTS], *[new_m[n] for n in WEIGHTS],
            *[new_v[n] for n in WEIGHTS])
```

```python
import functools

import jax
import jax.numpy as jnp
import numpy as np
from jax import lax
from jax.experimental import pallas as pl
from jax.experimental.pallas import tpu as pltpu

F32, BF16 = jnp.float32, jnp.bfloat16
HIGHEST = lax.Precision.HIGHEST
MESH = pl.DeviceIdType.MESH

D = 1024
T = 4096
DEPTH = 2
HQ, HKV, DH, WIN = 8, 2, 64, 128
GH, GD, GC, GK = 4, 128, 64, 4
DFF, FK = 2816, 3
EPS = 1e-6
N_MAIN = 2816
N_PROJ = N_MAIN + 128
COL_K, COL_V, COL_G, COL_Z, COL_BA = 4, 5, 6, 18, 22
N_CHUNK = T // GC
N_DEV, N_CHIP = 8, 4
FFN_CW = 2 * DFF // N_CHIP
LR, B1, B2, AEPS, WD, STEP = 0.001, 0.9, 0.999, 1e-08, 0.01, 10

VMEM_LIMIT = 56 * 1024 * 1024
ANY = pl.BlockSpec(memory_space=pl.ANY)
VMEM_SPEC = pl.BlockSpec(memory_space=pltpu.VMEM)


def _cp(*sem):
    return pltpu.CompilerParams(dimension_semantics=sem if sem else None, vmem_limit_bytes=VMEM_LIMIT)


class _Carry:
    def __init__(self, arrays, out_shapes, nsem, plan):
        self.arrays, self.out_shapes, self.nsem, self.plan = list(arrays), list(out_shapes), nsem, plan


def _seq_call(body, carry, *, name, steps, in_specs, out_specs, out_shape, args, scratch_shapes=(), aliases=None):
    in_specs, out_specs, out_shape, scratch = list(in_specs), list(out_specs), list(out_shape), list(scratch_shapes)
    n_in, n_out, n_scr = len(in_specs), len(out_specs), len(scratch)
    if carry is None:
        fn = body
    else:
        ci, co = len(carry.arrays), len(carry.out_shapes)

        def fn(*refs):
            ins, cins = refs[:n_in], refs[n_in:n_in + ci]
            outs, couts = refs[n_in + ci:n_in + ci + n_out], refs[n_in + ci + n_out:n_in + ci + n_out + co]
            scr, (ssem, rsem) = refs[n_in + ci + n_out + co:-2], refs[-2:]
            sends, recvs = carry.plan(cins, couts, ssem, rsem)

            @pl.when(pl.program_id(0) == 0)
            def _():
                for cp in sends:
                    cp.start()

            body(*ins, *outs, *scr)

            @pl.when(pl.program_id(0) == steps - 1)
            def _():
                for cp in recvs:
                    cp.wait_recv()
                for cp in sends:
                    cp.wait_send()

        in_specs += [ANY] * ci
        out_specs += [ANY] * co
        out_shape += carry.out_shapes
        scratch += [pltpu.SemaphoreType.DMA((carry.nsem,)), pltpu.SemaphoreType.DMA((carry.nsem,))]
        args = list(args) + carry.arrays
    outs = pl.pallas_call(
        fn, name=name, grid=(steps,), in_specs=in_specs, out_specs=out_specs, out_shape=out_shape,
        scratch_shapes=scratch, input_output_aliases=aliases or {}, compiler_params=_cp("arbitrary"),
    )(*args)
    return list(outs[:n_out]), list(outs[n_out:])


def _dot(a, b):
    return jnp.dot(a.astype(BF16), b.astype(BF16), preferred_element_type=F32)


def _dot_nt(a, b):
    return lax.dot_general(a.astype(BF16), b.astype(BF16), (((1,), (1,)), ((), ())), preferred_element_type=F32)


def _dot_tn(a, b):
    return lax.dot_general(a.astype(BF16), b.astype(BF16), (((0,), (0,)), ((), ())), preferred_element_type=F32)


def _iota(shape, dim):
    return lax.broadcasted_iota(jnp.int32, shape, dim)


def _col(x, idx):
    return jnp.sum(jnp.where(_iota(x.shape, 1) == idx, x, 0.0), axis=1, keepdims=True)


def _silu(y):
    return y * jax.nn.sigmoid(y)


_MM_DN = {"nn": (((1,), (0,)), ((), ())), "nt": (((1,), (1,)), ((), ())), "tn": (((0,), (0,)), ((), ()))}


def _mm(a, b, *, name, mode, dims, tm, tn, tk, res=None, out_dtype=F32, a_spec=None, b_spec=None, out_spec=None,
        out_shape=None, n_outer=False):
    M, N, K = dims
    assert M % tm == 0 and N % tn == 0 and K % tk == 0, (name, dims)
    nk = K // tk
    dn = _MM_DN[mode]

    def body(*refs):
        if res is None:
            a_ref, b_ref, o_ref, acc = refs
        else:
            a_ref, b_ref, r_ref, o_ref, acc = refs
        k = pl.program_id(2)

        @pl.when(k == 0)
        def _():
            acc[...] = jnp.zeros_like(acc)

        acc[...] += lax.dot_general(a_ref[...].astype(BF16), b_ref[...].astype(BF16), dn, preferred_element_type=F32)

        @pl.when(k == nk - 1)
        def _():
            r = acc[...]
            if res is not None:
                r = r + r_ref[...]
            o_ref[...] = r.astype(out_dtype)

    if a_spec is None:
        a_spec = pl.BlockSpec((tk, tm), lambda i, j, k: (k, i)) if mode == "tn" else pl.BlockSpec((tm, tk), lambda i, j, k: (i, k))
    if b_spec is None:
        b_spec = pl.BlockSpec((tn, tk), lambda i, j, k: (j, k)) if mode == "nt" else pl.BlockSpec((tk, tn), lambda i, j, k: (k, j))
    in_specs, args = [a_spec, b_spec], [a, b]
    if res is not None:
        in_specs.append(pl.BlockSpec((tm, tn), lambda i, j, k: (i, j)))
        args.append(res)
    out_spec = out_spec or pl.BlockSpec((tm, tn), lambda i, j, k: (i, j))
    grid = (M // tm, N // tn, nk)
    if n_outer:
        swap = lambda s: pl.BlockSpec(s.block_shape, lambda j, i, k, f=s.index_map: f(i, j, k))
        in_specs, out_spec, grid = [swap(s) for s in in_specs], swap(out_spec), (N // tn, M // tm, nk)
    return pl.pallas_call(
        body, name=name, grid=grid, in_specs=in_specs, out_specs=out_spec,
        out_shape=out_shape or jax.ShapeDtypeStruct((M, N), out_dtype),
        scratch_shapes=[pltpu.VMEM((tm, tn), F32)],
        compiler_params=_cp("parallel", "parallel", "arbitrary"),
    )(*args)


def _rms(x, w):
    return x * lax.rsqrt(jnp.mean(x * x, axis=-1, keepdims=True) + EPS) * w


def _rmsnorm_fwd(x, w, *, name, tm=512):
    def body(x_ref, w_ref, o_ref):
        o_ref[...] = _rms(x_ref[...], w_ref[...]).astype(BF16)

    return pl.pallas_call(
        body, name=name, grid=(x.shape[0] // tm,),
        in_specs=[pl.BlockSpec((tm, D), lambda i: (i, 0)), pl.BlockSpec((1, D), lambda i: (0, 0))],
        out_specs=pl.BlockSpec((tm, D), lambda i: (i, 0)),
        out_shape=jax.ShapeDtypeStruct(x.shape, BF16), compiler_params=_cp("parallel"),
    )(x, w)


def _rmsnorm_bwd(x, w, dh, dres, *, name, tm=512):
    def body(x_ref, w_ref, dh_ref, dr_ref, dx_ref, dw_ref):
        _, vjp = jax.vjp(_rms, x_ref[...], w_ref[...])
        dx, dw = vjp(dh_ref[...])
        dx_ref[...] = dx + dr_ref[...]

        @pl.when(pl.program_id(0) == 0)
        def _():
            dw_ref[...] = jnp.zeros_like(dw_ref)

        dw_ref[...] += dw

    row = pl.BlockSpec((tm, D), lambda i: (i, 0))
    vec = pl.BlockSpec((1, D), lambda i: (0, 0))
    return pl.pallas_call(
        body, name=name, grid=(x.shape[0] // tm,), in_specs=[row, vec, row, row], out_specs=[row, vec],
        out_shape=[jax.ShapeDtypeStruct(x.shape, F32), jax.ShapeDtypeStruct((1, D), F32)],
        compiler_params=_cp("arbitrary"),
    )(x, w, dh, dres)


def _loss_head(x, w, tgt, *, name, tm=512):
    def f(xv, wv, tv):
        err = _rms(xv, wv) - tv
        per_row = jnp.sum(err * err, axis=1, keepdims=True) * (0.5 / D)
        return jnp.sum(per_row, axis=0, keepdims=True)

    def body(x_ref, w_ref, t_ref, dx_ref, dw_ref, loss_ref):
        tv = t_ref[...]
        loss, vjp = jax.vjp(lambda xv, wv: f(xv, wv, tv), x_ref[...], w_ref[...])
        dx, dw = vjp(jnp.ones((1, 1), F32))
        dx_ref[...] = dx

        @pl.when(pl.program_id(0) == 0)
        def _():
            dw_ref[...] = jnp.zeros_like(dw_ref)
            loss_ref[...] = jnp.zeros_like(loss_ref)

        dw_ref[...] += dw
        loss_ref[...] += jnp.broadcast_to(loss, loss_ref.shape)

    row = pl.BlockSpec((tm, D), lambda i: (i, 0))
    vec = pl.BlockSpec((1, D), lambda i: (0, 0))
    return pl.pallas_call(
        body, name=name, grid=(x.shape[0] // tm,), in_specs=[row, vec, row],
        out_specs=[row, vec, pl.BlockSpec((1, 128), lambda i: (0, 0))],
        out_shape=[jax.ShapeDtypeStruct(x.shape, F32), jax.ShapeDtypeStruct((1, D), F32),
                   jax.ShapeDtypeStruct((1, 128), F32)],
        compiler_params=_cp("arbitrary"),
    )(x, w, tgt)


def _swa_bias():
    G = HQ // HKV
    r, c = np.arange(G * WIN)[:, None], np.arange(2 * WIN)[None, :]
    rel = (r % WIN) + WIN - c
    band = (rel >= 0) & (rel < WIN)
    out = np.empty((2, HKV, G * WIN, 2 * WIN), np.float32)
    for h2 in range(HKV):
        slope = 2.0 ** (-8.0 * (h2 * G + r // WIN + 1) / HQ)
        out[0, h2] = np.where(band & (c >= WIN), -slope * rel, -1e30)
        out[1, h2] = np.where(band, -slope * rel, -1e30)
    return jnp.asarray(out)


def _swa_group(qg, k2, v2, sk, bias, h2):
    s = _dot_nt(qg, k2) * (DH ** -0.5) + bias
    g1 = _iota((s.shape[0], 1), 0) >> 7
    sink = jnp.zeros((s.shape[0], 1), F32)
    for gi in range(HQ // HKV):
        sink = jnp.where(g1 == gi, _col(sk, h2 * (HQ // HKV) + gi), sink)
    m = lax.stop_gradient(jnp.maximum(jnp.max(s, axis=1, keepdims=True), sink))
    p = jnp.exp(s - m)
    den = jnp.sum(p, axis=1, keepdims=True) + jnp.exp(sink - m)
    return _dot(p * (1.0 / den), v2)


def _swa_split_q(q, h2):
    G = HQ // HKV
    return jnp.concatenate([q[:, (h2 * G + g) * DH:(h2 * G + g + 1) * DH] for g in range(G)], axis=0)


def _swa_merge_q(parts):
    G = HQ // HKV
    return jnp.concatenate([parts[h2][g * WIN:(g + 1) * WIN] for h2 in range(HKV) for g in range(G)], axis=1)


def _swa_specs():
    prev = lambda i: jnp.maximum(jnp.minimum(i, T // WIN - 1) - 1, 0)
    cur = lambda i: jnp.minimum(i, T // WIN - 1)
    return [
        pl.BlockSpec((WIN, HQ * DH), lambda i: (cur(i), 0)),
        pl.BlockSpec((WIN, 128), lambda i: (prev(i), COL_K)),
        pl.BlockSpec((WIN, 128), lambda i: (cur(i), COL_K)),
        pl.BlockSpec((WIN, 128), lambda i: (prev(i), COL_V)),
        pl.BlockSpec((WIN, 128), lambda i: (cur(i), COL_V)),
        pl.BlockSpec((1, 128), lambda i: (0, 0)),
        pl.BlockSpec((None, HKV, (HQ // HKV) * WIN, 2 * WIN), lambda i: (jnp.minimum(i, 1), 0, 0, 0)),
    ]


def _swa_fwd(proj, sinks, *, name):
    def body(q_ref, kp_ref, kc_ref, vp_ref, vc_ref, sk_ref, bias_ref, o_ref):
        q, sk = q_ref[...].astype(F32), sk_ref[...]
        outs = []
        for h2 in range(HKV):
            sl = slice(h2 * DH, (h2 + 1) * DH)
            k2 = jnp.concatenate([kp_ref[:, sl], kc_ref[:, sl]], axis=0).astype(F32)
            v2 = jnp.concatenate([vp_ref[:, sl], vc_ref[:, sl]], axis=0).astype(F32)
            outs.append(_swa_group(_swa_split_q(q, h2), k2, v2, sk, bias_ref[h2], h2))
        o_ref[...] = _swa_merge_q(outs).astype(BF16)

    return pl.pallas_call(
        body, name=name, grid=(T // WIN,), in_specs=_swa_specs(),
        out_specs=pl.BlockSpec((WIN, HQ * DH), lambda i: (i, 0)),
        out_shape=jax.ShapeDtypeStruct((T, HQ * DH + GH * GD), BF16), compiler_params=_cp("parallel"),
    )(proj, proj, proj, proj, proj, sinks, _swa_bias())


def _swa_bwd(proj, sinks, do, dproj, *, name):
    NB = T // WIN
    QW = HQ * DH

    def body(q_ref, kp_ref, kc_ref, vp_ref, vc_ref, sk_ref, bias_ref, do_ref, _, out_ref, dsk_ref, cq, ck, cv):
        i = pl.program_id(0)

        @pl.when(i == 0)
        def _():
            cq[...] = jnp.zeros_like(cq)
            ck[...] = jnp.zeros_like(ck)
            cv[...] = jnp.zeros_like(cv)
            dsk_ref[...] = jnp.zeros_like(dsk_ref)

        @pl.when(i < NB)
        def _():
            q, sk, dov = q_ref[...].astype(F32), sk_ref[...], do_ref[...].astype(F32)
            dqs, dkp, dkc, dvp, dvc = [], [], [], [], []
            dsk = jnp.zeros_like(sk)
            for h2 in range(HKV):
                sl = slice(h2 * DH, (h2 + 1) * DH)
                k2 = jnp.concatenate([kp_ref[:, sl], kc_ref[:, sl]], axis=0).astype(F32)
                v2 = jnp.concatenate([vp_ref[:, sl], vc_ref[:, sl]], axis=0).astype(F32)
                _, vjp = jax.vjp(functools.partial(_swa_group, bias=bias_ref[h2], h2=h2), _swa_split_q(q, h2), k2, v2, sk)
                dqg, dk2, dv2, dsk_h = vjp(_swa_split_q(dov, h2))
                dqs.append(dqg)
                dkp.append(dk2[:WIN]); dkc.append(dk2[WIN:])
                dvp.append(dv2[:WIN]); dvc.append(dv2[WIN:])
                dsk = dsk + dsk_h
            out_ref[:, :QW] = cq[...].astype(BF16)
            out_ref[:, QW:QW + 128] = (ck[...] + jnp.concatenate(dkp, axis=1)).astype(BF16)
            out_ref[:, QW + 128:] = (cv[...] + jnp.concatenate(dvp, axis=1)).astype(BF16)
            cq[...] = _swa_merge_q(dqs)
            ck[...] = jnp.concatenate(dkc, axis=1)
            cv[...] = jnp.concatenate(dvc, axis=1)
            dsk_ref[...] += dsk

        @pl.when(i == NB)
        def _():
            out_ref[:, :QW] = cq[...].astype(BF16)
            out_ref[:, QW:QW + 128] = ck[...].astype(BF16)
            out_ref[:, QW + 128:] = cv[...].astype(BF16)

    qblk = pl.BlockSpec((WIN, QW), lambda i: (jnp.minimum(i, NB - 1), 0))
    return pl.pallas_call(
        body, name=name, grid=(NB + 1,), in_specs=_swa_specs() + [qblk, ANY],
        out_specs=[pl.BlockSpec((WIN, QW + 256), lambda i: (jnp.maximum(i - 1, 0), 0)), pl.BlockSpec((1, 128), lambda i: (0, 0))],
        out_shape=[jax.ShapeDtypeStruct(dproj.shape, dproj.dtype), jax.ShapeDtypeStruct((1, 128), F32)],
        scratch_shapes=[pltpu.VMEM((WIN, QW), F32), pltpu.VMEM((WIN, 128), F32), pltpu.VMEM((WIN, 128), F32)],
        input_output_aliases={8: 0}, compiler_params=_cp("arbitrary"),
    )(proj, proj, proj, proj, proj, sinks, _swa_bias(), do, dproj)


RC = 256
HALO = 8


def _load_ext(ref, c):
    nch = T // RC
    r0 = pl.multiple_of(c * RC, RC)
    p0 = pl.multiple_of(jnp.maximum(r0 - HALO, 0), HALO)
    n0 = pl.multiple_of(jnp.minimum(r0 + RC, T - HALO), HALO)
    prev = jnp.where(c > 0, ref[pl.ds(p0, HALO), :].astype(F32), 0.0)
    nxt = jnp.where(c < nch - 1, ref[pl.ds(n0, HALO), :].astype(F32), 0.0)
    return jnp.concatenate([prev, ref[pl.ds(r0, RC), :].astype(F32), nxt], axis=0)


def _conv_ext(xe, w, K):
    y = w[K - 1:K, :] * xe
    for s in range(1, K):
        y = y + w[K - 1 - s:K - s, :] * pltpu.roll(xe, s, 0)
    return y


def _conv_bwd_ext(xe, dye, w, K, dw_ref):
    n = xe.shape[0]
    own = slice(HALO, HALO + RC)
    dx = w[K - 1:K, :] * dye
    dw_ref[K - 1:K, :] += jnp.sum(dye[own] * xe[own], axis=0, keepdims=True)
    for s in range(1, K):
        dx = dx + w[K - 1 - s:K - s, :] * pltpu.roll(dye, n - s, 0)
        dw_ref[K - 1 - s:K - s, :] += jnp.sum(dye[own] * pltpu.roll(xe, s, 0)[own], axis=0, keepdims=True)
    return dx[own]


def _gdn_post_conv(y, is_qk):
    a = _silu(y)
    nrm = a * lax.rsqrt(jnp.sum(a * a, axis=1, keepdims=True) + EPS)
    return jnp.where(is_qk, nrm, a)


def _gdn_prep_fwd(proj, conv_w, *, name):
    nblk = 3 * GH

    def body(x_ref, w_ref, o_ref):
        is_qk = pl.program_id(0) < 2 * GH
        w = w_ref[...]

        def chunk(c, carry):
            y = _conv_ext(_load_ext(x_ref, c), w, GK)[HALO:HALO + RC]
            o_ref[pl.ds(pl.multiple_of(c * RC, RC), RC), :] = _gdn_post_conv(y, is_qk)
            return carry

        lax.fori_loop(0, T // RC, chunk, 0)

    return pl.pallas_call(
        body, name=name, grid=(nblk,),
        in_specs=[pl.BlockSpec((T, 128), lambda j: (0, COL_G + j)), pl.BlockSpec((GK, 128), lambda j: (0, j))],
        out_specs=pl.BlockSpec((T, 128), lambda j: (0, j)),
        out_shape=jax.ShapeDtypeStruct((T, nblk * 128), F32), compiler_params=_cp("parallel"),
    )(proj, conv_w)


def _gdn_prep_bwd(proj, conv_w, dout, dproj, *, name):
    nblk = 3 * GH

    def body(x_ref, w_ref, d_ref, _, dx_ref, dw_ref):
        is_qk = pl.program_id(0) < 2 * GH
        w = w_ref[...]
        dw_ref[...] = jnp.zeros_like(dw_ref)

        def chunk(c, carry):
            xe = _load_ext(x_ref, c)
            _, vjp = jax.vjp(lambda y: _gdn_post_conv(y, is_qk), _conv_ext(xe, w, GK))
            (dye,) = vjp(_load_ext(d_ref, c))
            dx_ref[pl.ds(pl.multiple_of(c * RC, RC), RC), :] = _conv_bwd_ext(xe, dye, w, GK, dw_ref).astype(BF16)
            return carry

        lax.fori_loop(0, T // RC, chunk, 0)

    return pl.pallas_call(
        body, name=name, grid=(nblk,),
        in_specs=[pl.BlockSpec((T, 128), lambda j: (0, COL_G + j)), pl.BlockSpec((GK, 128), lambda j: (0, j)),
                  pl.BlockSpec((T, 128), lambda j: (0, j)), ANY],
        out_specs=[pl.BlockSpec((T, 128), lambda j: (0, COL_G + j)), pl.BlockSpec((GK, 128), lambda j: (0, j))],
        out_shape=[jax.ShapeDtypeStruct(dproj.shape, dproj.dtype), jax.ShapeDtypeStruct((GK, nblk * 128), F32)],
        input_output_aliases={3: 0}, compiler_params=_cp("parallel"),
    )(proj, conv_w, dout, dproj)


def _ffn_post_conv(y, b, up):
    return _silu(y + b) * up


def _ffn_act_fwd(gu, conv_w, conv_b, *, name, carry=None):
    nblk = DFF // 128

    def body(g_ref, u_ref, w_ref, b_ref, o_ref):
        w, b = w_ref[...], b_ref[...]

        def chunk(c, carry):
            rows = pl.ds(pl.multiple_of(c * RC, RC), RC)
            y = _conv_ext(_load_ext(g_ref, c), w, FK)[HALO:HALO + RC]
            o_ref[rows, :] = _ffn_post_conv(y, b, u_ref[rows, :].astype(F32)).astype(BF16)
            return carry

        lax.fori_loop(0, T // RC, chunk, 0)

    (act,), carried = _seq_call(
        body, carry, name=name, steps=nblk,
        in_specs=[pl.BlockSpec((T, 128), lambda j: (0, j)), pl.BlockSpec((T, 128), lambda j: (0, nblk + j)),
                  pl.BlockSpec((FK, 128), lambda j: (0, j)), pl.BlockSpec((1, 128), lambda j: (0, j))],
        out_specs=[pl.BlockSpec((T, 128), lambda j: (0, j))], out_shape=[jax.ShapeDtypeStruct((T, DFF), BF16)],
        args=(gu, gu, conv_w, conv_b))
    return act, carried


def _ffn_act_bwd(gu, conv_w, conv_b, dact, *, name):
    nblk = DFF // 128

    def body(g_ref, u_ref, w_ref, b_ref, d_ref, dgu_ref, dw_ref, db_ref):
        dg_ref, du_ref = dgu_ref.at[0], dgu_ref.at[1]
        w, b = w_ref[...], b_ref[...]
        dw_ref[...] = jnp.zeros_like(dw_ref)
        db_ref[...] = jnp.zeros_like(db_ref)

        def chunk(c, carry):
            rows = pl.ds(pl.multiple_of(c * RC, RC), RC)
            xe = _load_ext(g_ref, c)
            ue = _load_ext(u_ref, c)
            _, vjp = jax.vjp(_ffn_post_conv, _conv_ext(xe, w, FK), b, ue)
            dye, db, due = vjp(_load_ext(d_ref, c))
            du_ref[rows, :] = due[HALO:HALO + RC].astype(BF16)
            db_ref[...] += jnp.sum(dye[HALO:HALO + RC], axis=0, keepdims=True)
            dg_ref[rows, :] = _conv_bwd_ext(xe, dye, w, FK, dw_ref).astype(BF16)
            return carry

        lax.fori_loop(0, T // RC, chunk, 0)

    col = pl.BlockSpec((T, 128), lambda j: (0, j))
    return pl.pallas_call(
        body, name=name, grid=(nblk,),
        in_specs=[col, pl.BlockSpec((T, 128), lambda j: (0, nblk + j)), pl.BlockSpec((FK, 128), lambda j: (0, j)),
                  pl.BlockSpec((1, 128), lambda j: (0, j)), col],
        out_specs=[pl.BlockSpec((2, T, 128), lambda j: (0, 0, j)), pl.BlockSpec((FK, 128), lambda j: (0, j)),
                   pl.BlockSpec((1, 128), lambda j: (0, j))],
        out_shape=[jax.ShapeDtypeStruct((2, T, DFF), BF16), jax.ShapeDtypeStruct((FK, DFF), F32),
                   jax.ShapeDtypeStruct((1, DFF), F32)],
        compiler_params=_cp("parallel"),
    )(gu, gu, conv_w, conv_b, dact)


def _gdn_gates(ba, alog, dtb):
    beta = jax.nn.sigmoid(ba)
    g = -jnp.exp(alog) * jax.nn.softplus(ba + dtb)
    tril = (_iota((GC, GC), 0) >= _iota((GC, GC), 1)).astype(F32)
    return beta, jnp.dot(tril, g, precision=HIGHEST, preferred_element_type=F32)


def _hmap(f, *lists):
    return [f(*xs) for xs in zip(*lists)]


def _gdn_cols(beta_all, gc_all):
    return [_col(beta_all, h) for h in range(GH)], [_col(gc_all, GH + h) for h in range(GH)]


def _gdn_decay(Gcs):
    r, c = _iota((GC, GC), 0), _iota((GC, GC), 1)
    eye, ones = (r == c).astype(F32), jnp.ones((GC, GC), F32)
    grows = _hmap(lambda G: jnp.dot(ones, eye * G, precision=HIGHEST, preferred_element_type=F32), Gcs)
    return _hmap(lambda G, grow: jnp.exp(jnp.where(r >= c, G - grow, -1e30)), Gcs, grows)


def _gdn_A(ks, betas, Gcs):
    strict = _iota((GC, GC), 0) > _iota((GC, GC), 1)
    kk = _hmap(lambda k, b: _dot_nt(k * b, k), ks, betas)
    return _hmap(lambda a, d: jnp.where(strict, a * d, 0.0), kk, _gdn_decay(Gcs))


def _tri_inv(As):
    eye = (_iota((GC, GC), 0) == _iota((GC, GC), 1)).astype(F32)
    Tms, Ps = [eye - A for A in As], As
    for _ in range(GC.bit_length() - 2):
        Ps = _hmap(lambda P: _dot(P, P), Ps)
        Tms = _hmap(lambda Tm, P: Tm + _dot(Tm, P), Tms, Ps)
    return Tms


def _gdn_chunk(qs, ks, vs, betas, Gcs, Ss, Tms):
    decays = _gdn_decay(Gcs)
    eGs = _hmap(jnp.exp, Gcs)
    us = _hmap(lambda Tm, v, b: _dot(Tm, v * b), Tms, vs, betas)
    ws = _hmap(lambda Tm, k, b, eG: _dot(Tm, k * b * eG), Tms, ks, betas, eGs)
    qss = [q * (GD ** -0.5) for q in qs]
    qks = _hmap(lambda q, k, d: _dot_nt(q, k) * d, qss, ks, decays)
    glasts = [jnp.sum(jnp.where(_iota(G.shape, 0) == GC - 1, G, 0.0), axis=0, keepdims=True) for G in Gcs]
    kds = _hmap(lambda k, gl, G: k * jnp.exp(gl - G), ks, glasts, Gcs)
    v_news = _hmap(lambda u, w, S: u - _dot(w, S), us, ws, Ss)
    qS = _hmap(lambda q, eG, S: _dot(q * eG, S), qss, eGs, Ss)
    os = _hmap(lambda a, qk, vn: a + _dot(qk, vn), qS, qks, v_news)
    S_news = _hmap(lambda S, gl, kd, vn: S * jnp.exp(gl) + _dot_tn(kd, vn), Ss, glasts, kds, v_news)
    return os, S_news


def _gdn_chunk_fwd(qkv_c, ba, alog, dtb, *, name, carry=None):
    W3 = 3 * GH * GD

    def body(x_ref, ba_ref, al_ref, dt_ref, o_ref, s_ref, t_ref, S):
        @pl.when(pl.program_id(0) == 0)
        def _():
            S[...] = jnp.zeros_like(S)

        beta_all, gc_all = _gdn_gates(ba_ref[...], al_ref[...], dt_ref[...])
        qs, ks, vs = ([x_ref[:, (p * GH + h) * GD:(p * GH + h + 1) * GD] for h in range(GH)] for p in range(3))
        betas, Gcs = _gdn_cols(beta_all, gc_all)
        Tms = _tri_inv(_gdn_A(ks, betas, Gcs))
        Ss = [S[h] for h in range(GH)]
        os, S_news = _gdn_chunk(qs, ks, vs, betas, Gcs, Ss, Tms)
        for h in range(GH):
            s_ref[0, h] = Ss[h]
            t_ref[0, h] = Tms[h]
            o_ref[:, h * GD:(h + 1) * GD] = os[h]
            S[h] = S_news[h]

    vec = pl.BlockSpec((1, 128), lambda n: (0, 0))
    return _seq_call(
        body, carry, name=name, steps=N_CHUNK,
        in_specs=[pl.BlockSpec((GC, W3), lambda n: (n, 0)), pl.BlockSpec((GC, 128), lambda n: (n, 0)), vec, vec],
        out_specs=[pl.BlockSpec((GC, GH * GD), lambda n: (n, 0)),
                   pl.BlockSpec((1, GH, GD, GD), lambda n: (n, 0, 0, 0)),
                   pl.BlockSpec((1, GH, GC, GC), lambda n: (n, 0, 0, 0))],
        out_shape=[jax.ShapeDtypeStruct((T, GH * GD), F32), jax.ShapeDtypeStruct((N_CHUNK, GH, GD, GD), F32),
                   jax.ShapeDtypeStruct((N_CHUNK, GH, GC, GC), F32)],
        scratch_shapes=[pltpu.VMEM((GH, GD, GD), F32)], args=(qkv_c, ba, alog, dtb))


def _gdn_chunk_bwd(qkv_c, ba, alog, dtb, s_all, t_all, do, dproj, *, name, carry=None):
    W3 = 3 * GH * GD
    rev = lambda n: N_CHUNK - 1 - n

    def body(x_ref, ba_ref, al_ref, dt_ref, s_ref, t_ref, do_ref, _, dx_ref, dba_ref, dal_ref, ddt_ref, dS):
        @pl.when(pl.program_id(0) == 0)
        def _():
            dS[...] = jnp.zeros_like(dS)
            dal_ref[...] = jnp.zeros_like(dal_ref)
            ddt_ref[...] = jnp.zeros_like(ddt_ref)

        (beta_all, gc_all), vjp_gates = jax.vjp(_gdn_gates, ba_ref[...], al_ref[...], dt_ref[...])
        qs, ks, vs = ([x_ref[:, (p * GH + h) * GD:(p * GH + h + 1) * GD] for h in range(GH)] for p in range(3))
        Tms = [t_ref[0, h] for h in range(GH)]
        Ss = [s_ref[0, h] for h in range(GH)]
        dos = [do_ref[:, h * GD:(h + 1) * GD] for h in range(GH)]
        dSs = [dS[h] for h in range(GH)]

        def f(qs, ks, vs, b_all, g_all, Ss, Tms):
            return _gdn_chunk(qs, ks, vs, *_gdn_cols(b_all, g_all), Ss, Tms)

        def fa(ks, b_all, g_all):
            return _gdn_A(ks, *_gdn_cols(b_all, g_all))

        _, vjp = jax.vjp(f, qs, ks, vs, beta_all, gc_all, Ss, Tms)
        dqs, dks, dvs, db1, dg1, dS_prev, dTs = vjp((dos, dSs))
        dXs = _hmap(_dot_nt, dTs, Tms)
        dAs = _hmap(lambda Tm, dX: -_dot_tn(Tm, dX), Tms, dXs)
        _, vjp_a = jax.vjp(fa, ks, beta_all, gc_all)
        dks2, db2, dg2 = vjp_a(dAs)
        for h in range(GH):
            dS[h] = dS_prev[h]
            dx_ref[:, h * GD:(h + 1) * GD] = dqs[h]
            dx_ref[:, (GH + h) * GD:(GH + h + 1) * GD] = dks[h] + dks2[h]
            dx_ref[:, (2 * GH + h) * GD:(2 * GH + h + 1) * GD] = dvs[h]
        dba, dal, ddt = vjp_gates((db1 + db2, dg1 + dg2))
        dba_ref[...] = dba.astype(BF16)
        dal_ref[...] += dal
        ddt_ref[...] += ddt

    vec = pl.BlockSpec((1, 128), lambda n: (0, 0))
    return _seq_call(
        body, carry, name=name, steps=N_CHUNK,
        in_specs=[pl.BlockSpec((GC, W3), lambda n: (rev(n), 0)), pl.BlockSpec((GC, 128), lambda n: (rev(n), 0)),
                  vec, vec, pl.BlockSpec((1, GH, GD, GD), lambda n: (rev(n), 0, 0, 0)),
                  pl.BlockSpec((1, GH, GC, GC), lambda n: (rev(n), 0, 0, 0)),
                  pl.BlockSpec((GC, GH * GD), lambda n: (rev(n), 0)),
                  ANY],
        out_specs=[pl.BlockSpec((GC, W3), lambda n: (rev(n), 0)), pl.BlockSpec((GC, 128), lambda n: (rev(n), COL_BA)), vec, vec],
        out_shape=[jax.ShapeDtypeStruct((T, W3), F32), jax.ShapeDtypeStruct(dproj.shape, dproj.dtype),
                   jax.ShapeDtypeStruct((1, 128), F32), jax.ShapeDtypeStruct((1, 128), F32)],
        scratch_shapes=[pltpu.VMEM((GH, GD, GD), F32)], aliases={7: 1},
        args=(qkv_c, ba, alog, dtb, s_all, t_all, do, dproj))


def _gdn_post(o, z, nw):
    return o * lax.rsqrt(jnp.mean(o * o, axis=-1, keepdims=True) + EPS) * nw * _silu(z)


def _gdn_post_fwd(o_raw, proj, nw, mixed, *, name, tm=512):
    def body(o_ref, z_ref, w_ref, _, out_ref):
        out_ref[...] = _gdn_post(o_ref[...], z_ref[...].astype(F32), w_ref[...]).astype(BF16)

    return pl.pallas_call(
        body, name=name, grid=(T // tm, GH),
        in_specs=[pl.BlockSpec((tm, GD), lambda i, h: (i, h)), pl.BlockSpec((tm, GD), lambda i, h: (i, COL_Z + h)),
                  pl.BlockSpec((1, GD), lambda i, h: (0, 0)), ANY],
        out_specs=pl.BlockSpec((tm, GD), lambda i, h: (i, HQ * DH // GD + h)),
        out_shape=jax.ShapeDtypeStruct(mixed.shape, mixed.dtype), input_output_aliases={3: 0},
        compiler_params=_cp("parallel", "parallel"),
    )(o_raw, proj, nw, mixed)


def _gdn_post_bwd(o_raw, proj, nw, dmixed, *, name, tm=512):
    def body(o_ref, z_ref, w_ref, d_ref, do_ref, dz_ref, dw_ref):
        _, vjp = jax.vjp(_gdn_post, o_ref[...], z_ref[...].astype(F32), w_ref[...])
        do, dz, dw = vjp(d_ref[...])
        do_ref[...] = do
        dz_ref[...] = dz.astype(BF16)

        @pl.when((pl.program_id(0) == 0) & (pl.program_id(1) == 0))
        def _():
            dw_ref[...] = jnp.zeros_like(dw_ref)

        dw_ref[...] += dw

    blk = pl.BlockSpec((tm, GD), lambda i, h: (i, h))
    vec = pl.BlockSpec((1, GD), lambda i, h: (0, 0))
    return pl.pallas_call(
        body, name=name, grid=(T // tm, GH),
        in_specs=[blk, pl.BlockSpec((tm, GD), lambda i, h: (i, COL_Z + h)), vec,
                  pl.BlockSpec((tm, GD), lambda i, h: (i, GH + h))],
        out_specs=[blk, pl.BlockSpec((tm, GD), lambda i, h: (i, COL_Z + h)), vec],
        out_shape=[jax.ShapeDtypeStruct((T, GH * GD), F32), jax.ShapeDtypeStruct((T, N_PROJ), BF16),
                   jax.ShapeDtypeStruct((1, GD), F32)],
        compiler_params=_cp("arbitrary", "arbitrary"),
    )(o_raw, proj, nw, dmixed)


def _adamw(w, g, m, v, *, name):
    shape = w.shape
    cols = shape[-1]
    w2, g2, m2, v2 = (a.reshape(-1, cols) for a in (w, g, m, v))
    rows = w2.shape[0]
    tr = next((t for t in (512, 256, 128, 64, 32, 16, 8) if rows % t == 0), rows)

    def body(w_ref, g_ref, m_ref, v_ref, d_ref, nm_ref, nv_ref):
        gv = g_ref[...]
        nm = B1 * m_ref[...] + (1.0 - B1) * gv
        nv = B2 * v_ref[...] + (1.0 - B2) * jnp.square(gv)
        m_hat = nm / (1.0 - B1 ** STEP)
        v_hat = nv / (1.0 - B2 ** STEP)
        d_ref[...] = -LR * (m_hat / (jnp.sqrt(v_hat) + AEPS) + WD * w_ref[...])
        nm_ref[...] = nm
        nv_ref[...] = nv

    blk = pl.BlockSpec((tr, cols), lambda i: (i, 0))
    out = pl.pallas_call(
        body, name=name, grid=(rows // tr,), in_specs=[blk] * 4, out_specs=[blk] * 3,
        out_shape=[jax.ShapeDtypeStruct((rows, cols), F32)] * 3, compiler_params=_cp("parallel"),
    )(w2, g2, m2, v2)
    return tuple(o.reshape(shape) for o in out)


def _layer_weights(w_in):
    return jnp.pad(w_in, ((0, 0), (0, N_PROJ - w_in.shape[1])))


def _layer_params(attn_norm, sinks, gcw, a_log, dt_bias, gnw, ffn_norm, fcw, fcb):
    lanes4 = lambda v: jnp.pad(v, (GH, 128 - 2 * GH))[None]
    return dict(attn_norm=attn_norm[None], sinks=jnp.pad(sinks, (0, 128 - HQ))[None], gcw=gcw, alog=lanes4(a_log),
                dtb=lanes4(dt_bias), gnw=gnw[None], ffn_norm=ffn_norm[None], fcw=fcw, fcb=fcb[None])


def _mixer_fwd(x, W, P, l, carry=None):
    n = lambda s: f"l{l}_{s}"
    h = _rmsnorm_fwd(x, P["attn_norm"], name=n("norm1"))
    proj = _mm(h, W["all"], name=n("proj"), mode="nn", dims=(T, N_MAIN, D), tm=512, tn=N_MAIN, tk=D, out_dtype=BF16)
    ba = _mm(h, W["all"], name=n("proj_ba"), mode="nn", dims=(T, 128, D), tm=1024, tn=128, tk=D,
             b_spec=pl.BlockSpec((D, 128), lambda i, j, k: (k, COL_BA)))
    mixed = _swa_fwd(proj, P["sinks"], name=n("swa"))
    qkv_c = _gdn_prep_fwd(proj, P["gcw"], name=n("gdn_prep"))
    (o_raw, s_all, t_all), carried = _gdn_chunk_fwd(qkv_c, ba, P["alog"], P["dtb"], name=n("gdn_chunk"), carry=carry)
    mixed = _gdn_post_fwd(o_raw, proj, P["gnw"], mixed, name=n("gdn_post"))
    x1 = _mm(mixed, W["out"], res=x, name=n("out_proj"), mode="nn", dims=(T, D, D), tm=512, tn=D, tk=D)
    saved = dict(x=x, h=h, proj=proj, ba=ba, qkv_c=qkv_c, o_raw=o_raw, s_all=s_all, t_all=t_all, mixed=mixed, x1=x1)
    return x1, saved, carried


def _ffn_fwd(x1, W, P, l, carry=None):
    n = lambda s: f"l{l}_{s}"
    h2 = _rmsnorm_fwd(x1, P["ffn_norm"], name=n("norm2"))
    gu = _mm(h2, W["ffn"], name=n("ffn_in"), mode="nn", dims=(T, 2 * DFF, D), tm=512, tn=FFN_CW, tk=D, out_dtype=BF16,
             b_spec=pl.BlockSpec((None, D, FFN_CW), lambda i, j, k: (j, k, 0)), n_outer=True)
    act, carried = _ffn_act_fwd(gu, P["fcw"], P["fcb"], name=n("ffn_act"), carry=carry)
    x2 = _mm(act, W["down"], res=x1, name=n("ffn_down"), mode="nn", dims=(T, D, DFF), tm=512, tn=D, tk=DFF)
    return x2, dict(h2=h2, gu=gu, act=act), carried


def _layer_bwd(dx2, sv, W, P, l, carry=None):
    n = lambda s: f"l{l}_{s}"
    CW = FFN_CW
    dact = _mm(dx2, W["down"], name=n("d_act"), mode="nt", dims=(T, DFF, D), tm=512, tn=DFF // 2, tk=D, out_dtype=BF16,
               n_outer=True)
    g_down = _mm(sv["act"], dx2, name=n("g_down"), mode="tn", dims=(DFF, D, T), tm=DFF // 2, tn=D, tk=1024, out_dtype=BF16)
    dgu, g_fcw, g_fcb = _ffn_act_bwd(sv["gu"], P["fcw"], P["fcb"], dact, name=n("d_ffn_act"))
    dh2 = _mm(dgu, W["ffn"], name=n("d_h2"), mode="nt", dims=(T, D, 2 * DFF), tm=1024, tn=D, tk=CW,
              a_spec=pl.BlockSpec((None, 1024, CW), lambda i, j, k: (k // 2, i, k % 2)),
              b_spec=pl.BlockSpec((None, D, CW), lambda i, j, k: (k, j, 0)))
    g_ffn = _mm(sv["h2"], dgu, name=n("g_ffn"), mode="tn", dims=(D, 2 * DFF, T), tm=D, tn=CW, tk=1024,
                b_spec=pl.BlockSpec((None, 1024, CW), lambda i, j, k: (j // 2, k, j % 2)),
                out_spec=pl.BlockSpec((None, D, CW), lambda i, j, k: (j, i, 0)),
                out_shape=jax.ShapeDtypeStruct((N_CHIP, D, CW), BF16), out_dtype=BF16)
    dx1, g_ffn_norm = _rmsnorm_bwd(sv["x1"], P["ffn_norm"], dh2, dx2, name=n("d_norm2"))
    dmixed = _mm(dx1, W["out"], name=n("d_mixed"), mode="nt", dims=(T, D, D), tm=512, tn=D, tk=D)
    g_out = _mm(sv["mixed"], dx1, name=n("g_out"), mode="tn", dims=(D, D, T), tm=D, tn=D, tk=1024, out_dtype=BF16)
    do_raw, dproj, g_gnw = _gdn_post_bwd(sv["o_raw"], sv["proj"], P["gnw"], dmixed, name=n("d_gdn_post"))
    (dqkv_c, dproj, g_alog, g_dtb), carried = _gdn_chunk_bwd(
        sv["qkv_c"], sv["ba"], P["alog"], P["dtb"], sv["s_all"], sv["t_all"], do_raw, dproj, name=n("d_gdn_chunk"), carry=carry)
    dproj, g_gcw = _gdn_prep_bwd(sv["proj"], P["gcw"], dqkv_c, dproj, name=n("d_gdn_prep"))
    dproj, g_sinks = _swa_bwd(sv["proj"], P["sinks"], dmixed, dproj, name=n("d_swa"))
    dh = _mm(dproj, W["all"], name=n("d_h"), mode="nt", dims=(T, D, N_PROJ), tm=512, tn=D, tk=N_PROJ)
    g_all = _mm(sv["h"], dproj, name=n("g_in"), mode="tn", dims=(D, N_PROJ, T), tm=D, tn=N_PROJ, tk=1024, out_dtype=BF16)
    dx, g_attn_norm = _rmsnorm_bwd(sv["x"], P["attn_norm"], dh, dx1, name=n("d_norm1"))
    grads = dict(
        attn_norm=g_attn_norm[0], w_in=g_all, attn_sinks=g_sinks[0, :HQ], gdn_conv_w=g_gcw,
        gdn_a_log=g_alog[0, GH:2 * GH], gdn_dt_bias=g_dtb[0, GH:2 * GH], gdn_norm=g_gnw[0], w_out=g_out,
        ffn_norm=g_ffn_norm[0], w_ffn_in=g_ffn, ffn_conv_w=g_fcw, ffn_conv_b=g_fcb[0], w_down=g_down)
    return dx, grads, carried


def _pos():
    return lax.axis_index("x"), lax.axis_index("y"), lax.axis_index("c")


def _other_chips(x, y):
    return [(1 - x, y), (x, 1 - y), (1 - x, 1 - y)]


def _remote(src, dst, send_sems, recv_sems, k, to):
    return pltpu.make_async_remote_copy(src_ref=src, dst_ref=dst, send_sem=send_sems.at[k], recv_sem=recv_sems.at[k],
                                        device_id=to, device_id_type=MESH)


def _run_carry(carry, *, name):
    ci, co = len(carry.arrays), len(carry.out_shapes)

    def body(*refs):
        sends, recvs = carry.plan(refs[:ci], refs[ci:ci + co], refs[-2], refs[-1])
        for cp in sends:
            cp.start()
        for cp in recvs:
            cp.wait_recv()
        for cp in sends:
            cp.wait_send()

    return list(pl.pallas_call(
        body, name=name, in_specs=[ANY] * ci, out_specs=[ANY] * co, out_shape=carry.out_shapes,
        scratch_shapes=[pltpu.SemaphoreType.DMA((carry.nsem,)), pltpu.SemaphoreType.DMA((carry.nsem,))],
    )(*carry.arrays))


def _chip_index():
    return 2 * lax.axis_index("x") + lax.axis_index("y")


def _gather_carry(shards):
    def plan(srcs, outs, send_sems, recv_sems):
        x, y, c = _pos()
        chip = 2 * x + y
        others = [(k, px, py, 2 * px + py) for k, (px, py) in enumerate(_other_chips(x, y))]
        sends = [_remote(srcs[t], outs[t].at[chip], send_sems, recv_sems, 3 * t + k, (px, py, c))
                 for t in range(len(srcs)) for k, px, py, _ in others]
        recvs = [_remote(srcs[t], outs[t].at[j], send_sems, recv_sems, 3 * t + k, (x, y, c))
                 for t in range(len(srcs)) for k, _, _, j in others]
        return sends, recvs

    return _Carry(shards, [jax.ShapeDtypeStruct((N_CHIP,) + s.shape, s.dtype) for s in shards], 3 * len(shards), plan)


def _fill_own(outs, shards):
    return [lax.dynamic_update_index_in_dim(o, s, _chip_index(), 0) for o, s in zip(outs, shards)]


HBM_SPEC = pl.BlockSpec(memory_space=pltpu.HBM)
SEM_SPEC = pl.BlockSpec(memory_space=pltpu.SEMAPHORE)
DATAFLOW = pltpu.SideEffectType.DATAFLOW_SIDE_EFFECTING


def _gather_start(shards, *, name):
    nt, plan = len(shards), _gather_carry(shards).plan

    def body(*refs):
        srcs, lands, send_sems, recv_sems, token = refs[:nt], refs[nt:2 * nt], refs[2 * nt], refs[2 * nt + 1], refs[-1]
        for cp in plan(srcs, lands, send_sems, recv_sems)[0]:
            cp.start()
        token[...] = jnp.zeros_like(token)

    lands = [lax.empty((N_CHIP,) + s.shape, s.dtype) for s in shards]
    hbm = lambda a: pltpu.with_memory_space_constraint(a, pltpu.HBM)
    out = pl.pallas_call(
        body, name=name, in_specs=[HBM_SPEC] * (2 * nt),
        out_specs=[SEM_SPEC, SEM_SPEC] + [HBM_SPEC] * (2 * nt) + [VMEM_SPEC],
        out_shape=[pltpu.SemaphoreType.DMA((3 * nt,)), pltpu.SemaphoreType.DMA((3 * nt,))]
        + [pltpu.HBM(a.shape, a.dtype) for a in list(shards) + lands] + [jax.ShapeDtypeStruct((8, 128), F32)],
        input_output_aliases={t: 2 + t for t in range(2 * nt)},
        compiler_params=pltpu.CompilerParams(has_side_effects=DATAFLOW),
    )(*[hbm(a) for a in list(shards) + lands])
    return out[0], out[1], list(out[2:2 + nt]), list(out[2 + nt:2 + 2 * nt]), out[-1]


def _gather_wait(send_sems, recv_sems, shards, lands, after, *, name):
    nt, plan = len(shards), _gather_carry(shards).plan

    def body(*refs):
        srcs, lands_in, send, recv = refs[:nt], refs[nt:2 * nt], refs[2 * nt], refs[2 * nt + 1]
        sends, recvs = plan(srcs, lands_in, send, recv)
        for cp in sends:
            cp.wait_send()
        for cp in recvs:
            cp.wait_recv()

    out = pl.pallas_call(
        body, name=name, in_specs=[HBM_SPEC] * (2 * nt) + [SEM_SPEC, SEM_SPEC, ANY], out_specs=[HBM_SPEC] * (2 * nt),
        out_shape=[pltpu.HBM(a.shape, a.dtype) for a in list(shards) + list(lands)],
        input_output_aliases={t: t for t in range(2 * nt)},
        compiler_params=pltpu.CompilerParams(has_side_effects=DATAFLOW),
    )(*shards, *lands, send_sems, recv_sems, after)
    return _fill_own(out[nt:], out[:nt])


def _ag_small(v, *, name):
    m, n = v.shape

    def body(x_ref, out_ref, red_ref, send_sems, recv_sems, local_sem):
        x, y, c = _pos()
        me, sibling = (x, y, c), (x, y, 1 - c)
        chips = _other_chips(x, y)
        rows = lambda px, py, pc: out_ref.at[pl.ds(pl.multiple_of((4 * px + 2 * py + pc) * m, 8), m), :]
        mine = pltpu.make_async_copy(x_ref, rows(*me), local_sem)
        mine.start()
        first = [_remote(x_ref, rows(*me), send_sems, recv_sems, 0, sibling)]
        first += [_remote(x_ref, rows(*me), send_sems, recv_sems, 1 + k, (*chip, c)) for k, chip in enumerate(chips)]
        for cp in first:
            cp.start()
        passed = [_remote(rows(*chip, c), rows(*chip, c), send_sems, recv_sems, 4 + k, sibling) for k, chip in enumerate(chips)]
        for k, chip in enumerate(chips):
            _remote(rows(*chip, c), rows(*chip, c), send_sems, recv_sems, 1 + k, me).wait_recv()
            passed[k].start()
        _remote(rows(*sibling), rows(*sibling), send_sems, recv_sems, 0, me).wait_recv()
        for k, chip in enumerate(chips):
            _remote(rows(*chip, 1 - c), rows(*chip, 1 - c), send_sems, recv_sems, 4 + k, me).wait_recv()
        for cp in first + passed:
            cp.wait_send()
        mine.wait()
        acc = out_ref[0:m, :]
        for d in range(1, N_DEV):
            acc = acc + out_ref[d * m:(d + 1) * m, :]
        red_ref[...] = acc

    return pl.pallas_call(
        body, name=name, in_specs=[VMEM_SPEC], out_specs=[VMEM_SPEC, VMEM_SPEC],
        out_shape=[jax.ShapeDtypeStruct((N_DEV * m, n), v.dtype), jax.ShapeDtypeStruct((m, n), v.dtype)],
        scratch_shapes=[pltpu.SemaphoreType.DMA((7,)), pltpu.SemaphoreType.DMA((7,)), pltpu.SemaphoreType.DMA],
    )(v)


def _halves(ref, c):
    rh = ref.shape[1] // 2
    return ref.at[:, pl.ds(pl.multiple_of(c * rh, 16), rh), :]


def _rs_swap(gs, *, name):
    nt = len(gs)

    def body(*refs):
        g, theirs = refs[:nt], refs[nt:2 * nt]
        send_sems, recv_sems = refs[2 * nt:]
        x, y, c = _pos()
        swaps = [_remote(_halves(g[t], 1 - c), theirs[t], send_sems, recv_sems, t, (x, y, 1 - c)) for t in range(nt)]
        for cp in swaps:
            cp.start()
        for cp in swaps:
            cp.wait()

    return pl.pallas_call(
        body, name=name, in_specs=[ANY] * nt, out_specs=[ANY] * nt,
        out_shape=[jax.ShapeDtypeStruct((a.shape[0], a.shape[1] // 2, a.shape[2]), a.dtype) for a in gs],
        scratch_shapes=[pltpu.SemaphoreType.DMA((nt,)), pltpu.SemaphoreType.DMA((nt,))],
    )(*gs)


def _exchange_carry(ss):
    def plan(s, out, send_sems, recv_sems):
        x, y, c = _pos()
        chip = 2 * x + y
        others = [(k, px, py, 2 * px + py) for k, (px, py) in enumerate(_other_chips(x, y))]
        sends = [_remote(s[t].at[j], out[t].at[chip], send_sems, recv_sems, 3 * t + k, (px, py, c))
                 for t in range(len(s)) for k, px, py, j in others]
        recvs = [_remote(s[t].at[j], out[t].at[j], send_sems, recv_sems, 3 * t + k, (x, y, c))
                 for t in range(len(s)) for k, _, _, j in others]
        return sends, recvs

    return _Carry(ss, [jax.ShapeDtypeStruct(a.shape, a.dtype) for a in ss], 3 * len(ss), plan)


def _fill_own_slab(outs, ss):
    chip = _chip_index()
    return [lax.dynamic_update_index_in_dim(o, lax.dynamic_index_in_dim(s, chip, 0, keepdims=False), chip, 0)
            for o, s in zip(outs, ss)]


def _rs_join(rs, *, name):
    nt = len(rs)

    def body(*refs):
        r, theirs = refs[:nt], refs[nt:2 * nt]
        send_sems, recv_sems = refs[2 * nt:]
        x, y, c = _pos()
        swaps = [_remote(r[t], theirs[t], send_sems, recv_sems, t, (x, y, 1 - c)) for t in range(nt)]
        for cp in swaps:
            cp.start()
        for cp in swaps:
            cp.wait()

    theirs = pl.pallas_call(
        body, name=name, in_specs=[ANY] * nt, out_specs=[ANY] * nt,
        out_shape=[jax.ShapeDtypeStruct(a.shape, a.dtype) for a in rs],
        scratch_shapes=[pltpu.SemaphoreType.DMA((nt,)), pltpu.SemaphoreType.DMA((nt,))],
    )(*rs)
    first = lax.axis_index("c") == 0
    return [jnp.concatenate([jnp.where(first, a, b), jnp.where(first, b, a)], axis=0) for a, b in zip(rs, theirs)]


def _row_tile(rows, dtype):
    unit = 16 if dtype == BF16 else 8
    if rows <= 512:
        return rows
    return next((t for t in (512, 256, 128, 64, 32, 16, 8) if rows % t == 0 and t % unit == 0), rows)


def _add_pair(a, b, *, name):
    n, rh, cols = a.shape
    tr = _row_tile(rh, BF16)

    def body(a_ref, b_ref, o_ref):
        o_ref[...] = (a_ref[...].astype(F32) + b_ref[...].astype(F32)).astype(BF16)

    blk = pl.BlockSpec((None, tr, cols), lambda j, i: (j, i, 0))
    return pl.pallas_call(
        body, name=name, grid=(n, rh // tr), in_specs=[blk, blk], out_specs=blk,
        out_shape=jax.ShapeDtypeStruct(a.shape, BF16), compiler_params=_cp("parallel", "parallel"),
    )(a, b)


def _sum_chips(b, *, name):
    n, rh, cols = b.shape
    tr = _row_tile(rh, BF16)

    def body(b_ref, o_ref):
        acc = b_ref[0].astype(F32)
        for j in range(1, n):
            acc = acc + b_ref[j].astype(F32)
        o_ref[...] = acc

    return pl.pallas_call(
        body, name=name, grid=(rh // tr,), in_specs=[pl.BlockSpec((n, tr, cols), lambda i: (0, i, 0))],
        out_specs=pl.BlockSpec((tr, cols), lambda i: (i, 0)),
        out_shape=jax.ShapeDtypeStruct((rh, cols), F32), compiler_params=_cp("parallel"),
    )(b)


BIG = ("w_in", "w_out", "w_ffn_in", "w_down")
W_IN_SHARD = (N_MAIN + 2 * GH) // N_CHIP


def _rs_pairs(lg, l):
    g_in = jnp.stack([lg["w_in"][:, j * W_IN_SHARD:(j + 1) * W_IN_SHARD] for j in range(N_CHIP)])
    gs = [g_in, lg["w_out"].reshape(N_CHIP, D // N_CHIP, D), lg["w_ffn_in"], lg["w_down"].reshape(N_CHIP, DFF // N_CHIP, D)]
    c = lax.axis_index("c")
    mine = [lax.dynamic_slice_in_dim(a, c * (a.shape[1] // 2), a.shape[1] // 2, axis=1) for a in gs]
    theirs = _rs_swap(gs, name=f"rs{l}_swap")
    return [_add_pair(a, b, name=f"rs{l}_add_{n}") for a, b, n in zip(mine, theirs, BIG)]


def _rs_finish(got, pairs, l):
    got = _fill_own_slab(got, pairs)
    return _rs_join([_sum_chips(b, name=f"rs{l}_sum_{n}") for b, n in zip(got, BIG)], name=f"rs{l}_join")


def _size(shape):
    n = 1
    for s in shape:
        n *= s
    return n


def _pack_flat(parts, total, dtype):
    flat = jnp.concatenate([p.reshape(-1).astype(dtype) for p in parts])
    return jnp.pad(flat, (0, total - flat.shape[0]))


def _unpack_flat(flat, shapes):
    out, o = [], 0
    for s in shapes:
        out.append(flat[o:o + _size(s)].reshape(s))
        o += _size(s)
    return out


WEIGHTS = ("attn_norm", "w_in", "attn_sinks", "gdn_conv_w", "gdn_a_log", "gdn_dt_bias", "gdn_norm", "w_out", "ffn_norm",
           "w_ffn_in", "ffn_conv_w", "ffn_conv_b", "w_down", "final_norm")
SMALL = {"attn_norm": (DEPTH, D), "attn_sinks": (DEPTH, HQ), "gdn_a_log": (DEPTH, GH), "gdn_dt_bias": (DEPTH, GH),
         "gdn_norm": (DEPTH, GD), "ffn_norm": (DEPTH, D), "ffn_conv_b": (DEPTH, DFF), "final_norm": (D,)}
CONV_FULL = {"gdn_conv_w": (DEPTH, GK, 3 * GH * GD), "ffn_conv_w": (DEPTH, FK, DFF)}
CONV_SHARD = {"gdn_conv_w": (DEPTH, GK, 3 * GH * GD // N_CHIP), "ffn_conv_w": (DEPTH, FK, DFF // N_CHIP)}
CONV_ROWS, SMALLG_ROWS, SMALLW_ROWS = 64, 320, 144


def kernel(x, attn_norm, w_in, attn_sinks, gdn_conv_w, gdn_a_log, gdn_dt_bias, gdn_norm, w_out, ffn_norm, w_ffn_in, ffn_conv_w, ffn_conv_b, w_down, final_norm, loss_target, m_attn_norm, m_w_in, m_attn_sinks, m_gdn_conv_w, m_gdn_a_log, m_gdn_dt_bias, m_gdn_norm, m_w_out, m_ffn_norm, m_w_ffn_in, m_ffn_conv_w, m_ffn_conv_b, m_w_down, m_final_norm, v_attn_norm, v_w_in, v_attn_sinks, v_gdn_conv_w, v_gdn_a_log, v_gdn_dt_bias, v_gdn_norm, v_w_out, v_ffn_norm, v_w_ffn_in, v_ffn_conv_w, v_ffn_conv_b, v_w_down, v_final_norm):
    w = dict(zip(WEIGHTS, (attn_norm, w_in, attn_sinks, gdn_conv_w, gdn_a_log, gdn_dt_bias, gdn_norm, w_out, ffn_norm,
                           w_ffn_in, ffn_conv_w, ffn_conv_b, w_down, final_norm)))
    m = dict(zip(WEIGHTS, (m_attn_norm, m_w_in, m_attn_sinks, m_gdn_conv_w, m_gdn_a_log, m_gdn_dt_bias, m_gdn_norm, m_w_out,
                           m_ffn_norm, m_w_ffn_in, m_ffn_conv_w, m_ffn_conv_b, m_w_down, m_final_norm)))
    v = dict(zip(WEIGHTS, (v_attn_norm, v_w_in, v_attn_sinks, v_gdn_conv_w, v_gdn_a_log, v_gdn_dt_bias, v_gdn_norm, v_w_out,
                           v_ffn_norm, v_w_ffn_in, v_ffn_conv_w, v_ffn_conv_b, v_w_down, v_final_norm)))
    cx, cy, _ = _pos()
    chip = 2 * cx + cy

    cpack = _pack_flat([w[n] for n in CONV_SHARD], CONV_ROWS * 128, F32).reshape(CONV_ROWS, 128)
    cgath, _ = _ag_small(cpack, name="gather_conv_w")
    cgath = cgath.reshape(N_DEV, CONV_ROWS * 128)
    cshards = [_unpack_flat(cgath[2 * j], list(CONV_SHARD.values())) for j in range(N_CHIP)]
    conv = {n: jnp.concatenate([cshards[j][i] for j in range(N_CHIP)], axis=2) for i, n in enumerate(CONV_SHARD)}

    Ps = [_layer_params(attn_norm[l], attn_sinks[l], conv["gdn_conv_w"][l], gdn_a_log[l], gdn_dt_bias[l], gdn_norm[l],
                        ffn_norm[l], conv["ffn_conv_w"][l], ffn_conv_b[l]) for l in range(DEPTH)]

    wb = {n: [w[n][l].astype(BF16) for l in range(DEPTH)] for n in BIG}
    mixer_w = lambda g_in, g_out: dict(all=_layer_weights(jnp.concatenate(list(g_in), axis=1)), out=g_out.reshape(D, D))
    ffn_w = lambda g_ffn, g_down: dict(ffn=g_ffn, down=g_down.reshape(DFF, D))
    groups = [[wb[a][l], wb[b][l]] for l in range(DEPTH) for a, b in (("w_in", "w_out"), ("w_ffn_in", "w_down"))]
    started = [_gather_start(g, name=f"gather{i}_start") for i, g in enumerate(groups)]
    all_started = started[0][4] + started[1][4] + started[2][4] + started[3][4]
    Wm0 = mixer_w(*_gather_wait(*started[0][:4], all_started, name="gather0_wait"))
    h, sv0, _ = _mixer_fwd(x[0], Wm0, Ps[0], 0)
    Wf0 = ffn_w(*_gather_wait(*started[1][:4], h, name="gather1_wait"))
    h, sv0f, _ = _ffn_fwd(h, Wf0, Ps[0], 0)
    Wm1 = mixer_w(*_gather_wait(*started[2][:4], h, name="gather2_wait"))
    h, sv1, _ = _mixer_fwd(h, Wm1, Ps[1], 1)
    Wf1 = ffn_w(*_gather_wait(*started[3][:4], h, name="gather3_wait"))
    h, sv1f, _ = _ffn_fwd(h, Wf1, Ps[1], 1)
    dx, g_final, loss_part = _loss_head(h, final_norm[None], loss_target[0], name="loss_head")

    lg, reduced = [None] * DEPTH, [None] * DEPTH
    dx, lg[1], _ = _layer_bwd(dx, {**sv1, **sv1f}, {**Wm1, **Wf1}, Ps[1], 1)
    pairs1 = _rs_pairs(lg[1], 1)
    dx, lg[0], got1 = _layer_bwd(dx, {**sv0, **sv0f}, {**Wm0, **Wf0}, Ps[0], 0, carry=_exchange_carry(pairs1))
    reduced[1] = _rs_finish(got1, pairs1, 1)
    pairs0 = _rs_pairs(lg[0], 0)
    reduced[0] = _rs_finish(_run_carry(_exchange_carry(pairs0), name="rs0_exchange"), pairs0, 0)
    grad_x = dx[None]
    stacked = lambda n: jnp.stack([lg[l][n] for l in range(DEPTH)])

    small_parts = [g_final[0] if n == "final_norm" else stacked(n) for n in SMALL] + [stacked(n) for n in CONV_FULL]
    svec = _pack_flat(small_parts + [loss_part[0, :1]], SMALLG_ROWS * 128, F32).reshape(SMALLG_ROWS, 128)
    _, sred = _ag_small(svec, name="reduce_small")
    small_g = _unpack_flat(sred.reshape(-1), list(SMALL.values()) + list(CONV_FULL.values()) + [(1,)])
    g = dict(zip(list(SMALL) + list(CONV_FULL), small_g[:-1]))
    loss = small_g[-1][0]
    for n in CONV_FULL:
        wd = CONV_SHARD[n][2]
        g[n] = lax.dynamic_slice_in_dim(g[n], chip * wd, wd, axis=2)

    g.update({n: jnp.stack([reduced[l][i] for l in range(DEPTH)]) for i, n in enumerate(BIG)})

    delta, new_m, new_v = {}, {}, {}
    for n in BIG:
        delta[n], new_m[n], new_v[n] = _adamw(w[n], g[n], m[n], v[n], name=f"adamw_{n}")
    small_names = list(SMALL) + list(CONV_SHARD)
    small_shapes = list(SMALL.values()) + list(CONV_SHARD.values())
    packed = [_pack_flat([t[n] for n in small_names], SMALLW_ROWS * 128, F32).reshape(SMALLW_ROWS, 128) for t in (w, g, m, v)]
    for res, out in zip(_adamw(*packed, name="adamw_small"), (delta, new_m, new_v)):
        out.update(zip(small_names, _unpack_flat(res.reshape(-1), small_shapes)))

    return (loss, grad_x, *[g[n] for n in WEIGHTS], *[delta[n] for n in WEIGHTS], *[new_m[n] for n in WEIGHTS],
            *[new_v[n] for n in WEIGHTS])
```

```python
import functools

import jax
import jax.numpy as jnp
import numpy as np
from jax import lax
from jax.experimental import pallas as pl
from jax.experimental.pallas import tpu as pltpu

F32, BF16 = jnp.float32, jnp.bfloat16
HIGHEST = lax.Precision.HIGHEST
MESH = pl.DeviceIdType.MESH

D = 1024
T = 4096
DEPTH = 2
HQ, HKV, DH, WIN = 8, 2, 64, 128
GH, GD, GC, GK = 4, 128, 64, 4
DFF, FK = 2816, 3
EPS = 1e-6
N_MAIN = 2816
N_PROJ = N_MAIN + 128
COL_K, COL_V, COL_G, COL_Z, COL_BA = 4, 5, 6, 18, 22
N_CHUNK = T // GC
N_DEV, N_CHIP = 8, 4
FFN_CW = 2 * DFF // N_CHIP
LR, B1, B2, AEPS, WD, STEP = 0.001, 0.9, 0.999, 1e-08, 0.01, 10

VMEM_LIMIT = 56 * 1024 * 1024
ANY = pl.BlockSpec(memory_space=pl.ANY)
VMEM_SPEC = pl.BlockSpec(memory_space=pltpu.VMEM)


def _cp(*sem):
    return pltpu.CompilerParams(dimension_semantics=sem if sem else None, vmem_limit_bytes=VMEM_LIMIT)


class _Carry:
    def __init__(self, arrays, out_shapes, nsem, plan):
        self.arrays, self.out_shapes, self.nsem, self.plan = list(arrays), list(out_shapes), nsem, plan


def _seq_call(body, carry, *, name, steps, in_specs, out_specs, out_shape, args, scratch_shapes=(), aliases=None):
    in_specs, out_specs, out_shape, scratch = list(in_specs), list(out_specs), list(out_shape), list(scratch_shapes)
    n_in, n_out, n_scr = len(in_specs), len(out_specs), len(scratch)
    if carry is None:
        fn = body
    else:
        ci, co = len(carry.arrays), len(carry.out_shapes)

        def fn(*refs):
            ins, cins = refs[:n_in], refs[n_in:n_in + ci]
            outs, couts = refs[n_in + ci:n_in + ci + n_out], refs[n_in + ci + n_out:n_in + ci + n_out + co]
            scr, (ssem, rsem) = refs[n_in + ci + n_out + co:-2], refs[-2:]
            sends, recvs = carry.plan(cins, couts, ssem, rsem)

            @pl.when(pl.program_id(0) == 0)
            def _():
                for cp in sends:
                    cp.start()

            body(*ins, *outs, *scr)

            @pl.when(pl.program_id(0) == steps - 1)
            def _():
                for cp in recvs:
                    cp.wait_recv()
                for cp in sends:
                    cp.wait_send()

        in_specs += [ANY] * ci
        out_specs += [ANY] * co
        out_shape += carry.out_shapes
        scratch += [pltpu.SemaphoreType.DMA((carry.nsem,)), pltpu.SemaphoreType.DMA((carry.nsem,))]
        args = list(args) + carry.arrays
    outs = pl.pallas_call(
        fn, name=name, grid=(steps,), in_specs=in_specs, out_specs=out_specs, out_shape=out_shape,
        scratch_shapes=scratch, input_output_aliases=aliases or {}, compiler_params=_cp("arbitrary"),
    )(*args)
    return list(outs[:n_out]), list(outs[n_out:])


def _dot(a, b):
    return jnp.dot(a.astype(BF16), b.astype(BF16), preferred_element_type=F32)


def _dot_nt(a, b):
    return lax.dot_general(a.astype(BF16), b.astype(BF16), (((1,), (1,)), ((), ())), preferred_element_type=F32)


def _dot_tn(a, b):
    return lax.dot_general(a.astype(BF16), b.astype(BF16), (((0,), (0,)), ((), ())), preferred_element_type=F32)


def _iota(shape, dim):
    return lax.broadcasted_iota(jnp.int32, shape, dim)


def _col(x, idx):
    return jnp.sum(jnp.where(_iota(x.shape, 1) == idx, x, 0.0), axis=1, keepdims=True)


def _silu(y):
    return y * jax.nn.sigmoid(y)


_MM_DN = {"nn": (((1,), (0,)), ((), ())), "nt": (((1,), (1,)), ((), ())), "tn": (((0,), (0,)), ((), ()))}


def _mm(a, b, *, name, mode, dims, tm, tn, tk, res=None, out_dtype=F32, a_spec=None, b_spec=None, out_spec=None,
        out_shape=None, n_outer=False):
    M, N, K = dims
    assert M % tm == 0 and N % tn == 0 and K % tk == 0, (name, dims)
    nk = K // tk
    dn = _MM_DN[mode]

    def body(*refs):
        if res is None:
            a_ref, b_ref, o_ref, acc = refs
        else:
            a_ref, b_ref, r_ref, o_ref, acc = refs
        k = pl.program_id(2)

        @pl.when(k == 0)
        def _():
            acc[...] = jnp.zeros_like(acc)

        acc[...] += lax.dot_general(a_ref[...].astype(BF16), b_ref[...].astype(BF16), dn, preferred_element_type=F32)

        @pl.when(k == nk - 1)
        def _():
            r = acc[...]
            if res is not None:
                r = r + r_ref[...]
            o_ref[...] = r.astype(out_dtype)

    if a_spec is None:
        a_spec = pl.BlockSpec((tk, tm), lambda i, j, k: (k, i)) if mode == "tn" else pl.BlockSpec((tm, tk), lambda i, j, k: (i, k))
    if b_spec is None:
        b_spec = pl.BlockSpec((tn, tk), lambda i, j, k: (j, k)) if mode == "nt" else pl.BlockSpec((tk, tn), lambda i, j, k: (k, j))
    in_specs, args = [a_spec, b_spec], [a, b]
    if res is not None:
        in_specs.append(pl.BlockSpec((tm, tn), lambda i, j, k: (i, j)))
        args.append(res)
    out_spec = out_spec or pl.BlockSpec((tm, tn), lambda i, j, k: (i, j))
    grid = (M // tm, N // tn, nk)
    if n_outer:
        swap = lambda s: pl.BlockSpec(s.block_shape, lambda j, i, k, f=s.index_map: f(i, j, k))
        in_specs, out_spec, grid = [swap(s) for s in in_specs], swap(out_spec), (N // tn, M // tm, nk)
    return pl.pallas_call(
        body, name=name, grid=grid, in_specs=in_specs, out_specs=out_spec,
        out_shape=out_shape or jax.ShapeDtypeStruct((M, N), out_dtype),
        scratch_shapes=[pltpu.VMEM((tm, tn), F32)],
        compiler_params=_cp("parallel", "parallel", "arbitrary"),
    )(*args)


def _rms(x, w):
    return x * lax.rsqrt(jnp.mean(x * x, axis=-1, keepdims=True) + EPS) * w


def _rmsnorm_fwd(x, w, *, name, tm=512):
    def body(x_ref, w_ref, o_ref):
        o_ref[...] = _rms(x_ref[...], w_ref[...]).astype(BF16)

    return pl.pallas_call(
        body, name=name, grid=(x.shape[0] // tm,),
        in_specs=[pl.BlockSpec((tm, D), lambda i: (i, 0)), pl.BlockSpec((1, D), lambda i: (0, 0))],
        out_specs=pl.BlockSpec((tm, D), lambda i: (i, 0)),
        out_shape=jax.ShapeDtypeStruct(x.shape, BF16), compiler_params=_cp("parallel"),
    )(x, w)


def _rmsnorm_bwd(x, w, dh, dres, *, name, tm=512):
    def body(x_ref, w_ref, dh_ref, dr_ref, dx_ref, dw_ref):
        _, vjp = jax.vjp(_rms, x_ref[...], w_ref[...])
        dx, dw = vjp(dh_ref[...])
        dx_ref[...] = dx + dr_ref[...]

        @pl.when(pl.program_id(0) == 0)
        def _():
            dw_ref[...] = jnp.zeros_like(dw_ref)

        dw_ref[...] += dw

    row = pl.BlockSpec((tm, D), lambda i: (i, 0))
    vec = pl.BlockSpec((1, D), lambda i: (0, 0))
    return pl.pallas_call(
        body, name=name, grid=(x.shape[0] // tm,), in_specs=[row, vec, row, row], out_specs=[row, vec],
        out_shape=[jax.ShapeDtypeStruct(x.shape, F32), jax.ShapeDtypeStruct((1, D), F32)],
        compiler_params=_cp("arbitrary"),
    )(x, w, dh, dres)


def _loss_head(x, w, tgt, *, name, tm=512):
    def f(xv, wv, tv):
        err = _rms(xv, wv) - tv
        per_row = jnp.sum(err * err, axis=1, keepdims=True) * (0.5 / D)
        return jnp.sum(per_row, axis=0, keepdims=True)

    def body(x_ref, w_ref, t_ref, dx_ref, dw_ref, loss_ref):
        tv = t_ref[...]
        loss, vjp = jax.vjp(lambda xv, wv: f(xv, wv, tv), x_ref[...], w_ref[...])
        dx, dw = vjp(jnp.ones((1, 1), F32))
        dx_ref[...] = dx

        @pl.when(pl.program_id(0) == 0)
        def _():
            dw_ref[...] = jnp.zeros_like(dw_ref)
            loss_ref[...] = jnp.zeros_like(loss_ref)

        dw_ref[...] += dw
        loss_ref[...] += jnp.broadcast_to(loss, loss_ref.shape)

    row = pl.BlockSpec((tm, D), lambda i: (i, 0))
    vec = pl.BlockSpec((1, D), lambda i: (0, 0))
    return pl.pallas_call(
        body, name=name, grid=(x.shape[0] // tm,), in_specs=[row, vec, row],
        out_specs=[row, vec, pl.BlockSpec((1, 128), lambda i: (0, 0))],
        out_shape=[jax.ShapeDtypeStruct(x.shape, F32), jax.ShapeDtypeStruct((1, D), F32),
                   jax.ShapeDtypeStruct((1, 128), F32)],
        compiler_params=_cp("arbitrary"),
    )(x, w, tgt)


def _swa_bias():
    G = HQ // HKV
    r, c = np.arange(G * WIN)[:, None], np.arange(2 * WIN)[None, :]
    rel = (r % WIN) + WIN - c
    band = (rel >= 0) & (rel < WIN)
    out = np.empty((2, HKV, G * WIN, 2 * WIN), np.float32)
    for h2 in range(HKV):
        slope = 2.0 ** (-8.0 * (h2 * G + r // WIN + 1) / HQ)
        out[0, h2] = np.where(band & (c >= WIN), -slope * rel, -1e30)
        out[1, h2] = np.where(band, -slope * rel, -1e30)
    return jnp.asarray(out)


def _swa_group(qg, k2, v2, sk, bias, h2):
    s = _dot_nt(qg, k2) * (DH ** -0.5) + bias
    g1 = _iota((s.shape[0], 1), 0) >> 7
    sink = jnp.zeros((s.shape[0], 1), F32)
    for gi in range(HQ // HKV):
        sink = jnp.where(g1 == gi, _col(sk, h2 * (HQ // HKV) + gi), sink)
    m = lax.stop_gradient(jnp.maximum(jnp.max(s, axis=1, keepdims=True), sink))
    p = jnp.exp(s - m)
    den = jnp.sum(p, axis=1, keepdims=True) + jnp.exp(sink - m)
    return _dot(p * (1.0 / den), v2)


def _swa_split_q(q, h2):
    G = HQ // HKV
    return jnp.concatenate([q[:, (h2 * G + g) * DH:(h2 * G + g + 1) * DH] for g in range(G)], axis=0)


def _swa_merge_q(parts):
    G = HQ // HKV
    return jnp.concatenate([parts[h2][g * WIN:(g + 1) * WIN] for h2 in range(HKV) for g in range(G)], axis=1)


def _swa_specs():
    prev = lambda i: jnp.maximum(jnp.minimum(i, T // WIN - 1) - 1, 0)
    cur = lambda i: jnp.minimum(i, T // WIN - 1)
    return [
        pl.BlockSpec((WIN, HQ * DH), lambda i: (cur(i), 0)),
        pl.BlockSpec((WIN, 128), lambda i: (prev(i), COL_K)),
        pl.BlockSpec((WIN, 128), lambda i: (cur(i), COL_K)),
        pl.BlockSpec((WIN, 128), lambda i: (prev(i), COL_V)),
        pl.BlockSpec((WIN, 128), lambda i: (cur(i), COL_V)),
        pl.BlockSpec((1, 128), lambda i: (0, 0)),
        pl.BlockSpec((None, HKV, (HQ // HKV) * WIN, 2 * WIN), lambda i: (jnp.minimum(i, 1), 0, 0, 0)),
    ]


def _swa_fwd(proj, sinks, *, name):
    def body(q_ref, kp_ref, kc_ref, vp_ref, vc_ref, sk_ref, bias_ref, o_ref):
        q, sk = q_ref[...].astype(F32), sk_ref[...]
        outs = []
        for h2 in range(HKV):
            sl = slice(h2 * DH, (h2 + 1) * DH)
            k2 = jnp.concatenate([kp_ref[:, sl], kc_ref[:, sl]], axis=0).astype(F32)
            v2 = jnp.concatenate([vp_ref[:, sl], vc_ref[:, sl]], axis=0).astype(F32)
            outs.append(_swa_group(_swa_split_q(q, h2), k2, v2, sk, bias_ref[h2], h2))
        o_ref[...] = _swa_merge_q(outs).astype(BF16)

    return pl.pallas_call(
        body, name=name, grid=(T // WIN,), in_specs=_swa_specs(),
        out_specs=pl.BlockSpec((WIN, HQ * DH), lambda i: (i, 0)),
        out_shape=jax.ShapeDtypeStruct((T, HQ * DH + GH * GD), BF16), compiler_params=_cp("parallel"),
    )(proj, proj, proj, proj, proj, sinks, _swa_bias())


def _swa_bwd(proj, sinks, do, dproj, *, name):
    NB = T // WIN
    QW = HQ * DH

    def body(q_ref, kp_ref, kc_ref, vp_ref, vc_ref, sk_ref, bias_ref, do_ref, _, out_ref, dsk_ref, cq, ck, cv):
        i = pl.program_id(0)

        @pl.when(i == 0)
        def _():
            cq[...] = jnp.zeros_like(cq)
            ck[...] = jnp.zeros_like(ck)
            cv[...] = jnp.zeros_like(cv)
            dsk_ref[...] = jnp.zeros_like(dsk_ref)

        @pl.when(i < NB)
        def _():
            q, sk, dov = q_ref[...].astype(F32), sk_ref[...], do_ref[...].astype(F32)
            dqs, dkp, dkc, dvp, dvc = [], [], [], [], []
            dsk = jnp.zeros_like(sk)
            for h2 in range(HKV):
                sl = slice(h2 * DH, (h2 + 1) * DH)
                k2 = jnp.concatenate([kp_ref[:, sl], kc_ref[:, sl]], axis=0).astype(F32)
                v2 = jnp.concatenate([vp_ref[:, sl], vc_ref[:, sl]], axis=0).astype(F32)
                _, vjp = jax.vjp(functools.partial(_swa_group, bias=bias_ref[h2], h2=h2), _swa_split_q(q, h2), k2, v2, sk)
                dqg, dk2, dv2, dsk_h = vjp(_swa_split_q(dov, h2))
                dqs.append(dqg)
                dkp.append(dk2[:WIN]); dkc.append(dk2[WIN:])
                dvp.append(dv2[:WIN]); dvc.append(dv2[WIN:])
                dsk = dsk + dsk_h
            out_ref[:, :QW] = cq[...].astype(BF16)
            out_ref[:, QW:QW + 128] = (ck[...] + jnp.concatenate(dkp, axis=1)).astype(BF16)
            out_ref[:, QW + 128:] = (cv[...] + jnp.concatenate(dvp, axis=1)).astype(BF16)
            cq[...] = _swa_merge_q(dqs)
            ck[...] = jnp.concatenate(dkc, axis=1)
            cv[...] = jnp.concatenate(dvc, axis=1)
            dsk_ref[...] += dsk

        @pl.when(i == NB)
        def _():
            out_ref[:, :QW] = cq[...].astype(BF16)
            out_ref[:, QW:QW + 128] = ck[...].astype(BF16)
            out_ref[:, QW + 128:] = cv[...].astype(BF16)

    qblk = pl.BlockSpec((WIN, QW), lambda i: (jnp.minimum(i, NB - 1), 0))
    return pl.pallas_call(
        body, name=name, grid=(NB + 1,), in_specs=_swa_specs() + [qblk, ANY],
        out_specs=[pl.BlockSpec((WIN, QW + 256), lambda i: (jnp.maximum(i - 1, 0), 0)), pl.BlockSpec((1, 128), lambda i: (0, 0))],
        out_shape=[jax.ShapeDtypeStruct(dproj.shape, dproj.dtype), jax.ShapeDtypeStruct((1, 128), F32)],
        scratch_shapes=[pltpu.VMEM((WIN, QW), F32), pltpu.VMEM((WIN, 128), F32), pltpu.VMEM((WIN, 128), F32)],
        input_output_aliases={8: 0}, compiler_params=_cp("arbitrary"),
    )(proj, proj, proj, proj, proj, sinks, _swa_bias(), do, dproj)


RC = 256
HALO = 8


def _load_ext(ref, c):
    nch = T // RC
    r0 = pl.multiple_of(c * RC, RC)
    p0 = pl.multiple_of(jnp.maximum(r0 - HALO, 0), HALO)
    n0 = pl.multiple_of(jnp.minimum(r0 + RC, T - HALO), HALO)
    prev = jnp.where(c > 0, ref[pl.ds(p0, HALO), :].astype(F32), 0.0)
    nxt = jnp.where(c < nch - 1, ref[pl.ds(n0, HALO), :].astype(F32), 0.0)
    return jnp.concatenate([prev, ref[pl.ds(r0, RC), :].astype(F32), nxt], axis=0)


def _conv_ext(xe, w, K):
    y = w[K - 1:K, :] * xe
    for s in range(1, K):
        y = y + w[K - 1 - s:K - s, :] * pltpu.roll(xe, s, 0)
    return y


def _conv_bwd_ext(xe, dye, w, K, dw_ref):
    n = xe.shape[0]
    own = slice(HALO, HALO + RC)
    dx = w[K - 1:K, :] * dye
    dw_ref[K - 1:K, :] += jnp.sum(dye[own] * xe[own], axis=0, keepdims=True)
    for s in range(1, K):
        dx = dx + w[K - 1 - s:K - s, :] * pltpu.roll(dye, n - s, 0)
        dw_ref[K - 1 - s:K - s, :] += jnp.sum(dye[own] * pltpu.roll(xe, s, 0)[own], axis=0, keepdims=True)
    return dx[own]


def _gdn_post_conv(y, is_qk):
    a = _silu(y)
    nrm = a * lax.rsqrt(jnp.sum(a * a, axis=1, keepdims=True) + EPS)
    return jnp.where(is_qk, nrm, a)


def _gdn_prep_fwd(proj, conv_w, *, name):
    nblk = 3 * GH

    def body(x_ref, w_ref, o_ref):
        is_qk = pl.program_id(0) < 2 * GH
        w = w_ref[...]

        def chunk(c, carry):
            y = _conv_ext(_load_ext(x_ref, c), w, GK)[HALO:HALO + RC]
            o_ref[pl.ds(pl.multiple_of(c * RC, RC), RC), :] = _gdn_post_conv(y, is_qk)
            return carry

        lax.fori_loop(0, T // RC, chunk, 0)

    return pl.pallas_call(
        body, name=name, grid=(nblk,),
        in_specs=[pl.BlockSpec((T, 128), lambda j: (0, COL_G + j)), pl.BlockSpec((GK, 128), lambda j: (0, j))],
        out_specs=pl.BlockSpec((T, 128), lambda j: (0, j)),
        out_shape=jax.ShapeDtypeStruct((T, nblk * 128), F32), compiler_params=_cp("parallel"),
    )(proj, conv_w)


def _gdn_prep_bwd(proj, conv_w, dout, dproj, *, name):
    nblk = 3 * GH

    def body(x_ref, w_ref, d_ref, _, dx_ref, dw_ref):
        is_qk = pl.program_id(0) < 2 * GH
        w = w_ref[...]
        dw_ref[...] = jnp.zeros_like(dw_ref)

        def chunk(c, carry):
            xe = _load_ext(x_ref, c)
            _, vjp = jax.vjp(lambda y: _gdn_post_conv(y, is_qk), _conv_ext(xe, w, GK))
            (dye,) = vjp(_load_ext(d_ref, c))
            dx_ref[pl.ds(pl.multiple_of(c * RC, RC), RC), :] = _conv_bwd_ext(xe, dye, w, GK, dw_ref).astype(BF16)
            return carry

        lax.fori_loop(0, T // RC, chunk, 0)

    return pl.pallas_call(
        body, name=name, grid=(nblk,),
        in_specs=[pl.BlockSpec((T, 128), lambda j: (0, COL_G + j)), pl.BlockSpec((GK, 128), lambda j: (0, j)),
                  pl.BlockSpec((T, 128), lambda j: (0, j)), ANY],
        out_specs=[pl.BlockSpec((T, 128), lambda j: (0, COL_G + j)), pl.BlockSpec((GK, 128), lambda j: (0, j))],
        out_shape=[jax.ShapeDtypeStruct(dproj.shape, dproj.dtype), jax.ShapeDtypeStruct((GK, nblk * 128), F32)],
        input_output_aliases={3: 0}, compiler_params=_cp("parallel"),
    )(proj, conv_w, dout, dproj)


def _ffn_post_conv(y, b, up):
    return _silu(y + b) * up


def _ffn_act_fwd(gu, conv_w, conv_b, *, name, carry=None):
    nblk = DFF // 128

    def body(g_ref, u_ref, w_ref, b_ref, o_ref):
        w, b = w_ref[...], b_ref[...]

        def chunk(c, carry):
            rows = pl.ds(pl.multiple_of(c * RC, RC), RC)
            y = _conv_ext(_load_ext(g_ref, c), w, FK)[HALO:HALO + RC]
            o_ref[rows, :] = _ffn_post_conv(y, b, u_ref[rows, :].astype(F32)).astype(BF16)
            return carry

        lax.fori_loop(0, T // RC, chunk, 0)

    (act,), carried = _seq_call(
        body, carry, name=name, steps=nblk,
        in_specs=[pl.BlockSpec((T, 128), lambda j: (0, j)), pl.BlockSpec((T, 128), lambda j: (0, nblk + j)),
                  pl.BlockSpec((FK, 128), lambda j: (0, j)), pl.BlockSpec((1, 128), lambda j: (0, j))],
        out_specs=[pl.BlockSpec((T, 128), lambda j: (0, j))], out_shape=[jax.ShapeDtypeStruct((T, DFF), BF16)],
        args=(gu, gu, conv_w, conv_b))
    return act, carried


def _ffn_act_bwd(gu, conv_w, conv_b, dact, *, name):
    nblk = DFF // 128

    def body(g_ref, u_ref, w_ref, b_ref, d_ref, dgu_ref, dw_ref, db_ref):
        dg_ref, du_ref = dgu_ref.at[0], dgu_ref.at[1]
        w, b = w_ref[...], b_ref[...]
        dw_ref[...] = jnp.zeros_like(dw_ref)
        db_ref[...] = jnp.zeros_like(db_ref)

        def chunk(c, carry):
            rows = pl.ds(pl.multiple_of(c * RC, RC), RC)
            xe = _load_ext(g_ref, c)
            ue = _load_ext(u_ref, c)
            _, vjp = jax.vjp(_ffn_post_conv, _conv_ext(xe, w, FK), b, ue)
            dye, db, due = vjp(_load_ext(d_ref, c))
            du_ref[rows, :] = due[HALO:HALO + RC].astype(BF16)
            db_ref[...] += jnp.sum(dye[HALO:HALO + RC], axis=0, keepdims=True)
            dg_ref[rows, :] = _conv_bwd_ext(xe, dye, w, FK, dw_ref).astype(BF16)
            return carry

        lax.fori_loop(0, T // RC, chunk, 0)

    col = pl.BlockSpec((T, 128), lambda j: (0, j))
    return pl.pallas_call(
        body, name=name, grid=(nblk,),
        in_specs=[col, pl.BlockSpec((T, 128), lambda j: (0, nblk + j)), pl.BlockSpec((FK, 128), lambda j: (0, j)),
                  pl.BlockSpec((1, 128), lambda j: (0, j)), col],
        out_specs=[pl.BlockSpec((2, T, 128), lambda j: (0, 0, j)), pl.BlockSpec((FK, 128), lambda j: (0, j)),
                   pl.BlockSpec((1, 128), lambda j: (0, j))],
        out_shape=[jax.ShapeDtypeStruct((2, T, DFF), BF16), jax.ShapeDtypeStruct((FK, DFF), F32),
                   jax.ShapeDtypeStruct((1, DFF), F32)],
        compiler_params=_cp("parallel"),
    )(gu, gu, conv_w, conv_b, dact)


def _gdn_gates(ba, alog, dtb):
    beta = jax.nn.sigmoid(ba)
    g = -jnp.exp(alog) * jax.nn.softplus(ba + dtb)
    tril = (_iota((GC, GC), 0) >= _iota((GC, GC), 1)).astype(F32)
    return beta, jnp.dot(tril, g, precision=HIGHEST, preferred_element_type=F32)


def _hmap(f, *lists):
    return [f(*xs) for xs in zip(*lists)]


def _gdn_cols(beta_all, gc_all):
    return [_col(beta_all, h) for h in range(GH)], [_col(gc_all, GH + h) for h in range(GH)]


def _gdn_decay(Gcs):
    r, c = _iota((GC, GC), 0), _iota((GC, GC), 1)
    eye, ones = (r == c).astype(F32), jnp.ones((GC, GC), F32)
    grows = _hmap(lambda G: jnp.dot(ones, eye * G, precision=HIGHEST, preferred_element_type=F32), Gcs)
    return _hmap(lambda G, grow: jnp.exp(jnp.where(r >= c, G - grow, -1e30)), Gcs, grows)


def _gdn_A(ks, betas, Gcs):
    strict = _iota((GC, GC), 0) > _iota((GC, GC), 1)
    kk = _hmap(lambda k, b: _dot_nt(k * b, k), ks, betas)
    return _hmap(lambda a, d: jnp.where(strict, a * d, 0.0), kk, _gdn_decay(Gcs))


def _tri_inv(As):
    eye = (_iota((GC, GC), 0) == _iota((GC, GC), 1)).astype(F32)
    Tms, Ps = [eye - A for A in As], As
    for _ in range(GC.bit_length() - 2):
        Ps = _hmap(lambda P: _dot(P, P), Ps)
        Tms = _hmap(lambda Tm, P: Tm + _dot(Tm, P), Tms, Ps)
    return Tms


def _gdn_chunk(qs, ks, vs, betas, Gcs, Ss, Tms):
    decays = _gdn_decay(Gcs)
    eGs = _hmap(jnp.exp, Gcs)
    us = _hmap(lambda Tm, v, b: _dot(Tm, v * b), Tms, vs, betas)
    ws = _hmap(lambda Tm, k, b, eG: _dot(Tm, k * b * eG), Tms, ks, betas, eGs)
    qss = [q * (GD ** -0.5) for q in qs]
    qks = _hmap(lambda q, k, d: _dot_nt(q, k) * d, qss, ks, decays)
    glasts = [jnp.sum(jnp.where(_iota(G.shape, 0) == GC - 1, G, 0.0), axis=0, keepdims=True) for G in Gcs]
    kds = _hmap(lambda k, gl, G: k * jnp.exp(gl - G), ks, glasts, Gcs)
    v_news = _hmap(lambda u, w, S: u - _dot(w, S), us, ws, Ss)
    qS = _hmap(lambda q, eG, S: _dot(q * eG, S), qss, eGs, Ss)
    os = _hmap(lambda a, qk, vn: a + _dot(qk, vn), qS, qks, v_news)
    S_news = _hmap(lambda S, gl, kd, vn: S * jnp.exp(gl) + _dot_tn(kd, vn), Ss, glasts, kds, v_news)
    return os, S_news


def _gdn_chunk_fwd(qkv_c, ba, alog, dtb, *, name, carry=None):
    W3 = 3 * GH * GD

    def body(x_ref, ba_ref, al_ref, dt_ref, o_ref, s_ref, t_ref, S):
        @pl.when(pl.program_id(0) == 0)
        def _():
            S[...] = jnp.zeros_like(S)

        beta_all, gc_all = _gdn_gates(ba_ref[...], al_ref[...], dt_ref[...])
        qs, ks, vs = ([x_ref[:, (p * GH + h) * GD:(p * GH + h + 1) * GD] for h in range(GH)] for p in range(3))
        betas, Gcs = _gdn_cols(beta_all, gc_all)
        Tms = _tri_inv(_gdn_A(ks, betas, Gcs))
        Ss = [S[h] for h in range(GH)]
        os, S_news = _gdn_chunk(qs, ks, vs, betas, Gcs, Ss, Tms)
        for h in range(GH):
            s_ref[0, h] = Ss[h]
            t_ref[0, h] = Tms[h]
            o_ref[:, h * GD:(h + 1) * GD] = os[h]
            S[h] = S_news[h]

    vec = pl.BlockSpec((1, 128), lambda n: (0, 0))
    return _seq_call(
        body, carry, name=name, steps=N_CHUNK,
        in_specs=[pl.BlockSpec((GC, W3), lambda n: (n, 0)), pl.BlockSpec((GC, 128), lambda n: (n, 0)), vec, vec],
        out_specs=[pl.BlockSpec((GC, GH * GD), lambda n: (n, 0)),
                   pl.BlockSpec((1, GH, GD, GD), lambda n: (n, 0, 0, 0)),
                   pl.BlockSpec((1, GH, GC, GC), lambda n: (n, 0, 0, 0))],
        out_shape=[jax.ShapeDtypeStruct((T, GH * GD), F32), jax.ShapeDtypeStruct((N_CHUNK, GH, GD, GD), F32),
                   jax.ShapeDtypeStruct((N_CHUNK, GH, GC, GC), F32)],
        scratch_shapes=[pltpu.VMEM((GH, GD, GD), F32)], args=(qkv_c, ba, alog, dtb))


def _gdn_chunk_bwd(qkv_c, ba, alog, dtb, s_all, t_all, do, dproj, *, name, carry=None):
    W3 = 3 * GH * GD
    rev = lambda n: N_CHUNK - 1 - n

    def body(x_ref, ba_ref, al_ref, dt_ref, s_ref, t_ref, do_ref, _, dx_ref, dba_ref, dal_ref, ddt_ref, dS):
        @pl.when(pl.program_id(0) == 0)
        def _():
            dS[...] = jnp.zeros_like(dS)
            dal_ref[...] = jnp.zeros_like(dal_ref)
            ddt_ref[...] = jnp.zeros_like(ddt_ref)

        (beta_all, gc_all), vjp_gates = jax.vjp(_gdn_gates, ba_ref[...], al_ref[...], dt_ref[...])
        qs, ks, vs = ([x_ref[:, (p * GH + h) * GD:(p * GH + h + 1) * GD] for h in range(GH)] for p in range(3))
        Tms = [t_ref[0, h] for h in range(GH)]
        Ss = [s_ref[0, h] for h in range(GH)]
        dos = [do_ref[:, h * GD:(h + 1) * GD] for h in range(GH)]
        dSs = [dS[h] for h in range(GH)]

        def f(qs, ks, vs, b_all, g_all, Ss, Tms):
            return _gdn_chunk(qs, ks, vs, *_gdn_cols(b_all, g_all), Ss, Tms)

        def fa(ks, b_all, g_all):
            return _gdn_A(ks, *_gdn_cols(b_all, g_all))

        _, vjp = jax.vjp(f, qs, ks, vs, beta_all, gc_all, Ss, Tms)
        dqs, dks, dvs, db1, dg1, dS_prev, dTs = vjp((dos, dSs))
        dXs = _hmap(_dot_nt, dTs, Tms)
        dAs = _hmap(lambda Tm, dX: -_dot_tn(Tm, dX), Tms, dXs)
        _, vjp_a = jax.vjp(fa, ks, beta_all, gc_all)
        dks2, db2, dg2 = vjp_a(dAs)
        for h in range(GH):
            dS[h] = dS_prev[h]
            dx_ref[:, h * GD:(h + 1) * GD] = dqs[h]
            dx_ref[:, (GH + h) * GD:(GH + h + 1) * GD] = dks[h] + dks2[h]
            dx_ref[:, (2 * GH + h) * GD:(2 * GH + h + 1) * GD] = dvs[h]
        dba, dal, ddt = vjp_gates((db1 + db2, dg1 + dg2))
        dba_ref[...] = dba.astype(BF16)
        dal_ref[...] += dal
        ddt_ref[...] += ddt

    vec = pl.BlockSpec((1, 128), lambda n: (0, 0))
    return _seq_call(
        body, carry, name=name, steps=N_CHUNK,
        in_specs=[pl.BlockSpec((GC, W3), lambda n: (rev(n), 0)), pl.BlockSpec((GC, 128), lambda n: (rev(n), 0)),
                  vec, vec, pl.BlockSpec((1, GH, GD, GD), lambda n: (rev(n), 0, 0, 0)),
                  pl.BlockSpec((1, GH, GC, GC), lambda n: (rev(n), 0, 0, 0)),
                  pl.BlockSpec((GC, GH * GD), lambda n: (rev(n), 0)),
                  ANY],
        out_specs=[pl.BlockSpec((GC, W3), lambda n: (rev(n), 0)), pl.BlockSpec((GC, 128), lambda n: (rev(n), COL_BA)), vec, vec],
        out_shape=[jax.ShapeDtypeStruct((T, W3), F32), jax.ShapeDtypeStruct(dproj.shape, dproj.dtype),
                   jax.ShapeDtypeStruct((1, 128), F32), jax.ShapeDtypeStruct((1, 128), F32)],
        scratch_shapes=[pltpu.VMEM((GH, GD, GD), F32)], aliases={7: 1},
        args=(qkv_c, ba, alog, dtb, s_all, t_all, do, dproj))


def _gdn_post(o, z, nw):
    return o * lax.rsqrt(jnp.mean(o * o, axis=-1, keepdims=True) + EPS) * nw * _silu(z)


def _gdn_post_fwd(o_raw, proj, nw, mixed, *, name, tm=512):
    def body(o_ref, z_ref, w_ref, _, out_ref):
        out_ref[...] = _gdn_post(o_ref[...], z_ref[...].astype(F32), w_ref[...]).astype(BF16)

    return pl.pallas_call(
        body, name=name, grid=(T // tm, GH),
        in_specs=[pl.BlockSpec((tm, GD), lambda i, h: (i, h)), pl.BlockSpec((tm, GD), lambda i, h: (i, COL_Z + h)),
                  pl.BlockSpec((1, GD), lambda i, h: (0, 0)), ANY],
        out_specs=pl.BlockSpec((tm, GD), lambda i, h: (i, HQ * DH // GD + h)),
        out_shape=jax.ShapeDtypeStruct(mixed.shape, mixed.dtype), input_output_aliases={3: 0},
        compiler_params=_cp("parallel", "parallel"),
    )(o_raw, proj, nw, mixed)


def _gdn_post_bwd(o_raw, proj, nw, dmixed, *, name, tm=512):
    def body(o_ref, z_ref, w_ref, d_ref, do_ref, dz_ref, dw_ref):
        _, vjp = jax.vjp(_gdn_post, o_ref[...], z_ref[...].astype(F32), w_ref[...])
        do, dz, dw = vjp(d_ref[...])
        do_ref[...] = do
        dz_ref[...] = dz.astype(BF16)

        @pl.when((pl.program_id(0) == 0) & (pl.program_id(1) == 0))
        def _():
            dw_ref[...] = jnp.zeros_like(dw_ref)

        dw_ref[...] += dw

    blk = pl.BlockSpec((tm, GD), lambda i, h: (i, h))
    vec = pl.BlockSpec((1, GD), lambda i, h: (0, 0))
    return pl.pallas_call(
        body, name=name, grid=(T // tm, GH),
        in_specs=[blk, pl.BlockSpec((tm, GD), lambda i, h: (i, COL_Z + h)), vec,
                  pl.BlockSpec((tm, GD), lambda i, h: (i, GH + h))],
        out_specs=[blk, pl.BlockSpec((tm, GD), lambda i, h: (i, COL_Z + h)), vec],
        out_shape=[jax.ShapeDtypeStruct((T, GH * GD), F32), jax.ShapeDtypeStruct((T, N_PROJ), BF16),
                   jax.ShapeDtypeStruct((1, GD), F32)],
        compiler_params=_cp("arbitrary", "arbitrary"),
    )(o_raw, proj, nw, dmixed)


def _adamw(w, g, m, v, *, name):
    shape = w.shape
    cols = shape[-1]
    w2, g2, m2, v2 = (a.reshape(-1, cols) for a in (w, g, m, v))
    rows = w2.shape[0]
    tr = next((t for t in (512, 256, 128, 64, 32, 16, 8) if rows % t == 0), rows)

    def body(w_ref, g_ref, m_ref, v_ref, d_ref, nm_ref, nv_ref):
        gv = g_ref[...]
        nm = B1 * m_ref[...] + (1.0 - B1) * gv
        nv = B2 * v_ref[...] + (1.0 - B2) * jnp.square(gv)
        m_hat = nm / (1.0 - B1 ** STEP)
        v_hat = nv / (1.0 - B2 ** STEP)
        d_ref[...] = -LR * (m_hat / (jnp.sqrt(v_hat) + AEPS) + WD * w_ref[...])
        nm_ref[...] = nm
        nv_ref[...] = nv

    blk = pl.BlockSpec((tr, cols), lambda i: (i, 0))
    out = pl.pallas_call(
        body, name=name, grid=(rows // tr,), in_specs=[blk] * 4, out_specs=[blk] * 3,
        out_shape=[jax.ShapeDtypeStruct((rows, cols), F32)] * 3, compiler_params=_cp("parallel"),
    )(w2, g2, m2, v2)
    return tuple(o.reshape(shape) for o in out)


def _layer_weights(w_in):
    return jnp.pad(w_in, ((0, 0), (0, N_PROJ - w_in.shape[1])))


def _layer_params(attn_norm, sinks, gcw, a_log, dt_bias, gnw, ffn_norm, fcw, fcb):
    lanes4 = lambda v: jnp.pad(v, (GH, 128 - 2 * GH))[None]
    return dict(attn_norm=attn_norm[None], sinks=jnp.pad(sinks, (0, 128 - HQ))[None], gcw=gcw, alog=lanes4(a_log),
                dtb=lanes4(dt_bias), gnw=gnw[None], ffn_norm=ffn_norm[None], fcw=fcw, fcb=fcb[None])


def _mixer_fwd(x, W, P, l, carry=None):
    n = lambda s: f"l{l}_{s}"
    h = _rmsnorm_fwd(x, P["attn_norm"], name=n("norm1"))
    proj = _mm(h, W["all"], name=n("proj"), mode="nn", dims=(T, N_MAIN, D), tm=512, tn=N_MAIN, tk=D, out_dtype=BF16)
    ba = _mm(h, W["all"], name=n("proj_ba"), mode="nn", dims=(T, 128, D), tm=1024, tn=128, tk=D,
             b_spec=pl.BlockSpec((D, 128), lambda i, j, k: (k, COL_BA)))
    mixed = _swa_fwd(proj, P["sinks"], name=n("swa"))
    qkv_c = _gdn_prep_fwd(proj, P["gcw"], name=n("gdn_prep"))
    (o_raw, s_all, t_all), carried = _gdn_chunk_fwd(qkv_c, ba, P["alog"], P["dtb"], name=n("gdn_chunk"), carry=carry)
    mixed = _gdn_post_fwd(o_raw, proj, P["gnw"], mixed, name=n("gdn_post"))
    x1 = _mm(mixed, W["out"], res=x, name=n("out_proj"), mode="nn", dims=(T, D, D), tm=512, tn=D, tk=D)
    saved = dict(x=x, h=h, proj=proj, ba=ba, qkv_c=qkv_c, o_raw=o_raw, s_all=s_all, t_all=t_all, mixed=mixed, x1=x1)
    return x1, saved, carried


def _ffn_fwd(x1, W, P, l, carry=None):
    n = lambda s: f"l{l}_{s}"
    h2 = _rmsnorm_fwd(x1, P["ffn_norm"], name=n("norm2"))
    gu = _mm(h2, W["ffn"], name=n("ffn_in"), mode="nn", dims=(T, 2 * DFF, D), tm=512, tn=FFN_CW, tk=D, out_dtype=BF16,
             b_spec=pl.BlockSpec((None, D, FFN_CW), lambda i, j, k: (j, k, 0)), n_outer=True)
    act, carried = _ffn_act_fwd(gu, P["fcw"], P["fcb"], name=n("ffn_act"), carry=carry)
    x2 = _mm(act, W["down"], res=x1, name=n("ffn_down"), mode="nn", dims=(T, D, DFF), tm=512, tn=D, tk=DFF)
    return x2, dict(h2=h2, gu=gu, act=act), carried


def _layer_bwd(dx2, sv, W, P, l, carry=None):
    n = lambda s: f"l{l}_{s}"
    CW = FFN_CW
    dact = _mm(dx2, W["down"], name=n("d_act"), mode="nt", dims=(T, DFF, D), tm=512, tn=DFF // 2, tk=D, out_dtype=BF16,
               n_outer=True)
    g_down = _mm(sv["act"], dx2, name=n("g_down"), mode="tn", dims=(DFF, D, T), tm=DFF // 2, tn=D, tk=1024, out_dtype=BF16)
    dgu, g_fcw, g_fcb = _ffn_act_bwd(sv["gu"], P["fcw"], P["fcb"], dact, name=n("d_ffn_act"))
    dh2 = _mm(dgu, W["ffn"], name=n("d_h2"), mode="nt", dims=(T, D, 2 * DFF), tm=1024, tn=D, tk=CW,
              a_spec=pl.BlockSpec((None, 1024, CW), lambda i, j, k: (k // 2, i, k % 2)),
              b_spec=pl.BlockSpec((None, D, CW), lambda i, j, k: (k, j, 0)))
    g_ffn = _mm(sv["h2"], dgu, name=n("g_ffn"), mode="tn", dims=(D, 2 * DFF, T), tm=D, tn=CW, tk=1024,
                b_spec=pl.BlockSpec((None, 1024, CW), lambda i, j, k: (j // 2, k, j % 2)),
                out_spec=pl.BlockSpec((None, D, CW), lambda i, j, k: (j, i, 0)),
                out_shape=jax.ShapeDtypeStruct((N_CHIP, D, CW), BF16), out_dtype=BF16)
    dx1, g_ffn_norm = _rmsnorm_bwd(sv["x1"], P["ffn_norm"], dh2, dx2, name=n("d_norm2"))
    dmixed = _mm(dx1, W["out"], name=n("d_mixed"), mode="nt", dims=(T, D, D), tm=512, tn=D, tk=D)
    g_out = _mm(sv["mixed"], dx1, name=n("g_out"), mode="tn", dims=(D, D, T), tm=D, tn=D, tk=1024, out_dtype=BF16)
    do_raw, dproj, g_gnw = _gdn_post_bwd(sv["o_raw"], sv["proj"], P["gnw"], dmixed, name=n("d_gdn_post"))
    (dqkv_c, dproj, g_alog, g_dtb), carried = _gdn_chunk_bwd(
        sv["qkv_c"], sv["ba"], P["alog"], P["dtb"], sv["s_all"], sv["t_all"], do_raw, dproj, name=n("d_gdn_chunk"), carry=carry)
    dproj, g_gcw = _gdn_prep_bwd(sv["proj"], P["gcw"], dqkv_c, dproj, name=n("d_gdn_prep"))
    dproj, g_sinks = _swa_bwd(sv["proj"], P["sinks"], dmixed, dproj, name=n("d_swa"))
    dh = _mm(dproj, W["all"], name=n("d_h"), mode="nt", dims=(T, D, N_PROJ), tm=512, tn=D, tk=N_PROJ)
    g_all = _mm(sv["h"], dproj, name=n("g_in"), mode="tn", dims=(D, N_PROJ, T), tm=D, tn=N_PROJ, tk=1024, out_dtype=BF16)
    dx, g_attn_norm = _rmsnorm_bwd(sv["x"], P["attn_norm"], dh, dx1, name=n("d_norm1"))
    grads = dict(
        attn_norm=g_attn_norm[0], w_in=g_all, attn_sinks=g_sinks[0, :HQ], gdn_conv_w=g_gcw,
        gdn_a_log=g_alog[0, GH:2 * GH], gdn_dt_bias=g_dtb[0, GH:2 * GH], gdn_norm=g_gnw[0], w_out=g_out,
        ffn_norm=g_ffn_norm[0], w_ffn_in=g_ffn, ffn_conv_w=g_fcw, ffn_conv_b=g_fcb[0], w_down=g_down)
    return dx, grads, carried


def _pos():
    return lax.axis_index("x"), lax.axis_index("y"), lax.axis_index("c")


def _other_chips(x, y):
    return [(1 - x, y), (x, 1 - y), (1 - x, 1 - y)]


def _remote(src, dst, send_sems, recv_sems, k, to):
    return pltpu.make_async_remote_copy(src_ref=src, dst_ref=dst, send_sem=send_sems.at[k], recv_sem=recv_sems.at[k],
                                        device_id=to, device_id_type=MESH)


def _run_carry(carry, *, name):
    ci, co = len(carry.arrays), len(carry.out_shapes)

    def body(*refs):
        sends, recvs = carry.plan(refs[:ci], refs[ci:ci + co], refs[-2], refs[-1])
        for cp in sends:
            cp.start()
        for cp in recvs:
            cp.wait_recv()
        for cp in sends:
            cp.wait_send()

    return list(pl.pallas_call(
        body, name=name, in_specs=[ANY] * ci, out_specs=[ANY] * co, out_shape=carry.out_shapes,
        scratch_shapes=[pltpu.SemaphoreType.DMA((carry.nsem,)), pltpu.SemaphoreType.DMA((carry.nsem,))],
    )(*carry.arrays))


def _chip_index():
    return 2 * lax.axis_index("x") + lax.axis_index("y")


def _gather_carry(shards):
    def plan(srcs, outs, send_sems, recv_sems):
        x, y, c = _pos()
        chip = 2 * x + y
        others = [(k, px, py, 2 * px + py) for k, (px, py) in enumerate(_other_chips(x, y))]
        sends = [_remote(srcs[t], outs[t].at[chip], send_sems, recv_sems, 3 * t + k, (px, py, c))
                 for t in range(len(srcs)) for k, px, py, _ in others]
        recvs = [_remote(srcs[t], outs[t].at[j], send_sems, recv_sems, 3 * t + k, (x, y, c))
                 for t in range(len(srcs)) for k, _, _, j in others]
        return sends, recvs

    return _Carry(shards, [jax.ShapeDtypeStruct((N_CHIP,) + s.shape, s.dtype) for s in shards], 3 * len(shards), plan)


def _fill_own(outs, shards):
    return [lax.dynamic_update_index_in_dim(o, s, _chip_index(), 0) for o, s in zip(outs, shards)]


HBM_SPEC = pl.BlockSpec(memory_space=pltpu.HBM)
SEM_SPEC = pl.BlockSpec(memory_space=pltpu.SEMAPHORE)
DATAFLOW = pltpu.SideEffectType.DATAFLOW_SIDE_EFFECTING


def _gather_start(shards, *, name):
    nt, plan = len(shards), _gather_carry(shards).plan

    def body(*refs):
        srcs, lands, send_sems, recv_sems, token = refs[:nt], refs[nt:2 * nt], refs[2 * nt], refs[2 * nt + 1], refs[-1]
        for cp in plan(srcs, lands, send_sems, recv_sems)[0]:
            cp.start()
        token[...] = jnp.zeros_like(token)

    lands = [lax.empty((N_CHIP,) + s.shape, s.dtype) for s in shards]
    hbm = lambda a: pltpu.with_memory_space_constraint(a, pltpu.HBM)
    out = pl.pallas_call(
        body, name=name, in_specs=[HBM_SPEC] * (2 * nt),
        out_specs=[SEM_SPEC, SEM_SPEC] + [HBM_SPEC] * (2 * nt) + [VMEM_SPEC],
        out_shape=[pltpu.SemaphoreType.DMA((3 * nt,)), pltpu.SemaphoreType.DMA((3 * nt,))]
        + [pltpu.HBM(a.shape, a.dtype) for a in list(shards) + lands] + [jax.ShapeDtypeStruct((8, 128), F32)],
        input_output_aliases={t: 2 + t for t in range(2 * nt)},
        compiler_params=pltpu.CompilerParams(has_side_effects=DATAFLOW),
    )(*[hbm(a) for a in list(shards) + lands])
    return out[0], out[1], list(out[2:2 + nt]), list(out[2 + nt:2 + 2 * nt]), out[-1]


def _gather_wait(send_sems, recv_sems, shards, lands, after, *, name):
    nt, plan = len(shards), _gather_carry(shards).plan

    def body(*refs):
        srcs, lands_in, send, recv = refs[:nt], refs[nt:2 * nt], refs[2 * nt], refs[2 * nt + 1]
        sends, recvs = plan(srcs, lands_in, send, recv)
        for cp in sends:
            cp.wait_send()
        for cp in recvs:
            cp.wait_recv()

    out = pl.pallas_call(
        body, name=name, in_specs=[HBM_SPEC] * (2 * nt) + [SEM_SPEC, SEM_SPEC, ANY], out_specs=[HBM_SPEC] * (2 * nt),
        out_shape=[pltpu.HBM(a.shape, a.dtype) for a in list(shards) + list(lands)],
        input_output_aliases={t: t for t in range(2 * nt)},
        compiler_params=pltpu.CompilerParams(has_side_effects=DATAFLOW),
    )(*shards, *lands, send_sems, recv_sems, after)
    return _fill_own(out[nt:], out[:nt])


def _ag_small(v, *, name):
    m, n = v.shape

    def body(x_ref, out_ref, red_ref, send_sems, recv_sems, local_sem):
        x, y, c = _pos()
        me, sibling = (x, y, c), (x, y, 1 - c)
        chips = _other_chips(x, y)
        rows = lambda px, py, pc: out_ref.at[pl.ds(pl.multiple_of((4 * px + 2 * py + pc) * m, 8), m), :]
        mine = pltpu.make_async_copy(x_ref, rows(*me), local_sem)
        mine.start()
        first = [_remote(x_ref, rows(*me), send_sems, recv_sems, 0, sibling)]
        first += [_remote(x_ref, rows(*me), send_sems, recv_sems, 1 + k, (*chip, c)) for k, chip in enumerate(chips)]
        for cp in first:
            cp.start()
        passed = [_remote(rows(*chip, c), rows(*chip, c), send_sems, recv_sems, 4 + k, sibling) for k, chip in enumerate(chips)]
        for k, chip in enumerate(chips):
            _remote(rows(*chip, c), rows(*chip, c), send_sems, recv_sems, 1 + k, me).wait_recv()
            passed[k].start()
        _remote(rows(*sibling), rows(*sibling), send_sems, recv_sems, 0, me).wait_recv()
        for k, chip in enumerate(chips):
            _remote(rows(*chip, 1 - c), rows(*chip, 1 - c), send_sems, recv_sems, 4 + k, me).wait_recv()
        for cp in first + passed:
            cp.wait_send()
        mine.wait()
        acc = out_ref[0:m, :]
        for d in range(1, N_DEV):
            acc = acc + out_ref[d * m:(d + 1) * m, :]
        red_ref[...] = acc

    return pl.pallas_call(
        body, name=name, in_specs=[VMEM_SPEC], out_specs=[VMEM_SPEC, VMEM_SPEC],
        out_shape=[jax.ShapeDtypeStruct((N_DEV * m, n), v.dtype), jax.ShapeDtypeStruct((m, n), v.dtype)],
        scratch_shapes=[pltpu.SemaphoreType.DMA((7,)), pltpu.SemaphoreType.DMA((7,)), pltpu.SemaphoreType.DMA],
    )(v)


def _halves(ref, c):
    rh = ref.shape[1] // 2
    return ref.at[:, pl.ds(pl.multiple_of(c * rh, 16), rh), :]


def _rs_swap(gs, *, name):
    nt = len(gs)

    def body(*refs):
        g, theirs = refs[:nt], refs[nt:2 * nt]
        send_sems, recv_sems = refs[2 * nt:]
        x, y, c = _pos()
        swaps = [_remote(_halves(g[t], 1 - c), theirs[t], send_sems, recv_sems, t, (x, y, 1 - c)) for t in range(nt)]
        for cp in swaps:
            cp.start()
        for cp in swaps:
            cp.wait()

    return pl.pallas_call(
        body, name=name, in_specs=[ANY] * nt, out_specs=[ANY] * nt,
        out_shape=[jax.ShapeDtypeStruct((a.shape[0], a.shape[1] // 2, a.shape[2]), a.dtype) for a in gs],
        scratch_shapes=[pltpu.SemaphoreType.DMA((nt,)), pltpu.SemaphoreType.DMA((nt,))],
    )(*gs)


def _exchange_carry(ss):
    def plan(s, out, send_sems, recv_sems):
        x, y, c = _pos()
        chip = 2 * x + y
        others = [(k, px, py, 2 * px + py) for k, (px, py) in enumerate(_other_chips(x, y))]
        sends = [_remote(s[t].at[j], out[t].at[chip], send_sems, recv_sems, 3 * t + k, (px, py, c))
                 for t in range(len(s)) for k, px, py, j in others]
        recvs = [_remote(s[t].at[j], out[t].at[j], send_sems, recv_sems, 3 * t + k, (x, y, c))
                 for t in range(len(s)) for k, _, _, j in others]
        return sends, recvs

    return _Carry(ss, [jax.ShapeDtypeStruct(a.shape, a.dtype) for a in ss], 3 * len(ss), plan)


def _fill_own_slab(outs, ss):
    chip = _chip_index()
    return [lax.dynamic_update_index_in_dim(o, lax.dynamic_index_in_dim(s, chip, 0, keepdims=False), chip, 0)
            for o, s in zip(outs, ss)]


def _rs_join(rs, *, name):
    nt = len(rs)

    def body(*refs):
        r, theirs = refs[:nt], refs[nt:2 * nt]
        send_sems, recv_sems = refs[2 * nt:]
        x, y, c = _pos()
        swaps = [_remote(r[t], theirs[t], send_sems, recv_sems, t, (x, y, 1 - c)) for t in range(nt)]
        for cp in swaps:
            cp.start()
        for cp in swaps:
            cp.wait()

    theirs = pl.pallas_call(
        body, name=name, in_specs=[ANY] * nt, out_specs=[ANY] * nt,
        out_shape=[jax.ShapeDtypeStruct(a.shape, a.dtype) for a in rs],
        scratch_shapes=[pltpu.SemaphoreType.DMA((nt,)), pltpu.SemaphoreType.DMA((nt,))],
    )(*rs)
    first = lax.axis_index("c") == 0
    return [jnp.concatenate([jnp.where(first, a, b), jnp.where(first, b, a)], axis=0) for a, b in zip(rs, theirs)]


def _row_tile(rows, dtype):
    unit = 16 if dtype == BF16 else 8
    if rows <= 512:
        return rows
    return next((t for t in (512, 256, 128, 64, 32, 16, 8) if rows % t == 0 and t % unit == 0), rows)


def _add_pair(a, b, *, name):
    n, rh, cols = a.shape
    tr = _row_tile(rh, BF16)

    def body(a_ref, b_ref, o_ref):
        o_ref[...] = (a_ref[...].astype(F32) + b_ref[...].astype(F32)).astype(BF16)

    blk = pl.BlockSpec((None, tr, cols), lambda j, i: (j, i, 0))
    return pl.pallas_call(
        body, name=name, grid=(n, rh // tr), in_specs=[blk, blk], out_specs=blk,
        out_shape=jax.ShapeDtypeStruct(a.shape, BF16), compiler_params=_cp("parallel", "parallel"),
    )(a, b)


def _sum_chips(b, *, name):
    n, rh, cols = b.shape
    tr = _row_tile(rh, BF16)

    def body(b_ref, o_ref):
        acc = b_ref[0].astype(F32)
        for j in range(1, n):
            acc = acc + b_ref[j].astype(F32)
        o_ref[...] = acc

    return pl.pallas_call(
        body, name=name, grid=(rh // tr,), in_specs=[pl.BlockSpec((n, tr, cols), lambda i: (0, i, 0))],
        out_specs=pl.BlockSpec((tr, cols), lambda i: (i, 0)),
        out_shape=jax.ShapeDtypeStruct((rh, cols), F32), compiler_params=_cp("parallel"),
    )(b)


BIG = ("w_in", "w_out", "w_ffn_in", "w_down")
W_IN_SHARD = (N_MAIN + 2 * GH) // N_CHIP


def _rs_pairs(lg, l):
    g_in = jnp.stack([lg["w_in"][:, j * W_IN_SHARD:(j + 1) * W_IN_SHARD] for j in range(N_CHIP)])
    gs = [g_in, lg["w_out"].reshape(N_CHIP, D // N_CHIP, D), lg["w_ffn_in"], lg["w_down"].reshape(N_CHIP, DFF // N_CHIP, D)]
    c = lax.axis_index("c")
    mine = [lax.dynamic_slice_in_dim(a, c * (a.shape[1] // 2), a.shape[1] // 2, axis=1) for a in gs]
    theirs = _rs_swap(gs, name=f"rs{l}_swap")
    return [_add_pair(a, b, name=f"rs{l}_add_{n}") for a, b, n in zip(mine, theirs, BIG)]


def _rs_finish(got, pairs, l):
    got = _fill_own_slab(got, pairs)
    return _rs_join([_sum_chips(b, name=f"rs{l}_sum_{n}") for b, n in zip(got, BIG)], name=f"rs{l}_join")


def _size(shape):
    n = 1
    for s in shape:
        n *= s
    return n


def _pack_flat(parts, total, dtype):
    flat = jnp.concatenate([p.reshape(-1).astype(dtype) for p in parts])
    return jnp.pad(flat, (0, total - flat.shape[0]))


def _unpack_flat(flat, shapes):
    out, o = [], 0
    for s in shapes:
        out.append(flat[o:o + _size(s)].reshape(s))
        o += _size(s)
    return out


WEIGHTS = ("attn_norm", "w_in", "attn_sinks", "gdn_conv_w", "gdn_a_log", "gdn_dt_bias", "gdn_norm", "w_out", "ffn_norm",
           "w_ffn_in", "ffn_conv_w", "ffn_conv_b", "w_down", "final_norm")
SMALL = {"attn_norm": (DEPTH, D), "attn_sinks": (DEPTH, HQ), "gdn_a_log": (DEPTH, GH), "gdn_dt_bias": (DEPTH, GH),
         "gdn_norm": (DEPTH, GD), "ffn_norm": (DEPTH, D), "ffn_conv_b": (DEPTH, DFF), "final_norm": (D,)}
CONV_FULL = {"gdn_conv_w": (DEPTH, GK, 3 * GH * GD), "ffn_conv_w": (DEPTH, FK, DFF)}
CONV_SHARD = {"gdn_conv_w": (DEPTH, GK, 3 * GH * GD // N_CHIP), "ffn_conv_w": (DEPTH, FK, DFF // N_CHIP)}
CONV_ROWS, SMALLG_ROWS, SMALLW_ROWS = 64, 320, 144


def kernel(x, attn_norm, w_in, attn_sinks, gdn_conv_w, gdn_a_log, gdn_dt_bias, gdn_norm, w_out, ffn_norm, w_ffn_in, ffn_conv_w, ffn_conv_b, w_down, final_norm, loss_target, m_attn_norm, m_w_in, m_attn_sinks, m_gdn_conv_w, m_gdn_a_log, m_gdn_dt_bias, m_gdn_norm, m_w_out, m_ffn_norm, m_w_ffn_in, m_ffn_conv_w, m_ffn_conv_b, m_w_down, m_final_norm, v_attn_norm, v_w_in, v_attn_sinks, v_gdn_conv_w, v_gdn_a_log, v_gdn_dt_bias, v_gdn_norm, v_w_out, v_ffn_norm, v_w_ffn_in, v_ffn_conv_w, v_ffn_conv_b, v_w_down, v_final_norm):
    w = dict(zip(WEIGHTS, (attn_norm, w_in, attn_sinks, gdn_conv_w, gdn_a_log, gdn_dt_bias, gdn_norm, w_out, ffn_norm,
                           w_ffn_in, ffn_conv_w, ffn_conv_b, w_down, final_norm)))
    m = dict(zip(WEIGHTS, (m_attn_norm, m_w_in, m_attn_sinks, m_gdn_conv_w, m_gdn_a_log, m_gdn_dt_bias, m_gdn_norm, m_w_out,
                           m_ffn_norm, m_w_ffn_in, m_ffn_conv_w, m_ffn_conv_b, m_w_down, m_final_norm)))
    v = dict(zip(WEIGHTS, (v_attn_norm, v_w_in, v_attn_sinks, v_gdn_conv_w, v_gdn_a_log, v_gdn_dt_bias, v_gdn_norm, v_w_out,
                           v_ffn_norm, v_w_ffn_in, v_ffn_conv_w, v_ffn_conv_b, v_w_down, v_final_norm)))
    cx, cy, _ = _pos()
    chip = 2 * cx + cy

    cpack = _pack_flat([w[n] for n in CONV_SHARD], CONV_ROWS * 128, F32).reshape(CONV_ROWS, 128)
    cgath, _ = _ag_small(cpack, name="gather_conv_w")
    cgath = cgath.reshape(N_DEV, CONV_ROWS * 128)
    cshards = [_unpack_flat(cgath[2 * j], list(CONV_SHARD.values())) for j in range(N_CHIP)]
    conv = {n: jnp.concatenate([cshards[j][i] for j in range(N_CHIP)], axis=2) for i, n in enumerate(CONV_SHARD)}

    Ps = [_layer_params(attn_norm[l], attn_sinks[l], conv["gdn_conv_w"][l], gdn_a_log[l], gdn_dt_bias[l], gdn_norm[l],
                        ffn_norm[l], conv["ffn_conv_w"][l], ffn_conv_b[l]) for l in range(DEPTH)]

    wb = {n: [w[n][l].astype(BF16) for l in range(DEPTH)] for n in BIG}
    mixer_w = lambda g_in, g_out: dict(all=_layer_weights(jnp.concatenate(list(g_in), axis=1)), out=g_out.reshape(D, D))
    ffn_w = lambda g_ffn, g_down: dict(ffn=g_ffn, down=g_down.reshape(DFF, D))
    groups = [[wb[a][l], wb[b][l]] for l in range(DEPTH) for a, b in (("w_in", "w_out"), ("w_ffn_in", "w_down"))]
    def gather_next(i, prev):
        shards, _ = lax.optimization_barrier((groups[i], prev))
        return _gather_start(shards, name=f"gather{i}_start")

    def after_start(xv, started):
        return lax.optimization_barrier((xv, started[4]))[0]

    g0 = _gather_start(groups[0], name="gather0_start")
    Wm0 = mixer_w(*_gather_wait(*g0[:4], g0[4], name="gather0_wait"))
    g1 = gather_next(1, Wm0)
    h, sv0, _ = _mixer_fwd(after_start(x[0], g1), Wm0, Ps[0], 0)
    Wf0 = ffn_w(*_gather_wait(*g1[:4], h, name="gather1_wait"))
    g2 = gather_next(2, Wf0)
    h, sv0f, _ = _ffn_fwd(after_start(h, g2), Wf0, Ps[0], 0)
    Wm1 = mixer_w(*_gather_wait(*g2[:4], h, name="gather2_wait"))
    g3 = gather_next(3, Wm1)
    h, sv1, _ = _mixer_fwd(after_start(h, g3), Wm1, Ps[1], 1)
    Wf1 = ffn_w(*_gather_wait(*g3[:4], h, name="gather3_wait"))
    h, sv1f, _ = _ffn_fwd(h, Wf1, Ps[1], 1)
    dx, g_final, loss_part = _loss_head(h, final_norm[None], loss_target[0], name="loss_head")

    lg, reduced = [None] * DEPTH, [None] * DEPTH
    dx, lg[1], _ = _layer_bwd(dx, {**sv1, **sv1f}, {**Wm1, **Wf1}, Ps[1], 1)
    pairs1 = _rs_pairs(lg[1], 1)
    dx, lg[0], got1 = _layer_bwd(dx, {**sv0, **sv0f}, {**Wm0, **Wf0}, Ps[0], 0, carry=_exchange_carry(pairs1))
    reduced[1] = _rs_finish(got1, pairs1, 1)
    pairs0 = _rs_pairs(lg[0], 0)
    reduced[0] = _rs_finish(_run_carry(_exchange_carry(pairs0), name="rs0_exchange"), pairs0, 0)
    grad_x = dx[None]
    stacked = lambda n: jnp.stack([lg[l][n] for l in range(DEPTH)])

    small_parts = [g_final[0] if n == "final_norm" else stacked(n) for n in SMALL] + [stacked(n) for n in CONV_FULL]
    svec = _pack_flat(small_parts + [loss_part[0, :1]], SMALLG_ROWS * 128, F32).reshape(SMALLG_ROWS, 128)
    _, sred = _ag_small(svec, name="reduce_small")
    small_g = _unpack_flat(sred.reshape(-1), list(SMALL.values()) + list(CONV_FULL.values()) + [(1,)])
    g = dict(zip(list(SMALL) + list(CONV_FULL), small_g[:-1]))
    loss = small_g[-1][0]
    for n in CONV_FULL:
        wd = CONV_SHARD[n][2]
        g[n] = lax.dynamic_slice_in_dim(g[n], chip * wd, wd, axis=2)

    g.update({n: jnp.stack([reduced[l][i] for l in range(DEPTH)]) for i, n in enumerate(BIG)})

    delta, new_m, new_v = {}, {}, {}
    for n in BIG:
        delta[n], new_m[n], new_v[n] = _adamw(w[n], g[n], m[n], v[n], name=f"adamw_{n}")
    small_names = list(SMALL) + list(CONV_SHARD)
    small_shapes = list(SMALL.values()) + list(CONV_SHARD.values())
    packed = [_pack_flat([t[n] for n in small_names], SMALLW_ROWS * 128, F32).reshape(SMALLW_ROWS, 128) for t in (w, g, m, v)]
    for res, out in zip(_adamw(*packed, name="adamw_small"), (delta, new_m, new_v)):
        out.update(zip(small_names, _unpack_flat(res.reshape(-1), small_shapes)))

    return (loss, grad_x, *[g[n] for n in WEIGHTS], *[delta[n] for n in WEIGHTS], *[new_m[n] for n in WEIGHTS],
            *[new_v[n] for n in WEIGHTS])
```

```python
import functools

import jax
import jax.numpy as jnp
import numpy as np
from jax import lax
from jax.experimental import pallas as pl
from jax.experimental.pallas import tpu as pltpu

F32, BF16 = jnp.float32, jnp.bfloat16
HIGHEST = lax.Precision.HIGHEST
MESH = pl.DeviceIdType.MESH

D = 1024
T = 4096
DEPTH = 2
HQ, HKV, DH, WIN = 8, 2, 64, 128
GH, GD, GC, GK = 4, 128, 64, 4
DFF, FK = 2816, 3
EPS = 1e-6
N_MAIN = 2816
N_PROJ = N_MAIN + 128
COL_K, COL_V, COL_G, COL_Z, COL_BA = 4, 5, 6, 18, 22
N_CHUNK = T // GC
N_DEV, N_CHIP = 8, 4
FFN_CW = 2 * DFF // N_CHIP
LR, B1, B2, AEPS, WD, STEP = 0.001, 0.9, 0.999, 1e-08, 0.01, 10

VMEM_LIMIT = 56 * 1024 * 1024
ANY = pl.BlockSpec(memory_space=pl.ANY)
VMEM_SPEC = pl.BlockSpec(memory_space=pltpu.VMEM)


def _cp(*sem):
    return pltpu.CompilerParams(dimension_semantics=sem if sem else None, vmem_limit_bytes=VMEM_LIMIT)


class _Carry:
    def __init__(self, arrays, out_shapes, nsem, plan):
        self.arrays, self.out_shapes, self.nsem, self.plan = list(arrays), list(out_shapes), nsem, plan


def _seq_call(body, carry, *, name, steps, in_specs, out_specs, out_shape, args, scratch_shapes=(), aliases=None):
    in_specs, out_specs, out_shape, scratch = list(in_specs), list(out_specs), list(out_shape), list(scratch_shapes)
    n_in, n_out, n_scr = len(in_specs), len(out_specs), len(scratch)
    if carry is None:
        fn = body
    else:
        ci, co = len(carry.arrays), len(carry.out_shapes)

        def fn(*refs):
            ins, cins = refs[:n_in], refs[n_in:n_in + ci]
            outs, couts = refs[n_in + ci:n_in + ci + n_out], refs[n_in + ci + n_out:n_in + ci + n_out + co]
            scr, (ssem, rsem) = refs[n_in + ci + n_out + co:-2], refs[-2:]
            sends, recvs = carry.plan(cins, couts, ssem, rsem)

            @pl.when(pl.program_id(0) == 0)
            def _():
                for cp in sends:
                    cp.start()

            body(*ins, *outs, *scr)

            @pl.when(pl.program_id(0) == steps - 1)
            def _():
                for cp in recvs:
                    cp.wait_recv()
                for cp in sends:
                    cp.wait_send()

        in_specs += [ANY] * ci
        out_specs += [ANY] * co
        out_shape += carry.out_shapes
        scratch += [pltpu.SemaphoreType.DMA((carry.nsem,)), pltpu.SemaphoreType.DMA((carry.nsem,))]
        args = list(args) + carry.arrays
    outs = pl.pallas_call(
        fn, name=name, grid=(steps,), in_specs=in_specs, out_specs=out_specs, out_shape=out_shape,
        scratch_shapes=scratch, input_output_aliases=aliases or {}, compiler_params=_cp("arbitrary"),
    )(*args)
    return list(outs[:n_out]), list(outs[n_out:])


def _dot(a, b):
    return jnp.dot(a.astype(BF16), b.astype(BF16), preferred_element_type=F32)


def _dot_nt(a, b):
    return lax.dot_general(a.astype(BF16), b.astype(BF16), (((1,), (1,)), ((), ())), preferred_element_type=F32)


def _dot_tn(a, b):
    return lax.dot_general(a.astype(BF16), b.astype(BF16), (((0,), (0,)), ((), ())), preferred_element_type=F32)


def _iota(shape, dim):
    return lax.broadcasted_iota(jnp.int32, shape, dim)


def _col(x, idx):
    return jnp.sum(jnp.where(_iota(x.shape, 1) == idx, x, 0.0), axis=1, keepdims=True)


def _silu(y):
    return y * jax.nn.sigmoid(y)


_MM_DN = {"nn": (((1,), (0,)), ((), ())), "nt": (((1,), (1,)), ((), ())), "tn": (((0,), (0,)), ((), ()))}


def _mm(a, b, *, name, mode, dims, tm, tn, tk, res=None, out_dtype=F32, a_spec=None, b_spec=None, out_spec=None,
        out_shape=None, n_outer=False):
    M, N, K = dims
    assert M % tm == 0 and N % tn == 0 and K % tk == 0, (name, dims)
    nk = K // tk
    dn = _MM_DN[mode]

    def body(*refs):
        if res is None:
            a_ref, b_ref, o_ref, acc = refs
        else:
            a_ref, b_ref, r_ref, o_ref, acc = refs
        k = pl.program_id(2)

        @pl.when(k == 0)
        def _():
            acc[...] = jnp.zeros_like(acc)

        acc[...] += lax.dot_general(a_ref[...].astype(BF16), b_ref[...].astype(BF16), dn, preferred_element_type=F32)

        @pl.when(k == nk - 1)
        def _():
            r = acc[...]
            if res is not None:
                r = r + r_ref[...]
            o_ref[...] = r.astype(out_dtype)

    if a_spec is None:
        a_spec = pl.BlockSpec((tk, tm), lambda i, j, k: (k, i)) if mode == "tn" else pl.BlockSpec((tm, tk), lambda i, j, k: (i, k))
    if b_spec is None:
        b_spec = pl.BlockSpec((tn, tk), lambda i, j, k: (j, k)) if mode == "nt" else pl.BlockSpec((tk, tn), lambda i, j, k: (k, j))
    in_specs, args = [a_spec, b_spec], [a, b]
    if res is not None:
        in_specs.append(pl.BlockSpec((tm, tn), lambda i, j, k: (i, j)))
        args.append(res)
    out_spec = out_spec or pl.BlockSpec((tm, tn), lambda i, j, k: (i, j))
    grid = (M // tm, N // tn, nk)
    if n_outer:
        swap = lambda s: pl.BlockSpec(s.block_shape, lambda j, i, k, f=s.index_map: f(i, j, k))
        in_specs, out_spec, grid = [swap(s) for s in in_specs], swap(out_spec), (N // tn, M // tm, nk)
    return pl.pallas_call(
        body, name=name, grid=grid, in_specs=in_specs, out_specs=out_spec,
        out_shape=out_shape or jax.ShapeDtypeStruct((M, N), out_dtype),
        scratch_shapes=[pltpu.VMEM((tm, tn), F32)],
        compiler_params=_cp("parallel", "parallel", "arbitrary"),
    )(*args)


def _rms(x, w):
    return x * lax.rsqrt(jnp.mean(x * x, axis=-1, keepdims=True) + EPS) * w


def _rmsnorm_fwd(x, w, *, name, tm=512):
    def body(x_ref, w_ref, o_ref):
        o_ref[...] = _rms(x_ref[...], w_ref[...]).astype(BF16)

    return pl.pallas_call(
        body, name=name, grid=(x.shape[0] // tm,),
        in_specs=[pl.BlockSpec((tm, D), lambda i: (i, 0)), pl.BlockSpec((1, D), lambda i: (0, 0))],
        out_specs=pl.BlockSpec((tm, D), lambda i: (i, 0)),
        out_shape=jax.ShapeDtypeStruct(x.shape, BF16), compiler_params=_cp("parallel"),
    )(x, w)


def _rmsnorm_bwd(x, w, dh, dres, *, name, tm=512):
    def body(x_ref, w_ref, dh_ref, dr_ref, dx_ref, dw_ref):
        _, vjp = jax.vjp(_rms, x_ref[...], w_ref[...])
        dx, dw = vjp(dh_ref[...])
        dx_ref[...] = dx + dr_ref[...]

        @pl.when(pl.program_id(0) == 0)
        def _():
            dw_ref[...] = jnp.zeros_like(dw_ref)

        dw_ref[...] += dw

    row = pl.BlockSpec((tm, D), lambda i: (i, 0))
    vec = pl.BlockSpec((1, D), lambda i: (0, 0))
    return pl.pallas_call(
        body, name=name, grid=(x.shape[0] // tm,), in_specs=[row, vec, row, row], out_specs=[row, vec],
        out_shape=[jax.ShapeDtypeStruct(x.shape, F32), jax.ShapeDtypeStruct((1, D), F32)],
        compiler_params=_cp("arbitrary"),
    )(x, w, dh, dres)


def _loss_head(x, w, tgt, *, name, tm=512):
    def f(xv, wv, tv):
        err = _rms(xv, wv) - tv
        per_row = jnp.sum(err * err, axis=1, keepdims=True) * (0.5 / D)
        return jnp.sum(per_row, axis=0, keepdims=True)

    def body(x_ref, w_ref, t_ref, dx_ref, dw_ref, loss_ref):
        tv = t_ref[...]
        loss, vjp = jax.vjp(lambda xv, wv: f(xv, wv, tv), x_ref[...], w_ref[...])
        dx, dw = vjp(jnp.ones((1, 1), F32))
        dx_ref[...] = dx

        @pl.when(pl.program_id(0) == 0)
        def _():
            dw_ref[...] = jnp.zeros_like(dw_ref)
            loss_ref[...] = jnp.zeros_like(loss_ref)

        dw_ref[...] += dw
        loss_ref[...] += jnp.broadcast_to(loss, loss_ref.shape)

    row = pl.BlockSpec((tm, D), lambda i: (i, 0))
    vec = pl.BlockSpec((1, D), lambda i: (0, 0))
    return pl.pallas_call(
        body, name=name, grid=(x.shape[0] // tm,), in_specs=[row, vec, row],
        out_specs=[row, vec, pl.BlockSpec((1, 128), lambda i: (0, 0))],
        out_shape=[jax.ShapeDtypeStruct(x.shape, F32), jax.ShapeDtypeStruct((1, D), F32),
                   jax.ShapeDtypeStruct((1, 128), F32)],
        compiler_params=_cp("arbitrary"),
    )(x, w, tgt)


def _swa_bias():
    G = HQ // HKV
    r, c = np.arange(G * WIN)[:, None], np.arange(2 * WIN)[None, :]
    rel = (r % WIN) + WIN - c
    band = (rel >= 0) & (rel < WIN)
    out = np.empty((2, HKV, G * WIN, 2 * WIN), np.float32)
    for h2 in range(HKV):
        slope = 2.0 ** (-8.0 * (h2 * G + r // WIN + 1) / HQ)
        out[0, h2] = np.where(band & (c >= WIN), -slope * rel, -1e30)
        out[1, h2] = np.where(band, -slope * rel, -1e30)
    return jnp.asarray(out)


def _swa_group(qg, k2, v2, sk, bias, h2):
    s = _dot_nt(qg, k2) * (DH ** -0.5) + bias
    g1 = _iota((s.shape[0], 1), 0) >> 7
    sink = jnp.zeros((s.shape[0], 1), F32)
    for gi in range(HQ // HKV):
        sink = jnp.where(g1 == gi, _col(sk, h2 * (HQ // HKV) + gi), sink)
    m = lax.stop_gradient(jnp.maximum(jnp.max(s, axis=1, keepdims=True), sink))
    p = jnp.exp(s - m)
    den = jnp.sum(p, axis=1, keepdims=True) + jnp.exp(sink - m)
    return _dot(p * (1.0 / den), v2)


def _swa_split_q(q, h2):
    G = HQ // HKV
    return jnp.concatenate([q[:, (h2 * G + g) * DH:(h2 * G + g + 1) * DH] for g in range(G)], axis=0)


def _swa_merge_q(parts):
    G = HQ // HKV
    return jnp.concatenate([parts[h2][g * WIN:(g + 1) * WIN] for h2 in range(HKV) for g in range(G)], axis=1)


def _swa_specs():
    prev = lambda i: jnp.maximum(jnp.minimum(i, T // WIN - 1) - 1, 0)
    cur = lambda i: jnp.minimum(i, T // WIN - 1)
    return [
        pl.BlockSpec((WIN, HQ * DH), lambda i: (cur(i), 0)),
        pl.BlockSpec((WIN, 128), lambda i: (prev(i), COL_K)),
        pl.BlockSpec((WIN, 128), lambda i: (cur(i), COL_K)),
        pl.BlockSpec((WIN, 128), lambda i: (prev(i), COL_V)),
        pl.BlockSpec((WIN, 128), lambda i: (cur(i), COL_V)),
        pl.BlockSpec((1, 128), lambda i: (0, 0)),
        pl.BlockSpec((None, HKV, (HQ // HKV) * WIN, 2 * WIN), lambda i: (jnp.minimum(i, 1), 0, 0, 0)),
    ]


def _swa_fwd(proj, sinks, *, name):
    def body(q_ref, kp_ref, kc_ref, vp_ref, vc_ref, sk_ref, bias_ref, o_ref):
        q, sk = q_ref[...].astype(F32), sk_ref[...]
        outs = []
        for h2 in range(HKV):
            sl = slice(h2 * DH, (h2 + 1) * DH)
            k2 = jnp.concatenate([kp_ref[:, sl], kc_ref[:, sl]], axis=0).astype(F32)
            v2 = jnp.concatenate([vp_ref[:, sl], vc_ref[:, sl]], axis=0).astype(F32)
            outs.append(_swa_group(_swa_split_q(q, h2), k2, v2, sk, bias_ref[h2], h2))
        o_ref[...] = _swa_merge_q(outs).astype(BF16)

    return pl.pallas_call(
        body, name=name, grid=(T // WIN,), in_specs=_swa_specs(),
        out_specs=pl.BlockSpec((WIN, HQ * DH), lambda i: (i, 0)),
        out_shape=jax.ShapeDtypeStruct((T, HQ * DH + GH * GD), BF16), compiler_params=_cp("parallel"),
    )(proj, proj, proj, proj, proj, sinks, _swa_bias())


def _swa_bwd(proj, sinks, do, dproj, *, name):
    NB = T // WIN
    QW = HQ * DH

    def body(q_ref, kp_ref, kc_ref, vp_ref, vc_ref, sk_ref, bias_ref, do_ref, _, out_ref, dsk_ref, cq, ck, cv):
        i = pl.program_id(0)

        @pl.when(i == 0)
        def _():
            cq[...] = jnp.zeros_like(cq)
            ck[...] = jnp.zeros_like(ck)
            cv[...] = jnp.zeros_like(cv)
            dsk_ref[...] = jnp.zeros_like(dsk_ref)

        @pl.when(i < NB)
        def _():
            q, sk, dov = q_ref[...].astype(F32), sk_ref[...], do_ref[...].astype(F32)
            dqs, dkp, dkc, dvp, dvc = [], [], [], [], []
            dsk = jnp.zeros_like(sk)
            for h2 in range(HKV):
                sl = slice(h2 * DH, (h2 + 1) * DH)
                k2 = jnp.concatenate([kp_ref[:, sl], kc_ref[:, sl]], axis=0).astype(F32)
                v2 = jnp.concatenate([vp_ref[:, sl], vc_ref[:, sl]], axis=0).astype(F32)
                _, vjp = jax.vjp(functools.partial(_swa_group, bias=bias_ref[h2], h2=h2), _swa_split_q(q, h2), k2, v2, sk)
                dqg, dk2, dv2, dsk_h = vjp(_swa_split_q(dov, h2))
                dqs.append(dqg)
                dkp.append(dk2[:WIN]); dkc.append(dk2[WIN:])
                dvp.append(dv2[:WIN]); dvc.append(dv2[WIN:])
                dsk = dsk + dsk_h
            out_ref[:, :QW] = cq[...].astype(BF16)
            out_ref[:, QW:QW + 128] = (ck[...] + jnp.concatenate(dkp, axis=1)).astype(BF16)
            out_ref[:, QW + 128:] = (cv[...] + jnp.concatenate(dvp, axis=1)).astype(BF16)
            cq[...] = _swa_merge_q(dqs)
            ck[...] = jnp.concatenate(dkc, axis=1)
            cv[...] = jnp.concatenate(dvc, axis=1)
            dsk_ref[...] += dsk

        @pl.when(i == NB)
        def _():
            out_ref[:, :QW] = cq[...].astype(BF16)
            out_ref[:, QW:QW + 128] = ck[...].astype(BF16)
            out_ref[:, QW + 128:] = cv[...].astype(BF16)

    qblk = pl.BlockSpec((WIN, QW), lambda i: (jnp.minimum(i, NB - 1), 0))
    return pl.pallas_call(
        body, name=name, grid=(NB + 1,), in_specs=_swa_specs() + [qblk, ANY],
        out_specs=[pl.BlockSpec((WIN, QW + 256), lambda i: (jnp.maximum(i - 1, 0), 0)), pl.BlockSpec((1, 128), lambda i: (0, 0))],
        out_shape=[jax.ShapeDtypeStruct(dproj.shape, dproj.dtype), jax.ShapeDtypeStruct((1, 128), F32)],
        scratch_shapes=[pltpu.VMEM((WIN, QW), F32), pltpu.VMEM((WIN, 128), F32), pltpu.VMEM((WIN, 128), F32)],
        input_output_aliases={8: 0}, compiler_params=_cp("arbitrary"),
    )(proj, proj, proj, proj, proj, sinks, _swa_bias(), do, dproj)


RC = 256
HALO = 8


def _load_ext(ref, c):
    nch = T // RC
    r0 = pl.multiple_of(c * RC, RC)
    p0 = pl.multiple_of(jnp.maximum(r0 - HALO, 0), HALO)
    n0 = pl.multiple_of(jnp.minimum(r0 + RC, T - HALO), HALO)
    prev = jnp.where(c > 0, ref[pl.ds(p0, HALO), :].astype(F32), 0.0)
    nxt = jnp.where(c < nch - 1, ref[pl.ds(n0, HALO), :].astype(F32), 0.0)
    return jnp.concatenate([prev, ref[pl.ds(r0, RC), :].astype(F32), nxt], axis=0)


def _conv_ext(xe, w, K):
    y = w[K - 1:K, :] * xe
    for s in range(1, K):
        y = y + w[K - 1 - s:K - s, :] * pltpu.roll(xe, s, 0)
    return y


def _conv_bwd_ext(xe, dye, w, K, dw_ref):
    n = xe.shape[0]
    own = slice(HALO, HALO + RC)
    dx = w[K - 1:K, :] * dye
    dw_ref[K - 1:K, :] += jnp.sum(dye[own] * xe[own], axis=0, keepdims=True)
    for s in range(1, K):
        dx = dx + w[K - 1 - s:K - s, :] * pltpu.roll(dye, n - s, 0)
        dw_ref[K - 1 - s:K - s, :] += jnp.sum(dye[own] * pltpu.roll(xe, s, 0)[own], axis=0, keepdims=True)
    return dx[own]


def _gdn_post_conv(y, is_qk):
    a = _silu(y)
    nrm = a * lax.rsqrt(jnp.sum(a * a, axis=1, keepdims=True) + EPS)
    return jnp.where(is_qk, nrm, a)


def _gdn_prep_fwd(proj, conv_w, *, name):
    nblk = 3 * GH

    def body(x_ref, w_ref, o_ref):
        is_qk = pl.program_id(0) < 2 * GH
        w = w_ref[...]

        def chunk(c, carry):
            y = _conv_ext(_load_ext(x_ref, c), w, GK)[HALO:HALO + RC]
            o_ref[pl.ds(pl.multiple_of(c * RC, RC), RC), :] = _gdn_post_conv(y, is_qk)
            return carry

        lax.fori_loop(0, T // RC, chunk, 0)

    return pl.pallas_call(
        body, name=name, grid=(nblk,),
        in_specs=[pl.BlockSpec((T, 128), lambda j: (0, COL_G + j)), pl.BlockSpec((GK, 128), lambda j: (0, j))],
        out_specs=pl.BlockSpec((T, 128), lambda j: (0, j)),
        out_shape=jax.ShapeDtypeStruct((T, nblk * 128), F32), compiler_params=_cp("parallel"),
    )(proj, conv_w)


def _gdn_prep_bwd(proj, conv_w, dout, dproj, *, name):
    nblk = 3 * GH

    def body(x_ref, w_ref, d_ref, _, dx_ref, dw_ref):
        is_qk = pl.program_id(0) < 2 * GH
        w = w_ref[...]
        dw_ref[...] = jnp.zeros_like(dw_ref)

        def chunk(c, carry):
            xe = _load_ext(x_ref, c)
            _, vjp = jax.vjp(lambda y: _gdn_post_conv(y, is_qk), _conv_ext(xe, w, GK))
            (dye,) = vjp(_load_ext(d_ref, c))
            dx_ref[pl.ds(pl.multiple_of(c * RC, RC), RC), :] = _conv_bwd_ext(xe, dye, w, GK, dw_ref).astype(BF16)
            return carry

        lax.fori_loop(0, T // RC, chunk, 0)

    return pl.pallas_call(
        body, name=name, grid=(nblk,),
        in_specs=[pl.BlockSpec((T, 128), lambda j: (0, COL_G + j)), pl.BlockSpec((GK, 128), lambda j: (0, j)),
                  pl.BlockSpec((T, 128), lambda j: (0, j)), ANY],
        out_specs=[pl.BlockSpec((T, 128), lambda j: (0, COL_G + j)), pl.BlockSpec((GK, 128), lambda j: (0, j))],
        out_shape=[jax.ShapeDtypeStruct(dproj.shape, dproj.dtype), jax.ShapeDtypeStruct((GK, nblk * 128), F32)],
        input_output_aliases={3: 0}, compiler_params=_cp("parallel"),
    )(proj, conv_w, dout, dproj)


def _ffn_post_conv(y, b, up):
    return _silu(y + b) * up


def _ffn_act_fwd(gu, conv_w, conv_b, *, name, carry=None):
    nblk = DFF // 128

    def body(g_ref, u_ref, w_ref, b_ref, o_ref):
        w, b = w_ref[...], b_ref[...]

        def chunk(c, carry):
            rows = pl.ds(pl.multiple_of(c * RC, RC), RC)
            y = _conv_ext(_load_ext(g_ref, c), w, FK)[HALO:HALO + RC]
            o_ref[rows, :] = _ffn_post_conv(y, b, u_ref[rows, :].astype(F32)).astype(BF16)
            return carry

        lax.fori_loop(0, T // RC, chunk, 0)

    (act,), carried = _seq_call(
        body, carry, name=name, steps=nblk,
        in_specs=[pl.BlockSpec((T, 128), lambda j: (0, j)), pl.BlockSpec((T, 128), lambda j: (0, nblk + j)),
                  pl.BlockSpec((FK, 128), lambda j: (0, j)), pl.BlockSpec((1, 128), lambda j: (0, j))],
        out_specs=[pl.BlockSpec((T, 128), lambda j: (0, j))], out_shape=[jax.ShapeDtypeStruct((T, DFF), BF16)],
        args=(gu, gu, conv_w, conv_b))
    return act, carried


def _ffn_act_bwd(gu, conv_w, conv_b, dact, *, name):
    nblk = DFF // 128

    def body(g_ref, u_ref, w_ref, b_ref, d_ref, dgu_ref, dw_ref, db_ref):
        dg_ref, du_ref = dgu_ref.at[0], dgu_ref.at[1]
        w, b = w_ref[...], b_ref[...]
        dw_ref[...] = jnp.zeros_like(dw_ref)
        db_ref[...] = jnp.zeros_like(db_ref)

        def chunk(c, carry):
            rows = pl.ds(pl.multiple_of(c * RC, RC), RC)
            xe = _load_ext(g_ref, c)
            ue = _load_ext(u_ref, c)
            _, vjp = jax.vjp(_ffn_post_conv, _conv_ext(xe, w, FK), b, ue)
            dye, db, due = vjp(_load_ext(d_ref, c))
            du_ref[rows, :] = due[HALO:HALO + RC].astype(BF16)
            db_ref[...] += jnp.sum(dye[HALO:HALO + RC], axis=0, keepdims=True)
            dg_ref[rows, :] = _conv_bwd_ext(xe, dye, w, FK, dw_ref).astype(BF16)
            return carry

        lax.fori_loop(0, T // RC, chunk, 0)

    col = pl.BlockSpec((T, 128), lambda j: (0, j))
    return pl.pallas_call(
        body, name=name, grid=(nblk,),
        in_specs=[col, pl.BlockSpec((T, 128), lambda j: (0, nblk + j)), pl.BlockSpec((FK, 128), lambda j: (0, j)),
                  pl.BlockSpec((1, 128), lambda j: (0, j)), col],
        out_specs=[pl.BlockSpec((2, T, 128), lambda j: (0, 0, j)), pl.BlockSpec((FK, 128), lambda j: (0, j)),
                   pl.BlockSpec((1, 128), lambda j: (0, j))],
        out_shape=[jax.ShapeDtypeStruct((2, T, DFF), BF16), jax.ShapeDtypeStruct((FK, DFF), F32),
                   jax.ShapeDtypeStruct((1, DFF), F32)],
        compiler_params=_cp("parallel"),
    )(gu, gu, conv_w, conv_b, dact)


def _gdn_gates(ba, alog, dtb):
    beta = jax.nn.sigmoid(ba)
    g = -jnp.exp(alog) * jax.nn.softplus(ba + dtb)
    tril = (_iota((GC, GC), 0) >= _iota((GC, GC), 1)).astype(F32)
    return beta, jnp.dot(tril, g, precision=HIGHEST, preferred_element_type=F32)


def _hmap(f, *lists):
    return [f(*xs) for xs in zip(*lists)]


def _gdn_cols(beta_all, gc_all):
    return [_col(beta_all, h) for h in range(GH)], [_col(gc_all, GH + h) for h in range(GH)]


def _gdn_decay(Gcs):
    r, c = _iota((GC, GC), 0), _iota((GC, GC), 1)
    eye, ones = (r == c).astype(F32), jnp.ones((GC, GC), F32)
    grows = _hmap(lambda G: jnp.dot(ones, eye * G, precision=HIGHEST, preferred_element_type=F32), Gcs)
    return _hmap(lambda G, grow: jnp.exp(jnp.where(r >= c, G - grow, -1e30)), Gcs, grows)


def _gdn_A(ks, betas, Gcs):
    strict = _iota((GC, GC), 0) > _iota((GC, GC), 1)
    kk = _hmap(lambda k, b: _dot_nt(k * b, k), ks, betas)
    return _hmap(lambda a, d: jnp.where(strict, a * d, 0.0), kk, _gdn_decay(Gcs))


def _tri_inv(As):
    eye = (_iota((GC, GC), 0) == _iota((GC, GC), 1)).astype(F32)
    Tms, Ps = [eye - A for A in As], As
    for _ in range(GC.bit_length() - 2):
        Ps = _hmap(lambda P: _dot(P, P), Ps)
        Tms = _hmap(lambda Tm, P: Tm + _dot(Tm, P), Tms, Ps)
    return Tms


def _gdn_chunk(qs, ks, vs, betas, Gcs, Ss, Tms):
    decays = _gdn_decay(Gcs)
    eGs = _hmap(jnp.exp, Gcs)
    us = _hmap(lambda Tm, v, b: _dot(Tm, v * b), Tms, vs, betas)
    ws = _hmap(lambda Tm, k, b, eG: _dot(Tm, k * b * eG), Tms, ks, betas, eGs)
    qss = [q * (GD ** -0.5) for q in qs]
    qks = _hmap(lambda q, k, d: _dot_nt(q, k) * d, qss, ks, decays)
    glasts = [jnp.sum(jnp.where(_iota(G.shape, 0) == GC - 1, G, 0.0), axis=0, keepdims=True) for G in Gcs]
    kds = _hmap(lambda k, gl, G: k * jnp.exp(gl - G), ks, glasts, Gcs)
    v_news = _hmap(lambda u, w, S: u - _dot(w, S), us, ws, Ss)
    qS = _hmap(lambda q, eG, S: _dot(q * eG, S), qss, eGs, Ss)
    os = _hmap(lambda a, qk, vn: a + _dot(qk, vn), qS, qks, v_news)
    S_news = _hmap(lambda S, gl, kd, vn: S * jnp.exp(gl) + _dot_tn(kd, vn), Ss, glasts, kds, v_news)
    return os, S_news


def _gdn_chunk_fwd(qkv_c, ba, alog, dtb, *, name, carry=None):
    W3 = 3 * GH * GD

    def body(x_ref, ba_ref, al_ref, dt_ref, o_ref, s_ref, t_ref, S):
        @pl.when(pl.program_id(0) == 0)
        def _():
            S[...] = jnp.zeros_like(S)

        beta_all, gc_all = _gdn_gates(ba_ref[...], al_ref[...], dt_ref[...])
        qs, ks, vs = ([x_ref[:, (p * GH + h) * GD:(p * GH + h + 1) * GD] for h in range(GH)] for p in range(3))
        betas, Gcs = _gdn_cols(beta_all, gc_all)
        Tms = _tri_inv(_gdn_A(ks, betas, Gcs))
        Ss = [S[h] for h in range(GH)]
        os, S_news = _gdn_chunk(qs, ks, vs, betas, Gcs, Ss, Tms)
        for h in range(GH):
            s_ref[0, h] = Ss[h]
            t_ref[0, h] = Tms[h]
            o_ref[:, h * GD:(h + 1) * GD] = os[h]
            S[h] = S_news[h]

    vec = pl.BlockSpec((1, 128), lambda n: (0, 0))
    return _seq_call(
        body, carry, name=name, steps=N_CHUNK,
        in_specs=[pl.BlockSpec((GC, W3), lambda n: (n, 0)), pl.BlockSpec((GC, 128), lambda n: (n, 0)), vec, vec],
        out_specs=[pl.BlockSpec((GC, GH * GD), lambda n: (n, 0)),
                   pl.BlockSpec((1, GH, GD, GD), lambda n: (n, 0, 0, 0)),
                   pl.BlockSpec((1, GH, GC, GC), lambda n: (n, 0, 0, 0))],
        out_shape=[jax.ShapeDtypeStruct((T, GH * GD), F32), jax.ShapeDtypeStruct((N_CHUNK, GH, GD, GD), F32),
                   jax.ShapeDtypeStruct((N_CHUNK, GH, GC, GC), F32)],
        scratch_shapes=[pltpu.VMEM((GH, GD, GD), F32)], args=(qkv_c, ba, alog, dtb))


def _gdn_chunk_bwd(qkv_c, ba, alog, dtb, s_all, t_all, do, dproj, *, name, carry=None):
    W3 = 3 * GH * GD
    rev = lambda n: N_CHUNK - 1 - n

    def body(x_ref, ba_ref, al_ref, dt_ref, s_ref, t_ref, do_ref, _, dx_ref, dba_ref, dal_ref, ddt_ref, dS):
        @pl.when(pl.program_id(0) == 0)
        def _():
            dS[...] = jnp.zeros_like(dS)
            dal_ref[...] = jnp.zeros_like(dal_ref)
            ddt_ref[...] = jnp.zeros_like(ddt_ref)

        (beta_all, gc_all), vjp_gates = jax.vjp(_gdn_gates, ba_ref[...], al_ref[...], dt_ref[...])
        qs, ks, vs = ([x_ref[:, (p * GH + h) * GD:(p * GH + h + 1) * GD] for h in range(GH)] for p in range(3))
        Tms = [t_ref[0, h] for h in range(GH)]
        Ss = [s_ref[0, h] for h in range(GH)]
        dos = [do_ref[:, h * GD:(h + 1) * GD] for h in range(GH)]
        dSs = [dS[h] for h in range(GH)]

        def f(qs, ks, vs, b_all, g_all, Ss, Tms):
            return _gdn_chunk(qs, ks, vs, *_gdn_cols(b_all, g_all), Ss, Tms)

        def fa(ks, b_all, g_all):
            return _gdn_A(ks, *_gdn_cols(b_all, g_all))

        _, vjp = jax.vjp(f, qs, ks, vs, beta_all, gc_all, Ss, Tms)
        dqs, dks, dvs, db1, dg1, dS_prev, dTs = vjp((dos, dSs))
        dXs = _hmap(_dot_nt, dTs, Tms)
        dAs = _hmap(lambda Tm, dX: -_dot_tn(Tm, dX), Tms, dXs)
        _, vjp_a = jax.vjp(fa, ks, beta_all, gc_all)
        dks2, db2, dg2 = vjp_a(dAs)
        for h in range(GH):
            dS[h] = dS_prev[h]
            dx_ref[:, h * GD:(h + 1) * GD] = dqs[h]
            dx_ref[:, (GH + h) * GD:(GH + h + 1) * GD] = dks[h] + dks2[h]
            dx_ref[:, (2 * GH + h) * GD:(2 * GH + h + 1) * GD] = dvs[h]
        dba, dal, ddt = vjp_gates((db1 + db2, dg1 + dg2))
        dba_ref[...] = dba.astype(BF16)
        dal_ref[...] += dal
        ddt_ref[...] += ddt

    vec = pl.BlockSpec((1, 128), lambda n: (0, 0))
    return _seq_call(
        body, carry, name=name, steps=N_CHUNK,
        in_specs=[pl.BlockSpec((GC, W3), lambda n: (rev(n), 0)), pl.BlockSpec((GC, 128), lambda n: (rev(n), 0)),
                  vec, vec, pl.BlockSpec((1, GH, GD, GD), lambda n: (rev(n), 0, 0, 0)),
                  pl.BlockSpec((1, GH, GC, GC), lambda n: (rev(n), 0, 0, 0)),
                  pl.BlockSpec((GC, GH * GD), lambda n: (rev(n), 0)),
                  ANY],
        out_specs=[pl.BlockSpec((GC, W3), lambda n: (rev(n), 0)), pl.BlockSpec((GC, 128), lambda n: (rev(n), COL_BA)), vec, vec],
        out_shape=[jax.ShapeDtypeStruct((T, W3), F32), jax.ShapeDtypeStruct(dproj.shape, dproj.dtype),
                   jax.ShapeDtypeStruct((1, 128), F32), jax.ShapeDtypeStruct((1, 128), F32)],
        scratch_shapes=[pltpu.VMEM((GH, GD, GD), F32)], aliases={7: 1},
        args=(qkv_c, ba, alog, dtb, s_all, t_all, do, dproj))


def _gdn_post(o, z, nw):
    return o * lax.rsqrt(jnp.mean(o * o, axis=-1, keepdims=True) + EPS) * nw * _silu(z)


def _gdn_post_fwd(o_raw, proj, nw, mixed, *, name, tm=512):
    def body(o_ref, z_ref, w_ref, _, out_ref):
        out_ref[...] = _gdn_post(o_ref[...], z_ref[...].astype(F32), w_ref[...]).astype(BF16)

    return pl.pallas_call(
        body, name=name, grid=(T // tm, GH),
        in_specs=[pl.BlockSpec((tm, GD), lambda i, h: (i, h)), pl.BlockSpec((tm, GD), lambda i, h: (i, COL_Z + h)),
                  pl.BlockSpec((1, GD), lambda i, h: (0, 0)), ANY],
        out_specs=pl.BlockSpec((tm, GD), lambda i, h: (i, HQ * DH // GD + h)),
        out_shape=jax.ShapeDtypeStruct(mixed.shape, mixed.dtype), input_output_aliases={3: 0},
        compiler_params=_cp("parallel", "parallel"),
    )(o_raw, proj, nw, mixed)


def _gdn_post_bwd(o_raw, proj, nw, dmixed, *, name, tm=512):
    def body(o_ref, z_ref, w_ref, d_ref, do_ref, dz_ref, dw_ref):
        _, vjp = jax.vjp(_gdn_post, o_ref[...], z_ref[...].astype(F32), w_ref[...])
        do, dz, dw = vjp(d_ref[...])
        do_ref[...] = do
        dz_ref[...] = dz.astype(BF16)

        @pl.when((pl.program_id(0) == 0) & (pl.program_id(1) == 0))
        def _():
            dw_ref[...] = jnp.zeros_like(dw_ref)

        dw_ref[...] += dw

    blk = pl.BlockSpec((tm, GD), lambda i, h: (i, h))
    vec = pl.BlockSpec((1, GD), lambda i, h: (0, 0))
    return pl.pallas_call(
        body, name=name, grid=(T // tm, GH),
        in_specs=[blk, pl.BlockSpec((tm, GD), lambda i, h: (i, COL_Z + h)), vec,
                  pl.BlockSpec((tm, GD), lambda i, h: (i, GH + h))],
        out_specs=[blk, pl.BlockSpec((tm, GD), lambda i, h: (i, COL_Z + h)), vec],
        out_shape=[jax.ShapeDtypeStruct((T, GH * GD), F32), jax.ShapeDtypeStruct((T, N_PROJ), BF16),
                   jax.ShapeDtypeStruct((1, GD), F32)],
        compiler_params=_cp("arbitrary", "arbitrary"),
    )(o_raw, proj, nw, dmixed)


def _adamw(w, g, m, v, *, name):
    shape = w.shape
    cols = shape[-1]
    w2, g2, m2, v2 = (a.reshape(-1, cols) for a in (w, g, m, v))
    rows = w2.shape[0]
    tr = next((t for t in (512, 256, 128, 64, 32, 16, 8) if rows % t == 0), rows)

    def body(w_ref, g_ref, m_ref, v_ref, d_ref, nm_ref, nv_ref):
        gv = g_ref[...]
        nm = B1 * m_ref[...] + (1.0 - B1) * gv
        nv = B2 * v_ref[...] + (1.0 - B2) * jnp.square(gv)
        m_hat = nm / (1.0 - B1 ** STEP)
        v_hat = nv / (1.0 - B2 ** STEP)
        d_ref[...] = -LR * (m_hat / (jnp.sqrt(v_hat) + AEPS) + WD * w_ref[...])
        nm_ref[...] = nm
        nv_ref[...] = nv

    blk = pl.BlockSpec((tr, cols), lambda i: (i, 0))
    out = pl.pallas_call(
        body, name=name, grid=(rows // tr,), in_specs=[blk] * 4, out_specs=[blk] * 3,
        out_shape=[jax.ShapeDtypeStruct((rows, cols), F32)] * 3, compiler_params=_cp("parallel"),
    )(w2, g2, m2, v2)
    return tuple(o.reshape(shape) for o in out)


def _layer_weights(w_in):
    return jnp.pad(w_in, ((0, 0), (0, N_PROJ - w_in.shape[1])))


def _layer_params(attn_norm, sinks, gcw, a_log, dt_bias, gnw, ffn_norm, fcw, fcb):
    lanes4 = lambda v: jnp.pad(v, (GH, 128 - 2 * GH))[None]
    return dict(attn_norm=attn_norm[None], sinks=jnp.pad(sinks, (0, 128 - HQ))[None], gcw=gcw, alog=lanes4(a_log),
                dtb=lanes4(dt_bias), gnw=gnw[None], ffn_norm=ffn_norm[None], fcw=fcw, fcb=fcb[None])


def _mixer_fwd(x, W, P, l, carry=None):
    n = lambda s: f"l{l}_{s}"
    h = _rmsnorm_fwd(x, P["attn_norm"], name=n("norm1"))
    proj = _mm(h, W["all"], name=n("proj"), mode="nn", dims=(T, N_MAIN, D), tm=512, tn=N_MAIN, tk=D, out_dtype=BF16)
    ba = _mm(h, W["all"], name=n("proj_ba"), mode="nn", dims=(T, 128, D), tm=1024, tn=128, tk=D,
             b_spec=pl.BlockSpec((D, 128), lambda i, j, k: (k, COL_BA)))
    mixed = _swa_fwd(proj, P["sinks"], name=n("swa"))
    qkv_c = _gdn_prep_fwd(proj, P["gcw"], name=n("gdn_prep"))
    (o_raw, s_all, t_all), carried = _gdn_chunk_fwd(qkv_c, ba, P["alog"], P["dtb"], name=n("gdn_chunk"), carry=carry)
    mixed = _gdn_post_fwd(o_raw, proj, P["gnw"], mixed, name=n("gdn_post"))
    x1 = _mm(mixed, W["out"], res=x, name=n("out_proj"), mode="nn", dims=(T, D, D), tm=512, tn=D, tk=D)
    saved = dict(x=x, h=h, proj=proj, ba=ba, qkv_c=qkv_c, o_raw=o_raw, s_all=s_all, t_all=t_all, mixed=mixed, x1=x1)
    return x1, saved, carried


def _ffn_fwd(x1, W, P, l, carry=None):
    n = lambda s: f"l{l}_{s}"
    h2 = _rmsnorm_fwd(x1, P["ffn_norm"], name=n("norm2"))
    gu = _mm(h2, W["ffn"], name=n("ffn_in"), mode="nn", dims=(T, 2 * DFF, D), tm=512, tn=FFN_CW, tk=D, out_dtype=BF16,
             b_spec=pl.BlockSpec((None, D, FFN_CW), lambda i, j, k: (j, k, 0)), n_outer=True)
    act, carried = _ffn_act_fwd(gu, P["fcw"], P["fcb"], name=n("ffn_act"), carry=carry)
    x2 = _mm(act, W["down"], res=x1, name=n("ffn_down"), mode="nn", dims=(T, D, DFF), tm=512, tn=D, tk=DFF)
    return x2, dict(h2=h2, gu=gu, act=act), carried


def _layer_bwd(dx2, sv, W, P, l, carry=None):
    n = lambda s: f"l{l}_{s}"
    CW = FFN_CW
    dact = _mm(dx2, W["down"], name=n("d_act"), mode="nt", dims=(T, DFF, D), tm=512, tn=DFF // 2, tk=D, out_dtype=BF16,
               n_outer=True)
    g_down = _mm(sv["act"], dx2, name=n("g_down"), mode="tn", dims=(DFF, D, T), tm=DFF // 2, tn=D, tk=1024, out_dtype=BF16)
    dgu, g_fcw, g_fcb = _ffn_act_bwd(sv["gu"], P["fcw"], P["fcb"], dact, name=n("d_ffn_act"))
    dh2 = _mm(dgu, W["ffn"], name=n("d_h2"), mode="nt", dims=(T, D, 2 * DFF), tm=1024, tn=D, tk=CW,
              a_spec=pl.BlockSpec((None, 1024, CW), lambda i, j, k: (k // 2, i, k % 2)),
              b_spec=pl.BlockSpec((None, D, CW), lambda i, j, k: (k, j, 0)))
    g_ffn = _mm(sv["h2"], dgu, name=n("g_ffn"), mode="tn", dims=(D, 2 * DFF, T), tm=D, tn=CW, tk=1024,
                b_spec=pl.BlockSpec((None, 1024, CW), lambda i, j, k: (j // 2, k, j % 2)),
                out_spec=pl.BlockSpec((None, D, CW), lambda i, j, k: (j, i, 0)),
                out_shape=jax.ShapeDtypeStruct((N_CHIP, D, CW), BF16), out_dtype=BF16)
    dx1, g_ffn_norm = _rmsnorm_bwd(sv["x1"], P["ffn_norm"], dh2, dx2, name=n("d_norm2"))
    dmixed = _mm(dx1, W["out"], name=n("d_mixed"), mode="nt", dims=(T, D, D), tm=512, tn=D, tk=D)
    g_out = _mm(sv["mixed"], dx1, name=n("g_out"), mode="tn", dims=(D, D, T), tm=D, tn=D, tk=1024, out_dtype=BF16)
    do_raw, dproj, g_gnw = _gdn_post_bwd(sv["o_raw"], sv["proj"], P["gnw"], dmixed, name=n("d_gdn_post"))
    (dqkv_c, dproj, g_alog, g_dtb), carried = _gdn_chunk_bwd(
        sv["qkv_c"], sv["ba"], P["alog"], P["dtb"], sv["s_all"], sv["t_all"], do_raw, dproj, name=n("d_gdn_chunk"), carry=carry)
    dproj, g_gcw = _gdn_prep_bwd(sv["proj"], P["gcw"], dqkv_c, dproj, name=n("d_gdn_prep"))
    dproj, g_sinks = _swa_bwd(sv["proj"], P["sinks"], dmixed, dproj, name=n("d_swa"))
    dh = _mm(dproj, W["all"], name=n("d_h"), mode="nt", dims=(T, D, N_PROJ), tm=512, tn=D, tk=N_PROJ)
    g_all = _mm(sv["h"], dproj, name=n("g_in"), mode="tn", dims=(D, N_PROJ, T), tm=D, tn=N_PROJ, tk=1024, out_dtype=BF16)
    dx, g_attn_norm = _rmsnorm_bwd(sv["x"], P["attn_norm"], dh, dx1, name=n("d_norm1"))
    grads = dict(
        attn_norm=g_attn_norm[0], w_in=g_all, attn_sinks=g_sinks[0, :HQ], gdn_conv_w=g_gcw,
        gdn_a_log=g_alog[0, GH:2 * GH], gdn_dt_bias=g_dtb[0, GH:2 * GH], gdn_norm=g_gnw[0], w_out=g_out,
        ffn_norm=g_ffn_norm[0], w_ffn_in=g_ffn, ffn_conv_w=g_fcw, ffn_conv_b=g_fcb[0], w_down=g_down)
    return dx, grads, carried


def _pos():
    return lax.axis_index("x"), lax.axis_index("y"), lax.axis_index("c")


def _other_chips(x, y):
    return [(1 - x, y), (x, 1 - y), (1 - x, 1 - y)]


def _remote(src, dst, send_sems, recv_sems, k, to):
    return pltpu.make_async_remote_copy(src_ref=src, dst_ref=dst, send_sem=send_sems.at[k], recv_sem=recv_sems.at[k],
                                        device_id=to, device_id_type=MESH)


def _run_carry(carry, *, name):
    ci, co = len(carry.arrays), len(carry.out_shapes)

    def body(*refs):
        sends, recvs = carry.plan(refs[:ci], refs[ci:ci + co], refs[-2], refs[-1])
        for cp in sends:
            cp.start()
        for cp in recvs:
            cp.wait_recv()
        for cp in sends:
            cp.wait_send()

    return list(pl.pallas_call(
        body, name=name, in_specs=[ANY] * ci, out_specs=[ANY] * co, out_shape=carry.out_shapes,
        scratch_shapes=[pltpu.SemaphoreType.DMA((carry.nsem,)), pltpu.SemaphoreType.DMA((carry.nsem,))],
    )(*carry.arrays))


def _chip_index():
    return 2 * lax.axis_index("x") + lax.axis_index("y")


def _gather_carry(shards):
    def plan(srcs, outs, send_sems, recv_sems):
        x, y, c = _pos()
        chip = 2 * x + y
        others = [(k, px, py, 2 * px + py) for k, (px, py) in enumerate(_other_chips(x, y))]
        sends = [_remote(srcs[t], outs[t].at[chip], send_sems, recv_sems, 3 * t + k, (px, py, c))
                 for t in range(len(srcs)) for k, px, py, _ in others]
        recvs = [_remote(srcs[t], outs[t].at[j], send_sems, recv_sems, 3 * t + k, (x, y, c))
                 for t in range(len(srcs)) for k, _, _, j in others]
        return sends, recvs

    return _Carry(shards, [jax.ShapeDtypeStruct((N_CHIP,) + s.shape, s.dtype) for s in shards], 3 * len(shards), plan)


def _fill_own(outs, shards):
    return [lax.dynamic_update_index_in_dim(o, s, _chip_index(), 0) for o, s in zip(outs, shards)]


HBM_SPEC = pl.BlockSpec(memory_space=pltpu.HBM)
SEM_SPEC = pl.BlockSpec(memory_space=pltpu.SEMAPHORE)
DATAFLOW = pltpu.SideEffectType.DATAFLOW_SIDE_EFFECTING


def _gather_start(shards, after, *, name):
    nt, plan = len(shards), _gather_carry(shards).plan

    def body(*refs):
        srcs, lands, send_sems, recv_sems, token = refs[:nt], refs[nt:2 * nt], refs[2 * nt + 1], refs[2 * nt + 2], refs[-1]
        for cp in plan(srcs, lands, send_sems, recv_sems)[0]:
            cp.start()
        token[...] = jnp.zeros_like(token)

    lands = [lax.empty((N_CHIP,) + s.shape, s.dtype) for s in shards]
    hbm = lambda a: pltpu.with_memory_space_constraint(a, pltpu.HBM)
    out = pl.pallas_call(
        body, name=name, in_specs=[HBM_SPEC] * (2 * nt) + [ANY],
        out_specs=[SEM_SPEC, SEM_SPEC] + [HBM_SPEC] * (2 * nt) + [VMEM_SPEC],
        out_shape=[pltpu.SemaphoreType.DMA((3 * nt,)), pltpu.SemaphoreType.DMA((3 * nt,))]
        + [pltpu.HBM(a.shape, a.dtype) for a in list(shards) + lands] + [jax.ShapeDtypeStruct((8, 128), F32)],
        input_output_aliases={t: 2 + t for t in range(2 * nt)},
        compiler_params=pltpu.CompilerParams(has_side_effects=DATAFLOW),
    )(*[hbm(a) for a in list(shards) + lands], after)
    return out[0], out[1], list(out[2:2 + nt]), list(out[2 + nt:2 + 2 * nt]), out[-1]


def _gather_wait(send_sems, recv_sems, shards, lands, after, *, name):
    nt, plan = len(shards), _gather_carry(shards).plan

    def body(*refs):
        srcs, lands_in, send, recv = refs[:nt], refs[nt:2 * nt], refs[2 * nt], refs[2 * nt + 1]
        sends, recvs = plan(srcs, lands_in, send, recv)
        for cp in sends:
            cp.wait_send()
        for cp in recvs:
            cp.wait_recv()

    out = pl.pallas_call(
        body, name=name, in_specs=[HBM_SPEC] * (2 * nt) + [SEM_SPEC, SEM_SPEC, ANY], out_specs=[HBM_SPEC] * (2 * nt),
        out_shape=[pltpu.HBM(a.shape, a.dtype) for a in list(shards) + list(lands)],
        input_output_aliases={t: t for t in range(2 * nt)},
        compiler_params=pltpu.CompilerParams(has_side_effects=DATAFLOW),
    )(*shards, *lands, send_sems, recv_sems, after)
    return _fill_own(out[nt:], out[:nt])


def _ag_small(v, *, name):
    m, n = v.shape

    def body(x_ref, out_ref, red_ref, send_sems, recv_sems, local_sem):
        x, y, c = _pos()
        me, sibling = (x, y, c), (x, y, 1 - c)
        chips = _other_chips(x, y)
        rows = lambda px, py, pc: out_ref.at[pl.ds(pl.multiple_of((4 * px + 2 * py + pc) * m, 8), m), :]
        mine = pltpu.make_async_copy(x_ref, rows(*me), local_sem)
        mine.start()
        first = [_remote(x_ref, rows(*me), send_sems, recv_sems, 0, sibling)]
        first += [_remote(x_ref, rows(*me), send_sems, recv_sems, 1 + k, (*chip, c)) for k, chip in enumerate(chips)]
        for cp in first:
            cp.start()
        passed = [_remote(rows(*chip, c), rows(*chip, c), send_sems, recv_sems, 4 + k, sibling) for k, chip in enumerate(chips)]
        for k, chip in enumerate(chips):
            _remote(rows(*chip, c), rows(*chip, c), send_sems, recv_sems, 1 + k, me).wait_recv()
            passed[k].start()
        _remote(rows(*sibling), rows(*sibling), send_sems, recv_sems, 0, me).wait_recv()
        for k, chip in enumerate(chips):
            _remote(rows(*chip, 1 - c), rows(*chip, 1 - c), send_sems, recv_sems, 4 + k, me).wait_recv()
        for cp in first + passed:
            cp.wait_send()
        mine.wait()
        acc = out_ref[0:m, :]
        for d in range(1, N_DEV):
            acc = acc + out_ref[d * m:(d + 1) * m, :]
        red_ref[...] = acc

    return pl.pallas_call(
        body, name=name, in_specs=[VMEM_SPEC], out_specs=[VMEM_SPEC, VMEM_SPEC],
        out_shape=[jax.ShapeDtypeStruct((N_DEV * m, n), v.dtype), jax.ShapeDtypeStruct((m, n), v.dtype)],
        scratch_shapes=[pltpu.SemaphoreType.DMA((7,)), pltpu.SemaphoreType.DMA((7,)), pltpu.SemaphoreType.DMA],
    )(v)


def _halves(ref, c):
    rh = ref.shape[1] // 2
    return ref.at[:, pl.ds(pl.multiple_of(c * rh, 16), rh), :]


def _rs_swap(gs, *, name):
    nt = len(gs)

    def body(*refs):
        g, theirs = refs[:nt], refs[nt:2 * nt]
        send_sems, recv_sems = refs[2 * nt:]
        x, y, c = _pos()
        swaps = [_remote(_halves(g[t], 1 - c), theirs[t], send_sems, recv_sems, t, (x, y, 1 - c)) for t in range(nt)]
        for cp in swaps:
            cp.start()
        for cp in swaps:
            cp.wait()

    return pl.pallas_call(
        body, name=name, in_specs=[ANY] * nt, out_specs=[ANY] * nt,
        out_shape=[jax.ShapeDtypeStruct((a.shape[0], a.shape[1] // 2, a.shape[2]), a.dtype) for a in gs],
        scratch_shapes=[pltpu.SemaphoreType.DMA((nt,)), pltpu.SemaphoreType.DMA((nt,))],
    )(*gs)


def _exchange_carry(ss):
    def plan(s, out, send_sems, recv_sems):
        x, y, c = _pos()
        chip = 2 * x + y
        others = [(k, px, py, 2 * px + py) for k, (px, py) in enumerate(_other_chips(x, y))]
        sends = [_remote(s[t].at[j], out[t].at[chip], send_sems, recv_sems, 3 * t + k, (px, py, c))
                 for t in range(len(s)) for k, px, py, j in others]
        recvs = [_remote(s[t].at[j], out[t].at[j], send_sems, recv_sems, 3 * t + k, (x, y, c))
                 for t in range(len(s)) for k, _, _, j in others]
        return sends, recvs

    return _Carry(ss, [jax.ShapeDtypeStruct(a.shape, a.dtype) for a in ss], 3 * len(ss), plan)


def _fill_own_slab(outs, ss):
    chip = _chip_index()
    return [lax.dynamic_update_index_in_dim(o, lax.dynamic_index_in_dim(s, chip, 0, keepdims=False), chip, 0)
            for o, s in zip(outs, ss)]


def _rs_join(rs, *, name):
    nt = len(rs)

    def body(*refs):
        r, theirs = refs[:nt], refs[nt:2 * nt]
        send_sems, recv_sems = refs[2 * nt:]
        x, y, c = _pos()
        swaps = [_remote(r[t], theirs[t], send_sems, recv_sems, t, (x, y, 1 - c)) for t in range(nt)]
        for cp in swaps:
            cp.start()
        for cp in swaps:
            cp.wait()

    theirs = pl.pallas_call(
        body, name=name, in_specs=[ANY] * nt, out_specs=[ANY] * nt,
        out_shape=[jax.ShapeDtypeStruct(a.shape, a.dtype) for a in rs],
        scratch_shapes=[pltpu.SemaphoreType.DMA((nt,)), pltpu.SemaphoreType.DMA((nt,))],
    )(*rs)
    first = lax.axis_index("c") == 0
    return [jnp.concatenate([jnp.where(first, a, b), jnp.where(first, b, a)], axis=0) for a, b in zip(rs, theirs)]


def _row_tile(rows, dtype):
    unit = 16 if dtype == BF16 else 8
    if rows <= 512:
        return rows
    return next((t for t in (512, 256, 128, 64, 32, 16, 8) if rows % t == 0 and t % unit == 0), rows)


def _add_pair(a, b, *, name):
    n, rh, cols = a.shape
    tr = _row_tile(rh, BF16)

    def body(a_ref, b_ref, o_ref):
        o_ref[...] = (a_ref[...].astype(F32) + b_ref[...].astype(F32)).astype(BF16)

    blk = pl.BlockSpec((None, tr, cols), lambda j, i: (j, i, 0))
    return pl.pallas_call(
        body, name=name, grid=(n, rh // tr), in_specs=[blk, blk], out_specs=blk,
        out_shape=jax.ShapeDtypeStruct(a.shape, BF16), compiler_params=_cp("parallel", "parallel"),
    )(a, b)


def _sum_chips(b, *, name):
    n, rh, cols = b.shape
    tr = _row_tile(rh, BF16)

    def body(b_ref, o_ref):
        acc = b_ref[0].astype(F32)
        for j in range(1, n):
            acc = acc + b_ref[j].astype(F32)
        o_ref[...] = acc

    return pl.pallas_call(
        body, name=name, grid=(rh // tr,), in_specs=[pl.BlockSpec((n, tr, cols), lambda i: (0, i, 0))],
        out_specs=pl.BlockSpec((tr, cols), lambda i: (i, 0)),
        out_shape=jax.ShapeDtypeStruct((rh, cols), F32), compiler_params=_cp("parallel"),
    )(b)


BIG = ("w_in", "w_out", "w_ffn_in", "w_down")
W_IN_SHARD = (N_MAIN + 2 * GH) // N_CHIP


def _rs_pairs(lg, l):
    g_in = jnp.stack([lg["w_in"][:, j * W_IN_SHARD:(j + 1) * W_IN_SHARD] for j in range(N_CHIP)])
    gs = [g_in, lg["w_out"].reshape(N_CHIP, D // N_CHIP, D), lg["w_ffn_in"], lg["w_down"].reshape(N_CHIP, DFF // N_CHIP, D)]
    c = lax.axis_index("c")
    mine = [lax.dynamic_slice_in_dim(a, c * (a.shape[1] // 2), a.shape[1] // 2, axis=1) for a in gs]
    theirs = _rs_swap(gs, name=f"rs{l}_swap")
    return [_add_pair(a, b, name=f"rs{l}_add_{n}") for a, b, n in zip(mine, theirs, BIG)]


def _rs_finish(got, pairs, l):
    got = _fill_own_slab(got, pairs)
    return _rs_join([_sum_chips(b, name=f"rs{l}_sum_{n}") for b, n in zip(got, BIG)], name=f"rs{l}_join")


def _size(shape):
    n = 1
    for s in shape:
        n *= s
    return n


def _pack_flat(parts, total, dtype):
    flat = jnp.concatenate([p.reshape(-1).astype(dtype) for p in parts])
    return jnp.pad(flat, (0, total - flat.shape[0]))


def _unpack_flat(flat, shapes):
    out, o = [], 0
    for s in shapes:
        out.append(flat[o:o + _size(s)].reshape(s))
        o += _size(s)
    return out


WEIGHTS = ("attn_norm", "w_in", "attn_sinks", "gdn_conv_w", "gdn_a_log", "gdn_dt_bias", "gdn_norm", "w_out", "ffn_norm",
           "w_ffn_in", "ffn_conv_w", "ffn_conv_b", "w_down", "final_norm")
SMALL = {"attn_norm": (DEPTH, D), "attn_sinks": (DEPTH, HQ), "gdn_a_log": (DEPTH, GH), "gdn_dt_bias": (DEPTH, GH),
         "gdn_norm": (DEPTH, GD), "ffn_norm": (DEPTH, D), "ffn_conv_b": (DEPTH, DFF), "final_norm": (D,)}
CONV_FULL = {"gdn_conv_w": (DEPTH, GK, 3 * GH * GD), "ffn_conv_w": (DEPTH, FK, DFF)}
CONV_SHARD = {"gdn_conv_w": (DEPTH, GK, 3 * GH * GD // N_CHIP), "ffn_conv_w": (DEPTH, FK, DFF // N_CHIP)}
CONV_ROWS, SMALLG_ROWS, SMALLW_ROWS = 64, 320, 144


def kernel(x, attn_norm, w_in, attn_sinks, gdn_conv_w, gdn_a_log, gdn_dt_bias, gdn_norm, w_out, ffn_norm, w_ffn_in, ffn_conv_w, ffn_conv_b, w_down, final_norm, loss_target, m_attn_norm, m_w_in, m_attn_sinks, m_gdn_conv_w, m_gdn_a_log, m_gdn_dt_bias, m_gdn_norm, m_w_out, m_ffn_norm, m_w_ffn_in, m_ffn_conv_w, m_ffn_conv_b, m_w_down, m_final_norm, v_attn_norm, v_w_in, v_attn_sinks, v_gdn_conv_w, v_gdn_a_log, v_gdn_dt_bias, v_gdn_norm, v_w_out, v_ffn_norm, v_w_ffn_in, v_ffn_conv_w, v_ffn_conv_b, v_w_down, v_final_norm):
    w = dict(zip(WEIGHTS, (attn_norm, w_in, attn_sinks, gdn_conv_w, gdn_a_log, gdn_dt_bias, gdn_norm, w_out, ffn_norm,
                           w_ffn_in, ffn_conv_w, ffn_conv_b, w_down, final_norm)))
    m = dict(zip(WEIGHTS, (m_attn_norm, m_w_in, m_attn_sinks, m_gdn_conv_w, m_gdn_a_log, m_gdn_dt_bias, m_gdn_norm, m_w_out,
                           m_ffn_norm, m_w_ffn_in, m_ffn_conv_w, m_ffn_conv_b, m_w_down, m_final_norm)))
    v = dict(zip(WEIGHTS, (v_attn_norm, v_w_in, v_attn_sinks, v_gdn_conv_w, v_gdn_a_log, v_gdn_dt_bias, v_gdn_norm, v_w_out,
                           v_ffn_norm, v_w_ffn_in, v_ffn_conv_w, v_ffn_conv_b, v_w_down, v_final_norm)))
    cx, cy, _ = _pos()
    chip = 2 * cx + cy

    cpack = _pack_flat([w[n] for n in CONV_SHARD], CONV_ROWS * 128, F32).reshape(CONV_ROWS, 128)
    cgath, _ = _ag_small(cpack, name="gather_conv_w")
    cgath = cgath.reshape(N_DEV, CONV_ROWS * 128)
    cshards = [_unpack_flat(cgath[2 * j], list(CONV_SHARD.values())) for j in range(N_CHIP)]
    conv = {n: jnp.concatenate([cshards[j][i] for j in range(N_CHIP)], axis=2) for i, n in enumerate(CONV_SHARD)}

    Ps = [_layer_params(attn_norm[l], attn_sinks[l], conv["gdn_conv_w"][l], gdn_a_log[l], gdn_dt_bias[l], gdn_norm[l],
                        ffn_norm[l], conv["ffn_conv_w"][l], ffn_conv_b[l]) for l in range(DEPTH)]

    wb = {n: [w[n][l].astype(BF16) for l in range(DEPTH)] for n in BIG}
    mixer_w = lambda g_in, g_out: dict(all=_layer_weights(jnp.concatenate(list(g_in), axis=1)), out=g_out.reshape(D, D))
    ffn_w = lambda g_ffn, g_down: dict(ffn=g_ffn, down=g_down.reshape(DFF, D))
    groups = [[wb[a][l], wb[b][l]] for l in range(DEPTH) for a, b in (("w_in", "w_out"), ("w_ffn_in", "w_down"))]
    def after_start(P, key, started):
        return {**P, key: P[key] + started[4][:1, :1]}

    g0 = _gather_start(groups[0], x, name="gather0_start")
    got0 = _gather_wait(*g0[:4], g0[4], name="gather0_wait")
    g1 = _gather_start(groups[1], got0[1], name="gather1_start")
    Wm0 = mixer_w(*got0)
    h, sv0, _ = _mixer_fwd(x[0], Wm0, after_start(Ps[0], "attn_norm", g1), 0)
    got1 = _gather_wait(*g1[:4], h, name="gather1_wait")
    g2 = _gather_start(groups[2], got1[1], name="gather2_start")
    Wf0 = ffn_w(*got1)
    h, sv0f, _ = _ffn_fwd(h, Wf0, after_start(Ps[0], "ffn_norm", g2), 0)
    got2 = _gather_wait(*g2[:4], h, name="gather2_wait")
    g3 = _gather_start(groups[3], got2[1], name="gather3_start")
    Wm1 = mixer_w(*got2)
    h, sv1, _ = _mixer_fwd(h, Wm1, after_start(Ps[1], "attn_norm", g3), 1)
    Wf1 = ffn_w(*_gather_wait(*g3[:4], h, name="gather3_wait"))
    h, sv1f, _ = _ffn_fwd(h, Wf1, Ps[1], 1)
    dx, g_final, loss_part = _loss_head(h, final_norm[None], loss_target[0], name="loss_head")

    lg, reduced = [None] * DEPTH, [None] * DEPTH
    dx, lg[1], _ = _layer_bwd(dx, {**sv1, **sv1f}, {**Wm1, **Wf1}, Ps[1], 1)
    pairs1 = _rs_pairs(lg[1], 1)
    dx, lg[0], got1 = _layer_bwd(dx, {**sv0, **sv0f}, {**Wm0, **Wf0}, Ps[0], 0, carry=_exchange_carry(pairs1))
    reduced[1] = _rs_finish(got1, pairs1, 1)
    pairs0 = _rs_pairs(lg[0], 0)
    reduced[0] = _rs_finish(_run_carry(_exchange_carry(pairs0), name="rs0_exchange"), pairs0, 0)
    grad_x = dx[None]
    stacked = lambda n: jnp.stack([lg[l][n] for l in range(DEPTH)])

    small_parts = [g_final[0] if n == "final_norm" else stacked(n) for n in SMALL] + [stacked(n) for n in CONV_FULL]
    svec = _pack_flat(small_parts + [loss_part[0, :1]], SMALLG_ROWS * 128, F32).reshape(SMALLG_ROWS, 128)
    _, sred = _ag_small(svec, name="reduce_small")
    small_g = _unpack_flat(sred.reshape(-1), list(SMALL.values()) + list(CONV_FULL.values()) + [(1,)])
    g = dict(zip(list(SMALL) + list(CONV_FULL), small_g[:-1]))
    loss = small_g[-1][0]
    for n in CONV_FULL:
        wd = CONV_SHARD[n][2]
        g[n] = lax.dynamic_slice_in_dim(g[n], chip * wd, wd, axis=2)

    g.update({n: jnp.stack([reduced[l][i] for l in range(DEPTH)]) for i, n in enumerate(BIG)})

    delta, new_m, new_v = {}, {}, {}
    for n in BIG:
        delta[n], new_m[n], new_v[n] = _adamw(w[n], g[n], m[n], v[n], name=f"adamw_{n}")
    small_names = list(SMALL) + list(CONV_SHARD)
    small_shapes = list(SMALL.values()) + list(CONV_SHARD.values())
    packed = [_pack_flat([t[n] for n in small_names], SMALLW_ROWS * 128, F32).reshape(SMALLW_ROWS, 128) for t in (w, g, m, v)]
    for res, out in zip(_adamw(*packed, name="adamw_small"), (delta, new_m, new_v)):
        out.update(zip(small_names, _unpack_flat(res.reshape(-1), small_shapes)))

    return (loss, grad_x, *[g[n] for n in WEIGHTS], *[delta[n] for n in WEIGHTS], *[new_m[n] for n in WEIGHTS],
            *[new_v[n] for n in WEIGHTS])
```

```python
import functools

import jax
import jax.numpy as jnp
import numpy as np
from jax import lax
from jax.experimental import pallas as pl
from jax.experimental.pallas import tpu as pltpu

F32, BF16 = jnp.float32, jnp.bfloat16
HIGHEST = lax.Precision.HIGHEST
MESH = pl.DeviceIdType.MESH

D = 1024
T = 4096
DEPTH = 2
HQ, HKV, DH, WIN = 8, 2, 64, 128
GH, GD, GC, GK = 4, 128, 64, 4
DFF, FK = 2816, 3
EPS = 1e-6
N_MAIN = 2816
N_PROJ = N_MAIN + 128
COL_K, COL_V, COL_G, COL_Z, COL_BA = 4, 5, 6, 18, 22
N_CHUNK = T // GC
N_DEV, N_CHIP = 8, 4
FFN_CW = 2 * DFF // N_CHIP
LR, B1, B2, AEPS, WD, STEP = 0.001, 0.9, 0.999, 1e-08, 0.01, 10

VMEM_LIMIT = 56 * 1024 * 1024
ANY = pl.BlockSpec(memory_space=pl.ANY)
VMEM_SPEC = pl.BlockSpec(memory_space=pltpu.VMEM)


def _cp(*sem):
    return pltpu.CompilerParams(dimension_semantics=sem if sem else None, vmem_limit_bytes=VMEM_LIMIT)


class _Carry:
    def __init__(self, arrays, out_shapes, nsem, plan):
        self.arrays, self.out_shapes, self.nsem, self.plan = list(arrays), list(out_shapes), nsem, plan


def _seq_call(body, carry, *, name, steps, in_specs, out_specs, out_shape, args, scratch_shapes=(), aliases=None):
    in_specs, out_specs, out_shape, scratch = list(in_specs), list(out_specs), list(out_shape), list(scratch_shapes)
    n_in, n_out, n_scr = len(in_specs), len(out_specs), len(scratch)
    if carry is None:
        fn = body
    else:
        ci, co = len(carry.arrays), len(carry.out_shapes)

        def fn(*refs):
            ins, cins = refs[:n_in], refs[n_in:n_in + ci]
            outs, couts = refs[n_in + ci:n_in + ci + n_out], refs[n_in + ci + n_out:n_in + ci + n_out + co]
            scr, (ssem, rsem) = refs[n_in + ci + n_out + co:-2], refs[-2:]
            sends, recvs = carry.plan(cins, couts, ssem, rsem)

            @pl.when(pl.program_id(0) == 0)
            def _():
                for cp in sends:
                    cp.start()

            body(*ins, *outs, *scr)

            @pl.when(pl.program_id(0) == steps - 1)
            def _():
                for cp in recvs:
                    cp.wait_recv()
                for cp in sends:
                    cp.wait_send()

        in_specs += [ANY] * ci
        out_specs += [ANY] * co
        out_shape += carry.out_shapes
        scratch += [pltpu.SemaphoreType.DMA((carry.nsem,)), pltpu.SemaphoreType.DMA((carry.nsem,))]
        args = list(args) + carry.arrays
    outs = pl.pallas_call(
        fn, name=name, grid=(steps,), in_specs=in_specs, out_specs=out_specs, out_shape=out_shape,
        scratch_shapes=scratch, input_output_aliases=aliases or {}, compiler_params=_cp("arbitrary"),
    )(*args)
    return list(outs[:n_out]), list(outs[n_out:])


def _dot(a, b):
    return jnp.dot(a.astype(BF16), b.astype(BF16), preferred_element_type=F32)


def _dot_nt(a, b):
    return lax.dot_general(a.astype(BF16), b.astype(BF16), (((1,), (1,)), ((), ())), preferred_element_type=F32)


def _dot_tn(a, b):
    return lax.dot_general(a.astype(BF16), b.astype(BF16), (((0,), (0,)), ((), ())), preferred_element_type=F32)


def _iota(shape, dim):
    return lax.broadcasted_iota(jnp.int32, shape, dim)


def _col(x, idx):
    return jnp.sum(jnp.where(_iota(x.shape, 1) == idx, x, 0.0), axis=1, keepdims=True)


def _silu(y):
    return y * jax.nn.sigmoid(y)


_MM_DN = {"nn": (((1,), (0,)), ((), ())), "nt": (((1,), (1,)), ((), ())), "tn": (((0,), (0,)), ((), ()))}


def _mm(a, b, *, name, mode, dims, tm, tn, tk, res=None, out_dtype=F32, a_spec=None, b_spec=None, out_spec=None,
        out_shape=None, n_outer=False):
    M, N, K = dims
    assert M % tm == 0 and N % tn == 0 and K % tk == 0, (name, dims)
    nk = K // tk
    dn = _MM_DN[mode]

    def body(*refs):
        if res is None:
            a_ref, b_ref, o_ref, acc = refs
        else:
            a_ref, b_ref, r_ref, o_ref, acc = refs
        k = pl.program_id(2)

        @pl.when(k == 0)
        def _():
            acc[...] = jnp.zeros_like(acc)

        acc[...] += lax.dot_general(a_ref[...].astype(BF16), b_ref[...].astype(BF16), dn, preferred_element_type=F32)

        @pl.when(k == nk - 1)
        def _():
            r = acc[...]
            if res is not None:
                r = r + r_ref[...]
            o_ref[...] = r.astype(out_dtype)

    if a_spec is None:
        a_spec = pl.BlockSpec((tk, tm), lambda i, j, k: (k, i)) if mode == "tn" else pl.BlockSpec((tm, tk), lambda i, j, k: (i, k))
    if b_spec is None:
        b_spec = pl.BlockSpec((tn, tk), lambda i, j, k: (j, k)) if mode == "nt" else pl.BlockSpec((tk, tn), lambda i, j, k: (k, j))
    in_specs, args = [a_spec, b_spec], [a, b]
    if res is not None:
        in_specs.append(pl.BlockSpec((tm, tn), lambda i, j, k: (i, j)))
        args.append(res)
    out_spec = out_spec or pl.BlockSpec((tm, tn), lambda i, j, k: (i, j))
    grid = (M // tm, N // tn, nk)
    if n_outer:
        swap = lambda s: pl.BlockSpec(s.block_shape, lambda j, i, k, f=s.index_map: f(i, j, k))
        in_specs, out_spec, grid = [swap(s) for s in in_specs], swap(out_spec), (N // tn, M // tm, nk)
    return pl.pallas_call(
        body, name=name, grid=grid, in_specs=in_specs, out_specs=out_spec,
        out_shape=out_shape or jax.ShapeDtypeStruct((M, N), out_dtype),
        scratch_shapes=[pltpu.VMEM((tm, tn), F32)],
        compiler_params=_cp("parallel", "parallel", "arbitrary"),
    )(*args)


def _rms(x, w):
    return x * lax.rsqrt(jnp.mean(x * x, axis=-1, keepdims=True) + EPS) * w


def _rmsnorm_fwd(x, w, *, name, tm=512):
    def body(x_ref, w_ref, o_ref):
        o_ref[...] = _rms(x_ref[...], w_ref[...]).astype(BF16)

    return pl.pallas_call(
        body, name=name, grid=(x.shape[0] // tm,),
        in_specs=[pl.BlockSpec((tm, D), lambda i: (i, 0)), pl.BlockSpec((1, D), lambda i: (0, 0))],
        out_specs=pl.BlockSpec((tm, D), lambda i: (i, 0)),
        out_shape=jax.ShapeDtypeStruct(x.shape, BF16), compiler_params=_cp("parallel"),
    )(x, w)


def _rmsnorm_bwd(x, w, dh, dres, *, name, tm=512):
    def body(x_ref, w_ref, dh_ref, dr_ref, dx_ref, dw_ref):
        _, vjp = jax.vjp(_rms, x_ref[...], w_ref[...])
        dx, dw = vjp(dh_ref[...])
        dx_ref[...] = dx + dr_ref[...]

        @pl.when(pl.program_id(0) == 0)
        def _():
            dw_ref[...] = jnp.zeros_like(dw_ref)

        dw_ref[...] += dw

    row = pl.BlockSpec((tm, D), lambda i: (i, 0))
    vec = pl.BlockSpec((1, D), lambda i: (0, 0))
    return pl.pallas_call(
        body, name=name, grid=(x.shape[0] // tm,), in_specs=[row, vec, row, row], out_specs=[row, vec],
        out_shape=[jax.ShapeDtypeStruct(x.shape, F32), jax.ShapeDtypeStruct((1, D), F32)],
        compiler_params=_cp("arbitrary"),
    )(x, w, dh, dres)


def _loss_head(x, w, tgt, *, name, tm=512):
    def f(xv, wv, tv):
        err = _rms(xv, wv) - tv
        per_row = jnp.sum(err * err, axis=1, keepdims=True) * (0.5 / D)
        return jnp.sum(per_row, axis=0, keepdims=True)

    def body(x_ref, w_ref, t_ref, dx_ref, dw_ref, loss_ref):
        tv = t_ref[...]
        loss, vjp = jax.vjp(lambda xv, wv: f(xv, wv, tv), x_ref[...], w_ref[...])
        dx, dw = vjp(jnp.ones((1, 1), F32))
        dx_ref[...] = dx

        @pl.when(pl.program_id(0) == 0)
        def _():
            dw_ref[...] = jnp.zeros_like(dw_ref)
            loss_ref[...] = jnp.zeros_like(loss_ref)

        dw_ref[...] += dw
        loss_ref[...] += jnp.broadcast_to(loss, loss_ref.shape)

    row = pl.BlockSpec((tm, D), lambda i: (i, 0))
    vec = pl.BlockSpec((1, D), lambda i: (0, 0))
    return pl.pallas_call(
        body, name=name, grid=(x.shape[0] // tm,), in_specs=[row, vec, row],
        out_specs=[row, vec, pl.BlockSpec((1, 128), lambda i: (0, 0))],
        out_shape=[jax.ShapeDtypeStruct(x.shape, F32), jax.ShapeDtypeStruct((1, D), F32),
                   jax.ShapeDtypeStruct((1, 128), F32)],
        compiler_params=_cp("arbitrary"),
    )(x, w, tgt)


def _swa_bias():
    G = HQ // HKV
    r, c = np.arange(G * WIN)[:, None], np.arange(2 * WIN)[None, :]
    rel = (r % WIN) + WIN - c
    band = (rel >= 0) & (rel < WIN)
    out = np.empty((2, HKV, G * WIN, 2 * WIN), np.float32)
    for h2 in range(HKV):
        slope = 2.0 ** (-8.0 * (h2 * G + r // WIN + 1) / HQ)
        out[0, h2] = np.where(band & (c >= WIN), -slope * rel, -1e30)
        out[1, h2] = np.where(band, -slope * rel, -1e30)
    return jnp.asarray(out)


def _swa_group(qg, k2, v2, sk, bias, h2):
    s = _dot_nt(qg, k2) * (DH ** -0.5) + bias
    g1 = _iota((s.shape[0], 1), 0) >> 7
    sink = jnp.zeros((s.shape[0], 1), F32)
    for gi in range(HQ // HKV):
        sink = jnp.where(g1 == gi, _col(sk, h2 * (HQ // HKV) + gi), sink)
    m = lax.stop_gradient(jnp.maximum(jnp.max(s, axis=1, keepdims=True), sink))
    p = jnp.exp(s - m)
    den = jnp.sum(p, axis=1, keepdims=True) + jnp.exp(sink - m)
    return _dot(p * (1.0 / den), v2)


def _swa_split_q(q, h2):
    G = HQ // HKV
    return jnp.concatenate([q[:, (h2 * G + g) * DH:(h2 * G + g + 1) * DH] for g in range(G)], axis=0)


def _swa_merge_q(parts):
    G = HQ // HKV
    return jnp.concatenate([parts[h2][g * WIN:(g + 1) * WIN] for h2 in range(HKV) for g in range(G)], axis=1)


def _swa_specs():
    prev = lambda i: jnp.maximum(jnp.minimum(i, T // WIN - 1) - 1, 0)
    cur = lambda i: jnp.minimum(i, T // WIN - 1)
    return [
        pl.BlockSpec((WIN, HQ * DH), lambda i: (cur(i), 0)),
        pl.BlockSpec((WIN, 128), lambda i: (prev(i), COL_K)),
        pl.BlockSpec((WIN, 128), lambda i: (cur(i), COL_K)),
        pl.BlockSpec((WIN, 128), lambda i: (prev(i), COL_V)),
        pl.BlockSpec((WIN, 128), lambda i: (cur(i), COL_V)),
        pl.BlockSpec((1, 128), lambda i: (0, 0)),
        pl.BlockSpec((None, HKV, (HQ // HKV) * WIN, 2 * WIN), lambda i: (jnp.minimum(i, 1), 0, 0, 0)),
    ]


def _swa_fwd(proj, sinks, *, name):
    def body(q_ref, kp_ref, kc_ref, vp_ref, vc_ref, sk_ref, bias_ref, o_ref):
        q, sk = q_ref[...].astype(F32), sk_ref[...]
        outs = []
        for h2 in range(HKV):
            sl = slice(h2 * DH, (h2 + 1) * DH)
            k2 = jnp.concatenate([kp_ref[:, sl], kc_ref[:, sl]], axis=0).astype(F32)
            v2 = jnp.concatenate([vp_ref[:, sl], vc_ref[:, sl]], axis=0).astype(F32)
            outs.append(_swa_group(_swa_split_q(q, h2), k2, v2, sk, bias_ref[h2], h2))
        o_ref[...] = _swa_merge_q(outs).astype(BF16)

    return pl.pallas_call(
        body, name=name, grid=(T // WIN,), in_specs=_swa_specs(),
        out_specs=pl.BlockSpec((WIN, HQ * DH), lambda i: (i, 0)),
        out_shape=jax.ShapeDtypeStruct((T, HQ * DH + GH * GD), BF16), compiler_params=_cp("parallel"),
    )(proj, proj, proj, proj, proj, sinks, _swa_bias())


def _swa_bwd(proj, sinks, do, dproj, *, name):
    NB = T // WIN
    QW = HQ * DH

    def body(q_ref, kp_ref, kc_ref, vp_ref, vc_ref, sk_ref, bias_ref, do_ref, _, out_ref, dsk_ref, cq, ck, cv):
        i = pl.program_id(0)

        @pl.when(i == 0)
        def _():
            cq[...] = jnp.zeros_like(cq)
            ck[...] = jnp.zeros_like(ck)
            cv[...] = jnp.zeros_like(cv)
            dsk_ref[...] = jnp.zeros_like(dsk_ref)

        @pl.when(i < NB)
        def _():
            q, sk, dov = q_ref[...].astype(F32), sk_ref[...], do_ref[...].astype(F32)
            dqs, dkp, dkc, dvp, dvc = [], [], [], [], []
            dsk = jnp.zeros_like(sk)
            for h2 in range(HKV):
                sl = slice(h2 * DH, (h2 + 1) * DH)
                k2 = jnp.concatenate([kp_ref[:, sl], kc_ref[:, sl]], axis=0).astype(F32)
                v2 = jnp.concatenate([vp_ref[:, sl], vc_ref[:, sl]], axis=0).astype(F32)
                _, vjp = jax.vjp(functools.partial(_swa_group, bias=bias_ref[h2], h2=h2), _swa_split_q(q, h2), k2, v2, sk)
                dqg, dk2, dv2, dsk_h = vjp(_swa_split_q(dov, h2))
                dqs.append(dqg)
                dkp.append(dk2[:WIN]); dkc.append(dk2[WIN:])
                dvp.append(dv2[:WIN]); dvc.append(dv2[WIN:])
                dsk = dsk + dsk_h
            out_ref[:, :QW] = cq[...].astype(BF16)
            out_ref[:, QW:QW + 128] = (ck[...] + jnp.concatenate(dkp, axis=1)).astype(BF16)
            out_ref[:, QW + 128:] = (cv[...] + jnp.concatenate(dvp, axis=1)).astype(BF16)
            cq[...] = _swa_merge_q(dqs)
            ck[...] = jnp.concatenate(dkc, axis=1)
            cv[...] = jnp.concatenate(dvc, axis=1)
            dsk_ref[...] += dsk

        @pl.when(i == NB)
        def _():
            out_ref[:, :QW] = cq[...].astype(BF16)
            out_ref[:, QW:QW + 128] = ck[...].astype(BF16)
            out_ref[:, QW + 128:] = cv[...].astype(BF16)

    qblk = pl.BlockSpec((WIN, QW), lambda i: (jnp.minimum(i, NB - 1), 0))
    return pl.pallas_call(
        body, name=name, grid=(NB + 1,), in_specs=_swa_specs() + [qblk, ANY],
        out_specs=[pl.BlockSpec((WIN, QW + 256), lambda i: (jnp.maximum(i - 1, 0), 0)), pl.BlockSpec((1, 128), lambda i: (0, 0))],
        out_shape=[jax.ShapeDtypeStruct(dproj.shape, dproj.dtype), jax.ShapeDtypeStruct((1, 128), F32)],
        scratch_shapes=[pltpu.VMEM((WIN, QW), F32), pltpu.VMEM((WIN, 128), F32), pltpu.VMEM((WIN, 128), F32)],
        input_output_aliases={8: 0}, compiler_params=_cp("arbitrary"),
    )(proj, proj, proj, proj, proj, sinks, _swa_bias(), do, dproj)


RC = 256
HALO = 8


def _load_ext(ref, c):
    nch = T // RC
    r0 = pl.multiple_of(c * RC, RC)
    p0 = pl.multiple_of(jnp.maximum(r0 - HALO, 0), HALO)
    n0 = pl.multiple_of(jnp.minimum(r0 + RC, T - HALO), HALO)
    prev = jnp.where(c > 0, ref[pl.ds(p0, HALO), :].astype(F32), 0.0)
    nxt = jnp.where(c < nch - 1, ref[pl.ds(n0, HALO), :].astype(F32), 0.0)
    return jnp.concatenate([prev, ref[pl.ds(r0, RC), :].astype(F32), nxt], axis=0)


def _conv_ext(xe, w, K):
    y = w[K - 1:K, :] * xe
    for s in range(1, K):
        y = y + w[K - 1 - s:K - s, :] * pltpu.roll(xe, s, 0)
    return y


def _conv_bwd_ext(xe, dye, w, K, dw_ref):
    n = xe.shape[0]
    own = slice(HALO, HALO + RC)
    dx = w[K - 1:K, :] * dye
    dw_ref[K - 1:K, :] += jnp.sum(dye[own] * xe[own], axis=0, keepdims=True)
    for s in range(1, K):
        dx = dx + w[K - 1 - s:K - s, :] * pltpu.roll(dye, n - s, 0)
        dw_ref[K - 1 - s:K - s, :] += jnp.sum(dye[own] * pltpu.roll(xe, s, 0)[own], axis=0, keepdims=True)
    return dx[own]


def _gdn_post_conv(y, is_qk):
    a = _silu(y)
    nrm = a * lax.rsqrt(jnp.sum(a * a, axis=1, keepdims=True) + EPS)
    return jnp.where(is_qk, nrm, a)


def _gdn_prep_fwd(proj, conv_w, *, name):
    nblk = 3 * GH

    def body(x_ref, w_ref, o_ref):
        is_qk = pl.program_id(0) < 2 * GH
        w = w_ref[...]

        def chunk(c, carry):
            y = _conv_ext(_load_ext(x_ref, c), w, GK)[HALO:HALO + RC]
            o_ref[pl.ds(pl.multiple_of(c * RC, RC), RC), :] = _gdn_post_conv(y, is_qk)
            return carry

        lax.fori_loop(0, T // RC, chunk, 0)

    return pl.pallas_call(
        body, name=name, grid=(nblk,),
        in_specs=[pl.BlockSpec((T, 128), lambda j: (0, COL_G + j)), pl.BlockSpec((GK, 128), lambda j: (0, j))],
        out_specs=pl.BlockSpec((T, 128), lambda j: (0, j)),
        out_shape=jax.ShapeDtypeStruct((T, nblk * 128), F32), compiler_params=_cp("parallel"),
    )(proj, conv_w)


def _gdn_prep_bwd(proj, conv_w, dout, dproj, *, name):
    nblk = 3 * GH

    def body(x_ref, w_ref, d_ref, _, dx_ref, dw_ref):
        is_qk = pl.program_id(0) < 2 * GH
        w = w_ref[...]
        dw_ref[...] = jnp.zeros_like(dw_ref)

        def chunk(c, carry):
            xe = _load_ext(x_ref, c)
            _, vjp = jax.vjp(lambda y: _gdn_post_conv(y, is_qk), _conv_ext(xe, w, GK))
            (dye,) = vjp(_load_ext(d_ref, c))
            dx_ref[pl.ds(pl.multiple_of(c * RC, RC), RC), :] = _conv_bwd_ext(xe, dye, w, GK, dw_ref).astype(BF16)
            return carry

        lax.fori_loop(0, T // RC, chunk, 0)

    return pl.pallas_call(
        body, name=name, grid=(nblk,),
        in_specs=[pl.BlockSpec((T, 128), lambda j: (0, COL_G + j)), pl.BlockSpec((GK, 128), lambda j: (0, j)),
                  pl.BlockSpec((T, 128), lambda j: (0, j)), ANY],
        out_specs=[pl.BlockSpec((T, 128), lambda j: (0, COL_G + j)), pl.BlockSpec((GK, 128), lambda j: (0, j))],
        out_shape=[jax.ShapeDtypeStruct(dproj.shape, dproj.dtype), jax.ShapeDtypeStruct((GK, nblk * 128), F32)],
        input_output_aliases={3: 0}, compiler_params=_cp("parallel"),
    )(proj, conv_w, dout, dproj)


def _ffn_post_conv(y, b, up):
    return _silu(y + b) * up


def _ffn_act_fwd(gu, conv_w, conv_b, *, name, carry=None):
    nblk = DFF // 128

    def body(g_ref, u_ref, w_ref, b_ref, o_ref):
        w, b = w_ref[...], b_ref[...]

        def chunk(c, carry):
            rows = pl.ds(pl.multiple_of(c * RC, RC), RC)
            y = _conv_ext(_load_ext(g_ref, c), w, FK)[HALO:HALO + RC]
            o_ref[rows, :] = _ffn_post_conv(y, b, u_ref[rows, :].astype(F32)).astype(BF16)
            return carry

        lax.fori_loop(0, T // RC, chunk, 0)

    (act,), carried = _seq_call(
        body, carry, name=name, steps=nblk,
        in_specs=[pl.BlockSpec((T, 128), lambda j: (0, j)), pl.BlockSpec((T, 128), lambda j: (0, nblk + j)),
                  pl.BlockSpec((FK, 128), lambda j: (0, j)), pl.BlockSpec((1, 128), lambda j: (0, j))],
        out_specs=[pl.BlockSpec((T, 128), lambda j: (0, j))], out_shape=[jax.ShapeDtypeStruct((T, DFF), BF16)],
        args=(gu, gu, conv_w, conv_b))
    return act, carried


def _ffn_act_bwd(gu, conv_w, conv_b, dact, *, name):
    nblk = DFF // 128

    def body(g_ref, u_ref, w_ref, b_ref, d_ref, dgu_ref, dw_ref, db_ref):
        dg_ref, du_ref = dgu_ref.at[0], dgu_ref.at[1]
        w, b = w_ref[...], b_ref[...]
        dw_ref[...] = jnp.zeros_like(dw_ref)
        db_ref[...] = jnp.zeros_like(db_ref)

        def chunk(c, carry):
            rows = pl.ds(pl.multiple_of(c * RC, RC), RC)
            xe = _load_ext(g_ref, c)
            ue = _load_ext(u_ref, c)
            _, vjp = jax.vjp(_ffn_post_conv, _conv_ext(xe, w, FK), b, ue)
            dye, db, due = vjp(_load_ext(d_ref, c))
            du_ref[rows, :] = due[HALO:HALO + RC].astype(BF16)
            db_ref[...] += jnp.sum(dye[HALO:HALO + RC], axis=0, keepdims=True)
            dg_ref[rows, :] = _conv_bwd_ext(xe, dye, w, FK, dw_ref).astype(BF16)
            return carry

        lax.fori_loop(0, T // RC, chunk, 0)

    col = pl.BlockSpec((T, 128), lambda j: (0, j))
    return pl.pallas_call(
        body, name=name, grid=(nblk,),
        in_specs=[col, pl.BlockSpec((T, 128), lambda j: (0, nblk + j)), pl.BlockSpec((FK, 128), lambda j: (0, j)),
                  pl.BlockSpec((1, 128), lambda j: (0, j)), col],
        out_specs=[pl.BlockSpec((2, T, 128), lambda j: (0, 0, j)), pl.BlockSpec((FK, 128), lambda j: (0, j)),
                   pl.BlockSpec((1, 128), lambda j: (0, j))],
        out_shape=[jax.ShapeDtypeStruct((2, T, DFF), BF16), jax.ShapeDtypeStruct((FK, DFF), F32),
                   jax.ShapeDtypeStruct((1, DFF), F32)],
        compiler_params=_cp("parallel"),
    )(gu, gu, conv_w, conv_b, dact)


def _gdn_gates(ba, alog, dtb):
    beta = jax.nn.sigmoid(ba)
    g = -jnp.exp(alog) * jax.nn.softplus(ba + dtb)
    tril = (_iota((GC, GC), 0) >= _iota((GC, GC), 1)).astype(F32)
    return beta, jnp.dot(tril, g, precision=HIGHEST, preferred_element_type=F32)


def _hmap(f, *lists):
    return [f(*xs) for xs in zip(*lists)]


def _gdn_cols(beta_all, gc_all):
    return [_col(beta_all, h) for h in range(GH)], [_col(gc_all, GH + h) for h in range(GH)]


def _gdn_decay(Gcs):
    r, c = _iota((GC, GC), 0), _iota((GC, GC), 1)
    eye, ones = (r == c).astype(F32), jnp.ones((GC, GC), F32)
    grows = _hmap(lambda G: jnp.dot(ones, eye * G, precision=HIGHEST, preferred_element_type=F32), Gcs)
    return _hmap(lambda G, grow: jnp.exp(jnp.where(r >= c, G - grow, -1e30)), Gcs, grows)


def _gdn_pre(beta_all, gc_all):
    betas, Gcs = _gdn_cols(beta_all, gc_all)
    return betas, Gcs, _gdn_decay(Gcs)


def _gdn_A(ks, betas, decays):
    strict = _iota((GC, GC), 0) > _iota((GC, GC), 1)
    kk = _hmap(lambda k, b: _dot_nt(k * b, k), ks, betas)
    return _hmap(lambda a, d: jnp.where(strict, a * d, 0.0), kk, decays)


def _tri_inv(As):
    eye = (_iota((GC, GC), 0) == _iota((GC, GC), 1)).astype(F32)
    Tms, Ps = [eye - A for A in As], As
    for _ in range(GC.bit_length() - 2):
        Ps = _hmap(lambda P: _dot(P, P), Ps)
        Tms = _hmap(lambda Tm, P: Tm + _dot(Tm, P), Tms, Ps)
    return Tms


def _gdn_chunk(qs, ks, vs, betas, Gcs, decays, Ss, Tms):
    eGs = _hmap(jnp.exp, Gcs)
    us = _hmap(lambda Tm, v, b: _dot(Tm, v * b), Tms, vs, betas)
    ws = _hmap(lambda Tm, k, b, eG: _dot(Tm, k * b * eG), Tms, ks, betas, eGs)
    qss = [q * (GD ** -0.5) for q in qs]
    qks = _hmap(lambda q, k, d: _dot_nt(q, k) * d, qss, ks, decays)
    glasts = [jnp.sum(jnp.where(_iota(G.shape, 0) == GC - 1, G, 0.0), axis=0, keepdims=True) for G in Gcs]
    kds = _hmap(lambda k, gl, G: k * jnp.exp(gl - G), ks, glasts, Gcs)
    v_news = _hmap(lambda u, w, S: u - _dot(w, S), us, ws, Ss)
    qS = _hmap(lambda q, eG, S: _dot(q * eG, S), qss, eGs, Ss)
    os = _hmap(lambda a, qk, vn: a + _dot(qk, vn), qS, qks, v_news)
    S_news = _hmap(lambda S, gl, kd, vn: S * jnp.exp(gl) + _dot_tn(kd, vn), Ss, glasts, kds, v_news)
    return os, S_news


def _gdn_chunk_fwd(qkv_c, ba, alog, dtb, *, name, carry=None):
    W3 = 3 * GH * GD

    def body(x_ref, ba_ref, al_ref, dt_ref, o_ref, s_ref, t_ref, S):
        @pl.when(pl.program_id(0) == 0)
        def _():
            S[...] = jnp.zeros_like(S)

        beta_all, gc_all = _gdn_gates(ba_ref[...], al_ref[...], dt_ref[...])
        qs, ks, vs = ([x_ref[:, (p * GH + h) * GD:(p * GH + h + 1) * GD] for h in range(GH)] for p in range(3))
        betas, Gcs, decays = _gdn_pre(beta_all, gc_all)
        Tms = _tri_inv(_gdn_A(ks, betas, decays))
        Ss = [S[h] for h in range(GH)]
        os, S_news = _gdn_chunk(qs, ks, vs, betas, Gcs, decays, Ss, Tms)
        for h in range(GH):
            s_ref[0, h] = Ss[h]
            t_ref[0, h] = Tms[h]
            o_ref[:, h * GD:(h + 1) * GD] = os[h]
            S[h] = S_news[h]

    vec = pl.BlockSpec((1, 128), lambda n: (0, 0))
    return _seq_call(
        body, carry, name=name, steps=N_CHUNK,
        in_specs=[pl.BlockSpec((GC, W3), lambda n: (n, 0)), pl.BlockSpec((GC, 128), lambda n: (n, 0)), vec, vec],
        out_specs=[pl.BlockSpec((GC, GH * GD), lambda n: (n, 0)),
                   pl.BlockSpec((1, GH, GD, GD), lambda n: (n, 0, 0, 0)),
                   pl.BlockSpec((1, GH, GC, GC), lambda n: (n, 0, 0, 0))],
        out_shape=[jax.ShapeDtypeStruct((T, GH * GD), F32), jax.ShapeDtypeStruct((N_CHUNK, GH, GD, GD), F32),
                   jax.ShapeDtypeStruct((N_CHUNK, GH, GC, GC), F32)],
        scratch_shapes=[pltpu.VMEM((GH, GD, GD), F32)], args=(qkv_c, ba, alog, dtb))


def _gdn_chunk_bwd(qkv_c, ba, alog, dtb, s_all, t_all, do, dproj, *, name, carry=None):
    W3 = 3 * GH * GD
    rev = lambda n: N_CHUNK - 1 - n

    def body(x_ref, ba_ref, al_ref, dt_ref, s_ref, t_ref, do_ref, _, dx_ref, dba_ref, dal_ref, ddt_ref, dS):
        @pl.when(pl.program_id(0) == 0)
        def _():
            dS[...] = jnp.zeros_like(dS)
            dal_ref[...] = jnp.zeros_like(dal_ref)
            ddt_ref[...] = jnp.zeros_like(ddt_ref)

        (beta_all, gc_all), vjp_gates = jax.vjp(_gdn_gates, ba_ref[...], al_ref[...], dt_ref[...])
        qs, ks, vs = ([x_ref[:, (p * GH + h) * GD:(p * GH + h + 1) * GD] for h in range(GH)] for p in range(3))
        Tms = [t_ref[0, h] for h in range(GH)]
        Ss = [s_ref[0, h] for h in range(GH)]
        dos = [do_ref[:, h * GD:(h + 1) * GD] for h in range(GH)]
        dSs = [dS[h] for h in range(GH)]

        (betas, Gcs, decays), vjp_pre = jax.vjp(_gdn_pre, beta_all, gc_all)
        _, vjp = jax.vjp(_gdn_chunk, qs, ks, vs, betas, Gcs, decays, Ss, Tms)
        dqs, dks, dvs, dbetas, dGcs, ddecays, dS_prev, dTs = vjp((dos, dSs))
        dXs = _hmap(_dot_nt, dTs, Tms)
        dAs = _hmap(lambda Tm, dX: -_dot_tn(Tm, dX), Tms, dXs)
        _, vjp_a = jax.vjp(_gdn_A, ks, betas, decays)
        dks2, dbetas2, ddecays2 = vjp_a(dAs)
        add = lambda a, b: _hmap(jnp.add, a, b)
        db_all, dg_all = vjp_pre((add(dbetas, dbetas2), dGcs, add(ddecays, ddecays2)))
        for h in range(GH):
            dS[h] = dS_prev[h]
            dx_ref[:, h * GD:(h + 1) * GD] = dqs[h]
            dx_ref[:, (GH + h) * GD:(GH + h + 1) * GD] = dks[h] + dks2[h]
            dx_ref[:, (2 * GH + h) * GD:(2 * GH + h + 1) * GD] = dvs[h]
        dba, dal, ddt = vjp_gates((db_all, dg_all))
        dba_ref[...] = dba.astype(BF16)
        dal_ref[...] += dal
        ddt_ref[...] += ddt

    vec = pl.BlockSpec((1, 128), lambda n: (0, 0))
    return _seq_call(
        body, carry, name=name, steps=N_CHUNK,
        in_specs=[pl.BlockSpec((GC, W3), lambda n: (rev(n), 0)), pl.BlockSpec((GC, 128), lambda n: (rev(n), 0)),
                  vec, vec, pl.BlockSpec((1, GH, GD, GD), lambda n: (rev(n), 0, 0, 0)),
                  pl.BlockSpec((1, GH, GC, GC), lambda n: (rev(n), 0, 0, 0)),
                  pl.BlockSpec((GC, GH * GD), lambda n: (rev(n), 0)),
                  ANY],
        out_specs=[pl.BlockSpec((GC, W3), lambda n: (rev(n), 0)), pl.BlockSpec((GC, 128), lambda n: (rev(n), COL_BA)), vec, vec],
        out_shape=[jax.ShapeDtypeStruct((T, W3), F32), jax.ShapeDtypeStruct(dproj.shape, dproj.dtype),
                   jax.ShapeDtypeStruct((1, 128), F32), jax.ShapeDtypeStruct((1, 128), F32)],
        scratch_shapes=[pltpu.VMEM((GH, GD, GD), F32)], aliases={7: 1},
        args=(qkv_c, ba, alog, dtb, s_all, t_all, do, dproj))


def _gdn_post(o, z, nw):
    return o * lax.rsqrt(jnp.mean(o * o, axis=-1, keepdims=True) + EPS) * nw * _silu(z)


def _gdn_post_fwd(o_raw, proj, nw, mixed, *, name, tm=512):
    def body(o_ref, z_ref, w_ref, _, out_ref):
        out_ref[...] = _gdn_post(o_ref[...], z_ref[...].astype(F32), w_ref[...]).astype(BF16)

    return pl.pallas_call(
        body, name=name, grid=(T // tm, GH),
        in_specs=[pl.BlockSpec((tm, GD), lambda i, h: (i, h)), pl.BlockSpec((tm, GD), lambda i, h: (i, COL_Z + h)),
                  pl.BlockSpec((1, GD), lambda i, h: (0, 0)), ANY],
        out_specs=pl.BlockSpec((tm, GD), lambda i, h: (i, HQ * DH // GD + h)),
        out_shape=jax.ShapeDtypeStruct(mixed.shape, mixed.dtype), input_output_aliases={3: 0},
        compiler_params=_cp("parallel", "parallel"),
    )(o_raw, proj, nw, mixed)


def _gdn_post_bwd(o_raw, proj, nw, dmixed, *, name, tm=512):
    def body(o_ref, z_ref, w_ref, d_ref, do_ref, dz_ref, dw_ref):
        _, vjp = jax.vjp(_gdn_post, o_ref[...], z_ref[...].astype(F32), w_ref[...])
        do, dz, dw = vjp(d_ref[...])
        do_ref[...] = do
        dz_ref[...] = dz.astype(BF16)

        @pl.when((pl.program_id(0) == 0) & (pl.program_id(1) == 0))
        def _():
            dw_ref[...] = jnp.zeros_like(dw_ref)

        dw_ref[...] += dw

    blk = pl.BlockSpec((tm, GD), lambda i, h: (i, h))
    vec = pl.BlockSpec((1, GD), lambda i, h: (0, 0))
    return pl.pallas_call(
        body, name=name, grid=(T // tm, GH),
        in_specs=[blk, pl.BlockSpec((tm, GD), lambda i, h: (i, COL_Z + h)), vec,
                  pl.BlockSpec((tm, GD), lambda i, h: (i, GH + h))],
        out_specs=[blk, pl.BlockSpec((tm, GD), lambda i, h: (i, COL_Z + h)), vec],
        out_shape=[jax.ShapeDtypeStruct((T, GH * GD), F32), jax.ShapeDtypeStruct((T, N_PROJ), BF16),
                   jax.ShapeDtypeStruct((1, GD), F32)],
        compiler_params=_cp("arbitrary", "arbitrary"),
    )(o_raw, proj, nw, dmixed)


def _adamw(w, g, m, v, *, name):
    shape = w.shape
    cols = shape[-1]
    w2, g2, m2, v2 = (a.reshape(-1, cols) for a in (w, g, m, v))
    rows = w2.shape[0]
    tr = next((t for t in (512, 256, 128, 64, 32, 16, 8) if rows % t == 0), rows)

    def body(w_ref, g_ref, m_ref, v_ref, d_ref, nm_ref, nv_ref):
        gv = g_ref[...]
        nm = B1 * m_ref[...] + (1.0 - B1) * gv
        nv = B2 * v_ref[...] + (1.0 - B2) * jnp.square(gv)
        m_hat = nm / (1.0 - B1 ** STEP)
        v_hat = nv / (1.0 - B2 ** STEP)
        d_ref[...] = -LR * (m_hat / (jnp.sqrt(v_hat) + AEPS) + WD * w_ref[...])
        nm_ref[...] = nm
        nv_ref[...] = nv

    blk = pl.BlockSpec((tr, cols), lambda i: (i, 0))
    out = pl.pallas_call(
        body, name=name, grid=(rows // tr,), in_specs=[blk] * 4, out_specs=[blk] * 3,
        out_shape=[jax.ShapeDtypeStruct((rows, cols), F32)] * 3, compiler_params=_cp("parallel"),
    )(w2, g2, m2, v2)
    return tuple(o.reshape(shape) for o in out)


def _layer_weights(w_in):
    return jnp.pad(w_in, ((0, 0), (0, N_PROJ - w_in.shape[1])))


def _layer_params(attn_norm, sinks, gcw, a_log, dt_bias, gnw, ffn_norm, fcw, fcb):
    lanes4 = lambda v: jnp.pad(v, (GH, 128 - 2 * GH))[None]
    return dict(attn_norm=attn_norm[None], sinks=jnp.pad(sinks, (0, 128 - HQ))[None], gcw=gcw, alog=lanes4(a_log),
                dtb=lanes4(dt_bias), gnw=gnw[None], ffn_norm=ffn_norm[None], fcw=fcw, fcb=fcb[None])


def _mixer_fwd(x, W, P, l, carry=None):
    n = lambda s: f"l{l}_{s}"
    h = _rmsnorm_fwd(x, P["attn_norm"], name=n("norm1"))
    proj = _mm(h, W["all"], name=n("proj"), mode="nn", dims=(T, N_MAIN, D), tm=512, tn=N_MAIN, tk=D, out_dtype=BF16)
    ba = _mm(h, W["all"], name=n("proj_ba"), mode="nn", dims=(T, 128, D), tm=1024, tn=128, tk=D,
             b_spec=pl.BlockSpec((D, 128), lambda i, j, k: (k, COL_BA)))
    mixed = _swa_fwd(proj, P["sinks"], name=n("swa"))
    qkv_c = _gdn_prep_fwd(proj, P["gcw"], name=n("gdn_prep"))
    (o_raw, s_all, t_all), carried = _gdn_chunk_fwd(qkv_c, ba, P["alog"], P["dtb"], name=n("gdn_chunk"), carry=carry)
    mixed = _gdn_post_fwd(o_raw, proj, P["gnw"], mixed, name=n("gdn_post"))
    x1 = _mm(mixed, W["out"], res=x, name=n("out_proj"), mode="nn", dims=(T, D, D), tm=512, tn=D, tk=D)
    saved = dict(x=x, h=h, proj=proj, ba=ba, qkv_c=qkv_c, o_raw=o_raw, s_all=s_all, t_all=t_all, mixed=mixed, x1=x1)
    return x1, saved, carried


def _ffn_fwd(x1, W, P, l, carry=None):
    n = lambda s: f"l{l}_{s}"
    h2 = _rmsnorm_fwd(x1, P["ffn_norm"], name=n("norm2"))
    gu = _mm(h2, W["ffn"], name=n("ffn_in"), mode="nn", dims=(T, 2 * DFF, D), tm=512, tn=FFN_CW, tk=D, out_dtype=BF16,
             b_spec=pl.BlockSpec((None, D, FFN_CW), lambda i, j, k: (j, k, 0)), n_outer=True)
    act, carried = _ffn_act_fwd(gu, P["fcw"], P["fcb"], name=n("ffn_act"), carry=carry)
    x2 = _mm(act, W["down"], res=x1, name=n("ffn_down"), mode="nn", dims=(T, D, DFF), tm=512, tn=D, tk=DFF)
    return x2, dict(h2=h2, gu=gu, act=act), carried


def _ffn_bwd(dx2, sv, W, P, l):
    n = lambda s: f"l{l}_{s}"
    CW = FFN_CW
    dact = _mm(dx2, W["down"], name=n("d_act"), mode="nt", dims=(T, DFF, D), tm=512, tn=DFF // 2, tk=D, out_dtype=BF16,
               n_outer=True)
    g_down = _mm(sv["act"], dx2, name=n("g_down"), mode="tn", dims=(DFF, D, T), tm=DFF // 2, tn=D, tk=1024, out_dtype=BF16)
    dgu, g_fcw, g_fcb = _ffn_act_bwd(sv["gu"], P["fcw"], P["fcb"], dact, name=n("d_ffn_act"))
    dh2 = _mm(dgu, W["ffn"], name=n("d_h2"), mode="nt", dims=(T, D, 2 * DFF), tm=1024, tn=D, tk=CW,
              a_spec=pl.BlockSpec((None, 1024, CW), lambda i, j, k: (k // 2, i, k % 2)),
              b_spec=pl.BlockSpec((None, D, CW), lambda i, j, k: (k, j, 0)))
    g_ffn = _mm(sv["h2"], dgu, name=n("g_ffn"), mode="tn", dims=(D, 2 * DFF, T), tm=D, tn=CW, tk=1024,
                b_spec=pl.BlockSpec((None, 1024, CW), lambda i, j, k: (j // 2, k, j % 2)),
                out_spec=pl.BlockSpec((None, D, CW), lambda i, j, k: (j, i, 0)),
                out_shape=jax.ShapeDtypeStruct((N_CHIP, D, CW), BF16), out_dtype=BF16)
    dx1, g_ffn_norm = _rmsnorm_bwd(sv["x1"], P["ffn_norm"], dh2, dx2, name=n("d_norm2"))
    return dx1, dict(ffn_norm=g_ffn_norm[0], w_ffn_in=g_ffn, ffn_conv_w=g_fcw, ffn_conv_b=g_fcb[0], w_down=g_down)


def _mixer_bwd(dx1, sv, W, P, l, carry=None):
    n = lambda s: f"l{l}_{s}"
    dmixed = _mm(dx1, W["out"], name=n("d_mixed"), mode="nt", dims=(T, D, D), tm=512, tn=D, tk=D)
    g_out = _mm(sv["mixed"], dx1, name=n("g_out"), mode="tn", dims=(D, D, T), tm=D, tn=D, tk=1024, out_dtype=BF16)
    do_raw, dproj, g_gnw = _gdn_post_bwd(sv["o_raw"], sv["proj"], P["gnw"], dmixed, name=n("d_gdn_post"))
    (dqkv_c, dproj, g_alog, g_dtb), carried = _gdn_chunk_bwd(
        sv["qkv_c"], sv["ba"], P["alog"], P["dtb"], sv["s_all"], sv["t_all"], do_raw, dproj, name=n("d_gdn_chunk"), carry=carry)
    dproj, g_gcw = _gdn_prep_bwd(sv["proj"], P["gcw"], dqkv_c, dproj, name=n("d_gdn_prep"))
    dproj, g_sinks = _swa_bwd(sv["proj"], P["sinks"], dmixed, dproj, name=n("d_swa"))
    dh = _mm(dproj, W["all"], name=n("d_h"), mode="nt", dims=(T, D, N_PROJ), tm=512, tn=D, tk=N_PROJ)
    g_all = _mm(sv["h"], dproj, name=n("g_in"), mode="tn", dims=(D, N_PROJ, T), tm=D, tn=N_PROJ, tk=1024, out_dtype=BF16)
    dx, g_attn_norm = _rmsnorm_bwd(sv["x"], P["attn_norm"], dh, dx1, name=n("d_norm1"))
    grads = dict(attn_norm=g_attn_norm[0], w_in=g_all, attn_sinks=g_sinks[0, :HQ], gdn_conv_w=g_gcw,
                 gdn_a_log=g_alog[0, GH:2 * GH], gdn_dt_bias=g_dtb[0, GH:2 * GH], gdn_norm=g_gnw[0], w_out=g_out)
    return dx, grads, carried


def _pos():
    return lax.axis_index("x"), lax.axis_index("y"), lax.axis_index("c")


def _other_chips(x, y):
    return [(1 - x, y), (x, 1 - y), (1 - x, 1 - y)]


def _remote(src, dst, send_sems, recv_sems, k, to):
    return pltpu.make_async_remote_copy(src_ref=src, dst_ref=dst, send_sem=send_sems.at[k], recv_sem=recv_sems.at[k],
                                        device_id=to, device_id_type=MESH)


def _run_carry(carry, *, name):
    ci, co = len(carry.arrays), len(carry.out_shapes)

    def body(*refs):
        sends, recvs = carry.plan(refs[:ci], refs[ci:ci + co], refs[-2], refs[-1])
        for cp in sends:
            cp.start()
        for cp in recvs:
            cp.wait_recv()
        for cp in sends:
            cp.wait_send()

    return list(pl.pallas_call(
        body, name=name, in_specs=[ANY] * ci, out_specs=[ANY] * co, out_shape=carry.out_shapes,
        scratch_shapes=[pltpu.SemaphoreType.DMA((carry.nsem,)), pltpu.SemaphoreType.DMA((carry.nsem,))],
    )(*carry.arrays))


def _chip_index():
    return 2 * lax.axis_index("x") + lax.axis_index("y")


def _gather_carry(shards):
    def plan(srcs, outs, send_sems, recv_sems):
        x, y, c = _pos()
        chip = 2 * x + y
        others = [(k, px, py, 2 * px + py) for k, (px, py) in enumerate(_other_chips(x, y))]
        sends = [_remote(srcs[t], outs[t].at[chip], send_sems, recv_sems, 3 * t + k, (px, py, c))
                 for t in range(len(srcs)) for k, px, py, _ in others]
        recvs = [_remote(srcs[t], outs[t].at[j], send_sems, recv_sems, 3 * t + k, (x, y, c))
                 for t in range(len(srcs)) for k, _, _, j in others]
        return sends, recvs

    return _Carry(shards, [jax.ShapeDtypeStruct((N_CHIP,) + s.shape, s.dtype) for s in shards], 3 * len(shards), plan)


def _fill_own(outs, shards):
    return [lax.dynamic_update_index_in_dim(o, s, _chip_index(), 0) for o, s in zip(outs, shards)]


HBM_SPEC = pl.BlockSpec(memory_space=pltpu.HBM)
SEM_SPEC = pl.BlockSpec(memory_space=pltpu.SEMAPHORE)
DATAFLOW = pltpu.SideEffectType.DATAFLOW_SIDE_EFFECTING


def _split_start(carry, after, *, name):
    ci, co = len(carry.arrays), len(carry.out_shapes)

    def body(*refs):
        srcs, lands, send_sems, recv_sems, token = refs[:ci], refs[ci:ci + co], refs[ci + co + 1], refs[ci + co + 2], refs[-1]
        for cp in carry.plan(srcs, lands, send_sems, recv_sems)[0]:
            cp.start()
        token[...] = jnp.zeros_like(token)

    lands = [lax.empty(s.shape, s.dtype) for s in carry.out_shapes]
    hbm = lambda a: pltpu.with_memory_space_constraint(a, pltpu.HBM)
    out = pl.pallas_call(
        body, name=name, in_specs=[HBM_SPEC] * (ci + co) + [ANY],
        out_specs=[SEM_SPEC, SEM_SPEC] + [HBM_SPEC] * (ci + co) + [VMEM_SPEC],
        out_shape=[pltpu.SemaphoreType.DMA((carry.nsem,)), pltpu.SemaphoreType.DMA((carry.nsem,))]
        + [pltpu.HBM(a.shape, a.dtype) for a in carry.arrays + lands] + [jax.ShapeDtypeStruct((8, 128), F32)],
        input_output_aliases={t: 2 + t for t in range(ci + co)},
        compiler_params=pltpu.CompilerParams(has_side_effects=DATAFLOW),
    )(*[hbm(a) for a in carry.arrays + lands], after)
    return out[0], out[1], list(out[2:2 + ci]), list(out[2 + ci:2 + ci + co]), out[-1]


def _split_wait(carry, send_sems, recv_sems, srcs, lands, after, *, name):
    ci, co = len(srcs), len(lands)

    def body(*refs):
        sends, recvs = carry.plan(refs[:ci], refs[ci:ci + co], refs[ci + co], refs[ci + co + 1])
        for cp in sends:
            cp.wait_send()
        for cp in recvs:
            cp.wait_recv()

    out = pl.pallas_call(
        body, name=name, in_specs=[HBM_SPEC] * (ci + co) + [SEM_SPEC, SEM_SPEC, ANY], out_specs=[HBM_SPEC] * (ci + co),
        out_shape=[pltpu.HBM(a.shape, a.dtype) for a in list(srcs) + list(lands)],
        input_output_aliases={t: t for t in range(ci + co)},
        compiler_params=pltpu.CompilerParams(has_side_effects=DATAFLOW),
    )(*srcs, *lands, send_sems, recv_sems, after)
    return list(out[:ci]), list(out[ci:])


def _gather_start(shards, after, *, name):
    return _split_start(_gather_carry(shards), after, name=name)


def _gather_wait(send_sems, recv_sems, shards, lands, after, *, name):
    srcs, got = _split_wait(_gather_carry(shards), send_sems, recv_sems, shards, lands, after, name=name)
    return _fill_own(got, srcs)


def _ag_small(v, *, name):
    m, n = v.shape

    def body(x_ref, out_ref, red_ref, send_sems, recv_sems, local_sem):
        x, y, c = _pos()
        me, sibling = (x, y, c), (x, y, 1 - c)
        chips = _other_chips(x, y)
        rows = lambda px, py, pc: out_ref.at[pl.ds(pl.multiple_of((4 * px + 2 * py + pc) * m, 8), m), :]
        mine = pltpu.make_async_copy(x_ref, rows(*me), local_sem)
        mine.start()
        first = [_remote(x_ref, rows(*me), send_sems, recv_sems, 0, sibling)]
        first += [_remote(x_ref, rows(*me), send_sems, recv_sems, 1 + k, (*chip, c)) for k, chip in enumerate(chips)]
        for cp in first:
            cp.start()
        passed = [_remote(rows(*chip, c), rows(*chip, c), send_sems, recv_sems, 4 + k, sibling) for k, chip in enumerate(chips)]
        for k, chip in enumerate(chips):
            _remote(rows(*chip, c), rows(*chip, c), send_sems, recv_sems, 1 + k, me).wait_recv()
            passed[k].start()
        _remote(rows(*sibling), rows(*sibling), send_sems, recv_sems, 0, me).wait_recv()
        for k, chip in enumerate(chips):
            _remote(rows(*chip, 1 - c), rows(*chip, 1 - c), send_sems, recv_sems, 4 + k, me).wait_recv()
        for cp in first + passed:
            cp.wait_send()
        mine.wait()
        acc = out_ref[0:m, :]
        for d in range(1, N_DEV):
            acc = acc + out_ref[d * m:(d + 1) * m, :]
        red_ref[...] = acc

    return pl.pallas_call(
        body, name=name, in_specs=[VMEM_SPEC], out_specs=[VMEM_SPEC, VMEM_SPEC],
        out_shape=[jax.ShapeDtypeStruct((N_DEV * m, n), v.dtype), jax.ShapeDtypeStruct((m, n), v.dtype)],
        scratch_shapes=[pltpu.SemaphoreType.DMA((7,)), pltpu.SemaphoreType.DMA((7,)), pltpu.SemaphoreType.DMA],
    )(v)


def _halves(ref, c):
    rh = ref.shape[1] // 2
    return ref.at[:, pl.ds(pl.multiple_of(c * rh, 16), rh), :]


def _rs_swap(gs, *, name):
    nt = len(gs)

    def body(*refs):
        g, theirs = refs[:nt], refs[nt:2 * nt]
        send_sems, recv_sems = refs[2 * nt:]
        x, y, c = _pos()
        swaps = [_remote(_halves(g[t], 1 - c), theirs[t], send_sems, recv_sems, t, (x, y, 1 - c)) for t in range(nt)]
        for cp in swaps:
            cp.start()
        for cp in swaps:
            cp.wait()

    return pl.pallas_call(
        body, name=name, in_specs=[ANY] * nt, out_specs=[ANY] * nt,
        out_shape=[jax.ShapeDtypeStruct((a.shape[0], a.shape[1] // 2, a.shape[2]), a.dtype) for a in gs],
        scratch_shapes=[pltpu.SemaphoreType.DMA((nt,)), pltpu.SemaphoreType.DMA((nt,))],
    )(*gs)


def _exchange_carry(ss):
    def plan(s, out, send_sems, recv_sems):
        x, y, c = _pos()
        chip = 2 * x + y
        others = [(k, px, py, 2 * px + py) for k, (px, py) in enumerate(_other_chips(x, y))]
        sends = [_remote(s[t].at[j], out[t].at[chip], send_sems, recv_sems, 3 * t + k, (px, py, c))
                 for t in range(len(s)) for k, px, py, j in others]
        recvs = [_remote(s[t].at[j], out[t].at[j], send_sems, recv_sems, 3 * t + k, (x, y, c))
                 for t in range(len(s)) for k, _, _, j in others]
        return sends, recvs

    return _Carry(ss, [jax.ShapeDtypeStruct(a.shape, a.dtype) for a in ss], 3 * len(ss), plan)


def _fill_own_slab(outs, ss):
    chip = _chip_index()
    return [lax.dynamic_update_index_in_dim(o, lax.dynamic_index_in_dim(s, chip, 0, keepdims=False), chip, 0)
            for o, s in zip(outs, ss)]


def _rs_join(rs, *, name):
    nt = len(rs)

    def body(*refs):
        r, theirs = refs[:nt], refs[nt:2 * nt]
        send_sems, recv_sems = refs[2 * nt:]
        x, y, c = _pos()
        swaps = [_remote(r[t], theirs[t], send_sems, recv_sems, t, (x, y, 1 - c)) for t in range(nt)]
        for cp in swaps:
            cp.start()
        for cp in swaps:
            cp.wait()

    theirs = pl.pallas_call(
        body, name=name, in_specs=[ANY] * nt, out_specs=[ANY] * nt,
        out_shape=[jax.ShapeDtypeStruct(a.shape, a.dtype) for a in rs],
        scratch_shapes=[pltpu.SemaphoreType.DMA((nt,)), pltpu.SemaphoreType.DMA((nt,))],
    )(*rs)
    first = lax.axis_index("c") == 0
    return [jnp.concatenate([jnp.where(first, a, b), jnp.where(first, b, a)], axis=0) for a, b in zip(rs, theirs)]


def _row_tile(rows, dtype):
    unit = 16 if dtype == BF16 else 8
    if rows <= 512:
        return rows
    return next((t for t in (512, 256, 128, 64, 32, 16, 8) if rows % t == 0 and t % unit == 0), rows)


def _add_pair(a, b, *, name):
    n, rh, cols = a.shape
    tr = _row_tile(rh, BF16)

    def body(a_ref, b_ref, o_ref):
        o_ref[...] = (a_ref[...].astype(F32) + b_ref[...].astype(F32)).astype(BF16)

    blk = pl.BlockSpec((None, tr, cols), lambda j, i: (j, i, 0))
    return pl.pallas_call(
        body, name=name, grid=(n, rh // tr), in_specs=[blk, blk], out_specs=blk,
        out_shape=jax.ShapeDtypeStruct(a.shape, BF16), compiler_params=_cp("parallel", "parallel"),
    )(a, b)


def _sum_chips(b, *, name):
    n, rh, cols = b.shape
    tr = _row_tile(rh, BF16)

    def body(b_ref, o_ref):
        acc = b_ref[0].astype(F32)
        for j in range(1, n):
            acc = acc + b_ref[j].astype(F32)
        o_ref[...] = acc

    return pl.pallas_call(
        body, name=name, grid=(rh // tr,), in_specs=[pl.BlockSpec((n, tr, cols), lambda i: (0, i, 0))],
        out_specs=pl.BlockSpec((tr, cols), lambda i: (i, 0)),
        out_shape=jax.ShapeDtypeStruct((rh, cols), F32), compiler_params=_cp("parallel"),
    )(b)


BIG = ("w_in", "w_out", "w_ffn_in", "w_down")
W_IN_SHARD = (N_MAIN + 2 * GH) // N_CHIP


def _chip_major(name, g):
    if name == "w_in":
        return jnp.stack([g[:, j * W_IN_SHARD:(j + 1) * W_IN_SHARD] for j in range(N_CHIP)])
    if name == "w_ffn_in":
        return g
    return g.reshape(N_CHIP, g.shape[0] // N_CHIP, g.shape[1])


def _rs_pairs(gs, tag):
    c = lax.axis_index("c")
    mine = [lax.dynamic_slice_in_dim(a, c * (a.shape[1] // 2), a.shape[1] // 2, axis=1) for a in gs]
    theirs = _rs_swap(gs, name=f"{tag}_swap")
    return [_add_pair(a, b, name=f"{tag}_add{i}") for i, (a, b) in enumerate(zip(mine, theirs))]


def _rs_sums(got, pairs, tag):
    return [_sum_chips(b, name=f"{tag}_sum{i}") for i, b in enumerate(_fill_own_slab(got, pairs))]


def _size(shape):
    n = 1
    for s in shape:
        n *= s
    return n


def _pack_flat(parts, total, dtype):
    flat = jnp.concatenate([p.reshape(-1).astype(dtype) for p in parts])
    return jnp.pad(flat, (0, total - flat.shape[0]))


def _unpack_flat(flat, shapes):
    out, o = [], 0
    for s in shapes:
        out.append(flat[o:o + _size(s)].reshape(s))
        o += _size(s)
    return out


WEIGHTS = ("attn_norm", "w_in", "attn_sinks", "gdn_conv_w", "gdn_a_log", "gdn_dt_bias", "gdn_norm", "w_out", "ffn_norm",
           "w_ffn_in", "ffn_conv_w", "ffn_conv_b", "w_down", "final_norm")
SMALL = {"attn_norm": (DEPTH, D), "attn_sinks": (DEPTH, HQ), "gdn_a_log": (DEPTH, GH), "gdn_dt_bias": (DEPTH, GH),
         "gdn_norm": (DEPTH, GD), "ffn_norm": (DEPTH, D), "ffn_conv_b": (DEPTH, DFF), "final_norm": (D,)}
CONV_FULL = {"gdn_conv_w": (DEPTH, GK, 3 * GH * GD), "ffn_conv_w": (DEPTH, FK, DFF)}
CONV_SHARD = {"gdn_conv_w": (DEPTH, GK, 3 * GH * GD // N_CHIP), "ffn_conv_w": (DEPTH, FK, DFF // N_CHIP)}
CONV_ROWS, SMALLG_ROWS, SMALLW_ROWS = 64, 320, 144


def kernel(x, attn_norm, w_in, attn_sinks, gdn_conv_w, gdn_a_log, gdn_dt_bias, gdn_norm, w_out, ffn_norm, w_ffn_in, ffn_conv_w, ffn_conv_b, w_down, final_norm, loss_target, m_attn_norm, m_w_in, m_attn_sinks, m_gdn_conv_w, m_gdn_a_log, m_gdn_dt_bias, m_gdn_norm, m_w_out, m_ffn_norm, m_w_ffn_in, m_ffn_conv_w, m_ffn_conv_b, m_w_down, m_final_norm, v_attn_norm, v_w_in, v_attn_sinks, v_gdn_conv_w, v_gdn_a_log, v_gdn_dt_bias, v_gdn_norm, v_w_out, v_ffn_norm, v_w_ffn_in, v_ffn_conv_w, v_ffn_conv_b, v_w_down, v_final_norm):
    w = dict(zip(WEIGHTS, (attn_norm, w_in, attn_sinks, gdn_conv_w, gdn_a_log, gdn_dt_bias, gdn_norm, w_out, ffn_norm,
                           w_ffn_in, ffn_conv_w, ffn_conv_b, w_down, final_norm)))
    m = dict(zip(WEIGHTS, (m_attn_norm, m_w_in, m_attn_sinks, m_gdn_conv_w, m_gdn_a_log, m_gdn_dt_bias, m_gdn_norm, m_w_out,
                           m_ffn_norm, m_w_ffn_in, m_ffn_conv_w, m_ffn_conv_b, m_w_down, m_final_norm)))
    v = dict(zip(WEIGHTS, (v_attn_norm, v_w_in, v_attn_sinks, v_gdn_conv_w, v_gdn_a_log, v_gdn_dt_bias, v_gdn_norm, v_w_out,
                           v_ffn_norm, v_w_ffn_in, v_ffn_conv_w, v_ffn_conv_b, v_w_down, v_final_norm)))
    cx, cy, _ = _pos()
    chip = 2 * cx + cy

    cpack = _pack_flat([w[n] for n in CONV_SHARD], CONV_ROWS * 128, F32).reshape(CONV_ROWS, 128)
    cgath, _ = _ag_small(cpack, name="gather_conv_w")
    cgath = cgath.reshape(N_DEV, CONV_ROWS * 128)
    cshards = [_unpack_flat(cgath[2 * j], list(CONV_SHARD.values())) for j in range(N_CHIP)]
    conv = {n: jnp.concatenate([cshards[j][i] for j in range(N_CHIP)], axis=2) for i, n in enumerate(CONV_SHARD)}

    Ps = [_layer_params(attn_norm[l], attn_sinks[l], conv["gdn_conv_w"][l], gdn_a_log[l], gdn_dt_bias[l], gdn_norm[l],
                        ffn_norm[l], conv["ffn_conv_w"][l], ffn_conv_b[l]) for l in range(DEPTH)]

    wb = {n: [w[n][l].astype(BF16) for l in range(DEPTH)] for n in BIG}
    mixer_w = lambda g_in, g_out: dict(all=_layer_weights(jnp.concatenate(list(g_in), axis=1)), out=g_out.reshape(D, D))
    ffn_w = lambda g_ffn, g_down: dict(ffn=g_ffn, down=g_down.reshape(DFF, D))
    groups = [[wb[a][l], wb[b][l]] for l in range(DEPTH) for a, b in (("w_in", "w_out"), ("w_ffn_in", "w_down"))]
    def after_start(P, key, started):
        return {**P, key: P[key] + started[4][:1, :1]}

    g0 = _gather_start(groups[0], x, name="gather0_start")
    got0 = _gather_wait(*g0[:4], g0[4], name="gather0_wait")
    g1 = _gather_start(groups[1], got0[1], name="gather1_start")
    Wm0 = mixer_w(*got0)
    h, sv0, _ = _mixer_fwd(x[0], Wm0, after_start(Ps[0], "attn_norm", g1), 0)
    got1 = _gather_wait(*g1[:4], h, name="gather1_wait")
    g2 = _gather_start(groups[2], got1[1], name="gather2_start")
    Wf0 = ffn_w(*got1)
    h, sv0f, _ = _ffn_fwd(h, Wf0, after_start(Ps[0], "ffn_norm", g2), 0)
    got2 = _gather_wait(*g2[:4], h, name="gather2_wait")
    g3 = _gather_start(groups[3], got2[1], name="gather3_start")
    Wm1 = mixer_w(*got2)
    h, sv1, _ = _mixer_fwd(h, Wm1, after_start(Ps[1], "attn_norm", g3), 1)
    Wf1 = ffn_w(*_gather_wait(*g3[:4], h, name="gather3_wait"))
    h, sv1f, _ = _ffn_fwd(h, Wf1, Ps[1], 1)
    dx, g_final, loss_part = _loss_head(h, final_norm[None], loss_target[0], name="loss_head")

    def exchange_start(pairs, tag):
        carry = _exchange_carry(pairs)
        return carry, _split_start(carry, pairs[0], name=f"{tag}_exchange_start")

    def exchange_wait(carry, started, after, tag):
        srcs, got = _split_wait(carry, *started[:4], after, name=f"{tag}_exchange_wait")
        return _rs_sums(got, srcs, tag)

    tied = lambda P, key, started: {**P, key: P[key] + started[4][:1, :1]}
    dx, gf1 = _ffn_bwd(dx, {**sv1, **sv1f}, Wf1, Ps[1], 1)
    dx, gm1, _ = _mixer_bwd(dx, sv1, Wm1, Ps[1], 1)
    lg1 = {**gf1, **gm1}
    pairs1 = _rs_pairs([_chip_major(n, lg1[n]) for n in BIG], "rs1")
    ex1 = exchange_start(pairs1, "rs1")
    dx, gf0 = _ffn_bwd(dx, {**sv0, **sv0f}, Wf0, tied(Ps[0], "fcb", ex1[1]), 0)
    pairs0a = _rs_pairs([_chip_major(n, gf0[n]) for n in BIG[2:]], "rs0a")
    ex0a = exchange_start(pairs0a, "rs0a")
    dx, gm0, _ = _mixer_bwd(dx, sv0, Wm0, tied(Ps[0], "gnw", ex0a[1]), 0)
    lg = [{**gf0, **gm0}, lg1]
    pairs0b = _rs_pairs([_chip_major(n, gm0[n]) for n in BIG[:2]], "rs0b")
    sums0b = _rs_sums(_run_carry(_exchange_carry(pairs0b), name="rs0b_exchange"), pairs0b, "rs0b")
    sums1 = exchange_wait(*ex1, sums0b[0], "rs1")
    sums0a = exchange_wait(*ex0a, sums1[0], "rs0a")
    joined = _rs_join(sums0b + sums0a + sums1, name="rs_join")
    reduced = [joined[:4], joined[4:]]
    grad_x = dx[None]
    stacked = lambda n: jnp.stack([lg[l][n] for l in range(DEPTH)])

    small_parts = [g_final[0] if n == "final_norm" else stacked(n) for n in SMALL] + [stacked(n) for n in CONV_FULL]
    svec = _pack_flat(small_parts + [loss_part[0, :1]], SMALLG_ROWS * 128, F32).reshape(SMALLG_ROWS, 128)
    _, sred = _ag_small(svec, name="reduce_small")
    small_g = _unpack_flat(sred.reshape(-1), list(SMALL.values()) + list(CONV_FULL.values()) + [(1,)])
    g = dict(zip(list(SMALL) + list(CONV_FULL), small_g[:-1]))
    loss = small_g[-1][0]
    for n in CONV_FULL:
        wd = CONV_SHARD[n][2]
        g[n] = lax.dynamic_slice_in_dim(g[n], chip * wd, wd, axis=2)

    g.update({n: jnp.stack([reduced[l][i] for l in range(DEPTH)]) for i, n in enumerate(BIG)})

    delta, new_m, new_v = {}, {}, {}
    for n in BIG:
        delta[n], new_m[n], new_v[n] = _adamw(w[n], g[n], m[n], v[n], name=f"adamw_{n}")
    small_names = list(SMALL) + list(CONV_SHARD)
    small_shapes = list(SMALL.values()) + list(CONV_SHARD.values())
    packed = [_pack_flat([t[n] for n in small_names], SMALLW_ROWS * 128, F32).reshape(SMALLW_ROWS, 128) for t in (w, g, m, v)]
    for res, out in zip(_adamw(*packed, name="adamw_small"), (delta, new_m, new_v)):
        out.update(zip(small_names, _unpack_flat(res.reshape(-1), small_shapes)))

    return (loss, grad_x, *[g[n] for n in WEIGHTS], *[delta[n] for n in WEIGHTS], *[new_m[n] for n in WEIGHTS],
            *[new_v[n] for n in WEIGHTS])
```

```python
import functools

import jax
import jax.numpy as jnp
import numpy as np
from jax import lax
from jax.experimental import pallas as pl
from jax.experimental.pallas import tpu as pltpu

F32, BF16 = jnp.float32, jnp.bfloat16
HIGHEST = lax.Precision.HIGHEST
MESH = pl.DeviceIdType.MESH

D = 1024
T = 4096
DEPTH = 2
HQ, HKV, DH, WIN = 8, 2, 64, 128
GH, GD, GC, GK = 4, 128, 64, 4
DFF, FK = 2816, 3
EPS = 1e-6
N_MAIN = 2816
N_PROJ = N_MAIN + 128
COL_K, COL_V, COL_G, COL_Z, COL_BA = 4, 5, 6, 18, 22
N_CHUNK = T // GC
N_DEV, N_CHIP = 8, 4
FFN_CW = 2 * DFF // N_CHIP
LR, B1, B2, AEPS, WD, STEP = 0.001, 0.9, 0.999, 1e-08, 0.01, 10

VMEM_LIMIT = 56 * 1024 * 1024
ANY = pl.BlockSpec(memory_space=pl.ANY)
VMEM_SPEC = pl.BlockSpec(memory_space=pltpu.VMEM)


def _cp(*sem):
    return pltpu.CompilerParams(dimension_semantics=sem if sem else None, vmem_limit_bytes=VMEM_LIMIT)


class _Carry:
    def __init__(self, arrays, out_shapes, nsem, plan):
        self.arrays, self.out_shapes, self.nsem, self.plan = list(arrays), list(out_shapes), nsem, plan


def _seq_call(body, carry, *, name, steps, in_specs, out_specs, out_shape, args, scratch_shapes=(), aliases=None):
    in_specs, out_specs, out_shape, scratch = list(in_specs), list(out_specs), list(out_shape), list(scratch_shapes)
    n_in, n_out, n_scr = len(in_specs), len(out_specs), len(scratch)
    if carry is None:
        fn = body
    else:
        ci, co = len(carry.arrays), len(carry.out_shapes)

        def fn(*refs):
            ins, cins = refs[:n_in], refs[n_in:n_in + ci]
            outs, couts = refs[n_in + ci:n_in + ci + n_out], refs[n_in + ci + n_out:n_in + ci + n_out + co]
            scr, (ssem, rsem) = refs[n_in + ci + n_out + co:-2], refs[-2:]
            sends, recvs = carry.plan(cins, couts, ssem, rsem)

            @pl.when(pl.program_id(0) == 0)
            def _():
                for cp in sends:
                    cp.start()

            body(*ins, *outs, *scr)

            @pl.when(pl.program_id(0) == steps - 1)
            def _():
                for cp in recvs:
                    cp.wait_recv()
                for cp in sends:
                    cp.wait_send()

        in_specs += [ANY] * ci
        out_specs += [ANY] * co
        out_shape += carry.out_shapes
        scratch += [pltpu.SemaphoreType.DMA((carry.nsem,)), pltpu.SemaphoreType.DMA((carry.nsem,))]
        args = list(args) + carry.arrays
    outs = pl.pallas_call(
        fn, name=name, grid=(steps,), in_specs=in_specs, out_specs=out_specs, out_shape=out_shape,
        scratch_shapes=scratch, input_output_aliases=aliases or {}, compiler_params=_cp("arbitrary"),
    )(*args)
    return list(outs[:n_out]), list(outs[n_out:])


def _dot(a, b):
    return jnp.dot(a.astype(BF16), b.astype(BF16), preferred_element_type=F32)


def _dot_nt(a, b):
    return lax.dot_general(a.astype(BF16), b.astype(BF16), (((1,), (1,)), ((), ())), preferred_element_type=F32)


def _dot_tn(a, b):
    return lax.dot_general(a.astype(BF16), b.astype(BF16), (((0,), (0,)), ((), ())), preferred_element_type=F32)


def _iota(shape, dim):
    return lax.broadcasted_iota(jnp.int32, shape, dim)


def _col(x, idx):
    return jnp.sum(jnp.where(_iota(x.shape, 1) == idx, x, 0.0), axis=1, keepdims=True)


def _silu(y):
    return y * jax.nn.sigmoid(y)


_MM_DN = {"nn": (((1,), (0,)), ((), ())), "nt": (((1,), (1,)), ((), ())), "tn": (((0,), (0,)), ((), ()))}


def _mm(a, b, *, name, mode, dims, tm, tn, tk, res=None, out_dtype=F32, a_spec=None, b_spec=None, out_spec=None,
        out_shape=None, n_outer=False):
    M, N, K = dims
    assert M % tm == 0 and N % tn == 0 and K % tk == 0, (name, dims)
    nk = K // tk
    dn = _MM_DN[mode]

    def body(*refs):
        a_ref, b_ref = refs[:2]
        r_ref = None if res is None else refs[2]
        o_ref = refs[2 if res is None else 3]
        part = lax.dot_general(a_ref[...].astype(BF16), b_ref[...].astype(BF16), dn, preferred_element_type=F32)
        if nk == 1:
            o_ref[...] = (part if res is None else part + r_ref[...]).astype(out_dtype)
            return
        acc, k = refs[-1], pl.program_id(2)

        @pl.when(k == 0)
        def _():
            acc[...] = part

        @pl.when(k > 0)
        def _():
            acc[...] += part

        @pl.when(k == nk - 1)
        def _():
            r = acc[...]
            if res is not None:
                r = r + r_ref[...]
            o_ref[...] = r.astype(out_dtype)

    if a_spec is None:
        a_spec = pl.BlockSpec((tk, tm), lambda i, j, k: (k, i)) if mode == "tn" else pl.BlockSpec((tm, tk), lambda i, j, k: (i, k))
    if b_spec is None:
        b_spec = pl.BlockSpec((tn, tk), lambda i, j, k: (j, k)) if mode == "nt" else pl.BlockSpec((tk, tn), lambda i, j, k: (k, j))
    in_specs, args = [a_spec, b_spec], [a, b]
    if res is not None:
        in_specs.append(pl.BlockSpec((tm, tn), lambda i, j, k: (i, j)))
        args.append(res)
    out_spec = out_spec or pl.BlockSpec((tm, tn), lambda i, j, k: (i, j))
    grid = (M // tm, N // tn, nk)
    if n_outer:
        swap = lambda s: pl.BlockSpec(s.block_shape, lambda j, i, k, f=s.index_map: f(i, j, k))
        in_specs, out_spec, grid = [swap(s) for s in in_specs], swap(out_spec), (N // tn, M // tm, nk)
    return pl.pallas_call(
        body, name=name, grid=grid, in_specs=in_specs, out_specs=out_spec,
        out_shape=out_shape or jax.ShapeDtypeStruct((M, N), out_dtype),
        scratch_shapes=[pltpu.VMEM((tm, tn), F32)] if nk > 1 else [],
        compiler_params=_cp("parallel", "parallel", "arbitrary"),
    )(*args)


def _rms(x, w):
    return x * lax.rsqrt(jnp.mean(x * x, axis=-1, keepdims=True) + EPS) * w


def _rmsnorm_fwd(x, w, *, name, tm=512):
    def body(x_ref, w_ref, o_ref):
        o_ref[...] = _rms(x_ref[...], w_ref[...]).astype(BF16)

    return pl.pallas_call(
        body, name=name, grid=(x.shape[0] // tm,),
        in_specs=[pl.BlockSpec((tm, D), lambda i: (i, 0)), pl.BlockSpec((1, D), lambda i: (0, 0))],
        out_specs=pl.BlockSpec((tm, D), lambda i: (i, 0)),
        out_shape=jax.ShapeDtypeStruct(x.shape, BF16), compiler_params=_cp("parallel"),
    )(x, w)


def _rmsnorm_bwd(x, w, dh, dres, *, name, tm=512):
    def body(x_ref, w_ref, dh_ref, dr_ref, dx_ref, dw_ref):
        _, vjp = jax.vjp(_rms, x_ref[...], w_ref[...])
        dx, dw = vjp(dh_ref[...])
        dx_ref[...] = dx + dr_ref[...]

        @pl.when(pl.program_id(0) == 0)
        def _():
            dw_ref[...] = jnp.zeros_like(dw_ref)

        dw_ref[...] += dw

    row = pl.BlockSpec((tm, D), lambda i: (i, 0))
    vec = pl.BlockSpec((1, D), lambda i: (0, 0))
    return pl.pallas_call(
        body, name=name, grid=(x.shape[0] // tm,), in_specs=[row, vec, row, row], out_specs=[row, vec],
        out_shape=[jax.ShapeDtypeStruct(x.shape, F32), jax.ShapeDtypeStruct((1, D), F32)],
        compiler_params=_cp("arbitrary"),
    )(x, w, dh, dres)


def _loss_head(x, w, tgt, *, name, tm=512):
    def f(xv, wv, tv):
        err = _rms(xv, wv) - tv
        per_row = jnp.sum(err * err, axis=1, keepdims=True) * (0.5 / D)
        return jnp.sum(per_row, axis=0, keepdims=True)

    def body(x_ref, w_ref, t_ref, dx_ref, dw_ref, loss_ref):
        tv = t_ref[...]
        loss, vjp = jax.vjp(lambda xv, wv: f(xv, wv, tv), x_ref[...], w_ref[...])
        dx, dw = vjp(jnp.ones((1, 1), F32))
        dx_ref[...] = dx

        @pl.when(pl.program_id(0) == 0)
        def _():
            dw_ref[...] = jnp.zeros_like(dw_ref)
            loss_ref[...] = jnp.zeros_like(loss_ref)

        dw_ref[...] += dw
        loss_ref[...] += jnp.broadcast_to(loss, loss_ref.shape)

    row = pl.BlockSpec((tm, D), lambda i: (i, 0))
    vec = pl.BlockSpec((1, D), lambda i: (0, 0))
    return pl.pallas_call(
        body, name=name, grid=(x.shape[0] // tm,), in_specs=[row, vec, row],
        out_specs=[row, vec, pl.BlockSpec((1, 128), lambda i: (0, 0))],
        out_shape=[jax.ShapeDtypeStruct(x.shape, F32), jax.ShapeDtypeStruct((1, D), F32),
                   jax.ShapeDtypeStruct((1, 128), F32)],
        compiler_params=_cp("arbitrary"),
    )(x, w, tgt)


def _swa_bias():
    G = HQ // HKV
    r, c = np.arange(G * WIN)[:, None], np.arange(2 * WIN)[None, :]
    rel = (r % WIN) + WIN - c
    band = (rel >= 0) & (rel < WIN)
    out = np.empty((2, HKV, G * WIN, 2 * WIN), np.float32)
    for h2 in range(HKV):
        slope = 2.0 ** (-8.0 * (h2 * G + r // WIN + 1) / HQ)
        out[0, h2] = np.where(band & (c >= WIN), -slope * rel, -1e30)
        out[1, h2] = np.where(band, -slope * rel, -1e30)
    return jnp.asarray(out)


def _swa_group(qg, k2, v2, sk, bias, h2):
    s = _dot_nt(qg, k2) * (DH ** -0.5) + bias
    g1 = _iota((s.shape[0], 1), 0) >> 7
    sink = jnp.zeros((s.shape[0], 1), F32)
    for gi in range(HQ // HKV):
        sink = jnp.where(g1 == gi, _col(sk, h2 * (HQ // HKV) + gi), sink)
    m = lax.stop_gradient(jnp.maximum(jnp.max(s, axis=1, keepdims=True), sink))
    p = jnp.exp(s - m)
    den = jnp.sum(p, axis=1, keepdims=True) + jnp.exp(sink - m)
    return _dot(p * (1.0 / den), v2)


def _swa_split_q(q, h2):
    G = HQ // HKV
    return jnp.concatenate([q[:, (h2 * G + g) * DH:(h2 * G + g + 1) * DH] for g in range(G)], axis=0)


def _swa_merge_q(parts):
    G = HQ // HKV
    return jnp.concatenate([parts[h2][g * WIN:(g + 1) * WIN] for h2 in range(HKV) for g in range(G)], axis=1)


def _swa_specs():
    prev = lambda i: jnp.maximum(jnp.minimum(i, T // WIN - 1) - 1, 0)
    cur = lambda i: jnp.minimum(i, T // WIN - 1)
    return [
        pl.BlockSpec((WIN, HQ * DH), lambda i: (cur(i), 0)),
        pl.BlockSpec((WIN, 128), lambda i: (prev(i), COL_K)),
        pl.BlockSpec((WIN, 128), lambda i: (cur(i), COL_K)),
        pl.BlockSpec((WIN, 128), lambda i: (prev(i), COL_V)),
        pl.BlockSpec((WIN, 128), lambda i: (cur(i), COL_V)),
        pl.BlockSpec((1, 128), lambda i: (0, 0)),
        pl.BlockSpec((None, HKV, (HQ // HKV) * WIN, 2 * WIN), lambda i: (jnp.minimum(i, 1), 0, 0, 0)),
    ]


def _swa_fwd(proj, sinks, *, name):
    def body(q_ref, kp_ref, kc_ref, vp_ref, vc_ref, sk_ref, bias_ref, o_ref):
        q, sk = q_ref[...].astype(F32), sk_ref[...]
        outs = []
        for h2 in range(HKV):
            sl = slice(h2 * DH, (h2 + 1) * DH)
            k2 = jnp.concatenate([kp_ref[:, sl], kc_ref[:, sl]], axis=0).astype(F32)
            v2 = jnp.concatenate([vp_ref[:, sl], vc_ref[:, sl]], axis=0).astype(F32)
            outs.append(_swa_group(_swa_split_q(q, h2), k2, v2, sk, bias_ref[h2], h2))
        o_ref[...] = _swa_merge_q(outs).astype(BF16)

    return pl.pallas_call(
        body, name=name, grid=(T // WIN,), in_specs=_swa_specs(),
        out_specs=pl.BlockSpec((WIN, HQ * DH), lambda i: (i, 0)),
        out_shape=jax.ShapeDtypeStruct((T, HQ * DH + GH * GD), BF16), compiler_params=_cp("parallel"),
    )(proj, proj, proj, proj, proj, sinks, _swa_bias())


def _swa_bwd(proj, sinks, do, dproj, *, name):
    NB = T // WIN
    QW = HQ * DH

    def body(q_ref, kp_ref, kc_ref, vp_ref, vc_ref, sk_ref, bias_ref, do_ref, _, out_ref, dsk_ref, cq, ck, cv):
        i = pl.program_id(0)

        @pl.when(i == 0)
        def _():
            cq[...] = jnp.zeros_like(cq)
            ck[...] = jnp.zeros_like(ck)
            cv[...] = jnp.zeros_like(cv)
            dsk_ref[...] = jnp.zeros_like(dsk_ref)

        @pl.when(i < NB)
        def _():
            q, sk, dov = q_ref[...].astype(F32), sk_ref[...], do_ref[...].astype(F32)
            dqs, dkp, dkc, dvp, dvc = [], [], [], [], []
            dsk = jnp.zeros_like(sk)
            for h2 in range(HKV):
                sl = slice(h2 * DH, (h2 + 1) * DH)
                k2 = jnp.concatenate([kp_ref[:, sl], kc_ref[:, sl]], axis=0).astype(F32)
                v2 = jnp.concatenate([vp_ref[:, sl], vc_ref[:, sl]], axis=0).astype(F32)
                _, vjp = jax.vjp(functools.partial(_swa_group, bias=bias_ref[h2], h2=h2), _swa_split_q(q, h2), k2, v2, sk)
                dqg, dk2, dv2, dsk_h = vjp(_swa_split_q(dov, h2))
                dqs.append(dqg)
                dkp.append(dk2[:WIN]); dkc.append(dk2[WIN:])
                dvp.append(dv2[:WIN]); dvc.append(dv2[WIN:])
                dsk = dsk + dsk_h
            out_ref[:, :QW] = cq[...].astype(BF16)
            out_ref[:, QW:QW + 128] = (ck[...] + jnp.concatenate(dkp, axis=1)).astype(BF16)
            out_ref[:, QW + 128:] = (cv[...] + jnp.concatenate(dvp, axis=1)).astype(BF16)
            cq[...] = _swa_merge_q(dqs)
            ck[...] = jnp.concatenate(dkc, axis=1)
            cv[...] = jnp.concatenate(dvc, axis=1)
            dsk_ref[...] += dsk

        @pl.when(i == NB)
        def _():
            out_ref[:, :QW] = cq[...].astype(BF16)
            out_ref[:, QW:QW + 128] = ck[...].astype(BF16)
            out_ref[:, QW + 128:] = cv[...].astype(BF16)

    qblk = pl.BlockSpec((WIN, QW), lambda i: (jnp.minimum(i, NB - 1), 0))
    return pl.pallas_call(
        body, name=name, grid=(NB + 1,), in_specs=_swa_specs() + [qblk, ANY],
        out_specs=[pl.BlockSpec((WIN, QW + 256), lambda i: (jnp.maximum(i - 1, 0), 0)), pl.BlockSpec((1, 128), lambda i: (0, 0))],
        out_shape=[jax.ShapeDtypeStruct(dproj.shape, dproj.dtype), jax.ShapeDtypeStruct((1, 128), F32)],
        scratch_shapes=[pltpu.VMEM((WIN, QW), F32), pltpu.VMEM((WIN, 128), F32), pltpu.VMEM((WIN, 128), F32)],
        input_output_aliases={8: 0}, compiler_params=_cp("arbitrary"),
    )(proj, proj, proj, proj, proj, sinks, _swa_bias(), do, dproj)


RC = 256
HALO = 8


def _load_ext(ref, c):
    nch = T // RC
    r0 = pl.multiple_of(c * RC, RC)
    p0 = pl.multiple_of(jnp.maximum(r0 - HALO, 0), HALO)
    n0 = pl.multiple_of(jnp.minimum(r0 + RC, T - HALO), HALO)
    prev = jnp.where(c > 0, ref[pl.ds(p0, HALO), :].astype(F32), 0.0)
    nxt = jnp.where(c < nch - 1, ref[pl.ds(n0, HALO), :].astype(F32), 0.0)
    return jnp.concatenate([prev, ref[pl.ds(r0, RC), :].astype(F32), nxt], axis=0)


def _conv_ext(xe, w, K):
    y = w[K - 1:K, :] * xe
    for s in range(1, K):
        y = y + w[K - 1 - s:K - s, :] * pltpu.roll(xe, s, 0)
    return y


def _conv_bwd_ext(xe, dye, w, K, dw_ref):
    n = xe.shape[0]
    own = slice(HALO, HALO + RC)
    dx = w[K - 1:K, :] * dye
    dw_ref[K - 1:K, :] += jnp.sum(dye[own] * xe[own], axis=0, keepdims=True)
    for s in range(1, K):
        dx = dx + w[K - 1 - s:K - s, :] * pltpu.roll(dye, n - s, 0)
        dw_ref[K - 1 - s:K - s, :] += jnp.sum(dye[own] * pltpu.roll(xe, s, 0)[own], axis=0, keepdims=True)
    return dx[own]


def _gdn_post_conv(y, is_qk):
    a = _silu(y)
    nrm = a * lax.rsqrt(jnp.sum(a * a, axis=1, keepdims=True) + EPS)
    return jnp.where(is_qk, nrm, a)


def _gdn_prep_fwd(proj, conv_w, *, name):
    nblk = 3 * GH

    def body(x_ref, w_ref, o_ref):
        is_qk = pl.program_id(0) < 2 * GH
        w = w_ref[...]

        def chunk(c, carry):
            y = _conv_ext(_load_ext(x_ref, c), w, GK)[HALO:HALO + RC]
            o_ref[pl.ds(pl.multiple_of(c * RC, RC), RC), :] = _gdn_post_conv(y, is_qk)
            return carry

        lax.fori_loop(0, T // RC, chunk, 0)

    return pl.pallas_call(
        body, name=name, grid=(nblk,),
        in_specs=[pl.BlockSpec((T, 128), lambda j: (0, COL_G + j)), pl.BlockSpec((GK, 128), lambda j: (0, j))],
        out_specs=pl.BlockSpec((T, 128), lambda j: (0, j)),
        out_shape=jax.ShapeDtypeStruct((T, nblk * 128), F32), compiler_params=_cp("parallel"),
    )(proj, conv_w)


def _gdn_prep_bwd(proj, conv_w, dout, dproj, *, name):
    nblk = 3 * GH

    def body(x_ref, w_ref, d_ref, _, dx_ref, dw_ref):
        is_qk = pl.program_id(0) < 2 * GH
        w = w_ref[...]
        dw_ref[...] = jnp.zeros_like(dw_ref)

        def chunk(c, carry):
            xe = _load_ext(x_ref, c)
            _, vjp = jax.vjp(lambda y: _gdn_post_conv(y, is_qk), _conv_ext(xe, w, GK))
            (dye,) = vjp(_load_ext(d_ref, c))
            dx_ref[pl.ds(pl.multiple_of(c * RC, RC), RC), :] = _conv_bwd_ext(xe, dye, w, GK, dw_ref).astype(BF16)
            return carry

        lax.fori_loop(0, T // RC, chunk, 0)

    return pl.pallas_call(
        body, name=name, grid=(nblk,),
        in_specs=[pl.BlockSpec((T, 128), lambda j: (0, COL_G + j)), pl.BlockSpec((GK, 128), lambda j: (0, j)),
                  pl.BlockSpec((T, 128), lambda j: (0, j)), ANY],
        out_specs=[pl.BlockSpec((T, 128), lambda j: (0, COL_G + j)), pl.BlockSpec((GK, 128), lambda j: (0, j))],
        out_shape=[jax.ShapeDtypeStruct(dproj.shape, dproj.dtype), jax.ShapeDtypeStruct((GK, nblk * 128), F32)],
        input_output_aliases={3: 0}, compiler_params=_cp("parallel"),
    )(proj, conv_w, dout, dproj)


def _ffn_post_conv(y, b, up):
    return _silu(y + b) * up


def _ffn_act_fwd(gu, conv_w, conv_b, *, name, carry=None):
    nblk = DFF // 128

    def body(g_ref, u_ref, w_ref, b_ref, o_ref):
        w, b = w_ref[...], b_ref[...]

        def chunk(c, carry):
            rows = pl.ds(pl.multiple_of(c * RC, RC), RC)
            y = _conv_ext(_load_ext(g_ref, c), w, FK)[HALO:HALO + RC]
            o_ref[rows, :] = _ffn_post_conv(y, b, u_ref[rows, :].astype(F32)).astype(BF16)
            return carry

        lax.fori_loop(0, T // RC, chunk, 0)

    (act,), carried = _seq_call(
        body, carry, name=name, steps=nblk,
        in_specs=[pl.BlockSpec((T, 128), lambda j: (0, j)), pl.BlockSpec((T, 128), lambda j: (0, nblk + j)),
                  pl.BlockSpec((FK, 128), lambda j: (0, j)), pl.BlockSpec((1, 128), lambda j: (0, j))],
        out_specs=[pl.BlockSpec((T, 128), lambda j: (0, j))], out_shape=[jax.ShapeDtypeStruct((T, DFF), BF16)],
        args=(gu, gu, conv_w, conv_b))
    return act, carried


def _ffn_act_bwd(gu, conv_w, conv_b, dact, *, name):
    nblk = DFF // 128

    def body(g_ref, u_ref, w_ref, b_ref, d_ref, dgu_ref, dw_ref, db_ref):
        dg_ref, du_ref = dgu_ref.at[0], dgu_ref.at[1]
        w, b = w_ref[...], b_ref[...]
        dw_ref[...] = jnp.zeros_like(dw_ref)
        db_ref[...] = jnp.zeros_like(db_ref)

        def chunk(c, carry):
            rows = pl.ds(pl.multiple_of(c * RC, RC), RC)
            xe = _load_ext(g_ref, c)
            ue = _load_ext(u_ref, c)
            _, vjp = jax.vjp(_ffn_post_conv, _conv_ext(xe, w, FK), b, ue)
            dye, db, due = vjp(_load_ext(d_ref, c))
            du_ref[rows, :] = due[HALO:HALO + RC].astype(BF16)
            db_ref[...] += jnp.sum(dye[HALO:HALO + RC], axis=0, keepdims=True)
            dg_ref[rows, :] = _conv_bwd_ext(xe, dye, w, FK, dw_ref).astype(BF16)
            return carry

        lax.fori_loop(0, T // RC, chunk, 0)

    col = pl.BlockSpec((T, 128), lambda j: (0, j))
    return pl.pallas_call(
        body, name=name, grid=(nblk,),
        in_specs=[col, pl.BlockSpec((T, 128), lambda j: (0, nblk + j)), pl.BlockSpec((FK, 128), lambda j: (0, j)),
                  pl.BlockSpec((1, 128), lambda j: (0, j)), col],
        out_specs=[pl.BlockSpec((2, T, 128), lambda j: (0, 0, j)), pl.BlockSpec((FK, 128), lambda j: (0, j)),
                   pl.BlockSpec((1, 128), lambda j: (0, j))],
        out_shape=[jax.ShapeDtypeStruct((2, T, DFF), BF16), jax.ShapeDtypeStruct((FK, DFF), F32),
                   jax.ShapeDtypeStruct((1, DFF), F32)],
        compiler_params=_cp("parallel"),
    )(gu, gu, conv_w, conv_b, dact)


def _gdn_gates(ba, alog, dtb):
    beta = jax.nn.sigmoid(ba)
    g = -jnp.exp(alog) * jax.nn.softplus(ba + dtb)
    tril = (_iota((GC, GC), 0) >= _iota((GC, GC), 1)).astype(F32)
    return beta, jnp.dot(tril, g, precision=HIGHEST, preferred_element_type=F32)


def _hmap(f, *lists):
    return [f(*xs) for xs in zip(*lists)]


def _gdn_cols(beta_all, gc_all):
    return [_col(beta_all, h) for h in range(GH)], [_col(gc_all, GH + h) for h in range(GH)]


def _gdn_decay(Gcs):
    r, c = _iota((GC, GC), 0), _iota((GC, GC), 1)
    eye, ones = (r == c).astype(F32), jnp.ones((GC, GC), F32)
    grows = _hmap(lambda G: jnp.dot(ones, eye * G, precision=HIGHEST, preferred_element_type=F32), Gcs)
    return _hmap(lambda G, grow: jnp.exp(jnp.where(r >= c, G - grow, -1e30)), Gcs, grows)


def _gdn_pre(beta_all, gc_all):
    betas, Gcs = _gdn_cols(beta_all, gc_all)
    return betas, Gcs, _gdn_decay(Gcs)


def _gdn_A(ks, betas, decays):
    strict = _iota((GC, GC), 0) > _iota((GC, GC), 1)
    kk = _hmap(lambda k, b: _dot_nt(k * b, k), ks, betas)
    return _hmap(lambda a, d: jnp.where(strict, a * d, 0.0), kk, decays)


def _tri_inv(As):
    eye = (_iota((GC, GC), 0) == _iota((GC, GC), 1)).astype(F32)
    Tms, Ps = [eye - A for A in As], As
    for _ in range(GC.bit_length() - 2):
        Ps = _hmap(lambda P: _dot(P, P), Ps)
        Tms = _hmap(lambda Tm, P: Tm + _dot(Tm, P), Tms, Ps)
    return Tms


def _gdn_chunk(qs, ks, vs, betas, Gcs, decays, Ss, Tms):
    eGs = _hmap(jnp.exp, Gcs)
    us = _hmap(lambda Tm, v, b: _dot(Tm, v * b), Tms, vs, betas)
    ws = _hmap(lambda Tm, k, b, eG: _dot(Tm, k * b * eG), Tms, ks, betas, eGs)
    qss = [q * (GD ** -0.5) for q in qs]
    qks = _hmap(lambda q, k, d: _dot_nt(q, k) * d, qss, ks, decays)
    glasts = [jnp.sum(jnp.where(_iota(G.shape, 0) == GC - 1, G, 0.0), axis=0, keepdims=True) for G in Gcs]
    kds = _hmap(lambda k, gl, G: k * jnp.exp(gl - G), ks, glasts, Gcs)
    v_news = _hmap(lambda u, w, S: u - _dot(w, S), us, ws, Ss)
    qS = _hmap(lambda q, eG, S: _dot(q * eG, S), qss, eGs, Ss)
    os = _hmap(lambda a, qk, vn: a + _dot(qk, vn), qS, qks, v_news)
    S_news = _hmap(lambda S, gl, kd, vn: S * jnp.exp(gl) + _dot_tn(kd, vn), Ss, glasts, kds, v_news)
    return os, S_news


def _gdn_chunk_fwd(qkv_c, ba, alog, dtb, *, name, carry=None):
    W3 = 3 * GH * GD

    def body(x_ref, ba_ref, al_ref, dt_ref, o_ref, s_ref, t_ref, S):
        @pl.when(pl.program_id(0) == 0)
        def _():
            S[...] = jnp.zeros_like(S)

        beta_all, gc_all = _gdn_gates(ba_ref[...], al_ref[...], dt_ref[...])
        qs, ks, vs = ([x_ref[:, (p * GH + h) * GD:(p * GH + h + 1) * GD] for h in range(GH)] for p in range(3))
        betas, Gcs, decays = _gdn_pre(beta_all, gc_all)
        Tms = _tri_inv(_gdn_A(ks, betas, decays))
        Ss = [S[h] for h in range(GH)]
        os, S_news = _gdn_chunk(qs, ks, vs, betas, Gcs, decays, Ss, Tms)
        for h in range(GH):
            s_ref[0, h] = Ss[h]
            t_ref[0, h] = Tms[h]
            o_ref[:, h * GD:(h + 1) * GD] = os[h]
            S[h] = S_news[h]

    vec = pl.BlockSpec((1, 128), lambda n: (0, 0))
    return _seq_call(
        body, carry, name=name, steps=N_CHUNK,
        in_specs=[pl.BlockSpec((GC, W3), lambda n: (n, 0)), pl.BlockSpec((GC, 128), lambda n: (n, 0)), vec, vec],
        out_specs=[pl.BlockSpec((GC, GH * GD), lambda n: (n, 0)),
                   pl.BlockSpec((1, GH, GD, GD), lambda n: (n, 0, 0, 0)),
                   pl.BlockSpec((1, GH, GC, GC), lambda n: (n, 0, 0, 0))],
        out_shape=[jax.ShapeDtypeStruct((T, GH * GD), F32), jax.ShapeDtypeStruct((N_CHUNK, GH, GD, GD), F32),
                   jax.ShapeDtypeStruct((N_CHUNK, GH, GC, GC), F32)],
        scratch_shapes=[pltpu.VMEM((GH, GD, GD), F32)], args=(qkv_c, ba, alog, dtb))


def _gdn_chunk_bwd(qkv_c, ba, alog, dtb, s_all, t_all, do, dproj, *, name, carry=None):
    W3 = 3 * GH * GD
    rev = lambda n: N_CHUNK - 1 - n

    def body(x_ref, ba_ref, al_ref, dt_ref, s_ref, t_ref, do_ref, _, dx_ref, dba_ref, dal_ref, ddt_ref, dS):
        @pl.when(pl.program_id(0) == 0)
        def _():
            dS[...] = jnp.zeros_like(dS)
            dal_ref[...] = jnp.zeros_like(dal_ref)
            ddt_ref[...] = jnp.zeros_like(ddt_ref)

        (beta_all, gc_all), vjp_gates = jax.vjp(_gdn_gates, ba_ref[...], al_ref[...], dt_ref[...])
        qs, ks, vs = ([x_ref[:, (p * GH + h) * GD:(p * GH + h + 1) * GD] for h in range(GH)] for p in range(3))
        Tms = [t_ref[0, h] for h in range(GH)]
        Ss = [s_ref[0, h] for h in range(GH)]
        dos = [do_ref[:, h * GD:(h + 1) * GD] for h in range(GH)]
        dSs = [dS[h] for h in range(GH)]

        (betas, Gcs, decays), vjp_pre = jax.vjp(_gdn_pre, beta_all, gc_all)
        _, vjp = jax.vjp(_gdn_chunk, qs, ks, vs, betas, Gcs, decays, Ss, Tms)
        dqs, dks, dvs, dbetas, dGcs, ddecays, dS_prev, dTs = vjp((dos, dSs))
        dXs = _hmap(_dot_nt, dTs, Tms)
        dAs = _hmap(lambda Tm, dX: -_dot_tn(Tm, dX), Tms, dXs)
        _, vjp_a = jax.vjp(_gdn_A, ks, betas, decays)
        dks2, dbetas2, ddecays2 = vjp_a(dAs)
        add = lambda a, b: _hmap(jnp.add, a, b)
        db_all, dg_all = vjp_pre((add(dbetas, dbetas2), dGcs, add(ddecays, ddecays2)))
        for h in range(GH):
            dS[h] = dS_prev[h]
            dx_ref[:, h * GD:(h + 1) * GD] = dqs[h]
            dx_ref[:, (GH + h) * GD:(GH + h + 1) * GD] = dks[h] + dks2[h]
            dx_ref[:, (2 * GH + h) * GD:(2 * GH + h + 1) * GD] = dvs[h]
        dba, dal, ddt = vjp_gates((db_all, dg_all))
        dba_ref[...] = dba.astype(BF16)
        dal_ref[...] += dal
        ddt_ref[...] += ddt

    vec = pl.BlockSpec((1, 128), lambda n: (0, 0))
    return _seq_call(
        body, carry, name=name, steps=N_CHUNK,
        in_specs=[pl.BlockSpec((GC, W3), lambda n: (rev(n), 0)), pl.BlockSpec((GC, 128), lambda n: (rev(n), 0)),
                  vec, vec, pl.BlockSpec((1, GH, GD, GD), lambda n: (rev(n), 0, 0, 0)),
                  pl.BlockSpec((1, GH, GC, GC), lambda n: (rev(n), 0, 0, 0)),
                  pl.BlockSpec((GC, GH * GD), lambda n: (rev(n), 0)),
                  ANY],
        out_specs=[pl.BlockSpec((GC, W3), lambda n: (rev(n), 0)), pl.BlockSpec((GC, 128), lambda n: (rev(n), COL_BA)), vec, vec],
        out_shape=[jax.ShapeDtypeStruct((T, W3), F32), jax.ShapeDtypeStruct(dproj.shape, dproj.dtype),
                   jax.ShapeDtypeStruct((1, 128), F32), jax.ShapeDtypeStruct((1, 128), F32)],
        scratch_shapes=[pltpu.VMEM((GH, GD, GD), F32)], aliases={7: 1},
        args=(qkv_c, ba, alog, dtb, s_all, t_all, do, dproj))


def _gdn_post(o, z, nw):
    return o * lax.rsqrt(jnp.mean(o * o, axis=-1, keepdims=True) + EPS) * nw * _silu(z)


def _gdn_post_fwd(o_raw, proj, nw, mixed, *, name, tm=512):
    def body(o_ref, z_ref, w_ref, _, out_ref):
        out_ref[...] = _gdn_post(o_ref[...], z_ref[...].astype(F32), w_ref[...]).astype(BF16)

    return pl.pallas_call(
        body, name=name, grid=(T // tm, GH),
        in_specs=[pl.BlockSpec((tm, GD), lambda i, h: (i, h)), pl.BlockSpec((tm, GD), lambda i, h: (i, COL_Z + h)),
                  pl.BlockSpec((1, GD), lambda i, h: (0, 0)), ANY],
        out_specs=pl.BlockSpec((tm, GD), lambda i, h: (i, HQ * DH // GD + h)),
        out_shape=jax.ShapeDtypeStruct(mixed.shape, mixed.dtype), input_output_aliases={3: 0},
        compiler_params=_cp("parallel", "parallel"),
    )(o_raw, proj, nw, mixed)


def _gdn_post_bwd(o_raw, proj, nw, dmixed, *, name, tm=512):
    def body(o_ref, z_ref, w_ref, d_ref, do_ref, dz_ref, dw_ref):
        _, vjp = jax.vjp(_gdn_post, o_ref[...], z_ref[...].astype(F32), w_ref[...])
        do, dz, dw = vjp(d_ref[...])
        do_ref[...] = do
        dz_ref[...] = dz.astype(BF16)

        @pl.when((pl.program_id(0) == 0) & (pl.program_id(1) == 0))
        def _():
            dw_ref[...] = jnp.zeros_like(dw_ref)

        dw_ref[...] += dw

    blk = pl.BlockSpec((tm, GD), lambda i, h: (i, h))
    vec = pl.BlockSpec((1, GD), lambda i, h: (0, 0))
    return pl.pallas_call(
        body, name=name, grid=(T // tm, GH),
        in_specs=[blk, pl.BlockSpec((tm, GD), lambda i, h: (i, COL_Z + h)), vec,
                  pl.BlockSpec((tm, GD), lambda i, h: (i, GH + h))],
        out_specs=[blk, pl.BlockSpec((tm, GD), lambda i, h: (i, COL_Z + h)), vec],
        out_shape=[jax.ShapeDtypeStruct((T, GH * GD), F32), jax.ShapeDtypeStruct((T, N_PROJ), BF16),
                   jax.ShapeDtypeStruct((1, GD), F32)],
        compiler_params=_cp("arbitrary", "arbitrary"),
    )(o_raw, proj, nw, dmixed)


def _adamw(w, g, m, v, *, name):
    shape = w.shape
    cols = shape[-1]
    w2, g2, m2, v2 = (a.reshape(-1, cols) for a in (w, g, m, v))
    rows = w2.shape[0]
    tr = next((t for t in (512, 256, 128, 64, 32, 16, 8) if rows % t == 0), rows)

    def body(w_ref, g_ref, m_ref, v_ref, d_ref, nm_ref, nv_ref):
        gv = g_ref[...]
        nm = B1 * m_ref[...] + (1.0 - B1) * gv
        nv = B2 * v_ref[...] + (1.0 - B2) * jnp.square(gv)
        m_hat = nm / (1.0 - B1 ** STEP)
        v_hat = nv / (1.0 - B2 ** STEP)
        d_ref[...] = -LR * (m_hat / (jnp.sqrt(v_hat) + AEPS) + WD * w_ref[...])
        nm_ref[...] = nm
        nv_ref[...] = nv

    blk = pl.BlockSpec((tr, cols), lambda i: (i, 0))
    out = pl.pallas_call(
        body, name=name, grid=(rows // tr,), in_specs=[blk] * 4, out_specs=[blk] * 3,
        out_shape=[jax.ShapeDtypeStruct((rows, cols), F32)] * 3, compiler_params=_cp("parallel"),
    )(w2, g2, m2, v2)
    return tuple(o.reshape(shape) for o in out)


def _layer_weights(w_in):
    return jnp.pad(w_in, ((0, 0), (0, N_PROJ - w_in.shape[1])))


def _layer_params(attn_norm, sinks, gcw, a_log, dt_bias, gnw, ffn_norm, fcw, fcb):
    lanes4 = lambda v: jnp.pad(v, (GH, 128 - 2 * GH))[None]
    return dict(attn_norm=attn_norm[None], sinks=jnp.pad(sinks, (0, 128 - HQ))[None], gcw=gcw, alog=lanes4(a_log),
                dtb=lanes4(dt_bias), gnw=gnw[None], ffn_norm=ffn_norm[None], fcw=fcw, fcb=fcb[None])


def _mixer_fwd(x, W, P, l, carry=None):
    n = lambda s: f"l{l}_{s}"
    h = _rmsnorm_fwd(x, P["attn_norm"], name=n("norm1"))
    proj = _mm(h, W["all"], name=n("proj"), mode="nn", dims=(T, N_MAIN, D), tm=512, tn=N_MAIN, tk=D, out_dtype=BF16)
    ba = _mm(h, W["all"], name=n("proj_ba"), mode="nn", dims=(T, 128, D), tm=1024, tn=128, tk=D,
             b_spec=pl.BlockSpec((D, 128), lambda i, j, k: (k, COL_BA)))
    mixed = _swa_fwd(proj, P["sinks"], name=n("swa"))
    qkv_c = _gdn_prep_fwd(proj, P["gcw"], name=n("gdn_prep"))
    (o_raw, s_all, t_all), carried = _gdn_chunk_fwd(qkv_c, ba, P["alog"], P["dtb"], name=n("gdn_chunk"), carry=carry)
    mixed = _gdn_post_fwd(o_raw, proj, P["gnw"], mixed, name=n("gdn_post"))
    x1 = _mm(mixed, W["out"], res=x, name=n("out_proj"), mode="nn", dims=(T, D, D), tm=512, tn=D, tk=D)
    saved = dict(x=x, h=h, proj=proj, ba=ba, qkv_c=qkv_c, o_raw=o_raw, s_all=s_all, t_all=t_all, mixed=mixed, x1=x1)
    return x1, saved, carried


def _ffn_fwd(x1, W, P, l, carry=None):
    n = lambda s: f"l{l}_{s}"
    h2 = _rmsnorm_fwd(x1, P["ffn_norm"], name=n("norm2"))
    gu = _mm(h2, W["ffn"], name=n("ffn_in"), mode="nn", dims=(T, 2 * DFF, D), tm=512, tn=FFN_CW, tk=D, out_dtype=BF16,
             b_spec=pl.BlockSpec((None, D, FFN_CW), lambda i, j, k: (j, k, 0)), n_outer=True)
    act, carried = _ffn_act_fwd(gu, P["fcw"], P["fcb"], name=n("ffn_act"), carry=carry)
    x2 = _mm(act, W["down"], res=x1, name=n("ffn_down"), mode="nn", dims=(T, D, DFF), tm=512, tn=D, tk=DFF)
    return x2, dict(h2=h2, gu=gu, act=act), carried


def _ffn_bwd(dx2, sv, W, P, l):
    n = lambda s: f"l{l}_{s}"
    CW = FFN_CW
    dact = _mm(dx2, W["down"], name=n("d_act"), mode="nt", dims=(T, DFF, D), tm=512, tn=DFF // 2, tk=D, out_dtype=BF16,
               n_outer=True)
    g_down = _mm(sv["act"], dx2, name=n("g_down"), mode="tn", dims=(DFF, D, T), tm=DFF // 2, tn=D, tk=min(T, 2048), out_dtype=BF16)
    dgu, g_fcw, g_fcb = _ffn_act_bwd(sv["gu"], P["fcw"], P["fcb"], dact, name=n("d_ffn_act"))
    dh2 = _mm(dgu, W["ffn"], name=n("d_h2"), mode="nt", dims=(T, D, 2 * DFF), tm=1024, tn=D, tk=CW,
              a_spec=pl.BlockSpec((None, 1024, CW), lambda i, j, k: (k // 2, i, k % 2)),
              b_spec=pl.BlockSpec((None, D, CW), lambda i, j, k: (k, j, 0)))
    g_ffn = _mm(sv["h2"], dgu, name=n("g_ffn"), mode="tn", dims=(D, 2 * DFF, T), tm=512, tn=CW, tk=T, n_outer=True,
                b_spec=pl.BlockSpec((None, T, CW), lambda i, j, k: (j // 2, k, j % 2)),
                out_spec=pl.BlockSpec((None, 512, CW), lambda i, j, k: (j, i, 0)),
                out_shape=jax.ShapeDtypeStruct((N_CHIP, D, CW), BF16), out_dtype=BF16)
    dx1, g_ffn_norm = _rmsnorm_bwd(sv["x1"], P["ffn_norm"], dh2, dx2, name=n("d_norm2"))
    return dx1, dict(ffn_norm=g_ffn_norm[0], w_ffn_in=g_ffn, ffn_conv_w=g_fcw, ffn_conv_b=g_fcb[0], w_down=g_down)


def _mixer_bwd(dx1, sv, W, P, l, carry=None):
    n = lambda s: f"l{l}_{s}"
    dmixed = _mm(dx1, W["out"], name=n("d_mixed"), mode="nt", dims=(T, D, D), tm=512, tn=D, tk=D)
    g_out = _mm(sv["mixed"], dx1, name=n("g_out"), mode="tn", dims=(D, D, T), tm=D, tn=D, tk=min(T, 2048), out_dtype=BF16)
    do_raw, dproj, g_gnw = _gdn_post_bwd(sv["o_raw"], sv["proj"], P["gnw"], dmixed, name=n("d_gdn_post"))
    (dqkv_c, dproj, g_alog, g_dtb), carried = _gdn_chunk_bwd(
        sv["qkv_c"], sv["ba"], P["alog"], P["dtb"], sv["s_all"], sv["t_all"], do_raw, dproj, name=n("d_gdn_chunk"), carry=carry)
    dproj, g_gcw = _gdn_prep_bwd(sv["proj"], P["gcw"], dqkv_c, dproj, name=n("d_gdn_prep"))
    dproj, g_sinks = _swa_bwd(sv["proj"], P["sinks"], dmixed, dproj, name=n("d_swa"))
    dh = _mm(dproj, W["all"], name=n("d_h"), mode="nt", dims=(T, D, N_PROJ), tm=512, tn=D, tk=N_PROJ)
    g_all = _mm(sv["h"], dproj, name=n("g_in"), mode="tn", dims=(D, N_PROJ, T), tm=512, tn=N_PROJ, tk=min(T, 2048), out_dtype=BF16)
    dx, g_attn_norm = _rmsnorm_bwd(sv["x"], P["attn_norm"], dh, dx1, name=n("d_norm1"))
    grads = dict(attn_norm=g_attn_norm[0], w_in=g_all, attn_sinks=g_sinks[0, :HQ], gdn_conv_w=g_gcw,
                 gdn_a_log=g_alog[0, GH:2 * GH], gdn_dt_bias=g_dtb[0, GH:2 * GH], gdn_norm=g_gnw[0], w_out=g_out)
    return dx, grads, carried


def _pos():
    return lax.axis_index("x"), lax.axis_index("y"), lax.axis_index("c")


def _other_chips(x, y):
    return [(1 - x, y), (x, 1 - y), (1 - x, 1 - y)]


def _remote(src, dst, send_sems, recv_sems, k, to):
    return pltpu.make_async_remote_copy(src_ref=src, dst_ref=dst, send_sem=send_sems.at[k], recv_sem=recv_sems.at[k],
                                        device_id=to, device_id_type=MESH)


def _run_carry(carry, *, name):
    ci, co = len(carry.arrays), len(carry.out_shapes)

    def body(*refs):
        sends, recvs = carry.plan(refs[:ci], refs[ci:ci + co], refs[-2], refs[-1])
        for cp in sends:
            cp.start()
        for cp in recvs:
            cp.wait_recv()
        for cp in sends:
            cp.wait_send()

    return list(pl.pallas_call(
        body, name=name, in_specs=[ANY] * ci, out_specs=[ANY] * co, out_shape=carry.out_shapes,
        scratch_shapes=[pltpu.SemaphoreType.DMA((carry.nsem,)), pltpu.SemaphoreType.DMA((carry.nsem,))],
    )(*carry.arrays))


def _chip_index():
    return 2 * lax.axis_index("x") + lax.axis_index("y")


def _gather_carry(shards):
    def plan(srcs, outs, send_sems, recv_sems):
        x, y, c = _pos()
        chip = 2 * x + y
        others = [(k, px, py, 2 * px + py) for k, (px, py) in enumerate(_other_chips(x, y))]
        sends = [_remote(srcs[t], outs[t].at[chip], send_sems, recv_sems, 3 * t + k, (px, py, c))
                 for t in range(len(srcs)) for k, px, py, _ in others]
        recvs = [_remote(srcs[t], outs[t].at[j], send_sems, recv_sems, 3 * t + k, (x, y, c))
                 for t in range(len(srcs)) for k, _, _, j in others]
        return sends, recvs

    return _Carry(shards, [jax.ShapeDtypeStruct((N_CHIP,) + s.shape, s.dtype) for s in shards], 3 * len(shards), plan)


def _fill_own(outs, shards):
    return [lax.dynamic_update_index_in_dim(o, s, _chip_index(), 0) for o, s in zip(outs, shards)]


HBM_SPEC = pl.BlockSpec(memory_space=pltpu.HBM)
SEM_SPEC = pl.BlockSpec(memory_space=pltpu.SEMAPHORE)
DATAFLOW = pltpu.SideEffectType.DATAFLOW_SIDE_EFFECTING


def _split_start(carry, after, *, name):
    ci, co = len(carry.arrays), len(carry.out_shapes)

    def body(*refs):
        srcs, lands, send_sems, recv_sems, token = refs[:ci], refs[ci:ci + co], refs[ci + co + 1], refs[ci + co + 2], refs[-1]
        for cp in carry.plan(srcs, lands, send_sems, recv_sems)[0]:
            cp.start()
        token[...] = jnp.zeros_like(token)

    lands = [lax.empty(s.shape, s.dtype) for s in carry.out_shapes]
    hbm = lambda a: pltpu.with_memory_space_constraint(a, pltpu.HBM)
    out = pl.pallas_call(
        body, name=name, in_specs=[HBM_SPEC] * (ci + co) + [ANY],
        out_specs=[SEM_SPEC, SEM_SPEC] + [HBM_SPEC] * (ci + co) + [VMEM_SPEC],
        out_shape=[pltpu.SemaphoreType.DMA((carry.nsem,)), pltpu.SemaphoreType.DMA((carry.nsem,))]
        + [pltpu.HBM(a.shape, a.dtype) for a in carry.arrays + lands] + [jax.ShapeDtypeStruct((8, 128), F32)],
        input_output_aliases={t: 2 + t for t in range(ci + co)},
        compiler_params=pltpu.CompilerParams(has_side_effects=DATAFLOW),
    )(*[hbm(a) for a in carry.arrays + lands], after)
    return out[0], out[1], list(out[2:2 + ci]), list(out[2 + ci:2 + ci + co]), out[-1]


def _split_wait(carry, send_sems, recv_sems, srcs, lands, after, *, name):
    ci, co = len(srcs), len(lands)

    def body(*refs):
        sends, recvs = carry.plan(refs[:ci], refs[ci:ci + co], refs[ci + co], refs[ci + co + 1])
        for cp in sends:
            cp.wait_send()
        for cp in recvs:
            cp.wait_recv()

    out = pl.pallas_call(
        body, name=name, in_specs=[HBM_SPEC] * (ci + co) + [SEM_SPEC, SEM_SPEC, ANY], out_specs=[HBM_SPEC] * (ci + co),
        out_shape=[pltpu.HBM(a.shape, a.dtype) for a in list(srcs) + list(lands)],
        input_output_aliases={t: t for t in range(ci + co)},
        compiler_params=pltpu.CompilerParams(has_side_effects=DATAFLOW),
    )(*srcs, *lands, send_sems, recv_sems, after)
    return list(out[:ci]), list(out[ci:])


def _gather_start(shards, after, *, name):
    return _split_start(_gather_carry(shards), after, name=name)


def _gather_wait(send_sems, recv_sems, shards, lands, after, *, name):
    srcs, got = _split_wait(_gather_carry(shards), send_sems, recv_sems, shards, lands, after, name=name)
    return _fill_own(got, srcs)


def _ag_small(v, *, name):
    m, n = v.shape

    def body(x_ref, out_ref, red_ref, send_sems, recv_sems, local_sem):
        x, y, c = _pos()
        me, sibling = (x, y, c), (x, y, 1 - c)
        chips = _other_chips(x, y)
        rows = lambda px, py, pc: out_ref.at[pl.ds(pl.multiple_of((4 * px + 2 * py + pc) * m, 8), m), :]
        mine = pltpu.make_async_copy(x_ref, rows(*me), local_sem)
        mine.start()
        first = [_remote(x_ref, rows(*me), send_sems, recv_sems, 0, sibling)]
        first += [_remote(x_ref, rows(*me), send_sems, recv_sems, 1 + k, (*chip, c)) for k, chip in enumerate(chips)]
        for cp in first:
            cp.start()
        passed = [_remote(rows(*chip, c), rows(*chip, c), send_sems, recv_sems, 4 + k, sibling) for k, chip in enumerate(chips)]
        for k, chip in enumerate(chips):
            _remote(rows(*chip, c), rows(*chip, c), send_sems, recv_sems, 1 + k, me).wait_recv()
            passed[k].start()
        _remote(rows(*sibling), rows(*sibling), send_sems, recv_sems, 0, me).wait_recv()
        for k, chip in enumerate(chips):
            _remote(rows(*chip, 1 - c), rows(*chip, 1 - c), send_sems, recv_sems, 4 + k, me).wait_recv()
        for cp in first + passed:
            cp.wait_send()
        mine.wait()
        acc = out_ref[0:m, :]
        for d in range(1, N_DEV):
            acc = acc + out_ref[d * m:(d + 1) * m, :]
        red_ref[...] = acc

    return pl.pallas_call(
        body, name=name, in_specs=[VMEM_SPEC], out_specs=[VMEM_SPEC, VMEM_SPEC],
        out_shape=[jax.ShapeDtypeStruct((N_DEV * m, n), v.dtype), jax.ShapeDtypeStruct((m, n), v.dtype)],
        scratch_shapes=[pltpu.SemaphoreType.DMA((7,)), pltpu.SemaphoreType.DMA((7,)), pltpu.SemaphoreType.DMA],
    )(v)


def _halves(ref, c):
    rh = ref.shape[1] // 2
    return ref.at[:, pl.ds(pl.multiple_of(c * rh, 16), rh), :]


def _rs_swap(gs, *, name):
    nt = len(gs)

    def body(*refs):
        g, theirs = refs[:nt], refs[nt:2 * nt]
        send_sems, recv_sems = refs[2 * nt:]
        x, y, c = _pos()
        swaps = [_remote(_halves(g[t], 1 - c), theirs[t], send_sems, recv_sems, t, (x, y, 1 - c)) for t in range(nt)]
        for cp in swaps:
            cp.start()
        for cp in swaps:
            cp.wait()

    return pl.pallas_call(
        body, name=name, in_specs=[ANY] * nt, out_specs=[ANY] * nt,
        out_shape=[jax.ShapeDtypeStruct((a.shape[0], a.shape[1] // 2, a.shape[2]), a.dtype) for a in gs],
        scratch_shapes=[pltpu.SemaphoreType.DMA((nt,)), pltpu.SemaphoreType.DMA((nt,))],
    )(*gs)


def _exchange_carry(ss):
    def plan(s, out, send_sems, recv_sems):
        x, y, c = _pos()
        chip = 2 * x + y
        others = [(k, px, py, 2 * px + py) for k, (px, py) in enumerate(_other_chips(x, y))]
        sends = [_remote(s[t].at[j], out[t].at[chip], send_sems, recv_sems, 3 * t + k, (px, py, c))
                 for t in range(len(s)) for k, px, py, j in others]
        recvs = [_remote(s[t].at[j], out[t].at[j], send_sems, recv_sems, 3 * t + k, (x, y, c))
                 for t in range(len(s)) for k, _, _, j in others]
        return sends, recvs

    return _Carry(ss, [jax.ShapeDtypeStruct(a.shape, a.dtype) for a in ss], 3 * len(ss), plan)


def _fill_own_slab(outs, ss):
    chip = _chip_index()
    return [lax.dynamic_update_index_in_dim(o, lax.dynamic_index_in_dim(s, chip, 0, keepdims=False), chip, 0)
            for o, s in zip(outs, ss)]


def _rs_join(rs, *, name):
    nt = len(rs)

    def body(*refs):
        r, theirs = refs[:nt], refs[nt:2 * nt]
        send_sems, recv_sems = refs[2 * nt:]
        x, y, c = _pos()
        swaps = [_remote(r[t], theirs[t], send_sems, recv_sems, t, (x, y, 1 - c)) for t in range(nt)]
        for cp in swaps:
            cp.start()
        for cp in swaps:
            cp.wait()

    theirs = pl.pallas_call(
        body, name=name, in_specs=[ANY] * nt, out_specs=[ANY] * nt,
        out_shape=[jax.ShapeDtypeStruct(a.shape, a.dtype) for a in rs],
        scratch_shapes=[pltpu.SemaphoreType.DMA((nt,)), pltpu.SemaphoreType.DMA((nt,))],
    )(*rs)
    first = lax.axis_index("c") == 0
    return [jnp.concatenate([jnp.where(first, a, b), jnp.where(first, b, a)], axis=0) for a, b in zip(rs, theirs)]


def _row_tile(rows, dtype):
    unit = 16 if dtype == BF16 else 8
    if rows <= 512:
        return rows
    return next((t for t in (512, 256, 128, 64, 32, 16, 8) if rows % t == 0 and t % unit == 0), rows)


def _add_pair(a, b, *, name):
    n, rh, cols = a.shape
    tr = _row_tile(rh, BF16)

    def body(a_ref, b_ref, o_ref):
        o_ref[...] = (a_ref[...].astype(F32) + b_ref[...].astype(F32)).astype(BF16)

    blk = pl.BlockSpec((None, tr, cols), lambda j, i: (j, i, 0))
    return pl.pallas_call(
        body, name=name, grid=(n, rh // tr), in_specs=[blk, blk], out_specs=blk,
        out_shape=jax.ShapeDtypeStruct(a.shape, BF16), compiler_params=_cp("parallel", "parallel"),
    )(a, b)


def _sum_chips(b, *, name):
    n, rh, cols = b.shape
    tr = _row_tile(rh, BF16)

    def body(b_ref, o_ref):
        acc = b_ref[0].astype(F32)
        for j in range(1, n):
            acc = acc + b_ref[j].astype(F32)
        o_ref[...] = acc

    return pl.pallas_call(
        body, name=name, grid=(rh // tr,), in_specs=[pl.BlockSpec((n, tr, cols), lambda i: (0, i, 0))],
        out_specs=pl.BlockSpec((tr, cols), lambda i: (i, 0)),
        out_shape=jax.ShapeDtypeStruct((rh, cols), F32), compiler_params=_cp("parallel"),
    )(b)


BIG = ("w_in", "w_out", "w_ffn_in", "w_down")
W_IN_SHARD = (N_MAIN + 2 * GH) // N_CHIP


def _chip_major(name, g):
    if name == "w_in":
        return jnp.stack([g[:, j * W_IN_SHARD:(j + 1) * W_IN_SHARD] for j in range(N_CHIP)])
    if name == "w_ffn_in":
        return g
    return g.reshape(N_CHIP, g.shape[0] // N_CHIP, g.shape[1])


def _rs_pairs(gs, tag):
    c = lax.axis_index("c")
    mine = [lax.dynamic_slice_in_dim(a, c * (a.shape[1] // 2), a.shape[1] // 2, axis=1) for a in gs]
    theirs = _rs_swap(gs, name=f"{tag}_swap")
    return [_add_pair(a, b, name=f"{tag}_add{i}") for i, (a, b) in enumerate(zip(mine, theirs))]


def _rs_sums(got, pairs, tag):
    return [_sum_chips(b, name=f"{tag}_sum{i}") for i, b in enumerate(_fill_own_slab(got, pairs))]


def _size(shape):
    n = 1
    for s in shape:
        n *= s
    return n


def _pack_flat(parts, total, dtype):
    flat = jnp.concatenate([p.reshape(-1).astype(dtype) for p in parts])
    return jnp.pad(flat, (0, total - flat.shape[0]))


def _unpack_flat(flat, shapes):
    out, o = [], 0
    for s in shapes:
        out.append(flat[o:o + _size(s)].reshape(s))
        o += _size(s)
    return out


WEIGHTS = ("attn_norm", "w_in", "attn_sinks", "gdn_conv_w", "gdn_a_log", "gdn_dt_bias", "gdn_norm", "w_out", "ffn_norm",
           "w_ffn_in", "ffn_conv_w", "ffn_conv_b", "w_down", "final_norm")
SMALL = {"attn_norm": (DEPTH, D), "attn_sinks": (DEPTH, HQ), "gdn_a_log": (DEPTH, GH), "gdn_dt_bias": (DEPTH, GH),
         "gdn_norm": (DEPTH, GD), "ffn_norm": (DEPTH, D), "ffn_conv_b": (DEPTH, DFF), "final_norm": (D,)}
CONV_FULL = {"gdn_conv_w": (DEPTH, GK, 3 * GH * GD), "ffn_conv_w": (DEPTH, FK, DFF)}
CONV_SHARD = {"gdn_conv_w": (DEPTH, GK, 3 * GH * GD // N_CHIP), "ffn_conv_w": (DEPTH, FK, DFF // N_CHIP)}
CONV_ROWS, SMALLG_ROWS, SMALLW_ROWS = 64, 320, 144


def kernel(x, attn_norm, w_in, attn_sinks, gdn_conv_w, gdn_a_log, gdn_dt_bias, gdn_norm, w_out, ffn_norm, w_ffn_in, ffn_conv_w, ffn_conv_b, w_down, final_norm, loss_target, m_attn_norm, m_w_in, m_attn_sinks, m_gdn_conv_w, m_gdn_a_log, m_gdn_dt_bias, m_gdn_norm, m_w_out, m_ffn_norm, m_w_ffn_in, m_ffn_conv_w, m_ffn_conv_b, m_w_down, m_final_norm, v_attn_norm, v_w_in, v_attn_sinks, v_gdn_conv_w, v_gdn_a_log, v_gdn_dt_bias, v_gdn_norm, v_w_out, v_ffn_norm, v_w_ffn_in, v_ffn_conv_w, v_ffn_conv_b, v_w_down, v_final_norm):
    w = dict(zip(WEIGHTS, (attn_norm, w_in, attn_sinks, gdn_conv_w, gdn_a_log, gdn_dt_bias, gdn_norm, w_out, ffn_norm,
                           w_ffn_in, ffn_conv_w, ffn_conv_b, w_down, final_norm)))
    m = dict(zip(WEIGHTS, (m_attn_norm, m_w_in, m_attn_sinks, m_gdn_conv_w, m_gdn_a_log, m_gdn_dt_bias, m_gdn_norm, m_w_out,
                           m_ffn_norm, m_w_ffn_in, m_ffn_conv_w, m_ffn_conv_b, m_w_down, m_final_norm)))
    v = dict(zip(WEIGHTS, (v_attn_norm, v_w_in, v_attn_sinks, v_gdn_conv_w, v_gdn_a_log, v_gdn_dt_bias, v_gdn_norm, v_w_out,
                           v_ffn_norm, v_w_ffn_in, v_ffn_conv_w, v_ffn_conv_b, v_w_down, v_final_norm)))
    cx, cy, _ = _pos()
    chip = 2 * cx + cy

    cpack = _pack_flat([w[n] for n in CONV_SHARD], CONV_ROWS * 128, F32).reshape(CONV_ROWS, 128)
    cgath, _ = _ag_small(cpack, name="gather_conv_w")
    cgath = cgath.reshape(N_DEV, CONV_ROWS * 128)
    cshards = [_unpack_flat(cgath[2 * j], list(CONV_SHARD.values())) for j in range(N_CHIP)]
    conv = {n: jnp.concatenate([cshards[j][i] for j in range(N_CHIP)], axis=2) for i, n in enumerate(CONV_SHARD)}

    Ps = [_layer_params(attn_norm[l], attn_sinks[l], conv["gdn_conv_w"][l], gdn_a_log[l], gdn_dt_bias[l], gdn_norm[l],
                        ffn_norm[l], conv["ffn_conv_w"][l], ffn_conv_b[l]) for l in range(DEPTH)]

    wb = {n: [w[n][l].astype(BF16) for l in range(DEPTH)] for n in BIG}
    mixer_w = lambda g_in, g_out: dict(all=_layer_weights(jnp.concatenate(list(g_in), axis=1)), out=g_out.reshape(D, D))
    ffn_w = lambda g_ffn, g_down: dict(ffn=g_ffn, down=g_down.reshape(DFF, D))
    groups = [[wb[a][l], wb[b][l]] for l in range(DEPTH) for a, b in (("w_in", "w_out"), ("w_ffn_in", "w_down"))]
    def after_start(P, key, started):
        return {**P, key: P[key] + started[4][:1, :1]}

    g0 = _gather_start(groups[0], cgath, name="gather0_start")
    got0 = _gather_wait(*g0[:4], g0[4], name="gather0_wait")
    g1 = _gather_start(groups[1], got0[1], name="gather1_start")
    Wm0 = mixer_w(*got0)
    h, sv0, _ = _mixer_fwd(x[0], Wm0, after_start(Ps[0], "attn_norm", g1), 0)
    got1 = _gather_wait(*g1[:4], h, name="gather1_wait")
    g2 = _gather_start(groups[2], got1[1], name="gather2_start")
    Wf0 = ffn_w(*got1)
    h, sv0f, _ = _ffn_fwd(h, Wf0, after_start(Ps[0], "ffn_norm", g2), 0)
    got2 = _gather_wait(*g2[:4], h, name="gather2_wait")
    g3 = _gather_start(groups[3], got2[1], name="gather3_start")
    Wm1 = mixer_w(*got2)
    h, sv1, _ = _mixer_fwd(h, Wm1, after_start(Ps[1], "attn_norm", g3), 1)
    Wf1 = ffn_w(*_gather_wait(*g3[:4], h, name="gather3_wait"))
    h, sv1f, _ = _ffn_fwd(h, Wf1, Ps[1], 1)
    dx, g_final, loss_part = _loss_head(h, final_norm[None], loss_target[0], name="loss_head")

    def exchange_start(pairs, tag):
        carry = _exchange_carry(pairs)
        return carry, _split_start(carry, pairs[0], name=f"{tag}_exchange_start")

    def exchange_wait(carry, started, after, tag):
        srcs, got = _split_wait(carry, *started[:4], after, name=f"{tag}_exchange_wait")
        return _rs_sums(got, srcs, tag)

    tied = lambda P, key, started: {**P, key: P[key] + started[4][:1, :1]}
    dx, gf1 = _ffn_bwd(dx, {**sv1, **sv1f}, Wf1, Ps[1], 1)
    dx, gm1, _ = _mixer_bwd(dx, sv1, Wm1, Ps[1], 1)
    lg1 = {**gf1, **gm1}
    pairs1 = _rs_pairs([_chip_major(n, lg1[n]) for n in BIG], "rs1")
    ex1 = exchange_start(pairs1, "rs1")
    dx, gf0 = _ffn_bwd(dx, {**sv0, **sv0f}, Wf0, tied(Ps[0], "fcb", ex1[1]), 0)
    pairs0a = _rs_pairs([_chip_major(n, gf0[n]) for n in BIG[2:]], "rs0a")
    ex0a = exchange_start(pairs0a, "rs0a")
    dx, gm0, _ = _mixer_bwd(dx, sv0, Wm0, tied(Ps[0], "gnw", ex0a[1]), 0)
    lg = [{**gf0, **gm0}, lg1]
    pairs0b = _rs_pairs([_chip_major(n, gm0[n]) for n in BIG[:2]], "rs0b")
    sums0b = _rs_sums(_run_carry(_exchange_carry(pairs0b), name="rs0b_exchange"), pairs0b, "rs0b")
    sums1 = exchange_wait(*ex1, sums0b[0], "rs1")
    sums0a = exchange_wait(*ex0a, sums1[0], "rs0a")
    joined = _rs_join(sums0b + sums0a + sums1, name="rs_join")
    reduced = [joined[:4], joined[4:]]
    grad_x = dx[None]
    stacked = lambda n: jnp.stack([lg[l][n] for l in range(DEPTH)])

    small_parts = [g_final[0] if n == "final_norm" else stacked(n) for n in SMALL] + [stacked(n) for n in CONV_FULL]
    svec = _pack_flat(small_parts + [loss_part[0, :1]], SMALLG_ROWS * 128, F32).reshape(SMALLG_ROWS, 128)
    _, sred = _ag_small(svec, name="reduce_small")
    small_g = _unpack_flat(sred.reshape(-1), list(SMALL.values()) + list(CONV_FULL.values()) + [(1,)])
    g = dict(zip(list(SMALL) + list(CONV_FULL), small_g[:-1]))
    loss = small_g[-1][0]
    for n in CONV_FULL:
        wd = CONV_SHARD[n][2]
        g[n] = lax.dynamic_slice_in_dim(g[n], chip * wd, wd, axis=2)

    g.update({n: jnp.stack([reduced[l][i] for l in range(DEPTH)]) for i, n in enumerate(BIG)})

    delta, new_m, new_v = {}, {}, {}
    for n in BIG:
        delta[n], new_m[n], new_v[n] = _adamw(w[n], g[n], m[n], v[n], name=f"adamw_{n}")
    small_names = list(SMALL) + list(CONV_SHARD)
    small_shapes = list(SMALL.values()) + list(CONV_SHARD.values())
    packed = [_pack_flat([t[n] for n in small_names], SMALLW_ROWS * 128, F32).reshape(SMALLW_ROWS, 128) for t in (w, g, m, v)]
    for res, out in zip(_adamw(*packed, name="adamw_small"), (delta, new_m, new_v)):
        out.update(zip(small_names, _unpack_flat(res.reshape(-1), small_shapes)))

    return (loss, grad_x, *[g[n] for n in WEIGHTS], *[delta[n] for n in WEIGHTS], *[new_m[n] for n in WEIGHTS],
            *[new_v[n] for n in WEIGHTS])
```

```python
import functools

import jax
import jax.numpy as jnp
import numpy as np
from jax import lax
from jax.experimental import pallas as pl
from jax.experimental.pallas import tpu as pltpu

F32, BF16 = jnp.float32, jnp.bfloat16
HIGHEST = lax.Precision.HIGHEST
MESH = pl.DeviceIdType.MESH

D = 1024
T = 4096
DEPTH = 2
HQ, HKV, DH, WIN = 8, 2, 64, 128
GH, GD, GC, GK = 4, 128, 64, 4
DFF, FK = 2816, 3
EPS = 1e-6
N_MAIN = 2816
N_PROJ = N_MAIN + 128
COL_K, COL_V, COL_G, COL_Z, COL_BA = 4, 5, 6, 18, 22
N_CHUNK = T // GC
N_DEV, N_CHIP = 8, 4
FFN_CW = 2 * DFF // N_CHIP
LR, B1, B2, AEPS, WD, STEP = 0.001, 0.9, 0.999, 1e-08, 0.01, 10

VMEM_LIMIT = 56 * 1024 * 1024
ANY = pl.BlockSpec(memory_space=pl.ANY)
VMEM_SPEC = pl.BlockSpec(memory_space=pltpu.VMEM)


def _cp(*sem):
    return pltpu.CompilerParams(dimension_semantics=sem if sem else None, vmem_limit_bytes=VMEM_LIMIT)


class _Carry:
    def __init__(self, arrays, out_shapes, nsem, plan):
        self.arrays, self.out_shapes, self.nsem, self.plan = list(arrays), list(out_shapes), nsem, plan


def _seq_call(body, carry, *, name, steps, in_specs, out_specs, out_shape, args, scratch_shapes=(), aliases=None):
    in_specs, out_specs, out_shape, scratch = list(in_specs), list(out_specs), list(out_shape), list(scratch_shapes)
    n_in, n_out, n_scr = len(in_specs), len(out_specs), len(scratch)
    if carry is None:
        fn = body
    else:
        ci, co = len(carry.arrays), len(carry.out_shapes)

        def fn(*refs):
            ins, cins = refs[:n_in], refs[n_in:n_in + ci]
            outs, couts = refs[n_in + ci:n_in + ci + n_out], refs[n_in + ci + n_out:n_in + ci + n_out + co]
            scr, (ssem, rsem) = refs[n_in + ci + n_out + co:-2], refs[-2:]
            sends, recvs = carry.plan(cins, couts, ssem, rsem)

            @pl.when(pl.program_id(0) == 0)
            def _():
                for cp in sends:
                    cp.start()

            body(*ins, *outs, *scr)

            @pl.when(pl.program_id(0) == steps - 1)
            def _():
                for cp in recvs:
                    cp.wait_recv()
                for cp in sends:
                    cp.wait_send()

        in_specs += [ANY] * ci
        out_specs += [ANY] * co
        out_shape += carry.out_shapes
        scratch += [pltpu.SemaphoreType.DMA((carry.nsem,)), pltpu.SemaphoreType.DMA((carry.nsem,))]
        args = list(args) + carry.arrays
    outs = pl.pallas_call(
        fn, name=name, grid=(steps,), in_specs=in_specs, out_specs=out_specs, out_shape=out_shape,
        scratch_shapes=scratch, input_output_aliases=aliases or {}, compiler_params=_cp("arbitrary"),
    )(*args)
    return list(outs[:n_out]), list(outs[n_out:])


def _dot(a, b):
    return jnp.dot(a.astype(BF16), b.astype(BF16), preferred_element_type=F32)


def _dot_nt(a, b):
    return lax.dot_general(a.astype(BF16), b.astype(BF16), (((1,), (1,)), ((), ())), preferred_element_type=F32)


def _dot_tn(a, b):
    return lax.dot_general(a.astype(BF16), b.astype(BF16), (((0,), (0,)), ((), ())), preferred_element_type=F32)


def _iota(shape, dim):
    return lax.broadcasted_iota(jnp.int32, shape, dim)


def _col(x, idx):
    return jnp.sum(jnp.where(_iota(x.shape, 1) == idx, x, 0.0), axis=1, keepdims=True)


def _silu(y):
    return y * jax.nn.sigmoid(y)


_MM_DN = {"nn": (((1,), (0,)), ((), ())), "nt": (((1,), (1,)), ((), ())), "tn": (((0,), (0,)), ((), ()))}


def _mm(a, b, *, name, mode, dims, tm, tn, tk, res=None, out_dtype=F32, a_spec=None, b_spec=None, out_spec=None,
        out_shape=None, n_outer=False):
    M, N, K = dims
    assert M % tm == 0 and N % tn == 0 and K % tk == 0, (name, dims)
    nk = K // tk
    dn = _MM_DN[mode]

    def body(*refs):
        a_ref, b_ref = refs[:2]
        r_ref = None if res is None else refs[2]
        o_ref = refs[2 if res is None else 3]
        part = lax.dot_general(a_ref[...].astype(BF16), b_ref[...].astype(BF16), dn, preferred_element_type=F32)
        if nk == 1:
            o_ref[...] = (part if res is None else part + r_ref[...]).astype(out_dtype)
            return
        acc, k = refs[-1], pl.program_id(2)

        @pl.when(k == 0)
        def _():
            acc[...] = part

        @pl.when(k > 0)
        def _():
            acc[...] += part

        @pl.when(k == nk - 1)
        def _():
            r = acc[...]
            if res is not None:
                r = r + r_ref[...]
            o_ref[...] = r.astype(out_dtype)

    if a_spec is None:
        a_spec = pl.BlockSpec((tk, tm), lambda i, j, k: (k, i)) if mode == "tn" else pl.BlockSpec((tm, tk), lambda i, j, k: (i, k))
    if b_spec is None:
        b_spec = pl.BlockSpec((tn, tk), lambda i, j, k: (j, k)) if mode == "nt" else pl.BlockSpec((tk, tn), lambda i, j, k: (k, j))
    in_specs, args = [a_spec, b_spec], [a, b]
    if res is not None:
        in_specs.append(pl.BlockSpec((tm, tn), lambda i, j, k: (i, j)))
        args.append(res)
    out_spec = out_spec or pl.BlockSpec((tm, tn), lambda i, j, k: (i, j))
    grid = (M // tm, N // tn, nk)
    if n_outer:
        swap = lambda s: pl.BlockSpec(s.block_shape, lambda j, i, k, f=s.index_map: f(i, j, k))
        in_specs, out_spec, grid = [swap(s) for s in in_specs], swap(out_spec), (N // tn, M // tm, nk)
    return pl.pallas_call(
        body, name=name, grid=grid, in_specs=in_specs, out_specs=out_spec,
        out_shape=out_shape or jax.ShapeDtypeStruct((M, N), out_dtype),
        scratch_shapes=[pltpu.VMEM((tm, tn), F32)] if nk > 1 else [],
        compiler_params=_cp("parallel", "parallel", "arbitrary"),
    )(*args)


def _rms(x, w):
    return x * lax.rsqrt(jnp.mean(x * x, axis=-1, keepdims=True) + EPS) * w


def _rmsnorm_fwd(x, w, *, name, tm=512):
    def body(x_ref, w_ref, o_ref):
        o_ref[...] = _rms(x_ref[...], w_ref[...]).astype(BF16)

    return pl.pallas_call(
        body, name=name, grid=(x.shape[0] // tm,),
        in_specs=[pl.BlockSpec((tm, D), lambda i: (i, 0)), pl.BlockSpec((1, D), lambda i: (0, 0))],
        out_specs=pl.BlockSpec((tm, D), lambda i: (i, 0)),
        out_shape=jax.ShapeDtypeStruct(x.shape, BF16), compiler_params=_cp("parallel"),
    )(x, w)


def _mm_rmsnorm_bwd(a, b, x, w, dres, *, name, K, tm, tk, a_spec=None, b_spec=None):
    M = x.shape[0]
    nk = K // tk
    assert M % tm == 0 and K % tk == 0, (name, M, K)

    def body(a_ref, b_ref, x_ref, w_ref, dr_ref, dx_ref, dw_ref, acc):
        i, k = pl.program_id(0), pl.program_id(1)
        part = lax.dot_general(a_ref[...].astype(BF16), b_ref[...].astype(BF16), _MM_DN["nt"], preferred_element_type=F32)

        @pl.when(k == 0)
        def _():
            acc[...] = part

        @pl.when(k > 0)
        def _():
            acc[...] += part

        @pl.when((i == 0) & (k == 0))
        def _():
            dw_ref[...] = jnp.zeros_like(dw_ref)

        @pl.when(k == nk - 1)
        def _():
            _, vjp = jax.vjp(_rms, x_ref[...], w_ref[...])
            dx, dw = vjp(acc[...])
            dx_ref[...] = dx + dr_ref[...]
            dw_ref[...] += dw

    row = pl.BlockSpec((tm, D), lambda i, k: (i, 0))
    vec = pl.BlockSpec((1, D), lambda i, k: (0, 0))
    return pl.pallas_call(
        body, name=name, grid=(M // tm, nk),
        in_specs=[a_spec or pl.BlockSpec((tm, tk), lambda i, k: (i, k)), b_spec or pl.BlockSpec((D, tk), lambda i, k: (0, k)),
                  row, vec, row],
        out_specs=[row, vec], out_shape=[jax.ShapeDtypeStruct((M, D), F32), jax.ShapeDtypeStruct((1, D), F32)],
        scratch_shapes=[pltpu.VMEM((tm, D), F32)], compiler_params=_cp("arbitrary", "arbitrary"),
    )(a, b, x, w, dres)


def _rmsnorm_bwd(x, w, dh, dres, *, name, tm=512):
    def body(x_ref, w_ref, dh_ref, dr_ref, dx_ref, dw_ref):
        _, vjp = jax.vjp(_rms, x_ref[...], w_ref[...])
        dx, dw = vjp(dh_ref[...])
        dx_ref[...] = dx + dr_ref[...]

        @pl.when(pl.program_id(0) == 0)
        def _():
            dw_ref[...] = jnp.zeros_like(dw_ref)

        dw_ref[...] += dw

    row = pl.BlockSpec((tm, D), lambda i: (i, 0))
    vec = pl.BlockSpec((1, D), lambda i: (0, 0))
    return pl.pallas_call(
        body, name=name, grid=(x.shape[0] // tm,), in_specs=[row, vec, row, row], out_specs=[row, vec],
        out_shape=[jax.ShapeDtypeStruct(x.shape, F32), jax.ShapeDtypeStruct((1, D), F32)],
        compiler_params=_cp("arbitrary"),
    )(x, w, dh, dres)


def _loss_head(x, w, tgt, *, name, tm=512):
    def f(xv, wv, tv):
        err = _rms(xv, wv) - tv
        per_row = jnp.sum(err * err, axis=1, keepdims=True) * (0.5 / D)
        return jnp.sum(per_row, axis=0, keepdims=True)

    def body(x_ref, w_ref, t_ref, dx_ref, dw_ref, loss_ref):
        tv = t_ref[...]
        loss, vjp = jax.vjp(lambda xv, wv: f(xv, wv, tv), x_ref[...], w_ref[...])
        dx, dw = vjp(jnp.ones((1, 1), F32))
        dx_ref[...] = dx

        @pl.when(pl.program_id(0) == 0)
        def _():
            dw_ref[...] = jnp.zeros_like(dw_ref)
            loss_ref[...] = jnp.zeros_like(loss_ref)

        dw_ref[...] += dw
        loss_ref[...] += jnp.broadcast_to(loss, loss_ref.shape)

    row = pl.BlockSpec((tm, D), lambda i: (i, 0))
    vec = pl.BlockSpec((1, D), lambda i: (0, 0))
    return pl.pallas_call(
        body, name=name, grid=(x.shape[0] // tm,), in_specs=[row, vec, row],
        out_specs=[row, vec, pl.BlockSpec((1, 128), lambda i: (0, 0))],
        out_shape=[jax.ShapeDtypeStruct(x.shape, F32), jax.ShapeDtypeStruct((1, D), F32),
                   jax.ShapeDtypeStruct((1, 128), F32)],
        compiler_params=_cp("arbitrary"),
    )(x, w, tgt)


def _swa_bias():
    G = HQ // HKV
    r, c = np.arange(G * WIN)[:, None], np.arange(2 * WIN)[None, :]
    rel = (r % WIN) + WIN - c
    band = (rel >= 0) & (rel < WIN)
    out = np.empty((2, HKV, G * WIN, 2 * WIN), np.float32)
    for h2 in range(HKV):
        slope = 2.0 ** (-8.0 * (h2 * G + r // WIN + 1) / HQ)
        out[0, h2] = np.where(band & (c >= WIN), -slope * rel, -1e30)
        out[1, h2] = np.where(band, -slope * rel, -1e30)
    return jnp.asarray(out)


def _swa_group(qg, k2, v2, sk, bias, h2):
    s = _dot_nt(qg, k2) * (DH ** -0.5) + bias
    g1 = _iota((s.shape[0], 1), 0) >> 7
    sink = jnp.zeros((s.shape[0], 1), F32)
    for gi in range(HQ // HKV):
        sink = jnp.where(g1 == gi, _col(sk, h2 * (HQ // HKV) + gi), sink)
    m = lax.stop_gradient(jnp.maximum(jnp.max(s, axis=1, keepdims=True), sink))
    p = jnp.exp(s - m)
    den = jnp.sum(p, axis=1, keepdims=True) + jnp.exp(sink - m)
    return _dot(p * (1.0 / den), v2)


def _swa_split_q(q, h2):
    G = HQ // HKV
    return jnp.concatenate([q[:, (h2 * G + g) * DH:(h2 * G + g + 1) * DH] for g in range(G)], axis=0)


def _swa_merge_q(parts):
    G = HQ // HKV
    return jnp.concatenate([parts[h2][g * WIN:(g + 1) * WIN] for h2 in range(HKV) for g in range(G)], axis=1)


def _swa_specs():
    prev = lambda i: jnp.maximum(jnp.minimum(i, T // WIN - 1) - 1, 0)
    cur = lambda i: jnp.minimum(i, T // WIN - 1)
    return [
        pl.BlockSpec((WIN, HQ * DH), lambda i: (cur(i), 0)),
        pl.BlockSpec((WIN, 128), lambda i: (prev(i), COL_K)),
        pl.BlockSpec((WIN, 128), lambda i: (cur(i), COL_K)),
        pl.BlockSpec((WIN, 128), lambda i: (prev(i), COL_V)),
        pl.BlockSpec((WIN, 128), lambda i: (cur(i), COL_V)),
        pl.BlockSpec((1, 128), lambda i: (0, 0)),
        pl.BlockSpec((None, HKV, (HQ // HKV) * WIN, 2 * WIN), lambda i: (jnp.minimum(i, 1), 0, 0, 0)),
    ]


def _swa_fwd(proj, sinks, *, name):
    def body(q_ref, kp_ref, kc_ref, vp_ref, vc_ref, sk_ref, bias_ref, o_ref):
        q, sk = q_ref[...].astype(F32), sk_ref[...]
        outs = []
        for h2 in range(HKV):
            sl = slice(h2 * DH, (h2 + 1) * DH)
            k2 = jnp.concatenate([kp_ref[:, sl], kc_ref[:, sl]], axis=0).astype(F32)
            v2 = jnp.concatenate([vp_ref[:, sl], vc_ref[:, sl]], axis=0).astype(F32)
            outs.append(_swa_group(_swa_split_q(q, h2), k2, v2, sk, bias_ref[h2], h2))
        o_ref[...] = _swa_merge_q(outs).astype(BF16)

    return pl.pallas_call(
        body, name=name, grid=(T // WIN,), in_specs=_swa_specs(),
        out_specs=pl.BlockSpec((WIN, HQ * DH), lambda i: (i, 0)),
        out_shape=jax.ShapeDtypeStruct((T, HQ * DH + GH * GD), BF16), compiler_params=_cp("parallel"),
    )(proj, proj, proj, proj, proj, sinks, _swa_bias())


def _swa_bwd(proj, sinks, do, dproj, *, name):
    NB = T // WIN
    QW = HQ * DH

    def body(q_ref, kp_ref, kc_ref, vp_ref, vc_ref, sk_ref, bias_ref, do_ref, _, out_ref, dsk_ref, cq, ck, cv):
        i = pl.program_id(0)

        @pl.when(i == 0)
        def _():
            cq[...] = jnp.zeros_like(cq)
            ck[...] = jnp.zeros_like(ck)
            cv[...] = jnp.zeros_like(cv)
            dsk_ref[...] = jnp.zeros_like(dsk_ref)

        @pl.when(i < NB)
        def _():
            q, sk, dov = q_ref[...].astype(F32), sk_ref[...], do_ref[...].astype(F32)
            dqs, dkp, dkc, dvp, dvc = [], [], [], [], []
            dsk = jnp.zeros_like(sk)
            for h2 in range(HKV):
                sl = slice(h2 * DH, (h2 + 1) * DH)
                k2 = jnp.concatenate([kp_ref[:, sl], kc_ref[:, sl]], axis=0).astype(F32)
                v2 = jnp.concatenate([vp_ref[:, sl], vc_ref[:, sl]], axis=0).astype(F32)
                _, vjp = jax.vjp(functools.partial(_swa_group, bias=bias_ref[h2], h2=h2), _swa_split_q(q, h2), k2, v2, sk)
                dqg, dk2, dv2, dsk_h = vjp(_swa_split_q(dov, h2))
                dqs.append(dqg)
                dkp.append(dk2[:WIN]); dkc.append(dk2[WIN:])
                dvp.append(dv2[:WIN]); dvc.append(dv2[WIN:])
                dsk = dsk + dsk_h
            out_ref[:, :QW] = cq[...].astype(BF16)
            out_ref[:, QW:QW + 128] = (ck[...] + jnp.concatenate(dkp, axis=1)).astype(BF16)
            out_ref[:, QW + 128:] = (cv[...] + jnp.concatenate(dvp, axis=1)).astype(BF16)
            cq[...] = _swa_merge_q(dqs)
            ck[...] = jnp.concatenate(dkc, axis=1)
            cv[...] = jnp.concatenate(dvc, axis=1)
            dsk_ref[...] += dsk

        @pl.when(i == NB)
        def _():
            out_ref[:, :QW] = cq[...].astype(BF16)
            out_ref[:, QW:QW + 128] = ck[...].astype(BF16)
            out_ref[:, QW + 128:] = cv[...].astype(BF16)

    qblk = pl.BlockSpec((WIN, QW), lambda i: (jnp.minimum(i, NB - 1), 0))
    return pl.pallas_call(
        body, name=name, grid=(NB + 1,), in_specs=_swa_specs() + [qblk, ANY],
        out_specs=[pl.BlockSpec((WIN, QW + 256), lambda i: (jnp.maximum(i - 1, 0), 0)), pl.BlockSpec((1, 128), lambda i: (0, 0))],
        out_shape=[jax.ShapeDtypeStruct(dproj.shape, dproj.dtype), jax.ShapeDtypeStruct((1, 128), F32)],
        scratch_shapes=[pltpu.VMEM((WIN, QW), F32), pltpu.VMEM((WIN, 128), F32), pltpu.VMEM((WIN, 128), F32)],
        input_output_aliases={8: 0}, compiler_params=_cp("arbitrary"),
    )(proj, proj, proj, proj, proj, sinks, _swa_bias(), do, dproj)


RC = 256
HALO = 8


def _load_ext(ref, c):
    nch = T // RC
    r0 = pl.multiple_of(c * RC, RC)
    p0 = pl.multiple_of(jnp.maximum(r0 - HALO, 0), HALO)
    n0 = pl.multiple_of(jnp.minimum(r0 + RC, T - HALO), HALO)
    prev = jnp.where(c > 0, ref[pl.ds(p0, HALO), :].astype(F32), 0.0)
    nxt = jnp.where(c < nch - 1, ref[pl.ds(n0, HALO), :].astype(F32), 0.0)
    return jnp.concatenate([prev, ref[pl.ds(r0, RC), :].astype(F32), nxt], axis=0)


def _conv_ext(xe, w, K):
    y = w[K - 1:K, :] * xe
    for s in range(1, K):
        y = y + w[K - 1 - s:K - s, :] * pltpu.roll(xe, s, 0)
    return y


def _conv_bwd_ext(xe, dye, w, K, dw_ref):
    n = xe.shape[0]
    own = slice(HALO, HALO + RC)
    dx = w[K - 1:K, :] * dye
    dw_ref[K - 1:K, :] += jnp.sum(dye[own] * xe[own], axis=0, keepdims=True)
    for s in range(1, K):
        dx = dx + w[K - 1 - s:K - s, :] * pltpu.roll(dye, n - s, 0)
        dw_ref[K - 1 - s:K - s, :] += jnp.sum(dye[own] * pltpu.roll(xe, s, 0)[own], axis=0, keepdims=True)
    return dx[own]


def _gdn_post_conv(y, is_qk):
    a = _silu(y)
    nrm = a * lax.rsqrt(jnp.sum(a * a, axis=1, keepdims=True) + EPS)
    return jnp.where(is_qk, nrm, a)


def _gdn_prep_fwd(proj, conv_w, *, name):
    nblk = 3 * GH

    def body(x_ref, w_ref, o_ref):
        is_qk = pl.program_id(0) < 2 * GH
        w = w_ref[...]

        def chunk(c, carry):
            y = _conv_ext(_load_ext(x_ref, c), w, GK)[HALO:HALO + RC]
            o_ref[pl.ds(pl.multiple_of(c * RC, RC), RC), :] = _gdn_post_conv(y, is_qk)
            return carry

        lax.fori_loop(0, T // RC, chunk, 0)

    return pl.pallas_call(
        body, name=name, grid=(nblk,),
        in_specs=[pl.BlockSpec((T, 128), lambda j: (0, COL_G + j)), pl.BlockSpec((GK, 128), lambda j: (0, j))],
        out_specs=pl.BlockSpec((T, 128), lambda j: (0, j)),
        out_shape=jax.ShapeDtypeStruct((T, nblk * 128), F32), compiler_params=_cp("parallel"),
    )(proj, conv_w)


def _gdn_prep_bwd(proj, conv_w, dout, dproj, *, name):
    nblk = 3 * GH

    def body(x_ref, w_ref, d_ref, _, dx_ref, dw_ref):
        is_qk = pl.program_id(0) < 2 * GH
        w = w_ref[...]
        dw_ref[...] = jnp.zeros_like(dw_ref)

        def chunk(c, carry):
            xe = _load_ext(x_ref, c)
            _, vjp = jax.vjp(lambda y: _gdn_post_conv(y, is_qk), _conv_ext(xe, w, GK))
            (dye,) = vjp(_load_ext(d_ref, c))
            dx_ref[pl.ds(pl.multiple_of(c * RC, RC), RC), :] = _conv_bwd_ext(xe, dye, w, GK, dw_ref).astype(BF16)
            return carry

        lax.fori_loop(0, T // RC, chunk, 0)

    return pl.pallas_call(
        body, name=name, grid=(nblk,),
        in_specs=[pl.BlockSpec((T, 128), lambda j: (0, COL_G + j)), pl.BlockSpec((GK, 128), lambda j: (0, j)),
                  pl.BlockSpec((T, 128), lambda j: (0, j)), ANY],
        out_specs=[pl.BlockSpec((T, 128), lambda j: (0, COL_G + j)), pl.BlockSpec((GK, 128), lambda j: (0, j))],
        out_shape=[jax.ShapeDtypeStruct(dproj.shape, dproj.dtype), jax.ShapeDtypeStruct((GK, nblk * 128), F32)],
        input_output_aliases={3: 0}, compiler_params=_cp("parallel"),
    )(proj, conv_w, dout, dproj)


def _ffn_post_conv(y, b, up):
    return _silu(y + b) * up


def _ffn_act_fwd(gu, conv_w, conv_b, *, name, carry=None):
    nblk = DFF // 128

    def body(g_ref, u_ref, w_ref, b_ref, o_ref):
        w, b = w_ref[...], b_ref[...]

        def chunk(c, carry):
            rows = pl.ds(pl.multiple_of(c * RC, RC), RC)
            y = _conv_ext(_load_ext(g_ref, c), w, FK)[HALO:HALO + RC]
            o_ref[rows, :] = _ffn_post_conv(y, b, u_ref[rows, :].astype(F32)).astype(BF16)
            return carry

        lax.fori_loop(0, T // RC, chunk, 0)

    (act,), carried = _seq_call(
        body, carry, name=name, steps=nblk,
        in_specs=[pl.BlockSpec((T, 128), lambda j: (0, j)), pl.BlockSpec((T, 128), lambda j: (0, nblk + j)),
                  pl.BlockSpec((FK, 128), lambda j: (0, j)), pl.BlockSpec((1, 128), lambda j: (0, j))],
        out_specs=[pl.BlockSpec((T, 128), lambda j: (0, j))], out_shape=[jax.ShapeDtypeStruct((T, DFF), BF16)],
        args=(gu, gu, conv_w, conv_b))
    return act, carried


def _ffn_act_bwd(gu, conv_w, conv_b, dact, *, name):
    nblk = DFF // 128

    def body(g_ref, u_ref, w_ref, b_ref, d_ref, dgu_ref, dw_ref, db_ref):
        dg_ref, du_ref = dgu_ref.at[0], dgu_ref.at[1]
        w, b = w_ref[...], b_ref[...]
        dw_ref[...] = jnp.zeros_like(dw_ref)
        db_ref[...] = jnp.zeros_like(db_ref)

        def chunk(c, carry):
            rows = pl.ds(pl.multiple_of(c * RC, RC), RC)
            xe = _load_ext(g_ref, c)
            ue = _load_ext(u_ref, c)
            _, vjp = jax.vjp(_ffn_post_conv, _conv_ext(xe, w, FK), b, ue)
            dye, db, due = vjp(_load_ext(d_ref, c))
            du_ref[rows, :] = due[HALO:HALO + RC].astype(BF16)
            db_ref[...] += jnp.sum(dye[HALO:HALO + RC], axis=0, keepdims=True)
            dg_ref[rows, :] = _conv_bwd_ext(xe, dye, w, FK, dw_ref).astype(BF16)
            return carry

        lax.fori_loop(0, T // RC, chunk, 0)

    col = pl.BlockSpec((T, 128), lambda j: (0, j))
    return pl.pallas_call(
        body, name=name, grid=(nblk,),
        in_specs=[col, pl.BlockSpec((T, 128), lambda j: (0, nblk + j)), pl.BlockSpec((FK, 128), lambda j: (0, j)),
                  pl.BlockSpec((1, 128), lambda j: (0, j)), col],
        out_specs=[pl.BlockSpec((2, T, 128), lambda j: (0, 0, j)), pl.BlockSpec((FK, 128), lambda j: (0, j)),
                   pl.BlockSpec((1, 128), lambda j: (0, j))],
        out_shape=[jax.ShapeDtypeStruct((2, T, DFF), BF16), jax.ShapeDtypeStruct((FK, DFF), F32),
                   jax.ShapeDtypeStruct((1, DFF), F32)],
        compiler_params=_cp("parallel"),
    )(gu, gu, conv_w, conv_b, dact)


def _gdn_gates(ba, alog, dtb):
    beta = jax.nn.sigmoid(ba)
    g = -jnp.exp(alog) * jax.nn.softplus(ba + dtb)
    tril = (_iota((GC, GC), 0) >= _iota((GC, GC), 1)).astype(F32)
    return beta, jnp.dot(tril, g, precision=HIGHEST, preferred_element_type=F32)


def _hmap(f, *lists):
    return [f(*xs) for xs in zip(*lists)]


def _gdn_cols(beta_all, gc_all):
    return [_col(beta_all, h) for h in range(GH)], [_col(gc_all, GH + h) for h in range(GH)]


def _gdn_decay(Gcs):
    r, c = _iota((GC, GC), 0), _iota((GC, GC), 1)
    eye, ones = (r == c).astype(F32), jnp.ones((GC, GC), F32)
    grows = _hmap(lambda G: jnp.dot(ones, eye * G, precision=HIGHEST, preferred_element_type=F32), Gcs)
    return _hmap(lambda G, grow: jnp.exp(jnp.where(r >= c, G - grow, -1e30)), Gcs, grows)


def _gdn_pre(beta_all, gc_all):
    betas, Gcs = _gdn_cols(beta_all, gc_all)
    return betas, Gcs, _gdn_decay(Gcs)


def _gdn_A(ks, betas, decays):
    strict = _iota((GC, GC), 0) > _iota((GC, GC), 1)
    kk = _hmap(lambda k, b: _dot_nt(k * b, k), ks, betas)
    return _hmap(lambda a, d: jnp.where(strict, a * d, 0.0), kk, decays)


def _tri_inv(As):
    eye = (_iota((GC, GC), 0) == _iota((GC, GC), 1)).astype(F32)
    Tms, Ps = [eye - A for A in As], As
    for _ in range(GC.bit_length() - 2):
        Ps = _hmap(lambda P: _dot(P, P), Ps)
        Tms = _hmap(lambda Tm, P: Tm + _dot(Tm, P), Tms, Ps)
    return Tms


def _gdn_chunk(qs, ks, vs, betas, Gcs, decays, Ss, Tms):
    eGs = _hmap(jnp.exp, Gcs)
    us = _hmap(lambda Tm, v, b: _dot(Tm, v * b), Tms, vs, betas)
    ws = _hmap(lambda Tm, k, b, eG: _dot(Tm, k * b * eG), Tms, ks, betas, eGs)
    qss = [q * (GD ** -0.5) for q in qs]
    qks = _hmap(lambda q, k, d: _dot_nt(q, k) * d, qss, ks, decays)
    glasts = [jnp.sum(jnp.where(_iota(G.shape, 0) == GC - 1, G, 0.0), axis=0, keepdims=True) for G in Gcs]
    kds = _hmap(lambda k, gl, G: k * jnp.exp(gl - G), ks, glasts, Gcs)
    v_news = _hmap(lambda u, w, S: u - _dot(w, S), us, ws, Ss)
    qS = _hmap(lambda q, eG, S: _dot(q * eG, S), qss, eGs, Ss)
    os = _hmap(lambda a, qk, vn: a + _dot(qk, vn), qS, qks, v_news)
    S_news = _hmap(lambda S, gl, kd, vn: S * jnp.exp(gl) + _dot_tn(kd, vn), Ss, glasts, kds, v_news)
    return os, S_news


def _gdn_chunk_fwd(qkv_c, ba, alog, dtb, *, name, carry=None):
    W3 = 3 * GH * GD

    def body(x_ref, ba_ref, al_ref, dt_ref, o_ref, s_ref, t_ref, S):
        @pl.when(pl.program_id(0) == 0)
        def _():
            S[...] = jnp.zeros_like(S)

        beta_all, gc_all = _gdn_gates(ba_ref[...], al_ref[...], dt_ref[...])
        qs, ks, vs = ([x_ref[:, (p * GH + h) * GD:(p * GH + h + 1) * GD] for h in range(GH)] for p in range(3))
        betas, Gcs, decays = _gdn_pre(beta_all, gc_all)
        Tms = _tri_inv(_gdn_A(ks, betas, decays))
        Ss = [S[h] for h in range(GH)]
        os, S_news = _gdn_chunk(qs, ks, vs, betas, Gcs, decays, Ss, Tms)
        for h in range(GH):
            s_ref[0, h] = Ss[h]
            t_ref[0, h] = Tms[h]
            o_ref[:, h * GD:(h + 1) * GD] = os[h]
            S[h] = S_news[h]

    vec = pl.BlockSpec((1, 128), lambda n: (0, 0))
    return _seq_call(
        body, carry, name=name, steps=N_CHUNK,
        in_specs=[pl.BlockSpec((GC, W3), lambda n: (n, 0)), pl.BlockSpec((GC, 128), lambda n: (n, 0)), vec, vec],
        out_specs=[pl.BlockSpec((GC, GH * GD), lambda n: (n, 0)),
                   pl.BlockSpec((1, GH, GD, GD), lambda n: (n, 0, 0, 0)),
                   pl.BlockSpec((1, GH, GC, GC), lambda n: (n, 0, 0, 0))],
        out_shape=[jax.ShapeDtypeStruct((T, GH * GD), F32), jax.ShapeDtypeStruct((N_CHUNK, GH, GD, GD), F32),
                   jax.ShapeDtypeStruct((N_CHUNK, GH, GC, GC), F32)],
        scratch_shapes=[pltpu.VMEM((GH, GD, GD), F32)], args=(qkv_c, ba, alog, dtb))


def _gdn_chunk_bwd(qkv_c, ba, alog, dtb, s_all, t_all, do, dproj, *, name, carry=None):
    W3 = 3 * GH * GD
    rev = lambda n: N_CHUNK - 1 - n

    def body(x_ref, ba_ref, al_ref, dt_ref, s_ref, t_ref, do_ref, _, dx_ref, dba_ref, dal_ref, ddt_ref, dS):
        @pl.when(pl.program_id(0) == 0)
        def _():
            dS[...] = jnp.zeros_like(dS)
            dal_ref[...] = jnp.zeros_like(dal_ref)
            ddt_ref[...] = jnp.zeros_like(ddt_ref)

        (beta_all, gc_all), vjp_gates = jax.vjp(_gdn_gates, ba_ref[...], al_ref[...], dt_ref[...])
        qs, ks, vs = ([x_ref[:, (p * GH + h) * GD:(p * GH + h + 1) * GD] for h in range(GH)] for p in range(3))
        Tms = [t_ref[0, h] for h in range(GH)]
        Ss = [s_ref[0, h] for h in range(GH)]
        dos = [do_ref[:, h * GD:(h + 1) * GD] for h in range(GH)]
        dSs = [dS[h] for h in range(GH)]

        (betas, Gcs, decays), vjp_pre = jax.vjp(_gdn_pre, beta_all, gc_all)
        _, vjp = jax.vjp(_gdn_chunk, qs, ks, vs, betas, Gcs, decays, Ss, Tms)
        dqs, dks, dvs, dbetas, dGcs, ddecays, dS_prev, dTs = vjp((dos, dSs))
        dXs = _hmap(_dot_nt, dTs, Tms)
        dAs = _hmap(lambda Tm, dX: -_dot_tn(Tm, dX), Tms, dXs)
        _, vjp_a = jax.vjp(_gdn_A, ks, betas, decays)
        dks2, dbetas2, ddecays2 = vjp_a(dAs)
        add = lambda a, b: _hmap(jnp.add, a, b)
        db_all, dg_all = vjp_pre((add(dbetas, dbetas2), dGcs, add(ddecays, ddecays2)))
        for h in range(GH):
            dS[h] = dS_prev[h]
            dx_ref[:, h * GD:(h + 1) * GD] = dqs[h]
            dx_ref[:, (GH + h) * GD:(GH + h + 1) * GD] = dks[h] + dks2[h]
            dx_ref[:, (2 * GH + h) * GD:(2 * GH + h + 1) * GD] = dvs[h]
        dba, dal, ddt = vjp_gates((db_all, dg_all))
        dba_ref[...] = dba.astype(BF16)
        dal_ref[...] += dal
        ddt_ref[...] += ddt

    vec = pl.BlockSpec((1, 128), lambda n: (0, 0))
    return _seq_call(
        body, carry, name=name, steps=N_CHUNK,
        in_specs=[pl.BlockSpec((GC, W3), lambda n: (rev(n), 0)), pl.BlockSpec((GC, 128), lambda n: (rev(n), 0)),
                  vec, vec, pl.BlockSpec((1, GH, GD, GD), lambda n: (rev(n), 0, 0, 0)),
                  pl.BlockSpec((1, GH, GC, GC), lambda n: (rev(n), 0, 0, 0)),
                  pl.BlockSpec((GC, GH * GD), lambda n: (rev(n), 0)),
                  ANY],
        out_specs=[pl.BlockSpec((GC, W3), lambda n: (rev(n), 0)), pl.BlockSpec((GC, 128), lambda n: (rev(n), COL_BA)), vec, vec],
        out_shape=[jax.ShapeDtypeStruct((T, W3), F32), jax.ShapeDtypeStruct(dproj.shape, dproj.dtype),
                   jax.ShapeDtypeStruct((1, 128), F32), jax.ShapeDtypeStruct((1, 128), F32)],
        scratch_shapes=[pltpu.VMEM((GH, GD, GD), F32)], aliases={7: 1},
        args=(qkv_c, ba, alog, dtb, s_all, t_all, do, dproj))


def _gdn_post(o, z, nw):
    return o * lax.rsqrt(jnp.mean(o * o, axis=-1, keepdims=True) + EPS) * nw * _silu(z)


def _gdn_post_fwd(o_raw, proj, nw, mixed, *, name, tm=512):
    def body(o_ref, z_ref, w_ref, _, out_ref):
        out_ref[...] = _gdn_post(o_ref[...], z_ref[...].astype(F32), w_ref[...]).astype(BF16)

    return pl.pallas_call(
        body, name=name, grid=(T // tm, GH),
        in_specs=[pl.BlockSpec((tm, GD), lambda i, h: (i, h)), pl.BlockSpec((tm, GD), lambda i, h: (i, COL_Z + h)),
                  pl.BlockSpec((1, GD), lambda i, h: (0, 0)), ANY],
        out_specs=pl.BlockSpec((tm, GD), lambda i, h: (i, HQ * DH // GD + h)),
        out_shape=jax.ShapeDtypeStruct(mixed.shape, mixed.dtype), input_output_aliases={3: 0},
        compiler_params=_cp("parallel", "parallel"),
    )(o_raw, proj, nw, mixed)


def _gdn_post_bwd(o_raw, proj, nw, dmixed, *, name, tm=512):
    def body(o_ref, z_ref, w_ref, d_ref, do_ref, dz_ref, dw_ref):
        _, vjp = jax.vjp(_gdn_post, o_ref[...], z_ref[...].astype(F32), w_ref[...])
        do, dz, dw = vjp(d_ref[...])
        do_ref[...] = do
        dz_ref[...] = dz.astype(BF16)

        @pl.when((pl.program_id(0) == 0) & (pl.program_id(1) == 0))
        def _():
            dw_ref[...] = jnp.zeros_like(dw_ref)

        dw_ref[...] += dw

    blk = pl.BlockSpec((tm, GD), lambda i, h: (i, h))
    vec = pl.BlockSpec((1, GD), lambda i, h: (0, 0))
    return pl.pallas_call(
        body, name=name, grid=(T // tm, GH),
        in_specs=[blk, pl.BlockSpec((tm, GD), lambda i, h: (i, COL_Z + h)), vec,
                  pl.BlockSpec((tm, GD), lambda i, h: (i, GH + h))],
        out_specs=[blk, pl.BlockSpec((tm, GD), lambda i, h: (i, COL_Z + h)), vec],
        out_shape=[jax.ShapeDtypeStruct((T, GH * GD), F32), jax.ShapeDtypeStruct((T, N_PROJ), BF16),
                   jax.ShapeDtypeStruct((1, GD), F32)],
        compiler_params=_cp("arbitrary", "arbitrary"),
    )(o_raw, proj, nw, dmixed)


def _adamw(w, g, m, v, *, name):
    shape = w.shape
    cols = shape[-1]
    w2, g2, m2, v2 = (a.reshape(-1, cols) for a in (w, g, m, v))
    rows = w2.shape[0]
    tr = next((t for t in (512, 256, 128, 64, 32, 16, 8) if rows % t == 0), rows)

    def body(w_ref, g_ref, m_ref, v_ref, d_ref, nm_ref, nv_ref):
        gv = g_ref[...]
        nm = B1 * m_ref[...] + (1.0 - B1) * gv
        nv = B2 * v_ref[...] + (1.0 - B2) * jnp.square(gv)
        m_hat = nm / (1.0 - B1 ** STEP)
        v_hat = nv / (1.0 - B2 ** STEP)
        d_ref[...] = -LR * (m_hat / (jnp.sqrt(v_hat) + AEPS) + WD * w_ref[...])
        nm_ref[...] = nm
        nv_ref[...] = nv

    blk = pl.BlockSpec((tr, cols), lambda i: (i, 0))
    out = pl.pallas_call(
        body, name=name, grid=(rows // tr,), in_specs=[blk] * 4, out_specs=[blk] * 3,
        out_shape=[jax.ShapeDtypeStruct((rows, cols), F32)] * 3, compiler_params=_cp("parallel"),
    )(w2, g2, m2, v2)
    return tuple(o.reshape(shape) for o in out)


def _layer_weights(w_in):
    return jnp.pad(w_in, ((0, 0), (0, N_PROJ - w_in.shape[1])))


def _layer_params(attn_norm, sinks, gcw, a_log, dt_bias, gnw, ffn_norm, fcw, fcb):
    lanes4 = lambda v: jnp.pad(v, (GH, 128 - 2 * GH))[None]
    return dict(attn_norm=attn_norm[None], sinks=jnp.pad(sinks, (0, 128 - HQ))[None], gcw=gcw, alog=lanes4(a_log),
                dtb=lanes4(dt_bias), gnw=gnw[None], ffn_norm=ffn_norm[None], fcw=fcw, fcb=fcb[None])


def _mixer_fwd(x, W, P, l, carry=None):
    n = lambda s: f"l{l}_{s}"
    h = _rmsnorm_fwd(x, P["attn_norm"], name=n("norm1"))
    proj = _mm(h, W["all"], name=n("proj"), mode="nn", dims=(T, N_MAIN, D), tm=512, tn=N_MAIN, tk=D, out_dtype=BF16)
    ba = _mm(h, W["all"], name=n("proj_ba"), mode="nn", dims=(T, 128, D), tm=1024, tn=128, tk=D,
             b_spec=pl.BlockSpec((D, 128), lambda i, j, k: (k, COL_BA)))
    mixed = _swa_fwd(proj, P["sinks"], name=n("swa"))
    qkv_c = _gdn_prep_fwd(proj, P["gcw"], name=n("gdn_prep"))
    (o_raw, s_all, t_all), carried = _gdn_chunk_fwd(qkv_c, ba, P["alog"], P["dtb"], name=n("gdn_chunk"), carry=carry)
    mixed = _gdn_post_fwd(o_raw, proj, P["gnw"], mixed, name=n("gdn_post"))
    x1 = _mm(mixed, W["out"], res=x, name=n("out_proj"), mode="nn", dims=(T, D, D), tm=512, tn=D, tk=D)
    saved = dict(x=x, h=h, proj=proj, ba=ba, qkv_c=qkv_c, o_raw=o_raw, s_all=s_all, t_all=t_all, mixed=mixed, x1=x1)
    return x1, saved, carried


def _ffn_fwd(x1, W, P, l, carry=None):
    n = lambda s: f"l{l}_{s}"
    h2 = _rmsnorm_fwd(x1, P["ffn_norm"], name=n("norm2"))
    gu = _mm(h2, W["ffn"], name=n("ffn_in"), mode="nn", dims=(T, 2 * DFF, D), tm=512, tn=FFN_CW, tk=D, out_dtype=BF16,
             b_spec=pl.BlockSpec((None, D, FFN_CW), lambda i, j, k: (j, k, 0)), n_outer=True)
    act, carried = _ffn_act_fwd(gu, P["fcw"], P["fcb"], name=n("ffn_act"), carry=carry)
    x2 = _mm(act, W["down"], res=x1, name=n("ffn_down"), mode="nn", dims=(T, D, DFF), tm=512, tn=D, tk=DFF)
    return x2, dict(h2=h2, gu=gu, act=act), carried


def _ffn_bwd(dx2, sv, W, P, l):
    n = lambda s: f"l{l}_{s}"
    CW = FFN_CW
    dact = _mm(dx2, W["down"], name=n("d_act"), mode="nt", dims=(T, DFF, D), tm=512, tn=DFF // 2, tk=D, out_dtype=BF16,
               n_outer=True)
    g_down = _mm(sv["act"], dx2, name=n("g_down"), mode="tn", dims=(DFF, D, T), tm=DFF // 2, tn=D, tk=min(T, 2048), out_dtype=BF16)
    dgu, g_fcw, g_fcb = _ffn_act_bwd(sv["gu"], P["fcw"], P["fcb"], dact, name=n("d_ffn_act"))
    dx1, g_ffn_norm = _mm_rmsnorm_bwd(
        dgu, W["ffn"], sv["x1"], P["ffn_norm"], dx2, name=n("d_h2_norm2"), K=2 * DFF, tm=512, tk=CW,
        a_spec=pl.BlockSpec((None, 512, CW), lambda i, k: (k // 2, i, k % 2)),
        b_spec=pl.BlockSpec((None, D, CW), lambda i, k: (k, 0, 0)))
    g_ffn = _mm(sv["h2"], dgu, name=n("g_ffn"), mode="tn", dims=(D, 2 * DFF, T), tm=512, tn=CW, tk=T, n_outer=True,
                b_spec=pl.BlockSpec((None, T, CW), lambda i, j, k: (j // 2, k, j % 2)),
                out_spec=pl.BlockSpec((None, 512, CW), lambda i, j, k: (j, i, 0)),
                out_shape=jax.ShapeDtypeStruct((N_CHIP, D, CW), BF16), out_dtype=BF16)
    return dx1, dict(ffn_norm=g_ffn_norm[0], w_ffn_in=g_ffn, ffn_conv_w=g_fcw, ffn_conv_b=g_fcb[0], w_down=g_down)


def _mixer_bwd(dx1, sv, W, P, l, carry=None):
    n = lambda s: f"l{l}_{s}"
    dmixed = _mm(dx1, W["out"], name=n("d_mixed"), mode="nt", dims=(T, D, D), tm=512, tn=D, tk=D)
    g_out = _mm(sv["mixed"], dx1, name=n("g_out"), mode="tn", dims=(D, D, T), tm=D, tn=D, tk=min(T, 2048), out_dtype=BF16)
    do_raw, dproj, g_gnw = _gdn_post_bwd(sv["o_raw"], sv["proj"], P["gnw"], dmixed, name=n("d_gdn_post"))
    (dqkv_c, dproj, g_alog, g_dtb), carried = _gdn_chunk_bwd(
        sv["qkv_c"], sv["ba"], P["alog"], P["dtb"], sv["s_all"], sv["t_all"], do_raw, dproj, name=n("d_gdn_chunk"), carry=carry)
    dproj, g_gcw = _gdn_prep_bwd(sv["proj"], P["gcw"], dqkv_c, dproj, name=n("d_gdn_prep"))
    dproj, g_sinks = _swa_bwd(sv["proj"], P["sinks"], dmixed, dproj, name=n("d_swa"))
    dx, g_attn_norm = _mm_rmsnorm_bwd(dproj, W["all"], sv["x"], P["attn_norm"], dx1, name=n("d_h_norm1"), K=N_PROJ,
                                      tm=512, tk=N_PROJ)
    g_all = _mm(sv["h"], dproj, name=n("g_in"), mode="tn", dims=(D, N_PROJ, T), tm=512, tn=N_PROJ, tk=min(T, 2048), out_dtype=BF16)
    grads = dict(attn_norm=g_attn_norm[0], w_in=g_all, attn_sinks=g_sinks[0, :HQ], gdn_conv_w=g_gcw,
                 gdn_a_log=g_alog[0, GH:2 * GH], gdn_dt_bias=g_dtb[0, GH:2 * GH], gdn_norm=g_gnw[0], w_out=g_out)
    return dx, grads, carried


def _pos():
    return lax.axis_index("x"), lax.axis_index("y"), lax.axis_index("c")


def _other_chips(x, y):
    return [(1 - x, y), (x, 1 - y), (1 - x, 1 - y)]


def _remote(src, dst, send_sems, recv_sems, k, to):
    return pltpu.make_async_remote_copy(src_ref=src, dst_ref=dst, send_sem=send_sems.at[k], recv_sem=recv_sems.at[k],
                                        device_id=to, device_id_type=MESH)


def _run_carry(carry, *, name):
    ci, co = len(carry.arrays), len(carry.out_shapes)

    def body(*refs):
        sends, recvs = carry.plan(refs[:ci], refs[ci:ci + co], refs[-2], refs[-1])
        for cp in sends:
            cp.start()
        for cp in recvs:
            cp.wait_recv()
        for cp in sends:
            cp.wait_send()

    return list(pl.pallas_call(
        body, name=name, in_specs=[ANY] * ci, out_specs=[ANY] * co, out_shape=carry.out_shapes,
        scratch_shapes=[pltpu.SemaphoreType.DMA((carry.nsem,)), pltpu.SemaphoreType.DMA((carry.nsem,))],
    )(*carry.arrays))


def _chip_index():
    return 2 * lax.axis_index("x") + lax.axis_index("y")


def _gather_carry(shards):
    def plan(srcs, outs, send_sems, recv_sems):
        x, y, c = _pos()
        chip = 2 * x + y
        others = [(k, px, py, 2 * px + py) for k, (px, py) in enumerate(_other_chips(x, y))]
        sends = [_remote(srcs[t], outs[t].at[chip], send_sems, recv_sems, 3 * t + k, (px, py, c))
                 for t in range(len(srcs)) for k, px, py, _ in others]
        recvs = [_remote(srcs[t], outs[t].at[j], send_sems, recv_sems, 3 * t + k, (x, y, c))
                 for t in range(len(srcs)) for k, _, _, j in others]
        return sends, recvs

    return _Carry(shards, [jax.ShapeDtypeStruct((N_CHIP,) + s.shape, s.dtype) for s in shards], 3 * len(shards), plan)


def _fill_own(outs, shards):
    return [lax.dynamic_update_index_in_dim(o, s, _chip_index(), 0) for o, s in zip(outs, shards)]


def _gather_halves(shards, after, *, name):
    nt = len(shards)

    def body(*refs):
        srcs, outs, send_sems, recv_sems = refs[:nt], refs[nt + 1:2 * nt + 1], refs[-2], refs[-1]
        x, y, c = _pos()
        me, sibling, chip = (x, y, c), (x, y, 1 - c), 2 * x + y
        others = [(k, px, py, 2 * px + py) for k, (px, py) in enumerate(_other_chips(x, y))]

        def rows(t, h):
            rh = srcs[t].shape[0] // 2
            return pl.ds(pl.multiple_of(h * rh, 16), rh)

        first = [_remote(srcs[t].at[rows(t, c), :], outs[t].at[chip, rows(t, c), :], send_sems, recv_sems, 3 * t + k, (px, py, c))
                 for t in range(nt) for k, px, py, _ in others]
        for cp in first:
            cp.start()
        passed = []
        for t in range(nt):
            for k, _, _, j in others:
                slot = outs[t].at[j, rows(t, c), :]
                _remote(slot, slot, send_sems, recv_sems, 3 * t + k, me).wait_recv()
                passed.append(_remote(slot, slot, send_sems, recv_sems, 3 * (nt + t) + k, sibling))
                passed[-1].start()
        for t in range(nt):
            for k, _, _, j in others:
                slot = outs[t].at[j, rows(t, 1 - c), :]
                _remote(slot, slot, send_sems, recv_sems, 3 * (nt + t) + k, me).wait_recv()
        for cp in first + passed:
            cp.wait_send()

    outs = pl.pallas_call(
        body, name=name, in_specs=[ANY] * (nt + 1), out_specs=[ANY] * nt,
        out_shape=[jax.ShapeDtypeStruct((N_CHIP,) + s.shape, s.dtype) for s in shards],
        scratch_shapes=[pltpu.SemaphoreType.DMA((6 * nt,)), pltpu.SemaphoreType.DMA((6 * nt,))],
    )(*shards, after)
    return _fill_own(outs, shards)


HBM_SPEC = pl.BlockSpec(memory_space=pltpu.HBM)
SEM_SPEC = pl.BlockSpec(memory_space=pltpu.SEMAPHORE)
DATAFLOW = pltpu.SideEffectType.DATAFLOW_SIDE_EFFECTING


def _split_start(carry, after, *, name):
    ci, co = len(carry.arrays), len(carry.out_shapes)

    def body(*refs):
        srcs, lands, send_sems, recv_sems, token = refs[:ci], refs[ci:ci + co], refs[ci + co + 1], refs[ci + co + 2], refs[-1]
        for cp in carry.plan(srcs, lands, send_sems, recv_sems)[0]:
            cp.start()
        token[...] = jnp.zeros_like(token)

    lands = [lax.empty(s.shape, s.dtype) for s in carry.out_shapes]
    hbm = lambda a: pltpu.with_memory_space_constraint(a, pltpu.HBM)
    out = pl.pallas_call(
        body, name=name, in_specs=[HBM_SPEC] * (ci + co) + [ANY],
        out_specs=[SEM_SPEC, SEM_SPEC] + [HBM_SPEC] * (ci + co) + [VMEM_SPEC],
        out_shape=[pltpu.SemaphoreType.DMA((carry.nsem,)), pltpu.SemaphoreType.DMA((carry.nsem,))]
        + [pltpu.HBM(a.shape, a.dtype) for a in carry.arrays + lands] + [jax.ShapeDtypeStruct((8, 128), F32)],
        input_output_aliases={t: 2 + t for t in range(ci + co)},
        compiler_params=pltpu.CompilerParams(has_side_effects=DATAFLOW),
    )(*[hbm(a) for a in carry.arrays + lands], after)
    return out[0], out[1], list(out[2:2 + ci]), list(out[2 + ci:2 + ci + co]), out[-1]


def _split_wait(carry, send_sems, recv_sems, srcs, lands, after, *, name):
    ci, co = len(srcs), len(lands)

    def body(*refs):
        sends, recvs = carry.plan(refs[:ci], refs[ci:ci + co], refs[ci + co], refs[ci + co + 1])
        for cp in sends:
            cp.wait_send()
        for cp in recvs:
            cp.wait_recv()

    out = pl.pallas_call(
        body, name=name, in_specs=[HBM_SPEC] * (ci + co) + [SEM_SPEC, SEM_SPEC, ANY], out_specs=[HBM_SPEC] * (ci + co),
        out_shape=[pltpu.HBM(a.shape, a.dtype) for a in list(srcs) + list(lands)],
        input_output_aliases={t: t for t in range(ci + co)},
        compiler_params=pltpu.CompilerParams(has_side_effects=DATAFLOW),
    )(*srcs, *lands, send_sems, recv_sems, after)
    return list(out[:ci]), list(out[ci:])


def _gather_start(shards, after, *, name):
    return _split_start(_gather_carry(shards), after, name=name)


def _gather_wait(send_sems, recv_sems, shards, lands, after, *, name):
    srcs, got = _split_wait(_gather_carry(shards), send_sems, recv_sems, shards, lands, after, name=name)
    return _fill_own(got, srcs)


def _ag_small(v, *, name):
    m, n = v.shape

    def body(x_ref, out_ref, red_ref, send_sems, recv_sems, local_sem):
        x, y, c = _pos()
        me, sibling = (x, y, c), (x, y, 1 - c)
        chips = _other_chips(x, y)
        rows = lambda px, py, pc: out_ref.at[pl.ds(pl.multiple_of((4 * px + 2 * py + pc) * m, 8), m), :]
        mine = pltpu.make_async_copy(x_ref, rows(*me), local_sem)
        mine.start()
        first = [_remote(x_ref, rows(*me), send_sems, recv_sems, 0, sibling)]
        first += [_remote(x_ref, rows(*me), send_sems, recv_sems, 1 + k, (*chip, c)) for k, chip in enumerate(chips)]
        for cp in first:
            cp.start()
        passed = [_remote(rows(*chip, c), rows(*chip, c), send_sems, recv_sems, 4 + k, sibling) for k, chip in enumerate(chips)]
        for k, chip in enumerate(chips):
            _remote(rows(*chip, c), rows(*chip, c), send_sems, recv_sems, 1 + k, me).wait_recv()
            passed[k].start()
        _remote(rows(*sibling), rows(*sibling), send_sems, recv_sems, 0, me).wait_recv()
        for k, chip in enumerate(chips):
            _remote(rows(*chip, 1 - c), rows(*chip, 1 - c), send_sems, recv_sems, 4 + k, me).wait_recv()
        for cp in first + passed:
            cp.wait_send()
        mine.wait()
        acc = out_ref[0:m, :]
        for d in range(1, N_DEV):
            acc = acc + out_ref[d * m:(d + 1) * m, :]
        red_ref[...] = acc

    return pl.pallas_call(
        body, name=name, in_specs=[VMEM_SPEC], out_specs=[VMEM_SPEC, VMEM_SPEC],
        out_shape=[jax.ShapeDtypeStruct((N_DEV * m, n), v.dtype), jax.ShapeDtypeStruct((m, n), v.dtype)],
        scratch_shapes=[pltpu.SemaphoreType.DMA((7,)), pltpu.SemaphoreType.DMA((7,)), pltpu.SemaphoreType.DMA],
    )(v)


def _halves(ref, c):
    rh = ref.shape[1] // 2
    return ref.at[:, pl.ds(pl.multiple_of(c * rh, 16), rh), :]


def _rs_swap(gs, *, name):
    nt = len(gs)

    def body(*refs):
        g, theirs = refs[:nt], refs[nt:2 * nt]
        send_sems, recv_sems = refs[2 * nt:]
        x, y, c = _pos()
        swaps = [_remote(_halves(g[t], 1 - c), theirs[t], send_sems, recv_sems, t, (x, y, 1 - c)) for t in range(nt)]
        for cp in swaps:
            cp.start()
        for cp in swaps:
            cp.wait()

    return pl.pallas_call(
        body, name=name, in_specs=[ANY] * nt, out_specs=[ANY] * nt,
        out_shape=[jax.ShapeDtypeStruct((a.shape[0], a.shape[1] // 2, a.shape[2]), a.dtype) for a in gs],
        scratch_shapes=[pltpu.SemaphoreType.DMA((nt,)), pltpu.SemaphoreType.DMA((nt,))],
    )(*gs)


def _exchange_carry(ss):
    def plan(s, out, send_sems, recv_sems):
        x, y, c = _pos()
        chip = 2 * x + y
        others = [(k, px, py, 2 * px + py) for k, (px, py) in enumerate(_other_chips(x, y))]
        sends = [_remote(s[t].at[j], out[t].at[chip], send_sems, recv_sems, 3 * t + k, (px, py, c))
                 for t in range(len(s)) for k, px, py, j in others]
        recvs = [_remote(s[t].at[j], out[t].at[j], send_sems, recv_sems, 3 * t + k, (x, y, c))
                 for t in range(len(s)) for k, _, _, j in others]
        return sends, recvs

    return _Carry(ss, [jax.ShapeDtypeStruct(a.shape, a.dtype) for a in ss], 3 * len(ss), plan)


def _fill_own_slab(outs, ss):
    chip = _chip_index()
    return [lax.dynamic_update_index_in_dim(o, lax.dynamic_index_in_dim(s, chip, 0, keepdims=False), chip, 0)
            for o, s in zip(outs, ss)]


def _rs_join(rs, *, name):
    nt = len(rs)

    def body(*refs):
        r, theirs = refs[:nt], refs[nt:2 * nt]
        send_sems, recv_sems = refs[2 * nt:]
        x, y, c = _pos()
        swaps = [_remote(r[t], theirs[t], send_sems, recv_sems, t, (x, y, 1 - c)) for t in range(nt)]
        for cp in swaps:
            cp.start()
        for cp in swaps:
            cp.wait()

    theirs = pl.pallas_call(
        body, name=name, in_specs=[ANY] * nt, out_specs=[ANY] * nt,
        out_shape=[jax.ShapeDtypeStruct(a.shape, a.dtype) for a in rs],
        scratch_shapes=[pltpu.SemaphoreType.DMA((nt,)), pltpu.SemaphoreType.DMA((nt,))],
    )(*rs)
    first = lax.axis_index("c") == 0
    return [jnp.concatenate([jnp.where(first, a, b), jnp.where(first, b, a)], axis=0) for a, b in zip(rs, theirs)]


def _row_tile(rows, dtype):
    unit = 16 if dtype == BF16 else 8
    if rows <= 512:
        return rows
    return next((t for t in (512, 256, 128, 64, 32, 16, 8) if rows % t == 0 and t % unit == 0), rows)


def _add_pair(a, b, *, name):
    n, rh, cols = a.shape
    tr = _row_tile(rh, BF16)

    def body(a_ref, b_ref, o_ref):
        o_ref[...] = (a_ref[...].astype(F32) + b_ref[...].astype(F32)).astype(BF16)

    blk = pl.BlockSpec((None, tr, cols), lambda j, i: (j, i, 0))
    return pl.pallas_call(
        body, name=name, grid=(n, rh // tr), in_specs=[blk, blk], out_specs=blk,
        out_shape=jax.ShapeDtypeStruct(a.shape, BF16), compiler_params=_cp("parallel", "parallel"),
    )(a, b)


def _sum_chips(b, *, name):
    n, rh, cols = b.shape
    tr = _row_tile(rh, BF16)

    def body(b_ref, o_ref):
        acc = b_ref[0].astype(F32)
        for j in range(1, n):
            acc = acc + b_ref[j].astype(F32)
        o_ref[...] = acc

    return pl.pallas_call(
        body, name=name, grid=(rh // tr,), in_specs=[pl.BlockSpec((n, tr, cols), lambda i: (0, i, 0))],
        out_specs=pl.BlockSpec((tr, cols), lambda i: (i, 0)),
        out_shape=jax.ShapeDtypeStruct((rh, cols), F32), compiler_params=_cp("parallel"),
    )(b)


BIG = ("w_in", "w_out", "w_ffn_in", "w_down")
W_IN_SHARD = (N_MAIN + 2 * GH) // N_CHIP


def _chip_major(name, g):
    if name == "w_in":
        return jnp.stack([g[:, j * W_IN_SHARD:(j + 1) * W_IN_SHARD] for j in range(N_CHIP)])
    if name == "w_ffn_in":
        return g
    return g.reshape(N_CHIP, g.shape[0] // N_CHIP, g.shape[1])


def _rs_pairs(gs, tag):
    c = lax.axis_index("c")
    mine = [lax.dynamic_slice_in_dim(a, c * (a.shape[1] // 2), a.shape[1] // 2, axis=1) for a in gs]
    theirs = _rs_swap(gs, name=f"{tag}_swap")
    return [_add_pair(a, b, name=f"{tag}_add{i}") for i, (a, b) in enumerate(zip(mine, theirs))]


def _rs_sums(got, pairs, tag):
    return [_sum_chips(b, name=f"{tag}_sum{i}") for i, b in enumerate(_fill_own_slab(got, pairs))]


def _size(shape):
    n = 1
    for s in shape:
        n *= s
    return n


def _pack_flat(parts, total, dtype):
    flat = jnp.concatenate([p.reshape(-1).astype(dtype) for p in parts])
    return jnp.pad(flat, (0, total - flat.shape[0]))


def _unpack_flat(flat, shapes):
    out, o = [], 0
    for s in shapes:
        out.append(flat[o:o + _size(s)].reshape(s))
        o += _size(s)
    return out


WEIGHTS = ("attn_norm", "w_in", "attn_sinks", "gdn_conv_w", "gdn_a_log", "gdn_dt_bias", "gdn_norm", "w_out", "ffn_norm",
           "w_ffn_in", "ffn_conv_w", "ffn_conv_b", "w_down", "final_norm")
SMALL = {"attn_norm": (DEPTH, D), "attn_sinks": (DEPTH, HQ), "gdn_a_log": (DEPTH, GH), "gdn_dt_bias": (DEPTH, GH),
         "gdn_norm": (DEPTH, GD), "ffn_norm": (DEPTH, D), "ffn_conv_b": (DEPTH, DFF), "final_norm": (D,)}
CONV_FULL = {"gdn_conv_w": (DEPTH, GK, 3 * GH * GD), "ffn_conv_w": (DEPTH, FK, DFF)}
CONV_SHARD = {"gdn_conv_w": (DEPTH, GK, 3 * GH * GD // N_CHIP), "ffn_conv_w": (DEPTH, FK, DFF // N_CHIP)}
CONV_ROWS, SMALLG_ROWS, SMALLW_ROWS = 64, 320, 144


def kernel(x, attn_norm, w_in, attn_sinks, gdn_conv_w, gdn_a_log, gdn_dt_bias, gdn_norm, w_out, ffn_norm, w_ffn_in, ffn_conv_w, ffn_conv_b, w_down, final_norm, loss_target, m_attn_norm, m_w_in, m_attn_sinks, m_gdn_conv_w, m_gdn_a_log, m_gdn_dt_bias, m_gdn_norm, m_w_out, m_ffn_norm, m_w_ffn_in, m_ffn_conv_w, m_ffn_conv_b, m_w_down, m_final_norm, v_attn_norm, v_w_in, v_attn_sinks, v_gdn_conv_w, v_gdn_a_log, v_gdn_dt_bias, v_gdn_norm, v_w_out, v_ffn_norm, v_w_ffn_in, v_ffn_conv_w, v_ffn_conv_b, v_w_down, v_final_norm):
    w = dict(zip(WEIGHTS, (attn_norm, w_in, attn_sinks, gdn_conv_w, gdn_a_log, gdn_dt_bias, gdn_norm, w_out, ffn_norm,
                           w_ffn_in, ffn_conv_w, ffn_conv_b, w_down, final_norm)))
    m = dict(zip(WEIGHTS, (m_attn_norm, m_w_in, m_attn_sinks, m_gdn_conv_w, m_gdn_a_log, m_gdn_dt_bias, m_gdn_norm, m_w_out,
                           m_ffn_norm, m_w_ffn_in, m_ffn_conv_w, m_ffn_conv_b, m_w_down, m_final_norm)))
    v = dict(zip(WEIGHTS, (v_attn_norm, v_w_in, v_attn_sinks, v_gdn_conv_w, v_gdn_a_log, v_gdn_dt_bias, v_gdn_norm, v_w_out,
                           v_ffn_norm, v_w_ffn_in, v_ffn_conv_w, v_ffn_conv_b, v_w_down, v_final_norm)))
    cx, cy, _ = _pos()
    chip = 2 * cx + cy

    cpack = _pack_flat([w[n] for n in CONV_SHARD], CONV_ROWS * 128, F32).reshape(CONV_ROWS, 128)
    cgath, _ = _ag_small(cpack, name="gather_conv_w")
    cgath = cgath.reshape(N_DEV, CONV_ROWS * 128)
    cshards = [_unpack_flat(cgath[2 * j], list(CONV_SHARD.values())) for j in range(N_CHIP)]
    conv = {n: jnp.concatenate([cshards[j][i] for j in range(N_CHIP)], axis=2) for i, n in enumerate(CONV_SHARD)}

    Ps = [_layer_params(attn_norm[l], attn_sinks[l], conv["gdn_conv_w"][l], gdn_a_log[l], gdn_dt_bias[l], gdn_norm[l],
                        ffn_norm[l], conv["ffn_conv_w"][l], ffn_conv_b[l]) for l in range(DEPTH)]

    wb = {n: [w[n][l].astype(BF16) for l in range(DEPTH)] for n in BIG}
    mixer_w = lambda g_in, g_out: dict(all=_layer_weights(jnp.concatenate(list(g_in), axis=1)), out=g_out.reshape(D, D))
    ffn_w = lambda g_ffn, g_down: dict(ffn=g_ffn, down=g_down.reshape(DFF, D))
    groups = [[wb[a][l], wb[b][l]] for l in range(DEPTH) for a, b in (("w_in", "w_out"), ("w_ffn_in", "w_down"))]
    def after_start(P, key, started):
        return {**P, key: P[key] + started[4][:1, :1]}

    got0 = _gather_halves(groups[0], cgath, name="gather0")
    g1 = _gather_start(groups[1], got0[1], name="gather1_start")
    Wm0 = mixer_w(*got0)
    h, sv0, _ = _mixer_fwd(x[0], Wm0, after_start(Ps[0], "attn_norm", g1), 0)
    got1 = _gather_wait(*g1[:4], h, name="gather1_wait")
    g2 = _gather_start(groups[2], got1[1], name="gather2_start")
    Wf0 = ffn_w(*got1)
    h, sv0f, _ = _ffn_fwd(h, Wf0, after_start(Ps[0], "ffn_norm", g2), 0)
    got2 = _gather_wait(*g2[:4], h, name="gather2_wait")
    g3 = _gather_start(groups[3], got2[1], name="gather3_start")
    Wm1 = mixer_w(*got2)
    h, sv1, _ = _mixer_fwd(h, Wm1, after_start(Ps[1], "attn_norm", g3), 1)
    Wf1 = ffn_w(*_gather_wait(*g3[:4], h, name="gather3_wait"))
    h, sv1f, _ = _ffn_fwd(h, Wf1, Ps[1], 1)
    dx, g_final, loss_part = _loss_head(h, final_norm[None], loss_target[0], name="loss_head")

    def exchange_start(pairs, tag):
        carry = _exchange_carry(pairs)
        return carry, _split_start(carry, pairs[0], name=f"{tag}_exchange_start")

    def exchange_wait(carry, started, after, tag):
        srcs, got = _split_wait(carry, *started[:4], after, name=f"{tag}_exchange_wait")
        return _rs_sums(got, srcs, tag)

    tied = lambda P, key, started: {**P, key: P[key] + started[4][:1, :1]}
    dx, gf1 = _ffn_bwd(dx, {**sv1, **sv1f}, Wf1, Ps[1], 1)
    dx, gm1, _ = _mixer_bwd(dx, sv1, Wm1, Ps[1], 1)
    lg1 = {**gf1, **gm1}
    pairs1 = _rs_pairs([_chip_major(n, lg1[n]) for n in BIG], "rs1")
    ex1 = exchange_start(pairs1, "rs1")
    dx, gf0 = _ffn_bwd(dx, {**sv0, **sv0f}, Wf0, tied(Ps[0], "fcb", ex1[1]), 0)
    pairs0a = _rs_pairs([_chip_major(n, gf0[n]) for n in BIG[2:]], "rs0a")
    ex0a = exchange_start(pairs0a, "rs0a")
    dx, gm0, _ = _mixer_bwd(dx, sv0, Wm0, tied(Ps[0], "gnw", ex0a[1]), 0)
    lg = [{**gf0, **gm0}, lg1]
    pairs0b = _rs_pairs([_chip_major(n, gm0[n]) for n in BIG[:2]], "rs0b")
    sums0b = _rs_sums(_run_carry(_exchange_carry(pairs0b), name="rs0b_exchange"), pairs0b, "rs0b")
    sums1 = exchange_wait(*ex1, sums0b[0], "rs1")
    sums0a = exchange_wait(*ex0a, sums1[0], "rs0a")
    joined = _rs_join(sums0b + sums0a + sums1, name="rs_join")
    reduced = [joined[:4], joined[4:]]
    grad_x = dx[None]
    stacked = lambda n: jnp.stack([lg[l][n] for l in range(DEPTH)])

    small_parts = [g_final[0] if n == "final_norm" else stacked(n) for n in SMALL] + [stacked(n) for n in CONV_FULL]
    svec = _pack_flat(small_parts + [loss_part[0, :1]], SMALLG_ROWS * 128, F32).reshape(SMALLG_ROWS, 128)
    _, sred = _ag_small(svec, name="reduce_small")
    small_g = _unpack_flat(sred.reshape(-1), list(SMALL.values()) + list(CONV_FULL.values()) + [(1,)])
    g = dict(zip(list(SMALL) + list(CONV_FULL), small_g[:-1]))
    loss = small_g[-1][0]
    for n in CONV_FULL:
        wd = CONV_SHARD[n][2]
        g[n] = lax.dynamic_slice_in_dim(g[n], chip * wd, wd, axis=2)

    g.update({n: jnp.stack([reduced[l][i] for l in range(DEPTH)]) for i, n in enumerate(BIG)})

    delta, new_m, new_v = {}, {}, {}
    for n in BIG:
        delta[n], new_m[n], new_v[n] = _adamw(w[n], g[n], m[n], v[n], name=f"adamw_{n}")
    small_names = list(SMALL) + list(CONV_SHARD)
    small_shapes = list(SMALL.values()) + list(CONV_SHARD.values())
    packed = [_pack_flat([t[n] for n in small_names], SMALLW_ROWS * 128, F32).reshape(SMALLW_ROWS, 128) for t in (w, g, m, v)]
    for res, out in zip(_adamw(*packed, name="adamw_small"), (delta, new_m, new_v)):
        out.update(zip(small_names, _unpack_flat(res.reshape(-1), small_shapes)))

    return (loss, grad_x, *[g[n] for n in WEIGHTS], *[delta[n] for n in WEIGHTS], *[new_m[n] for n in WEIGHTS],
            *[new_v[n] for n in WEIGHTS])
```

```python
import functools

import jax
import jax.numpy as jnp
import numpy as np
from jax import lax
from jax.experimental import pallas as pl
from jax.experimental.pallas import tpu as pltpu

F32, BF16 = jnp.float32, jnp.bfloat16
HIGHEST = lax.Precision.HIGHEST
MESH = pl.DeviceIdType.MESH

D = 1024
T = 4096
DEPTH = 2
HQ, HKV, DH, WIN = 8, 2, 64, 128
GH, GD, GC, GK = 4, 128, 64, 4
DFF, FK = 2816, 3
EPS = 1e-6
N_MAIN = 2816
N_PROJ = N_MAIN + 128
COL_K, COL_V, COL_G, COL_Z, COL_BA = 4, 5, 6, 18, 22
N_CHUNK = T // GC
N_DEV, N_CHIP = 8, 4
FFN_CW = 2 * DFF // N_CHIP
LR, B1, B2, AEPS, WD, STEP = 0.001, 0.9, 0.999, 1e-08, 0.01, 10

VMEM_LIMIT = 56 * 1024 * 1024
ANY = pl.BlockSpec(memory_space=pl.ANY)
VMEM_SPEC = pl.BlockSpec(memory_space=pltpu.VMEM)


def _cp(*sem):
    return pltpu.CompilerParams(dimension_semantics=sem if sem else None, vmem_limit_bytes=VMEM_LIMIT)


class _Carry:
    def __init__(self, arrays, out_shapes, nsem, plan):
        self.arrays, self.out_shapes, self.nsem, self.plan = list(arrays), list(out_shapes), nsem, plan


def _seq_call(body, carry, *, name, steps, in_specs, out_specs, out_shape, args, scratch_shapes=(), aliases=None):
    in_specs, out_specs, out_shape, scratch = list(in_specs), list(out_specs), list(out_shape), list(scratch_shapes)
    n_in, n_out, n_scr = len(in_specs), len(out_specs), len(scratch)
    if carry is None:
        fn = body
    else:
        ci, co = len(carry.arrays), len(carry.out_shapes)

        def fn(*refs):
            ins, cins = refs[:n_in], refs[n_in:n_in + ci]
            outs, couts = refs[n_in + ci:n_in + ci + n_out], refs[n_in + ci + n_out:n_in + ci + n_out + co]
            scr, (ssem, rsem) = refs[n_in + ci + n_out + co:-2], refs[-2:]
            sends, recvs = carry.plan(cins, couts, ssem, rsem)

            @pl.when(pl.program_id(0) == 0)
            def _():
                for cp in sends:
                    cp.start()

            body(*ins, *outs, *scr)

            @pl.when(pl.program_id(0) == steps - 1)
            def _():
                for cp in recvs:
                    cp.wait_recv()
                for cp in sends:
                    cp.wait_send()

        in_specs += [ANY] * ci
        out_specs += [ANY] * co
        out_shape += carry.out_shapes
        scratch += [pltpu.SemaphoreType.DMA((carry.nsem,)), pltpu.SemaphoreType.DMA((carry.nsem,))]
        args = list(args) + carry.arrays
    outs = pl.pallas_call(
        fn, name=name, grid=(steps,), in_specs=in_specs, out_specs=out_specs, out_shape=out_shape,
        scratch_shapes=scratch, input_output_aliases=aliases or {}, compiler_params=_cp("arbitrary"),
    )(*args)
    return list(outs[:n_out]), list(outs[n_out:])


def _dot(a, b):
    return jnp.dot(a.astype(BF16), b.astype(BF16), preferred_element_type=F32)


def _dot_nt(a, b):
    return lax.dot_general(a.astype(BF16), b.astype(BF16), (((1,), (1,)), ((), ())), preferred_element_type=F32)


def _dot_tn(a, b):
    return lax.dot_general(a.astype(BF16), b.astype(BF16), (((0,), (0,)), ((), ())), preferred_element_type=F32)


def _iota(shape, dim):
    return lax.broadcasted_iota(jnp.int32, shape, dim)


def _col(x, idx):
    return jnp.sum(jnp.where(_iota(x.shape, 1) == idx, x, 0.0), axis=1, keepdims=True)


def _silu(y):
    return y * jax.nn.sigmoid(y)


_MM_DN = {"nn": (((1,), (0,)), ((), ())), "nt": (((1,), (1,)), ((), ())), "tn": (((0,), (0,)), ((), ()))}


def _mm(a, b, *, name, mode, dims, tm, tn, tk, res=None, out_dtype=F32, a_spec=None, b_spec=None, out_spec=None,
        out_shape=None, n_outer=False):
    M, N, K = dims
    assert M % tm == 0 and N % tn == 0 and K % tk == 0, (name, dims)
    nk = K // tk
    dn = _MM_DN[mode]

    def body(*refs):
        a_ref, b_ref = refs[:2]
        r_ref = None if res is None else refs[2]
        o_ref = refs[2 if res is None else 3]
        part = lax.dot_general(a_ref[...].astype(BF16), b_ref[...].astype(BF16), dn, preferred_element_type=F32)
        if nk == 1:
            o_ref[...] = (part if res is None else part + r_ref[...]).astype(out_dtype)
            return
        acc, k = refs[-1], pl.program_id(2)

        @pl.when(k == 0)
        def _():
            acc[...] = part

        @pl.when(k > 0)
        def _():
            acc[...] += part

        @pl.when(k == nk - 1)
        def _():
            r = acc[...]
            if res is not None:
                r = r + r_ref[...]
            o_ref[...] = r.astype(out_dtype)

    if a_spec is None:
        a_spec = pl.BlockSpec((tk, tm), lambda i, j, k: (k, i)) if mode == "tn" else pl.BlockSpec((tm, tk), lambda i, j, k: (i, k))
    if b_spec is None:
        b_spec = pl.BlockSpec((tn, tk), lambda i, j, k: (j, k)) if mode == "nt" else pl.BlockSpec((tk, tn), lambda i, j, k: (k, j))
    in_specs, args = [a_spec, b_spec], [a, b]
    if res is not None:
        in_specs.append(pl.BlockSpec((tm, tn), lambda i, j, k: (i, j)))
        args.append(res)
    out_spec = out_spec or pl.BlockSpec((tm, tn), lambda i, j, k: (i, j))
    grid = (M // tm, N // tn, nk)
    if n_outer:
        swap = lambda s: pl.BlockSpec(s.block_shape, lambda j, i, k, f=s.index_map: f(i, j, k))
        in_specs, out_spec, grid = [swap(s) for s in in_specs], swap(out_spec), (N // tn, M // tm, nk)
    return pl.pallas_call(
        body, name=name, grid=grid, in_specs=in_specs, out_specs=out_spec,
        out_shape=out_shape or jax.ShapeDtypeStruct((M, N), out_dtype),
        scratch_shapes=[pltpu.VMEM((tm, tn), F32)] if nk > 1 else [],
        compiler_params=_cp("parallel", "parallel", "arbitrary"),
    )(*args)


def _rms(x, w):
    return x * lax.rsqrt(jnp.mean(x * x, axis=-1, keepdims=True) + EPS) * w


def _rmsnorm_fwd(x, w, *, name, tm=512):
    def body(x_ref, w_ref, o_ref):
        o_ref[...] = _rms(x_ref[...], w_ref[...]).astype(BF16)

    return pl.pallas_call(
        body, name=name, grid=(x.shape[0] // tm,),
        in_specs=[pl.BlockSpec((tm, D), lambda i: (i, 0)), pl.BlockSpec((1, D), lambda i: (0, 0))],
        out_specs=pl.BlockSpec((tm, D), lambda i: (i, 0)),
        out_shape=jax.ShapeDtypeStruct(x.shape, BF16), compiler_params=_cp("parallel"),
    )(x, w)


def _rmsnorm_proj(x, w, w_all, *, name, tm=512):
    def body(x_ref, w_ref, b_ref, h_ref, p_ref, ba_ref):
        h = _rms(x_ref[...], w_ref[...]).astype(BF16)
        h_ref[...] = h
        r = jnp.dot(h, b_ref[...], preferred_element_type=F32)
        p_ref[...] = r[:, :N_MAIN].astype(BF16)
        ba_ref[...] = r[:, N_MAIN:]

    row = lambda n: pl.BlockSpec((tm, n), lambda i: (i, 0))
    return pl.pallas_call(
        body, name=name, grid=(x.shape[0] // tm,),
        in_specs=[row(D), pl.BlockSpec((1, D), lambda i: (0, 0)), pl.BlockSpec((D, N_PROJ), lambda i: (0, 0))],
        out_specs=[row(D), row(N_MAIN), row(N_PROJ - N_MAIN)],
        out_shape=[jax.ShapeDtypeStruct((x.shape[0], D), BF16), jax.ShapeDtypeStruct((x.shape[0], N_MAIN), BF16),
                   jax.ShapeDtypeStruct((x.shape[0], N_PROJ - N_MAIN), F32)],
        compiler_params=_cp("parallel"),
    )(x, w, w_all)


def _mm_rmsnorm_bwd(a, b, x, w, dres, *, name, K, tm, tk, a_spec=None, b_spec=None):
    M = x.shape[0]
    nk = K // tk
    assert M % tm == 0 and K % tk == 0, (name, M, K)

    def body(a_ref, b_ref, x_ref, w_ref, dr_ref, dx_ref, dw_ref, acc):
        i, k = pl.program_id(0), pl.program_id(1)
        part = lax.dot_general(a_ref[...].astype(BF16), b_ref[...].astype(BF16), _MM_DN["nt"], preferred_element_type=F32)

        @pl.when(k == 0)
        def _():
            acc[...] = part

        @pl.when(k > 0)
        def _():
            acc[...] += part

        @pl.when((i == 0) & (k == 0))
        def _():
            dw_ref[...] = jnp.zeros_like(dw_ref)

        @pl.when(k == nk - 1)
        def _():
            _, vjp = jax.vjp(_rms, x_ref[...], w_ref[...])
            dx, dw = vjp(acc[...])
            dx_ref[...] = dx + dr_ref[...]
            dw_ref[...] += dw

    row = pl.BlockSpec((tm, D), lambda i, k: (i, 0))
    vec = pl.BlockSpec((1, D), lambda i, k: (0, 0))
    return pl.pallas_call(
        body, name=name, grid=(M // tm, nk),
        in_specs=[a_spec or pl.BlockSpec((tm, tk), lambda i, k: (i, k)), b_spec or pl.BlockSpec((D, tk), lambda i, k: (0, k)),
                  row, vec, row],
        out_specs=[row, vec], out_shape=[jax.ShapeDtypeStruct((M, D), F32), jax.ShapeDtypeStruct((1, D), F32)],
        scratch_shapes=[pltpu.VMEM((tm, D), F32)], compiler_params=_cp("arbitrary", "arbitrary"),
    )(a, b, x, w, dres)


def _rmsnorm_bwd(x, w, dh, dres, *, name, tm=512):
    def body(x_ref, w_ref, dh_ref, dr_ref, dx_ref, dw_ref):
        _, vjp = jax.vjp(_rms, x_ref[...], w_ref[...])
        dx, dw = vjp(dh_ref[...])
        dx_ref[...] = dx + dr_ref[...]

        @pl.when(pl.program_id(0) == 0)
        def _():
            dw_ref[...] = jnp.zeros_like(dw_ref)

        dw_ref[...] += dw

    row = pl.BlockSpec((tm, D), lambda i: (i, 0))
    vec = pl.BlockSpec((1, D), lambda i: (0, 0))
    return pl.pallas_call(
        body, name=name, grid=(x.shape[0] // tm,), in_specs=[row, vec, row, row], out_specs=[row, vec],
        out_shape=[jax.ShapeDtypeStruct(x.shape, F32), jax.ShapeDtypeStruct((1, D), F32)],
        compiler_params=_cp("arbitrary"),
    )(x, w, dh, dres)


def _loss_head(x, w, tgt, *, name, tm=512):
    def f(xv, wv, tv):
        err = _rms(xv, wv) - tv
        per_row = jnp.sum(err * err, axis=1, keepdims=True) * (0.5 / D)
        return jnp.sum(per_row, axis=0, keepdims=True)

    def body(x_ref, w_ref, t_ref, dx_ref, dw_ref, loss_ref):
        tv = t_ref[...]
        loss, vjp = jax.vjp(lambda xv, wv: f(xv, wv, tv), x_ref[...], w_ref[...])
        dx, dw = vjp(jnp.ones((1, 1), F32))
        dx_ref[...] = dx

        @pl.when(pl.program_id(0) == 0)
        def _():
            dw_ref[...] = jnp.zeros_like(dw_ref)
            loss_ref[...] = jnp.zeros_like(loss_ref)

        dw_ref[...] += dw
        loss_ref[...] += jnp.broadcast_to(loss, loss_ref.shape)

    row = pl.BlockSpec((tm, D), lambda i: (i, 0))
    vec = pl.BlockSpec((1, D), lambda i: (0, 0))
    return pl.pallas_call(
        body, name=name, grid=(x.shape[0] // tm,), in_specs=[row, vec, row],
        out_specs=[row, vec, pl.BlockSpec((1, 128), lambda i: (0, 0))],
        out_shape=[jax.ShapeDtypeStruct(x.shape, F32), jax.ShapeDtypeStruct((1, D), F32),
                   jax.ShapeDtypeStruct((1, 128), F32)],
        compiler_params=_cp("arbitrary"),
    )(x, w, tgt)


def _swa_bias():
    G = HQ // HKV
    r, c = np.arange(G * WIN)[:, None], np.arange(2 * WIN)[None, :]
    rel = (r % WIN) + WIN - c
    band = (rel >= 0) & (rel < WIN)
    out = np.empty((2, HKV, G * WIN, 2 * WIN), np.float32)
    for h2 in range(HKV):
        slope = 2.0 ** (-8.0 * (h2 * G + r // WIN + 1) / HQ)
        out[0, h2] = np.where(band & (c >= WIN), -slope * rel, -1e30)
        out[1, h2] = np.where(band, -slope * rel, -1e30)
    return jnp.asarray(out)


def _swa_heads(qs, k2s, v2s, sk, biases, hqs=range(HQ)):
    first_kv = hqs[0] // (HQ // HKV)
    kv = lambda xs: [xs[hq // (HQ // HKV) - first_kv] for hq in hqs]
    ss = _hmap(lambda q, k2, b: _dot_nt(q, k2) * (DH ** -0.5) + b, qs, kv(k2s), biases)
    sinks = [_col(sk, hq) for hq in hqs]
    ms = _hmap(lambda s, sink: lax.stop_gradient(jnp.maximum(jnp.max(s, axis=1, keepdims=True), sink)), ss, sinks)
    ps = _hmap(lambda s, m: jnp.exp(s - m), ss, ms)
    dens = _hmap(lambda p, sink, m: jnp.sum(p, axis=1, keepdims=True) + jnp.exp(sink - m), ps, sinks, ms)
    return _hmap(lambda p, den, v2: _dot(p * (1.0 / den), v2), ps, dens, kv(v2s))


def _swa_load(q_ref, kp_ref, kc_ref, vp_ref, vc_ref, bias_ref):
    q, G = q_ref[...].astype(F32), HQ // HKV
    qs = [q[:, hq * DH:(hq + 1) * DH] for hq in range(HQ)]
    both = lambda a, b, h2: jnp.concatenate([a[:, h2 * DH:(h2 + 1) * DH], b[:, h2 * DH:(h2 + 1) * DH]], axis=0).astype(F32)
    k2s = [both(kp_ref, kc_ref, h2) for h2 in range(HKV)]
    v2s = [both(vp_ref, vc_ref, h2) for h2 in range(HKV)]
    biases = [bias_ref[hq // G, (hq % G) * WIN:(hq % G + 1) * WIN, :] for hq in range(HQ)]
    return qs, k2s, v2s, biases


def _swa_specs():
    prev = lambda i: jnp.maximum(jnp.minimum(i, T // WIN - 1) - 1, 0)
    cur = lambda i: jnp.minimum(i, T // WIN - 1)
    return [
        pl.BlockSpec((WIN, HQ * DH), lambda i: (cur(i), 0)),
        pl.BlockSpec((WIN, 128), lambda i: (prev(i), COL_K)),
        pl.BlockSpec((WIN, 128), lambda i: (cur(i), COL_K)),
        pl.BlockSpec((WIN, 128), lambda i: (prev(i), COL_V)),
        pl.BlockSpec((WIN, 128), lambda i: (cur(i), COL_V)),
        pl.BlockSpec((1, 128), lambda i: (0, 0)),
        pl.BlockSpec((None, HKV, (HQ // HKV) * WIN, 2 * WIN), lambda i: (jnp.minimum(i, 1), 0, 0, 0)),
    ]


def _swa_fwd(proj, sinks, *, name):
    def body(q_ref, kp_ref, kc_ref, vp_ref, vc_ref, sk_ref, bias_ref, o_ref):
        qs, k2s, v2s, biases = _swa_load(q_ref, kp_ref, kc_ref, vp_ref, vc_ref, bias_ref)
        o_ref[...] = jnp.concatenate(_swa_heads(qs, k2s, v2s, sk_ref[...], biases), axis=1).astype(BF16)

    return pl.pallas_call(
        body, name=name, grid=(T // WIN,), in_specs=_swa_specs(),
        out_specs=pl.BlockSpec((WIN, HQ * DH), lambda i: (i, 0)),
        out_shape=jax.ShapeDtypeStruct((T, HQ * DH + GH * GD), BF16), compiler_params=_cp("parallel"),
    )(proj, proj, proj, proj, proj, sinks, _swa_bias())


def _swa_bwd(proj, sinks, do, dproj, *, name):
    NB = T // WIN
    QW = HQ * DH

    def body(q_ref, kp_ref, kc_ref, vp_ref, vc_ref, sk_ref, bias_ref, do_ref, _, out_ref, dsk_ref, cq, ck, cv):
        i = pl.program_id(0)

        @pl.when(i == 0)
        def _():
            cq[...] = jnp.zeros_like(cq)
            ck[...] = jnp.zeros_like(ck)
            cv[...] = jnp.zeros_like(cv)
            dsk_ref[...] = jnp.zeros_like(dsk_ref)

        @pl.when(i < NB)
        def _():
            qs, k2s, v2s, biases = _swa_load(q_ref, kp_ref, kc_ref, vp_ref, vc_ref, bias_ref)
            dov = do_ref[...].astype(F32)
            dqs, dk2s, dv2s, dsk = [], [], [], jnp.zeros_like(sk_ref[...])
            for h2 in range(HKV):
                hqs = range(h2 * (HQ // HKV), (h2 + 1) * (HQ // HKV))
                _, vjp = jax.vjp(functools.partial(_swa_heads, biases=[biases[hq] for hq in hqs], hqs=hqs),
                                 [qs[hq] for hq in hqs], [k2s[h2]], [v2s[h2]], sk_ref[...])
                dq, dk2, dv2, dsk_h = vjp([dov[:, hq * DH:(hq + 1) * DH] for hq in hqs])
                dqs, dk2s, dv2s, dsk = dqs + dq, dk2s + dk2, dv2s + dv2, dsk + dsk_h
            out_ref[:, :QW] = cq[...].astype(BF16)
            out_ref[:, QW:QW + 128] = (ck[...] + jnp.concatenate([d[:WIN] for d in dk2s], axis=1)).astype(BF16)
            out_ref[:, QW + 128:] = (cv[...] + jnp.concatenate([d[:WIN] for d in dv2s], axis=1)).astype(BF16)
            cq[...] = jnp.concatenate(dqs, axis=1)
            ck[...] = jnp.concatenate([d[WIN:] for d in dk2s], axis=1)
            cv[...] = jnp.concatenate([d[WIN:] for d in dv2s], axis=1)
            dsk_ref[...] += dsk

        @pl.when(i == NB)
        def _():
            out_ref[:, :QW] = cq[...].astype(BF16)
            out_ref[:, QW:QW + 128] = ck[...].astype(BF16)
            out_ref[:, QW + 128:] = cv[...].astype(BF16)

    qblk = pl.BlockSpec((WIN, QW), lambda i: (jnp.minimum(i, NB - 1), 0))
    return pl.pallas_call(
        body, name=name, grid=(NB + 1,), in_specs=_swa_specs() + [qblk, ANY],
        out_specs=[pl.BlockSpec((WIN, QW + 256), lambda i: (jnp.maximum(i - 1, 0), 0)), pl.BlockSpec((1, 128), lambda i: (0, 0))],
        out_shape=[jax.ShapeDtypeStruct(dproj.shape, dproj.dtype), jax.ShapeDtypeStruct((1, 128), F32)],
        scratch_shapes=[pltpu.VMEM((WIN, QW), F32), pltpu.VMEM((WIN, 128), F32), pltpu.VMEM((WIN, 128), F32)],
        input_output_aliases={8: 0}, compiler_params=_cp("arbitrary"),
    )(proj, proj, proj, proj, proj, sinks, _swa_bias(), do, dproj)


RC = 256
HALO = 8


def _load_ext(ref, c):
    nch = T // RC
    r0 = pl.multiple_of(c * RC, RC)
    p0 = pl.multiple_of(jnp.maximum(r0 - HALO, 0), HALO)
    n0 = pl.multiple_of(jnp.minimum(r0 + RC, T - HALO), HALO)
    prev = jnp.where(c > 0, ref[pl.ds(p0, HALO), :].astype(F32), 0.0)
    nxt = jnp.where(c < nch - 1, ref[pl.ds(n0, HALO), :].astype(F32), 0.0)
    return jnp.concatenate([prev, ref[pl.ds(r0, RC), :].astype(F32), nxt], axis=0)


def _conv_ext(xe, w, K):
    y = w[K - 1:K, :] * xe
    for s in range(1, K):
        y = y + w[K - 1 - s:K - s, :] * pltpu.roll(xe, s, 0)
    return y


def _conv_bwd_ext(xe, dye, w, K, dw_ref):
    n = xe.shape[0]
    own = slice(HALO, HALO + RC)
    dx = w[K - 1:K, :] * dye
    dw_ref[K - 1:K, :] += jnp.sum(dye[own] * xe[own], axis=0, keepdims=True)
    for s in range(1, K):
        dx = dx + w[K - 1 - s:K - s, :] * pltpu.roll(dye, n - s, 0)
        dw_ref[K - 1 - s:K - s, :] += jnp.sum(dye[own] * pltpu.roll(xe, s, 0)[own], axis=0, keepdims=True)
    return dx[own]


def _gdn_post_conv(y, is_qk):
    a = _silu(y)
    nrm = a * lax.rsqrt(jnp.sum(a * a, axis=1, keepdims=True) + EPS)
    return jnp.where(is_qk, nrm, a)


def _gdn_prep_fwd(proj, conv_w, *, name):
    nblk = 3 * GH

    def body(x_ref, w_ref, o_ref):
        is_qk = pl.program_id(0) < 2 * GH
        w = w_ref[...]

        def chunk(c, carry):
            y = _conv_ext(_load_ext(x_ref, c), w, GK)[HALO:HALO + RC]
            o_ref[pl.ds(pl.multiple_of(c * RC, RC), RC), :] = _gdn_post_conv(y, is_qk)
            return carry

        lax.fori_loop(0, T // RC, chunk, 0)

    return pl.pallas_call(
        body, name=name, grid=(nblk,),
        in_specs=[pl.BlockSpec((T, 128), lambda j: (0, COL_G + j)), pl.BlockSpec((GK, 128), lambda j: (0, j))],
        out_specs=pl.BlockSpec((T, 128), lambda j: (0, j)),
        out_shape=jax.ShapeDtypeStruct((T, nblk * 128), F32), compiler_params=_cp("parallel"),
    )(proj, conv_w)


def _gdn_prep_bwd(proj, conv_w, dout, dproj, *, name):
    nblk = 3 * GH

    def body(x_ref, w_ref, d_ref, _, dx_ref, dw_ref):
        is_qk = pl.program_id(0) < 2 * GH
        w = w_ref[...]
        dw_ref[...] = jnp.zeros_like(dw_ref)

        def chunk(c, carry):
            xe = _load_ext(x_ref, c)
            _, vjp = jax.vjp(lambda y: _gdn_post_conv(y, is_qk), _conv_ext(xe, w, GK))
            (dye,) = vjp(_load_ext(d_ref, c))
            dx_ref[pl.ds(pl.multiple_of(c * RC, RC), RC), :] = _conv_bwd_ext(xe, dye, w, GK, dw_ref).astype(BF16)
            return carry

        lax.fori_loop(0, T // RC, chunk, 0)

    return pl.pallas_call(
        body, name=name, grid=(nblk,),
        in_specs=[pl.BlockSpec((T, 128), lambda j: (0, COL_G + j)), pl.BlockSpec((GK, 128), lambda j: (0, j)),
                  pl.BlockSpec((T, 128), lambda j: (0, j)), ANY],
        out_specs=[pl.BlockSpec((T, 128), lambda j: (0, COL_G + j)), pl.BlockSpec((GK, 128), lambda j: (0, j))],
        out_shape=[jax.ShapeDtypeStruct(dproj.shape, dproj.dtype), jax.ShapeDtypeStruct((GK, nblk * 128), F32)],
        input_output_aliases={3: 0}, compiler_params=_cp("parallel"),
    )(proj, conv_w, dout, dproj)


def _ffn_post_conv(y, b, up):
    return _silu(y + b) * up


def _ffn_act_fwd(gu, conv_w, conv_b, *, name, carry=None):
    nblk = DFF // 128

    def body(g_ref, u_ref, w_ref, b_ref, o_ref):
        w, b = w_ref[...], b_ref[...]

        def chunk(c, carry):
            rows = pl.ds(pl.multiple_of(c * RC, RC), RC)
            y = _conv_ext(_load_ext(g_ref, c), w, FK)[HALO:HALO + RC]
            o_ref[rows, :] = _ffn_post_conv(y, b, u_ref[rows, :].astype(F32)).astype(BF16)
            return carry

        lax.fori_loop(0, T // RC, chunk, 0)

    (act,), carried = _seq_call(
        body, carry, name=name, steps=nblk,
        in_specs=[pl.BlockSpec((T, 128), lambda j: (0, j)), pl.BlockSpec((T, 128), lambda j: (0, nblk + j)),
                  pl.BlockSpec((FK, 128), lambda j: (0, j)), pl.BlockSpec((1, 128), lambda j: (0, j))],
        out_specs=[pl.BlockSpec((T, 128), lambda j: (0, j))], out_shape=[jax.ShapeDtypeStruct((T, DFF), BF16)],
        args=(gu, gu, conv_w, conv_b))
    return act, carried


def _ffn_act_bwd(gu, conv_w, conv_b, dact, *, name):
    nblk = DFF // 128

    def body(g_ref, u_ref, w_ref, b_ref, d_ref, dgu_ref, dw_ref, db_ref):
        dg_ref, du_ref = dgu_ref.at[0], dgu_ref.at[1]
        w, b = w_ref[...], b_ref[...]
        dw_ref[...] = jnp.zeros_like(dw_ref)
        db_ref[...] = jnp.zeros_like(db_ref)

        def chunk(c, carry):
            rows = pl.ds(pl.multiple_of(c * RC, RC), RC)
            xe = _load_ext(g_ref, c)
            ue = _load_ext(u_ref, c)
            _, vjp = jax.vjp(_ffn_post_conv, _conv_ext(xe, w, FK), b, ue)
            dye, db, due = vjp(_load_ext(d_ref, c))
            du_ref[rows, :] = due[HALO:HALO + RC].astype(BF16)
            db_ref[...] += jnp.sum(dye[HALO:HALO + RC], axis=0, keepdims=True)
            dg_ref[rows, :] = _conv_bwd_ext(xe, dye, w, FK, dw_ref).astype(BF16)
            return carry

        lax.fori_loop(0, T // RC, chunk, 0)

    col = pl.BlockSpec((T, 128), lambda j: (0, j))
    return pl.pallas_call(
        body, name=name, grid=(nblk,),
        in_specs=[col, pl.BlockSpec((T, 128), lambda j: (0, nblk + j)), pl.BlockSpec((FK, 128), lambda j: (0, j)),
                  pl.BlockSpec((1, 128), lambda j: (0, j)), col],
        out_specs=[pl.BlockSpec((2, T, 128), lambda j: (0, 0, j)), pl.BlockSpec((FK, 128), lambda j: (0, j)),
                   pl.BlockSpec((1, 128), lambda j: (0, j))],
        out_shape=[jax.ShapeDtypeStruct((2, T, DFF), BF16), jax.ShapeDtypeStruct((FK, DFF), F32),
                   jax.ShapeDtypeStruct((1, DFF), F32)],
        compiler_params=_cp("parallel"),
    )(gu, gu, conv_w, conv_b, dact)


def _gdn_gates(ba, alog, dtb):
    beta = jax.nn.sigmoid(ba)
    g = -jnp.exp(alog) * jax.nn.softplus(ba + dtb)
    tril = (_iota((GC, GC), 0) >= _iota((GC, GC), 1)).astype(F32)
    return beta, jnp.dot(tril, g, precision=HIGHEST, preferred_element_type=F32)


def _hmap(f, *lists):
    return [f(*xs) for xs in zip(*lists)]


def _gdn_cols(beta_all, gc_all):
    return [_col(beta_all, h) for h in range(GH)], [_col(gc_all, GH + h) for h in range(GH)]


def _gdn_decay(Gcs):
    r, c = _iota((GC, GC), 0), _iota((GC, GC), 1)
    eye, ones = (r == c).astype(F32), jnp.ones((GC, GC), F32)
    grows = _hmap(lambda G: jnp.dot(ones, eye * G, precision=HIGHEST, preferred_element_type=F32), Gcs)
    return _hmap(lambda G, grow: jnp.exp(jnp.where(r >= c, G - grow, -1e30)), Gcs, grows)


def _gdn_pre(beta_all, gc_all):
    betas, Gcs = _gdn_cols(beta_all, gc_all)
    return betas, Gcs, _gdn_decay(Gcs)


def _gdn_A(ks, betas, decays):
    strict = _iota((GC, GC), 0) > _iota((GC, GC), 1)
    kk = _hmap(lambda k, b: _dot_nt(k * b, k), ks, betas)
    return _hmap(lambda a, d: jnp.where(strict, a * d, 0.0), kk, decays)


def _tri_inv(As):
    eye = (_iota((GC, GC), 0) == _iota((GC, GC), 1)).astype(F32)
    Tms, Ps = [eye - A for A in As], As
    for _ in range(GC.bit_length() - 2):
        Ps = _hmap(lambda P: _dot(P, P), Ps)
        Tms = _hmap(lambda Tm, P: Tm + _dot(Tm, P), Tms, Ps)
    return Tms


def _gdn_chunk(qs, ks, vs, betas, Gcs, decays, Ss, Tms):
    eGs = _hmap(jnp.exp, Gcs)
    us = _hmap(lambda Tm, v, b: _dot(Tm, v * b), Tms, vs, betas)
    ws = _hmap(lambda Tm, k, b, eG: _dot(Tm, k * b * eG), Tms, ks, betas, eGs)
    qss = [q * (GD ** -0.5) for q in qs]
    qks = _hmap(lambda q, k, d: _dot_nt(q, k) * d, qss, ks, decays)
    glasts = [jnp.sum(jnp.where(_iota(G.shape, 0) == GC - 1, G, 0.0), axis=0, keepdims=True) for G in Gcs]
    kds = _hmap(lambda k, gl, G: k * jnp.exp(gl - G), ks, glasts, Gcs)
    v_news = _hmap(lambda u, w, S: u - _dot(w, S), us, ws, Ss)
    qS = _hmap(lambda q, eG, S: _dot(q * eG, S), qss, eGs, Ss)
    os = _hmap(lambda a, qk, vn: a + _dot(qk, vn), qS, qks, v_news)
    S_news = _hmap(lambda S, gl, kd, vn: S * jnp.exp(gl) + _dot_tn(kd, vn), Ss, glasts, kds, v_news)
    return os, S_news


def _gdn_chunk_fwd(qkv_c, ba, alog, dtb, *, name, carry=None):
    W3 = 3 * GH * GD

    def body(x_ref, ba_ref, al_ref, dt_ref, o_ref, s_ref, t_ref, S):
        @pl.when(pl.program_id(0) == 0)
        def _():
            S[...] = jnp.zeros_like(S)

        beta_all, gc_all = _gdn_gates(ba_ref[...], al_ref[...], dt_ref[...])
        qs, ks, vs = ([x_ref[:, (p * GH + h) * GD:(p * GH + h + 1) * GD] for h in range(GH)] for p in range(3))
        betas, Gcs, decays = _gdn_pre(beta_all, gc_all)
        Tms = _tri_inv(_gdn_A(ks, betas, decays))
        Ss = [S[h] for h in range(GH)]
        os, S_news = _gdn_chunk(qs, ks, vs, betas, Gcs, decays, Ss, Tms)
        for h in range(GH):
            s_ref[0, h] = Ss[h]
            t_ref[0, h] = Tms[h]
            o_ref[:, h * GD:(h + 1) * GD] = os[h]
            S[h] = S_news[h]

    vec = pl.BlockSpec((1, 128), lambda n: (0, 0))
    return _seq_call(
        body, carry, name=name, steps=N_CHUNK,
        in_specs=[pl.BlockSpec((GC, W3), lambda n: (n, 0)), pl.BlockSpec((GC, 128), lambda n: (n, 0)), vec, vec],
        out_specs=[pl.BlockSpec((GC, GH * GD), lambda n: (n, 0)),
                   pl.BlockSpec((1, GH, GD, GD), lambda n: (n, 0, 0, 0)),
                   pl.BlockSpec((1, GH, GC, GC), lambda n: (n, 0, 0, 0))],
        out_shape=[jax.ShapeDtypeStruct((T, GH * GD), F32), jax.ShapeDtypeStruct((N_CHUNK, GH, GD, GD), F32),
                   jax.ShapeDtypeStruct((N_CHUNK, GH, GC, GC), F32)],
        scratch_shapes=[pltpu.VMEM((GH, GD, GD), F32)], args=(qkv_c, ba, alog, dtb))


def _gdn_chunk_bwd(qkv_c, ba, alog, dtb, s_all, t_all, do, dproj, *, name, carry=None):
    W3 = 3 * GH * GD
    rev = lambda n: N_CHUNK - 1 - n

    def body(x_ref, ba_ref, al_ref, dt_ref, s_ref, t_ref, do_ref, _, dx_ref, dba_ref, dal_ref, ddt_ref, dS):
        @pl.when(pl.program_id(0) == 0)
        def _():
            dS[...] = jnp.zeros_like(dS)
            dal_ref[...] = jnp.zeros_like(dal_ref)
            ddt_ref[...] = jnp.zeros_like(ddt_ref)

        (beta_all, gc_all), vjp_gates = jax.vjp(_gdn_gates, ba_ref[...], al_ref[...], dt_ref[...])
        qs, ks, vs = ([x_ref[:, (p * GH + h) * GD:(p * GH + h + 1) * GD] for h in range(GH)] for p in range(3))
        Tms = [t_ref[0, h] for h in range(GH)]
        Ss = [s_ref[0, h] for h in range(GH)]
        dos = [do_ref[:, h * GD:(h + 1) * GD] for h in range(GH)]
        dSs = [dS[h] for h in range(GH)]

        (betas, Gcs, decays), vjp_pre = jax.vjp(_gdn_pre, beta_all, gc_all)
        _, vjp = jax.vjp(_gdn_chunk, qs, ks, vs, betas, Gcs, decays, Ss, Tms)
        dqs, dks, dvs, dbetas, dGcs, ddecays, dS_prev, dTs = vjp((dos, dSs))
        dXs = _hmap(_dot_nt, dTs, Tms)
        dAs = _hmap(lambda Tm, dX: -_dot_tn(Tm, dX), Tms, dXs)
        _, vjp_a = jax.vjp(_gdn_A, ks, betas, decays)
        dks2, dbetas2, ddecays2 = vjp_a(dAs)
        add = lambda a, b: _hmap(jnp.add, a, b)
        db_all, dg_all = vjp_pre((add(dbetas, dbetas2), dGcs, add(ddecays, ddecays2)))
        for h in range(GH):
            dS[h] = dS_prev[h]
            dx_ref[:, h * GD:(h + 1) * GD] = dqs[h]
            dx_ref[:, (GH + h) * GD:(GH + h + 1) * GD] = dks[h] + dks2[h]
            dx_ref[:, (2 * GH + h) * GD:(2 * GH + h + 1) * GD] = dvs[h]
        dba, dal, ddt = vjp_gates((db_all, dg_all))
        dba_ref[...] = dba.astype(BF16)
        dal_ref[...] += dal
        ddt_ref[...] += ddt

    vec = pl.BlockSpec((1, 128), lambda n: (0, 0))
    return _seq_call(
        body, carry, name=name, steps=N_CHUNK,
        in_specs=[pl.BlockSpec((GC, W3), lambda n: (rev(n), 0)), pl.BlockSpec((GC, 128), lambda n: (rev(n), 0)),
                  vec, vec, pl.BlockSpec((1, GH, GD, GD), lambda n: (rev(n), 0, 0, 0)),
                  pl.BlockSpec((1, GH, GC, GC), lambda n: (rev(n), 0, 0, 0)),
                  pl.BlockSpec((GC, GH * GD), lambda n: (rev(n), 0)),
                  ANY],
        out_specs=[pl.BlockSpec((GC, W3), lambda n: (rev(n), 0)), pl.BlockSpec((GC, 128), lambda n: (rev(n), COL_BA)), vec, vec],
        out_shape=[jax.ShapeDtypeStruct((T, W3), F32), jax.ShapeDtypeStruct(dproj.shape, dproj.dtype),
                   jax.ShapeDtypeStruct((1, 128), F32), jax.ShapeDtypeStruct((1, 128), F32)],
        scratch_shapes=[pltpu.VMEM((GH, GD, GD), F32)], aliases={7: 1},
        args=(qkv_c, ba, alog, dtb, s_all, t_all, do, dproj))


def _gdn_post(o, z, nw):
    return o * lax.rsqrt(jnp.mean(o * o, axis=-1, keepdims=True) + EPS) * nw * _silu(z)


def _gdn_post_fwd(o_raw, proj, nw, mixed, *, name, tm=512):
    def body(o_ref, z_ref, w_ref, _, out_ref):
        out_ref[...] = _gdn_post(o_ref[...], z_ref[...].astype(F32), w_ref[...]).astype(BF16)

    return pl.pallas_call(
        body, name=name, grid=(T // tm, GH),
        in_specs=[pl.BlockSpec((tm, GD), lambda i, h: (i, h)), pl.BlockSpec((tm, GD), lambda i, h: (i, COL_Z + h)),
                  pl.BlockSpec((1, GD), lambda i, h: (0, 0)), ANY],
        out_specs=pl.BlockSpec((tm, GD), lambda i, h: (i, HQ * DH // GD + h)),
        out_shape=jax.ShapeDtypeStruct(mixed.shape, mixed.dtype), input_output_aliases={3: 0},
        compiler_params=_cp("parallel", "parallel"),
    )(o_raw, proj, nw, mixed)


def _gdn_post_bwd(o_raw, proj, nw, dmixed, *, name, tm=512):
    def body(o_ref, z_ref, w_ref, d_ref, do_ref, dz_ref, dw_ref):
        _, vjp = jax.vjp(_gdn_post, o_ref[...], z_ref[...].astype(F32), w_ref[...])
        do, dz, dw = vjp(d_ref[...])
        do_ref[...] = do
        dz_ref[...] = dz.astype(BF16)

        @pl.when((pl.program_id(0) == 0) & (pl.program_id(1) == 0))
        def _():
            dw_ref[...] = jnp.zeros_like(dw_ref)

        dw_ref[...] += dw

    blk = pl.BlockSpec((tm, GD), lambda i, h: (i, h))
    vec = pl.BlockSpec((1, GD), lambda i, h: (0, 0))
    return pl.pallas_call(
        body, name=name, grid=(T // tm, GH),
        in_specs=[blk, pl.BlockSpec((tm, GD), lambda i, h: (i, COL_Z + h)), vec,
                  pl.BlockSpec((tm, GD), lambda i, h: (i, GH + h))],
        out_specs=[blk, pl.BlockSpec((tm, GD), lambda i, h: (i, COL_Z + h)), vec],
        out_shape=[jax.ShapeDtypeStruct((T, GH * GD), F32), jax.ShapeDtypeStruct((T, N_PROJ), BF16),
                   jax.ShapeDtypeStruct((1, GD), F32)],
        compiler_params=_cp("arbitrary", "arbitrary"),
    )(o_raw, proj, nw, dmixed)


def _adamw(w, g, m, v, *, name):
    shape = w.shape
    cols = shape[-1]
    w2, g2, m2, v2 = (a.reshape(-1, cols) for a in (w, g, m, v))
    rows = w2.shape[0]
    tr = next((t for t in (512, 256, 128, 64, 32, 16, 8) if rows % t == 0), rows)

    def body(w_ref, g_ref, m_ref, v_ref, d_ref, nm_ref, nv_ref):
        gv = g_ref[...]
        nm = B1 * m_ref[...] + (1.0 - B1) * gv
        nv = B2 * v_ref[...] + (1.0 - B2) * jnp.square(gv)
        m_hat = nm / (1.0 - B1 ** STEP)
        v_hat = nv / (1.0 - B2 ** STEP)
        d_ref[...] = -LR * (m_hat / (jnp.sqrt(v_hat) + AEPS) + WD * w_ref[...])
        nm_ref[...] = nm
        nv_ref[...] = nv

    blk = pl.BlockSpec((tr, cols), lambda i: (i, 0))
    out = pl.pallas_call(
        body, name=name, grid=(rows // tr,), in_specs=[blk] * 4, out_specs=[blk] * 3,
        out_shape=[jax.ShapeDtypeStruct((rows, cols), F32)] * 3, compiler_params=_cp("parallel"),
    )(w2, g2, m2, v2)
    return tuple(o.reshape(shape) for o in out)


def _layer_weights(w_in):
    return jnp.pad(w_in, ((0, 0), (0, N_PROJ - w_in.shape[1])))


def _layer_params(attn_norm, sinks, gcw, a_log, dt_bias, gnw, ffn_norm, fcw, fcb):
    lanes4 = lambda v: jnp.pad(v, (GH, 128 - 2 * GH))[None]
    return dict(attn_norm=attn_norm[None], sinks=jnp.pad(sinks, (0, 128 - HQ))[None], gcw=gcw, alog=lanes4(a_log),
                dtb=lanes4(dt_bias), gnw=gnw[None], ffn_norm=ffn_norm[None], fcw=fcw, fcb=fcb[None])


def _mixer_fwd(x, W, P, l, carry=None):
    n = lambda s: f"l{l}_{s}"
    h, proj, ba = _rmsnorm_proj(x, P["attn_norm"], W["all"], name=n("norm1_proj"))
    mixed = _swa_fwd(proj, P["sinks"], name=n("swa"))
    qkv_c = _gdn_prep_fwd(proj, P["gcw"], name=n("gdn_prep"))
    (o_raw, s_all, t_all), carried = _gdn_chunk_fwd(qkv_c, ba, P["alog"], P["dtb"], name=n("gdn_chunk"), carry=carry)
    mixed = _gdn_post_fwd(o_raw, proj, P["gnw"], mixed, name=n("gdn_post"))
    x1 = _mm(mixed, W["out"], res=x, name=n("out_proj"), mode="nn", dims=(T, D, D), tm=512, tn=D, tk=D)
    saved = dict(x=x, h=h, proj=proj, ba=ba, qkv_c=qkv_c, o_raw=o_raw, s_all=s_all, t_all=t_all, mixed=mixed, x1=x1)
    return x1, saved, carried


def _ffn_fwd(x1, W, P, l, carry=None):
    n = lambda s: f"l{l}_{s}"
    h2 = _rmsnorm_fwd(x1, P["ffn_norm"], name=n("norm2"))
    gu = _mm(h2, W["ffn"], name=n("ffn_in"), mode="nn", dims=(T, 2 * DFF, D), tm=512, tn=FFN_CW, tk=D, out_dtype=BF16,
             b_spec=pl.BlockSpec((None, D, FFN_CW), lambda i, j, k: (j, k, 0)), n_outer=True)
    act, carried = _ffn_act_fwd(gu, P["fcw"], P["fcb"], name=n("ffn_act"), carry=carry)
    x2 = _mm(act, W["down"], res=x1, name=n("ffn_down"), mode="nn", dims=(T, D, DFF), tm=512, tn=D, tk=DFF)
    return x2, dict(h2=h2, gu=gu, act=act), carried


def _ffn_bwd(dx2, sv, W, P, l):
    n = lambda s: f"l{l}_{s}"
    CW = FFN_CW
    dact = _mm(dx2, W["down"], name=n("d_act"), mode="nt", dims=(T, DFF, D), tm=512, tn=DFF // 2, tk=D, out_dtype=BF16,
               n_outer=True)
    g_down = _mm(sv["act"], dx2, name=n("g_down"), mode="tn", dims=(DFF, D, T), tm=DFF // 2, tn=D, tk=min(T, 2048), out_dtype=BF16)
    dgu, g_fcw, g_fcb = _ffn_act_bwd(sv["gu"], P["fcw"], P["fcb"], dact, name=n("d_ffn_act"))
    dx1, g_ffn_norm = _mm_rmsnorm_bwd(
        dgu, W["ffn"], sv["x1"], P["ffn_norm"], dx2, name=n("d_h2_norm2"), K=2 * DFF, tm=512, tk=CW,
        a_spec=pl.BlockSpec((None, 512, CW), lambda i, k: (k // 2, i, k % 2)),
        b_spec=pl.BlockSpec((None, D, CW), lambda i, k: (k, 0, 0)))
    g_ffn = _mm(sv["h2"], dgu, name=n("g_ffn"), mode="tn", dims=(D, 2 * DFF, T), tm=512, tn=CW, tk=T, n_outer=True,
                b_spec=pl.BlockSpec((None, T, CW), lambda i, j, k: (j // 2, k, j % 2)),
                out_spec=pl.BlockSpec((None, 512, CW), lambda i, j, k: (j, i, 0)),
                out_shape=jax.ShapeDtypeStruct((N_CHIP, D, CW), BF16), out_dtype=BF16)
    return dx1, dict(ffn_norm=g_ffn_norm[0], w_ffn_in=g_ffn, ffn_conv_w=g_fcw, ffn_conv_b=g_fcb[0], w_down=g_down)


def _mixer_bwd(dx1, sv, W, P, l, carry=None):
    n = lambda s: f"l{l}_{s}"
    dmixed = _mm(dx1, W["out"], name=n("d_mixed"), mode="nt", dims=(T, D, D), tm=512, tn=D, tk=D)
    g_out = _mm(sv["mixed"], dx1, name=n("g_out"), mode="tn", dims=(D, D, T), tm=D, tn=D, tk=min(T, 2048), out_dtype=BF16)
    do_raw, dproj, g_gnw = _gdn_post_bwd(sv["o_raw"], sv["proj"], P["gnw"], dmixed, name=n("d_gdn_post"))
    (dqkv_c, dproj, g_alog, g_dtb), carried = _gdn_chunk_bwd(
        sv["qkv_c"], sv["ba"], P["alog"], P["dtb"], sv["s_all"], sv["t_all"], do_raw, dproj, name=n("d_gdn_chunk"), carry=carry)
    dproj, g_gcw = _gdn_prep_bwd(sv["proj"], P["gcw"], dqkv_c, dproj, name=n("d_gdn_prep"))
    dproj, g_sinks = _swa_bwd(sv["proj"], P["sinks"], dmixed, dproj, name=n("d_swa"))
    dx, g_attn_norm = _mm_rmsnorm_bwd(dproj, W["all"], sv["x"], P["attn_norm"], dx1, name=n("d_h_norm1"), K=N_PROJ,
                                      tm=512, tk=N_PROJ)
    g_all = _mm(sv["h"], dproj, name=n("g_in"), mode="tn", dims=(D, N_PROJ, T), tm=512, tn=N_PROJ, tk=min(T, 2048), out_dtype=BF16)
    grads = dict(attn_norm=g_attn_norm[0], w_in=g_all, attn_sinks=g_sinks[0, :HQ], gdn_conv_w=g_gcw,
                 gdn_a_log=g_alog[0, GH:2 * GH], gdn_dt_bias=g_dtb[0, GH:2 * GH], gdn_norm=g_gnw[0], w_out=g_out)
    return dx, grads, carried


def _pos():
    return lax.axis_index("x"), lax.axis_index("y"), lax.axis_index("c")


def _other_chips(x, y):
    return [(1 - x, y), (x, 1 - y), (1 - x, 1 - y)]


def _remote(src, dst, send_sems, recv_sems, k, to):
    return pltpu.make_async_remote_copy(src_ref=src, dst_ref=dst, send_sem=send_sems.at[k], recv_sem=recv_sems.at[k],
                                        device_id=to, device_id_type=MESH)


def _run_carry(carry, *, name):
    ci, co = len(carry.arrays), len(carry.out_shapes)

    def body(*refs):
        sends, recvs = carry.plan(refs[:ci], refs[ci:ci + co], refs[-2], refs[-1])
        for cp in sends:
            cp.start()
        for cp in recvs:
            cp.wait_recv()
        for cp in sends:
            cp.wait_send()

    return list(pl.pallas_call(
        body, name=name, in_specs=[ANY] * ci, out_specs=[ANY] * co, out_shape=carry.out_shapes,
        scratch_shapes=[pltpu.SemaphoreType.DMA((carry.nsem,)), pltpu.SemaphoreType.DMA((carry.nsem,))],
    )(*carry.arrays))


def _chip_index():
    return 2 * lax.axis_index("x") + lax.axis_index("y")


def _gather_carry(shards):
    def plan(srcs, outs, send_sems, recv_sems):
        x, y, c = _pos()
        chip = 2 * x + y
        others = [(k, px, py, 2 * px + py) for k, (px, py) in enumerate(_other_chips(x, y))]
        sends = [_remote(srcs[t], outs[t].at[chip], send_sems, recv_sems, 3 * t + k, (px, py, c))
                 for t in range(len(srcs)) for k, px, py, _ in others]
        recvs = [_remote(srcs[t], outs[t].at[j], send_sems, recv_sems, 3 * t + k, (x, y, c))
                 for t in range(len(srcs)) for k, _, _, j in others]
        return sends, recvs

    return _Carry(shards, [jax.ShapeDtypeStruct((N_CHIP,) + s.shape, s.dtype) for s in shards], 3 * len(shards), plan)


def _fill_own(outs, shards):
    return [lax.dynamic_update_index_in_dim(o, s, _chip_index(), 0) for o, s in zip(outs, shards)]


def _gather_halves(shards, after, *, name):
    nt = len(shards)

    def body(*refs):
        srcs, outs, send_sems, recv_sems = refs[:nt], refs[nt + 1:2 * nt + 1], refs[-2], refs[-1]
        x, y, c = _pos()
        me, sibling, chip = (x, y, c), (x, y, 1 - c), 2 * x + y
        others = [(k, px, py, 2 * px + py) for k, (px, py) in enumerate(_other_chips(x, y))]

        def rows(t, h):
            rh = srcs[t].shape[0] // 2
            return pl.ds(pl.multiple_of(h * rh, 16), rh)

        first = [_remote(srcs[t].at[rows(t, c), :], outs[t].at[chip, rows(t, c), :], send_sems, recv_sems, 3 * t + k, (px, py, c))
                 for t in range(nt) for k, px, py, _ in others]
        for cp in first:
            cp.start()
        passed = []
        for t in range(nt):
            for k, _, _, j in others:
                slot = outs[t].at[j, rows(t, c), :]
                _remote(slot, slot, send_sems, recv_sems, 3 * t + k, me).wait_recv()
                passed.append(_remote(slot, slot, send_sems, recv_sems, 3 * (nt + t) + k, sibling))
                passed[-1].start()
        for t in range(nt):
            for k, _, _, j in others:
                slot = outs[t].at[j, rows(t, 1 - c), :]
                _remote(slot, slot, send_sems, recv_sems, 3 * (nt + t) + k, me).wait_recv()
        for cp in first + passed:
            cp.wait_send()

    outs = pl.pallas_call(
        body, name=name, in_specs=[ANY] * (nt + 1), out_specs=[ANY] * nt,
        out_shape=[jax.ShapeDtypeStruct((N_CHIP,) + s.shape, s.dtype) for s in shards],
        scratch_shapes=[pltpu.SemaphoreType.DMA((6 * nt,)), pltpu.SemaphoreType.DMA((6 * nt,))],
    )(*shards, after)
    return _fill_own(outs, shards)


HBM_SPEC = pl.BlockSpec(memory_space=pltpu.HBM)
SEM_SPEC = pl.BlockSpec(memory_space=pltpu.SEMAPHORE)
DATAFLOW = pltpu.SideEffectType.DATAFLOW_SIDE_EFFECTING


def _split_start(carry, after, *, name):
    ci, co = len(carry.arrays), len(carry.out_shapes)

    def body(*refs):
        srcs, lands, send_sems, recv_sems, token = refs[:ci], refs[ci:ci + co], refs[ci + co + 1], refs[ci + co + 2], refs[-1]
        for cp in carry.plan(srcs, lands, send_sems, recv_sems)[0]:
            cp.start()
        token[...] = jnp.zeros_like(token)

    lands = [lax.empty(s.shape, s.dtype) for s in carry.out_shapes]
    hbm = lambda a: pltpu.with_memory_space_constraint(a, pltpu.HBM)
    out = pl.pallas_call(
        body, name=name, in_specs=[HBM_SPEC] * (ci + co) + [ANY],
        out_specs=[SEM_SPEC, SEM_SPEC] + [HBM_SPEC] * (ci + co) + [VMEM_SPEC],
        out_shape=[pltpu.SemaphoreType.DMA((carry.nsem,)), pltpu.SemaphoreType.DMA((carry.nsem,))]
        + [pltpu.HBM(a.shape, a.dtype) for a in carry.arrays + lands] + [jax.ShapeDtypeStruct((8, 128), F32)],
        input_output_aliases={t: 2 + t for t in range(ci + co)},
        compiler_params=pltpu.CompilerParams(has_side_effects=DATAFLOW),
    )(*[hbm(a) for a in carry.arrays + lands], after)
    return out[0], out[1], list(out[2:2 + ci]), list(out[2 + ci:2 + ci + co]), out[-1]


def _split_wait(carry, send_sems, recv_sems, srcs, lands, after, *, name):
    ci, co = len(srcs), len(lands)

    def body(*refs):
        sends, recvs = carry.plan(refs[:ci], refs[ci:ci + co], refs[ci + co], refs[ci + co + 1])
        for cp in sends:
            cp.wait_send()
        for cp in recvs:
            cp.wait_recv()

    out = pl.pallas_call(
        body, name=name, in_specs=[HBM_SPEC] * (ci + co) + [SEM_SPEC, SEM_SPEC, ANY], out_specs=[HBM_SPEC] * (ci + co),
        out_shape=[pltpu.HBM(a.shape, a.dtype) for a in list(srcs) + list(lands)],
        input_output_aliases={t: t for t in range(ci + co)},
        compiler_params=pltpu.CompilerParams(has_side_effects=DATAFLOW),
    )(*srcs, *lands, send_sems, recv_sems, after)
    return list(out[:ci]), list(out[ci:])


def _gather_start(shards, after, *, name):
    return _split_start(_gather_carry(shards), after, name=name)


def _gather_wait(send_sems, recv_sems, shards, lands, after, *, name):
    srcs, got = _split_wait(_gather_carry(shards), send_sems, recv_sems, shards, lands, after, name=name)
    return _fill_own(got, srcs)


def _ag_small(v, *, name):
    m, n = v.shape

    def body(x_ref, out_ref, red_ref, send_sems, recv_sems, local_sem):
        x, y, c = _pos()
        me, sibling = (x, y, c), (x, y, 1 - c)
        chips = _other_chips(x, y)
        rows = lambda px, py, pc: out_ref.at[pl.ds(pl.multiple_of((4 * px + 2 * py + pc) * m, 8), m), :]
        mine = pltpu.make_async_copy(x_ref, rows(*me), local_sem)
        mine.start()
        first = [_remote(x_ref, rows(*me), send_sems, recv_sems, 0, sibling)]
        first += [_remote(x_ref, rows(*me), send_sems, recv_sems, 1 + k, (*chip, c)) for k, chip in enumerate(chips)]
        for cp in first:
            cp.start()
        passed = [_remote(rows(*chip, c), rows(*chip, c), send_sems, recv_sems, 4 + k, sibling) for k, chip in enumerate(chips)]
        for k, chip in enumerate(chips):
            _remote(rows(*chip, c), rows(*chip, c), send_sems, recv_sems, 1 + k, me).wait_recv()
            passed[k].start()
        _remote(rows(*sibling), rows(*sibling), send_sems, recv_sems, 0, me).wait_recv()
        for k, chip in enumerate(chips):
            _remote(rows(*chip, 1 - c), rows(*chip, 1 - c), send_sems, recv_sems, 4 + k, me).wait_recv()
        for cp in first + passed:
            cp.wait_send()
        mine.wait()
        acc = out_ref[0:m, :]
        for d in range(1, N_DEV):
            acc = acc + out_ref[d * m:(d + 1) * m, :]
        red_ref[...] = acc

    return pl.pallas_call(
        body, name=name, in_specs=[VMEM_SPEC], out_specs=[VMEM_SPEC, VMEM_SPEC],
        out_shape=[jax.ShapeDtypeStruct((N_DEV * m, n), v.dtype), jax.ShapeDtypeStruct((m, n), v.dtype)],
        scratch_shapes=[pltpu.SemaphoreType.DMA((7,)), pltpu.SemaphoreType.DMA((7,)), pltpu.SemaphoreType.DMA],
    )(v)


def _halves(ref, c):
    rh = ref.shape[1] // 2
    return ref.at[:, pl.ds(pl.multiple_of(c * rh, 16), rh), :]


def _rs_swap(gs, *, name):
    nt = len(gs)

    def body(*refs):
        g, theirs = refs[:nt], refs[nt:2 * nt]
        send_sems, recv_sems = refs[2 * nt:]
        x, y, c = _pos()
        swaps = [_remote(_halves(g[t], 1 - c), theirs[t], send_sems, recv_sems, t, (x, y, 1 - c)) for t in range(nt)]
        for cp in swaps:
            cp.start()
        for cp in swaps:
            cp.wait()

    return pl.pallas_call(
        body, name=name, in_specs=[ANY] * nt, out_specs=[ANY] * nt,
        out_shape=[jax.ShapeDtypeStruct((a.shape[0], a.shape[1] // 2, a.shape[2]), a.dtype) for a in gs],
        scratch_shapes=[pltpu.SemaphoreType.DMA((nt,)), pltpu.SemaphoreType.DMA((nt,))],
    )(*gs)


def _exchange_carry(ss):
    def plan(s, out, send_sems, recv_sems):
        x, y, c = _pos()
        chip = 2 * x + y
        others = [(k, px, py, 2 * px + py) for k, (px, py) in enumerate(_other_chips(x, y))]
        sends = [_remote(s[t].at[j], out[t].at[chip], send_sems, recv_sems, 3 * t + k, (px, py, c))
                 for t in range(len(s)) for k, px, py, j in others]
        recvs = [_remote(s[t].at[j], out[t].at[j], send_sems, recv_sems, 3 * t + k, (x, y, c))
                 for t in range(len(s)) for k, _, _, j in others]
        return sends, recvs

    return _Carry(ss, [jax.ShapeDtypeStruct(a.shape, a.dtype) for a in ss], 3 * len(ss), plan)


def _fill_own_slab(outs, ss):
    chip = _chip_index()
    return [lax.dynamic_update_index_in_dim(o, lax.dynamic_index_in_dim(s, chip, 0, keepdims=False), chip, 0)
            for o, s in zip(outs, ss)]


def _rs_join(rs, *, name):
    nt = len(rs)

    def body(*refs):
        r, theirs = refs[:nt], refs[nt:2 * nt]
        send_sems, recv_sems = refs[2 * nt:]
        x, y, c = _pos()
        swaps = [_remote(r[t], theirs[t], send_sems, recv_sems, t, (x, y, 1 - c)) for t in range(nt)]
        for cp in swaps:
            cp.start()
        for cp in swaps:
            cp.wait()

    theirs = pl.pallas_call(
        body, name=name, in_specs=[ANY] * nt, out_specs=[ANY] * nt,
        out_shape=[jax.ShapeDtypeStruct(a.shape, a.dtype) for a in rs],
        scratch_shapes=[pltpu.SemaphoreType.DMA((nt,)), pltpu.SemaphoreType.DMA((nt,))],
    )(*rs)
    first = lax.axis_index("c") == 0
    return [jnp.concatenate([jnp.where(first, a, b), jnp.where(first, b, a)], axis=0) for a, b in zip(rs, theirs)]


def _row_tile(rows, dtype):
    unit = 16 if dtype == BF16 else 8
    if rows <= 512:
        return rows
    return next((t for t in (512, 256, 128, 64, 32, 16, 8) if rows % t == 0 and t % unit == 0), rows)


def _add_pair(g, b, half, *, name):
    n, rh, cols = b.shape
    tr = _row_tile(rh, BF16)
    nb = rh // tr

    def body(half_ref, g_ref, b_ref, o_ref):
        o_ref[...] = (g_ref[...].astype(F32) + b_ref[...].astype(F32)).astype(BF16)

    blk = pl.BlockSpec((None, tr, cols), lambda j, i, h: (j, i, 0))
    return pl.pallas_call(
        body, name=name,
        grid_spec=pltpu.PrefetchScalarGridSpec(
            num_scalar_prefetch=1, grid=(n, nb),
            in_specs=[pl.BlockSpec((None, tr, cols), lambda j, i, h: (j, h[0] * nb + i, 0)), blk], out_specs=blk),
        out_shape=jax.ShapeDtypeStruct(b.shape, BF16), compiler_params=_cp("parallel", "parallel"),
    )(half, g, b)


def _sum_chips(b, *, name):
    n, rh, cols = b.shape
    tr = _row_tile(rh, BF16)

    def body(b_ref, o_ref):
        acc = b_ref[0].astype(F32)
        for j in range(1, n):
            acc = acc + b_ref[j].astype(F32)
        o_ref[...] = acc

    return pl.pallas_call(
        body, name=name, grid=(rh // tr,), in_specs=[pl.BlockSpec((n, tr, cols), lambda i: (0, i, 0))],
        out_specs=pl.BlockSpec((tr, cols), lambda i: (i, 0)),
        out_shape=jax.ShapeDtypeStruct((rh, cols), F32), compiler_params=_cp("parallel"),
    )(b)


BIG = ("w_in", "w_out", "w_ffn_in", "w_down")
W_IN_SHARD = (N_MAIN + 2 * GH) // N_CHIP


def _chip_major(name, g):
    if name == "w_in":
        return jnp.stack([g[:, j * W_IN_SHARD:(j + 1) * W_IN_SHARD] for j in range(N_CHIP)])
    if name == "w_ffn_in":
        return g
    return g.reshape(N_CHIP, g.shape[0] // N_CHIP, g.shape[1])


def _rs_pairs(gs, tag):
    half = lax.axis_index("c").astype(jnp.int32).reshape(1)
    theirs = _rs_swap(gs, name=f"{tag}_swap")
    return [_add_pair(a, b, half, name=f"{tag}_add{i}") for i, (a, b) in enumerate(zip(gs, theirs))]


def _rs_sums(got, pairs, tag):
    return [_sum_chips(b, name=f"{tag}_sum{i}") for i, b in enumerate(_fill_own_slab(got, pairs))]


def _size(shape):
    n = 1
    for s in shape:
        n *= s
    return n


def _pack_flat(parts, total, dtype):
    flat = jnp.concatenate([p.reshape(-1).astype(dtype) for p in parts])
    return jnp.pad(flat, (0, total - flat.shape[0]))


def _unpack_flat(flat, shapes):
    out, o = [], 0
    for s in shapes:
        out.append(flat[o:o + _size(s)].reshape(s))
        o += _size(s)
    return out


WEIGHTS = ("attn_norm", "w_in", "attn_sinks", "gdn_conv_w", "gdn_a_log", "gdn_dt_bias", "gdn_norm", "w_out", "ffn_norm",
           "w_ffn_in", "ffn_conv_w", "ffn_conv_b", "w_down", "final_norm")
SMALL = {"attn_norm": (DEPTH, D), "attn_sinks": (DEPTH, HQ), "gdn_a_log": (DEPTH, GH), "gdn_dt_bias": (DEPTH, GH),
         "gdn_norm": (DEPTH, GD), "ffn_norm": (DEPTH, D), "ffn_conv_b": (DEPTH, DFF), "final_norm": (D,)}
CONV_FULL = {"gdn_conv_w": (DEPTH, GK, 3 * GH * GD), "ffn_conv_w": (DEPTH, FK, DFF)}
CONV_SHARD = {"gdn_conv_w": (DEPTH, GK, 3 * GH * GD // N_CHIP), "ffn_conv_w": (DEPTH, FK, DFF // N_CHIP)}
CONV_ROWS, SMALLG_ROWS, SMALLW_ROWS = 64, 320, 144


def kernel(x, attn_norm, w_in, attn_sinks, gdn_conv_w, gdn_a_log, gdn_dt_bias, gdn_norm, w_out, ffn_norm, w_ffn_in, ffn_conv_w, ffn_conv_b, w_down, final_norm, loss_target, m_attn_norm, m_w_in, m_attn_sinks, m_gdn_conv_w, m_gdn_a_log, m_gdn_dt_bias, m_gdn_norm, m_w_out, m_ffn_norm, m_w_ffn_in, m_ffn_conv_w, m_ffn_conv_b, m_w_down, m_final_norm, v_attn_norm, v_w_in, v_attn_sinks, v_gdn_conv_w, v_gdn_a_log, v_gdn_dt_bias, v_gdn_norm, v_w_out, v_ffn_norm, v_w_ffn_in, v_ffn_conv_w, v_ffn_conv_b, v_w_down, v_final_norm):
    w = dict(zip(WEIGHTS, (attn_norm, w_in, attn_sinks, gdn_conv_w, gdn_a_log, gdn_dt_bias, gdn_norm, w_out, ffn_norm,
                           w_ffn_in, ffn_conv_w, ffn_conv_b, w_down, final_norm)))
    m = dict(zip(WEIGHTS, (m_attn_norm, m_w_in, m_attn_sinks, m_gdn_conv_w, m_gdn_a_log, m_gdn_dt_bias, m_gdn_norm, m_w_out,
                           m_ffn_norm, m_w_ffn_in, m_ffn_conv_w, m_ffn_conv_b, m_w_down, m_final_norm)))
    v = dict(zip(WEIGHTS, (v_attn_norm, v_w_in, v_attn_sinks, v_gdn_conv_w, v_gdn_a_log, v_gdn_dt_bias, v_gdn_norm, v_w_out,
                           v_ffn_norm, v_w_ffn_in, v_ffn_conv_w, v_ffn_conv_b, v_w_down, v_final_norm)))
    cx, cy, _ = _pos()
    chip = 2 * cx + cy

    cpack = _pack_flat([w[n] for n in CONV_SHARD], CONV_ROWS * 128, F32).reshape(CONV_ROWS, 128)
    cgath, _ = _ag_small(cpack, name="gather_conv_w")
    cgath = cgath.reshape(N_DEV, CONV_ROWS * 128)
    cshards = [_unpack_flat(cgath[2 * j], list(CONV_SHARD.values())) for j in range(N_CHIP)]
    conv = {n: jnp.concatenate([cshards[j][i] for j in range(N_CHIP)], axis=2) for i, n in enumerate(CONV_SHARD)}

    Ps = [_layer_params(attn_norm[l], attn_sinks[l], conv["gdn_conv_w"][l], gdn_a_log[l], gdn_dt_bias[l], gdn_norm[l],
                        ffn_norm[l], conv["ffn_conv_w"][l], ffn_conv_b[l]) for l in range(DEPTH)]

    wb = {n: [w[n][l].astype(BF16) for l in range(DEPTH)] for n in BIG}
    mixer_w = lambda g_in, g_out: dict(all=_layer_weights(jnp.concatenate(list(g_in), axis=1)), out=g_out.reshape(D, D))
    ffn_w = lambda g_ffn, g_down: dict(ffn=g_ffn, down=g_down.reshape(DFF, D))
    groups = [[wb[a][l], wb[b][l]] for l in range(DEPTH) for a, b in (("w_in", "w_out"), ("w_ffn_in", "w_down"))]
    def after_start(P, key, started):
        return {**P, key: P[key] + started[4][:1, :1]}

    got0 = _gather_halves(groups[0], cgath, name="gather0")
    g1 = _gather_start(groups[1], got0[1], name="gather1_start")
    Wm0 = mixer_w(*got0)
    h, sv0, _ = _mixer_fwd(x[0], Wm0, after_start(Ps[0], "attn_norm", g1), 0)
    got1 = _gather_wait(*g1[:4], h, name="gather1_wait")
    g2 = _gather_start(groups[2], got1[1], name="gather2_start")
    Wf0 = ffn_w(*got1)
    h, sv0f, _ = _ffn_fwd(h, Wf0, after_start(Ps[0], "ffn_norm", g2), 0)
    got2 = _gather_wait(*g2[:4], h, name="gather2_wait")
    g3 = _gather_start(groups[3], got2[1], name="gather3_start")
    Wm1 = mixer_w(*got2)
    h, sv1, _ = _mixer_fwd(h, Wm1, after_start(Ps[1], "attn_norm", g3), 1)
    Wf1 = ffn_w(*_gather_wait(*g3[:4], h, name="gather3_wait"))
    h, sv1f, _ = _ffn_fwd(h, Wf1, Ps[1], 1)
    dx, g_final, loss_part = _loss_head(h, final_norm[None], loss_target[0], name="loss_head")

    def exchange_start(pairs, tag):
        carry = _exchange_carry(pairs)
        return carry, _split_start(carry, pairs[0], name=f"{tag}_exchange_start")

    def exchange_wait(carry, started, after, tag):
        srcs, got = _split_wait(carry, *started[:4], after, name=f"{tag}_exchange_wait")
        return _rs_sums(got, srcs, tag)

    tied = lambda P, key, started: {**P, key: P[key] + started[4][:1, :1]}
    dx, gf1 = _ffn_bwd(dx, {**sv1, **sv1f}, Wf1, Ps[1], 1)
    dx, gm1, _ = _mixer_bwd(dx, sv1, Wm1, Ps[1], 1)
    lg1 = {**gf1, **gm1}
    pairs1 = _rs_pairs([_chip_major(n, lg1[n]) for n in BIG], "rs1")
    ex1 = exchange_start(pairs1, "rs1")
    dx, gf0 = _ffn_bwd(dx, {**sv0, **sv0f}, Wf0, tied(Ps[0], "fcb", ex1[1]), 0)
    pairs0a = _rs_pairs([_chip_major(n, gf0[n]) for n in BIG[2:]], "rs0a")
    ex0a = exchange_start(pairs0a, "rs0a")
    dx, gm0, _ = _mixer_bwd(dx, sv0, Wm0, tied(Ps[0], "gnw", ex0a[1]), 0)
    lg = [{**gf0, **gm0}, lg1]
    pairs0b = _rs_pairs([_chip_major(n, gm0[n]) for n in BIG[:2]], "rs0b")
    sums0b = _rs_sums(_run_carry(_exchange_carry(pairs0b), name="rs0b_exchange"), pairs0b, "rs0b")
    sums1 = exchange_wait(*ex1, sums0b[0], "rs1")
    sums0a = exchange_wait(*ex0a, sums1[0], "rs0a")
    joined = _rs_join(sums0b + sums0a + sums1, name="rs_join")
    reduced = [joined[:4], joined[4:]]
    grad_x = dx[None]
    stacked = lambda n: jnp.stack([lg[l][n] for l in range(DEPTH)])

    small_parts = [g_final[0] if n == "final_norm" else stacked(n) for n in SMALL] + [stacked(n) for n in CONV_FULL]
    svec = _pack_flat(small_parts + [loss_part[0, :1]], SMALLG_ROWS * 128, F32).reshape(SMALLG_ROWS, 128)
    _, sred = _ag_small(svec, name="reduce_small")
    small_g = _unpack_flat(sred.reshape(-1), list(SMALL.values()) + list(CONV_FULL.values()) + [(1,)])
    g = dict(zip(list(SMALL) + list(CONV_FULL), small_g[:-1]))
    loss = small_g[-1][0]
    for n in CONV_FULL:
        wd = CONV_SHARD[n][2]
        g[n] = lax.dynamic_slice_in_dim(g[n], chip * wd, wd, axis=2)

    g.update({n: jnp.stack([reduced[l][i] for l in range(DEPTH)]) for i, n in enumerate(BIG)})

    delta, new_m, new_v = {}, {}, {}
    for n in BIG:
        delta[n], new_m[n], new_v[n] = _adamw(w[n], g[n], m[n], v[n], name=f"adamw_{n}")
    small_names = list(SMALL) + list(CONV_SHARD)
    small_shapes = list(SMALL.values()) + list(CONV_SHARD.values())
    packed = [_pack_flat([t[n] for n in small_names], SMALLW_ROWS * 128, F32).reshape(SMALLW_ROWS, 128) for t in (w, g, m, v)]
    for res, out in zip(_adamw(*packed, name="adamw_small"), (delta, new_m, new_v)):
        out.update(zip(small_names, _unpack_flat(res.reshape(-1), small_shapes)))

    return (loss, grad_x, *[g[n] for n in WEIGHTS], *[delta[n] for n in WEIGHTS], *[new_m[n] for n in WEIGHTS],
            *[new_v[n] for n in WEIGHTS])
```

```python
import functools

import jax
import jax.numpy as jnp
import numpy as np
from jax import lax
from jax.experimental import pallas as pl
from jax.experimental.pallas import tpu as pltpu

F32, BF16 = jnp.float32, jnp.bfloat16
HIGHEST = lax.Precision.HIGHEST
MESH = pl.DeviceIdType.MESH

D = 1024
T = 4096
DEPTH = 2
HQ, HKV, DH, WIN = 8, 2, 64, 128
GH, GD, GC, GK = 4, 128, 64, 4
DFF, FK = 2816, 3
EPS = 1e-6
N_MAIN = 2816
N_PROJ = N_MAIN + 128
COL_K, COL_V, COL_G, COL_Z, COL_BA = 4, 5, 6, 18, 22
N_CHUNK = T // GC
N_DEV, N_CHIP = 8, 4
FFN_CW = 2 * DFF // N_CHIP
LR, B1, B2, AEPS, WD, STEP = 0.001, 0.9, 0.999, 1e-08, 0.01, 10

VMEM_LIMIT = 56 * 1024 * 1024
ANY = pl.BlockSpec(memory_space=pl.ANY)
VMEM_SPEC = pl.BlockSpec(memory_space=pltpu.VMEM)


def _cp(*sem):
    return pltpu.CompilerParams(dimension_semantics=sem if sem else None, vmem_limit_bytes=VMEM_LIMIT)


class _Carry:
    def __init__(self, arrays, out_shapes, nsem, plan):
        self.arrays, self.out_shapes, self.nsem, self.plan = list(arrays), list(out_shapes), nsem, plan


def _seq_call(body, carry, *, name, steps, in_specs, out_specs, out_shape, args, scratch_shapes=(), aliases=None):
    in_specs, out_specs, out_shape, scratch = list(in_specs), list(out_specs), list(out_shape), list(scratch_shapes)
    n_in, n_out, n_scr = len(in_specs), len(out_specs), len(scratch)
    if carry is None:
        fn = body
    else:
        ci, co = len(carry.arrays), len(carry.out_shapes)

        def fn(*refs):
            ins, cins = refs[:n_in], refs[n_in:n_in + ci]
            outs, couts = refs[n_in + ci:n_in + ci + n_out], refs[n_in + ci + n_out:n_in + ci + n_out + co]
            scr, (ssem, rsem) = refs[n_in + ci + n_out + co:-2], refs[-2:]
            sends, recvs = carry.plan(cins, couts, ssem, rsem)

            @pl.when(pl.program_id(0) == 0)
            def _():
                for cp in sends:
                    cp.start()

            body(*ins, *outs, *scr)

            @pl.when(pl.program_id(0) == steps - 1)
            def _():
                for cp in recvs:
                    cp.wait_recv()
                for cp in sends:
                    cp.wait_send()

        in_specs += [ANY] * ci
        out_specs += [ANY] * co
        out_shape += carry.out_shapes
        scratch += [pltpu.SemaphoreType.DMA((carry.nsem,)), pltpu.SemaphoreType.DMA((carry.nsem,))]
        args = list(args) + carry.arrays
    outs = pl.pallas_call(
        fn, name=name, grid=(steps,), in_specs=in_specs, out_specs=out_specs, out_shape=out_shape,
        scratch_shapes=scratch, input_output_aliases=aliases or {}, compiler_params=_cp("arbitrary"),
    )(*args)
    return list(outs[:n_out]), list(outs[n_out:])


def _dot(a, b):
    return jnp.dot(a.astype(BF16), b.astype(BF16), preferred_element_type=F32)


def _dot_nt(a, b):
    return lax.dot_general(a.astype(BF16), b.astype(BF16), (((1,), (1,)), ((), ())), preferred_element_type=F32)


def _dot_tn(a, b):
    return lax.dot_general(a.astype(BF16), b.astype(BF16), (((0,), (0,)), ((), ())), preferred_element_type=F32)


def _iota(shape, dim):
    return lax.broadcasted_iota(jnp.int32, shape, dim)


def _col(x, idx):
    return jnp.sum(jnp.where(_iota(x.shape, 1) == idx, x, 0.0), axis=1, keepdims=True)


def _silu(y):
    return y * jax.nn.sigmoid(y)


_MM_DN = {"nn": (((1,), (0,)), ((), ())), "nt": (((1,), (1,)), ((), ())), "tn": (((0,), (0,)), ((), ()))}


def _mm(a, b, *, name, mode, dims, tm, tn, tk, res=None, out_dtype=F32, a_spec=None, b_spec=None, out_spec=None,
        out_shape=None, n_outer=False):
    M, N, K = dims
    assert M % tm == 0 and N % tn == 0 and K % tk == 0, (name, dims)
    nk = K // tk
    dn = _MM_DN[mode]

    def body(*refs):
        a_ref, b_ref = refs[:2]
        r_ref = None if res is None else refs[2]
        o_ref = refs[2 if res is None else 3]
        part = lax.dot_general(a_ref[...].astype(BF16), b_ref[...].astype(BF16), dn, preferred_element_type=F32)
        if nk == 1:
            o_ref[...] = (part if res is None else part + r_ref[...]).astype(out_dtype)
            return
        acc, k = refs[-1], pl.program_id(2)

        @pl.when(k == 0)
        def _():
            acc[...] = part

        @pl.when(k > 0)
        def _():
            acc[...] += part

        @pl.when(k == nk - 1)
        def _():
            r = acc[...]
            if res is not None:
                r = r + r_ref[...]
            o_ref[...] = r.astype(out_dtype)

    if a_spec is None:
        a_spec = pl.BlockSpec((tk, tm), lambda i, j, k: (k, i)) if mode == "tn" else pl.BlockSpec((tm, tk), lambda i, j, k: (i, k))
    if b_spec is None:
        b_spec = pl.BlockSpec((tn, tk), lambda i, j, k: (j, k)) if mode == "nt" else pl.BlockSpec((tk, tn), lambda i, j, k: (k, j))
    in_specs, args = [a_spec, b_spec], [a, b]
    if res is not None:
        in_specs.append(pl.BlockSpec((tm, tn), lambda i, j, k: (i, j)))
        args.append(res)
    out_spec = out_spec or pl.BlockSpec((tm, tn), lambda i, j, k: (i, j))
    grid = (M // tm, N // tn, nk)
    if n_outer:
        swap = lambda s: pl.BlockSpec(s.block_shape, lambda j, i, k, f=s.index_map: f(i, j, k))
        in_specs, out_spec, grid = [swap(s) for s in in_specs], swap(out_spec), (N // tn, M // tm, nk)
    return pl.pallas_call(
        body, name=name, grid=grid, in_specs=in_specs, out_specs=out_spec,
        out_shape=out_shape or jax.ShapeDtypeStruct((M, N), out_dtype),
        scratch_shapes=[pltpu.VMEM((tm, tn), F32)] if nk > 1 else [],
        compiler_params=_cp("parallel", "parallel", "arbitrary"),
    )(*args)


def _rms(x, w):
    return x * lax.rsqrt(jnp.mean(x * x, axis=-1, keepdims=True) + EPS) * w


def _rmsnorm_fwd(x, w, *, name, tm=512):
    def body(x_ref, w_ref, o_ref):
        o_ref[...] = _rms(x_ref[...], w_ref[...]).astype(BF16)

    return pl.pallas_call(
        body, name=name, grid=(x.shape[0] // tm,),
        in_specs=[pl.BlockSpec((tm, D), lambda i: (i, 0)), pl.BlockSpec((1, D), lambda i: (0, 0))],
        out_specs=pl.BlockSpec((tm, D), lambda i: (i, 0)),
        out_shape=jax.ShapeDtypeStruct(x.shape, BF16), compiler_params=_cp("parallel"),
    )(x, w)


def _rmsnorm_proj(x, w, w_all, *, name, tm=512):
    def body(x_ref, w_ref, b_ref, h_ref, p_ref, ba_ref):
        h = _rms(x_ref[...], w_ref[...]).astype(BF16)
        h_ref[...] = h
        r = jnp.dot(h, b_ref[...], preferred_element_type=F32)
        p_ref[...] = r[:, :N_MAIN].astype(BF16)
        ba_ref[...] = r[:, N_MAIN:]

    row = lambda n: pl.BlockSpec((tm, n), lambda i: (i, 0))
    return pl.pallas_call(
        body, name=name, grid=(x.shape[0] // tm,),
        in_specs=[row(D), pl.BlockSpec((1, D), lambda i: (0, 0)), pl.BlockSpec((D, N_PROJ), lambda i: (0, 0))],
        out_specs=[row(D), row(N_MAIN), row(N_PROJ - N_MAIN)],
        out_shape=[jax.ShapeDtypeStruct((x.shape[0], D), BF16), jax.ShapeDtypeStruct((x.shape[0], N_MAIN), BF16),
                   jax.ShapeDtypeStruct((x.shape[0], N_PROJ - N_MAIN), F32)],
        compiler_params=_cp("parallel"),
    )(x, w, w_all)


def _mm_rmsnorm_bwd(a, b, x, w, dres, *, name, K, tm, tk, a_spec=None, b_spec=None):
    M = x.shape[0]
    nk = K // tk
    assert M % tm == 0 and K % tk == 0, (name, M, K)

    def body(a_ref, b_ref, x_ref, w_ref, dr_ref, dx_ref, dw_ref, acc):
        i, k = pl.program_id(0), pl.program_id(1)
        part = lax.dot_general(a_ref[...].astype(BF16), b_ref[...].astype(BF16), _MM_DN["nt"], preferred_element_type=F32)

        @pl.when(k == 0)
        def _():
            acc[...] = part

        @pl.when(k > 0)
        def _():
            acc[...] += part

        @pl.when((i == 0) & (k == 0))
        def _():
            dw_ref[...] = jnp.zeros_like(dw_ref)

        @pl.when(k == nk - 1)
        def _():
            _, vjp = jax.vjp(_rms, x_ref[...], w_ref[...])
            dx, dw = vjp(acc[...])
            dx_ref[...] = dx + dr_ref[...]
            dw_ref[...] += dw

    row = pl.BlockSpec((tm, D), lambda i, k: (i, 0))
    vec = pl.BlockSpec((1, D), lambda i, k: (0, 0))
    return pl.pallas_call(
        body, name=name, grid=(M // tm, nk),
        in_specs=[a_spec or pl.BlockSpec((tm, tk), lambda i, k: (i, k)), b_spec or pl.BlockSpec((D, tk), lambda i, k: (0, k)),
                  row, vec, row],
        out_specs=[row, vec], out_shape=[jax.ShapeDtypeStruct((M, D), F32), jax.ShapeDtypeStruct((1, D), F32)],
        scratch_shapes=[pltpu.VMEM((tm, D), F32)], compiler_params=_cp("arbitrary", "arbitrary"),
    )(a, b, x, w, dres)


def _rmsnorm_bwd(x, w, dh, dres, *, name, tm=512):
    def body(x_ref, w_ref, dh_ref, dr_ref, dx_ref, dw_ref):
        _, vjp = jax.vjp(_rms, x_ref[...], w_ref[...])
        dx, dw = vjp(dh_ref[...])
        dx_ref[...] = dx + dr_ref[...]

        @pl.when(pl.program_id(0) == 0)
        def _():
            dw_ref[...] = jnp.zeros_like(dw_ref)

        dw_ref[...] += dw

    row = pl.BlockSpec((tm, D), lambda i: (i, 0))
    vec = pl.BlockSpec((1, D), lambda i: (0, 0))
    return pl.pallas_call(
        body, name=name, grid=(x.shape[0] // tm,), in_specs=[row, vec, row, row], out_specs=[row, vec],
        out_shape=[jax.ShapeDtypeStruct(x.shape, F32), jax.ShapeDtypeStruct((1, D), F32)],
        compiler_params=_cp("arbitrary"),
    )(x, w, dh, dres)


def _loss_head(x, w, tgt, *, name, tm=512):
    def f(xv, wv, tv):
        err = _rms(xv, wv) - tv
        per_row = jnp.sum(err * err, axis=1, keepdims=True) * (0.5 / D)
        return jnp.sum(per_row, axis=0, keepdims=True)

    def body(x_ref, w_ref, t_ref, dx_ref, dw_ref, loss_ref):
        tv = t_ref[...]
        loss, vjp = jax.vjp(lambda xv, wv: f(xv, wv, tv), x_ref[...], w_ref[...])
        dx, dw = vjp(jnp.ones((1, 1), F32))
        dx_ref[...] = dx

        @pl.when(pl.program_id(0) == 0)
        def _():
            dw_ref[...] = jnp.zeros_like(dw_ref)
            loss_ref[...] = jnp.zeros_like(loss_ref)

        dw_ref[...] += dw
        loss_ref[...] += jnp.broadcast_to(loss, loss_ref.shape)

    row = pl.BlockSpec((tm, D), lambda i: (i, 0))
    vec = pl.BlockSpec((1, D), lambda i: (0, 0))
    return pl.pallas_call(
        body, name=name, grid=(x.shape[0] // tm,), in_specs=[row, vec, row],
        out_specs=[row, vec, pl.BlockSpec((1, 128), lambda i: (0, 0))],
        out_shape=[jax.ShapeDtypeStruct(x.shape, F32), jax.ShapeDtypeStruct((1, D), F32),
                   jax.ShapeDtypeStruct((1, 128), F32)],
        compiler_params=_cp("arbitrary"),
    )(x, w, tgt)


def _swa_bias():
    G = HQ // HKV
    r, c = np.arange(G * WIN)[:, None], np.arange(2 * WIN)[None, :]
    rel = (r % WIN) + WIN - c
    band = (rel >= 0) & (rel < WIN)
    out = np.empty((2, HKV, G * WIN, 2 * WIN), np.float32)
    for h2 in range(HKV):
        slope = 2.0 ** (-8.0 * (h2 * G + r // WIN + 1) / HQ)
        out[0, h2] = np.where(band & (c >= WIN), -slope * rel, -1e30)
        out[1, h2] = np.where(band, -slope * rel, -1e30)
    return jnp.asarray(out)


def _swa_heads(qs, k2s, v2s, sk, biases, hqs=range(HQ)):
    first_kv = hqs[0] // (HQ // HKV)
    kv = lambda xs: [xs[hq // (HQ // HKV) - first_kv] for hq in hqs]
    ss = _hmap(lambda q, k2, b: _dot_nt(q, k2) * (DH ** -0.5) + b, qs, kv(k2s), biases)
    sinks = [_col(sk, hq) for hq in hqs]
    ms = _hmap(lambda s, sink: lax.stop_gradient(jnp.maximum(jnp.max(s, axis=1, keepdims=True), sink)), ss, sinks)
    ps = _hmap(lambda s, m: jnp.exp(s - m), ss, ms)
    dens = _hmap(lambda p, sink, m: jnp.sum(p, axis=1, keepdims=True) + jnp.exp(sink - m), ps, sinks, ms)
    return _hmap(lambda p, den, v2: _dot(p * (1.0 / den), v2), ps, dens, kv(v2s))


def _swa_load(q_ref, kp_ref, kc_ref, vp_ref, vc_ref, bias_ref):
    q, G = q_ref[...].astype(F32), HQ // HKV
    qs = [q[:, hq * DH:(hq + 1) * DH] for hq in range(HQ)]
    both = lambda a, b, h2: jnp.concatenate([a[:, h2 * DH:(h2 + 1) * DH], b[:, h2 * DH:(h2 + 1) * DH]], axis=0).astype(F32)
    k2s = [both(kp_ref, kc_ref, h2) for h2 in range(HKV)]
    v2s = [both(vp_ref, vc_ref, h2) for h2 in range(HKV)]
    biases = [bias_ref[hq // G, (hq % G) * WIN:(hq % G + 1) * WIN, :] for hq in range(HQ)]
    return qs, k2s, v2s, biases


def _swa_specs():
    prev = lambda i: jnp.maximum(jnp.minimum(i, T // WIN - 1) - 1, 0)
    cur = lambda i: jnp.minimum(i, T // WIN - 1)
    return [
        pl.BlockSpec((WIN, HQ * DH), lambda i: (cur(i), 0)),
        pl.BlockSpec((WIN, 128), lambda i: (prev(i), COL_K)),
        pl.BlockSpec((WIN, 128), lambda i: (cur(i), COL_K)),
        pl.BlockSpec((WIN, 128), lambda i: (prev(i), COL_V)),
        pl.BlockSpec((WIN, 128), lambda i: (cur(i), COL_V)),
        pl.BlockSpec((1, 128), lambda i: (0, 0)),
        pl.BlockSpec((None, HKV, (HQ // HKV) * WIN, 2 * WIN), lambda i: (jnp.minimum(i, 1), 0, 0, 0)),
    ]


def _swa_fwd(proj, sinks, *, name):
    def body(q_ref, kp_ref, kc_ref, vp_ref, vc_ref, sk_ref, bias_ref, o_ref):
        qs, k2s, v2s, biases = _swa_load(q_ref, kp_ref, kc_ref, vp_ref, vc_ref, bias_ref)
        o_ref[...] = jnp.concatenate(_swa_heads(qs, k2s, v2s, sk_ref[...], biases), axis=1).astype(BF16)

    return pl.pallas_call(
        body, name=name, grid=(T // WIN,), in_specs=_swa_specs(),
        out_specs=pl.BlockSpec((WIN, HQ * DH), lambda i: (i, 0)),
        out_shape=jax.ShapeDtypeStruct((T, HQ * DH + GH * GD), BF16), compiler_params=_cp("parallel"),
    )(proj, proj, proj, proj, proj, sinks, _swa_bias())


def _swa_bwd(proj, sinks, do, dproj, *, name):
    NB = T // WIN
    QW = HQ * DH

    def body(q_ref, kp_ref, kc_ref, vp_ref, vc_ref, sk_ref, bias_ref, do_ref, _, out_ref, dsk_ref, cq, ck, cv):
        i = pl.program_id(0)

        @pl.when(i == 0)
        def _():
            cq[...] = jnp.zeros_like(cq)
            ck[...] = jnp.zeros_like(ck)
            cv[...] = jnp.zeros_like(cv)
            dsk_ref[...] = jnp.zeros_like(dsk_ref)

        @pl.when(i < NB)
        def _():
            qs, k2s, v2s, biases = _swa_load(q_ref, kp_ref, kc_ref, vp_ref, vc_ref, bias_ref)
            dov = do_ref[...].astype(F32)
            dqs, dk2s, dv2s, dsk = [], [], [], jnp.zeros_like(sk_ref[...])
            for h2 in range(HKV):
                hqs = range(h2 * (HQ // HKV), (h2 + 1) * (HQ // HKV))
                _, vjp = jax.vjp(functools.partial(_swa_heads, biases=[biases[hq] for hq in hqs], hqs=hqs),
                                 [qs[hq] for hq in hqs], [k2s[h2]], [v2s[h2]], sk_ref[...])
                dq, dk2, dv2, dsk_h = vjp([dov[:, hq * DH:(hq + 1) * DH] for hq in hqs])
                dqs, dk2s, dv2s, dsk = dqs + dq, dk2s + dk2, dv2s + dv2, dsk + dsk_h
            out_ref[:, :QW] = cq[...].astype(BF16)
            out_ref[:, QW:QW + 128] = (ck[...] + jnp.concatenate([d[:WIN] for d in dk2s], axis=1)).astype(BF16)
            out_ref[:, QW + 128:] = (cv[...] + jnp.concatenate([d[:WIN] for d in dv2s], axis=1)).astype(BF16)
            cq[...] = jnp.concatenate(dqs, axis=1)
            ck[...] = jnp.concatenate([d[WIN:] for d in dk2s], axis=1)
            cv[...] = jnp.concatenate([d[WIN:] for d in dv2s], axis=1)
            dsk_ref[...] += dsk

        @pl.when(i == NB)
        def _():
            out_ref[:, :QW] = cq[...].astype(BF16)
            out_ref[:, QW:QW + 128] = ck[...].astype(BF16)
            out_ref[:, QW + 128:] = cv[...].astype(BF16)

    qblk = pl.BlockSpec((WIN, QW), lambda i: (jnp.minimum(i, NB - 1), 0))
    return pl.pallas_call(
        body, name=name, grid=(NB + 1,), in_specs=_swa_specs() + [qblk, ANY],
        out_specs=[pl.BlockSpec((WIN, QW + 256), lambda i: (jnp.maximum(i - 1, 0), 0)), pl.BlockSpec((1, 128), lambda i: (0, 0))],
        out_shape=[jax.ShapeDtypeStruct(dproj.shape, dproj.dtype), jax.ShapeDtypeStruct((1, 128), F32)],
        scratch_shapes=[pltpu.VMEM((WIN, QW), F32), pltpu.VMEM((WIN, 128), F32), pltpu.VMEM((WIN, 128), F32)],
        input_output_aliases={8: 0}, compiler_params=_cp("arbitrary"),
    )(proj, proj, proj, proj, proj, sinks, _swa_bias(), do, dproj)


RC = 256
HALO = 8


def _load_ext(ref, c):
    nch = T // RC
    r0 = pl.multiple_of(c * RC, RC)
    p0 = pl.multiple_of(jnp.maximum(r0 - HALO, 0), HALO)
    n0 = pl.multiple_of(jnp.minimum(r0 + RC, T - HALO), HALO)
    prev = jnp.where(c > 0, ref[pl.ds(p0, HALO), :].astype(F32), 0.0)
    nxt = jnp.where(c < nch - 1, ref[pl.ds(n0, HALO), :].astype(F32), 0.0)
    return jnp.concatenate([prev, ref[pl.ds(r0, RC), :].astype(F32), nxt], axis=0)


def _conv_ext(xe, w, K):
    y = w[K - 1:K, :] * xe
    for s in range(1, K):
        y = y + w[K - 1 - s:K - s, :] * pltpu.roll(xe, s, 0)
    return y


def _conv_bwd_ext(xe, dye, w, K, dw_ref):
    n = xe.shape[0]
    own = slice(HALO, HALO + RC)
    dx = w[K - 1:K, :] * dye
    dw_ref[K - 1:K, :] += jnp.sum(dye[own] * xe[own], axis=0, keepdims=True)
    for s in range(1, K):
        dx = dx + w[K - 1 - s:K - s, :] * pltpu.roll(dye, n - s, 0)
        dw_ref[K - 1 - s:K - s, :] += jnp.sum(dye[own] * pltpu.roll(xe, s, 0)[own], axis=0, keepdims=True)
    return dx[own]


def _gdn_post_conv(y, is_qk):
    a = _silu(y)
    nrm = a * lax.rsqrt(jnp.sum(a * a, axis=1, keepdims=True) + EPS)
    return jnp.where(is_qk, nrm, a)


def _gdn_prep_fwd(proj, conv_w, *, name):
    nblk = 3 * GH

    def body(x_ref, w_ref, o_ref):
        is_qk = pl.program_id(0) < 2 * GH
        w = w_ref[...]

        def chunk(c, carry):
            y = _conv_ext(_load_ext(x_ref, c), w, GK)[HALO:HALO + RC]
            o_ref[pl.ds(pl.multiple_of(c * RC, RC), RC), :] = _gdn_post_conv(y, is_qk)
            return carry

        lax.fori_loop(0, T // RC, chunk, 0)

    return pl.pallas_call(
        body, name=name, grid=(nblk,),
        in_specs=[pl.BlockSpec((T, 128), lambda j: (0, COL_G + j)), pl.BlockSpec((GK, 128), lambda j: (0, j))],
        out_specs=pl.BlockSpec((T, 128), lambda j: (0, j)),
        out_shape=jax.ShapeDtypeStruct((T, nblk * 128), F32), compiler_params=_cp("parallel"),
    )(proj, conv_w)


def _gdn_prep_bwd(proj, conv_w, dout, dproj, *, name):
    nblk = 3 * GH

    def body(x_ref, w_ref, d_ref, _, dx_ref, dw_ref):
        is_qk = pl.program_id(0) < 2 * GH
        w = w_ref[...]
        dw_ref[...] = jnp.zeros_like(dw_ref)

        def chunk(c, carry):
            xe = _load_ext(x_ref, c)
            _, vjp = jax.vjp(lambda y: _gdn_post_conv(y, is_qk), _conv_ext(xe, w, GK))
            (dye,) = vjp(_load_ext(d_ref, c))
            dx_ref[pl.ds(pl.multiple_of(c * RC, RC), RC), :] = _conv_bwd_ext(xe, dye, w, GK, dw_ref).astype(BF16)
            return carry

        lax.fori_loop(0, T // RC, chunk, 0)

    return pl.pallas_call(
        body, name=name, grid=(nblk,),
        in_specs=[pl.BlockSpec((T, 128), lambda j: (0, COL_G + j)), pl.BlockSpec((GK, 128), lambda j: (0, j)),
                  pl.BlockSpec((T, 128), lambda j: (0, j)), ANY],
        out_specs=[pl.BlockSpec((T, 128), lambda j: (0, COL_G + j)), pl.BlockSpec((GK, 128), lambda j: (0, j))],
        out_shape=[jax.ShapeDtypeStruct(dproj.shape, dproj.dtype), jax.ShapeDtypeStruct((GK, nblk * 128), F32)],
        input_output_aliases={3: 0}, compiler_params=_cp("parallel"),
    )(proj, conv_w, dout, dproj)


def _ffn_post_conv(y, b, up):
    return _silu(y + b) * up


def _ffn_act_fwd(gu, conv_w, conv_b, *, name, carry=None):
    nblk = DFF // 128

    def body(g_ref, u_ref, w_ref, b_ref, o_ref):
        w, b = w_ref[...], b_ref[...]

        def chunk(c, carry):
            rows = pl.ds(pl.multiple_of(c * RC, RC), RC)
            y = _conv_ext(_load_ext(g_ref, c), w, FK)[HALO:HALO + RC]
            o_ref[rows, :] = _ffn_post_conv(y, b, u_ref[rows, :].astype(F32)).astype(BF16)
            return carry

        lax.fori_loop(0, T // RC, chunk, 0)

    (act,), carried = _seq_call(
        body, carry, name=name, steps=nblk,
        in_specs=[pl.BlockSpec((T, 128), lambda j: (0, j)), pl.BlockSpec((T, 128), lambda j: (0, nblk + j)),
                  pl.BlockSpec((FK, 128), lambda j: (0, j)), pl.BlockSpec((1, 128), lambda j: (0, j))],
        out_specs=[pl.BlockSpec((T, 128), lambda j: (0, j))], out_shape=[jax.ShapeDtypeStruct((T, DFF), BF16)],
        args=(gu, gu, conv_w, conv_b))
    return act, carried


def _ffn_act_bwd(gu, conv_w, conv_b, dact, *, name):
    nblk = DFF // 128

    def body(g_ref, u_ref, w_ref, b_ref, d_ref, dgu_ref, dw_ref, db_ref):
        dg_ref, du_ref = dgu_ref.at[0], dgu_ref.at[1]
        w, b = w_ref[...], b_ref[...]
        dw_ref[...] = jnp.zeros_like(dw_ref)
        db_ref[...] = jnp.zeros_like(db_ref)

        def chunk(c, carry):
            rows = pl.ds(pl.multiple_of(c * RC, RC), RC)
            xe = _load_ext(g_ref, c)
            ue = _load_ext(u_ref, c)
            _, vjp = jax.vjp(_ffn_post_conv, _conv_ext(xe, w, FK), b, ue)
            dye, db, due = vjp(_load_ext(d_ref, c))
            du_ref[rows, :] = due[HALO:HALO + RC].astype(BF16)
            db_ref[...] += jnp.sum(dye[HALO:HALO + RC], axis=0, keepdims=True)
            dg_ref[rows, :] = _conv_bwd_ext(xe, dye, w, FK, dw_ref).astype(BF16)
            return carry

        lax.fori_loop(0, T // RC, chunk, 0)

    col = pl.BlockSpec((T, 128), lambda j: (0, j))
    return pl.pallas_call(
        body, name=name, grid=(nblk,),
        in_specs=[col, pl.BlockSpec((T, 128), lambda j: (0, nblk + j)), pl.BlockSpec((FK, 128), lambda j: (0, j)),
                  pl.BlockSpec((1, 128), lambda j: (0, j)), col],
        out_specs=[pl.BlockSpec((2, T, 128), lambda j: (0, 0, j)), pl.BlockSpec((FK, 128), lambda j: (0, j)),
                   pl.BlockSpec((1, 128), lambda j: (0, j))],
        out_shape=[jax.ShapeDtypeStruct((2, T, DFF), BF16), jax.ShapeDtypeStruct((FK, DFF), F32),
                   jax.ShapeDtypeStruct((1, DFF), F32)],
        compiler_params=_cp("parallel"),
    )(gu, gu, conv_w, conv_b, dact)


def _gdn_gates(ba, alog, dtb):
    beta = jax.nn.sigmoid(ba)
    g = -jnp.exp(alog) * jax.nn.softplus(ba + dtb)
    tril = (_iota((GC, GC), 0) >= _iota((GC, GC), 1)).astype(F32)
    return beta, jnp.dot(tril, g, precision=HIGHEST, preferred_element_type=F32)


def _hmap(f, *lists):
    return [f(*xs) for xs in zip(*lists)]


def _gdn_cols(beta_all, gc_all):
    return [_col(beta_all, h) for h in range(GH)], [_col(gc_all, GH + h) for h in range(GH)]


def _gdn_decay(Gcs):
    r, c = _iota((GC, GC), 0), _iota((GC, GC), 1)
    eye, ones = (r == c).astype(F32), jnp.ones((GC, GC), F32)
    grows = _hmap(lambda G: jnp.dot(ones, eye * G, precision=HIGHEST, preferred_element_type=F32), Gcs)
    return _hmap(lambda G, grow: jnp.exp(jnp.where(r >= c, G - grow, -1e30)), Gcs, grows)


def _gdn_pre(beta_all, gc_all):
    betas, Gcs = _gdn_cols(beta_all, gc_all)
    return betas, Gcs, _gdn_decay(Gcs)


def _gdn_A(ks, betas, decays):
    strict = _iota((GC, GC), 0) > _iota((GC, GC), 1)
    kk = _hmap(lambda k, b: _dot_nt(k * b, k), ks, betas)
    return _hmap(lambda a, d: jnp.where(strict, a * d, 0.0), kk, decays)


def _tri_inv(As):
    eye = (_iota((GC, GC), 0) == _iota((GC, GC), 1)).astype(F32)
    Tms, Ps = [eye - A for A in As], As
    for _ in range(GC.bit_length() - 2):
        Ps = _hmap(lambda P: _dot(P, P), Ps)
        Tms = _hmap(lambda Tm, P: Tm + _dot(Tm, P), Tms, Ps)
    return Tms


def _gdn_chunk_pre(qs, ks, vs, betas, Gcs, decays, Tms):
    eGs = _hmap(jnp.exp, Gcs)
    us = _hmap(lambda Tm, v, b: _dot(Tm, v * b), Tms, vs, betas)
    ws = _hmap(lambda Tm, k, b, eG: _dot(Tm, k * b * eG), Tms, ks, betas, eGs)
    qss = [q * (GD ** -0.5) for q in qs]
    qks = _hmap(lambda q, k, d: _dot_nt(q, k) * d, qss, ks, decays)
    glasts = [jnp.sum(jnp.where(_iota(G.shape, 0) == GC - 1, G, 0.0), axis=0, keepdims=True) for G in Gcs]
    kds = _hmap(lambda k, gl, G: k * jnp.exp(gl - G), ks, glasts, Gcs)
    qes = _hmap(lambda q, eG: q * eG, qss, eGs)
    return us, ws, qks, kds, qes, glasts


def _gdn_chunk_state(us, ws, qks, kds, qes, glasts, Ss):
    v_news = _hmap(lambda u, w, S: u - _dot(w, S), us, ws, Ss)
    qS = _hmap(_dot, qes, Ss)
    os = _hmap(lambda a, qk, vn: a + _dot(qk, vn), qS, qks, v_news)
    S_news = _hmap(lambda S, gl, kd, vn: S * jnp.exp(gl) + _dot_tn(kd, vn), Ss, glasts, kds, v_news)
    return os, S_news


def _gdn_chunk(qs, ks, vs, betas, Gcs, decays, Ss, Tms):
    return _gdn_chunk_state(*_gdn_chunk_pre(qs, ks, vs, betas, Gcs, decays, Tms), Ss)


GDN_CPS = 4


def _gdn_chunk_fwd(qkv_c, ba, alog, dtb, *, name, carry=None):
    W3, CPS = 3 * GH * GD, GDN_CPS

    def body(x_ref, ba_ref, al_ref, dt_ref, o_ref, s_ref, t_ref, S):
        @pl.when(pl.program_id(0) == 0)
        def _():
            S[...] = jnp.zeros_like(S)

        betas, Gcs, decays, qkv = [], [], [], ([], [], [])
        for sub in range(CPS):
            rows = slice(sub * GC, (sub + 1) * GC)
            b, g, d = _gdn_pre(*_gdn_gates(ba_ref[rows, :], al_ref[...], dt_ref[...]))
            betas, Gcs, decays = betas + b, Gcs + g, decays + d
            for p in range(3):
                qkv[p].extend(x_ref[rows, (p * GH + h) * GD:(p * GH + h + 1) * GD] for h in range(GH))
        qs, ks, vs = qkv
        Tms = _tri_inv(_gdn_A(ks, betas, decays))
        pre = _gdn_chunk_pre(qs, ks, vs, betas, Gcs, decays, Tms)
        Ss = [S[h] for h in range(GH)]
        for sub in range(CPS):
            mine = slice(sub * GH, (sub + 1) * GH)
            os, S_news = _gdn_chunk_state(*[part[mine] for part in pre], Ss)
            for h in range(GH):
                s_ref[sub, h] = Ss[h]
                t_ref[sub, h] = Tms[sub * GH + h]
                o_ref[sub * GC:(sub + 1) * GC, h * GD:(h + 1) * GD] = os[h]
            Ss = S_news
        for h in range(GH):
            S[h] = Ss[h]

    vec = pl.BlockSpec((1, 128), lambda n: (0, 0))
    return _seq_call(
        body, carry, name=name, steps=N_CHUNK // CPS,
        in_specs=[pl.BlockSpec((CPS * GC, W3), lambda n: (n, 0)), pl.BlockSpec((CPS * GC, 128), lambda n: (n, 0)), vec, vec],
        out_specs=[pl.BlockSpec((CPS * GC, GH * GD), lambda n: (n, 0)),
                   pl.BlockSpec((CPS, GH, GD, GD), lambda n: (n, 0, 0, 0)),
                   pl.BlockSpec((CPS, GH, GC, GC), lambda n: (n, 0, 0, 0))],
        out_shape=[jax.ShapeDtypeStruct((T, GH * GD), F32), jax.ShapeDtypeStruct((N_CHUNK, GH, GD, GD), F32),
                   jax.ShapeDtypeStruct((N_CHUNK, GH, GC, GC), F32)],
        scratch_shapes=[pltpu.VMEM((GH, GD, GD), F32)], args=(qkv_c, ba, alog, dtb))


def _gdn_chunk_bwd(qkv_c, ba, alog, dtb, s_all, t_all, do, dproj, *, name, carry=None):
    W3 = 3 * GH * GD
    rev = lambda n: N_CHUNK - 1 - n

    def body(x_ref, ba_ref, al_ref, dt_ref, s_ref, t_ref, do_ref, _, dx_ref, dba_ref, dal_ref, ddt_ref, dS):
        @pl.when(pl.program_id(0) == 0)
        def _():
            dS[...] = jnp.zeros_like(dS)
            dal_ref[...] = jnp.zeros_like(dal_ref)
            ddt_ref[...] = jnp.zeros_like(ddt_ref)

        (beta_all, gc_all), vjp_gates = jax.vjp(_gdn_gates, ba_ref[...], al_ref[...], dt_ref[...])
        qs, ks, vs = ([x_ref[:, (p * GH + h) * GD:(p * GH + h + 1) * GD] for h in range(GH)] for p in range(3))
        Tms = [t_ref[0, h] for h in range(GH)]
        Ss = [s_ref[0, h] for h in range(GH)]
        dos = [do_ref[:, h * GD:(h + 1) * GD] for h in range(GH)]
        dSs = [dS[h] for h in range(GH)]

        (betas, Gcs, decays), vjp_pre = jax.vjp(_gdn_pre, beta_all, gc_all)
        _, vjp = jax.vjp(_gdn_chunk, qs, ks, vs, betas, Gcs, decays, Ss, Tms)
        dqs, dks, dvs, dbetas, dGcs, ddecays, dS_prev, dTs = vjp((dos, dSs))
        dXs = _hmap(_dot_nt, dTs, Tms)
        dAs = _hmap(lambda Tm, dX: -_dot_tn(Tm, dX), Tms, dXs)
        _, vjp_a = jax.vjp(_gdn_A, ks, betas, decays)
        dks2, dbetas2, ddecays2 = vjp_a(dAs)
        add = lambda a, b: _hmap(jnp.add, a, b)
        db_all, dg_all = vjp_pre((add(dbetas, dbetas2), dGcs, add(ddecays, ddecays2)))
        for h in range(GH):
            dS[h] = dS_prev[h]
            dx_ref[:, h * GD:(h + 1) * GD] = dqs[h]
            dx_ref[:, (GH + h) * GD:(GH + h + 1) * GD] = dks[h] + dks2[h]
            dx_ref[:, (2 * GH + h) * GD:(2 * GH + h + 1) * GD] = dvs[h]
        dba, dal, ddt = vjp_gates((db_all, dg_all))
        dba_ref[...] = dba.astype(BF16)
        dal_ref[...] += dal
        ddt_ref[...] += ddt

    vec = pl.BlockSpec((1, 128), lambda n: (0, 0))
    return _seq_call(
        body, carry, name=name, steps=N_CHUNK,
        in_specs=[pl.BlockSpec((GC, W3), lambda n: (rev(n), 0)), pl.BlockSpec((GC, 128), lambda n: (rev(n), 0)),
                  vec, vec, pl.BlockSpec((1, GH, GD, GD), lambda n: (rev(n), 0, 0, 0)),
                  pl.BlockSpec((1, GH, GC, GC), lambda n: (rev(n), 0, 0, 0)),
                  pl.BlockSpec((GC, GH * GD), lambda n: (rev(n), 0)),
                  ANY],
        out_specs=[pl.BlockSpec((GC, W3), lambda n: (rev(n), 0)), pl.BlockSpec((GC, 128), lambda n: (rev(n), COL_BA)), vec, vec],
        out_shape=[jax.ShapeDtypeStruct((T, W3), F32), jax.ShapeDtypeStruct(dproj.shape, dproj.dtype),
                   jax.ShapeDtypeStruct((1, 128), F32), jax.ShapeDtypeStruct((1, 128), F32)],
        scratch_shapes=[pltpu.VMEM((GH, GD, GD), F32)], aliases={7: 1},
        args=(qkv_c, ba, alog, dtb, s_all, t_all, do, dproj))


def _gdn_post(o, z, nw):
    return o * lax.rsqrt(jnp.mean(o * o, axis=-1, keepdims=True) + EPS) * nw * _silu(z)


def _gdn_post_fwd(o_raw, proj, nw, mixed, *, name, tm=512):
    def body(o_ref, z_ref, w_ref, _, out_ref):
        out_ref[...] = _gdn_post(o_ref[...], z_ref[...].astype(F32), w_ref[...]).astype(BF16)

    return pl.pallas_call(
        body, name=name, grid=(T // tm, GH),
        in_specs=[pl.BlockSpec((tm, GD), lambda i, h: (i, h)), pl.BlockSpec((tm, GD), lambda i, h: (i, COL_Z + h)),
                  pl.BlockSpec((1, GD), lambda i, h: (0, 0)), ANY],
        out_specs=pl.BlockSpec((tm, GD), lambda i, h: (i, HQ * DH // GD + h)),
        out_shape=jax.ShapeDtypeStruct(mixed.shape, mixed.dtype), input_output_aliases={3: 0},
        compiler_params=_cp("parallel", "parallel"),
    )(o_raw, proj, nw, mixed)


def _gdn_post_bwd(o_raw, proj, nw, dmixed, *, name, tm=512):
    def body(o_ref, z_ref, w_ref, d_ref, do_ref, dz_ref, dw_ref):
        _, vjp = jax.vjp(_gdn_post, o_ref[...], z_ref[...].astype(F32), w_ref[...])
        do, dz, dw = vjp(d_ref[...])
        do_ref[...] = do
        dz_ref[...] = dz.astype(BF16)

        @pl.when((pl.program_id(0) == 0) & (pl.program_id(1) == 0))
        def _():
            dw_ref[...] = jnp.zeros_like(dw_ref)

        dw_ref[...] += dw

    blk = pl.BlockSpec((tm, GD), lambda i, h: (i, h))
    vec = pl.BlockSpec((1, GD), lambda i, h: (0, 0))
    return pl.pallas_call(
        body, name=name, grid=(T // tm, GH),
        in_specs=[blk, pl.BlockSpec((tm, GD), lambda i, h: (i, COL_Z + h)), vec,
                  pl.BlockSpec((tm, GD), lambda i, h: (i, GH + h))],
        out_specs=[blk, pl.BlockSpec((tm, GD), lambda i, h: (i, COL_Z + h)), vec],
        out_shape=[jax.ShapeDtypeStruct((T, GH * GD), F32), jax.ShapeDtypeStruct((T, N_PROJ), BF16),
                   jax.ShapeDtypeStruct((1, GD), F32)],
        compiler_params=_cp("arbitrary", "arbitrary"),
    )(o_raw, proj, nw, dmixed)


def _adamw(w, g, m, v, *, name):
    shape = w.shape
    cols = shape[-1]
    w2, g2, m2, v2 = (a.reshape(-1, cols) for a in (w, g, m, v))
    rows = w2.shape[0]
    tr = next((t for t in (512, 256, 128, 64, 32, 16, 8) if rows % t == 0), rows)

    def body(w_ref, g_ref, m_ref, v_ref, d_ref, nm_ref, nv_ref):
        gv = g_ref[...]
        nm = B1 * m_ref[...] + (1.0 - B1) * gv
        nv = B2 * v_ref[...] + (1.0 - B2) * jnp.square(gv)
        m_hat = nm / (1.0 - B1 ** STEP)
        v_hat = nv / (1.0 - B2 ** STEP)
        d_ref[...] = -LR * (m_hat / (jnp.sqrt(v_hat) + AEPS) + WD * w_ref[...])
        nm_ref[...] = nm
        nv_ref[...] = nv

    blk = pl.BlockSpec((tr, cols), lambda i: (i, 0))
    out = pl.pallas_call(
        body, name=name, grid=(rows // tr,), in_specs=[blk] * 4, out_specs=[blk] * 3,
        out_shape=[jax.ShapeDtypeStruct((rows, cols), F32)] * 3, compiler_params=_cp("parallel"),
    )(w2, g2, m2, v2)
    return tuple(o.reshape(shape) for o in out)


def _layer_weights(w_in):
    return jnp.pad(w_in, ((0, 0), (0, N_PROJ - w_in.shape[1])))


def _layer_params(attn_norm, sinks, gcw, a_log, dt_bias, gnw, ffn_norm, fcw, fcb):
    lanes4 = lambda v: jnp.pad(v, (GH, 128 - 2 * GH))[None]
    return dict(attn_norm=attn_norm[None], sinks=jnp.pad(sinks, (0, 128 - HQ))[None], gcw=gcw, alog=lanes4(a_log),
                dtb=lanes4(dt_bias), gnw=gnw[None], ffn_norm=ffn_norm[None], fcw=fcw, fcb=fcb[None])


def _mixer_fwd(x, W, P, l, carry=None):
    n = lambda s: f"l{l}_{s}"
    h, proj, ba = _rmsnorm_proj(x, P["attn_norm"], W["all"], name=n("norm1_proj"))
    mixed = _swa_fwd(proj, P["sinks"], name=n("swa"))
    qkv_c = _gdn_prep_fwd(proj, P["gcw"], name=n("gdn_prep"))
    (o_raw, s_all, t_all), carried = _gdn_chunk_fwd(qkv_c, ba, P["alog"], P["dtb"], name=n("gdn_chunk"), carry=carry)
    mixed = _gdn_post_fwd(o_raw, proj, P["gnw"], mixed, name=n("gdn_post"))
    x1 = _mm(mixed, W["out"], res=x, name=n("out_proj"), mode="nn", dims=(T, D, D), tm=512, tn=D, tk=D)
    saved = dict(x=x, h=h, proj=proj, ba=ba, qkv_c=qkv_c, o_raw=o_raw, s_all=s_all, t_all=t_all, mixed=mixed, x1=x1)
    return x1, saved, carried


def _ffn_fwd(x1, W, P, l, carry=None):
    n = lambda s: f"l{l}_{s}"
    h2 = _rmsnorm_fwd(x1, P["ffn_norm"], name=n("norm2"))
    gu = _mm(h2, W["ffn"], name=n("ffn_in"), mode="nn", dims=(T, 2 * DFF, D), tm=512, tn=FFN_CW, tk=D, out_dtype=BF16,
             b_spec=pl.BlockSpec((None, D, FFN_CW), lambda i, j, k: (j, k, 0)), n_outer=True)
    act, carried = _ffn_act_fwd(gu, P["fcw"], P["fcb"], name=n("ffn_act"), carry=carry)
    x2 = _mm(act, W["down"], res=x1, name=n("ffn_down"), mode="nn", dims=(T, D, DFF), tm=512, tn=D, tk=DFF)
    return x2, dict(h2=h2, gu=gu, act=act), carried


def _ffn_bwd(dx2, sv, W, P, l):
    n = lambda s: f"l{l}_{s}"
    CW = FFN_CW
    dact = _mm(dx2, W["down"], name=n("d_act"), mode="nt", dims=(T, DFF, D), tm=512, tn=DFF // 2, tk=D, out_dtype=BF16,
               n_outer=True)
    g_down = _mm(sv["act"], dx2, name=n("g_down"), mode="tn", dims=(DFF, D, T), tm=DFF // 2, tn=D, tk=min(T, 2048), out_dtype=BF16)
    dgu, g_fcw, g_fcb = _ffn_act_bwd(sv["gu"], P["fcw"], P["fcb"], dact, name=n("d_ffn_act"))
    dx1, g_ffn_norm = _mm_rmsnorm_bwd(
        dgu, W["ffn"], sv["x1"], P["ffn_norm"], dx2, name=n("d_h2_norm2"), K=2 * DFF, tm=512, tk=CW,
        a_spec=pl.BlockSpec((None, 512, CW), lambda i, k: (k // 2, i, k % 2)),
        b_spec=pl.BlockSpec((None, D, CW), lambda i, k: (k, 0, 0)))
    g_ffn = _mm(sv["h2"], dgu, name=n("g_ffn"), mode="tn", dims=(D, 2 * DFF, T), tm=512, tn=CW, tk=T, n_outer=True,
                b_spec=pl.BlockSpec((None, T, CW), lambda i, j, k: (j // 2, k, j % 2)),
                out_spec=pl.BlockSpec((None, 512, CW), lambda i, j, k: (j, i, 0)),
                out_shape=jax.ShapeDtypeStruct((N_CHIP, D, CW), BF16), out_dtype=BF16)
    return dx1, dict(ffn_norm=g_ffn_norm[0], w_ffn_in=g_ffn, ffn_conv_w=g_fcw, ffn_conv_b=g_fcb[0], w_down=g_down)


def _mixer_bwd(dx1, sv, W, P, l, carry=None):
    n = lambda s: f"l{l}_{s}"
    dmixed = _mm(dx1, W["out"], name=n("d_mixed"), mode="nt", dims=(T, D, D), tm=512, tn=D, tk=D)
    g_out = _mm(sv["mixed"], dx1, name=n("g_out"), mode="tn", dims=(D, D, T), tm=D, tn=D, tk=min(T, 2048), out_dtype=BF16)
    do_raw, dproj, g_gnw = _gdn_post_bwd(sv["o_raw"], sv["proj"], P["gnw"], dmixed, name=n("d_gdn_post"))
    (dqkv_c, dproj, g_alog, g_dtb), carried = _gdn_chunk_bwd(
        sv["qkv_c"], sv["ba"], P["alog"], P["dtb"], sv["s_all"], sv["t_all"], do_raw, dproj, name=n("d_gdn_chunk"), carry=carry)
    dproj, g_gcw = _gdn_prep_bwd(sv["proj"], P["gcw"], dqkv_c, dproj, name=n("d_gdn_prep"))
    dproj, g_sinks = _swa_bwd(sv["proj"], P["sinks"], dmixed, dproj, name=n("d_swa"))
    dx, g_attn_norm = _mm_rmsnorm_bwd(dproj, W["all"], sv["x"], P["attn_norm"], dx1, name=n("d_h_norm1"), K=N_PROJ,
                                      tm=512, tk=N_PROJ)
    g_all = _mm(sv["h"], dproj, name=n("g_in"), mode="tn", dims=(D, N_PROJ, T), tm=512, tn=N_PROJ, tk=min(T, 2048), out_dtype=BF16)
    grads = dict(attn_norm=g_attn_norm[0], w_in=g_all, attn_sinks=g_sinks[0, :HQ], gdn_conv_w=g_gcw,
                 gdn_a_log=g_alog[0, GH:2 * GH], gdn_dt_bias=g_dtb[0, GH:2 * GH], gdn_norm=g_gnw[0], w_out=g_out)
    return dx, grads, carried


def _pos():
    return lax.axis_index("x"), lax.axis_index("y"), lax.axis_index("c")


def _other_chips(x, y):
    return [(1 - x, y), (x, 1 - y), (1 - x, 1 - y)]


def _remote(src, dst, send_sems, recv_sems, k, to):
    return pltpu.make_async_remote_copy(src_ref=src, dst_ref=dst, send_sem=send_sems.at[k], recv_sem=recv_sems.at[k],
                                        device_id=to, device_id_type=MESH)


def _run_carry(carry, *, name):
    ci, co = len(carry.arrays), len(carry.out_shapes)

    def body(*refs):
        sends, recvs = carry.plan(refs[:ci], refs[ci:ci + co], refs[-2], refs[-1])
        for cp in sends:
            cp.start()
        for cp in recvs:
            cp.wait_recv()
        for cp in sends:
            cp.wait_send()

    return list(pl.pallas_call(
        body, name=name, in_specs=[ANY] * ci, out_specs=[ANY] * co, out_shape=carry.out_shapes,
        scratch_shapes=[pltpu.SemaphoreType.DMA((carry.nsem,)), pltpu.SemaphoreType.DMA((carry.nsem,))],
    )(*carry.arrays))


def _chip_index():
    return 2 * lax.axis_index("x") + lax.axis_index("y")


def _gather_carry(shards):
    def plan(srcs, outs, send_sems, recv_sems):
        x, y, c = _pos()
        chip = 2 * x + y
        others = [(k, px, py, 2 * px + py) for k, (px, py) in enumerate(_other_chips(x, y))]
        sends = [_remote(srcs[t], outs[t].at[chip], send_sems, recv_sems, 3 * t + k, (px, py, c))
                 for t in range(len(srcs)) for k, px, py, _ in others]
        recvs = [_remote(srcs[t], outs[t].at[j], send_sems, recv_sems, 3 * t + k, (x, y, c))
                 for t in range(len(srcs)) for k, _, _, j in others]
        return sends, recvs

    return _Carry(shards, [jax.ShapeDtypeStruct((N_CHIP,) + s.shape, s.dtype) for s in shards], 3 * len(shards), plan)


def _fill_own(outs, shards):
    return [lax.dynamic_update_index_in_dim(o, s, _chip_index(), 0) for o, s in zip(outs, shards)]


def _gather_halves(shards, after, *, name):
    nt = len(shards)

    def body(*refs):
        srcs, outs, send_sems, recv_sems = refs[:nt], refs[nt + 1:2 * nt + 1], refs[-2], refs[-1]
        x, y, c = _pos()
        me, sibling, chip = (x, y, c), (x, y, 1 - c), 2 * x + y
        others = [(k, px, py, 2 * px + py) for k, (px, py) in enumerate(_other_chips(x, y))]

        def rows(t, h):
            rh = srcs[t].shape[0] // 2
            return pl.ds(pl.multiple_of(h * rh, 16), rh)

        first = [_remote(srcs[t].at[rows(t, c), :], outs[t].at[chip, rows(t, c), :], send_sems, recv_sems, 3 * t + k, (px, py, c))
                 for t in range(nt) for k, px, py, _ in others]
        for cp in first:
            cp.start()
        passed = []
        for t in range(nt):
            for k, _, _, j in others:
                slot = outs[t].at[j, rows(t, c), :]
                _remote(slot, slot, send_sems, recv_sems, 3 * t + k, me).wait_recv()
                passed.append(_remote(slot, slot, send_sems, recv_sems, 3 * (nt + t) + k, sibling))
                passed[-1].start()
        for t in range(nt):
            for k, _, _, j in others:
                slot = outs[t].at[j, rows(t, 1 - c), :]
                _remote(slot, slot, send_sems, recv_sems, 3 * (nt + t) + k, me).wait_recv()
        for cp in first + passed:
            cp.wait_send()

    outs = pl.pallas_call(
        body, name=name, in_specs=[ANY] * (nt + 1), out_specs=[ANY] * nt,
        out_shape=[jax.ShapeDtypeStruct((N_CHIP,) + s.shape, s.dtype) for s in shards],
        scratch_shapes=[pltpu.SemaphoreType.DMA((6 * nt,)), pltpu.SemaphoreType.DMA((6 * nt,))],
    )(*shards, after)
    return _fill_own(outs, shards)


HBM_SPEC = pl.BlockSpec(memory_space=pltpu.HBM)
SEM_SPEC = pl.BlockSpec(memory_space=pltpu.SEMAPHORE)
DATAFLOW = pltpu.SideEffectType.DATAFLOW_SIDE_EFFECTING


def _split_start(carry, after, *, name):
    ci, co = len(carry.arrays), len(carry.out_shapes)

    def body(*refs):
        srcs, lands, send_sems, recv_sems, token = refs[:ci], refs[ci:ci + co], refs[ci + co + 1], refs[ci + co + 2], refs[-1]
        for cp in carry.plan(srcs, lands, send_sems, recv_sems)[0]:
            cp.start()
        token[...] = jnp.zeros_like(token)

    lands = [lax.empty(s.shape, s.dtype) for s in carry.out_shapes]
    hbm = lambda a: pltpu.with_memory_space_constraint(a, pltpu.HBM)
    out = pl.pallas_call(
        body, name=name, in_specs=[HBM_SPEC] * (ci + co) + [ANY],
        out_specs=[SEM_SPEC, SEM_SPEC] + [HBM_SPEC] * (ci + co) + [VMEM_SPEC],
        out_shape=[pltpu.SemaphoreType.DMA((carry.nsem,)), pltpu.SemaphoreType.DMA((carry.nsem,))]
        + [pltpu.HBM(a.shape, a.dtype) for a in carry.arrays + lands] + [jax.ShapeDtypeStruct((8, 128), F32)],
        input_output_aliases={t: 2 + t for t in range(ci + co)},
        compiler_params=pltpu.CompilerParams(has_side_effects=DATAFLOW),
    )(*[hbm(a) for a in carry.arrays + lands], after)
    return out[0], out[1], list(out[2:2 + ci]), list(out[2 + ci:2 + ci + co]), out[-1]


def _split_wait(carry, send_sems, recv_sems, srcs, lands, after, *, name):
    ci, co = len(srcs), len(lands)

    def body(*refs):
        sends, recvs = carry.plan(refs[:ci], refs[ci:ci + co], refs[ci + co], refs[ci + co + 1])
        for cp in sends:
            cp.wait_send()
        for cp in recvs:
            cp.wait_recv()

    out = pl.pallas_call(
        body, name=name, in_specs=[HBM_SPEC] * (ci + co) + [SEM_SPEC, SEM_SPEC, ANY], out_specs=[HBM_SPEC] * (ci + co),
        out_shape=[pltpu.HBM(a.shape, a.dtype) for a in list(srcs) + list(lands)],
        input_output_aliases={t: t for t in range(ci + co)},
        compiler_params=pltpu.CompilerParams(has_side_effects=DATAFLOW),
    )(*srcs, *lands, send_sems, recv_sems, after)
    return list(out[:ci]), list(out[ci:])


def _gather_start(shards, after, *, name):
    return _split_start(_gather_carry(shards), after, name=name)


def _gather_wait(send_sems, recv_sems, shards, lands, after, *, name):
    srcs, got = _split_wait(_gather_carry(shards), send_sems, recv_sems, shards, lands, after, name=name)
    return _fill_own(got, srcs)


def _ag_small(v, *, name):
    m, n = v.shape

    def body(x_ref, out_ref, red_ref, send_sems, recv_sems, local_sem):
        x, y, c = _pos()
        me, sibling = (x, y, c), (x, y, 1 - c)
        chips = _other_chips(x, y)
        rows = lambda px, py, pc: out_ref.at[pl.ds(pl.multiple_of((4 * px + 2 * py + pc) * m, 8), m), :]
        mine = pltpu.make_async_copy(x_ref, rows(*me), local_sem)
        mine.start()
        first = [_remote(x_ref, rows(*me), send_sems, recv_sems, 0, sibling)]
        first += [_remote(x_ref, rows(*me), send_sems, recv_sems, 1 + k, (*chip, c)) for k, chip in enumerate(chips)]
        for cp in first:
            cp.start()
        passed = [_remote(rows(*chip, c), rows(*chip, c), send_sems, recv_sems, 4 + k, sibling) for k, chip in enumerate(chips)]
        for k, chip in enumerate(chips):
            _remote(rows(*chip, c), rows(*chip, c), send_sems, recv_sems, 1 + k, me).wait_recv()
            passed[k].start()
        _remote(rows(*sibling), rows(*sibling), send_sems, recv_sems, 0, me).wait_recv()
        for k, chip in enumerate(chips):
            _remote(rows(*chip, 1 - c), rows(*chip, 1 - c), send_sems, recv_sems, 4 + k, me).wait_recv()
        for cp in first + passed:
            cp.wait_send()
        mine.wait()
        acc = out_ref[0:m, :]
        for d in range(1, N_DEV):
            acc = acc + out_ref[d * m:(d + 1) * m, :]
        red_ref[...] = acc

    return pl.pallas_call(
        body, name=name, in_specs=[VMEM_SPEC], out_specs=[VMEM_SPEC, VMEM_SPEC],
        out_shape=[jax.ShapeDtypeStruct((N_DEV * m, n), v.dtype), jax.ShapeDtypeStruct((m, n), v.dtype)],
        scratch_shapes=[pltpu.SemaphoreType.DMA((7,)), pltpu.SemaphoreType.DMA((7,)), pltpu.SemaphoreType.DMA],
    )(v)


def _halves(ref, c):
    rh = ref.shape[1] // 2
    return ref.at[:, pl.ds(pl.multiple_of(c * rh, 16), rh), :]


def _rs_swap(gs, *, name):
    nt = len(gs)

    def body(*refs):
        g, theirs = refs[:nt], refs[nt:2 * nt]
        send_sems, recv_sems = refs[2 * nt:]
        x, y, c = _pos()
        swaps = [_remote(_halves(g[t], 1 - c), theirs[t], send_sems, recv_sems, t, (x, y, 1 - c)) for t in range(nt)]
        for cp in swaps:
            cp.start()
        for cp in swaps:
            cp.wait()

    return pl.pallas_call(
        body, name=name, in_specs=[ANY] * nt, out_specs=[ANY] * nt,
        out_shape=[jax.ShapeDtypeStruct((a.shape[0], a.shape[1] // 2, a.shape[2]), a.dtype) for a in gs],
        scratch_shapes=[pltpu.SemaphoreType.DMA((nt,)), pltpu.SemaphoreType.DMA((nt,))],
    )(*gs)


def _exchange_carry(ss):
    def plan(s, out, send_sems, recv_sems):
        x, y, c = _pos()
        chip = 2 * x + y
        others = [(k, px, py, 2 * px + py) for k, (px, py) in enumerate(_other_chips(x, y))]
        sends = [_remote(s[t].at[j], out[t].at[chip], send_sems, recv_sems, 3 * t + k, (px, py, c))
                 for t in range(len(s)) for k, px, py, j in others]
        recvs = [_remote(s[t].at[j], out[t].at[j], send_sems, recv_sems, 3 * t + k, (x, y, c))
                 for t in range(len(s)) for k, _, _, j in others]
        return sends, recvs

    return _Carry(ss, [jax.ShapeDtypeStruct(a.shape, a.dtype) for a in ss], 3 * len(ss), plan)


def _fill_own_slab(outs, ss):
    chip = _chip_index()
    return [lax.dynamic_update_index_in_dim(o, lax.dynamic_index_in_dim(s, chip, 0, keepdims=False), chip, 0)
            for o, s in zip(outs, ss)]


def _rs_join(rs, *, name):
    nt = len(rs)

    def body(*refs):
        r, theirs = refs[:nt], refs[nt:2 * nt]
        send_sems, recv_sems = refs[2 * nt:]
        x, y, c = _pos()
        swaps = [_remote(r[t], theirs[t], send_sems, recv_sems, t, (x, y, 1 - c)) for t in range(nt)]
        for cp in swaps:
            cp.start()
        for cp in swaps:
            cp.wait()

    theirs = pl.pallas_call(
        body, name=name, in_specs=[ANY] * nt, out_specs=[ANY] * nt,
        out_shape=[jax.ShapeDtypeStruct(a.shape, a.dtype) for a in rs],
        scratch_shapes=[pltpu.SemaphoreType.DMA((nt,)), pltpu.SemaphoreType.DMA((nt,))],
    )(*rs)
    first = lax.axis_index("c") == 0
    return [jnp.concatenate([jnp.where(first, a, b), jnp.where(first, b, a)], axis=0) for a, b in zip(rs, theirs)]


def _row_tile(rows, dtype):
    unit = 16 if dtype == BF16 else 8
    if rows <= 512:
        return rows
    return next((t for t in (512, 256, 128, 64, 32, 16, 8) if rows % t == 0 and t % unit == 0), rows)


def _add_pair(g, b, half, *, name):
    n, rh, cols = b.shape
    tr = _row_tile(rh, BF16)
    nb = rh // tr

    def body(half_ref, g_ref, b_ref, o_ref):
        o_ref[...] = (g_ref[...].astype(F32) + b_ref[...].astype(F32)).astype(BF16)

    blk = pl.BlockSpec((None, tr, cols), lambda j, i, h: (j, i, 0))
    return pl.pallas_call(
        body, name=name,
        grid_spec=pltpu.PrefetchScalarGridSpec(
            num_scalar_prefetch=1, grid=(n, nb),
            in_specs=[pl.BlockSpec((None, tr, cols), lambda j, i, h: (j, h[0] * nb + i, 0)), blk], out_specs=blk),
        out_shape=jax.ShapeDtypeStruct(b.shape, BF16), compiler_params=_cp("parallel", "parallel"),
    )(half, g, b)


def _sum_chips(b, *, name):
    n, rh, cols = b.shape
    tr = _row_tile(rh, BF16)

    def body(b_ref, o_ref):
        acc = b_ref[0].astype(F32)
        for j in range(1, n):
            acc = acc + b_ref[j].astype(F32)
        o_ref[...] = acc

    return pl.pallas_call(
        body, name=name, grid=(rh // tr,), in_specs=[pl.BlockSpec((n, tr, cols), lambda i: (0, i, 0))],
        out_specs=pl.BlockSpec((tr, cols), lambda i: (i, 0)),
        out_shape=jax.ShapeDtypeStruct((rh, cols), F32), compiler_params=_cp("parallel"),
    )(b)


BIG = ("w_in", "w_out", "w_ffn_in", "w_down")
W_IN_SHARD = (N_MAIN + 2 * GH) // N_CHIP


def _chip_major(name, g):
    if name == "w_in":
        return jnp.stack([g[:, j * W_IN_SHARD:(j + 1) * W_IN_SHARD] for j in range(N_CHIP)])
    if name == "w_ffn_in":
        return g
    return g.reshape(N_CHIP, g.shape[0] // N_CHIP, g.shape[1])


def _rs_pairs(gs, tag):
    half = lax.axis_index("c").astype(jnp.int32).reshape(1)
    theirs = _rs_swap(gs, name=f"{tag}_swap")
    return [_add_pair(a, b, half, name=f"{tag}_add{i}") for i, (a, b) in enumerate(zip(gs, theirs))]


def _rs_sums(got, pairs, tag):
    return [_sum_chips(b, name=f"{tag}_sum{i}") for i, b in enumerate(_fill_own_slab(got, pairs))]


def _size(shape):
    n = 1
    for s in shape:
        n *= s
    return n


def _pack_flat(parts, total, dtype):
    flat = jnp.concatenate([p.reshape(-1).astype(dtype) for p in parts])
    return jnp.pad(flat, (0, total - flat.shape[0]))


def _unpack_flat(flat, shapes):
    out, o = [], 0
    for s in shapes:
        out.append(flat[o:o + _size(s)].reshape(s))
        o += _size(s)
    return out


WEIGHTS = ("attn_norm", "w_in", "attn_sinks", "gdn_conv_w", "gdn_a_log", "gdn_dt_bias", "gdn_norm", "w_out", "ffn_norm",
           "w_ffn_in", "ffn_conv_w", "ffn_conv_b", "w_down", "final_norm")
SMALL = {"attn_norm": (DEPTH, D), "attn_sinks": (DEPTH, HQ), "gdn_a_log": (DEPTH, GH), "gdn_dt_bias": (DEPTH, GH),
         "gdn_norm": (DEPTH, GD), "ffn_norm": (DEPTH, D), "ffn_conv_b": (DEPTH, DFF), "final_norm": (D,)}
CONV_FULL = {"gdn_conv_w": (DEPTH, GK, 3 * GH * GD), "ffn_conv_w": (DEPTH, FK, DFF)}
CONV_SHARD = {"gdn_conv_w": (DEPTH, GK, 3 * GH * GD // N_CHIP), "ffn_conv_w": (DEPTH, FK, DFF // N_CHIP)}
CONV_ROWS, SMALLG_ROWS, SMALLW_ROWS = 64, 320, 144


def kernel(x, attn_norm, w_in, attn_sinks, gdn_conv_w, gdn_a_log, gdn_dt_bias, gdn_norm, w_out, ffn_norm, w_ffn_in, ffn_conv_w, ffn_conv_b, w_down, final_norm, loss_target, m_attn_norm, m_w_in, m_attn_sinks, m_gdn_conv_w, m_gdn_a_log, m_gdn_dt_bias, m_gdn_norm, m_w_out, m_ffn_norm, m_w_ffn_in, m_ffn_conv_w, m_ffn_conv_b, m_w_down, m_final_norm, v_attn_norm, v_w_in, v_attn_sinks, v_gdn_conv_w, v_gdn_a_log, v_gdn_dt_bias, v_gdn_norm, v_w_out, v_ffn_norm, v_w_ffn_in, v_ffn_conv_w, v_ffn_conv_b, v_w_down, v_final_norm):
    w = dict(zip(WEIGHTS, (attn_norm, w_in, attn_sinks, gdn_conv_w, gdn_a_log, gdn_dt_bias, gdn_norm, w_out, ffn_norm,
                           w_ffn_in, ffn_conv_w, ffn_conv_b, w_down, final_norm)))
    m = dict(zip(WEIGHTS, (m_attn_norm, m_w_in, m_attn_sinks, m_gdn_conv_w, m_gdn_a_log, m_gdn_dt_bias, m_gdn_norm, m_w_out,
                           m_ffn_norm, m_w_ffn_in, m_ffn_conv_w, m_ffn_conv_b, m_w_down, m_final_norm)))
    v = dict(zip(WEIGHTS, (v_attn_norm, v_w_in, v_attn_sinks, v_gdn_conv_w, v_gdn_a_log, v_gdn_dt_bias, v_gdn_norm, v_w_out,
                           v_ffn_norm, v_w_ffn_in, v_ffn_conv_w, v_ffn_conv_b, v_w_down, v_final_norm)))
    cx, cy, _ = _pos()
    chip = 2 * cx + cy

    cpack = _pack_flat([w[n] for n in CONV_SHARD], CONV_ROWS * 128, F32).reshape(CONV_ROWS, 128)
    cgath, _ = _ag_small(cpack, name="gather_conv_w")
    cgath = cgath.reshape(N_DEV, CONV_ROWS * 128)
    cshards = [_unpack_flat(cgath[2 * j], list(CONV_SHARD.values())) for j in range(N_CHIP)]
    conv = {n: jnp.concatenate([cshards[j][i] for j in range(N_CHIP)], axis=2) for i, n in enumerate(CONV_SHARD)}

    Ps = [_layer_params(attn_norm[l], attn_sinks[l], conv["gdn_conv_w"][l], gdn_a_log[l], gdn_dt_bias[l], gdn_norm[l],
                        ffn_norm[l], conv["ffn_conv_w"][l], ffn_conv_b[l]) for l in range(DEPTH)]

    wb = {n: [w[n][l].astype(BF16) for l in range(DEPTH)] for n in BIG}
    mixer_w = lambda g_in, g_out: dict(all=_layer_weights(jnp.concatenate(list(g_in), axis=1)), out=g_out.reshape(D, D))
    ffn_w = lambda g_ffn, g_down: dict(ffn=g_ffn, down=g_down.reshape(DFF, D))
    groups = [[wb[a][l], wb[b][l]] for l in range(DEPTH) for a, b in (("w_in", "w_out"), ("w_ffn_in", "w_down"))]
    def after_start(P, key, started):
        return {**P, key: P[key] + started[4][:1, :1]}

    got0 = _gather_halves(groups[0], cgath, name="gather0")
    g1 = _gather_start(groups[1], got0[1], name="gather1_start")
    Wm0 = mixer_w(*got0)
    h, sv0, _ = _mixer_fwd(x[0], Wm0, after_start(Ps[0], "attn_norm", g1), 0)
    got1 = _gather_wait(*g1[:4], h, name="gather1_wait")
    g2 = _gather_start(groups[2], got1[1], name="gather2_start")
    Wf0 = ffn_w(*got1)
    h, sv0f, _ = _ffn_fwd(h, Wf0, after_start(Ps[0], "ffn_norm", g2), 0)
    got2 = _gather_wait(*g2[:4], h, name="gather2_wait")
    g3 = _gather_start(groups[3], got2[1], name="gather3_start")
    Wm1 = mixer_w(*got2)
    h, sv1, _ = _mixer_fwd(h, Wm1, after_start(Ps[1], "attn_norm", g3), 1)
    Wf1 = ffn_w(*_gather_wait(*g3[:4], h, name="gather3_wait"))
    h, sv1f, _ = _ffn_fwd(h, Wf1, Ps[1], 1)
    dx, g_final, loss_part = _loss_head(h, final_norm[None], loss_target[0], name="loss_head")

    def exchange_start(pairs, tag):
        carry = _exchange_carry(pairs)
        return carry, _split_start(carry, pairs[0], name=f"{tag}_exchange_start")

    def exchange_wait(carry, started, after, tag):
        srcs, got = _split_wait(carry, *started[:4], after, name=f"{tag}_exchange_wait")
        return _rs_sums(got, srcs, tag)

    tied = lambda P, key, started: {**P, key: P[key] + started[4][:1, :1]}
    dx, gf1 = _ffn_bwd(dx, {**sv1, **sv1f}, Wf1, Ps[1], 1)
    dx, gm1, _ = _mixer_bwd(dx, sv1, Wm1, Ps[1], 1)
    lg1 = {**gf1, **gm1}
    pairs1 = _rs_pairs([_chip_major(n, lg1[n]) for n in BIG], "rs1")
    ex1 = exchange_start(pairs1, "rs1")
    dx, gf0 = _ffn_bwd(dx, {**sv0, **sv0f}, Wf0, tied(Ps[0], "fcb", ex1[1]), 0)
    pairs0a = _rs_pairs([_chip_major(n, gf0[n]) for n in BIG[2:]], "rs0a")
    ex0a = exchange_start(pairs0a, "rs0a")
    dx, gm0, _ = _mixer_bwd(dx, sv0, Wm0, tied(Ps[0], "gnw", ex0a[1]), 0)
    lg = [{**gf0, **gm0}, lg1]
    pairs0b = _rs_pairs([_chip_major(n, gm0[n]) for n in BIG[:2]], "rs0b")
    sums0b = _rs_sums(_run_carry(_exchange_carry(pairs0b), name="rs0b_exchange"), pairs0b, "rs0b")
    sums1 = exchange_wait(*ex1, sums0b[0], "rs1")
    sums0a = exchange_wait(*ex0a, sums1[0], "rs0a")
    joined = _rs_join(sums0b + sums0a + sums1, name="rs_join")
    reduced = [joined[:4], joined[4:]]
    grad_x = dx[None]
    stacked = lambda n: jnp.stack([lg[l][n] for l in range(DEPTH)])

    small_parts = [g_final[0] if n == "final_norm" else stacked(n) for n in SMALL] + [stacked(n) for n in CONV_FULL]
    svec = _pack_flat(small_parts + [loss_part[0, :1]], SMALLG_ROWS * 128, F32).reshape(SMALLG_ROWS, 128)
    _, sred = _ag_small(svec, name="reduce_small")
    small_g = _unpack_flat(sred.reshape(-1), list(SMALL.values()) + list(CONV_FULL.values()) + [(1,)])
    g = dict(zip(list(SMALL) + list(CONV_FULL), small_g[:-1]))
    loss = small_g[-1][0]
    for n in CONV_FULL:
        wd = CONV_SHARD[n][2]
        g[n] = lax.dynamic_slice_in_dim(g[n], chip * wd, wd, axis=2)

    g.update({n: jnp.stack([reduced[l][i] for l in range(DEPTH)]) for i, n in enumerate(BIG)})

    delta, new_m, new_v = {}, {}, {}
    for n in BIG:
        delta[n], new_m[n], new_v[n] = _adamw(w[n], g[n], m[n], v[n], name=f"adamw_{n}")
    small_names = list(SMALL) + list(CONV_SHARD)
    small_shapes = list(SMALL.values()) + list(CONV_SHARD.values())
    packed = [_pack_flat([t[n] for n in small_names], SMALLW_ROWS * 128, F32).reshape(SMALLW_ROWS, 128) for t in (w, g, m, v)]
    for res, out in zip(_adamw(*packed, name="adamw_small"), (delta, new_m, new_v)):
        out.update(zip(small_names, _unpack_flat(res.reshape(-1), small_shapes)))

    return (loss, grad_x, *[g[n] for n in WEIGHTS], *[delta[n] for n in WEIGHTS], *[new_m[n] for n in WEIGHTS],
            *[new_v[n] for n in WEIGHTS])
```

```python
import functools

import jax
import jax.numpy as jnp
import numpy as np
from jax import lax
from jax.experimental import pallas as pl
from jax.experimental.pallas import tpu as pltpu

F32, BF16 = jnp.float32, jnp.bfloat16
HIGHEST = lax.Precision.HIGHEST
MESH = pl.DeviceIdType.MESH

D = 1024
T = 4096
DEPTH = 2
HQ, HKV, DH, WIN = 8, 2, 64, 128
GH, GD, GC, GK = 4, 128, 64, 4
DFF, FK = 2816, 3
EPS = 1e-6
N_MAIN = 2816
N_PROJ = N_MAIN + 128
COL_K, COL_V, COL_G, COL_Z, COL_BA = 4, 5, 6, 18, 22
N_CHUNK = T // GC
N_DEV, N_CHIP = 8, 4
FFN_CW = 2 * DFF // N_CHIP
LR, B1, B2, AEPS, WD, STEP = 0.001, 0.9, 0.999, 1e-08, 0.01, 10

VMEM_LIMIT = 56 * 1024 * 1024
ANY = pl.BlockSpec(memory_space=pl.ANY)
VMEM_SPEC = pl.BlockSpec(memory_space=pltpu.VMEM)


def _cp(*sem):
    return pltpu.CompilerParams(dimension_semantics=sem if sem else None, vmem_limit_bytes=VMEM_LIMIT)


class _Carry:
    def __init__(self, arrays, out_shapes, nsem, plan):
        self.arrays, self.out_shapes, self.nsem, self.plan = list(arrays), list(out_shapes), nsem, plan


def _seq_call(body, carry, *, name, steps, in_specs, out_specs, out_shape, args, scratch_shapes=(), aliases=None):
    in_specs, out_specs, out_shape, scratch = list(in_specs), list(out_specs), list(out_shape), list(scratch_shapes)
    n_in, n_out, n_scr = len(in_specs), len(out_specs), len(scratch)
    if carry is None:
        fn = body
    else:
        ci, co = len(carry.arrays), len(carry.out_shapes)

        def fn(*refs):
            ins, cins = refs[:n_in], refs[n_in:n_in + ci]
            outs, couts = refs[n_in + ci:n_in + ci + n_out], refs[n_in + ci + n_out:n_in + ci + n_out + co]
            scr, (ssem, rsem) = refs[n_in + ci + n_out + co:-2], refs[-2:]
            sends, recvs = carry.plan(cins, couts, ssem, rsem)

            @pl.when(pl.program_id(0) == 0)
            def _():
                for cp in sends:
                    cp.start()

            body(*ins, *outs, *scr)

            @pl.when(pl.program_id(0) == steps - 1)
            def _():
                for cp in recvs:
                    cp.wait_recv()
                for cp in sends:
                    cp.wait_send()

        in_specs += [ANY] * ci
        out_specs += [ANY] * co
        out_shape += carry.out_shapes
        scratch += [pltpu.SemaphoreType.DMA((carry.nsem,)), pltpu.SemaphoreType.DMA((carry.nsem,))]
        args = list(args) + carry.arrays
    outs = pl.pallas_call(
        fn, name=name, grid=(steps,), in_specs=in_specs, out_specs=out_specs, out_shape=out_shape,
        scratch_shapes=scratch, input_output_aliases=aliases or {}, compiler_params=_cp("arbitrary"),
    )(*args)
    return list(outs[:n_out]), list(outs[n_out:])


def _dot(a, b):
    return jnp.dot(a.astype(BF16), b.astype(BF16), preferred_element_type=F32)


def _dot_nt(a, b):
    return lax.dot_general(a.astype(BF16), b.astype(BF16), (((1,), (1,)), ((), ())), preferred_element_type=F32)


def _dot_tn(a, b):
    return lax.dot_general(a.astype(BF16), b.astype(BF16), (((0,), (0,)), ((), ())), preferred_element_type=F32)


def _iota(shape, dim):
    return lax.broadcasted_iota(jnp.int32, shape, dim)


def _col(x, idx):
    return jnp.sum(jnp.where(_iota(x.shape, 1) == idx, x, 0.0), axis=1, keepdims=True)


def _silu(y):
    return y * jax.nn.sigmoid(y)


_MM_DN = {"nn": (((1,), (0,)), ((), ())), "nt": (((1,), (1,)), ((), ())), "tn": (((0,), (0,)), ((), ()))}


def _mm(a, b, *, name, mode, dims, tm, tn, tk, res=None, out_dtype=F32, a_spec=None, b_spec=None, out_spec=None,
        out_shape=None, n_outer=False):
    M, N, K = dims
    assert M % tm == 0 and N % tn == 0 and K % tk == 0, (name, dims)
    nk = K // tk
    dn = _MM_DN[mode]

    def body(*refs):
        a_ref, b_ref = refs[:2]
        r_ref = None if res is None else refs[2]
        o_ref = refs[2 if res is None else 3]
        part = lax.dot_general(a_ref[...].astype(BF16), b_ref[...].astype(BF16), dn, preferred_element_type=F32)
        if nk == 1:
            o_ref[...] = (part if res is None else part + r_ref[...]).astype(out_dtype)
            return
        acc, k = refs[-1], pl.program_id(2)

        @pl.when(k == 0)
        def _():
            acc[...] = part

        @pl.when(k > 0)
        def _():
            acc[...] += part

        @pl.when(k == nk - 1)
        def _():
            r = acc[...]
            if res is not None:
                r = r + r_ref[...]
            o_ref[...] = r.astype(out_dtype)

    if a_spec is None:
        a_spec = pl.BlockSpec((tk, tm), lambda i, j, k: (k, i)) if mode == "tn" else pl.BlockSpec((tm, tk), lambda i, j, k: (i, k))
    if b_spec is None:
        b_spec = pl.BlockSpec((tn, tk), lambda i, j, k: (j, k)) if mode == "nt" else pl.BlockSpec((tk, tn), lambda i, j, k: (k, j))
    in_specs, args = [a_spec, b_spec], [a, b]
    if res is not None:
        in_specs.append(pl.BlockSpec((tm, tn), lambda i, j, k: (i, j)))
        args.append(res)
    out_spec = out_spec or pl.BlockSpec((tm, tn), lambda i, j, k: (i, j))
    grid = (M // tm, N // tn, nk)
    if n_outer:
        swap = lambda s: pl.BlockSpec(s.block_shape, lambda j, i, k, f=s.index_map: f(i, j, k))
        in_specs, out_spec, grid = [swap(s) for s in in_specs], swap(out_spec), (N // tn, M // tm, nk)
    return pl.pallas_call(
        body, name=name, grid=grid, in_specs=in_specs, out_specs=out_spec,
        out_shape=out_shape or jax.ShapeDtypeStruct((M, N), out_dtype),
        scratch_shapes=[pltpu.VMEM((tm, tn), F32)] if nk > 1 else [],
        compiler_params=_cp("parallel", "parallel", "arbitrary"),
    )(*args)


def _rms(x, w):
    return x * lax.rsqrt(jnp.mean(x * x, axis=-1, keepdims=True) + EPS) * w


def _rmsnorm_fwd(x, w, *, name, tm=512):
    def body(x_ref, w_ref, o_ref):
        o_ref[...] = _rms(x_ref[...], w_ref[...]).astype(BF16)

    return pl.pallas_call(
        body, name=name, grid=(x.shape[0] // tm,),
        in_specs=[pl.BlockSpec((tm, D), lambda i: (i, 0)), pl.BlockSpec((1, D), lambda i: (0, 0))],
        out_specs=pl.BlockSpec((tm, D), lambda i: (i, 0)),
        out_shape=jax.ShapeDtypeStruct(x.shape, BF16), compiler_params=_cp("parallel"),
    )(x, w)


def _rmsnorm_proj(x, w, w_all, *, name, tm=512):
    def body(x_ref, w_ref, b_ref, h_ref, p_ref, ba_ref):
        h = _rms(x_ref[...], w_ref[...]).astype(BF16)
        h_ref[...] = h
        r = jnp.dot(h, b_ref[...], preferred_element_type=F32)
        p_ref[...] = r[:, :N_MAIN].astype(BF16)
        ba_ref[...] = r[:, N_MAIN:]

    row = lambda n: pl.BlockSpec((tm, n), lambda i: (i, 0))
    return pl.pallas_call(
        body, name=name, grid=(x.shape[0] // tm,),
        in_specs=[row(D), pl.BlockSpec((1, D), lambda i: (0, 0)), pl.BlockSpec((D, N_PROJ), lambda i: (0, 0))],
        out_specs=[row(D), row(N_MAIN), row(N_PROJ - N_MAIN)],
        out_shape=[jax.ShapeDtypeStruct((x.shape[0], D), BF16), jax.ShapeDtypeStruct((x.shape[0], N_MAIN), BF16),
                   jax.ShapeDtypeStruct((x.shape[0], N_PROJ - N_MAIN), F32)],
        compiler_params=_cp("parallel"),
    )(x, w, w_all)


def _mm_rmsnorm_bwd(a, b, x, w, dres, *, name, K, tm, tk, a_spec=None, b_spec=None):
    M = x.shape[0]
    nk = K // tk
    assert M % tm == 0 and K % tk == 0, (name, M, K)

    def body(a_ref, b_ref, x_ref, w_ref, dr_ref, dx_ref, dw_ref, acc):
        i, k = pl.program_id(0), pl.program_id(1)
        part = lax.dot_general(a_ref[...].astype(BF16), b_ref[...].astype(BF16), _MM_DN["nt"], preferred_element_type=F32)

        @pl.when(k == 0)
        def _():
            acc[...] = part

        @pl.when(k > 0)
        def _():
            acc[...] += part

        @pl.when((i == 0) & (k == 0))
        def _():
            dw_ref[...] = jnp.zeros_like(dw_ref)

        @pl.when(k == nk - 1)
        def _():
            _, vjp = jax.vjp(_rms, x_ref[...], w_ref[...])
            dx, dw = vjp(acc[...])
            dx_ref[...] = dx + dr_ref[...]
            dw_ref[...] += dw

    row = pl.BlockSpec((tm, D), lambda i, k: (i, 0))
    vec = pl.BlockSpec((1, D), lambda i, k: (0, 0))
    return pl.pallas_call(
        body, name=name, grid=(M // tm, nk),
        in_specs=[a_spec or pl.BlockSpec((tm, tk), lambda i, k: (i, k)), b_spec or pl.BlockSpec((D, tk), lambda i, k: (0, k)),
                  row, vec, row],
        out_specs=[row, vec], out_shape=[jax.ShapeDtypeStruct((M, D), F32), jax.ShapeDtypeStruct((1, D), F32)],
        scratch_shapes=[pltpu.VMEM((tm, D), F32)], compiler_params=_cp("arbitrary", "arbitrary"),
    )(a, b, x, w, dres)


def _rmsnorm_bwd(x, w, dh, dres, *, name, tm=512):
    def body(x_ref, w_ref, dh_ref, dr_ref, dx_ref, dw_ref):
        _, vjp = jax.vjp(_rms, x_ref[...], w_ref[...])
        dx, dw = vjp(dh_ref[...])
        dx_ref[...] = dx + dr_ref[...]

        @pl.when(pl.program_id(0) == 0)
        def _():
            dw_ref[...] = jnp.zeros_like(dw_ref)

        dw_ref[...] += dw

    row = pl.BlockSpec((tm, D), lambda i: (i, 0))
    vec = pl.BlockSpec((1, D), lambda i: (0, 0))
    return pl.pallas_call(
        body, name=name, grid=(x.shape[0] // tm,), in_specs=[row, vec, row, row], out_specs=[row, vec],
        out_shape=[jax.ShapeDtypeStruct(x.shape, F32), jax.ShapeDtypeStruct((1, D), F32)],
        compiler_params=_cp("arbitrary"),
    )(x, w, dh, dres)


def _loss_head(x, w, tgt, *, name, tm=512):
    def f(xv, wv, tv):
        err = _rms(xv, wv) - tv
        per_row = jnp.sum(err * err, axis=1, keepdims=True) * (0.5 / D)
        return jnp.sum(per_row, axis=0, keepdims=True)

    def body(x_ref, w_ref, t_ref, dx_ref, dw_ref, loss_ref):
        tv = t_ref[...]
        loss, vjp = jax.vjp(lambda xv, wv: f(xv, wv, tv), x_ref[...], w_ref[...])
        dx, dw = vjp(jnp.ones((1, 1), F32))
        dx_ref[...] = dx

        @pl.when(pl.program_id(0) == 0)
        def _():
            dw_ref[...] = jnp.zeros_like(dw_ref)
            loss_ref[...] = jnp.zeros_like(loss_ref)

        dw_ref[...] += dw
        loss_ref[...] += jnp.broadcast_to(loss, loss_ref.shape)

    row = pl.BlockSpec((tm, D), lambda i: (i, 0))
    vec = pl.BlockSpec((1, D), lambda i: (0, 0))
    return pl.pallas_call(
        body, name=name, grid=(x.shape[0] // tm,), in_specs=[row, vec, row],
        out_specs=[row, vec, pl.BlockSpec((1, 128), lambda i: (0, 0))],
        out_shape=[jax.ShapeDtypeStruct(x.shape, F32), jax.ShapeDtypeStruct((1, D), F32),
                   jax.ShapeDtypeStruct((1, 128), F32)],
        compiler_params=_cp("arbitrary"),
    )(x, w, tgt)


def _swa_bias():
    G = HQ // HKV
    r, c = np.arange(G * WIN)[:, None], np.arange(2 * WIN)[None, :]
    rel = (r % WIN) + WIN - c
    band = (rel >= 0) & (rel < WIN)
    out = np.empty((2, HKV, G * WIN, 2 * WIN), np.float32)
    for h2 in range(HKV):
        slope = 2.0 ** (-8.0 * (h2 * G + r // WIN + 1) / HQ)
        out[0, h2] = np.where(band & (c >= WIN), -slope * rel, -1e30)
        out[1, h2] = np.where(band, -slope * rel, -1e30)
    return jnp.asarray(out)


def _swa_heads(qs, k2s, v2s, sk, biases, hqs=range(HQ)):
    first_kv = hqs[0] // (HQ // HKV)
    kv = lambda xs: [xs[hq // (HQ // HKV) - first_kv] for hq in hqs]
    ss = _hmap(lambda q, k2, b: _dot_nt(q, k2) * (DH ** -0.5) + b, qs, kv(k2s), biases)
    sinks = [_col(sk, hq) for hq in hqs]
    ms = _hmap(lambda s, sink: lax.stop_gradient(jnp.maximum(jnp.max(s, axis=1, keepdims=True), sink)), ss, sinks)
    ps = _hmap(lambda s, m: jnp.exp(s - m), ss, ms)
    dens = _hmap(lambda p, sink, m: jnp.sum(p, axis=1, keepdims=True) + jnp.exp(sink - m), ps, sinks, ms)
    return _hmap(lambda p, den, v2: _dot(p * (1.0 / den), v2), ps, dens, kv(v2s))


def _swa_load(q_ref, kp_ref, kc_ref, vp_ref, vc_ref, bias_ref):
    q, G = q_ref[...].astype(F32), HQ // HKV
    qs = [q[:, hq * DH:(hq + 1) * DH] for hq in range(HQ)]
    both = lambda a, b, h2: jnp.concatenate([a[:, h2 * DH:(h2 + 1) * DH], b[:, h2 * DH:(h2 + 1) * DH]], axis=0).astype(F32)
    k2s = [both(kp_ref, kc_ref, h2) for h2 in range(HKV)]
    v2s = [both(vp_ref, vc_ref, h2) for h2 in range(HKV)]
    biases = [bias_ref[hq // G, (hq % G) * WIN:(hq % G + 1) * WIN, :] for hq in range(HQ)]
    return qs, k2s, v2s, biases


def _swa_specs():
    prev = lambda i: jnp.maximum(jnp.minimum(i, T // WIN - 1) - 1, 0)
    cur = lambda i: jnp.minimum(i, T // WIN - 1)
    return [
        pl.BlockSpec((WIN, HQ * DH), lambda i: (cur(i), 0)),
        pl.BlockSpec((WIN, 128), lambda i: (prev(i), COL_K)),
        pl.BlockSpec((WIN, 128), lambda i: (cur(i), COL_K)),
        pl.BlockSpec((WIN, 128), lambda i: (prev(i), COL_V)),
        pl.BlockSpec((WIN, 128), lambda i: (cur(i), COL_V)),
        pl.BlockSpec((1, 128), lambda i: (0, 0)),
        pl.BlockSpec((None, HKV, (HQ // HKV) * WIN, 2 * WIN), lambda i: (jnp.minimum(i, 1), 0, 0, 0)),
    ]


def _swa_fwd(proj, sinks, *, name):
    def body(q_ref, kp_ref, kc_ref, vp_ref, vc_ref, sk_ref, bias_ref, o_ref):
        qs, k2s, v2s, biases = _swa_load(q_ref, kp_ref, kc_ref, vp_ref, vc_ref, bias_ref)
        o_ref[...] = jnp.concatenate(_swa_heads(qs, k2s, v2s, sk_ref[...], biases), axis=1).astype(BF16)

    return pl.pallas_call(
        body, name=name, grid=(T // WIN,), in_specs=_swa_specs(),
        out_specs=pl.BlockSpec((WIN, HQ * DH), lambda i: (i, 0)),
        out_shape=jax.ShapeDtypeStruct((T, HQ * DH + GH * GD), BF16), compiler_params=_cp("parallel"),
    )(proj, proj, proj, proj, proj, sinks, _swa_bias())


def _swa_bwd(proj, sinks, do, dproj, *, name):
    NB = T // WIN
    QW = HQ * DH

    def body(q_ref, kp_ref, kc_ref, vp_ref, vc_ref, sk_ref, bias_ref, do_ref, _, out_ref, dsk_ref, cq, ck, cv):
        i = pl.program_id(0)

        @pl.when(i == 0)
        def _():
            cq[...] = jnp.zeros_like(cq)
            ck[...] = jnp.zeros_like(ck)
            cv[...] = jnp.zeros_like(cv)
            dsk_ref[...] = jnp.zeros_like(dsk_ref)

        @pl.when(i < NB)
        def _():
            qs, k2s, v2s, biases = _swa_load(q_ref, kp_ref, kc_ref, vp_ref, vc_ref, bias_ref)
            dov = do_ref[...].astype(F32)
            dqs, dk2s, dv2s, dsk = [], [], [], jnp.zeros_like(sk_ref[...])
            for h2 in range(HKV):
                hqs = range(h2 * (HQ // HKV), (h2 + 1) * (HQ // HKV))
                _, vjp = jax.vjp(functools.partial(_swa_heads, biases=[biases[hq] for hq in hqs], hqs=hqs),
                                 [qs[hq] for hq in hqs], [k2s[h2]], [v2s[h2]], sk_ref[...])
                dq, dk2, dv2, dsk_h = vjp([dov[:, hq * DH:(hq + 1) * DH] for hq in hqs])
                dqs, dk2s, dv2s, dsk = dqs + dq, dk2s + dk2, dv2s + dv2, dsk + dsk_h
            out_ref[:, :QW] = cq[...].astype(BF16)
            out_ref[:, QW:QW + 128] = (ck[...] + jnp.concatenate([d[:WIN] for d in dk2s], axis=1)).astype(BF16)
            out_ref[:, QW + 128:] = (cv[...] + jnp.concatenate([d[:WIN] for d in dv2s], axis=1)).astype(BF16)
            cq[...] = jnp.concatenate(dqs, axis=1)
            ck[...] = jnp.concatenate([d[WIN:] for d in dk2s], axis=1)
            cv[...] = jnp.concatenate([d[WIN:] for d in dv2s], axis=1)
            dsk_ref[...] += dsk

        @pl.when(i == NB)
        def _():
            out_ref[:, :QW] = cq[...].astype(BF16)
            out_ref[:, QW:QW + 128] = ck[...].astype(BF16)
            out_ref[:, QW + 128:] = cv[...].astype(BF16)

    qblk = pl.BlockSpec((WIN, QW), lambda i: (jnp.minimum(i, NB - 1), 0))
    return pl.pallas_call(
        body, name=name, grid=(NB + 1,), in_specs=_swa_specs() + [qblk, ANY],
        out_specs=[pl.BlockSpec((WIN, QW + 256), lambda i: (jnp.maximum(i - 1, 0), 0)), pl.BlockSpec((1, 128), lambda i: (0, 0))],
        out_shape=[jax.ShapeDtypeStruct(dproj.shape, dproj.dtype), jax.ShapeDtypeStruct((1, 128), F32)],
        scratch_shapes=[pltpu.VMEM((WIN, QW), F32), pltpu.VMEM((WIN, 128), F32), pltpu.VMEM((WIN, 128), F32)],
        input_output_aliases={8: 0}, compiler_params=_cp("arbitrary"),
    )(proj, proj, proj, proj, proj, sinks, _swa_bias(), do, dproj)


RC = 256
HALO = 8


def _load_ext(ref, c):
    nch = T // RC
    r0 = pl.multiple_of(c * RC, RC)
    p0 = pl.multiple_of(jnp.maximum(r0 - HALO, 0), HALO)
    n0 = pl.multiple_of(jnp.minimum(r0 + RC, T - HALO), HALO)
    prev = jnp.where(c > 0, ref[pl.ds(p0, HALO), :].astype(F32), 0.0)
    nxt = jnp.where(c < nch - 1, ref[pl.ds(n0, HALO), :].astype(F32), 0.0)
    return jnp.concatenate([prev, ref[pl.ds(r0, RC), :].astype(F32), nxt], axis=0)


def _conv_ext(xe, w, K):
    y = w[K - 1:K, :] * xe
    for s in range(1, K):
        y = y + w[K - 1 - s:K - s, :] * pltpu.roll(xe, s, 0)
    return y


def _conv_bwd_ext(xe, dye, w, K, dw_ref):
    n = xe.shape[0]
    own = slice(HALO, HALO + RC)
    dx = w[K - 1:K, :] * dye
    dw_ref[K - 1:K, :] += jnp.sum(dye[own] * xe[own], axis=0, keepdims=True)
    for s in range(1, K):
        dx = dx + w[K - 1 - s:K - s, :] * pltpu.roll(dye, n - s, 0)
        dw_ref[K - 1 - s:K - s, :] += jnp.sum(dye[own] * pltpu.roll(xe, s, 0)[own], axis=0, keepdims=True)
    return dx[own]


def _gdn_post_conv(y, is_qk):
    a = _silu(y)
    nrm = a * lax.rsqrt(jnp.sum(a * a, axis=1, keepdims=True) + EPS)
    return jnp.where(is_qk, nrm, a)


def _gdn_prep_fwd(proj, conv_w, *, name):
    nblk = 3 * GH

    def body(x_ref, w_ref, o_ref):
        is_qk = pl.program_id(0) < 2 * GH
        w = w_ref[...]

        def chunk(c, carry):
            y = _conv_ext(_load_ext(x_ref, c), w, GK)[HALO:HALO + RC]
            o_ref[pl.ds(pl.multiple_of(c * RC, RC), RC), :] = _gdn_post_conv(y, is_qk)
            return carry

        lax.fori_loop(0, T // RC, chunk, 0)

    return pl.pallas_call(
        body, name=name, grid=(nblk,),
        in_specs=[pl.BlockSpec((T, 128), lambda j: (0, COL_G + j)), pl.BlockSpec((GK, 128), lambda j: (0, j))],
        out_specs=pl.BlockSpec((T, 128), lambda j: (0, j)),
        out_shape=jax.ShapeDtypeStruct((T, nblk * 128), F32), compiler_params=_cp("parallel"),
    )(proj, conv_w)


def _gdn_prep_bwd(proj, conv_w, dout, dproj, *, name):
    nblk = 3 * GH

    def body(x_ref, w_ref, d_ref, _, dx_ref, dw_ref):
        is_qk = pl.program_id(0) < 2 * GH
        w = w_ref[...]
        dw_ref[...] = jnp.zeros_like(dw_ref)

        def chunk(c, carry):
            xe = _load_ext(x_ref, c)
            _, vjp = jax.vjp(lambda y: _gdn_post_conv(y, is_qk), _conv_ext(xe, w, GK))
            (dye,) = vjp(_load_ext(d_ref, c))
            dx_ref[pl.ds(pl.multiple_of(c * RC, RC), RC), :] = _conv_bwd_ext(xe, dye, w, GK, dw_ref).astype(BF16)
            return carry

        lax.fori_loop(0, T // RC, chunk, 0)

    return pl.pallas_call(
        body, name=name, grid=(nblk,),
        in_specs=[pl.BlockSpec((T, 128), lambda j: (0, COL_G + j)), pl.BlockSpec((GK, 128), lambda j: (0, j)),
                  pl.BlockSpec((T, 128), lambda j: (0, j)), ANY],
        out_specs=[pl.BlockSpec((T, 128), lambda j: (0, COL_G + j)), pl.BlockSpec((GK, 128), lambda j: (0, j))],
        out_shape=[jax.ShapeDtypeStruct(dproj.shape, dproj.dtype), jax.ShapeDtypeStruct((GK, nblk * 128), F32)],
        input_output_aliases={3: 0}, compiler_params=_cp("parallel"),
    )(proj, conv_w, dout, dproj)


def _ffn_post_conv(y, b, up):
    return _silu(y + b) * up


def _ffn_act_fwd(gu, conv_w, conv_b, *, name, carry=None):
    nblk = DFF // 128

    def body(g_ref, u_ref, w_ref, b_ref, o_ref):
        w, b = w_ref[...], b_ref[...]

        def chunk(c, carry):
            rows = pl.ds(pl.multiple_of(c * RC, RC), RC)
            y = _conv_ext(_load_ext(g_ref, c), w, FK)[HALO:HALO + RC]
            o_ref[rows, :] = _ffn_post_conv(y, b, u_ref[rows, :].astype(F32)).astype(BF16)
            return carry

        lax.fori_loop(0, T // RC, chunk, 0)

    (act,), carried = _seq_call(
        body, carry, name=name, steps=nblk,
        in_specs=[pl.BlockSpec((T, 128), lambda j: (0, j)), pl.BlockSpec((T, 128), lambda j: (0, nblk + j)),
                  pl.BlockSpec((FK, 128), lambda j: (0, j)), pl.BlockSpec((1, 128), lambda j: (0, j))],
        out_specs=[pl.BlockSpec((T, 128), lambda j: (0, j))], out_shape=[jax.ShapeDtypeStruct((T, DFF), BF16)],
        args=(gu, gu, conv_w, conv_b))
    return act, carried


def _ffn_act_bwd(gu, conv_w, conv_b, dact, *, name):
    nblk = DFF // 128

    def body(g_ref, u_ref, w_ref, b_ref, d_ref, dgu_ref, dw_ref, db_ref):
        dg_ref, du_ref = dgu_ref.at[0], dgu_ref.at[1]
        w, b = w_ref[...], b_ref[...]
        dw_ref[...] = jnp.zeros_like(dw_ref)
        db_ref[...] = jnp.zeros_like(db_ref)

        def chunk(c, carry):
            rows = pl.ds(pl.multiple_of(c * RC, RC), RC)
            xe = _load_ext(g_ref, c)
            ue = _load_ext(u_ref, c)
            _, vjp = jax.vjp(_ffn_post_conv, _conv_ext(xe, w, FK), b, ue)
            dye, db, due = vjp(_load_ext(d_ref, c))
            du_ref[rows, :] = due[HALO:HALO + RC].astype(BF16)
            db_ref[...] += jnp.sum(dye[HALO:HALO + RC], axis=0, keepdims=True)
            dg_ref[rows, :] = _conv_bwd_ext(xe, dye, w, FK, dw_ref).astype(BF16)
            return carry

        lax.fori_loop(0, T // RC, chunk, 0)

    col = pl.BlockSpec((T, 128), lambda j: (0, j))
    return pl.pallas_call(
        body, name=name, grid=(nblk,),
        in_specs=[col, pl.BlockSpec((T, 128), lambda j: (0, nblk + j)), pl.BlockSpec((FK, 128), lambda j: (0, j)),
                  pl.BlockSpec((1, 128), lambda j: (0, j)), col],
        out_specs=[pl.BlockSpec((2, T, 128), lambda j: (0, 0, j)), pl.BlockSpec((FK, 128), lambda j: (0, j)),
                   pl.BlockSpec((1, 128), lambda j: (0, j))],
        out_shape=[jax.ShapeDtypeStruct((2, T, DFF), BF16), jax.ShapeDtypeStruct((FK, DFF), F32),
                   jax.ShapeDtypeStruct((1, DFF), F32)],
        compiler_params=_cp("parallel"),
    )(gu, gu, conv_w, conv_b, dact)


def _gdn_gates(ba, alog, dtb):
    beta = jax.nn.sigmoid(ba)
    g = -jnp.exp(alog) * jax.nn.softplus(ba + dtb)
    tril = (_iota((GC, GC), 0) >= _iota((GC, GC), 1)).astype(F32)
    return beta, jnp.dot(tril, g, precision=HIGHEST, preferred_element_type=F32)


def _hmap(f, *lists):
    return [f(*xs) for xs in zip(*lists)]


def _gdn_cols(beta_all, gc_all):
    return [_col(beta_all, h) for h in range(GH)], [_col(gc_all, GH + h) for h in range(GH)]


def _gdn_decay(Gcs):
    r, c = _iota((GC, GC), 0), _iota((GC, GC), 1)
    eye, ones = (r == c).astype(F32), jnp.ones((GC, GC), F32)
    grows = _hmap(lambda G: jnp.dot(ones, eye * G, precision=HIGHEST, preferred_element_type=F32), Gcs)
    return _hmap(lambda G, grow: jnp.exp(jnp.where(r >= c, G - grow, -1e30)), Gcs, grows)


def _gdn_pre(beta_all, gc_all):
    betas, Gcs = _gdn_cols(beta_all, gc_all)
    return betas, Gcs, _gdn_decay(Gcs)


def _gdn_A(ks, betas, decays):
    strict = _iota((GC, GC), 0) > _iota((GC, GC), 1)
    kk = _hmap(lambda k, b: _dot_nt(k * b, k), ks, betas)
    return _hmap(lambda a, d: jnp.where(strict, a * d, 0.0), kk, decays)


def _tri_inv(As):
    eye = (_iota((GC, GC), 0) == _iota((GC, GC), 1)).astype(F32)
    Tms, Ps = [eye - A for A in As], As
    for _ in range(GC.bit_length() - 2):
        Ps = _hmap(lambda P: _dot(P, P), Ps)
        Tms = _hmap(lambda Tm, P: Tm + _dot(Tm, P), Tms, Ps)
    return Tms


def _gdn_chunk_pre(qs, ks, vs, betas, Gcs, decays, Tms):
    eGs = _hmap(jnp.exp, Gcs)
    us = _hmap(lambda Tm, v, b: _dot(Tm, v * b), Tms, vs, betas)
    ws = _hmap(lambda Tm, k, b, eG: _dot(Tm, k * b * eG), Tms, ks, betas, eGs)
    qss = [q * (GD ** -0.5) for q in qs]
    qks = _hmap(lambda q, k, d: _dot_nt(q, k) * d, qss, ks, decays)
    glasts = [jnp.sum(jnp.where(_iota(G.shape, 0) == GC - 1, G, 0.0), axis=0, keepdims=True) for G in Gcs]
    kds = _hmap(lambda k, gl, G: k * jnp.exp(gl - G), ks, glasts, Gcs)
    qes = _hmap(lambda q, eG: q * eG, qss, eGs)
    return us, ws, qks, kds, qes, glasts


def _gdn_chunk_state(us, ws, qks, kds, qes, glasts, Ss):
    v_news = _hmap(lambda u, w, S: u - _dot(w, S), us, ws, Ss)
    qS = _hmap(_dot, qes, Ss)
    os = _hmap(lambda a, qk, vn: a + _dot(qk, vn), qS, qks, v_news)
    S_news = _hmap(lambda S, gl, kd, vn: S * jnp.exp(gl) + _dot_tn(kd, vn), Ss, glasts, kds, v_news)
    return os, S_news


def _gdn_chunk(qs, ks, vs, betas, Gcs, decays, Ss, Tms):
    return _gdn_chunk_state(*_gdn_chunk_pre(qs, ks, vs, betas, Gcs, decays, Tms), Ss)


GDN_BWD_CPS = 2
GDN_CPS = 4


def _gdn_chunk_fwd(qkv_c, ba, alog, dtb, *, name, carry=None):
    W3, CPS = 3 * GH * GD, GDN_CPS

    def body(x_ref, ba_ref, al_ref, dt_ref, o_ref, s_ref, t_ref, S):
        @pl.when(pl.program_id(0) == 0)
        def _():
            S[...] = jnp.zeros_like(S)

        betas, Gcs, decays, qkv = [], [], [], ([], [], [])
        for sub in range(CPS):
            rows = slice(sub * GC, (sub + 1) * GC)
            b, g, d = _gdn_pre(*_gdn_gates(ba_ref[rows, :], al_ref[...], dt_ref[...]))
            betas, Gcs, decays = betas + b, Gcs + g, decays + d
            for p in range(3):
                qkv[p].extend(x_ref[rows, (p * GH + h) * GD:(p * GH + h + 1) * GD] for h in range(GH))
        qs, ks, vs = qkv
        Tms = _tri_inv(_gdn_A(ks, betas, decays))
        pre = _gdn_chunk_pre(qs, ks, vs, betas, Gcs, decays, Tms)
        Ss = [S[h] for h in range(GH)]
        for sub in range(CPS):
            mine = slice(sub * GH, (sub + 1) * GH)
            os, S_news = _gdn_chunk_state(*[part[mine] for part in pre], Ss)
            for h in range(GH):
                s_ref[sub, h] = Ss[h]
                t_ref[sub, h] = Tms[sub * GH + h]
                o_ref[sub * GC:(sub + 1) * GC, h * GD:(h + 1) * GD] = os[h]
            Ss = S_news
        for h in range(GH):
            S[h] = Ss[h]

    vec = pl.BlockSpec((1, 128), lambda n: (0, 0))
    return _seq_call(
        body, carry, name=name, steps=N_CHUNK // CPS,
        in_specs=[pl.BlockSpec((CPS * GC, W3), lambda n: (n, 0)), pl.BlockSpec((CPS * GC, 128), lambda n: (n, 0)), vec, vec],
        out_specs=[pl.BlockSpec((CPS * GC, GH * GD), lambda n: (n, 0)),
                   pl.BlockSpec((CPS, GH, GD, GD), lambda n: (n, 0, 0, 0)),
                   pl.BlockSpec((CPS, GH, GC, GC), lambda n: (n, 0, 0, 0))],
        out_shape=[jax.ShapeDtypeStruct((T, GH * GD), F32), jax.ShapeDtypeStruct((N_CHUNK, GH, GD, GD), F32),
                   jax.ShapeDtypeStruct((N_CHUNK, GH, GC, GC), F32)],
        scratch_shapes=[pltpu.VMEM((GH, GD, GD), F32)], args=(qkv_c, ba, alog, dtb))


def _gdn_chunk_bwd(qkv_c, ba, alog, dtb, s_all, t_all, do, dproj, *, name, carry=None):
    W3, CPS = 3 * GH * GD, GDN_BWD_CPS
    steps = N_CHUNK // CPS
    rev = lambda n: steps - 1 - n

    def body(x_ref, ba_ref, al_ref, dt_ref, s_ref, t_ref, do_ref, _, dx_ref, dba_ref, dal_ref, ddt_ref, dS):
        @pl.when(pl.program_id(0) == 0)
        def _():
            dS[...] = jnp.zeros_like(dS)
            dal_ref[...] = jnp.zeros_like(dal_ref)
            ddt_ref[...] = jnp.zeros_like(ddt_ref)

        rows = lambda sub: slice(sub * GC, (sub + 1) * GC)
        gates = [jax.vjp(_gdn_gates, ba_ref[rows(sub), :], al_ref[...], dt_ref[...]) for sub in range(CPS)]

        def pre_all(beta_alls, gc_alls):
            betas, Gcs, decays = [], [], []
            for b_all, g_all in zip(beta_alls, gc_alls):
                b, g, d = _gdn_pre(b_all, g_all)
                betas, Gcs, decays = betas + b, Gcs + g, decays + d
            return betas, Gcs, decays

        (betas, Gcs, decays), vjp_pre = jax.vjp(pre_all, [g[0][0] for g in gates], [g[0][1] for g in gates])
        qs, ks, vs = ([x_ref[rows(sub), (p * GH + h) * GD:(p * GH + h + 1) * GD] for sub in range(CPS) for h in range(GH)]
                      for p in range(3))
        Tms = [t_ref[sub, h] for sub in range(CPS) for h in range(GH)]
        pre, vjp_p = jax.vjp(_gdn_chunk_pre, qs, ks, vs, betas, Gcs, decays, Tms)
        dSs = [dS[h] for h in range(GH)]
        dpre = [[None] * (CPS * GH) for _ in pre]
        for sub in reversed(range(CPS)):
            mine = slice(sub * GH, (sub + 1) * GH)
            Ss = [s_ref[sub, h] for h in range(GH)]
            dos = [do_ref[rows(sub), h * GD:(h + 1) * GD] for h in range(GH)]
            _, vjp_s = jax.vjp(_gdn_chunk_state, *[part[mine] for part in pre], Ss)
            *dparts, dSs = vjp_s((dos, dSs))
            for full, part in zip(dpre, dparts):
                full[mine] = part
        for h in range(GH):
            dS[h] = dSs[h]
        dqs, dks, dvs, dbetas, dGcs, ddecays, dTs = vjp_p(tuple(dpre))
        dXs = _hmap(_dot_nt, dTs, Tms)
        dAs = _hmap(lambda Tm, dX: -_dot_tn(Tm, dX), Tms, dXs)
        _, vjp_a = jax.vjp(_gdn_A, ks, betas, decays)
        dks2, dbetas2, ddecays2 = vjp_a(dAs)
        add = lambda a, b: _hmap(jnp.add, a, b)
        db_alls, dg_alls = vjp_pre((add(dbetas, dbetas2), dGcs, add(ddecays, ddecays2)))
        for sub in range(CPS):
            for h in range(GH):
                i = sub * GH + h
                dx_ref[rows(sub), h * GD:(h + 1) * GD] = dqs[i]
                dx_ref[rows(sub), (GH + h) * GD:(GH + h + 1) * GD] = dks[i] + dks2[i]
                dx_ref[rows(sub), (2 * GH + h) * GD:(2 * GH + h + 1) * GD] = dvs[i]
            dba, dal, ddt = gates[sub][1]((db_alls[sub], dg_alls[sub]))
            dba_ref[rows(sub), :] = dba.astype(BF16)
            dal_ref[...] += dal
            ddt_ref[...] += ddt

    vec = pl.BlockSpec((1, 128), lambda n: (0, 0))
    return _seq_call(
        body, carry, name=name, steps=steps,
        in_specs=[pl.BlockSpec((CPS * GC, W3), lambda n: (rev(n), 0)), pl.BlockSpec((CPS * GC, 128), lambda n: (rev(n), 0)),
                  vec, vec, pl.BlockSpec((CPS, GH, GD, GD), lambda n: (rev(n), 0, 0, 0)),
                  pl.BlockSpec((CPS, GH, GC, GC), lambda n: (rev(n), 0, 0, 0)),
                  pl.BlockSpec((CPS * GC, GH * GD), lambda n: (rev(n), 0)),
                  ANY],
        out_specs=[pl.BlockSpec((CPS * GC, W3), lambda n: (rev(n), 0)),
                   pl.BlockSpec((CPS * GC, 128), lambda n: (rev(n), COL_BA)), vec, vec],
        out_shape=[jax.ShapeDtypeStruct((T, W3), F32), jax.ShapeDtypeStruct(dproj.shape, dproj.dtype),
                   jax.ShapeDtypeStruct((1, 128), F32), jax.ShapeDtypeStruct((1, 128), F32)],
        scratch_shapes=[pltpu.VMEM((GH, GD, GD), F32)], aliases={7: 1},
        args=(qkv_c, ba, alog, dtb, s_all, t_all, do, dproj))


def _gdn_post(o, z, nw):
    return o * lax.rsqrt(jnp.mean(o * o, axis=-1, keepdims=True) + EPS) * nw * _silu(z)


def _gdn_post_fwd(o_raw, proj, nw, mixed, *, name, tm=512):
    def body(o_ref, z_ref, w_ref, _, out_ref):
        out_ref[...] = _gdn_post(o_ref[...], z_ref[...].astype(F32), w_ref[...]).astype(BF16)

    return pl.pallas_call(
        body, name=name, grid=(T // tm, GH),
        in_specs=[pl.BlockSpec((tm, GD), lambda i, h: (i, h)), pl.BlockSpec((tm, GD), lambda i, h: (i, COL_Z + h)),
                  pl.BlockSpec((1, GD), lambda i, h: (0, 0)), ANY],
        out_specs=pl.BlockSpec((tm, GD), lambda i, h: (i, HQ * DH // GD + h)),
        out_shape=jax.ShapeDtypeStruct(mixed.shape, mixed.dtype), input_output_aliases={3: 0},
        compiler_params=_cp("parallel", "parallel"),
    )(o_raw, proj, nw, mixed)


def _gdn_post_bwd(o_raw, proj, nw, dmixed, *, name, tm=512):
    def body(o_ref, z_ref, w_ref, d_ref, do_ref, dz_ref, dw_ref):
        _, vjp = jax.vjp(_gdn_post, o_ref[...], z_ref[...].astype(F32), w_ref[...])
        do, dz, dw = vjp(d_ref[...])
        do_ref[...] = do
        dz_ref[...] = dz.astype(BF16)

        @pl.when((pl.program_id(0) == 0) & (pl.program_id(1) == 0))
        def _():
            dw_ref[...] = jnp.zeros_like(dw_ref)

        dw_ref[...] += dw

    blk = pl.BlockSpec((tm, GD), lambda i, h: (i, h))
    vec = pl.BlockSpec((1, GD), lambda i, h: (0, 0))
    return pl.pallas_call(
        body, name=name, grid=(T // tm, GH),
        in_specs=[blk, pl.BlockSpec((tm, GD), lambda i, h: (i, COL_Z + h)), vec,
                  pl.BlockSpec((tm, GD), lambda i, h: (i, GH + h))],
        out_specs=[blk, pl.BlockSpec((tm, GD), lambda i, h: (i, COL_Z + h)), vec],
        out_shape=[jax.ShapeDtypeStruct((T, GH * GD), F32), jax.ShapeDtypeStruct((T, N_PROJ), BF16),
                   jax.ShapeDtypeStruct((1, GD), F32)],
        compiler_params=_cp("arbitrary", "arbitrary"),
    )(o_raw, proj, nw, dmixed)


def _adamw(w, g, m, v, *, name):
    shape = w.shape
    cols = shape[-1]
    w2, g2, m2, v2 = (a.reshape(-1, cols) for a in (w, g, m, v))
    rows = w2.shape[0]
    tr = next((t for t in (512, 256, 128, 64, 32, 16, 8) if rows % t == 0), rows)

    def body(w_ref, g_ref, m_ref, v_ref, d_ref, nm_ref, nv_ref):
        gv = g_ref[...]
        nm = B1 * m_ref[...] + (1.0 - B1) * gv
        nv = B2 * v_ref[...] + (1.0 - B2) * jnp.square(gv)
        m_hat = nm / (1.0 - B1 ** STEP)
        v_hat = nv / (1.0 - B2 ** STEP)
        d_ref[...] = -LR * (m_hat / (jnp.sqrt(v_hat) + AEPS) + WD * w_ref[...])
        nm_ref[...] = nm
        nv_ref[...] = nv

    blk = pl.BlockSpec((tr, cols), lambda i: (i, 0))
    out = pl.pallas_call(
        body, name=name, grid=(rows // tr,), in_specs=[blk] * 4, out_specs=[blk] * 3,
        out_shape=[jax.ShapeDtypeStruct((rows, cols), F32)] * 3, compiler_params=_cp("parallel"),
    )(w2, g2, m2, v2)
    return tuple(o.reshape(shape) for o in out)


def _layer_weights(w_in):
    return jnp.pad(w_in, ((0, 0), (0, N_PROJ - w_in.shape[1])))


def _layer_params(attn_norm, sinks, gcw, a_log, dt_bias, gnw, ffn_norm, fcw, fcb):
    lanes4 = lambda v: jnp.pad(v, (GH, 128 - 2 * GH))[None]
    return dict(attn_norm=attn_norm[None], sinks=jnp.pad(sinks, (0, 128 - HQ))[None], gcw=gcw, alog=lanes4(a_log),
                dtb=lanes4(dt_bias), gnw=gnw[None], ffn_norm=ffn_norm[None], fcw=fcw, fcb=fcb[None])


def _mixer_fwd(x, W, P, l, carry=None):
    n = lambda s: f"l{l}_{s}"
    h, proj, ba = _rmsnorm_proj(x, P["attn_norm"], W["all"], name=n("norm1_proj"))
    mixed = _swa_fwd(proj, P["sinks"], name=n("swa"))
    qkv_c = _gdn_prep_fwd(proj, P["gcw"], name=n("gdn_prep"))
    (o_raw, s_all, t_all), carried = _gdn_chunk_fwd(qkv_c, ba, P["alog"], P["dtb"], name=n("gdn_chunk"), carry=carry)
    mixed = _gdn_post_fwd(o_raw, proj, P["gnw"], mixed, name=n("gdn_post"))
    x1 = _mm(mixed, W["out"], res=x, name=n("out_proj"), mode="nn", dims=(T, D, D), tm=512, tn=D, tk=D)
    saved = dict(x=x, h=h, proj=proj, ba=ba, qkv_c=qkv_c, o_raw=o_raw, s_all=s_all, t_all=t_all, mixed=mixed, x1=x1)
    return x1, saved, carried


def _ffn_fwd(x1, W, P, l, carry=None):
    n = lambda s: f"l{l}_{s}"
    h2 = _rmsnorm_fwd(x1, P["ffn_norm"], name=n("norm2"))
    gu = _mm(h2, W["ffn"], name=n("ffn_in"), mode="nn", dims=(T, 2 * DFF, D), tm=512, tn=FFN_CW, tk=D, out_dtype=BF16,
             b_spec=pl.BlockSpec((None, D, FFN_CW), lambda i, j, k: (j, k, 0)), n_outer=True)
    act, carried = _ffn_act_fwd(gu, P["fcw"], P["fcb"], name=n("ffn_act"), carry=carry)
    x2 = _mm(act, W["down"], res=x1, name=n("ffn_down"), mode="nn", dims=(T, D, DFF), tm=512, tn=D, tk=DFF)
    return x2, dict(h2=h2, gu=gu, act=act), carried


def _ffn_bwd(dx2, sv, W, P, l):
    n = lambda s: f"l{l}_{s}"
    CW = FFN_CW
    dact = _mm(dx2, W["down"], name=n("d_act"), mode="nt", dims=(T, DFF, D), tm=512, tn=DFF // 2, tk=D, out_dtype=BF16,
               n_outer=True)
    g_down = _mm(sv["act"], dx2, name=n("g_down"), mode="tn", dims=(DFF, D, T), tm=DFF // 2, tn=D, tk=min(T, 2048), out_dtype=BF16)
    dgu, g_fcw, g_fcb = _ffn_act_bwd(sv["gu"], P["fcw"], P["fcb"], dact, name=n("d_ffn_act"))
    dx1, g_ffn_norm = _mm_rmsnorm_bwd(
        dgu, W["ffn"], sv["x1"], P["ffn_norm"], dx2, name=n("d_h2_norm2"), K=2 * DFF, tm=512, tk=CW,
        a_spec=pl.BlockSpec((None, 512, CW), lambda i, k: (k // 2, i, k % 2)),
        b_spec=pl.BlockSpec((None, D, CW), lambda i, k: (k, 0, 0)))
    g_ffn = _mm(sv["h2"], dgu, name=n("g_ffn"), mode="tn", dims=(D, 2 * DFF, T), tm=512, tn=CW, tk=T, n_outer=True,
                b_spec=pl.BlockSpec((None, T, CW), lambda i, j, k: (j // 2, k, j % 2)),
                out_spec=pl.BlockSpec((None, 512, CW), lambda i, j, k: (j, i, 0)),
                out_shape=jax.ShapeDtypeStruct((N_CHIP, D, CW), BF16), out_dtype=BF16)
    return dx1, dict(ffn_norm=g_ffn_norm[0], w_ffn_in=g_ffn, ffn_conv_w=g_fcw, ffn_conv_b=g_fcb[0], w_down=g_down)


def _mixer_bwd(dx1, sv, W, P, l, carry=None):
    n = lambda s: f"l{l}_{s}"
    dmixed = _mm(dx1, W["out"], name=n("d_mixed"), mode="nt", dims=(T, D, D), tm=512, tn=D, tk=D)
    g_out = _mm(sv["mixed"], dx1, name=n("g_out"), mode="tn", dims=(D, D, T), tm=D, tn=D, tk=min(T, 2048), out_dtype=BF16)
    do_raw, dproj, g_gnw = _gdn_post_bwd(sv["o_raw"], sv["proj"], P["gnw"], dmixed, name=n("d_gdn_post"))
    (dqkv_c, dproj, g_alog, g_dtb), carried = _gdn_chunk_bwd(
        sv["qkv_c"], sv["ba"], P["alog"], P["dtb"], sv["s_all"], sv["t_all"], do_raw, dproj, name=n("d_gdn_chunk"), carry=carry)
    dproj, g_gcw = _gdn_prep_bwd(sv["proj"], P["gcw"], dqkv_c, dproj, name=n("d_gdn_prep"))
    dproj, g_sinks = _swa_bwd(sv["proj"], P["sinks"], dmixed, dproj, name=n("d_swa"))
    dx, g_attn_norm = _mm_rmsnorm_bwd(dproj, W["all"], sv["x"], P["attn_norm"], dx1, name=n("d_h_norm1"), K=N_PROJ,
                                      tm=512, tk=N_PROJ)
    g_all = _mm(sv["h"], dproj, name=n("g_in"), mode="tn", dims=(D, N_PROJ, T), tm=512, tn=N_PROJ, tk=min(T, 2048), out_dtype=BF16)
    grads = dict(attn_norm=g_attn_norm[0], w_in=g_all, attn_sinks=g_sinks[0, :HQ], gdn_conv_w=g_gcw,
                 gdn_a_log=g_alog[0, GH:2 * GH], gdn_dt_bias=g_dtb[0, GH:2 * GH], gdn_norm=g_gnw[0], w_out=g_out)
    return dx, grads, carried


def _pos():
    return lax.axis_index("x"), lax.axis_index("y"), lax.axis_index("c")


def _other_chips(x, y):
    return [(1 - x, y), (x, 1 - y), (1 - x, 1 - y)]


def _remote(src, dst, send_sems, recv_sems, k, to):
    return pltpu.make_async_remote_copy(src_ref=src, dst_ref=dst, send_sem=send_sems.at[k], recv_sem=recv_sems.at[k],
                                        device_id=to, device_id_type=MESH)


def _run_carry(carry, *, name):
    ci, co = len(carry.arrays), len(carry.out_shapes)

    def body(*refs):
        sends, recvs = carry.plan(refs[:ci], refs[ci:ci + co], refs[-2], refs[-1])
        for cp in sends:
            cp.start()
        for cp in recvs:
            cp.wait_recv()
        for cp in sends:
            cp.wait_send()

    return list(pl.pallas_call(
        body, name=name, in_specs=[ANY] * ci, out_specs=[ANY] * co, out_shape=carry.out_shapes,
        scratch_shapes=[pltpu.SemaphoreType.DMA((carry.nsem,)), pltpu.SemaphoreType.DMA((carry.nsem,))],
    )(*carry.arrays))


def _chip_index():
    return 2 * lax.axis_index("x") + lax.axis_index("y")


def _gather_carry(shards):
    def plan(srcs, outs, send_sems, recv_sems):
        x, y, c = _pos()
        chip = 2 * x + y
        others = [(k, px, py, 2 * px + py) for k, (px, py) in enumerate(_other_chips(x, y))]
        sends = [_remote(srcs[t], outs[t].at[chip], send_sems, recv_sems, 3 * t + k, (px, py, c))
                 for t in range(len(srcs)) for k, px, py, _ in others]
        recvs = [_remote(srcs[t], outs[t].at[j], send_sems, recv_sems, 3 * t + k, (x, y, c))
                 for t in range(len(srcs)) for k, _, _, j in others]
        return sends, recvs

    return _Carry(shards, [jax.ShapeDtypeStruct((N_CHIP,) + s.shape, s.dtype) for s in shards], 3 * len(shards), plan)


def _fill_own(outs, shards):
    return [lax.dynamic_update_index_in_dim(o, s, _chip_index(), 0) for o, s in zip(outs, shards)]


def _gather_halves(shards, after, *, name):
    nt = len(shards)

    def body(*refs):
        srcs, outs, send_sems, recv_sems = refs[:nt], refs[nt + 1:2 * nt + 1], refs[-2], refs[-1]
        x, y, c = _pos()
        me, sibling, chip = (x, y, c), (x, y, 1 - c), 2 * x + y
        others = [(k, px, py, 2 * px + py) for k, (px, py) in enumerate(_other_chips(x, y))]

        def rows(t, h):
            rh = srcs[t].shape[0] // 2
            return pl.ds(pl.multiple_of(h * rh, 16), rh)

        first = [_remote(srcs[t].at[rows(t, c), :], outs[t].at[chip, rows(t, c), :], send_sems, recv_sems, 3 * t + k, (px, py, c))
                 for t in range(nt) for k, px, py, _ in others]
        for cp in first:
            cp.start()
        passed = []
        for t in range(nt):
            for k, _, _, j in others:
                slot = outs[t].at[j, rows(t, c), :]
                _remote(slot, slot, send_sems, recv_sems, 3 * t + k, me).wait_recv()
                passed.append(_remote(slot, slot, send_sems, recv_sems, 3 * (nt + t) + k, sibling))
                passed[-1].start()
        for t in range(nt):
            for k, _, _, j in others:
                slot = outs[t].at[j, rows(t, 1 - c), :]
                _remote(slot, slot, send_sems, recv_sems, 3 * (nt + t) + k, me).wait_recv()
        for cp in first + passed:
            cp.wait_send()

    outs = pl.pallas_call(
        body, name=name, in_specs=[ANY] * (nt + 1), out_specs=[ANY] * nt,
        out_shape=[jax.ShapeDtypeStruct((N_CHIP,) + s.shape, s.dtype) for s in shards],
        scratch_shapes=[pltpu.SemaphoreType.DMA((6 * nt,)), pltpu.SemaphoreType.DMA((6 * nt,))],
    )(*shards, after)
    return _fill_own(outs, shards)


HBM_SPEC = pl.BlockSpec(memory_space=pltpu.HBM)
SEM_SPEC = pl.BlockSpec(memory_space=pltpu.SEMAPHORE)
DATAFLOW = pltpu.SideEffectType.DATAFLOW_SIDE_EFFECTING


def _split_start(carry, after, *, name):
    ci, co = len(carry.arrays), len(carry.out_shapes)

    def body(*refs):
        srcs, lands, send_sems, recv_sems, token = refs[:ci], refs[ci:ci + co], refs[ci + co + 1], refs[ci + co + 2], refs[-1]
        for cp in carry.plan(srcs, lands, send_sems, recv_sems)[0]:
            cp.start()
        token[...] = jnp.zeros_like(token)

    lands = [lax.empty(s.shape, s.dtype) for s in carry.out_shapes]
    hbm = lambda a: pltpu.with_memory_space_constraint(a, pltpu.HBM)
    out = pl.pallas_call(
        body, name=name, in_specs=[HBM_SPEC] * (ci + co) + [ANY],
        out_specs=[SEM_SPEC, SEM_SPEC] + [HBM_SPEC] * (ci + co) + [VMEM_SPEC],
        out_shape=[pltpu.SemaphoreType.DMA((carry.nsem,)), pltpu.SemaphoreType.DMA((carry.nsem,))]
        + [pltpu.HBM(a.shape, a.dtype) for a in carry.arrays + lands] + [jax.ShapeDtypeStruct((8, 128), F32)],
        input_output_aliases={t: 2 + t for t in range(ci + co)},
        compiler_params=pltpu.CompilerParams(has_side_effects=DATAFLOW),
    )(*[hbm(a) for a in carry.arrays + lands], after)
    return out[0], out[1], list(out[2:2 + ci]), list(out[2 + ci:2 + ci + co]), out[-1]


def _split_wait(carry, send_sems, recv_sems, srcs, lands, after, *, name):
    ci, co = len(srcs), len(lands)

    def body(*refs):
        sends, recvs = carry.plan(refs[:ci], refs[ci:ci + co], refs[ci + co], refs[ci + co + 1])
        for cp in sends:
            cp.wait_send()
        for cp in recvs:
            cp.wait_recv()

    out = pl.pallas_call(
        body, name=name, in_specs=[HBM_SPEC] * (ci + co) + [SEM_SPEC, SEM_SPEC, ANY], out_specs=[HBM_SPEC] * (ci + co),
        out_shape=[pltpu.HBM(a.shape, a.dtype) for a in list(srcs) + list(lands)],
        input_output_aliases={t: t for t in range(ci + co)},
        compiler_params=pltpu.CompilerParams(has_side_effects=DATAFLOW),
    )(*srcs, *lands, send_sems, recv_sems, after)
    return list(out[:ci]), list(out[ci:])


def _gather_start(shards, after, *, name):
    return _split_start(_gather_carry(shards), after, name=name)


def _gather_wait(send_sems, recv_sems, shards, lands, after, *, name):
    srcs, got = _split_wait(_gather_carry(shards), send_sems, recv_sems, shards, lands, after, name=name)
    return _fill_own(got, srcs)


def _ag_small(v, *, name):
    m, n = v.shape

    def body(x_ref, out_ref, red_ref, send_sems, recv_sems, local_sem):
        x, y, c = _pos()
        me, sibling = (x, y, c), (x, y, 1 - c)
        chips = _other_chips(x, y)
        rows = lambda px, py, pc: out_ref.at[pl.ds(pl.multiple_of((4 * px + 2 * py + pc) * m, 8), m), :]
        mine = pltpu.make_async_copy(x_ref, rows(*me), local_sem)
        mine.start()
        first = [_remote(x_ref, rows(*me), send_sems, recv_sems, 0, sibling)]
        first += [_remote(x_ref, rows(*me), send_sems, recv_sems, 1 + k, (*chip, c)) for k, chip in enumerate(chips)]
        for cp in first:
            cp.start()
        passed = [_remote(rows(*chip, c), rows(*chip, c), send_sems, recv_sems, 4 + k, sibling) for k, chip in enumerate(chips)]
        for k, chip in enumerate(chips):
            _remote(rows(*chip, c), rows(*chip, c), send_sems, recv_sems, 1 + k, me).wait_recv()
            passed[k].start()
        _remote(rows(*sibling), rows(*sibling), send_sems, recv_sems, 0, me).wait_recv()
        for k, chip in enumerate(chips):
            _remote(rows(*chip, 1 - c), rows(*chip, 1 - c), send_sems, recv_sems, 4 + k, me).wait_recv()
        for cp in first + passed:
            cp.wait_send()
        mine.wait()
        acc = out_ref[0:m, :]
        for d in range(1, N_DEV):
            acc = acc + out_ref[d * m:(d + 1) * m, :]
        red_ref[...] = acc

    return pl.pallas_call(
        body, name=name, in_specs=[VMEM_SPEC], out_specs=[VMEM_SPEC, VMEM_SPEC],
        out_shape=[jax.ShapeDtypeStruct((N_DEV * m, n), v.dtype), jax.ShapeDtypeStruct((m, n), v.dtype)],
        scratch_shapes=[pltpu.SemaphoreType.DMA((7,)), pltpu.SemaphoreType.DMA((7,)), pltpu.SemaphoreType.DMA],
    )(v)


def _halves(ref, c):
    rh = ref.shape[1] // 2
    return ref.at[:, pl.ds(pl.multiple_of(c * rh, 16), rh), :]


def _rs_swap(gs, *, name):
    nt = len(gs)

    def body(*refs):
        g, theirs = refs[:nt], refs[nt:2 * nt]
        send_sems, recv_sems = refs[2 * nt:]
        x, y, c = _pos()
        swaps = [_remote(_halves(g[t], 1 - c), theirs[t], send_sems, recv_sems, t, (x, y, 1 - c)) for t in range(nt)]
        for cp in swaps:
            cp.start()
        for cp in swaps:
            cp.wait()

    return pl.pallas_call(
        body, name=name, in_specs=[ANY] * nt, out_specs=[ANY] * nt,
        out_shape=[jax.ShapeDtypeStruct((a.shape[0], a.shape[1] // 2, a.shape[2]), a.dtype) for a in gs],
        scratch_shapes=[pltpu.SemaphoreType.DMA((nt,)), pltpu.SemaphoreType.DMA((nt,))],
    )(*gs)


def _exchange_carry(ss):
    def plan(s, out, send_sems, recv_sems):
        x, y, c = _pos()
        chip = 2 * x + y
        others = [(k, px, py, 2 * px + py) for k, (px, py) in enumerate(_other_chips(x, y))]
        sends = [_remote(s[t].at[j], out[t].at[chip], send_sems, recv_sems, 3 * t + k, (px, py, c))
                 for t in range(len(s)) for k, px, py, j in others]
        recvs = [_remote(s[t].at[j], out[t].at[j], send_sems, recv_sems, 3 * t + k, (x, y, c))
                 for t in range(len(s)) for k, _, _, j in others]
        return sends, recvs

    return _Carry(ss, [jax.ShapeDtypeStruct(a.shape, a.dtype) for a in ss], 3 * len(ss), plan)


def _fill_own_slab(outs, ss):
    chip = _chip_index()
    return [lax.dynamic_update_index_in_dim(o, lax.dynamic_index_in_dim(s, chip, 0, keepdims=False), chip, 0)
            for o, s in zip(outs, ss)]


def _rs_join(rs, *, name):
    nt = len(rs)

    def body(*refs):
        r, theirs = refs[:nt], refs[nt:2 * nt]
        send_sems, recv_sems = refs[2 * nt:]
        x, y, c = _pos()
        swaps = [_remote(r[t], theirs[t], send_sems, recv_sems, t, (x, y, 1 - c)) for t in range(nt)]
        for cp in swaps:
            cp.start()
        for cp in swaps:
            cp.wait()

    theirs = pl.pallas_call(
        body, name=name, in_specs=[ANY] * nt, out_specs=[ANY] * nt,
        out_shape=[jax.ShapeDtypeStruct(a.shape, a.dtype) for a in rs],
        scratch_shapes=[pltpu.SemaphoreType.DMA((nt,)), pltpu.SemaphoreType.DMA((nt,))],
    )(*rs)
    first = lax.axis_index("c") == 0
    return [jnp.concatenate([jnp.where(first, a, b), jnp.where(first, b, a)], axis=0) for a, b in zip(rs, theirs)]


def _row_tile(rows, dtype):
    unit = 16 if dtype == BF16 else 8
    if rows <= 512:
        return rows
    return next((t for t in (512, 256, 128, 64, 32, 16, 8) if rows % t == 0 and t % unit == 0), rows)


def _add_pair(g, b, half, *, name):
    n, rh, cols = b.shape
    tr = _row_tile(rh, BF16)
    nb = rh // tr

    def body(half_ref, g_ref, b_ref, o_ref):
        o_ref[...] = (g_ref[...].astype(F32) + b_ref[...].astype(F32)).astype(BF16)

    blk = pl.BlockSpec((None, tr, cols), lambda j, i, h: (j, i, 0))
    return pl.pallas_call(
        body, name=name,
        grid_spec=pltpu.PrefetchScalarGridSpec(
            num_scalar_prefetch=1, grid=(n, nb),
            in_specs=[pl.BlockSpec((None, tr, cols), lambda j, i, h: (j, h[0] * nb + i, 0)), blk], out_specs=blk),
        out_shape=jax.ShapeDtypeStruct(b.shape, BF16), compiler_params=_cp("parallel", "parallel"),
    )(half, g, b)


def _sum_chips(b, *, name):
    n, rh, cols = b.shape
    tr = _row_tile(rh, BF16)

    def body(b_ref, o_ref):
        acc = b_ref[0].astype(F32)
        for j in range(1, n):
            acc = acc + b_ref[j].astype(F32)
        o_ref[...] = acc

    return pl.pallas_call(
        body, name=name, grid=(rh // tr,), in_specs=[pl.BlockSpec((n, tr, cols), lambda i: (0, i, 0))],
        out_specs=pl.BlockSpec((tr, cols), lambda i: (i, 0)),
        out_shape=jax.ShapeDtypeStruct((rh, cols), F32), compiler_params=_cp("parallel"),
    )(b)


BIG = ("w_in", "w_out", "w_ffn_in", "w_down")
W_IN_SHARD = (N_MAIN + 2 * GH) // N_CHIP


def _chip_major(name, g):
    if name == "w_in":
        return jnp.stack([g[:, j * W_IN_SHARD:(j + 1) * W_IN_SHARD] for j in range(N_CHIP)])
    if name == "w_ffn_in":
        return g
    return g.reshape(N_CHIP, g.shape[0] // N_CHIP, g.shape[1])


def _rs_pairs(gs, tag):
    half = lax.axis_index("c").astype(jnp.int32).reshape(1)
    theirs = _rs_swap(gs, name=f"{tag}_swap")
    return [_add_pair(a, b, half, name=f"{tag}_add{i}") for i, (a, b) in enumerate(zip(gs, theirs))]


def _rs_sums(got, pairs, tag):
    return [_sum_chips(b, name=f"{tag}_sum{i}") for i, b in enumerate(_fill_own_slab(got, pairs))]


def _size(shape):
    n = 1
    for s in shape:
        n *= s
    return n


def _pack_flat(parts, total, dtype):
    flat = jnp.concatenate([p.reshape(-1).astype(dtype) for p in parts])
    return jnp.pad(flat, (0, total - flat.shape[0]))


def _unpack_flat(flat, shapes):
    out, o = [], 0
    for s in shapes:
        out.append(flat[o:o + _size(s)].reshape(s))
        o += _size(s)
    return out


WEIGHTS = ("attn_norm", "w_in", "attn_sinks", "gdn_conv_w", "gdn_a_log", "gdn_dt_bias", "gdn_norm", "w_out", "ffn_norm",
           "w_ffn_in", "ffn_conv_w", "ffn_conv_b", "w_down", "final_norm")
SMALL = {"attn_norm": (DEPTH, D), "attn_sinks": (DEPTH, HQ), "gdn_a_log": (DEPTH, GH), "gdn_dt_bias": (DEPTH, GH),
         "gdn_norm": (DEPTH, GD), "ffn_norm": (DEPTH, D), "ffn_conv_b": (DEPTH, DFF), "final_norm": (D,)}
CONV_FULL = {"gdn_conv_w": (DEPTH, GK, 3 * GH * GD), "ffn_conv_w": (DEPTH, FK, DFF)}
CONV_SHARD = {"gdn_conv_w": (DEPTH, GK, 3 * GH * GD // N_CHIP), "ffn_conv_w": (DEPTH, FK, DFF // N_CHIP)}
CONV_ROWS, SMALLG_ROWS, SMALLW_ROWS = 64, 320, 144


def kernel(x, attn_norm, w_in, attn_sinks, gdn_conv_w, gdn_a_log, gdn_dt_bias, gdn_norm, w_out, ffn_norm, w_ffn_in, ffn_conv_w, ffn_conv_b, w_down, final_norm, loss_target, m_attn_norm, m_w_in, m_attn_sinks, m_gdn_conv_w, m_gdn_a_log, m_gdn_dt_bias, m_gdn_norm, m_w_out, m_ffn_norm, m_w_ffn_in, m_ffn_conv_w, m_ffn_conv_b, m_w_down, m_final_norm, v_attn_norm, v_w_in, v_attn_sinks, v_gdn_conv_w, v_gdn_a_log, v_gdn_dt_bias, v_gdn_norm, v_w_out, v_ffn_norm, v_w_ffn_in, v_ffn_conv_w, v_ffn_conv_b, v_w_down, v_final_norm):
    w = dict(zip(WEIGHTS, (attn_norm, w_in, attn_sinks, gdn_conv_w, gdn_a_log, gdn_dt_bias, gdn_norm, w_out, ffn_norm,
                           w_ffn_in, ffn_conv_w, ffn_conv_b, w_down, final_norm)))
    m = dict(zip(WEIGHTS, (m_attn_norm, m_w_in, m_attn_sinks, m_gdn_conv_w, m_gdn_a_log, m_gdn_dt_bias, m_gdn_norm, m_w_out,
                           m_ffn_norm, m_w_ffn_in, m_ffn_conv_w, m_ffn_conv_b, m_w_down, m_final_norm)))
    v = dict(zip(WEIGHTS, (v_attn_norm, v_w_in, v_attn_sinks, v_gdn_conv_w, v_gdn_a_log, v_gdn_dt_bias, v_gdn_norm, v_w_out,
                           v_ffn_norm, v_w_ffn_in, v_ffn_conv_w, v_ffn_conv_b, v_w_down, v_final_norm)))
    cx, cy, _ = _pos()
    chip = 2 * cx + cy

    cpack = _pack_flat([w[n] for n in CONV_SHARD], CONV_ROWS * 128, F32).reshape(CONV_ROWS, 128)
    cgath, _ = _ag_small(cpack, name="gather_conv_w")
    cgath = cgath.reshape(N_DEV, CONV_ROWS * 128)
    cshards = [_unpack_flat(cgath[2 * j], list(CONV_SHARD.values())) for j in range(N_CHIP)]
    conv = {n: jnp.concatenate([cshards[j][i] for j in range(N_CHIP)], axis=2) for i, n in enumerate(CONV_SHARD)}

    Ps = [_layer_params(attn_norm[l], attn_sinks[l], conv["gdn_conv_w"][l], gdn_a_log[l], gdn_dt_bias[l], gdn_norm[l],
                        ffn_norm[l], conv["ffn_conv_w"][l], ffn_conv_b[l]) for l in range(DEPTH)]

    wb = {n: [w[n][l].astype(BF16) for l in range(DEPTH)] for n in BIG}
    mixer_w = lambda g_in, g_out: dict(all=_layer_weights(jnp.concatenate(list(g_in), axis=1)), out=g_out.reshape(D, D))
    ffn_w = lambda g_ffn, g_down: dict(ffn=g_ffn, down=g_down.reshape(DFF, D))
    groups = [[wb[a][l], wb[b][l]] for l in range(DEPTH) for a, b in (("w_in", "w_out"), ("w_ffn_in", "w_down"))]
    def after_start(P, key, started):
        return {**P, key: P[key] + started[4][:1, :1]}

    got0 = _gather_halves(groups[0], cgath, name="gather0")
    g1 = _gather_start(groups[1], got0[1], name="gather1_start")
    Wm0 = mixer_w(*got0)
    h, sv0, _ = _mixer_fwd(x[0], Wm0, after_start(Ps[0], "attn_norm", g1), 0)
    got1 = _gather_wait(*g1[:4], h, name="gather1_wait")
    g2 = _gather_start(groups[2], got1[1], name="gather2_start")
    Wf0 = ffn_w(*got1)
    h, sv0f, _ = _ffn_fwd(h, Wf0, after_start(Ps[0], "ffn_norm", g2), 0)
    got2 = _gather_wait(*g2[:4], h, name="gather2_wait")
    g3 = _gather_start(groups[3], got2[1], name="gather3_start")
    Wm1 = mixer_w(*got2)
    h, sv1, _ = _mixer_fwd(h, Wm1, after_start(Ps[1], "attn_norm", g3), 1)
    Wf1 = ffn_w(*_gather_wait(*g3[:4], h, name="gather3_wait"))
    h, sv1f, _ = _ffn_fwd(h, Wf1, Ps[1], 1)
    dx, g_final, loss_part = _loss_head(h, final_norm[None], loss_target[0], name="loss_head")

    def exchange_start(pairs, tag):
        carry = _exchange_carry(pairs)
        return carry, _split_start(carry, pairs[0], name=f"{tag}_exchange_start")

    def exchange_wait(carry, started, after, tag):
        srcs, got = _split_wait(carry, *started[:4], after, name=f"{tag}_exchange_wait")
        return _rs_sums(got, srcs, tag)

    tied = lambda P, key, started: {**P, key: P[key] + started[4][:1, :1]}
    dx, gf1 = _ffn_bwd(dx, {**sv1, **sv1f}, Wf1, Ps[1], 1)
    dx, gm1, _ = _mixer_bwd(dx, sv1, Wm1, Ps[1], 1)
    lg1 = {**gf1, **gm1}
    pairs1 = _rs_pairs([_chip_major(n, lg1[n]) for n in BIG], "rs1")
    ex1 = exchange_start(pairs1, "rs1")
    dx, gf0 = _ffn_bwd(dx, {**sv0, **sv0f}, Wf0, tied(Ps[0], "fcb", ex1[1]), 0)
    pairs0a = _rs_pairs([_chip_major(n, gf0[n]) for n in BIG[2:]], "rs0a")
    ex0a = exchange_start(pairs0a, "rs0a")
    dx, gm0, _ = _mixer_bwd(dx, sv0, Wm0, tied(Ps[0], "gnw", ex0a[1]), 0)
    lg = [{**gf0, **gm0}, lg1]
    pairs0b = _rs_pairs([_chip_major(n, gm0[n]) for n in BIG[:2]], "rs0b")
    sums0b = _rs_sums(_run_carry(_exchange_carry(pairs0b), name="rs0b_exchange"), pairs0b, "rs0b")
    sums1 = exchange_wait(*ex1, sums0b[0], "rs1")
    sums0a = exchange_wait(*ex0a, sums1[0], "rs0a")
    joined = _rs_join(sums0b + sums0a + sums1, name="rs_join")
    reduced = [joined[:4], joined[4:]]
    grad_x = dx[None]
    stacked = lambda n: jnp.stack([lg[l][n] for l in range(DEPTH)])

    small_parts = [g_final[0] if n == "final_norm" else stacked(n) for n in SMALL] + [stacked(n) for n in CONV_FULL]
    svec = _pack_flat(small_parts + [loss_part[0, :1]], SMALLG_ROWS * 128, F32).reshape(SMALLG_ROWS, 128)
    _, sred = _ag_small(svec, name="reduce_small")
    small_g = _unpack_flat(sred.reshape(-1), list(SMALL.values()) + list(CONV_FULL.values()) + [(1,)])
    g = dict(zip(list(SMALL) + list(CONV_FULL), small_g[:-1]))
    loss = small_g[-1][0]
    for n in CONV_FULL:
        wd = CONV_SHARD[n][2]
        g[n] = lax.dynamic_slice_in_dim(g[n], chip * wd, wd, axis=2)

    g.update({n: jnp.stack([reduced[l][i] for l in range(DEPTH)]) for i, n in enumerate(BIG)})

    delta, new_m, new_v = {}, {}, {}
    for n in BIG:
        delta[n], new_m[n], new_v[n] = _adamw(w[n], g[n], m[n], v[n], name=f"adamw_{n}")
    small_names = list(SMALL) + list(CONV_SHARD)
    small_shapes = list(SMALL.values()) + list(CONV_SHARD.values())
    packed = [_pack_flat([t[n] for n in small_names], SMALLW_ROWS * 128, F32).reshape(SMALLW_ROWS, 128) for t in (w, g, m, v)]
    for res, out in zip(_adamw(*packed, name="adamw_small"), (delta, new_m, new_v)):
        out.update(zip(small_names, _unpack_flat(res.reshape(-1), small_shapes)))

    return (loss, grad_x, *[g[n] for n in WEIGHTS], *[delta[n] for n in WEIGHTS], *[new_m[n] for n in WEIGHTS],
            *[new_v[n] for n in WEIGHTS])
```

```python
import functools

import jax
import jax.numpy as jnp
import numpy as np
from jax import lax
from jax.experimental import pallas as pl
from jax.experimental.pallas import tpu as pltpu

F32, BF16 = jnp.float32, jnp.bfloat16
HIGHEST = lax.Precision.HIGHEST
MESH = pl.DeviceIdType.MESH

D = 1024
T = 4096
DEPTH = 2
HQ, HKV, DH, WIN = 8, 2, 64, 128
GH, GD, GC, GK = 4, 128, 64, 4
DFF, FK = 2816, 3
EPS = 1e-6
N_MAIN = 2816
N_PROJ = N_MAIN + 128
COL_K, COL_V, COL_G, COL_Z, COL_BA = 4, 5, 6, 18, 22
N_CHUNK = T // GC
N_DEV, N_CHIP = 8, 4
FFN_CW = 2 * DFF // N_CHIP
LR, B1, B2, AEPS, WD, STEP = 0.001, 0.9, 0.999, 1e-08, 0.01, 10

VMEM_LIMIT = 56 * 1024 * 1024
ANY = pl.BlockSpec(memory_space=pl.ANY)
VMEM_SPEC = pl.BlockSpec(memory_space=pltpu.VMEM)


def _cp(*sem):
    return pltpu.CompilerParams(dimension_semantics=sem if sem else None, vmem_limit_bytes=VMEM_LIMIT)


class _Carry:
    def __init__(self, arrays, out_shapes, nsem, plan):
        self.arrays, self.out_shapes, self.nsem, self.plan = list(arrays), list(out_shapes), nsem, plan


def _seq_call(body, carry, *, name, steps, in_specs, out_specs, out_shape, args, scratch_shapes=(), aliases=None):
    in_specs, out_specs, out_shape, scratch = list(in_specs), list(out_specs), list(out_shape), list(scratch_shapes)
    n_in, n_out, n_scr = len(in_specs), len(out_specs), len(scratch)
    if carry is None:
        fn = body
    else:
        ci, co = len(carry.arrays), len(carry.out_shapes)

        def fn(*refs):
            ins, cins = refs[:n_in], refs[n_in:n_in + ci]
            outs, couts = refs[n_in + ci:n_in + ci + n_out], refs[n_in + ci + n_out:n_in + ci + n_out + co]
            scr, (ssem, rsem) = refs[n_in + ci + n_out + co:-2], refs[-2:]
            sends, recvs = carry.plan(cins, couts, ssem, rsem)

            @pl.when(pl.program_id(0) == 0)
            def _():
                for cp in sends:
                    cp.start()

            body(*ins, *outs, *scr)

            @pl.when(pl.program_id(0) == steps - 1)
            def _():
                for cp in recvs:
                    cp.wait_recv()
                for cp in sends:
                    cp.wait_send()

        in_specs += [ANY] * ci
        out_specs += [ANY] * co
        out_shape += carry.out_shapes
        scratch += [pltpu.SemaphoreType.DMA((carry.nsem,)), pltpu.SemaphoreType.DMA((carry.nsem,))]
        args = list(args) + carry.arrays
    outs = pl.pallas_call(
        fn, name=name, grid=(steps,), in_specs=in_specs, out_specs=out_specs, out_shape=out_shape,
        scratch_shapes=scratch, input_output_aliases=aliases or {}, compiler_params=_cp("arbitrary"),
    )(*args)
    return list(outs[:n_out]), list(outs[n_out:])


def _dot(a, b):
    return jnp.dot(a.astype(BF16), b.astype(BF16), preferred_element_type=F32)


def _dot_nt(a, b):
    return lax.dot_general(a.astype(BF16), b.astype(BF16), (((1,), (1,)), ((), ())), preferred_element_type=F32)


def _dot_tn(a, b):
    return lax.dot_general(a.astype(BF16), b.astype(BF16), (((0,), (0,)), ((), ())), preferred_element_type=F32)


def _iota(shape, dim):
    return lax.broadcasted_iota(jnp.int32, shape, dim)


def _col(x, idx):
    return jnp.sum(jnp.where(_iota(x.shape, 1) == idx, x, 0.0), axis=1, keepdims=True)


def _silu(y):
    return y * jax.nn.sigmoid(y)


_MM_DN = {"nn": (((1,), (0,)), ((), ())), "nt": (((1,), (1,)), ((), ())), "tn": (((0,), (0,)), ((), ()))}


def _mm(a, b, *, name, mode, dims, tm, tn, tk, res=None, out_dtype=F32, a_spec=None, b_spec=None, out_spec=None,
        out_shape=None, n_outer=False):
    M, N, K = dims
    assert M % tm == 0 and N % tn == 0 and K % tk == 0, (name, dims)
    nk = K // tk
    dn = _MM_DN[mode]

    def body(*refs):
        a_ref, b_ref = refs[:2]
        r_ref = None if res is None else refs[2]
        o_ref = refs[2 if res is None else 3]
        part = lax.dot_general(a_ref[...].astype(BF16), b_ref[...].astype(BF16), dn, preferred_element_type=F32)
        if nk == 1:
            o_ref[...] = (part if res is None else part + r_ref[...]).astype(out_dtype)
            return
        acc, k = refs[-1], pl.program_id(2)

        @pl.when(k == 0)
        def _():
            acc[...] = part

        @pl.when(k > 0)
        def _():
            acc[...] += part

        @pl.when(k == nk - 1)
        def _():
            r = acc[...]
            if res is not None:
                r = r + r_ref[...]
            o_ref[...] = r.astype(out_dtype)

    if a_spec is None:
        a_spec = pl.BlockSpec((tk, tm), lambda i, j, k: (k, i)) if mode == "tn" else pl.BlockSpec((tm, tk), lambda i, j, k: (i, k))
    if b_spec is None:
        b_spec = pl.BlockSpec((tn, tk), lambda i, j, k: (j, k)) if mode == "nt" else pl.BlockSpec((tk, tn), lambda i, j, k: (k, j))
    in_specs, args = [a_spec, b_spec], [a, b]
    if res is not None:
        in_specs.append(pl.BlockSpec((tm, tn), lambda i, j, k: (i, j)))
        args.append(res)
    out_spec = out_spec or pl.BlockSpec((tm, tn), lambda i, j, k: (i, j))
    grid = (M // tm, N // tn, nk)
    if n_outer:
        swap = lambda s: pl.BlockSpec(s.block_shape, lambda j, i, k, f=s.index_map: f(i, j, k))
        in_specs, out_spec, grid = [swap(s) for s in in_specs], swap(out_spec), (N // tn, M // tm, nk)
    return pl.pallas_call(
        body, name=name, grid=grid, in_specs=in_specs, out_specs=out_spec,
        out_shape=out_shape or jax.ShapeDtypeStruct((M, N), out_dtype),
        scratch_shapes=[pltpu.VMEM((tm, tn), F32)] if nk > 1 else [],
        compiler_params=_cp("parallel", "parallel", "arbitrary"),
    )(*args)


def _rms(x, w):
    return x * lax.rsqrt(jnp.mean(x * x, axis=-1, keepdims=True) + EPS) * w


def _rmsnorm_fwd(x, w, *, name, tm=512):
    def body(x_ref, w_ref, o_ref):
        o_ref[...] = _rms(x_ref[...], w_ref[...]).astype(BF16)

    return pl.pallas_call(
        body, name=name, grid=(x.shape[0] // tm,),
        in_specs=[pl.BlockSpec((tm, D), lambda i: (i, 0)), pl.BlockSpec((1, D), lambda i: (0, 0))],
        out_specs=pl.BlockSpec((tm, D), lambda i: (i, 0)),
        out_shape=jax.ShapeDtypeStruct(x.shape, BF16), compiler_params=_cp("parallel"),
    )(x, w)


def _rmsnorm_proj(x, w, w_all, *, name, tm=512):
    def body(x_ref, w_ref, b_ref, h_ref, p_ref, ba_ref):
        h = _rms(x_ref[...], w_ref[...]).astype(BF16)
        h_ref[...] = h
        r = jnp.dot(h, b_ref[...], preferred_element_type=F32)
        p_ref[...] = r[:, :N_MAIN].astype(BF16)
        ba_ref[...] = r[:, N_MAIN:]

    row = lambda n: pl.BlockSpec((tm, n), lambda i: (i, 0))
    return pl.pallas_call(
        body, name=name, grid=(x.shape[0] // tm,),
        in_specs=[row(D), pl.BlockSpec((1, D), lambda i: (0, 0)), pl.BlockSpec((D, N_PROJ), lambda i: (0, 0))],
        out_specs=[row(D), row(N_MAIN), row(N_PROJ - N_MAIN)],
        out_shape=[jax.ShapeDtypeStruct((x.shape[0], D), BF16), jax.ShapeDtypeStruct((x.shape[0], N_MAIN), BF16),
                   jax.ShapeDtypeStruct((x.shape[0], N_PROJ - N_MAIN), F32)],
        compiler_params=_cp("parallel"),
    )(x, w, w_all)


def _mm_rmsnorm_bwd(a, b, x, w, dres, *, name, K, tm, tk, a_spec=None, b_spec=None):
    M = x.shape[0]
    nk = K // tk
    assert M % tm == 0 and K % tk == 0, (name, M, K)

    def body(a_ref, b_ref, x_ref, w_ref, dr_ref, dx_ref, dw_ref, acc):
        i, k = pl.program_id(0), pl.program_id(1)
        part = lax.dot_general(a_ref[...].astype(BF16), b_ref[...].astype(BF16), _MM_DN["nt"], preferred_element_type=F32)

        @pl.when(k == 0)
        def _():
            acc[...] = part

        @pl.when(k > 0)
        def _():
            acc[...] += part

        @pl.when((i == 0) & (k == 0))
        def _():
            dw_ref[...] = jnp.zeros_like(dw_ref)

        @pl.when(k == nk - 1)
        def _():
            _, vjp = jax.vjp(_rms, x_ref[...], w_ref[...])
            dx, dw = vjp(acc[...])
            dx_ref[...] = dx + dr_ref[...]
            dw_ref[...] += dw

    row = pl.BlockSpec((tm, D), lambda i, k: (i, 0))
    vec = pl.BlockSpec((1, D), lambda i, k: (0, 0))
    return pl.pallas_call(
        body, name=name, grid=(M // tm, nk),
        in_specs=[a_spec or pl.BlockSpec((tm, tk), lambda i, k: (i, k)), b_spec or pl.BlockSpec((D, tk), lambda i, k: (0, k)),
                  row, vec, row],
        out_specs=[row, vec], out_shape=[jax.ShapeDtypeStruct((M, D), F32), jax.ShapeDtypeStruct((1, D), F32)],
        scratch_shapes=[pltpu.VMEM((tm, D), F32)], compiler_params=_cp("arbitrary", "arbitrary"),
    )(a, b, x, w, dres)


def _rmsnorm_bwd(x, w, dh, dres, *, name, tm=512):
    def body(x_ref, w_ref, dh_ref, dr_ref, dx_ref, dw_ref):
        _, vjp = jax.vjp(_rms, x_ref[...], w_ref[...])
        dx, dw = vjp(dh_ref[...])
        dx_ref[...] = dx + dr_ref[...]

        @pl.when(pl.program_id(0) == 0)
        def _():
            dw_ref[...] = jnp.zeros_like(dw_ref)

        dw_ref[...] += dw

    row = pl.BlockSpec((tm, D), lambda i: (i, 0))
    vec = pl.BlockSpec((1, D), lambda i: (0, 0))
    return pl.pallas_call(
        body, name=name, grid=(x.shape[0] // tm,), in_specs=[row, vec, row, row], out_specs=[row, vec],
        out_shape=[jax.ShapeDtypeStruct(x.shape, F32), jax.ShapeDtypeStruct((1, D), F32)],
        compiler_params=_cp("arbitrary"),
    )(x, w, dh, dres)


def _loss_head(x, w, tgt, *, name, tm=512):
    def f(xv, wv, tv):
        err = _rms(xv, wv) - tv
        per_row = jnp.sum(err * err, axis=1, keepdims=True) * (0.5 / D)
        return jnp.sum(per_row, axis=0, keepdims=True)

    def body(x_ref, w_ref, t_ref, dx_ref, dw_ref, loss_ref):
        tv = t_ref[...]
        loss, vjp = jax.vjp(lambda xv, wv: f(xv, wv, tv), x_ref[...], w_ref[...])
        dx, dw = vjp(jnp.ones((1, 1), F32))
        dx_ref[...] = dx

        @pl.when(pl.program_id(0) == 0)
        def _():
            dw_ref[...] = jnp.zeros_like(dw_ref)
            loss_ref[...] = jnp.zeros_like(loss_ref)

        dw_ref[...] += dw
        loss_ref[...] += jnp.broadcast_to(loss, loss_ref.shape)

    row = pl.BlockSpec((tm, D), lambda i: (i, 0))
    vec = pl.BlockSpec((1, D), lambda i: (0, 0))
    return pl.pallas_call(
        body, name=name, grid=(x.shape[0] // tm,), in_specs=[row, vec, row],
        out_specs=[row, vec, pl.BlockSpec((1, 128), lambda i: (0, 0))],
        out_shape=[jax.ShapeDtypeStruct(x.shape, F32), jax.ShapeDtypeStruct((1, D), F32),
                   jax.ShapeDtypeStruct((1, 128), F32)],
        compiler_params=_cp("arbitrary"),
    )(x, w, tgt)


def _swa_bias():
    G = HQ // HKV
    r, c = np.arange(G * WIN)[:, None], np.arange(2 * WIN)[None, :]
    rel = (r % WIN) + WIN - c
    band = (rel >= 0) & (rel < WIN)
    out = np.empty((2, HKV, G * WIN, 2 * WIN), np.float32)
    for h2 in range(HKV):
        slope = 2.0 ** (-8.0 * (h2 * G + r // WIN + 1) / HQ)
        out[0, h2] = np.where(band & (c >= WIN), -slope * rel, -1e30)
        out[1, h2] = np.where(band, -slope * rel, -1e30)
    return jnp.asarray(out)


def _swa_heads(qs, k2s, v2s, sk, biases, hqs=range(HQ)):
    first_kv = hqs[0] // (HQ // HKV)
    kv = lambda xs: [xs[hq // (HQ // HKV) - first_kv] for hq in hqs]
    ss = _hmap(lambda q, k2, b: _dot_nt(q, k2) * (DH ** -0.5) + b, qs, kv(k2s), biases)
    sinks = [_col(sk, hq) for hq in hqs]
    ms = _hmap(lambda s, sink: lax.stop_gradient(jnp.maximum(jnp.max(s, axis=1, keepdims=True), sink)), ss, sinks)
    ps = _hmap(lambda s, m: jnp.exp(s - m), ss, ms)
    dens = _hmap(lambda p, sink, m: jnp.sum(p, axis=1, keepdims=True) + jnp.exp(sink - m), ps, sinks, ms)
    return _hmap(lambda p, den, v2: _dot(p * (1.0 / den), v2), ps, dens, kv(v2s))


def _swa_load(q_ref, kp_ref, kc_ref, vp_ref, vc_ref, bias_ref):
    q, G = q_ref[...].astype(F32), HQ // HKV
    qs = [q[:, hq * DH:(hq + 1) * DH] for hq in range(HQ)]
    both = lambda a, b, h2: jnp.concatenate([a[:, h2 * DH:(h2 + 1) * DH], b[:, h2 * DH:(h2 + 1) * DH]], axis=0).astype(F32)
    k2s = [both(kp_ref, kc_ref, h2) for h2 in range(HKV)]
    v2s = [both(vp_ref, vc_ref, h2) for h2 in range(HKV)]
    biases = [bias_ref[hq // G, (hq % G) * WIN:(hq % G + 1) * WIN, :] for hq in range(HQ)]
    return qs, k2s, v2s, biases


def _swa_specs():
    prev = lambda i: jnp.maximum(jnp.minimum(i, T // WIN - 1) - 1, 0)
    cur = lambda i: jnp.minimum(i, T // WIN - 1)
    return [
        pl.BlockSpec((WIN, HQ * DH), lambda i: (cur(i), 0)),
        pl.BlockSpec((WIN, 128), lambda i: (prev(i), COL_K)),
        pl.BlockSpec((WIN, 128), lambda i: (cur(i), COL_K)),
        pl.BlockSpec((WIN, 128), lambda i: (prev(i), COL_V)),
        pl.BlockSpec((WIN, 128), lambda i: (cur(i), COL_V)),
        pl.BlockSpec((1, 128), lambda i: (0, 0)),
        pl.BlockSpec((None, HKV, (HQ // HKV) * WIN, 2 * WIN), lambda i: (jnp.minimum(i, 1), 0, 0, 0)),
    ]


def _swa_fwd(proj, sinks, *, name):
    def body(q_ref, kp_ref, kc_ref, vp_ref, vc_ref, sk_ref, bias_ref, o_ref):
        qs, k2s, v2s, biases = _swa_load(q_ref, kp_ref, kc_ref, vp_ref, vc_ref, bias_ref)
        o_ref[...] = jnp.concatenate(_swa_heads(qs, k2s, v2s, sk_ref[...], biases), axis=1).astype(BF16)

    return pl.pallas_call(
        body, name=name, grid=(T // WIN,), in_specs=_swa_specs(),
        out_specs=pl.BlockSpec((WIN, HQ * DH), lambda i: (i, 0)),
        out_shape=jax.ShapeDtypeStruct((T, HQ * DH + GH * GD), BF16), compiler_params=_cp("parallel"),
    )(proj, proj, proj, proj, proj, sinks, _swa_bias())


def _swa_bwd(proj, sinks, do, dproj, *, name):
    NB = T // WIN
    QW = HQ * DH

    def body(q_ref, kp_ref, kc_ref, vp_ref, vc_ref, sk_ref, bias_ref, do_ref, _, out_ref, dsk_ref, cq, ck, cv):
        i = pl.program_id(0)

        @pl.when(i == 0)
        def _():
            cq[...] = jnp.zeros_like(cq)
            ck[...] = jnp.zeros_like(ck)
            cv[...] = jnp.zeros_like(cv)
            dsk_ref[...] = jnp.zeros_like(dsk_ref)

        @pl.when(i < NB)
        def _():
            qs, k2s, v2s, biases = _swa_load(q_ref, kp_ref, kc_ref, vp_ref, vc_ref, bias_ref)
            dov = do_ref[...].astype(F32)
            dqs, dk2s, dv2s, dsk = [], [], [], jnp.zeros_like(sk_ref[...])
            for h2 in range(HKV):
                hqs = range(h2 * (HQ // HKV), (h2 + 1) * (HQ // HKV))
                _, vjp = jax.vjp(functools.partial(_swa_heads, biases=[biases[hq] for hq in hqs], hqs=hqs),
                                 [qs[hq] for hq in hqs], [k2s[h2]], [v2s[h2]], sk_ref[...])
                dq, dk2, dv2, dsk_h = vjp([dov[:, hq * DH:(hq + 1) * DH] for hq in hqs])
                dqs, dk2s, dv2s, dsk = dqs + dq, dk2s + dk2, dv2s + dv2, dsk + dsk_h
            out_ref[:, :QW] = cq[...].astype(BF16)
            out_ref[:, QW:QW + 128] = (ck[...] + jnp.concatenate([d[:WIN] for d in dk2s], axis=1)).astype(BF16)
            out_ref[:, QW + 128:] = (cv[...] + jnp.concatenate([d[:WIN] for d in dv2s], axis=1)).astype(BF16)
            cq[...] = jnp.concatenate(dqs, axis=1)
            ck[...] = jnp.concatenate([d[WIN:] for d in dk2s], axis=1)
            cv[...] = jnp.concatenate([d[WIN:] for d in dv2s], axis=1)
            dsk_ref[...] += dsk

        @pl.when(i == NB)
        def _():
            out_ref[:, :QW] = cq[...].astype(BF16)
            out_ref[:, QW:QW + 128] = ck[...].astype(BF16)
            out_ref[:, QW + 128:] = cv[...].astype(BF16)

    qblk = pl.BlockSpec((WIN, QW), lambda i: (jnp.minimum(i, NB - 1), 0))
    return pl.pallas_call(
        body, name=name, grid=(NB + 1,), in_specs=_swa_specs() + [qblk, ANY],
        out_specs=[pl.BlockSpec((WIN, QW + 256), lambda i: (jnp.maximum(i - 1, 0), 0)), pl.BlockSpec((1, 128), lambda i: (0, 0))],
        out_shape=[jax.ShapeDtypeStruct(dproj.shape, dproj.dtype), jax.ShapeDtypeStruct((1, 128), F32)],
        scratch_shapes=[pltpu.VMEM((WIN, QW), F32), pltpu.VMEM((WIN, 128), F32), pltpu.VMEM((WIN, 128), F32)],
        input_output_aliases={8: 0}, compiler_params=_cp("arbitrary"),
    )(proj, proj, proj, proj, proj, sinks, _swa_bias(), do, dproj)


RC = 256
HALO = 8


def _load_ext(ref, c):
    nch = T // RC
    r0 = pl.multiple_of(c * RC, RC)
    p0 = pl.multiple_of(jnp.maximum(r0 - HALO, 0), HALO)
    n0 = pl.multiple_of(jnp.minimum(r0 + RC, T - HALO), HALO)
    prev = jnp.where(c > 0, ref[pl.ds(p0, HALO), :].astype(F32), 0.0)
    nxt = jnp.where(c < nch - 1, ref[pl.ds(n0, HALO), :].astype(F32), 0.0)
    return jnp.concatenate([prev, ref[pl.ds(r0, RC), :].astype(F32), nxt], axis=0)


def _conv_ext(xe, w, K):
    y = w[K - 1:K, :] * xe
    for s in range(1, K):
        y = y + w[K - 1 - s:K - s, :] * pltpu.roll(xe, s, 0)
    return y


def _conv_bwd_ext(xe, dye, w, K, dw_ref):
    n = xe.shape[0]
    own = slice(HALO, HALO + RC)
    dx = w[K - 1:K, :] * dye
    dw_ref[K - 1:K, :] += jnp.sum(dye[own] * xe[own], axis=0, keepdims=True)
    for s in range(1, K):
        dx = dx + w[K - 1 - s:K - s, :] * pltpu.roll(dye, n - s, 0)
        dw_ref[K - 1 - s:K - s, :] += jnp.sum(dye[own] * pltpu.roll(xe, s, 0)[own], axis=0, keepdims=True)
    return dx[own]


def _gdn_post_conv(y, is_qk):
    a = _silu(y)
    nrm = a * lax.rsqrt(jnp.sum(a * a, axis=1, keepdims=True) + EPS)
    return jnp.where(is_qk, nrm, a)


def _gdn_prep_fwd(proj, conv_w, *, name):
    nblk = 3 * GH

    def body(x_ref, w_ref, o_ref):
        is_qk = pl.program_id(0) < 2 * GH
        w = w_ref[...]

        def chunk(c, carry):
            y = _conv_ext(_load_ext(x_ref, c), w, GK)[HALO:HALO + RC]
            o_ref[pl.ds(pl.multiple_of(c * RC, RC), RC), :] = _gdn_post_conv(y, is_qk)
            return carry

        lax.fori_loop(0, T // RC, chunk, 0)

    return pl.pallas_call(
        body, name=name, grid=(nblk,),
        in_specs=[pl.BlockSpec((T, 128), lambda j: (0, COL_G + j)), pl.BlockSpec((GK, 128), lambda j: (0, j))],
        out_specs=pl.BlockSpec((T, 128), lambda j: (0, j)),
        out_shape=jax.ShapeDtypeStruct((T, nblk * 128), F32), compiler_params=_cp("parallel"),
    )(proj, conv_w)


def _gdn_prep_bwd(proj, conv_w, dout, dproj, *, name):
    nblk = 3 * GH

    def body(x_ref, w_ref, d_ref, _, dx_ref, dw_ref):
        is_qk = pl.program_id(0) < 2 * GH
        w = w_ref[...]
        dw_ref[...] = jnp.zeros_like(dw_ref)

        def chunk(c, carry):
            xe = _load_ext(x_ref, c)
            _, vjp = jax.vjp(lambda y: _gdn_post_conv(y, is_qk), _conv_ext(xe, w, GK))
            (dye,) = vjp(_load_ext(d_ref, c))
            dx_ref[pl.ds(pl.multiple_of(c * RC, RC), RC), :] = _conv_bwd_ext(xe, dye, w, GK, dw_ref).astype(BF16)
            return carry

        lax.fori_loop(0, T // RC, chunk, 0)

    return pl.pallas_call(
        body, name=name, grid=(nblk,),
        in_specs=[pl.BlockSpec((T, 128), lambda j: (0, COL_G + j)), pl.BlockSpec((GK, 128), lambda j: (0, j)),
                  pl.BlockSpec((T, 128), lambda j: (0, j)), ANY],
        out_specs=[pl.BlockSpec((T, 128), lambda j: (0, COL_G + j)), pl.BlockSpec((GK, 128), lambda j: (0, j))],
        out_shape=[jax.ShapeDtypeStruct(dproj.shape, dproj.dtype), jax.ShapeDtypeStruct((GK, nblk * 128), F32)],
        input_output_aliases={3: 0}, compiler_params=_cp("parallel"),
    )(proj, conv_w, dout, dproj)


def _ffn_post_conv(y, b, up):
    return _silu(y + b) * up


def _ffn_act_fwd(gu, conv_w, conv_b, *, name, carry=None):
    nblk = DFF // 128

    def body(g_ref, u_ref, w_ref, b_ref, o_ref):
        w, b = w_ref[...], b_ref[...]

        def chunk(c, carry):
            rows = pl.ds(pl.multiple_of(c * RC, RC), RC)
            y = _conv_ext(_load_ext(g_ref, c), w, FK)[HALO:HALO + RC]
            o_ref[rows, :] = _ffn_post_conv(y, b, u_ref[rows, :].astype(F32)).astype(BF16)
            return carry

        lax.fori_loop(0, T // RC, chunk, 0)

    (act,), carried = _seq_call(
        body, carry, name=name, steps=nblk,
        in_specs=[pl.BlockSpec((T, 128), lambda j: (0, j)), pl.BlockSpec((T, 128), lambda j: (0, nblk + j)),
                  pl.BlockSpec((FK, 128), lambda j: (0, j)), pl.BlockSpec((1, 128), lambda j: (0, j))],
        out_specs=[pl.BlockSpec((T, 128), lambda j: (0, j))], out_shape=[jax.ShapeDtypeStruct((T, DFF), BF16)],
        args=(gu, gu, conv_w, conv_b))
    return act, carried


def _ffn_act_bwd(gu, conv_w, conv_b, dact, *, name):
    nblk = DFF // 128

    def body(g_ref, u_ref, w_ref, b_ref, d_ref, dgu_ref, dw_ref, db_ref):
        dg_ref, du_ref = dgu_ref.at[0], dgu_ref.at[1]
        w, b = w_ref[...], b_ref[...]
        dw_ref[...] = jnp.zeros_like(dw_ref)
        db_ref[...] = jnp.zeros_like(db_ref)

        def chunk(c, carry):
            rows = pl.ds(pl.multiple_of(c * RC, RC), RC)
            xe = _load_ext(g_ref, c)
            ue = _load_ext(u_ref, c)
            _, vjp = jax.vjp(_ffn_post_conv, _conv_ext(xe, w, FK), b, ue)
            dye, db, due = vjp(_load_ext(d_ref, c))
            du_ref[rows, :] = due[HALO:HALO + RC].astype(BF16)
            db_ref[...] += jnp.sum(dye[HALO:HALO + RC], axis=0, keepdims=True)
            dg_ref[rows, :] = _conv_bwd_ext(xe, dye, w, FK, dw_ref).astype(BF16)
            return carry

        lax.fori_loop(0, T // RC, chunk, 0)

    col = pl.BlockSpec((T, 128), lambda j: (0, j))
    return pl.pallas_call(
        body, name=name, grid=(nblk,),
        in_specs=[col, pl.BlockSpec((T, 128), lambda j: (0, nblk + j)), pl.BlockSpec((FK, 128), lambda j: (0, j)),
                  pl.BlockSpec((1, 128), lambda j: (0, j)), col],
        out_specs=[pl.BlockSpec((2, T, 128), lambda j: (0, 0, j)), pl.BlockSpec((FK, 128), lambda j: (0, j)),
                   pl.BlockSpec((1, 128), lambda j: (0, j))],
        out_shape=[jax.ShapeDtypeStruct((2, T, DFF), BF16), jax.ShapeDtypeStruct((FK, DFF), F32),
                   jax.ShapeDtypeStruct((1, DFF), F32)],
        compiler_params=_cp("parallel"),
    )(gu, gu, conv_w, conv_b, dact)


def _gdn_gates(ba, alog, dtb):
    beta = jax.nn.sigmoid(ba)
    g = -jnp.exp(alog) * jax.nn.softplus(ba + dtb)
    tril = (_iota((GC, GC), 0) >= _iota((GC, GC), 1)).astype(F32)
    return beta, jnp.dot(tril, g, precision=HIGHEST, preferred_element_type=F32)


def _hmap(f, *lists):
    return [f(*xs) for xs in zip(*lists)]


def _gdn_cols(beta_all, gc_all):
    return [_col(beta_all, h) for h in range(GH)], [_col(gc_all, GH + h) for h in range(GH)]


def _gdn_decay(Gcs):
    r, c = _iota((GC, GC), 0), _iota((GC, GC), 1)
    eye, ones = (r == c).astype(F32), jnp.ones((GC, GC), F32)
    grows = _hmap(lambda G: jnp.dot(ones, eye * G, precision=HIGHEST, preferred_element_type=F32), Gcs)
    return _hmap(lambda G, grow: jnp.exp(jnp.where(r >= c, G - grow, -1e30)), Gcs, grows)


def _gdn_pre(beta_all, gc_all):
    betas, Gcs = _gdn_cols(beta_all, gc_all)
    return betas, Gcs, _gdn_decay(Gcs)


def _gdn_A(ks, betas, decays):
    strict = _iota((GC, GC), 0) > _iota((GC, GC), 1)
    kk = _hmap(lambda k, b: _dot_nt(k * b, k), ks, betas)
    return _hmap(lambda a, d: jnp.where(strict, a * d, 0.0), kk, decays)


def _tri_inv(As):
    eye = (_iota((GC, GC), 0) == _iota((GC, GC), 1)).astype(F32)
    Tms, Ps = [eye - A for A in As], As
    for _ in range(GC.bit_length() - 2):
        Ps = _hmap(lambda P: _dot(P, P), Ps)
        Tms = _hmap(lambda Tm, P: Tm + _dot(Tm, P), Tms, Ps)
    return Tms


def _gdn_chunk_pre(qs, ks, vs, betas, Gcs, decays, Tms):
    eGs = _hmap(jnp.exp, Gcs)
    us = _hmap(lambda Tm, v, b: _dot(Tm, v * b), Tms, vs, betas)
    ws = _hmap(lambda Tm, k, b, eG: _dot(Tm, k * b * eG), Tms, ks, betas, eGs)
    qss = [q * (GD ** -0.5) for q in qs]
    qks = _hmap(lambda q, k, d: _dot_nt(q, k) * d, qss, ks, decays)
    glasts = [jnp.sum(jnp.where(_iota(G.shape, 0) == GC - 1, G, 0.0), axis=0, keepdims=True) for G in Gcs]
    kds = _hmap(lambda k, gl, G: k * jnp.exp(gl - G), ks, glasts, Gcs)
    qes = _hmap(lambda q, eG: q * eG, qss, eGs)
    return us, ws, qks, kds, qes, glasts


def _gdn_chunk_state(us, ws, qks, kds, qes, glasts, Ss):
    v_news = _hmap(lambda u, w, S: u - _dot(w, S), us, ws, Ss)
    qS = _hmap(_dot, qes, Ss)
    os = _hmap(lambda a, qk, vn: a + _dot(qk, vn), qS, qks, v_news)
    S_news = _hmap(lambda S, gl, kd, vn: S * jnp.exp(gl) + _dot_tn(kd, vn), Ss, glasts, kds, v_news)
    return os, S_news


def _gdn_chunk(qs, ks, vs, betas, Gcs, decays, Ss, Tms):
    return _gdn_chunk_state(*_gdn_chunk_pre(qs, ks, vs, betas, Gcs, decays, Tms), Ss)


GDN_BWD_CPS = 2
GDN_CPS = 4


def _gdn_chunk_fwd(qkv_c, ba, alog, dtb, *, name, carry=None):
    W3, CPS = 3 * GH * GD, GDN_CPS

    def body(x_ref, ba_ref, al_ref, dt_ref, o_ref, s_ref, t_ref, S):
        @pl.when(pl.program_id(0) == 0)
        def _():
            S[...] = jnp.zeros_like(S)

        betas, Gcs, decays, qkv = [], [], [], ([], [], [])
        for sub in range(CPS):
            rows = slice(sub * GC, (sub + 1) * GC)
            b, g, d = _gdn_pre(*_gdn_gates(ba_ref[rows, :], al_ref[...], dt_ref[...]))
            betas, Gcs, decays = betas + b, Gcs + g, decays + d
            for p in range(3):
                qkv[p].extend(x_ref[rows, (p * GH + h) * GD:(p * GH + h + 1) * GD] for h in range(GH))
        qs, ks, vs = qkv
        Tms = _tri_inv(_gdn_A(ks, betas, decays))
        pre = _gdn_chunk_pre(qs, ks, vs, betas, Gcs, decays, Tms)
        Ss = [S[h] for h in range(GH)]
        for sub in range(CPS):
            mine = slice(sub * GH, (sub + 1) * GH)
            os, S_news = _gdn_chunk_state(*[part[mine] for part in pre], Ss)
            for h in range(GH):
                s_ref[sub, h] = Ss[h]
                t_ref[sub, h] = Tms[sub * GH + h]
                o_ref[sub * GC:(sub + 1) * GC, h * GD:(h + 1) * GD] = os[h]
            Ss = S_news
        for h in range(GH):
            S[h] = Ss[h]

    vec = pl.BlockSpec((1, 128), lambda n: (0, 0))
    return _seq_call(
        body, carry, name=name, steps=N_CHUNK // CPS,
        in_specs=[pl.BlockSpec((CPS * GC, W3), lambda n: (n, 0)), pl.BlockSpec((CPS * GC, 128), lambda n: (n, 0)), vec, vec],
        out_specs=[pl.BlockSpec((CPS * GC, GH * GD), lambda n: (n, 0)),
                   pl.BlockSpec((CPS, GH, GD, GD), lambda n: (n, 0, 0, 0)),
                   pl.BlockSpec((CPS, GH, GC, GC), lambda n: (n, 0, 0, 0))],
        out_shape=[jax.ShapeDtypeStruct((T, GH * GD), F32), jax.ShapeDtypeStruct((N_CHUNK, GH, GD, GD), F32),
                   jax.ShapeDtypeStruct((N_CHUNK, GH, GC, GC), F32)],
        scratch_shapes=[pltpu.VMEM((GH, GD, GD), F32)], args=(qkv_c, ba, alog, dtb))


def _gdn_chunk_bwd(qkv_c, ba, alog, dtb, s_all, t_all, do, dproj, *, name, carry=None):
    W3, CPS = 3 * GH * GD, GDN_BWD_CPS
    steps = N_CHUNK // CPS
    rev = lambda n: steps - 1 - n

    def body(x_ref, ba_ref, al_ref, dt_ref, s_ref, t_ref, do_ref, _, dx_ref, dba_ref, dal_ref, ddt_ref, dS):
        @pl.when(pl.program_id(0) == 0)
        def _():
            dS[...] = jnp.zeros_like(dS)
            dal_ref[...] = jnp.zeros_like(dal_ref)
            ddt_ref[...] = jnp.zeros_like(ddt_ref)

        rows = lambda sub: slice(sub * GC, (sub + 1) * GC)
        gates = [jax.vjp(_gdn_gates, ba_ref[rows(sub), :], al_ref[...], dt_ref[...]) for sub in range(CPS)]

        def pre_all(beta_alls, gc_alls):
            betas, Gcs, decays = [], [], []
            for b_all, g_all in zip(beta_alls, gc_alls):
                b, g, d = _gdn_pre(b_all, g_all)
                betas, Gcs, decays = betas + b, Gcs + g, decays + d
            return betas, Gcs, decays

        (betas, Gcs, decays), vjp_pre = jax.vjp(pre_all, [g[0][0] for g in gates], [g[0][1] for g in gates])
        qs, ks, vs = ([x_ref[rows(sub), (p * GH + h) * GD:(p * GH + h + 1) * GD] for sub in range(CPS) for h in range(GH)]
                      for p in range(3))
        Tms = [t_ref[sub, h] for sub in range(CPS) for h in range(GH)]
        pre, vjp_p = jax.vjp(_gdn_chunk_pre, qs, ks, vs, betas, Gcs, decays, Tms)
        dSs = [dS[h] for h in range(GH)]
        dpre = [[None] * (CPS * GH) for _ in pre]
        for sub in reversed(range(CPS)):
            mine = slice(sub * GH, (sub + 1) * GH)
            Ss = [s_ref[sub, h] for h in range(GH)]
            dos = [do_ref[rows(sub), h * GD:(h + 1) * GD] for h in range(GH)]
            _, vjp_s = jax.vjp(_gdn_chunk_state, *[part[mine] for part in pre], Ss)
            *dparts, dSs = vjp_s((dos, dSs))
            for full, part in zip(dpre, dparts):
                full[mine] = part
        for h in range(GH):
            dS[h] = dSs[h]
        dqs, dks, dvs, dbetas, dGcs, ddecays, dTs = vjp_p(tuple(dpre))
        dXs = _hmap(_dot_nt, dTs, Tms)
        dAs = _hmap(lambda Tm, dX: -_dot_tn(Tm, dX), Tms, dXs)
        _, vjp_a = jax.vjp(_gdn_A, ks, betas, decays)
        dks2, dbetas2, ddecays2 = vjp_a(dAs)
        add = lambda a, b: _hmap(jnp.add, a, b)
        db_alls, dg_alls = vjp_pre((add(dbetas, dbetas2), dGcs, add(ddecays, ddecays2)))
        for sub in range(CPS):
            for h in range(GH):
                i = sub * GH + h
                dx_ref[rows(sub), h * GD:(h + 1) * GD] = dqs[i]
                dx_ref[rows(sub), (GH + h) * GD:(GH + h + 1) * GD] = dks[i] + dks2[i]
                dx_ref[rows(sub), (2 * GH + h) * GD:(2 * GH + h + 1) * GD] = dvs[i]
            dba, dal, ddt = gates[sub][1]((db_alls[sub], dg_alls[sub]))
            dba_ref[rows(sub), :] = dba.astype(BF16)
            dal_ref[...] += dal
            ddt_ref[...] += ddt

    vec = pl.BlockSpec((1, 128), lambda n: (0, 0))
    return _seq_call(
        body, carry, name=name, steps=steps,
        in_specs=[pl.BlockSpec((CPS * GC, W3), lambda n: (rev(n), 0)), pl.BlockSpec((CPS * GC, 128), lambda n: (rev(n), 0)),
                  vec, vec, pl.BlockSpec((CPS, GH, GD, GD), lambda n: (rev(n), 0, 0, 0)),
                  pl.BlockSpec((CPS, GH, GC, GC), lambda n: (rev(n), 0, 0, 0)),
                  pl.BlockSpec((CPS * GC, GH * GD), lambda n: (rev(n), 0)),
                  ANY],
        out_specs=[pl.BlockSpec((CPS * GC, W3), lambda n: (rev(n), 0)),
                   pl.BlockSpec((CPS * GC, 128), lambda n: (rev(n), COL_BA)), vec, vec],
        out_shape=[jax.ShapeDtypeStruct((T, W3), F32), jax.ShapeDtypeStruct(dproj.shape, dproj.dtype),
                   jax.ShapeDtypeStruct((1, 128), F32), jax.ShapeDtypeStruct((1, 128), F32)],
        scratch_shapes=[pltpu.VMEM((GH, GD, GD), F32)], aliases={7: 1},
        args=(qkv_c, ba, alog, dtb, s_all, t_all, do, dproj))


def _gdn_post(o, z, nw):
    return o * lax.rsqrt(jnp.mean(o * o, axis=-1, keepdims=True) + EPS) * nw * _silu(z)


def _gdn_post_fwd(o_raw, proj, nw, mixed, *, name, tm=512):
    def body(o_ref, z_ref, w_ref, _, out_ref):
        out_ref[...] = _gdn_post(o_ref[...], z_ref[...].astype(F32), w_ref[...]).astype(BF16)

    return pl.pallas_call(
        body, name=name, grid=(T // tm, GH),
        in_specs=[pl.BlockSpec((tm, GD), lambda i, h: (i, h)), pl.BlockSpec((tm, GD), lambda i, h: (i, COL_Z + h)),
                  pl.BlockSpec((1, GD), lambda i, h: (0, 0)), ANY],
        out_specs=pl.BlockSpec((tm, GD), lambda i, h: (i, HQ * DH // GD + h)),
        out_shape=jax.ShapeDtypeStruct(mixed.shape, mixed.dtype), input_output_aliases={3: 0},
        compiler_params=_cp("parallel", "parallel"),
    )(o_raw, proj, nw, mixed)


def _gdn_post_bwd(o_raw, proj, nw, dmixed, *, name, tm=512):
    def body(o_ref, z_ref, w_ref, d_ref, do_ref, dz_ref, dw_ref):
        _, vjp = jax.vjp(_gdn_post, o_ref[...], z_ref[...].astype(F32), w_ref[...])
        do, dz, dw = vjp(d_ref[...])
        do_ref[...] = do
        dz_ref[...] = dz.astype(BF16)

        @pl.when((pl.program_id(0) == 0) & (pl.program_id(1) == 0))
        def _():
            dw_ref[...] = jnp.zeros_like(dw_ref)

        dw_ref[...] += dw

    blk = pl.BlockSpec((tm, GD), lambda i, h: (i, h))
    vec = pl.BlockSpec((1, GD), lambda i, h: (0, 0))
    return pl.pallas_call(
        body, name=name, grid=(T // tm, GH),
        in_specs=[blk, pl.BlockSpec((tm, GD), lambda i, h: (i, COL_Z + h)), vec,
                  pl.BlockSpec((tm, GD), lambda i, h: (i, GH + h))],
        out_specs=[blk, pl.BlockSpec((tm, GD), lambda i, h: (i, COL_Z + h)), vec],
        out_shape=[jax.ShapeDtypeStruct((T, GH * GD), F32), jax.ShapeDtypeStruct((T, N_PROJ), BF16),
                   jax.ShapeDtypeStruct((1, GD), F32)],
        compiler_params=_cp("arbitrary", "arbitrary"),
    )(o_raw, proj, nw, dmixed)


def _adamw(w, g, m, v, *, name):
    shape = w.shape
    cols = shape[-1]
    w2, g2, m2, v2 = (a.reshape(-1, cols) for a in (w, g, m, v))
    rows = w2.shape[0]
    tr = next((t for t in (512, 256, 128, 64, 32, 16, 8) if rows % t == 0), rows)

    def body(w_ref, g_ref, m_ref, v_ref, d_ref, nm_ref, nv_ref):
        gv = g_ref[...]
        nm = B1 * m_ref[...] + (1.0 - B1) * gv
        nv = B2 * v_ref[...] + (1.0 - B2) * jnp.square(gv)
        m_hat = nm / (1.0 - B1 ** STEP)
        v_hat = nv / (1.0 - B2 ** STEP)
        d_ref[...] = -LR * (m_hat / (jnp.sqrt(v_hat) + AEPS) + WD * w_ref[...])
        nm_ref[...] = nm
        nv_ref[...] = nv

    blk = pl.BlockSpec((tr, cols), lambda i: (i, 0))
    out = pl.pallas_call(
        body, name=name, grid=(rows // tr,), in_specs=[blk] * 4, out_specs=[blk] * 3,
        out_shape=[jax.ShapeDtypeStruct((rows, cols), F32)] * 3, compiler_params=_cp("parallel"),
    )(w2, g2, m2, v2)
    return tuple(o.reshape(shape) for o in out)


def _layer_weights(w_in):
    return jnp.pad(w_in, ((0, 0), (0, N_PROJ - w_in.shape[1])))


def _layer_params(attn_norm, sinks, gcw, a_log, dt_bias, gnw, ffn_norm, fcw, fcb):
    lanes4 = lambda v: jnp.pad(v, (GH, 128 - 2 * GH))[None]
    return dict(attn_norm=attn_norm[None], sinks=jnp.pad(sinks, (0, 128 - HQ))[None], gcw=gcw, alog=lanes4(a_log),
                dtb=lanes4(dt_bias), gnw=gnw[None], ffn_norm=ffn_norm[None], fcw=fcw, fcb=fcb[None])


def _mixer_fwd(x, W, P, l, carry=None):
    n = lambda s: f"l{l}_{s}"
    h, proj, ba = _rmsnorm_proj(x, P["attn_norm"], W["all"], name=n("norm1_proj"))
    mixed = _swa_fwd(proj, P["sinks"], name=n("swa"))
    qkv_c = _gdn_prep_fwd(proj, P["gcw"], name=n("gdn_prep"))
    (o_raw, s_all, t_all), carried = _gdn_chunk_fwd(qkv_c, ba, P["alog"], P["dtb"], name=n("gdn_chunk"), carry=carry)
    mixed = _gdn_post_fwd(o_raw, proj, P["gnw"], mixed, name=n("gdn_post"))
    x1 = _mm(mixed, W["out"], res=x, name=n("out_proj"), mode="nn", dims=(T, D, D), tm=512, tn=D, tk=D)
    saved = dict(x=x, h=h, proj=proj, ba=ba, qkv_c=qkv_c, o_raw=o_raw, s_all=s_all, t_all=t_all, mixed=mixed, x1=x1)
    return x1, saved, carried


def _ffn_fwd(x1, W, P, l, carry=None):
    n = lambda s: f"l{l}_{s}"
    h2 = _rmsnorm_fwd(x1, P["ffn_norm"], name=n("norm2"))
    gu = _mm(h2, W["ffn"], name=n("ffn_in"), mode="nn", dims=(T, 2 * DFF, D), tm=512, tn=FFN_CW, tk=D, out_dtype=BF16,
             b_spec=pl.BlockSpec((None, D, FFN_CW), lambda i, j, k: (j, k, 0)), n_outer=True)
    act, carried = _ffn_act_fwd(gu, P["fcw"], P["fcb"], name=n("ffn_act"), carry=carry)
    x2 = _mm(act, W["down"], res=x1, name=n("ffn_down"), mode="nn", dims=(T, D, DFF), tm=512, tn=D, tk=DFF)
    return x2, dict(h2=h2, gu=gu, act=act), carried


def _ffn_bwd(dx2, sv, W, P, l):
    n = lambda s: f"l{l}_{s}"
    CW = FFN_CW
    dact = _mm(dx2, W["down"], name=n("d_act"), mode="nt", dims=(T, DFF, D), tm=512, tn=DFF // 2, tk=D, out_dtype=BF16,
               n_outer=True)
    g_down = _mm(sv["act"], dx2, name=n("g_down"), mode="tn", dims=(DFF, D, T), tm=DFF // 2, tn=D, tk=min(T, 2048), out_dtype=BF16)
    dgu, g_fcw, g_fcb = _ffn_act_bwd(sv["gu"], P["fcw"], P["fcb"], dact, name=n("d_ffn_act"))
    dx1, g_ffn_norm = _mm_rmsnorm_bwd(
        dgu, W["ffn"], sv["x1"], P["ffn_norm"], dx2, name=n("d_h2_norm2"), K=2 * DFF, tm=512, tk=CW,
        a_spec=pl.BlockSpec((None, 512, CW), lambda i, k: (k // 2, i, k % 2)),
        b_spec=pl.BlockSpec((None, D, CW), lambda i, k: (k, 0, 0)))
    g_ffn = _mm(sv["h2"], dgu, name=n("g_ffn"), mode="tn", dims=(D, 2 * DFF, T), tm=512, tn=CW, tk=T, n_outer=True,
                b_spec=pl.BlockSpec((None, T, CW), lambda i, j, k: (j // 2, k, j % 2)),
                out_spec=pl.BlockSpec((None, 512, CW), lambda i, j, k: (j, i, 0)),
                out_shape=jax.ShapeDtypeStruct((N_CHIP, D, CW), BF16), out_dtype=BF16)
    return dx1, dict(ffn_norm=g_ffn_norm[0], w_ffn_in=g_ffn, ffn_conv_w=g_fcw, ffn_conv_b=g_fcb[0], w_down=g_down)


def _mixer_bwd(dx1, sv, W, P, l, carry=None, last_hook=None):
    n = lambda s: f"l{l}_{s}"
    dmixed = _mm(dx1, W["out"], name=n("d_mixed"), mode="nt", dims=(T, D, D), tm=512, tn=D, tk=D)
    g_out = _mm(sv["mixed"], dx1, name=n("g_out"), mode="tn", dims=(D, D, T), tm=D, tn=D, tk=min(T, 2048), out_dtype=BF16)
    do_raw, dproj, g_gnw = _gdn_post_bwd(sv["o_raw"], sv["proj"], P["gnw"], dmixed, name=n("d_gdn_post"))
    (dqkv_c, dproj, g_alog, g_dtb), carried = _gdn_chunk_bwd(
        sv["qkv_c"], sv["ba"], P["alog"], P["dtb"], sv["s_all"], sv["t_all"], do_raw, dproj, name=n("d_gdn_chunk"), carry=carry)
    dproj, g_gcw = _gdn_prep_bwd(sv["proj"], P["gcw"], dqkv_c, dproj, name=n("d_gdn_prep"))
    dproj, g_sinks = _swa_bwd(sv["proj"], P["sinks"], dmixed, dproj, name=n("d_swa"))
    g_all = _mm(sv["h"], dproj, name=n("g_in"), mode="tn", dims=(D, N_PROJ, T), tm=512, tn=N_PROJ, tk=min(T, 2048), out_dtype=BF16)
    norm_w = P["attn_norm"] if last_hook is None else last_hook(dict(w_in=g_all, w_out=g_out))
    dx, g_attn_norm = _mm_rmsnorm_bwd(dproj, W["all"], sv["x"], norm_w, dx1, name=n("d_h_norm1"), K=N_PROJ,
                                      tm=512, tk=N_PROJ)
    grads = dict(attn_norm=g_attn_norm[0], w_in=g_all, attn_sinks=g_sinks[0, :HQ], gdn_conv_w=g_gcw,
                 gdn_a_log=g_alog[0, GH:2 * GH], gdn_dt_bias=g_dtb[0, GH:2 * GH], gdn_norm=g_gnw[0], w_out=g_out)
    return dx, grads, carried


def _pos():
    return lax.axis_index("x"), lax.axis_index("y"), lax.axis_index("c")


def _other_chips(x, y):
    return [(1 - x, y), (x, 1 - y), (1 - x, 1 - y)]


def _remote(src, dst, send_sems, recv_sems, k, to):
    return pltpu.make_async_remote_copy(src_ref=src, dst_ref=dst, send_sem=send_sems.at[k], recv_sem=recv_sems.at[k],
                                        device_id=to, device_id_type=MESH)


def _run_carry(carry, *, name):
    ci, co = len(carry.arrays), len(carry.out_shapes)

    def body(*refs):
        sends, recvs = carry.plan(refs[:ci], refs[ci:ci + co], refs[-2], refs[-1])
        for cp in sends:
            cp.start()
        for cp in recvs:
            cp.wait_recv()
        for cp in sends:
            cp.wait_send()

    return list(pl.pallas_call(
        body, name=name, in_specs=[ANY] * ci, out_specs=[ANY] * co, out_shape=carry.out_shapes,
        scratch_shapes=[pltpu.SemaphoreType.DMA((carry.nsem,)), pltpu.SemaphoreType.DMA((carry.nsem,))],
    )(*carry.arrays))


def _chip_index():
    return 2 * lax.axis_index("x") + lax.axis_index("y")


def _gather_carry(shards):
    def plan(srcs, outs, send_sems, recv_sems):
        x, y, c = _pos()
        chip = 2 * x + y
        others = [(k, px, py, 2 * px + py) for k, (px, py) in enumerate(_other_chips(x, y))]
        sends = [_remote(srcs[t], outs[t].at[chip], send_sems, recv_sems, 3 * t + k, (px, py, c))
                 for t in range(len(srcs)) for k, px, py, _ in others]
        recvs = [_remote(srcs[t], outs[t].at[j], send_sems, recv_sems, 3 * t + k, (x, y, c))
                 for t in range(len(srcs)) for k, _, _, j in others]
        return sends, recvs

    return _Carry(shards, [jax.ShapeDtypeStruct((N_CHIP,) + s.shape, s.dtype) for s in shards], 3 * len(shards), plan)


def _fill_own(outs, shards):
    return [lax.dynamic_update_index_in_dim(o, s, _chip_index(), 0) for o, s in zip(outs, shards)]


def _gather_halves(shards, after, *, name):
    nt = len(shards)

    def body(*refs):
        srcs, outs, send_sems, recv_sems = refs[:nt], refs[nt + 1:2 * nt + 1], refs[-2], refs[-1]
        x, y, c = _pos()
        me, sibling, chip = (x, y, c), (x, y, 1 - c), 2 * x + y
        others = [(k, px, py, 2 * px + py) for k, (px, py) in enumerate(_other_chips(x, y))]

        def rows(t, h):
            rh = srcs[t].shape[0] // 2
            return pl.ds(pl.multiple_of(h * rh, 16), rh)

        first = [_remote(srcs[t].at[rows(t, c), :], outs[t].at[chip, rows(t, c), :], send_sems, recv_sems, 3 * t + k, (px, py, c))
                 for t in range(nt) for k, px, py, _ in others]
        for cp in first:
            cp.start()
        passed = []
        for t in range(nt):
            for k, _, _, j in others:
                slot = outs[t].at[j, rows(t, c), :]
                _remote(slot, slot, send_sems, recv_sems, 3 * t + k, me).wait_recv()
                passed.append(_remote(slot, slot, send_sems, recv_sems, 3 * (nt + t) + k, sibling))
                passed[-1].start()
        for t in range(nt):
            for k, _, _, j in others:
                slot = outs[t].at[j, rows(t, 1 - c), :]
                _remote(slot, slot, send_sems, recv_sems, 3 * (nt + t) + k, me).wait_recv()
        for cp in first + passed:
            cp.wait_send()

    outs = pl.pallas_call(
        body, name=name, in_specs=[ANY] * (nt + 1), out_specs=[ANY] * nt,
        out_shape=[jax.ShapeDtypeStruct((N_CHIP,) + s.shape, s.dtype) for s in shards],
        scratch_shapes=[pltpu.SemaphoreType.DMA((6 * nt,)), pltpu.SemaphoreType.DMA((6 * nt,))],
    )(*shards, after)
    return _fill_own(outs, shards)


HBM_SPEC = pl.BlockSpec(memory_space=pltpu.HBM)
SEM_SPEC = pl.BlockSpec(memory_space=pltpu.SEMAPHORE)
DATAFLOW = pltpu.SideEffectType.DATAFLOW_SIDE_EFFECTING


def _split_start(carry, after, *, name):
    ci, co = len(carry.arrays), len(carry.out_shapes)

    def body(*refs):
        srcs, lands, send_sems, recv_sems, token = refs[:ci], refs[ci:ci + co], refs[ci + co + 1], refs[ci + co + 2], refs[-1]
        for cp in carry.plan(srcs, lands, send_sems, recv_sems)[0]:
            cp.start()
        token[...] = jnp.zeros_like(token)

    lands = [lax.empty(s.shape, s.dtype) for s in carry.out_shapes]
    hbm = lambda a: pltpu.with_memory_space_constraint(a, pltpu.HBM)
    out = pl.pallas_call(
        body, name=name, in_specs=[HBM_SPEC] * (ci + co) + [ANY],
        out_specs=[SEM_SPEC, SEM_SPEC] + [HBM_SPEC] * (ci + co) + [VMEM_SPEC],
        out_shape=[pltpu.SemaphoreType.DMA((carry.nsem,)), pltpu.SemaphoreType.DMA((carry.nsem,))]
        + [pltpu.HBM(a.shape, a.dtype) for a in carry.arrays + lands] + [jax.ShapeDtypeStruct((8, 128), F32)],
        input_output_aliases={t: 2 + t for t in range(ci + co)},
        compiler_params=pltpu.CompilerParams(has_side_effects=DATAFLOW),
    )(*[hbm(a) for a in carry.arrays + lands], after)
    return out[0], out[1], list(out[2:2 + ci]), list(out[2 + ci:2 + ci + co]), out[-1]


def _split_wait(carry, send_sems, recv_sems, srcs, lands, after, *, name):
    ci, co = len(srcs), len(lands)

    def body(*refs):
        sends, recvs = carry.plan(refs[:ci], refs[ci:ci + co], refs[ci + co], refs[ci + co + 1])
        for cp in sends:
            cp.wait_send()
        for cp in recvs:
            cp.wait_recv()

    out = pl.pallas_call(
        body, name=name, in_specs=[HBM_SPEC] * (ci + co) + [SEM_SPEC, SEM_SPEC, ANY], out_specs=[HBM_SPEC] * (ci + co),
        out_shape=[pltpu.HBM(a.shape, a.dtype) for a in list(srcs) + list(lands)],
        input_output_aliases={t: t for t in range(ci + co)},
        compiler_params=pltpu.CompilerParams(has_side_effects=DATAFLOW),
    )(*srcs, *lands, send_sems, recv_sems, after)
    return list(out[:ci]), list(out[ci:])


def _gather_start(shards, after, *, name):
    return _split_start(_gather_carry(shards), after, name=name)


def _gather_wait(send_sems, recv_sems, shards, lands, after, *, name):
    srcs, got = _split_wait(_gather_carry(shards), send_sems, recv_sems, shards, lands, after, name=name)
    return _fill_own(got, srcs)


def _ag_small(v, *, name):
    m, n = v.shape

    def body(x_ref, out_ref, red_ref, send_sems, recv_sems, local_sem):
        x, y, c = _pos()
        me, sibling = (x, y, c), (x, y, 1 - c)
        chips = _other_chips(x, y)
        rows = lambda px, py, pc: out_ref.at[pl.ds(pl.multiple_of((4 * px + 2 * py + pc) * m, 8), m), :]
        mine = pltpu.make_async_copy(x_ref, rows(*me), local_sem)
        mine.start()
        first = [_remote(x_ref, rows(*me), send_sems, recv_sems, 0, sibling)]
        first += [_remote(x_ref, rows(*me), send_sems, recv_sems, 1 + k, (*chip, c)) for k, chip in enumerate(chips)]
        for cp in first:
            cp.start()
        passed = [_remote(rows(*chip, c), rows(*chip, c), send_sems, recv_sems, 4 + k, sibling) for k, chip in enumerate(chips)]
        for k, chip in enumerate(chips):
            _remote(rows(*chip, c), rows(*chip, c), send_sems, recv_sems, 1 + k, me).wait_recv()
            passed[k].start()
        _remote(rows(*sibling), rows(*sibling), send_sems, recv_sems, 0, me).wait_recv()
        for k, chip in enumerate(chips):
            _remote(rows(*chip, 1 - c), rows(*chip, 1 - c), send_sems, recv_sems, 4 + k, me).wait_recv()
        for cp in first + passed:
            cp.wait_send()
        mine.wait()
        acc = out_ref[0:m, :]
        for d in range(1, N_DEV):
            acc = acc + out_ref[d * m:(d + 1) * m, :]
        red_ref[...] = acc

    return pl.pallas_call(
        body, name=name, in_specs=[VMEM_SPEC], out_specs=[VMEM_SPEC, VMEM_SPEC],
        out_shape=[jax.ShapeDtypeStruct((N_DEV * m, n), v.dtype), jax.ShapeDtypeStruct((m, n), v.dtype)],
        scratch_shapes=[pltpu.SemaphoreType.DMA((7,)), pltpu.SemaphoreType.DMA((7,)), pltpu.SemaphoreType.DMA],
    )(v)


def _halves(ref, c):
    rh = ref.shape[1] // 2
    return ref.at[:, pl.ds(pl.multiple_of(c * rh, 16), rh), :]


def _rs_swap(gs, *, name):
    nt = len(gs)

    def body(*refs):
        g, theirs = refs[:nt], refs[nt:2 * nt]
        send_sems, recv_sems = refs[2 * nt:]
        x, y, c = _pos()
        swaps = [_remote(_halves(g[t], 1 - c), theirs[t], send_sems, recv_sems, t, (x, y, 1 - c)) for t in range(nt)]
        for cp in swaps:
            cp.start()
        for cp in swaps:
            cp.wait()

    return pl.pallas_call(
        body, name=name, in_specs=[ANY] * nt, out_specs=[ANY] * nt,
        out_shape=[jax.ShapeDtypeStruct((a.shape[0], a.shape[1] // 2, a.shape[2]), a.dtype) for a in gs],
        scratch_shapes=[pltpu.SemaphoreType.DMA((nt,)), pltpu.SemaphoreType.DMA((nt,))],
    )(*gs)


def _exchange_carry(ss):
    def plan(s, out, send_sems, recv_sems):
        x, y, c = _pos()
        chip = 2 * x + y
        others = [(k, px, py, 2 * px + py) for k, (px, py) in enumerate(_other_chips(x, y))]
        sends = [_remote(s[t].at[j], out[t].at[chip], send_sems, recv_sems, 3 * t + k, (px, py, c))
                 for t in range(len(s)) for k, px, py, j in others]
        recvs = [_remote(s[t].at[j], out[t].at[j], send_sems, recv_sems, 3 * t + k, (x, y, c))
                 for t in range(len(s)) for k, _, _, j in others]
        return sends, recvs

    return _Carry(ss, [jax.ShapeDtypeStruct(a.shape, a.dtype) for a in ss], 3 * len(ss), plan)


def _fill_own_slab(outs, ss):
    chip = _chip_index()
    return [lax.dynamic_update_index_in_dim(o, lax.dynamic_index_in_dim(s, chip, 0, keepdims=False), chip, 0)
            for o, s in zip(outs, ss)]


def _rs_join(rs, *, name):
    nt = len(rs)

    def body(*refs):
        r, theirs = refs[:nt], refs[nt:2 * nt]
        send_sems, recv_sems = refs[2 * nt:]
        x, y, c = _pos()
        swaps = [_remote(r[t], theirs[t], send_sems, recv_sems, t, (x, y, 1 - c)) for t in range(nt)]
        for cp in swaps:
            cp.start()
        for cp in swaps:
            cp.wait()

    theirs = pl.pallas_call(
        body, name=name, in_specs=[ANY] * nt, out_specs=[ANY] * nt,
        out_shape=[jax.ShapeDtypeStruct(a.shape, a.dtype) for a in rs],
        scratch_shapes=[pltpu.SemaphoreType.DMA((nt,)), pltpu.SemaphoreType.DMA((nt,))],
    )(*rs)
    first = lax.axis_index("c") == 0
    return [jnp.concatenate([jnp.where(first, a, b), jnp.where(first, b, a)], axis=0) for a, b in zip(rs, theirs)]


def _row_tile(rows, dtype):
    unit = 16 if dtype == BF16 else 8
    if rows <= 512:
        return rows
    return next((t for t in (512, 256, 128, 64, 32, 16, 8) if rows % t == 0 and t % unit == 0), rows)


def _add_pair(g, b, half, *, name):
    n, rh, cols = b.shape
    tr = _row_tile(rh, BF16)
    nb = rh // tr

    def body(half_ref, g_ref, b_ref, o_ref):
        o_ref[...] = (g_ref[...].astype(F32) + b_ref[...].astype(F32)).astype(BF16)

    blk = pl.BlockSpec((None, tr, cols), lambda j, i, h: (j, i, 0))
    return pl.pallas_call(
        body, name=name,
        grid_spec=pltpu.PrefetchScalarGridSpec(
            num_scalar_prefetch=1, grid=(n, nb),
            in_specs=[pl.BlockSpec((None, tr, cols), lambda j, i, h: (j, h[0] * nb + i, 0)), blk], out_specs=blk),
        out_shape=jax.ShapeDtypeStruct(b.shape, BF16), compiler_params=_cp("parallel", "parallel"),
    )(half, g, b)


def _sum_chips(b, *, name):
    n, rh, cols = b.shape
    tr = _row_tile(rh, BF16)

    def body(b_ref, o_ref):
        acc = b_ref[0].astype(F32)
        for j in range(1, n):
            acc = acc + b_ref[j].astype(F32)
        o_ref[...] = acc

    return pl.pallas_call(
        body, name=name, grid=(rh // tr,), in_specs=[pl.BlockSpec((n, tr, cols), lambda i: (0, i, 0))],
        out_specs=pl.BlockSpec((tr, cols), lambda i: (i, 0)),
        out_shape=jax.ShapeDtypeStruct((rh, cols), F32), compiler_params=_cp("parallel"),
    )(b)


BIG = ("w_in", "w_out", "w_ffn_in", "w_down")
W_IN_SHARD = (N_MAIN + 2 * GH) // N_CHIP


def _chip_major(name, g):
    if name == "w_in":
        return jnp.stack([g[:, j * W_IN_SHARD:(j + 1) * W_IN_SHARD] for j in range(N_CHIP)])
    if name == "w_ffn_in":
        return g
    return g.reshape(N_CHIP, g.shape[0] // N_CHIP, g.shape[1])


def _rs_pairs(gs, tag):
    half = lax.axis_index("c").astype(jnp.int32).reshape(1)
    theirs = _rs_swap(gs, name=f"{tag}_swap")
    return [_add_pair(a, b, half, name=f"{tag}_add{i}") for i, (a, b) in enumerate(zip(gs, theirs))]


def _rs_sums(got, pairs, tag):
    return [_sum_chips(b, name=f"{tag}_sum{i}") for i, b in enumerate(_fill_own_slab(got, pairs))]


def _size(shape):
    n = 1
    for s in shape:
        n *= s
    return n


def _pack_flat(parts, total, dtype):
    flat = jnp.concatenate([p.reshape(-1).astype(dtype) for p in parts])
    return jnp.pad(flat, (0, total - flat.shape[0]))


def _unpack_flat(flat, shapes):
    out, o = [], 0
    for s in shapes:
        out.append(flat[o:o + _size(s)].reshape(s))
        o += _size(s)
    return out


WEIGHTS = ("attn_norm", "w_in", "attn_sinks", "gdn_conv_w", "gdn_a_log", "gdn_dt_bias", "gdn_norm", "w_out", "ffn_norm",
           "w_ffn_in", "ffn_conv_w", "ffn_conv_b", "w_down", "final_norm")
SMALL = {"attn_norm": (DEPTH, D), "attn_sinks": (DEPTH, HQ), "gdn_a_log": (DEPTH, GH), "gdn_dt_bias": (DEPTH, GH),
         "gdn_norm": (DEPTH, GD), "ffn_norm": (DEPTH, D), "ffn_conv_b": (DEPTH, DFF), "final_norm": (D,)}
CONV_FULL = {"gdn_conv_w": (DEPTH, GK, 3 * GH * GD), "ffn_conv_w": (DEPTH, FK, DFF)}
CONV_SHARD = {"gdn_conv_w": (DEPTH, GK, 3 * GH * GD // N_CHIP), "ffn_conv_w": (DEPTH, FK, DFF // N_CHIP)}
CONV_ROWS, SMALLG_ROWS, SMALLW_ROWS = 64, 320, 144


def kernel(x, attn_norm, w_in, attn_sinks, gdn_conv_w, gdn_a_log, gdn_dt_bias, gdn_norm, w_out, ffn_norm, w_ffn_in, ffn_conv_w, ffn_conv_b, w_down, final_norm, loss_target, m_attn_norm, m_w_in, m_attn_sinks, m_gdn_conv_w, m_gdn_a_log, m_gdn_dt_bias, m_gdn_norm, m_w_out, m_ffn_norm, m_w_ffn_in, m_ffn_conv_w, m_ffn_conv_b, m_w_down, m_final_norm, v_attn_norm, v_w_in, v_attn_sinks, v_gdn_conv_w, v_gdn_a_log, v_gdn_dt_bias, v_gdn_norm, v_w_out, v_ffn_norm, v_w_ffn_in, v_ffn_conv_w, v_ffn_conv_b, v_w_down, v_final_norm):
    w = dict(zip(WEIGHTS, (attn_norm, w_in, attn_sinks, gdn_conv_w, gdn_a_log, gdn_dt_bias, gdn_norm, w_out, ffn_norm,
                           w_ffn_in, ffn_conv_w, ffn_conv_b, w_down, final_norm)))
    m = dict(zip(WEIGHTS, (m_attn_norm, m_w_in, m_attn_sinks, m_gdn_conv_w, m_gdn_a_log, m_gdn_dt_bias, m_gdn_norm, m_w_out,
                           m_ffn_norm, m_w_ffn_in, m_ffn_conv_w, m_ffn_conv_b, m_w_down, m_final_norm)))
    v = dict(zip(WEIGHTS, (v_attn_norm, v_w_in, v_attn_sinks, v_gdn_conv_w, v_gdn_a_log, v_gdn_dt_bias, v_gdn_norm, v_w_out,
                           v_ffn_norm, v_w_ffn_in, v_ffn_conv_w, v_ffn_conv_b, v_w_down, v_final_norm)))
    cx, cy, _ = _pos()
    chip = 2 * cx + cy

    cpack = _pack_flat([w[n] for n in CONV_SHARD], CONV_ROWS * 128, F32).reshape(CONV_ROWS, 128)
    cgath, _ = _ag_small(cpack, name="gather_conv_w")
    cgath = cgath.reshape(N_DEV, CONV_ROWS * 128)
    cshards = [_unpack_flat(cgath[2 * j], list(CONV_SHARD.values())) for j in range(N_CHIP)]
    conv = {n: jnp.concatenate([cshards[j][i] for j in range(N_CHIP)], axis=2) for i, n in enumerate(CONV_SHARD)}

    Ps = [_layer_params(attn_norm[l], attn_sinks[l], conv["gdn_conv_w"][l], gdn_a_log[l], gdn_dt_bias[l], gdn_norm[l],
                        ffn_norm[l], conv["ffn_conv_w"][l], ffn_conv_b[l]) for l in range(DEPTH)]

    wb = {n: [w[n][l].astype(BF16) for l in range(DEPTH)] for n in BIG}
    mixer_w = lambda g_in, g_out: dict(all=_layer_weights(jnp.concatenate(list(g_in), axis=1)), out=g_out.reshape(D, D))
    ffn_w = lambda g_ffn, g_down: dict(ffn=g_ffn, down=g_down.reshape(DFF, D))
    groups = [[wb[a][l], wb[b][l]] for l in range(DEPTH) for a, b in (("w_in", "w_out"), ("w_ffn_in", "w_down"))]
    def after_start(P, key, started):
        return {**P, key: P[key] + started[4][:1, :1]}

    got0 = _gather_halves(groups[0], cgath, name="gather0")
    g1 = _gather_start(groups[1], got0[1], name="gather1_start")
    Wm0 = mixer_w(*got0)
    h, sv0, _ = _mixer_fwd(x[0], Wm0, after_start(Ps[0], "attn_norm", g1), 0)
    got1 = _gather_wait(*g1[:4], h, name="gather1_wait")
    g2 = _gather_start(groups[2], got1[1], name="gather2_start")
    Wf0 = ffn_w(*got1)
    h, sv0f, _ = _ffn_fwd(h, Wf0, after_start(Ps[0], "ffn_norm", g2), 0)
    got2 = _gather_wait(*g2[:4], h, name="gather2_wait")
    g3 = _gather_start(groups[3], got2[1], name="gather3_start")
    Wm1 = mixer_w(*got2)
    h, sv1, _ = _mixer_fwd(h, Wm1, after_start(Ps[1], "attn_norm", g3), 1)
    Wf1 = ffn_w(*_gather_wait(*g3[:4], h, name="gather3_wait"))
    h, sv1f, _ = _ffn_fwd(h, Wf1, Ps[1], 1)
    dx, g_final, loss_part = _loss_head(h, final_norm[None], loss_target[0], name="loss_head")

    def exchange_start(pairs, tag):
        carry = _exchange_carry(pairs)
        return carry, _split_start(carry, pairs[0], name=f"{tag}_exchange_start")

    def exchange_wait(carry, started, after, tag):
        srcs, got = _split_wait(carry, *started[:4], after, name=f"{tag}_exchange_wait")
        return _rs_sums(got, srcs, tag)

    tied = lambda P, key, started: {**P, key: P[key] + started[4][:1, :1]}
    dx, gf1 = _ffn_bwd(dx, {**sv1, **sv1f}, Wf1, Ps[1], 1)
    dx, gm1, _ = _mixer_bwd(dx, sv1, Wm1, Ps[1], 1)
    lg1 = {**gf1, **gm1}
    pairs1 = _rs_pairs([_chip_major(n, lg1[n]) for n in BIG], "rs1")
    ex1 = exchange_start(pairs1, "rs1")
    dx, gf0 = _ffn_bwd(dx, {**sv0, **sv0f}, Wf0, tied(Ps[0], "fcb", ex1[1]), 0)
    pairs0a = _rs_pairs([_chip_major(n, gf0[n]) for n in BIG[2:]], "rs0a")
    ex0a = exchange_start(pairs0a, "rs0a")
    last = {}

    def send_last(gm):
        pairs = _rs_pairs([_chip_major(n, gm[n]) for n in BIG[:2]], "rs0b")
        last["ex"] = exchange_start(pairs, "rs0b")
        return tied(Ps[0], "attn_norm", last["ex"][1])["attn_norm"]

    dx, gm0, _ = _mixer_bwd(dx, sv0, Wm0, tied(Ps[0], "gnw", ex0a[1]), 0, last_hook=send_last)
    lg = [{**gf0, **gm0}, lg1]
    sums0b = exchange_wait(*last["ex"], dx, "rs0b")
    sums1 = exchange_wait(*ex1, sums0b[0], "rs1")
    sums0a = exchange_wait(*ex0a, sums1[0], "rs0a")
    joined = _rs_join(sums0b + sums0a + sums1, name="rs_join")
    reduced = [joined[:4], joined[4:]]
    grad_x = dx[None]
    stacked = lambda n: jnp.stack([lg[l][n] for l in range(DEPTH)])

    small_parts = [g_final[0] if n == "final_norm" else stacked(n) for n in SMALL] + [stacked(n) for n in CONV_FULL]
    svec = _pack_flat(small_parts + [loss_part[0, :1]], SMALLG_ROWS * 128, F32).reshape(SMALLG_ROWS, 128)
    _, sred = _ag_small(svec, name="reduce_small")
    small_g = _unpack_flat(sred.reshape(-1), list(SMALL.values()) + list(CONV_FULL.values()) + [(1,)])
    g = dict(zip(list(SMALL) + list(CONV_FULL), small_g[:-1]))
    loss = small_g[-1][0]
    for n in CONV_FULL:
        wd = CONV_SHARD[n][2]
        g[n] = lax.dynamic_slice_in_dim(g[n], chip * wd, wd, axis=2)

    g.update({n: jnp.stack([reduced[l][i] for l in range(DEPTH)]) for i, n in enumerate(BIG)})

    delta, new_m, new_v = {}, {}, {}
    for n in BIG:
        delta[n], new_m[n], new_v[n] = _adamw(w[n], g[n], m[n], v[n], name=f"adamw_{n}")
    small_names = list(SMALL) + list(CONV_SHARD)
    small_shapes = list(SMALL.values()) + list(CONV_SHARD.values())
    packed = [_pack_flat([t[n] for n in small_names], SMALLW_ROWS * 128, F32).reshape(SMALLW_ROWS, 128) for t in (w, g, m, v)]
    for res, out in zip(_adamw(*packed, name="adamw_small"), (delta, new_m, new_v)):
        out.update(zip(small_names, _unpack_flat(res.reshape(-1), small_shapes)))

    return (loss, grad_x, *[g[n] for n in WEIGHTS], *[delta[n] for n in WEIGHTS], *[new_m[n] for n in WEIGHTS],
            *[new_v[n] for n in WEIGHTS])
```

```python
import functools

import jax
import jax.numpy as jnp
import numpy as np
from jax import lax
from jax.experimental import pallas as pl
from jax.experimental.pallas import tpu as pltpu

F32, BF16 = jnp.float32, jnp.bfloat16
HIGHEST = lax.Precision.HIGHEST
MESH = pl.DeviceIdType.MESH

D = 1024
T = 4096
DEPTH = 2
HQ, HKV, DH, WIN = 8, 2, 64, 128
GH, GD, GC, GK = 4, 128, 64, 4
DFF, FK = 2816, 3
EPS = 1e-6
N_MAIN = 2816
N_PROJ = N_MAIN + 128
COL_K, COL_V, COL_G, COL_Z, COL_BA = 4, 5, 6, 18, 22
N_CHUNK = T // GC
N_DEV, N_CHIP = 8, 4
FFN_CW = 2 * DFF // N_CHIP
LR, B1, B2, AEPS, WD, STEP = 0.001, 0.9, 0.999, 1e-08, 0.01, 10

VMEM_LIMIT = 56 * 1024 * 1024
ANY = pl.BlockSpec(memory_space=pl.ANY)
VMEM_SPEC = pl.BlockSpec(memory_space=pltpu.VMEM)


def _cp(*sem):
    return pltpu.CompilerParams(dimension_semantics=sem if sem else None, vmem_limit_bytes=VMEM_LIMIT)


class _Carry:
    def __init__(self, arrays, out_shapes, nsem, plan):
        self.arrays, self.out_shapes, self.nsem, self.plan = list(arrays), list(out_shapes), nsem, plan


def _seq_call(body, carry, *, name, steps, in_specs, out_specs, out_shape, args, scratch_shapes=(), aliases=None):
    in_specs, out_specs, out_shape, scratch = list(in_specs), list(out_specs), list(out_shape), list(scratch_shapes)
    n_in, n_out, n_scr = len(in_specs), len(out_specs), len(scratch)
    if carry is None:
        fn = body
    else:
        ci, co = len(carry.arrays), len(carry.out_shapes)

        def fn(*refs):
            ins, cins = refs[:n_in], refs[n_in:n_in + ci]
            outs, couts = refs[n_in + ci:n_in + ci + n_out], refs[n_in + ci + n_out:n_in + ci + n_out + co]
            scr, (ssem, rsem) = refs[n_in + ci + n_out + co:-2], refs[-2:]
            sends, recvs = carry.plan(cins, couts, ssem, rsem)

            @pl.when(pl.program_id(0) == 0)
            def _():
                for cp in sends:
                    cp.start()

            body(*ins, *outs, *scr)

            @pl.when(pl.program_id(0) == steps - 1)
            def _():
                for cp in recvs:
                    cp.wait_recv()
                for cp in sends:
                    cp.wait_send()

        in_specs += [ANY] * ci
        out_specs += [ANY] * co
        out_shape += carry.out_shapes
        scratch += [pltpu.SemaphoreType.DMA((carry.nsem,)), pltpu.SemaphoreType.DMA((carry.nsem,))]
        args = list(args) + carry.arrays
    outs = pl.pallas_call(
        fn, name=name, grid=(steps,), in_specs=in_specs, out_specs=out_specs, out_shape=out_shape,
        scratch_shapes=scratch, input_output_aliases=aliases or {}, compiler_params=_cp("arbitrary"),
    )(*args)
    return list(outs[:n_out]), list(outs[n_out:])


def _dot(a, b):
    return jnp.dot(a.astype(BF16), b.astype(BF16), preferred_element_type=F32)


def _dot_nt(a, b):
    return lax.dot_general(a.astype(BF16), b.astype(BF16), (((1,), (1,)), ((), ())), preferred_element_type=F32)


def _dot_tn(a, b):
    return lax.dot_general(a.astype(BF16), b.astype(BF16), (((0,), (0,)), ((), ())), preferred_element_type=F32)


def _iota(shape, dim):
    return lax.broadcasted_iota(jnp.int32, shape, dim)


def _col(x, idx):
    return jnp.sum(jnp.where(_iota(x.shape, 1) == idx, x, 0.0), axis=1, keepdims=True)


def _silu(y):
    return y * jax.nn.sigmoid(y)


_MM_DN = {"nn": (((1,), (0,)), ((), ())), "nt": (((1,), (1,)), ((), ())), "tn": (((0,), (0,)), ((), ()))}


def _mm(a, b, *, name, mode, dims, tm, tn, tk, res=None, out_dtype=F32, a_spec=None, b_spec=None, out_spec=None,
        out_shape=None, n_outer=False):
    M, N, K = dims
    assert M % tm == 0 and N % tn == 0 and K % tk == 0, (name, dims)
    nk = K // tk
    dn = _MM_DN[mode]

    def body(*refs):
        a_ref, b_ref = refs[:2]
        r_ref = None if res is None else refs[2]
        o_ref = refs[2 if res is None else 3]
        part = lax.dot_general(a_ref[...].astype(BF16), b_ref[...].astype(BF16), dn, preferred_element_type=F32)
        if nk == 1:
            o_ref[...] = (part if res is None else part + r_ref[...]).astype(out_dtype)
            return
        acc, k = refs[-1], pl.program_id(2)

        @pl.when(k == 0)
        def _():
            acc[...] = part

        @pl.when(k > 0)
        def _():
            acc[...] += part

        @pl.when(k == nk - 1)
        def _():
            r = acc[...]
            if res is not None:
                r = r + r_ref[...]
            o_ref[...] = r.astype(out_dtype)

    if a_spec is None:
        a_spec = pl.BlockSpec((tk, tm), lambda i, j, k: (k, i)) if mode == "tn" else pl.BlockSpec((tm, tk), lambda i, j, k: (i, k))
    if b_spec is None:
        b_spec = pl.BlockSpec((tn, tk), lambda i, j, k: (j, k)) if mode == "nt" else pl.BlockSpec((tk, tn), lambda i, j, k: (k, j))
    in_specs, args = [a_spec, b_spec], [a, b]
    if res is not None:
        in_specs.append(pl.BlockSpec((tm, tn), lambda i, j, k: (i, j)))
        args.append(res)
    out_spec = out_spec or pl.BlockSpec((tm, tn), lambda i, j, k: (i, j))
    grid = (M // tm, N // tn, nk)
    if n_outer:
        swap = lambda s: pl.BlockSpec(s.block_shape, lambda j, i, k, f=s.index_map: f(i, j, k))
        in_specs, out_spec, grid = [swap(s) for s in in_specs], swap(out_spec), (N // tn, M // tm, nk)
    return pl.pallas_call(
        body, name=name, grid=grid, in_specs=in_specs, out_specs=out_spec,
        out_shape=out_shape or jax.ShapeDtypeStruct((M, N), out_dtype),
        scratch_shapes=[pltpu.VMEM((tm, tn), F32)] if nk > 1 else [],
        compiler_params=_cp("parallel", "parallel", "arbitrary"),
    )(*args)


def _rms(x, w):
    return x * lax.rsqrt(jnp.mean(x * x, axis=-1, keepdims=True) + EPS) * w


def _rmsnorm_fwd(x, w, *, name, tm=512):
    def body(x_ref, w_ref, o_ref):
        o_ref[...] = _rms(x_ref[...], w_ref[...]).astype(BF16)

    return pl.pallas_call(
        body, name=name, grid=(x.shape[0] // tm,),
        in_specs=[pl.BlockSpec((tm, D), lambda i: (i, 0)), pl.BlockSpec((1, D), lambda i: (0, 0))],
        out_specs=pl.BlockSpec((tm, D), lambda i: (i, 0)),
        out_shape=jax.ShapeDtypeStruct(x.shape, BF16), compiler_params=_cp("parallel"),
    )(x, w)


def _rmsnorm_proj(x, w, w_all, *, name, tm=512):
    def body(x_ref, w_ref, b_ref, h_ref, p_ref, ba_ref):
        h = _rms(x_ref[...], w_ref[...]).astype(BF16)
        h_ref[...] = h
        r = jnp.dot(h, b_ref[...], preferred_element_type=F32)
        p_ref[...] = r[:, :N_MAIN].astype(BF16)
        ba_ref[...] = r[:, N_MAIN:]

    row = lambda n: pl.BlockSpec((tm, n), lambda i: (i, 0))
    return pl.pallas_call(
        body, name=name, grid=(x.shape[0] // tm,),
        in_specs=[row(D), pl.BlockSpec((1, D), lambda i: (0, 0)), pl.BlockSpec((D, N_PROJ), lambda i: (0, 0))],
        out_specs=[row(D), row(N_MAIN), row(N_PROJ - N_MAIN)],
        out_shape=[jax.ShapeDtypeStruct((x.shape[0], D), BF16), jax.ShapeDtypeStruct((x.shape[0], N_MAIN), BF16),
                   jax.ShapeDtypeStruct((x.shape[0], N_PROJ - N_MAIN), F32)],
        compiler_params=_cp("parallel"),
    )(x, w, w_all)


def _mm_rmsnorm_bwd(a, b, x, w, dres, *, name, K, tm, tk, a_spec=None, b_spec=None):
    M = x.shape[0]
    nk = K // tk
    assert M % tm == 0 and K % tk == 0, (name, M, K)

    def body(a_ref, b_ref, x_ref, w_ref, dr_ref, dx_ref, dw_ref, acc):
        i, k = pl.program_id(0), pl.program_id(1)
        part = lax.dot_general(a_ref[...].astype(BF16), b_ref[...].astype(BF16), _MM_DN["nt"], preferred_element_type=F32)

        @pl.when(k == 0)
        def _():
            acc[...] = part

        @pl.when(k > 0)
        def _():
            acc[...] += part

        @pl.when((i == 0) & (k == 0))
        def _():
            dw_ref[...] = jnp.zeros_like(dw_ref)

        @pl.when(k == nk - 1)
        def _():
            _, vjp = jax.vjp(_rms, x_ref[...], w_ref[...])
            dx, dw = vjp(acc[...])
            dx_ref[...] = dx + dr_ref[...]
            dw_ref[...] += dw

    row = pl.BlockSpec((tm, D), lambda i, k: (i, 0))
    vec = pl.BlockSpec((1, D), lambda i, k: (0, 0))
    return pl.pallas_call(
        body, name=name, grid=(M // tm, nk),
        in_specs=[a_spec or pl.BlockSpec((tm, tk), lambda i, k: (i, k)), b_spec or pl.BlockSpec((D, tk), lambda i, k: (0, k)),
                  row, vec, row],
        out_specs=[row, vec], out_shape=[jax.ShapeDtypeStruct((M, D), F32), jax.ShapeDtypeStruct((1, D), F32)],
        scratch_shapes=[pltpu.VMEM((tm, D), F32)], compiler_params=_cp("arbitrary", "arbitrary"),
    )(a, b, x, w, dres)


def _rmsnorm_bwd(x, w, dh, dres, *, name, tm=512):
    def body(x_ref, w_ref, dh_ref, dr_ref, dx_ref, dw_ref):
        _, vjp = jax.vjp(_rms, x_ref[...], w_ref[...])
        dx, dw = vjp(dh_ref[...])
        dx_ref[...] = dx + dr_ref[...]

        @pl.when(pl.program_id(0) == 0)
        def _():
            dw_ref[...] = jnp.zeros_like(dw_ref)

        dw_ref[...] += dw

    row = pl.BlockSpec((tm, D), lambda i: (i, 0))
    vec = pl.BlockSpec((1, D), lambda i: (0, 0))
    return pl.pallas_call(
        body, name=name, grid=(x.shape[0] // tm,), in_specs=[row, vec, row, row], out_specs=[row, vec],
        out_shape=[jax.ShapeDtypeStruct(x.shape, F32), jax.ShapeDtypeStruct((1, D), F32)],
        compiler_params=_cp("arbitrary"),
    )(x, w, dh, dres)


def _loss_head(x, w, tgt, *, name, tm=512):
    def f(xv, wv, tv):
        err = _rms(xv, wv) - tv
        per_row = jnp.sum(err * err, axis=1, keepdims=True) * (0.5 / D)
        return jnp.sum(per_row, axis=0, keepdims=True)

    def body(x_ref, w_ref, t_ref, dx_ref, dw_ref, loss_ref):
        tv = t_ref[...]
        loss, vjp = jax.vjp(lambda xv, wv: f(xv, wv, tv), x_ref[...], w_ref[...])
        dx, dw = vjp(jnp.ones((1, 1), F32))
        dx_ref[...] = dx

        @pl.when(pl.program_id(0) == 0)
        def _():
            dw_ref[...] = jnp.zeros_like(dw_ref)
            loss_ref[...] = jnp.zeros_like(loss_ref)

        dw_ref[...] += dw
        loss_ref[...] += jnp.broadcast_to(loss, loss_ref.shape)

    row = pl.BlockSpec((tm, D), lambda i: (i, 0))
    vec = pl.BlockSpec((1, D), lambda i: (0, 0))
    return pl.pallas_call(
        body, name=name, grid=(x.shape[0] // tm,), in_specs=[row, vec, row],
        out_specs=[row, vec, pl.BlockSpec((1, 128), lambda i: (0, 0))],
        out_shape=[jax.ShapeDtypeStruct(x.shape, F32), jax.ShapeDtypeStruct((1, D), F32),
                   jax.ShapeDtypeStruct((1, 128), F32)],
        compiler_params=_cp("arbitrary"),
    )(x, w, tgt)


def _swa_bias():
    G = HQ // HKV
    r, c = np.arange(G * WIN)[:, None], np.arange(2 * WIN)[None, :]
    rel = (r % WIN) + WIN - c
    band = (rel >= 0) & (rel < WIN)
    out = np.empty((2, HKV, G * WIN, 2 * WIN), np.float32)
    for h2 in range(HKV):
        slope = 2.0 ** (-8.0 * (h2 * G + r // WIN + 1) / HQ)
        out[0, h2] = np.where(band & (c >= WIN), -slope * rel, -1e30)
        out[1, h2] = np.where(band, -slope * rel, -1e30)
    return jnp.asarray(out)


def _swa_heads(qs, k2s, v2s, sk, biases, hqs=range(HQ)):
    first_kv = hqs[0] // (HQ // HKV)
    kv = lambda xs: [xs[hq // (HQ // HKV) - first_kv] for hq in hqs]
    ss = _hmap(lambda q, k2, b: _dot_nt(q, k2) * (DH ** -0.5) + b, qs, kv(k2s), biases)
    sinks = [_col(sk, hq) for hq in hqs]
    ms = _hmap(lambda s, sink: lax.stop_gradient(jnp.maximum(jnp.max(s, axis=1, keepdims=True), sink)), ss, sinks)
    ps = _hmap(lambda s, m: jnp.exp(s - m), ss, ms)
    dens = _hmap(lambda p, sink, m: jnp.sum(p, axis=1, keepdims=True) + jnp.exp(sink - m), ps, sinks, ms)
    return _hmap(lambda p, den, v2: _dot(p * (1.0 / den), v2), ps, dens, kv(v2s))


def _swa_load(q_ref, kp_ref, kc_ref, vp_ref, vc_ref, bias_ref):
    q, G = q_ref[...].astype(F32), HQ // HKV
    qs = [q[:, hq * DH:(hq + 1) * DH] for hq in range(HQ)]
    both = lambda a, b, h2: jnp.concatenate([a[:, h2 * DH:(h2 + 1) * DH], b[:, h2 * DH:(h2 + 1) * DH]], axis=0).astype(F32)
    k2s = [both(kp_ref, kc_ref, h2) for h2 in range(HKV)]
    v2s = [both(vp_ref, vc_ref, h2) for h2 in range(HKV)]
    biases = [bias_ref[hq // G, (hq % G) * WIN:(hq % G + 1) * WIN, :] for hq in range(HQ)]
    return qs, k2s, v2s, biases


def _swa_specs():
    prev = lambda i: jnp.maximum(jnp.minimum(i, T // WIN - 1) - 1, 0)
    cur = lambda i: jnp.minimum(i, T // WIN - 1)
    return [
        pl.BlockSpec((WIN, HQ * DH), lambda i: (cur(i), 0)),
        pl.BlockSpec((WIN, 128), lambda i: (prev(i), COL_K)),
        pl.BlockSpec((WIN, 128), lambda i: (cur(i), COL_K)),
        pl.BlockSpec((WIN, 128), lambda i: (prev(i), COL_V)),
        pl.BlockSpec((WIN, 128), lambda i: (cur(i), COL_V)),
        pl.BlockSpec((1, 128), lambda i: (0, 0)),
        pl.BlockSpec((None, HKV, (HQ // HKV) * WIN, 2 * WIN), lambda i: (jnp.minimum(i, 1), 0, 0, 0)),
    ]


def _swa_fwd(proj, sinks, *, name):
    def body(q_ref, kp_ref, kc_ref, vp_ref, vc_ref, sk_ref, bias_ref, o_ref):
        qs, k2s, v2s, biases = _swa_load(q_ref, kp_ref, kc_ref, vp_ref, vc_ref, bias_ref)
        o_ref[...] = jnp.concatenate(_swa_heads(qs, k2s, v2s, sk_ref[...], biases), axis=1).astype(BF16)

    return pl.pallas_call(
        body, name=name, grid=(T // WIN,), in_specs=_swa_specs(),
        out_specs=pl.BlockSpec((WIN, HQ * DH), lambda i: (i, 0)),
        out_shape=jax.ShapeDtypeStruct((T, HQ * DH + GH * GD), BF16), compiler_params=_cp("parallel"),
    )(proj, proj, proj, proj, proj, sinks, _swa_bias())


def _swa_bwd(proj, sinks, do, dproj, *, name):
    NB = T // WIN
    QW = HQ * DH

    def body(q_ref, kp_ref, kc_ref, vp_ref, vc_ref, sk_ref, bias_ref, do_ref, _, out_ref, dsk_ref, cq, ck, cv):
        i = pl.program_id(0)

        @pl.when(i == 0)
        def _():
            cq[...] = jnp.zeros_like(cq)
            ck[...] = jnp.zeros_like(ck)
            cv[...] = jnp.zeros_like(cv)
            dsk_ref[...] = jnp.zeros_like(dsk_ref)

        @pl.when(i < NB)
        def _():
            qs, k2s, v2s, biases = _swa_load(q_ref, kp_ref, kc_ref, vp_ref, vc_ref, bias_ref)
            dov = do_ref[...].astype(F32)
            dqs, dk2s, dv2s, dsk = [], [], [], jnp.zeros_like(sk_ref[...])
            for h2 in range(HKV):
                hqs = range(h2 * (HQ // HKV), (h2 + 1) * (HQ // HKV))
                _, vjp = jax.vjp(functools.partial(_swa_heads, biases=[biases[hq] for hq in hqs], hqs=hqs),
                                 [qs[hq] for hq in hqs], [k2s[h2]], [v2s[h2]], sk_ref[...])
                dq, dk2, dv2, dsk_h = vjp([dov[:, hq * DH:(hq + 1) * DH] for hq in hqs])
                dqs, dk2s, dv2s, dsk = dqs + dq, dk2s + dk2, dv2s + dv2, dsk + dsk_h
            out_ref[:, :QW] = cq[...].astype(BF16)
            out_ref[:, QW:QW + 128] = (ck[...] + jnp.concatenate([d[:WIN] for d in dk2s], axis=1)).astype(BF16)
            out_ref[:, QW + 128:] = (cv[...] + jnp.concatenate([d[:WIN] for d in dv2s], axis=1)).astype(BF16)
            cq[...] = jnp.concatenate(dqs, axis=1)
            ck[...] = jnp.concatenate([d[WIN:] for d in dk2s], axis=1)
            cv[...] = jnp.concatenate([d[WIN:] for d in dv2s], axis=1)
            dsk_ref[...] += dsk

        @pl.when(i == NB)
        def _():
            out_ref[:, :QW] = cq[...].astype(BF16)
            out_ref[:, QW:QW + 128] = ck[...].astype(BF16)
            out_ref[:, QW + 128:] = cv[...].astype(BF16)

    qblk = pl.BlockSpec((WIN, QW), lambda i: (jnp.minimum(i, NB - 1), 0))
    return pl.pallas_call(
        body, name=name, grid=(NB + 1,), in_specs=_swa_specs() + [qblk, ANY],
        out_specs=[pl.BlockSpec((WIN, QW + 256), lambda i: (jnp.maximum(i - 1, 0), 0)), pl.BlockSpec((1, 128), lambda i: (0, 0))],
        out_shape=[jax.ShapeDtypeStruct(dproj.shape, dproj.dtype), jax.ShapeDtypeStruct((1, 128), F32)],
        scratch_shapes=[pltpu.VMEM((WIN, QW), F32), pltpu.VMEM((WIN, 128), F32), pltpu.VMEM((WIN, 128), F32)],
        input_output_aliases={8: 0}, compiler_params=_cp("arbitrary"),
    )(proj, proj, proj, proj, proj, sinks, _swa_bias(), do, dproj)


RC = 256
HALO = 8


def _load_ext(ref, c):
    nch = T // RC
    r0 = pl.multiple_of(c * RC, RC)
    p0 = pl.multiple_of(jnp.maximum(r0 - HALO, 0), HALO)
    n0 = pl.multiple_of(jnp.minimum(r0 + RC, T - HALO), HALO)
    prev = jnp.where(c > 0, ref[pl.ds(p0, HALO), :].astype(F32), 0.0)
    nxt = jnp.where(c < nch - 1, ref[pl.ds(n0, HALO), :].astype(F32), 0.0)
    return jnp.concatenate([prev, ref[pl.ds(r0, RC), :].astype(F32), nxt], axis=0)


def _conv_ext(xe, w, K):
    y = w[K - 1:K, :] * xe
    for s in range(1, K):
        y = y + w[K - 1 - s:K - s, :] * pltpu.roll(xe, s, 0)
    return y


def _conv_bwd_ext(xe, dye, w, K, dw_ref):
    n = xe.shape[0]
    own = slice(HALO, HALO + RC)
    dx = w[K - 1:K, :] * dye
    dw_ref[K - 1:K, :] += jnp.sum(dye[own] * xe[own], axis=0, keepdims=True)
    for s in range(1, K):
        dx = dx + w[K - 1 - s:K - s, :] * pltpu.roll(dye, n - s, 0)
        dw_ref[K - 1 - s:K - s, :] += jnp.sum(dye[own] * pltpu.roll(xe, s, 0)[own], axis=0, keepdims=True)
    return dx[own]


def _gdn_post_conv(y, is_qk):
    a = _silu(y)
    nrm = a * lax.rsqrt(jnp.sum(a * a, axis=1, keepdims=True) + EPS)
    return jnp.where(is_qk, nrm, a)


def _gdn_prep_fwd(proj, conv_w, *, name):
    nblk = 3 * GH

    def body(x_ref, w_ref, o_ref):
        is_qk = pl.program_id(0) < 2 * GH
        w = w_ref[...]

        def chunk(c, carry):
            y = _conv_ext(_load_ext(x_ref, c), w, GK)[HALO:HALO + RC]
            o_ref[pl.ds(pl.multiple_of(c * RC, RC), RC), :] = _gdn_post_conv(y, is_qk)
            return carry

        lax.fori_loop(0, T // RC, chunk, 0)

    return pl.pallas_call(
        body, name=name, grid=(nblk,),
        in_specs=[pl.BlockSpec((T, 128), lambda j: (0, COL_G + j)), pl.BlockSpec((GK, 128), lambda j: (0, j))],
        out_specs=pl.BlockSpec((T, 128), lambda j: (0, j)),
        out_shape=jax.ShapeDtypeStruct((T, nblk * 128), F32), compiler_params=_cp("parallel"),
    )(proj, conv_w)


def _gdn_prep_bwd(proj, conv_w, dout, dproj, *, name):
    nblk = 3 * GH

    def body(x_ref, w_ref, d_ref, _, dx_ref, dw_ref):
        is_qk = pl.program_id(0) < 2 * GH
        w = w_ref[...]
        dw_ref[...] = jnp.zeros_like(dw_ref)

        def chunk(c, carry):
            xe = _load_ext(x_ref, c)
            _, vjp = jax.vjp(lambda y: _gdn_post_conv(y, is_qk), _conv_ext(xe, w, GK))
            (dye,) = vjp(_load_ext(d_ref, c))
            dx_ref[pl.ds(pl.multiple_of(c * RC, RC), RC), :] = _conv_bwd_ext(xe, dye, w, GK, dw_ref).astype(BF16)
            return carry

        lax.fori_loop(0, T // RC, chunk, 0)

    return pl.pallas_call(
        body, name=name, grid=(nblk,),
        in_specs=[pl.BlockSpec((T, 128), lambda j: (0, COL_G + j)), pl.BlockSpec((GK, 128), lambda j: (0, j)),
                  pl.BlockSpec((T, 128), lambda j: (0, j)), ANY],
        out_specs=[pl.BlockSpec((T, 128), lambda j: (0, COL_G + j)), pl.BlockSpec((GK, 128), lambda j: (0, j))],
        out_shape=[jax.ShapeDtypeStruct(dproj.shape, dproj.dtype), jax.ShapeDtypeStruct((GK, nblk * 128), F32)],
        input_output_aliases={3: 0}, compiler_params=_cp("parallel"),
    )(proj, conv_w, dout, dproj)


def _ffn_post_conv(y, b, up):
    return _silu(y + b) * up


def _ffn_act_fwd(gu, conv_w, conv_b, *, name, carry=None):
    nblk = DFF // 128

    def body(g_ref, u_ref, w_ref, b_ref, o_ref):
        w, b = w_ref[...], b_ref[...]

        def chunk(c, carry):
            rows = pl.ds(pl.multiple_of(c * RC, RC), RC)
            y = _conv_ext(_load_ext(g_ref, c), w, FK)[HALO:HALO + RC]
            o_ref[rows, :] = _ffn_post_conv(y, b, u_ref[rows, :].astype(F32)).astype(BF16)
            return carry

        lax.fori_loop(0, T // RC, chunk, 0)

    (act,), carried = _seq_call(
        body, carry, name=name, steps=nblk,
        in_specs=[pl.BlockSpec((T, 128), lambda j: (0, j)), pl.BlockSpec((T, 128), lambda j: (0, nblk + j)),
                  pl.BlockSpec((FK, 128), lambda j: (0, j)), pl.BlockSpec((1, 128), lambda j: (0, j))],
        out_specs=[pl.BlockSpec((T, 128), lambda j: (0, j))], out_shape=[jax.ShapeDtypeStruct((T, DFF), BF16)],
        args=(gu, gu, conv_w, conv_b))
    return act, carried


def _ffn_act_bwd(gu, conv_w, conv_b, dact, *, name):
    nblk = DFF // 128

    def body(g_ref, u_ref, w_ref, b_ref, d_ref, dgu_ref, dw_ref, db_ref):
        dg_ref, du_ref = dgu_ref.at[0], dgu_ref.at[1]
        w, b = w_ref[...], b_ref[...]
        dw_ref[...] = jnp.zeros_like(dw_ref)
        db_ref[...] = jnp.zeros_like(db_ref)

        def chunk(c, carry):
            rows = pl.ds(pl.multiple_of(c * RC, RC), RC)
            xe = _load_ext(g_ref, c)
            ue = _load_ext(u_ref, c)
            _, vjp = jax.vjp(_ffn_post_conv, _conv_ext(xe, w, FK), b, ue)
            dye, db, due = vjp(_load_ext(d_ref, c))
            du_ref[rows, :] = due[HALO:HALO + RC].astype(BF16)
            db_ref[...] += jnp.sum(dye[HALO:HALO + RC], axis=0, keepdims=True)
            dg_ref[rows, :] = _conv_bwd_ext(xe, dye, w, FK, dw_ref).astype(BF16)
            return carry

        lax.fori_loop(0, T // RC, chunk, 0)

    col = pl.BlockSpec((T, 128), lambda j: (0, j))
    return pl.pallas_call(
        body, name=name, grid=(nblk,),
        in_specs=[col, pl.BlockSpec((T, 128), lambda j: (0, nblk + j)), pl.BlockSpec((FK, 128), lambda j: (0, j)),
                  pl.BlockSpec((1, 128), lambda j: (0, j)), col],
        out_specs=[pl.BlockSpec((2, T, 128), lambda j: (0, 0, j)), pl.BlockSpec((FK, 128), lambda j: (0, j)),
                   pl.BlockSpec((1, 128), lambda j: (0, j))],
        out_shape=[jax.ShapeDtypeStruct((2, T, DFF), BF16), jax.ShapeDtypeStruct((FK, DFF), F32),
                   jax.ShapeDtypeStruct((1, DFF), F32)],
        compiler_params=_cp("parallel"),
    )(gu, gu, conv_w, conv_b, dact)


def _gdn_gates(ba, alog, dtb):
    beta = jax.nn.sigmoid(ba)
    g = -jnp.exp(alog) * jax.nn.softplus(ba + dtb)
    tril = (_iota((GC, GC), 0) >= _iota((GC, GC), 1)).astype(F32)
    return beta, jnp.dot(tril, g, precision=HIGHEST, preferred_element_type=F32)


def _hmap(f, *lists):
    return [f(*xs) for xs in zip(*lists)]


def _gdn_cols(beta_all, gc_all):
    return [_col(beta_all, h) for h in range(GH)], [_col(gc_all, GH + h) for h in range(GH)]


def _gdn_decay(Gcs):
    r, c = _iota((GC, GC), 0), _iota((GC, GC), 1)
    eye, ones = (r == c).astype(F32), jnp.ones((GC, GC), F32)
    grows = _hmap(lambda G: jnp.dot(ones, eye * G, precision=HIGHEST, preferred_element_type=F32), Gcs)
    return _hmap(lambda G, grow: jnp.exp(jnp.where(r >= c, G - grow, -1e30)), Gcs, grows)


def _gdn_pre(beta_all, gc_all):
    betas, Gcs = _gdn_cols(beta_all, gc_all)
    return betas, Gcs, _gdn_decay(Gcs)


def _gdn_A(ks, betas, decays):
    strict = _iota((GC, GC), 0) > _iota((GC, GC), 1)
    kk = _hmap(lambda k, b: _dot_nt(k * b, k), ks, betas)
    return _hmap(lambda a, d: jnp.where(strict, a * d, 0.0), kk, decays)


def _tri_inv(As):
    eye = (_iota((GC, GC), 0) == _iota((GC, GC), 1)).astype(F32)
    Tms, Ps = [eye - A for A in As], As
    for _ in range(GC.bit_length() - 2):
        Ps = _hmap(lambda P: _dot(P, P), Ps)
        Tms = _hmap(lambda Tm, P: Tm + _dot(Tm, P), Tms, Ps)
    return Tms


def _gdn_chunk_pre(qs, ks, vs, betas, Gcs, decays, Tms):
    eGs = _hmap(jnp.exp, Gcs)
    us = _hmap(lambda Tm, v, b: _dot(Tm, v * b), Tms, vs, betas)
    ws = _hmap(lambda Tm, k, b, eG: _dot(Tm, k * b * eG), Tms, ks, betas, eGs)
    qss = [q * (GD ** -0.5) for q in qs]
    qks = _hmap(lambda q, k, d: _dot_nt(q, k) * d, qss, ks, decays)
    glasts = [jnp.sum(jnp.where(_iota(G.shape, 0) == GC - 1, G, 0.0), axis=0, keepdims=True) for G in Gcs]
    kds = _hmap(lambda k, gl, G: k * jnp.exp(gl - G), ks, glasts, Gcs)
    qes = _hmap(lambda q, eG: q * eG, qss, eGs)
    return us, ws, qks, kds, qes, glasts


def _gdn_chunk_state(us, ws, qks, kds, qes, glasts, Ss):
    v_news = _hmap(lambda u, w, S: u - _dot(w, S), us, ws, Ss)
    qS = _hmap(_dot, qes, Ss)
    os = _hmap(lambda a, qk, vn: a + _dot(qk, vn), qS, qks, v_news)
    S_news = _hmap(lambda S, gl, kd, vn: S * jnp.exp(gl) + _dot_tn(kd, vn), Ss, glasts, kds, v_news)
    return os, S_news


def _gdn_chunk(qs, ks, vs, betas, Gcs, decays, Ss, Tms):
    return _gdn_chunk_state(*_gdn_chunk_pre(qs, ks, vs, betas, Gcs, decays, Tms), Ss)


GDN_BWD_CPS = 2
GDN_CPS = 8


def _gdn_chunk_fwd(qkv_c, ba, alog, dtb, *, name, carry=None):
    W3, CPS = 3 * GH * GD, GDN_CPS

    def body(x_ref, ba_ref, al_ref, dt_ref, o_ref, s_ref, t_ref, S):
        @pl.when(pl.program_id(0) == 0)
        def _():
            S[...] = jnp.zeros_like(S)

        betas, Gcs, decays, qkv = [], [], [], ([], [], [])
        for sub in range(CPS):
            rows = slice(sub * GC, (sub + 1) * GC)
            b, g, d = _gdn_pre(*_gdn_gates(ba_ref[rows, :], al_ref[...], dt_ref[...]))
            betas, Gcs, decays = betas + b, Gcs + g, decays + d
            for p in range(3):
                qkv[p].extend(x_ref[rows, (p * GH + h) * GD:(p * GH + h + 1) * GD] for h in range(GH))
        qs, ks, vs = qkv
        Tms = _tri_inv(_gdn_A(ks, betas, decays))
        pre = _gdn_chunk_pre(qs, ks, vs, betas, Gcs, decays, Tms)
        Ss = [S[h] for h in range(GH)]
        for sub in range(CPS):
            mine = slice(sub * GH, (sub + 1) * GH)
            os, S_news = _gdn_chunk_state(*[part[mine] for part in pre], Ss)
            for h in range(GH):
                s_ref[sub, h] = Ss[h]
                t_ref[sub, h] = Tms[sub * GH + h]
                o_ref[sub * GC:(sub + 1) * GC, h * GD:(h + 1) * GD] = os[h]
            Ss = S_news
        for h in range(GH):
            S[h] = Ss[h]

    vec = pl.BlockSpec((1, 128), lambda n: (0, 0))
    return _seq_call(
        body, carry, name=name, steps=N_CHUNK // CPS,
        in_specs=[pl.BlockSpec((CPS * GC, W3), lambda n: (n, 0)), pl.BlockSpec((CPS * GC, 128), lambda n: (n, 0)), vec, vec],
        out_specs=[pl.BlockSpec((CPS * GC, GH * GD), lambda n: (n, 0)),
                   pl.BlockSpec((CPS, GH, GD, GD), lambda n: (n, 0, 0, 0)),
                   pl.BlockSpec((CPS, GH, GC, GC), lambda n: (n, 0, 0, 0))],
        out_shape=[jax.ShapeDtypeStruct((T, GH * GD), F32), jax.ShapeDtypeStruct((N_CHUNK, GH, GD, GD), F32),
                   jax.ShapeDtypeStruct((N_CHUNK, GH, GC, GC), F32)],
        scratch_shapes=[pltpu.VMEM((GH, GD, GD), F32)], args=(qkv_c, ba, alog, dtb))


def _gdn_chunk_bwd(qkv_c, ba, alog, dtb, s_all, t_all, do, dproj, *, name, carry=None):
    W3, CPS = 3 * GH * GD, GDN_BWD_CPS
    steps = N_CHUNK // CPS
    rev = lambda n: steps - 1 - n

    def body(x_ref, ba_ref, al_ref, dt_ref, s_ref, t_ref, do_ref, _, dx_ref, dba_ref, dal_ref, ddt_ref, dS):
        @pl.when(pl.program_id(0) == 0)
        def _():
            dS[...] = jnp.zeros_like(dS)
            dal_ref[...] = jnp.zeros_like(dal_ref)
            ddt_ref[...] = jnp.zeros_like(ddt_ref)

        rows = lambda sub: slice(sub * GC, (sub + 1) * GC)
        gates = [jax.vjp(_gdn_gates, ba_ref[rows(sub), :], al_ref[...], dt_ref[...]) for sub in range(CPS)]

        def pre_all(beta_alls, gc_alls):
            betas, Gcs, decays = [], [], []
            for b_all, g_all in zip(beta_alls, gc_alls):
                b, g, d = _gdn_pre(b_all, g_all)
                betas, Gcs, decays = betas + b, Gcs + g, decays + d
            return betas, Gcs, decays

        (betas, Gcs, decays), vjp_pre = jax.vjp(pre_all, [g[0][0] for g in gates], [g[0][1] for g in gates])
        qs, ks, vs = ([x_ref[rows(sub), (p * GH + h) * GD:(p * GH + h + 1) * GD] for sub in range(CPS) for h in range(GH)]
                      for p in range(3))
        Tms = [t_ref[sub, h] for sub in range(CPS) for h in range(GH)]
        pre, vjp_p = jax.vjp(_gdn_chunk_pre, qs, ks, vs, betas, Gcs, decays, Tms)
        dSs = [dS[h] for h in range(GH)]
        dpre = [[None] * (CPS * GH) for _ in pre]
        for sub in reversed(range(CPS)):
            mine = slice(sub * GH, (sub + 1) * GH)
            Ss = [s_ref[sub, h] for h in range(GH)]
            dos = [do_ref[rows(sub), h * GD:(h + 1) * GD] for h in range(GH)]
            _, vjp_s = jax.vjp(_gdn_chunk_state, *[part[mine] for part in pre], Ss)
            *dparts, dSs = vjp_s((dos, dSs))
            for full, part in zip(dpre, dparts):
                full[mine] = part
        for h in range(GH):
            dS[h] = dSs[h]
        dqs, dks, dvs, dbetas, dGcs, ddecays, dTs = vjp_p(tuple(dpre))
        dXs = _hmap(_dot_nt, dTs, Tms)
        dAs = _hmap(lambda Tm, dX: -_dot_tn(Tm, dX), Tms, dXs)
        _, vjp_a = jax.vjp(_gdn_A, ks, betas, decays)
        dks2, dbetas2, ddecays2 = vjp_a(dAs)
        add = lambda a, b: _hmap(jnp.add, a, b)
        db_alls, dg_alls = vjp_pre((add(dbetas, dbetas2), dGcs, add(ddecays, ddecays2)))
        for sub in range(CPS):
            for h in range(GH):
                i = sub * GH + h
                dx_ref[rows(sub), h * GD:(h + 1) * GD] = dqs[i]
                dx_ref[rows(sub), (GH + h) * GD:(GH + h + 1) * GD] = dks[i] + dks2[i]
                dx_ref[rows(sub), (2 * GH + h) * GD:(2 * GH + h + 1) * GD] = dvs[i]
            dba, dal, ddt = gates[sub][1]((db_alls[sub], dg_alls[sub]))
            dba_ref[rows(sub), :] = dba.astype(BF16)
            dal_ref[...] += dal
            ddt_ref[...] += ddt

    vec = pl.BlockSpec((1, 128), lambda n: (0, 0))
    return _seq_call(
        body, carry, name=name, steps=steps,
        in_specs=[pl.BlockSpec((CPS * GC, W3), lambda n: (rev(n), 0)), pl.BlockSpec((CPS * GC, 128), lambda n: (rev(n), 0)),
                  vec, vec, pl.BlockSpec((CPS, GH, GD, GD), lambda n: (rev(n), 0, 0, 0)),
                  pl.BlockSpec((CPS, GH, GC, GC), lambda n: (rev(n), 0, 0, 0)),
                  pl.BlockSpec((CPS * GC, GH * GD), lambda n: (rev(n), 0)),
                  ANY],
        out_specs=[pl.BlockSpec((CPS * GC, W3), lambda n: (rev(n), 0)),
                   pl.BlockSpec((CPS * GC, 128), lambda n: (rev(n), COL_BA)), vec, vec],
        out_shape=[jax.ShapeDtypeStruct((T, W3), F32), jax.ShapeDtypeStruct(dproj.shape, dproj.dtype),
                   jax.ShapeDtypeStruct((1, 128), F32), jax.ShapeDtypeStruct((1, 128), F32)],
        scratch_shapes=[pltpu.VMEM((GH, GD, GD), F32)], aliases={7: 1},
        args=(qkv_c, ba, alog, dtb, s_all, t_all, do, dproj))


def _gdn_post(o, z, nw):
    return o * lax.rsqrt(jnp.mean(o * o, axis=-1, keepdims=True) + EPS) * nw * _silu(z)


def _gdn_post_fwd(o_raw, proj, nw, mixed, *, name, tm=512):
    def body(o_ref, z_ref, w_ref, _, out_ref):
        out_ref[...] = _gdn_post(o_ref[...], z_ref[...].astype(F32), w_ref[...]).astype(BF16)

    return pl.pallas_call(
        body, name=name, grid=(T // tm, GH),
        in_specs=[pl.BlockSpec((tm, GD), lambda i, h: (i, h)), pl.BlockSpec((tm, GD), lambda i, h: (i, COL_Z + h)),
                  pl.BlockSpec((1, GD), lambda i, h: (0, 0)), ANY],
        out_specs=pl.BlockSpec((tm, GD), lambda i, h: (i, HQ * DH // GD + h)),
        out_shape=jax.ShapeDtypeStruct(mixed.shape, mixed.dtype), input_output_aliases={3: 0},
        compiler_params=_cp("parallel", "parallel"),
    )(o_raw, proj, nw, mixed)


def _gdn_post_bwd(o_raw, proj, nw, dmixed, *, name, tm=512):
    def body(o_ref, z_ref, w_ref, d_ref, do_ref, dz_ref, dw_ref):
        _, vjp = jax.vjp(_gdn_post, o_ref[...], z_ref[...].astype(F32), w_ref[...])
        do, dz, dw = vjp(d_ref[...])
        do_ref[...] = do
        dz_ref[...] = dz.astype(BF16)

        @pl.when((pl.program_id(0) == 0) & (pl.program_id(1) == 0))
        def _():
            dw_ref[...] = jnp.zeros_like(dw_ref)

        dw_ref[...] += dw

    blk = pl.BlockSpec((tm, GD), lambda i, h: (i, h))
    vec = pl.BlockSpec((1, GD), lambda i, h: (0, 0))
    return pl.pallas_call(
        body, name=name, grid=(T // tm, GH),
        in_specs=[blk, pl.BlockSpec((tm, GD), lambda i, h: (i, COL_Z + h)), vec,
                  pl.BlockSpec((tm, GD), lambda i, h: (i, GH + h))],
        out_specs=[blk, pl.BlockSpec((tm, GD), lambda i, h: (i, COL_Z + h)), vec],
        out_shape=[jax.ShapeDtypeStruct((T, GH * GD), F32), jax.ShapeDtypeStruct((T, N_PROJ), BF16),
                   jax.ShapeDtypeStruct((1, GD), F32)],
        compiler_params=_cp("arbitrary", "arbitrary"),
    )(o_raw, proj, nw, dmixed)


def _adamw(w, g, m, v, *, name):
    shape = w.shape
    cols = shape[-1]
    w2, g2, m2, v2 = (a.reshape(-1, cols) for a in (w, g, m, v))
    rows = w2.shape[0]
    tr = next((t for t in (512, 256, 128, 64, 32, 16, 8) if rows % t == 0), rows)

    def body(w_ref, g_ref, m_ref, v_ref, d_ref, nm_ref, nv_ref):
        gv = g_ref[...]
        nm = B1 * m_ref[...] + (1.0 - B1) * gv
        nv = B2 * v_ref[...] + (1.0 - B2) * jnp.square(gv)
        m_hat = nm / (1.0 - B1 ** STEP)
        v_hat = nv / (1.0 - B2 ** STEP)
        d_ref[...] = -LR * (m_hat / (jnp.sqrt(v_hat) + AEPS) + WD * w_ref[...])
        nm_ref[...] = nm
        nv_ref[...] = nv

    blk = pl.BlockSpec((tr, cols), lambda i: (i, 0))
    out = pl.pallas_call(
        body, name=name, grid=(rows // tr,), in_specs=[blk] * 4, out_specs=[blk] * 3,
        out_shape=[jax.ShapeDtypeStruct((rows, cols), F32)] * 3, compiler_params=_cp("parallel"),
    )(w2, g2, m2, v2)
    return tuple(o.reshape(shape) for o in out)


def _layer_weights(w_in):
    return jnp.pad(w_in, ((0, 0), (0, N_PROJ - w_in.shape[1])))


def _layer_params(attn_norm, sinks, gcw, a_log, dt_bias, gnw, ffn_norm, fcw, fcb):
    lanes4 = lambda v: jnp.pad(v, (GH, 128 - 2 * GH))[None]
    return dict(attn_norm=attn_norm[None], sinks=jnp.pad(sinks, (0, 128 - HQ))[None], gcw=gcw, alog=lanes4(a_log),
                dtb=lanes4(dt_bias), gnw=gnw[None], ffn_norm=ffn_norm[None], fcw=fcw, fcb=fcb[None])


def _mixer_fwd(x, W, P, l, carry=None):
    n = lambda s: f"l{l}_{s}"
    h, proj, ba = _rmsnorm_proj(x, P["attn_norm"], W["all"], name=n("norm1_proj"))
    mixed = _swa_fwd(proj, P["sinks"], name=n("swa"))
    qkv_c = _gdn_prep_fwd(proj, P["gcw"], name=n("gdn_prep"))
    (o_raw, s_all, t_all), carried = _gdn_chunk_fwd(qkv_c, ba, P["alog"], P["dtb"], name=n("gdn_chunk"), carry=carry)
    mixed = _gdn_post_fwd(o_raw, proj, P["gnw"], mixed, name=n("gdn_post"))
    x1 = _mm(mixed, W["out"], res=x, name=n("out_proj"), mode="nn", dims=(T, D, D), tm=512, tn=D, tk=D)
    saved = dict(x=x, h=h, proj=proj, ba=ba, qkv_c=qkv_c, o_raw=o_raw, s_all=s_all, t_all=t_all, mixed=mixed, x1=x1)
    return x1, saved, carried


def _ffn_fwd(x1, W, P, l, carry=None):
    n = lambda s: f"l{l}_{s}"
    h2 = _rmsnorm_fwd(x1, P["ffn_norm"], name=n("norm2"))
    gu = _mm(h2, W["ffn"], name=n("ffn_in"), mode="nn", dims=(T, 2 * DFF, D), tm=512, tn=FFN_CW, tk=D, out_dtype=BF16,
             b_spec=pl.BlockSpec((None, D, FFN_CW), lambda i, j, k: (j, k, 0)), n_outer=True)
    act, carried = _ffn_act_fwd(gu, P["fcw"], P["fcb"], name=n("ffn_act"), carry=carry)
    x2 = _mm(act, W["down"], res=x1, name=n("ffn_down"), mode="nn", dims=(T, D, DFF), tm=512, tn=D, tk=DFF)
    return x2, dict(h2=h2, gu=gu, act=act), carried


def _ffn_bwd(dx2, sv, W, P, l):
    n = lambda s: f"l{l}_{s}"
    CW = FFN_CW
    dact = _mm(dx2, W["down"], name=n("d_act"), mode="nt", dims=(T, DFF, D), tm=512, tn=DFF // 2, tk=D, out_dtype=BF16,
               n_outer=True)
    g_down = _mm(sv["act"], dx2, name=n("g_down"), mode="tn", dims=(DFF, D, T), tm=DFF // 2, tn=D, tk=min(T, 2048), out_dtype=BF16)
    dgu, g_fcw, g_fcb = _ffn_act_bwd(sv["gu"], P["fcw"], P["fcb"], dact, name=n("d_ffn_act"))
    dx1, g_ffn_norm = _mm_rmsnorm_bwd(
        dgu, W["ffn"], sv["x1"], P["ffn_norm"], dx2, name=n("d_h2_norm2"), K=2 * DFF, tm=512, tk=CW,
        a_spec=pl.BlockSpec((None, 512, CW), lambda i, k: (k // 2, i, k % 2)),
        b_spec=pl.BlockSpec((None, D, CW), lambda i, k: (k, 0, 0)))
    g_ffn = _mm(sv["h2"], dgu, name=n("g_ffn"), mode="tn", dims=(D, 2 * DFF, T), tm=512, tn=CW, tk=T, n_outer=True,
                b_spec=pl.BlockSpec((None, T, CW), lambda i, j, k: (j // 2, k, j % 2)),
                out_spec=pl.BlockSpec((None, 512, CW), lambda i, j, k: (j, i, 0)),
                out_shape=jax.ShapeDtypeStruct((N_CHIP, D, CW), BF16), out_dtype=BF16)
    return dx1, dict(ffn_norm=g_ffn_norm[0], w_ffn_in=g_ffn, ffn_conv_w=g_fcw, ffn_conv_b=g_fcb[0], w_down=g_down)


def _mixer_bwd(dx1, sv, W, P, l, carry=None, last_hook=None):
    n = lambda s: f"l{l}_{s}"
    dmixed = _mm(dx1, W["out"], name=n("d_mixed"), mode="nt", dims=(T, D, D), tm=512, tn=D, tk=D)
    g_out = _mm(sv["mixed"], dx1, name=n("g_out"), mode="tn", dims=(D, D, T), tm=D, tn=D, tk=min(T, 2048), out_dtype=BF16)
    do_raw, dproj, g_gnw = _gdn_post_bwd(sv["o_raw"], sv["proj"], P["gnw"], dmixed, name=n("d_gdn_post"))
    (dqkv_c, dproj, g_alog, g_dtb), carried = _gdn_chunk_bwd(
        sv["qkv_c"], sv["ba"], P["alog"], P["dtb"], sv["s_all"], sv["t_all"], do_raw, dproj, name=n("d_gdn_chunk"), carry=carry)
    dproj, g_gcw = _gdn_prep_bwd(sv["proj"], P["gcw"], dqkv_c, dproj, name=n("d_gdn_prep"))
    dproj, g_sinks = _swa_bwd(sv["proj"], P["sinks"], dmixed, dproj, name=n("d_swa"))
    g_all = _mm(sv["h"], dproj, name=n("g_in"), mode="tn", dims=(D, N_PROJ, T), tm=512, tn=N_PROJ, tk=min(T, 2048), out_dtype=BF16)
    norm_w = P["attn_norm"] if last_hook is None else last_hook(dict(w_in=g_all, w_out=g_out))
    dx, g_attn_norm = _mm_rmsnorm_bwd(dproj, W["all"], sv["x"], norm_w, dx1, name=n("d_h_norm1"), K=N_PROJ,
                                      tm=512, tk=N_PROJ)
    grads = dict(attn_norm=g_attn_norm[0], w_in=g_all, attn_sinks=g_sinks[0, :HQ], gdn_conv_w=g_gcw,
                 gdn_a_log=g_alog[0, GH:2 * GH], gdn_dt_bias=g_dtb[0, GH:2 * GH], gdn_norm=g_gnw[0], w_out=g_out)
    return dx, grads, carried


def _pos():
    return lax.axis_index("x"), lax.axis_index("y"), lax.axis_index("c")


def _other_chips(x, y):
    return [(1 - x, y), (x, 1 - y), (1 - x, 1 - y)]


def _remote(src, dst, send_sems, recv_sems, k, to):
    return pltpu.make_async_remote_copy(src_ref=src, dst_ref=dst, send_sem=send_sems.at[k], recv_sem=recv_sems.at[k],
                                        device_id=to, device_id_type=MESH)


def _run_carry(carry, *, name):
    ci, co = len(carry.arrays), len(carry.out_shapes)

    def body(*refs):
        sends, recvs = carry.plan(refs[:ci], refs[ci:ci + co], refs[-2], refs[-1])
        for cp in sends:
            cp.start()
        for cp in recvs:
            cp.wait_recv()
        for cp in sends:
            cp.wait_send()

    return list(pl.pallas_call(
        body, name=name, in_specs=[ANY] * ci, out_specs=[ANY] * co, out_shape=carry.out_shapes,
        scratch_shapes=[pltpu.SemaphoreType.DMA((carry.nsem,)), pltpu.SemaphoreType.DMA((carry.nsem,))],
    )(*carry.arrays))


def _chip_index():
    return 2 * lax.axis_index("x") + lax.axis_index("y")


def _gather_carry(shards):
    def plan(srcs, outs, send_sems, recv_sems):
        x, y, c = _pos()
        chip = 2 * x + y
        others = [(k, px, py, 2 * px + py) for k, (px, py) in enumerate(_other_chips(x, y))]
        sends = [_remote(srcs[t], outs[t].at[chip], send_sems, recv_sems, 3 * t + k, (px, py, c))
                 for t in range(len(srcs)) for k, px, py, _ in others]
        recvs = [_remote(srcs[t], outs[t].at[j], send_sems, recv_sems, 3 * t + k, (x, y, c))
                 for t in range(len(srcs)) for k, _, _, j in others]
        return sends, recvs

    return _Carry(shards, [jax.ShapeDtypeStruct((N_CHIP,) + s.shape, s.dtype) for s in shards], 3 * len(shards), plan)


def _fill_own(outs, shards):
    return [lax.dynamic_update_index_in_dim(o, s, _chip_index(), 0) for o, s in zip(outs, shards)]


def _gather_halves(shards, after, *, name):
    nt = len(shards)

    def body(*refs):
        srcs, outs, send_sems, recv_sems = refs[:nt], refs[nt + 1:2 * nt + 1], refs[-2], refs[-1]
        x, y, c = _pos()
        me, sibling, chip = (x, y, c), (x, y, 1 - c), 2 * x + y
        others = [(k, px, py, 2 * px + py) for k, (px, py) in enumerate(_other_chips(x, y))]

        def rows(t, h):
            rh = srcs[t].shape[0] // 2
            return pl.ds(pl.multiple_of(h * rh, 16), rh)

        first = [_remote(srcs[t].at[rows(t, c), :], outs[t].at[chip, rows(t, c), :], send_sems, recv_sems, 3 * t + k, (px, py, c))
                 for t in range(nt) for k, px, py, _ in others]
        for cp in first:
            cp.start()
        passed = []
        for t in range(nt):
            for k, _, _, j in others:
                slot = outs[t].at[j, rows(t, c), :]
                _remote(slot, slot, send_sems, recv_sems, 3 * t + k, me).wait_recv()
                passed.append(_remote(slot, slot, send_sems, recv_sems, 3 * (nt + t) + k, sibling))
                passed[-1].start()
        for t in range(nt):
            for k, _, _, j in others:
                slot = outs[t].at[j, rows(t, 1 - c), :]
                _remote(slot, slot, send_sems, recv_sems, 3 * (nt + t) + k, me).wait_recv()
        for cp in first + passed:
            cp.wait_send()

    outs = pl.pallas_call(
        body, name=name, in_specs=[ANY] * (nt + 1), out_specs=[ANY] * nt,
        out_shape=[jax.ShapeDtypeStruct((N_CHIP,) + s.shape, s.dtype) for s in shards],
        scratch_shapes=[pltpu.SemaphoreType.DMA((6 * nt,)), pltpu.SemaphoreType.DMA((6 * nt,))],
    )(*shards, after)
    return _fill_own(outs, shards)


HBM_SPEC = pl.BlockSpec(memory_space=pltpu.HBM)
SEM_SPEC = pl.BlockSpec(memory_space=pltpu.SEMAPHORE)
DATAFLOW = pltpu.SideEffectType.DATAFLOW_SIDE_EFFECTING


def _split_start(carry, after, *, name):
    ci, co = len(carry.arrays), len(carry.out_shapes)

    def body(*refs):
        srcs, lands, send_sems, recv_sems, token = refs[:ci], refs[ci:ci + co], refs[ci + co + 1], refs[ci + co + 2], refs[-1]
        for cp in carry.plan(srcs, lands, send_sems, recv_sems)[0]:
            cp.start()
        token[...] = jnp.zeros_like(token)

    lands = [lax.empty(s.shape, s.dtype) for s in carry.out_shapes]
    hbm = lambda a: pltpu.with_memory_space_constraint(a, pltpu.HBM)
    out = pl.pallas_call(
        body, name=name, in_specs=[HBM_SPEC] * (ci + co) + [ANY],
        out_specs=[SEM_SPEC, SEM_SPEC] + [HBM_SPEC] * (ci + co) + [VMEM_SPEC],
        out_shape=[pltpu.SemaphoreType.DMA((carry.nsem,)), pltpu.SemaphoreType.DMA((carry.nsem,))]
        + [pltpu.HBM(a.shape, a.dtype) for a in carry.arrays + lands] + [jax.ShapeDtypeStruct((8, 128), F32)],
        input_output_aliases={t: 2 + t for t in range(ci + co)},
        compiler_params=pltpu.CompilerParams(has_side_effects=DATAFLOW),
    )(*[hbm(a) for a in carry.arrays + lands], after)
    return out[0], out[1], list(out[2:2 + ci]), list(out[2 + ci:2 + ci + co]), out[-1]


def _split_wait(carry, send_sems, recv_sems, srcs, lands, after, *, name):
    ci, co = len(srcs), len(lands)

    def body(*refs):
        sends, recvs = carry.plan(refs[:ci], refs[ci:ci + co], refs[ci + co], refs[ci + co + 1])
        for cp in sends:
            cp.wait_send()
        for cp in recvs:
            cp.wait_recv()

    out = pl.pallas_call(
        body, name=name, in_specs=[HBM_SPEC] * (ci + co) + [SEM_SPEC, SEM_SPEC, ANY], out_specs=[HBM_SPEC] * (ci + co),
        out_shape=[pltpu.HBM(a.shape, a.dtype) for a in list(srcs) + list(lands)],
        input_output_aliases={t: t for t in range(ci + co)},
        compiler_params=pltpu.CompilerParams(has_side_effects=DATAFLOW),
    )(*srcs, *lands, send_sems, recv_sems, after)
    return list(out[:ci]), list(out[ci:])


def _gather_start(shards, after, *, name):
    return _split_start(_gather_carry(shards), after, name=name)


def _gather_wait(send_sems, recv_sems, shards, lands, after, *, name):
    srcs, got = _split_wait(_gather_carry(shards), send_sems, recv_sems, shards, lands, after, name=name)
    return _fill_own(got, srcs)


def _ag_small(v, *, name):
    m, n = v.shape

    def body(x_ref, out_ref, red_ref, send_sems, recv_sems, local_sem):
        x, y, c = _pos()
        me, sibling = (x, y, c), (x, y, 1 - c)
        chips = _other_chips(x, y)
        rows = lambda px, py, pc: out_ref.at[pl.ds(pl.multiple_of((4 * px + 2 * py + pc) * m, 8), m), :]
        mine = pltpu.make_async_copy(x_ref, rows(*me), local_sem)
        mine.start()
        first = [_remote(x_ref, rows(*me), send_sems, recv_sems, 0, sibling)]
        first += [_remote(x_ref, rows(*me), send_sems, recv_sems, 1 + k, (*chip, c)) for k, chip in enumerate(chips)]
        for cp in first:
            cp.start()
        passed = [_remote(rows(*chip, c), rows(*chip, c), send_sems, recv_sems, 4 + k, sibling) for k, chip in enumerate(chips)]
        for k, chip in enumerate(chips):
            _remote(rows(*chip, c), rows(*chip, c), send_sems, recv_sems, 1 + k, me).wait_recv()
            passed[k].start()
        _remote(rows(*sibling), rows(*sibling), send_sems, recv_sems, 0, me).wait_recv()
        for k, chip in enumerate(chips):
            _remote(rows(*chip, 1 - c), rows(*chip, 1 - c), send_sems, recv_sems, 4 + k, me).wait_recv()
        for cp in first + passed:
            cp.wait_send()
        mine.wait()
        acc = out_ref[0:m, :]
        for d in range(1, N_DEV):
            acc = acc + out_ref[d * m:(d + 1) * m, :]
        red_ref[...] = acc

    return pl.pallas_call(
        body, name=name, in_specs=[VMEM_SPEC], out_specs=[VMEM_SPEC, VMEM_SPEC],
        out_shape=[jax.ShapeDtypeStruct((N_DEV * m, n), v.dtype), jax.ShapeDtypeStruct((m, n), v.dtype)],
        scratch_shapes=[pltpu.SemaphoreType.DMA((7,)), pltpu.SemaphoreType.DMA((7,)), pltpu.SemaphoreType.DMA],
    )(v)


def _halves(ref, c):
    rh = ref.shape[1] // 2
    return ref.at[:, pl.ds(pl.multiple_of(c * rh, 16), rh), :]


def _rs_swap(gs, *, name):
    nt = len(gs)

    def body(*refs):
        g, theirs = refs[:nt], refs[nt:2 * nt]
        send_sems, recv_sems = refs[2 * nt:]
        x, y, c = _pos()
        swaps = [_remote(_halves(g[t], 1 - c), theirs[t], send_sems, recv_sems, t, (x, y, 1 - c)) for t in range(nt)]
        for cp in swaps:
            cp.start()
        for cp in swaps:
            cp.wait()

    return pl.pallas_call(
        body, name=name, in_specs=[ANY] * nt, out_specs=[ANY] * nt,
        out_shape=[jax.ShapeDtypeStruct((a.shape[0], a.shape[1] // 2, a.shape[2]), a.dtype) for a in gs],
        scratch_shapes=[pltpu.SemaphoreType.DMA((nt,)), pltpu.SemaphoreType.DMA((nt,))],
    )(*gs)


def _exchange_carry(ss):
    def plan(s, out, send_sems, recv_sems):
        x, y, c = _pos()
        chip = 2 * x + y
        others = [(k, px, py, 2 * px + py) for k, (px, py) in enumerate(_other_chips(x, y))]
        sends = [_remote(s[t].at[j], out[t].at[chip], send_sems, recv_sems, 3 * t + k, (px, py, c))
                 for t in range(len(s)) for k, px, py, j in others]
        recvs = [_remote(s[t].at[j], out[t].at[j], send_sems, recv_sems, 3 * t + k, (x, y, c))
                 for t in range(len(s)) for k, _, _, j in others]
        return sends, recvs

    return _Carry(ss, [jax.ShapeDtypeStruct(a.shape, a.dtype) for a in ss], 3 * len(ss), plan)


def _fill_own_slab(outs, ss):
    chip = _chip_index()
    return [lax.dynamic_update_index_in_dim(o, lax.dynamic_index_in_dim(s, chip, 0, keepdims=False), chip, 0)
            for o, s in zip(outs, ss)]


def _rs_join(rs, *, name):
    nt = len(rs)

    def body(*refs):
        r, theirs = refs[:nt], refs[nt:2 * nt]
        send_sems, recv_sems = refs[2 * nt:]
        x, y, c = _pos()
        swaps = [_remote(r[t], theirs[t], send_sems, recv_sems, t, (x, y, 1 - c)) for t in range(nt)]
        for cp in swaps:
            cp.start()
        for cp in swaps:
            cp.wait()

    theirs = pl.pallas_call(
        body, name=name, in_specs=[ANY] * nt, out_specs=[ANY] * nt,
        out_shape=[jax.ShapeDtypeStruct(a.shape, a.dtype) for a in rs],
        scratch_shapes=[pltpu.SemaphoreType.DMA((nt,)), pltpu.SemaphoreType.DMA((nt,))],
    )(*rs)
    first = lax.axis_index("c") == 0
    return [jnp.concatenate([jnp.where(first, a, b), jnp.where(first, b, a)], axis=0) for a, b in zip(rs, theirs)]


def _row_tile(rows, dtype):
    unit = 16 if dtype == BF16 else 8
    if rows <= 512:
        return rows
    return next((t for t in (512, 256, 128, 64, 32, 16, 8) if rows % t == 0 and t % unit == 0), rows)


def _add_pair(g, b, half, *, name):
    n, rh, cols = b.shape
    tr = _row_tile(rh, BF16)
    nb = rh // tr

    def body(half_ref, g_ref, b_ref, o_ref):
        o_ref[...] = (g_ref[...].astype(F32) + b_ref[...].astype(F32)).astype(BF16)

    blk = pl.BlockSpec((None, tr, cols), lambda j, i, h: (j, i, 0))
    return pl.pallas_call(
        body, name=name,
        grid_spec=pltpu.PrefetchScalarGridSpec(
            num_scalar_prefetch=1, grid=(n, nb),
            in_specs=[pl.BlockSpec((None, tr, cols), lambda j, i, h: (j, h[0] * nb + i, 0)), blk], out_specs=blk),
        out_shape=jax.ShapeDtypeStruct(b.shape, BF16), compiler_params=_cp("parallel", "parallel"),
    )(half, g, b)


def _sum_chips(b, *, name):
    n, rh, cols = b.shape
    tr = _row_tile(rh, BF16)

    def body(b_ref, o_ref):
        acc = b_ref[0].astype(F32)
        for j in range(1, n):
            acc = acc + b_ref[j].astype(F32)
        o_ref[...] = acc

    return pl.pallas_call(
        body, name=name, grid=(rh // tr,), in_specs=[pl.BlockSpec((n, tr, cols), lambda i: (0, i, 0))],
        out_specs=pl.BlockSpec((tr, cols), lambda i: (i, 0)),
        out_shape=jax.ShapeDtypeStruct((rh, cols), F32), compiler_params=_cp("parallel"),
    )(b)


BIG = ("w_in", "w_out", "w_ffn_in", "w_down")
W_IN_SHARD = (N_MAIN + 2 * GH) // N_CHIP


def _chip_major(name, g):
    if name == "w_in":
        return jnp.stack([g[:, j * W_IN_SHARD:(j + 1) * W_IN_SHARD] for j in range(N_CHIP)])
    if name == "w_ffn_in":
        return g
    return g.reshape(N_CHIP, g.shape[0] // N_CHIP, g.shape[1])


def _rs_pairs(gs, tag):
    half = lax.axis_index("c").astype(jnp.int32).reshape(1)
    theirs = _rs_swap(gs, name=f"{tag}_swap")
    return [_add_pair(a, b, half, name=f"{tag}_add{i}") for i, (a, b) in enumerate(zip(gs, theirs))]


def _rs_sums(got, pairs, tag):
    return [_sum_chips(b, name=f"{tag}_sum{i}") for i, b in enumerate(_fill_own_slab(got, pairs))]


def _size(shape):
    n = 1
    for s in shape:
        n *= s
    return n


def _pack_flat(parts, total, dtype):
    flat = jnp.concatenate([p.reshape(-1).astype(dtype) for p in parts])
    return jnp.pad(flat, (0, total - flat.shape[0]))


def _unpack_flat(flat, shapes):
    out, o = [], 0
    for s in shapes:
        out.append(flat[o:o + _size(s)].reshape(s))
        o += _size(s)
    return out


WEIGHTS = ("attn_norm", "w_in", "attn_sinks", "gdn_conv_w", "gdn_a_log", "gdn_dt_bias", "gdn_norm", "w_out", "ffn_norm",
           "w_ffn_in", "ffn_conv_w", "ffn_conv_b", "w_down", "final_norm")
SMALL = {"attn_norm": (DEPTH, D), "attn_sinks": (DEPTH, HQ), "gdn_a_log": (DEPTH, GH), "gdn_dt_bias": (DEPTH, GH),
         "gdn_norm": (DEPTH, GD), "ffn_norm": (DEPTH, D), "ffn_conv_b": (DEPTH, DFF), "final_norm": (D,)}
CONV_FULL = {"gdn_conv_w": (DEPTH, GK, 3 * GH * GD), "ffn_conv_w": (DEPTH, FK, DFF)}
CONV_SHARD = {"gdn_conv_w": (DEPTH, GK, 3 * GH * GD // N_CHIP), "ffn_conv_w": (DEPTH, FK, DFF // N_CHIP)}
CONV_ROWS, SMALLG_ROWS, SMALLW_ROWS = 64, 320, 144


def kernel(x, attn_norm, w_in, attn_sinks, gdn_conv_w, gdn_a_log, gdn_dt_bias, gdn_norm, w_out, ffn_norm, w_ffn_in, ffn_conv_w, ffn_conv_b, w_down, final_norm, loss_target, m_attn_norm, m_w_in, m_attn_sinks, m_gdn_conv_w, m_gdn_a_log, m_gdn_dt_bias, m_gdn_norm, m_w_out, m_ffn_norm, m_w_ffn_in, m_ffn_conv_w, m_ffn_conv_b, m_w_down, m_final_norm, v_attn_norm, v_w_in, v_attn_sinks, v_gdn_conv_w, v_gdn_a_log, v_gdn_dt_bias, v_gdn_norm, v_w_out, v_ffn_norm, v_w_ffn_in, v_ffn_conv_w, v_ffn_conv_b, v_w_down, v_final_norm):
    w = dict(zip(WEIGHTS, (attn_norm, w_in, attn_sinks, gdn_conv_w, gdn_a_log, gdn_dt_bias, gdn_norm, w_out, ffn_norm,
                           w_ffn_in, ffn_conv_w, ffn_conv_b, w_down, final_norm)))
    m = dict(zip(WEIGHTS, (m_attn_norm, m_w_in, m_attn_sinks, m_gdn_conv_w, m_gdn_a_log, m_gdn_dt_bias, m_gdn_norm, m_w_out,
                           m_ffn_norm, m_w_ffn_in, m_ffn_conv_w, m_ffn_conv_b, m_w_down, m_final_norm)))
    v = dict(zip(WEIGHTS, (v_attn_norm, v_w_in, v_attn_sinks, v_gdn_conv_w, v_gdn_a_log, v_gdn_dt_bias, v_gdn_norm, v_w_out,
                           v_ffn_norm, v_w_ffn_in, v_ffn_conv_w, v_ffn_conv_b, v_w_down, v_final_norm)))
    cx, cy, _ = _pos()
    chip = 2 * cx + cy

    cpack = _pack_flat([w[n] for n in CONV_SHARD], CONV_ROWS * 128, F32).reshape(CONV_ROWS, 128)
    cgath, _ = _ag_small(cpack, name="gather_conv_w")
    cgath = cgath.reshape(N_DEV, CONV_ROWS * 128)
    cshards = [_unpack_flat(cgath[2 * j], list(CONV_SHARD.values())) for j in range(N_CHIP)]
    conv = {n: jnp.concatenate([cshards[j][i] for j in range(N_CHIP)], axis=2) for i, n in enumerate(CONV_SHARD)}

    Ps = [_layer_params(attn_norm[l], attn_sinks[l], conv["gdn_conv_w"][l], gdn_a_log[l], gdn_dt_bias[l], gdn_norm[l],
                        ffn_norm[l], conv["ffn_conv_w"][l], ffn_conv_b[l]) for l in range(DEPTH)]

    wb = {n: [w[n][l].astype(BF16) for l in range(DEPTH)] for n in BIG}
    mixer_w = lambda g_in, g_out: dict(all=_layer_weights(jnp.concatenate(list(g_in), axis=1)), out=g_out.reshape(D, D))
    ffn_w = lambda g_ffn, g_down: dict(ffn=g_ffn, down=g_down.reshape(DFF, D))
    groups = [[wb[a][l], wb[b][l]] for l in range(DEPTH) for a, b in (("w_in", "w_out"), ("w_ffn_in", "w_down"))]
    def after_start(P, key, started):
        return {**P, key: P[key] + started[4][:1, :1]}

    got0 = _gather_halves(groups[0], cgath, name="gather0")
    g1 = _gather_start(groups[1], got0[1], name="gather1_start")
    Wm0 = mixer_w(*got0)
    h, sv0, _ = _mixer_fwd(x[0], Wm0, after_start(Ps[0], "attn_norm", g1), 0)
    got1 = _gather_wait(*g1[:4], h, name="gather1_wait")
    g2 = _gather_start(groups[2], got1[1], name="gather2_start")
    Wf0 = ffn_w(*got1)
    h, sv0f, _ = _ffn_fwd(h, Wf0, after_start(Ps[0], "ffn_norm", g2), 0)
    got2 = _gather_wait(*g2[:4], h, name="gather2_wait")
    g3 = _gather_start(groups[3], got2[1], name="gather3_start")
    Wm1 = mixer_w(*got2)
    h, sv1, _ = _mixer_fwd(h, Wm1, after_start(Ps[1], "attn_norm", g3), 1)
    Wf1 = ffn_w(*_gather_wait(*g3[:4], h, name="gather3_wait"))
    h, sv1f, _ = _ffn_fwd(h, Wf1, Ps[1], 1)
    dx, g_final, loss_part = _loss_head(h, final_norm[None], loss_target[0], name="loss_head")

    def exchange_start(pairs, tag):
        carry = _exchange_carry(pairs)
        return carry, _split_start(carry, pairs[0], name=f"{tag}_exchange_start")

    def exchange_wait(carry, started, after, tag):
        srcs, got = _split_wait(carry, *started[:4], after, name=f"{tag}_exchange_wait")
        return _rs_sums(got, srcs, tag)

    tied = lambda P, key, started: {**P, key: P[key] + started[4][:1, :1]}
    dx, gf1 = _ffn_bwd(dx, {**sv1, **sv1f}, Wf1, Ps[1], 1)
    dx, gm1, _ = _mixer_bwd(dx, sv1, Wm1, Ps[1], 1)
    lg1 = {**gf1, **gm1}
    pairs1 = _rs_pairs([_chip_major(n, lg1[n]) for n in BIG], "rs1")
    ex1 = exchange_start(pairs1, "rs1")
    dx, gf0 = _ffn_bwd(dx, {**sv0, **sv0f}, Wf0, tied(Ps[0], "fcb", ex1[1]), 0)
    pairs0a = _rs_pairs([_chip_major(n, gf0[n]) for n in BIG[2:]], "rs0a")
    ex0a = exchange_start(pairs0a, "rs0a")
    last = {}

    def send_last(gm):
        pairs = _rs_pairs([_chip_major(n, gm[n]) for n in BIG[:2]], "rs0b")
        last["ex"] = exchange_start(pairs, "rs0b")
        return tied(Ps[0], "attn_norm", last["ex"][1])["attn_norm"]

    dx, gm0, _ = _mixer_bwd(dx, sv0, Wm0, tied(Ps[0], "gnw", ex0a[1]), 0, last_hook=send_last)
    lg = [{**gf0, **gm0}, lg1]
    sums0b = exchange_wait(*last["ex"], dx, "rs0b")
    sums1 = exchange_wait(*ex1, sums0b[0], "rs1")
    sums0a = exchange_wait(*ex0a, sums1[0], "rs0a")
    joined = _rs_join(sums0b + sums0a + sums1, name="rs_join")
    reduced = [joined[:4], joined[4:]]
    grad_x = dx[None]
    stacked = lambda n: jnp.stack([lg[l][n] for l in range(DEPTH)])

    small_parts = [g_final[0] if n == "final_norm" else stacked(n) for n in SMALL] + [stacked(n) for n in CONV_FULL]
    svec = _pack_flat(small_parts + [loss_part[0, :1]], SMALLG_ROWS * 128, F32).reshape(SMALLG_ROWS, 128)
    _, sred = _ag_small(svec, name="reduce_small")
    small_g = _unpack_flat(sred.reshape(-1), list(SMALL.values()) + list(CONV_FULL.values()) + [(1,)])
    g = dict(zip(list(SMALL) + list(CONV_FULL), small_g[:-1]))
    loss = small_g[-1][0]
    for n in CONV_FULL:
        wd = CONV_SHARD[n][2]
        g[n] = lax.dynamic_slice_in_dim(g[n], chip * wd, wd, axis=2)

    g.update({n: jnp.stack([reduced[l][i] for l in range(DEPTH)]) for i, n in enumerate(BIG)})

    delta, new_m, new_v = {}, {}, {}
    for n in BIG:
        delta[n], new_m[n], new_v[n] = _adamw(w[n], g[n], m[n], v[n], name=f"adamw_{n}")
    small_names = list(SMALL) + list(CONV_SHARD)
    small_shapes = list(SMALL.values()) + list(CONV_SHARD.values())
    packed = [_pack_flat([t[n] for n in small_names], SMALLW_ROWS * 128, F32).reshape(SMALLW_ROWS, 128) for t in (w, g, m, v)]
    for res, out in zip(_adamw(*packed, name="adamw_small"), (delta, new_m, new_v)):
        out.update(zip(small_names, _unpack_flat(res.reshape(-1), small_shapes)))

    return (loss, grad_x, *[g[n] for n in WEIGHTS], *[delta[n] for n in WEIGHTS], *[new_m[n] for n in WEIGHTS],
            *[new_v[n] for n in WEIGHTS])
```
